```python
import math
import jax
import jax.numpy as jnp
from jax import lax
import numpy as np

D_MODEL = 1024
BATCH = 4
SEQ = 4096
DEPTH = 1
DEC_BATCH = 128
DEC_SEQ = 8
PAST_LEN = 8192
PAGE_SIZE = 128

HEAD_DIM = 64
ATT_WIDTH = D_MODEL // 2
ATT_HEADS = ATT_WIDTH // HEAD_DIM
ATT_KV_HEADS = ATT_HEADS // 4
ATT_GROUP = ATT_HEADS // ATT_KV_HEADS
KV_WIDTH = ATT_KV_HEADS * HEAD_DIM
WINDOW = 128
MLSTM_WIDTH = D_MODEL - ATT_WIDTH
MLSTM_DV = 128
MLSTM_HEADS = MLSTM_WIDTH // MLSTM_DV
MLSTM_DK = MLSTM_DV // 2
MQK_WIDTH = MLSTM_HEADS * MLSTM_DK
MLSTM_CHUNK = 64
MIX_WIDTH = ATT_WIDTH + MLSTM_WIDTH
IN_WIDTH = ATT_WIDTH + 2 * KV_WIDTH + 2 * MQK_WIDTH + 2 * MLSTM_WIDTH + 2 * MLSTM_HEADS
MEM_TOKENS = 256
CROSS_HEADS = 4
CROSS_HEAD_DIM = 64
CROSS_WIDTH = CROSS_HEADS * CROSS_HEAD_DIM
FFN_HIDDEN = -(-(8 * D_MODEL) // (3 * 256)) * 256
FORGET_BIAS = 3.0
EPS = 1e-6

kernel_name = "hymba_swa_sink_mlstm_decoder_step"


def rmsnorm(x, g):
    xf = x.astype(jnp.float32)
    y = xf * lax.rsqrt(jnp.mean(xf * xf, axis=-1, keepdims=True) + EPS)
    return (y * g.astype(jnp.float32)).astype(x.dtype)


def project_mixers(xn, w_in, b_igate, b_fgate):
    B, T, _ = xn.shape
    f32 = jnp.float32
    z = jnp.einsum('btd,de->bte', xn, w_in)
    sizes = (ATT_WIDTH, KV_WIDTH, KV_WIDTH, MQK_WIDTH, MQK_WIDTH, MLSTM_WIDTH, MLSTM_WIDTH,
             MLSTM_HEADS, MLSTM_HEADS)
    points = []
    acc = 0
    for s in sizes[:-1]:
        acc += s
        points.append(acc)
    q, k, v, mq, mk, mv, og, ig, fg = jnp.split(z, points, axis=-1)
    q = q.reshape(B, T, ATT_KV_HEADS, ATT_GROUP, HEAD_DIM)
    k = k.reshape(B, T, ATT_KV_HEADS, HEAD_DIM)
    v = v.reshape(B, T, ATT_KV_HEADS, HEAD_DIM)
    mq = mq.astype(f32).reshape(B, T, MLSTM_HEADS, MLSTM_DK)
    mk = mk.astype(f32).reshape(B, T, MLSTM_HEADS, MLSTM_DK) * (MLSTM_DK ** -0.5)
    mv = mv.astype(f32).reshape(B, T, MLSTM_HEADS, MLSTM_DV)
    ig = ig.astype(f32) + b_igate.astype(f32)
    lf = jax.nn.log_sigmoid(fg.astype(f32) + b_fgate.astype(f32))
    return q, k, v, mq, mk, mv, og, ig, lf


def window_mask(qpos, kpos):
    d = qpos[:, :, None] - kpos[:, None, :]
    return (d >= 0) & (d <= WINDOW)


def attend_with_sinks(q, k, v, mask, sinks):
    s = jnp.einsum('bnqhgd,bnkhd->bnhgqk', q, k).astype(jnp.float32) * (HEAD_DIM ** -0.5)
    s = jnp.where(mask[None, :, None, None], s, -jnp.inf)
    sink = sinks.astype(jnp.float32).reshape(1, 1, ATT_KV_HEADS, ATT_GROUP, 1, 1)
    m = jnp.maximum(jnp.max(s, axis=-1, keepdims=True), sink)
    p = jnp.exp(s - m)
    p = p / (jnp.sum(p, axis=-1, keepdims=True) + jnp.exp(sink - m))
    return jnp.einsum('bnhgqk,bnkhd->bnqhgd', p.astype(v.dtype), v)


def swa_prompt(q, k, v, sinks):
    B, T = q.shape[:2]
    nb = T // WINDOW
    qb = q.reshape(B, nb, WINDOW, ATT_KV_HEADS, ATT_GROUP, HEAD_DIM)

    def band(a):
        ab = a.reshape(B, nb, WINDOW, ATT_KV_HEADS, HEAD_DIM)
        prev = jnp.pad(ab[:, :-1], ((0, 0), (1, 0), (0, 0), (0, 0), (0, 0)))
        return jnp.concatenate([prev, ab], axis=2)

    start = jnp.arange(nb)[:, None] * WINDOW
    qpos = start + jnp.arange(WINDOW)[None, :]
    kpos = start - WINDOW + jnp.arange(2 * WINDOW)[None, :]
    mask = window_mask(qpos, kpos) & (kpos >= 0)[:, None, :]
    o = attend_with_sinks(qb, band(k), band(v), mask, sinks)
    return o.reshape(B, T, ATT_WIDTH)


def swa_sample(q, k, v, buf_k, buf_v, sinks):
    B, T = q.shape[:2]
    P = buf_k.shape[1]
    kc = jnp.concatenate([buf_k.astype(k.dtype), k], axis=1)
    vc = jnp.concatenate([buf_v.astype(v.dtype), v], axis=1)
    qpos = jnp.arange(T)[None, :]
    kpos = (jnp.arange(P + T) - P)[None, :]
    mask = window_mask(qpos, kpos)
    o = attend_with_sinks(q[:, None], kc[:, None], vc[:, None], mask, sinks)
    return o.reshape(B, T, ATT_WIDTH), kc[:, T:], vc[:, T:]


def mlstm_chunk(carry, inp):
    C, n, m = carry
    q, k, v, ig, lf = inp
    L = q.shape[1]
    bt = jnp.swapaxes(jnp.cumsum(lf, axis=1), 1, 2)
    it = jnp.swapaxes(ig, 1, 2)
    causal = jnp.tril(jnp.ones((L, L), dtype=bool))
    D = jnp.where(causal, bt[..., :, None] - bt[..., None, :] + it[..., None, :], -jnp.inf)
    inter = bt + m[..., None]
    mt = jnp.maximum(jnp.max(D, axis=-1), inter)
    S = jnp.einsum('blhd,bshd->bhls', q, k) * jnp.exp(D - mt[..., None])
    wi = jnp.exp(inter - mt)
    num = (jnp.einsum('bhls,bshe->blhe', S, v)
           + jnp.swapaxes(wi, 1, 2)[..., None] * jnp.einsum('blhd,bhed->blhe', q, C))
    den = jnp.sum(S, axis=-1) + wi * jnp.einsum('blhd,bhd->bhl', q, n)
    lower = jnp.maximum(jnp.abs(den), jnp.exp(-mt))
    h = num / jnp.swapaxes(lower, 1, 2)[..., None]
    bL = bt[..., -1]
    m_new = mt[..., -1]
    wk = jnp.exp(bL[..., None] - bt + it - m_new[..., None])
    wc = jnp.exp(bL + m - m_new)
    C_new = wc[..., None, None] * C + jnp.einsum('bhs,bshe,bshd->bhed', wk, v, k)
    n_new = wc[..., None] * n + jnp.einsum('bhs,bshd->bhd', wk, k)
    return (C_new, n_new, m_new), h


def mlstm_prompt(mq, mk, mv, ig, lf):
    B, T = mq.shape[:2]
    nc = T // MLSTM_CHUNK

    def chunks(a):
        return jnp.moveaxis(a.reshape(B, nc, MLSTM_CHUNK, *a.shape[2:]), 1, 0)

    init = (jnp.zeros((B, MLSTM_HEADS, MLSTM_DV, MLSTM_DK), jnp.float32),
            jnp.zeros((B, MLSTM_HEADS, MLSTM_DK), jnp.float32),
            jnp.zeros((B, MLSTM_HEADS), jnp.float32))
    (C, n, m), h = lax.scan(mlstm_chunk, init,
                            (chunks(mq), chunks(mk), chunks(mv), chunks(ig), chunks(lf)))
    h = jnp.moveaxis(h, 0, 1).reshape(B, T, MLSTM_HEADS, MLSTM_DV)
    return h, C, n, m


def merge_mixers(att, h, og, g_head, w_out):
    B, T = h.shape[:2]
    hn = h * lax.rsqrt(jnp.mean(h * h, axis=-1, keepdims=True) + EPS)
    hm = hn.reshape(B, T, MLSTM_WIDTH) * g_head.astype(jnp.float32) * jax.nn.sigmoid(og.astype(jnp.float32))
    cat = jnp.concatenate([att, hm.astype(att.dtype)], axis=-1)
    return jnp.einsum('bte,ed->btd', cat, w_out)


def memory_kv(mem, g_mem, w_ck, w_cv):
    B, M, _ = mem.shape
    mn = rmsnorm(mem, g_mem)
    k = jnp.einsum('bmd,de->bme', mn, w_ck).reshape(B, M, CROSS_HEADS, CROSS_HEAD_DIM)
    v = jnp.einsum('bmd,de->bme', mn, w_cv).reshape(B, M, CROSS_HEADS, CROSS_HEAD_DIM)
    return k, v


def cross_attend(xn, mk, mv, w_cq, w_co):
    B, T, _ = xn.shape
    q = jnp.einsum('btd,de->bte', xn, w_cq).reshape(B, T, CROSS_HEADS, CROSS_HEAD_DIM)
    s = jnp.einsum('bthd,bmhd->bhtm', q, mk.astype(q.dtype)).astype(jnp.float32) * (CROSS_HEAD_DIM ** -0.5)
    p = jax.nn.softmax(s, axis=-1)
    o = jnp.einsum('bhtm,bmhd->bthd', p.astype(q.dtype), mv.astype(q.dtype))
    return jnp.einsum('bte,ed->btd', o.reshape(B, T, CROSS_WIDTH), w_co)


def swiglu(xn, w_gate, w_up, w_down):
    g = jnp.einsum('btd,df->btf', xn, w_gate)
    u = jnp.einsum('btd,df->btf', xn, w_up)
    return jnp.einsum('btf,fd->btd', jax.nn.silu(g) * u, w_down)


def setup_inputs(seed: int = 0) -> dict:
    key = jax.random.key(seed)
    ks = jax.random.split(key, 32)
    f32 = jnp.float32

    def nrm(k, shape, scale):
        return jax.random.normal(k, shape, f32) * scale

    buf = min(WINDOW, PAST_LEN)
    D = D_MODEL
    return {
        "x_prompt": nrm(ks[0], (BATCH, SEQ, D), 1.0),
        "x_sample": nrm(ks[1], (DEC_BATCH, DEC_SEQ, D), 1.0),
        "mem_prompt": nrm(ks[2], (BATCH, MEM_TOKENS, D), 1.0),
        "cache_swa_k": nrm(ks[3], (DEPTH, DEC_BATCH, buf, ATT_KV_HEADS, HEAD_DIM), 1.0),
        "cache_swa_v": nrm(ks[4], (DEPTH, DEC_BATCH, buf, ATT_KV_HEADS, HEAD_DIM), 1.0),
        "state_mlstm_C": nrm(ks[5], (DEPTH, DEC_BATCH, MLSTM_HEADS, MLSTM_DV, MLSTM_DK), 0.3),
        "state_mlstm_n": nrm(ks[6], (DEPTH, DEC_BATCH, MLSTM_HEADS, MLSTM_DK), 0.3),
        "state_mlstm_m": nrm(ks[7], (DEPTH, DEC_BATCH, MLSTM_HEADS), 1.0),
        "cache_mem_k": nrm(ks[8], (DEPTH, DEC_BATCH, MEM_TOKENS, CROSS_HEADS, CROSS_HEAD_DIM), 1.0),
        "cache_mem_v": nrm(ks[9], (DEPTH, DEC_BATCH, MEM_TOKENS, CROSS_HEADS, CROSS_HEAD_DIM), 1.0),
        "w_in": nrm(ks[10], (DEPTH, D, IN_WIDTH), D ** -0.5),
        "b_igate": nrm(ks[11], (DEPTH, MLSTM_HEADS), 0.1),
        "b_fgate": FORGET_BIAS + nrm(ks[12], (DEPTH, MLSTM_HEADS), 0.1),
        "attn_sinks": nrm(ks[13], (DEPTH, ATT_HEADS), 0.5),
        "g_mlstm_head": 1.0 + nrm(ks[14], (DEPTH, MLSTM_WIDTH), 0.05),
        "w_out": nrm(ks[15], (DEPTH, MIX_WIDTH, D), MIX_WIDTH ** -0.5),
        "g_mix": 1.0 + nrm(ks[16], (DEPTH, D), 0.05),
        "g_cross": 1.0 + nrm(ks[17], (DEPTH, D), 0.05),
        "g_mem": 1.0 + nrm(ks[18], (DEPTH, D), 0.05),
        "w_cq": nrm(ks[19], (DEPTH, D, CROSS_WIDTH), D ** -0.5),
        "w_ck": nrm(ks[20], (DEPTH, D, CROSS_WIDTH), D ** -0.5),
        "w_cv": nrm(ks[21], (DEPTH, D, CROSS_WIDTH), D ** -0.5),
        "w_co": nrm(ks[22], (DEPTH, CROSS_WIDTH, D), CROSS_WIDTH ** -0.5),
        "g_ffn": 1.0 + nrm(ks[23], (DEPTH, D), 0.05),
        "w_gate": nrm(ks[24], (DEPTH, D, FFN_HIDDEN), D ** -0.5),
        "w_up": nrm(ks[25], (DEPTH, D, FFN_HIDDEN), D ** -0.5),
        "w_down": nrm(ks[26], (DEPTH, FFN_HIDDEN, D), FFN_HIDDEN ** -0.5),
        "g_final": 1.0 + nrm(ks[27], (D,), 0.05),
    }


def reference(x_prompt, x_sample, mem_prompt, cache_swa_k, cache_swa_v, state_mlstm_C,
              state_mlstm_n, state_mlstm_m, cache_mem_k, cache_mem_v, w_in, b_igate, b_fgate,
              attn_sinks, g_mlstm_head, w_out, g_mix, g_cross, g_mem, w_cq, w_ck, w_cv, w_co,
              g_ffn, w_gate, w_up, w_down, g_final):
    f32 = jnp.float32
    yp, ys = x_prompt, x_sample
    skp, svp, Cp, np_, mp, mkp, mvp = [], [], [], [], [], [], []
    sks, svs, Cs, ns, ms = [], [], [], [], []
    for l in range(DEPTH):
        q, k, v, mq, mk, mv, og, ig, lf = project_mixers(rmsnorm(yp, g_mix[l]), w_in[l], b_igate[l], b_fgate[l])
        att = swa_prompt(q, k, v, attn_sinks[l])
        h, C, n, m = mlstm_prompt(mq, mk, mv, ig, lf)
        yp = yp + merge_mixers(att, h, og, g_mlstm_head[l], w_out[l])
        skp.append(k[:, -WINDOW:])
        svp.append(v[:, -WINDOW:])
        Cp.append(C)
        np_.append(n)
        mp.append(m)
        q, k, v, mq, mk, mv, og, ig, lf = project_mixers(rmsnorm(ys, g_mix[l]), w_in[l], b_igate[l], b_fgate[l])
        att, kbuf, vbuf = swa_sample(q, k, v, cache_swa_k[l], cache_swa_v[l], attn_sinks[l])
        carry = (state_mlstm_C[l].astype(f32), state_mlstm_n[l].astype(f32), state_mlstm_m[l].astype(f32))
        (C, n, m), h = mlstm_chunk(carry, (mq, mk, mv, ig, lf))
        ys = ys + merge_mixers(att, h, og, g_mlstm_head[l], w_out[l])
        sks.append(kbuf)
        svs.append(vbuf)
        Cs.append(C)
        ns.append(n)
        ms.append(m)
        memk, memv = memory_kv(mem_prompt, g_mem[l], w_ck[l], w_cv[l])
        yp = yp + cross_attend(rmsnorm(yp, g_cross[l]), memk, memv, w_cq[l], w_co[l])
        ys = ys + cross_attend(rmsnorm(ys, g_cross[l]), cache_mem_k[l], cache_mem_v[l], w_cq[l], w_co[l])
        mkp.append(memk)
        mvp.append(memv)
        yp = yp + swiglu(rmsnorm(yp, g_ffn[l]), w_gate[l], w_up[l], w_down[l])
        ys = ys + swiglu(rmsnorm(ys, g_ffn[l]), w_gate[l], w_up[l], w_down[l])
    y_prompt = rmsnorm(yp, g_final)
    y_sample = rmsnorm(ys, g_final)
    return (y_prompt, y_sample,
            jnp.stack(skp), jnp.stack(svp), jnp.stack(Cp), jnp.stack(np_), jnp.stack(mp),
            jnp.stack(mkp), jnp.stack(mvp),
            jnp.stack(sks), jnp.stack(svs), jnp.stack(Cs), jnp.stack(ns), jnp.stack(ms))
```

```python
import functools

import jax
import jax.numpy as jnp
from jax import lax
from jax.experimental import pallas as pl
from jax.experimental.pallas import tpu as pltpu

F32 = jnp.float32
BF16 = jnp.bfloat16

HEAD_DIM = 64
ATT_GROUP = 4
ATT_KV_HEADS = 2
ATT_WIDTH = 512
KV_WIDTH = ATT_KV_HEADS * HEAD_DIM
WINDOW = 128
LOG_WINDOW = 7
MLSTM_HEADS = 4
MLSTM_DV = 128
MLSTM_DK = 64
MQK_WIDTH = MLSTM_HEADS * MLSTM_DK
MLSTM_WIDTH = MLSTM_HEADS * MLSTM_DV
MAIN_WIDTH = ATT_WIDTH + 2 * KV_WIDTH + 2 * MQK_WIDTH + 2 * MLSTM_WIDTH
CROSS_HEADS = 4
CROSS_WIDTH = CROSS_HEADS * HEAD_DIM
EPS = 1e-6
NEG_INF = float("-inf")

LANES = 128
GATE_ROWS = 16
VMEM_LIMIT = 56 * 1024 * 1024

NT = (((1,), (1,)), ((), ()))
TN = (((0,), (0,)), ((), ()))


def _dot(a, b):
    return jnp.dot(a, b, preferred_element_type=F32)


def _dot_nt(a, b):
    return lax.dot_general(a, b, NT, preferred_element_type=F32)


def _dot_tn(a, b):
    return lax.dot_general(a, b, TN, preferred_element_type=F32)


def _rmsnorm(x, g):
    return x * lax.rsqrt(jnp.mean(x * x, axis=-1, keepdims=True) + EPS) * g


def _log_sigmoid(x):
    return jnp.minimum(x, 0.0) - jnp.log1p(jnp.exp(-jnp.abs(x)))


def _split3(x):
    hi = x.astype(BF16)
    r1 = x - hi.astype(F32)
    mid = r1.astype(BF16)
    lo = (r1 - mid.astype(F32)).astype(BF16)
    return hi, mid, lo


def _cumsum_cols(tri, x):
    hi, mid, lo = _split3(x)
    return _dot(tri, hi) + _dot(tri, mid) + _dot(tri, lo)


def _cumsum_rows(x, tri):
    hi, mid, lo = _split3(x)
    return _dot(hi, tri) + _dot(mid, tri) + _dot(lo, tri)


def _log2(n):
    assert n > 0 and n & (n - 1) == 0, n
    return n.bit_length() - 1


def _low_half(shape):
    return lax.broadcasted_iota(jnp.int32, shape, 1) < HEAD_DIM


def _proj_kernel(x_ref, g_ref, w_ref, wgc_ref, wgr_ref, bc_ref, br_ref,
                 q_ref, k_ref, v_ref, mq_ref, mk_ref, mv_ref, og_ref, gc_ref, gr_ref):
    xn = _rmsnorm(x_ref[...], g_ref[...]).astype(BF16)

    def mm(lo, hi):
        return _dot(xn, w_ref[:, lo:hi])

    o = 0
    q_ref[...] = mm(o, o + ATT_WIDTH).astype(q_ref.dtype)
    o += ATT_WIDTH
    kv = mm(o, o + 2 * KV_WIDTH)
    k_ref[...] = kv[:, :KV_WIDTH]
    v_ref[...] = kv[:, KV_WIDTH:]
    o += 2 * KV_WIDTH
    mqk = mm(o, o + 2 * MQK_WIDTH)
    mq_ref[...] = mqk[:, :MQK_WIDTH].astype(mq_ref.dtype)
    mk_ref[...] = (mqk[:, MQK_WIDTH:] * (MLSTM_DK ** -0.5)).astype(mk_ref.dtype)
    o += 2 * MQK_WIDTH
    mv_ref[...] = mm(o, o + MLSTM_WIDTH).astype(mv_ref.dtype)
    o += MLSTM_WIDTH
    og_ref[...] = mm(o, o + MLSTM_WIDTH).astype(og_ref.dtype)

    zc = _dot(xn, wgc_ref[...]) + bc_ref[...]
    lane = lax.broadcasted_iota(jnp.int32, zc.shape, 1)
    gc_ref[...] = jnp.where(lane < MLSTM_HEADS, zc, _log_sigmoid(zc))
    zr = _dot_nt(wgr_ref[...], xn) + br_ref[...]
    row = lax.broadcasted_iota(jnp.int32, zr.shape, 0)
    gr_ref[0] = jnp.where(row < MLSTM_HEADS, zr, _log_sigmoid(zr))


def _const_spec(shape):
    nd = len(shape)
    return pl.BlockSpec(shape, lambda *_: (0,) * nd, pipeline_mode=pl.Buffered(1))


def _proj(x, w, nseg, act_dtype, tm):
    t, d = x.shape
    seg = t // nseg
    tpb = seg // tm
    grid = (t // tm,)

    def rows(width):
        return pl.BlockSpec((tm, width), lambda i: (i, 0))

    out_shape = [
        jax.ShapeDtypeStruct((t, ATT_WIDTH), act_dtype),
        jax.ShapeDtypeStruct((t, KV_WIDTH), F32),
        jax.ShapeDtypeStruct((t, KV_WIDTH), F32),
        jax.ShapeDtypeStruct((t, MQK_WIDTH), act_dtype),
        jax.ShapeDtypeStruct((t, MQK_WIDTH), act_dtype),
        jax.ShapeDtypeStruct((t, MLSTM_WIDTH), act_dtype),
        jax.ShapeDtypeStruct((t, MLSTM_WIDTH), act_dtype),
        jax.ShapeDtypeStruct((t, LANES), F32),
        jax.ShapeDtypeStruct((nseg, GATE_ROWS, seg), F32),
    ]
    out_specs = [
        rows(ATT_WIDTH), rows(KV_WIDTH), rows(KV_WIDTH), rows(MQK_WIDTH), rows(MQK_WIDTH),
        rows(MLSTM_WIDTH), rows(MLSTM_WIDTH), rows(LANES),
        pl.BlockSpec((1, GATE_ROWS, tm), lambda i: (i // tpb, 0, i % tpb)),
    ]
    in_specs = [
        rows(d),
        _const_spec((1, d)),
        _const_spec(w["w_main"].shape),
        _const_spec(w["w_gate_cols"].shape),
        _const_spec(w["w_gate_rows"].shape),
        _const_spec(w["b_gate_cols"].shape),
        _const_spec(w["b_gate_rows"].shape),
    ]
    return pl.pallas_call(
        _proj_kernel,
        grid=grid,
        in_specs=in_specs,
        out_specs=out_specs,
        out_shape=out_shape,
        compiler_params=pltpu.CompilerParams(
            dimension_semantics=("arbitrary",), vmem_limit_bytes=VMEM_LIMIT),
    )(x, w["g_mix"], w["w_main"], w["w_gate_cols"], w["w_gate_rows"],
      w["b_gate_cols"], w["b_gate_rows"])


def _dup_heads(x):
    swapped = pltpu.roll(x, HEAD_DIM, axis=1)
    lo = _low_half(x.shape)
    return (jnp.where(lo, x, swapped).astype(BF16), jnp.where(lo, swapped, x).astype(BF16))


def _stack_query_heads(q, h):
    parts = []
    for g in range(ATT_GROUP):
        hq = h * ATT_GROUP + g
        slab = q[:, (hq // 2) * LANES:(hq // 2 + 1) * LANES]
        lo = _low_half(slab.shape)
        keep = lo if hq % 2 == 0 else jnp.logical_not(lo)
        parts.append(jnp.where(keep, slab, jnp.zeros_like(slab)))
    return jnp.concatenate(parts, axis=0).astype(BF16)


def _sink_column(sink_ref, h, rows_per_head):
    n = ATT_GROUP * rows_per_head
    grp = lax.broadcasted_iota(jnp.int32, (n, 1), 0) >> _log2(rows_per_head)
    col = jnp.full((n, 1), sink_ref[h * ATT_GROUP], F32)
    for g in range(1, ATT_GROUP):
        col = jnp.where(grp == g, sink_ref[h * ATT_GROUP + g], col)
    return col


def _sink_attention(qh, kd, vd, valid, sink):
    s = _dot_nt(qh, kd) * (HEAD_DIM ** -0.5)
    s = jnp.where(valid, s, NEG_INF)
    m = jnp.maximum(jnp.max(s, axis=-1, keepdims=True), sink)
    p = jnp.exp(s - m)
    denom = jnp.sum(p, axis=-1, keepdims=True) + jnp.exp(sink - m)
    return _dot(p.astype(BF16), vd) / denom


def _merge_query_heads(outs, rows_per_head):
    slabs = []
    for h in range(ATT_KV_HEADS):
        for j in range(ATT_GROUP // 2):
            even = outs[h][(2 * j) * rows_per_head:(2 * j + 1) * rows_per_head]
            odd = outs[h][(2 * j + 1) * rows_per_head:(2 * j + 2) * rows_per_head]
            slabs.append(jnp.where(_low_half(even.shape), even, odd))
    return jnp.concatenate(slabs, axis=1)


def _swa_prompt_kernel(sink_ref, q_ref, kc_ref, kp_ref, vc_ref, vp_ref, o_ref, *, qb):
    i = pl.program_id(1)
    k_all = jnp.concatenate([kp_ref[...], kc_ref[...]], axis=0)
    v_all = jnp.concatenate([vp_ref[...], vc_ref[...]], axis=0)
    kd = _dup_heads(k_all)
    vd = _dup_heads(v_all)

    n = ATT_GROUP * WINDOW
    row = lax.broadcasted_iota(jnp.int32, (n, 2 * WINDOW), 0) & (WINDOW - 1)
    col = lax.broadcasted_iota(jnp.int32, (n, 2 * WINDOW), 1)
    delta = col - row
    in_band = (delta >= 0) & (delta <= WINDOW)
    sinks = [_sink_column(sink_ref, h, WINDOW) for h in range(ATT_KV_HEADS)]

    for j in range(qb // WINDOW):
        valid = in_band & (col >= jnp.where(i == 0, WINDOW, 0)) if j == 0 else in_band
        q = q_ref[j * WINDOW:(j + 1) * WINDOW, :]
        outs = []
        for h in range(ATT_KV_HEADS):
            qh = _stack_query_heads(q, h)
            kband = kd[h][j * WINDOW:(j + 2) * WINDOW]
            vband = vd[h][j * WINDOW:(j + 2) * WINDOW]
            outs.append(_sink_attention(qh, kband, vband, valid, sinks[h]))
        o_ref[j * WINDOW:(j + 1) * WINDOW, :] = _merge_query_heads(outs, WINDOW).astype(o_ref.dtype)


def _swa_prompt(q, k, v, sinks, nbatch, qb):
    t = q.shape[0]
    seq = t // nbatch
    nq = seq // qb
    wpq = qb // WINDOW
    wps = seq // WINDOW

    def cur(width):
        return pl.BlockSpec((qb, width), lambda b, i: (b * nq + i, 0))

    def prev(width):
        return pl.BlockSpec(
            (WINDOW, width), lambda b, i: (b * wps + jnp.maximum(i * wpq - 1, 0), 0))

    return pl.pallas_call(
        functools.partial(_swa_prompt_kernel, qb=qb),
        grid=(nbatch, nq),
        in_specs=[
            pl.BlockSpec(memory_space=pltpu.SMEM),
            cur(ATT_WIDTH), cur(KV_WIDTH), prev(KV_WIDTH), cur(KV_WIDTH), prev(KV_WIDTH),
        ],
        out_specs=cur(ATT_WIDTH),
        out_shape=jax.ShapeDtypeStruct((t, ATT_WIDTH), BF16),
        compiler_params=pltpu.CompilerParams(
            dimension_semantics=("arbitrary", "arbitrary"), vmem_limit_bytes=VMEM_LIMIT),
    )(sinks, q, k, k, v, v)


def _swa_sample_kernel(sink_ref, q_ref, kn_ref, vn_ref, ck_ref, cv_ref,
                       o_ref, kbuf_ref, vbuf_ref, *, nb, tq):
    past = ck_ref.shape[0] // nb
    klen = 2 * past
    n = ATT_GROUP * tq
    row = lax.broadcasted_iota(jnp.int32, (n, klen), 0) & (tq - 1)
    col = lax.broadcasted_iota(jnp.int32, (n, klen), 1)
    delta = col - row
    valid = (delta >= 0) & (delta <= WINDOW)
    sinks = [_sink_column(sink_ref, h, tq) for h in range(ATT_KV_HEADS)]
    pad = jnp.zeros((klen - past - tq, KV_WIDTH), F32)

    for b in range(nb):
        kc = jnp.concatenate([ck_ref[b * past:(b + 1) * past], kn_ref[b * tq:(b + 1) * tq], pad], axis=0)
        vc = jnp.concatenate([cv_ref[b * past:(b + 1) * past], vn_ref[b * tq:(b + 1) * tq], pad], axis=0)
        kbuf_ref[b * past:(b + 1) * past] = kc[tq:tq + past]
        vbuf_ref[b * past:(b + 1) * past] = vc[tq:tq + past]
        kd = _dup_heads(kc)
        vd = _dup_heads(vc)
        q = q_ref[b * tq:(b + 1) * tq, :]
        outs = []
        for h in range(ATT_KV_HEADS):
            qh = _stack_query_heads(q, h)
            outs.append(_sink_attention(qh, kd[h], vd[h], valid, sinks[h]))
        o_ref[b * tq:(b + 1) * tq, :] = _merge_query_heads(outs, tq)


def _swa_sample(q, k, v, cache_k, cache_v, sinks, nbatch, nb):
    t = q.shape[0]
    tq = t // nbatch
    past = cache_k.shape[0] // nbatch
    assert past == WINDOW and tq & (tq - 1) == 0 and tq <= past

    def rows(r, width):
        return pl.BlockSpec((r, width), lambda i: (i, 0))

    return pl.pallas_call(
        functools.partial(_swa_sample_kernel, nb=nb, tq=tq),
        grid=(nbatch // nb,),
        in_specs=[
            pl.BlockSpec(memory_space=pltpu.SMEM),
            rows(nb * tq, ATT_WIDTH), rows(nb * tq, KV_WIDTH), rows(nb * tq, KV_WIDTH),
            rows(nb * past, KV_WIDTH), rows(nb * past, KV_WIDTH),
        ],
        out_specs=[rows(nb * tq, ATT_WIDTH), rows(nb * past, KV_WIDTH), rows(nb * past, KV_WIDTH)],
        out_shape=[
            jax.ShapeDtypeStruct((t, ATT_WIDTH), F32),
            jax.ShapeDtypeStruct(cache_k.shape, F32),
            jax.ShapeDtypeStruct(cache_v.shape, F32),
        ],
        compiler_params=pltpu.CompilerParams(
            dimension_semantics=("arbitrary",), vmem_limit_bytes=VMEM_LIMIT),
    )(sinks, q, k, v, cache_k, cache_v)


def _mlstm_intra(q, k, v, it_c, bt_c, it_r, bt_r, m_c, valid):
    d = jnp.where(valid, bt_c - bt_r + it_r, NEG_INF)
    inter = bt_c + m_c
    mt = jnp.maximum(jnp.max(d, axis=-1, keepdims=True), inter)
    sm = _dot_nt(q, k) * jnp.exp(d - mt)
    wi = jnp.exp(inter - mt)
    return _dot(sm.astype(BF16), v), jnp.sum(sm, axis=-1, keepdims=True), mt, wi


def _mlstm_head_out(num, den, mt, og, g_head):
    lower = jnp.maximum(jnp.abs(den), jnp.exp(-mt))
    h = num / lower
    hn = h * lax.rsqrt(jnp.mean(h * h, axis=-1, keepdims=True) + EPS)
    return hn * g_head * jax.nn.sigmoid(og)


def _mlstm_prompt_kernel(mq_ref, mk_ref, mv_ref, og_ref, gc_ref, gr_ref, gh_ref,
                         hm_ref, c_ref, n_ref, m_ref, *, nbatch, chunk):
    @pl.when(pl.program_id(0) == 0)
    def _():
        c_ref[...] = jnp.zeros_like(c_ref)
        n_ref[...] = jnp.zeros_like(n_ref)
        m_ref[...] = jnp.zeros_like(m_ref)

    r = lax.broadcasted_iota(jnp.int32, (chunk, chunk), 0)
    s = lax.broadcasted_iota(jnp.int32, (chunk, chunk), 1)
    causal = s <= r
    tril = jnp.where(causal, 1.0, 0.0).astype(BF16)
    triu = jnp.where(r <= s, 1.0, 0.0).astype(BF16)
    lo = _low_half((chunk, LANES))

    for b in range(nbatch):
        gc = gc_ref[b]
        gr = gr_ref[b]
        btc = _cumsum_cols(tril, gc)
        btr = _cumsum_rows(gr, triu)
        for h in range(MLSTM_HEADS):
            idx = b * MLSTM_HEADS + h
            slab = slice((h // 2) * LANES, (h // 2 + 1) * LANES)
            keep = lo if h % 2 == 0 else jnp.logical_not(lo)
            q = mq_ref[b, :, slab]
            q = jnp.where(keep, q, jnp.zeros_like(q))
            k = mk_ref[b, :, slab]
            k = jnp.where(keep, k, jnp.zeros_like(k))
            v = mv_ref[b, :, h * MLSTM_DV:(h + 1) * MLSTM_DV]
            it_c = gc[:, h:h + 1]
            bt_c = btc[:, MLSTM_HEADS + h:MLSTM_HEADS + h + 1]
            it_r = gr[h:h + 1, :]
            bt_r = btr[MLSTM_HEADS + h:MLSTM_HEADS + h + 1, :]
            c_old = c_ref[idx]
            n_old = n_ref[idx]
            m_old = m_ref[idx][:, 0:1]

            num, ssum, mt, wi = _mlstm_intra(q, k, v, it_c, bt_c, it_r, bt_r, m_old, causal)
            num = num + wi * _dot_nt(q, c_old.astype(BF16))
            den = ssum + wi * jnp.sum(q.astype(F32) * n_old, axis=-1, keepdims=True)
            og = og_ref[b, :, h * MLSTM_DV:(h + 1) * MLSTM_DV].astype(F32)
            g_head = gh_ref[:, h * MLSTM_DV:(h + 1) * MLSTM_DV]
            hm_ref[b, :, h * MLSTM_DV:(h + 1) * MLSTM_DV] = _mlstm_head_out(
                num, den, mt, og, g_head).astype(hm_ref.dtype)

            b_last = bt_c[chunk - 1:chunk]
            m_new = mt[chunk - 1:chunk]
            wk = jnp.exp(b_last - bt_c + it_c - m_new)
            wc = jnp.exp(b_last + m_old - m_new)
            c_ref[idx] = wc * c_old + _dot_tn((wk * v.astype(F32)).astype(BF16), k)
            n_ref[idx] = wc * n_old + jnp.sum(wk * k.astype(F32), axis=0, keepdims=True)
            m_ref[idx] = jnp.broadcast_to(m_new, (1, LANES))


def _mlstm_prompt(mq, mk, mv, og, gcol, grow, g_head, nbatch, chunk):
    t = mq.shape[0]
    seq = t // nbatch
    nstate = nbatch * MLSTM_HEADS

    def seq3(a):
        return a.reshape(nbatch, seq, a.shape[-1])

    def chunks(width):
        return pl.BlockSpec((nbatch, chunk, width), lambda c: (0, c, 0))

    def state(shape):
        return pl.BlockSpec(shape, lambda c: (0, 0, 0))

    hm, c_st, n_st, m_st = pl.pallas_call(
        functools.partial(_mlstm_prompt_kernel, nbatch=nbatch, chunk=chunk),
        grid=(seq // chunk,),
        in_specs=[
            chunks(MQK_WIDTH), chunks(MQK_WIDTH), chunks(MLSTM_WIDTH), chunks(MLSTM_WIDTH),
            chunks(LANES),
            pl.BlockSpec((nbatch, GATE_ROWS, chunk), lambda c: (0, 0, c)),
            _const_spec((1, MLSTM_WIDTH)),
        ],
        out_specs=[
            chunks(MLSTM_WIDTH),
            state((nstate, MLSTM_DV, LANES)), state((nstate, 1, LANES)), state((nstate, 1, LANES)),
        ],
        out_shape=[
            jax.ShapeDtypeStruct((nbatch, seq, MLSTM_WIDTH), BF16),
            jax.ShapeDtypeStruct((nstate, MLSTM_DV, LANES), F32),
            jax.ShapeDtypeStruct((nstate, 1, LANES), F32),
            jax.ShapeDtypeStruct((nstate, 1, LANES), F32),
        ],
        compiler_params=pltpu.CompilerParams(
            dimension_semantics=("arbitrary",), vmem_limit_bytes=VMEM_LIMIT),
    )(seq3(mq), seq3(mk), seq3(mv), seq3(og), seq3(gcol), grow, g_head)

    c_st = c_st.reshape(nbatch, MLSTM_HEADS, MLSTM_DV, 2, MLSTM_DK)
    n_st = n_st.reshape(nbatch, MLSTM_HEADS, 2, MLSTM_DK)
    c_fin = jnp.stack([c_st[:, h, :, h % 2, :] for h in range(MLSTM_HEADS)], axis=1)
    n_fin = jnp.stack([n_st[:, h, h % 2, :] for h in range(MLSTM_HEADS)], axis=1)
    m_fin = m_st[:, 0, 0].reshape(nbatch, MLSTM_HEADS)
    return hm.reshape(t, MLSTM_WIDTH), c_fin, n_fin, m_fin


def _mlstm_sample_kernel(mq_ref, mk_ref, mv_ref, og_ref, gc_ref, gr_ref, mrow_ref, c_ref, n_ref,
                         gh_ref, hm_ref, co_ref, no_ref, mt_ref, *, nb, tq):
    rows = nb * tq
    shift = tq.bit_length() - 1
    r = lax.broadcasted_iota(jnp.int32, (rows, rows), 0)
    s = lax.broadcasted_iota(jnp.int32, (rows, rows), 1)
    same = (r >> shift) == (s >> shift)
    valid = same & (s <= r)
    tril = jnp.where(valid, 1.0, 0.0).astype(BF16)
    triu = jnp.where(same & (r <= s), 1.0, 0.0).astype(BF16)
    row_batch = lax.broadcasted_iota(jnp.int32, (rows, 1), 0) >> shift
    lane = lax.broadcasted_iota(jnp.int32, (rows, LANES), 1)

    gc = gc_ref[...]
    gr = gr_ref[0]
    btc = _cumsum_cols(tril, gc)
    btr = _cumsum_rows(gr, triu)
    m_rows = mrow_ref[...]
    mt_all = jnp.zeros((rows, LANES), F32)

    def last_of_batch(col):
        parts = [jnp.broadcast_to(col[(b + 1) * tq - 1:(b + 1) * tq], (tq, 1)) for b in range(nb)]
        return jnp.concatenate(parts, axis=0)

    for h in range(MLSTM_HEADS):
        q = mq_ref[:, h * MLSTM_DK:(h + 1) * MLSTM_DK]
        k = mk_ref[:, h * MLSTM_DK:(h + 1) * MLSTM_DK]
        v = mv_ref[:, h * MLSTM_DV:(h + 1) * MLSTM_DV]
        qb, kb, vb = q.astype(BF16), k.astype(BF16), v.astype(BF16)
        it_c = gc[:, h:h + 1]
        bt_c = btc[:, MLSTM_HEADS + h:MLSTM_HEADS + h + 1]
        it_r = gr[h:h + 1, :]
        bt_r = btr[MLSTM_HEADS + h:MLSTM_HEADS + h + 1, :]
        m_c = m_rows[:, h:h + 1]

        num, ssum, mt, wi = _mlstm_intra(qb, kb, vb, it_c, bt_c, it_r, bt_r, m_c, valid)
        qc_parts, n_parts = [], []
        for b in range(nb):
            idx = b * MLSTM_HEADS + h
            qc_parts.append(_dot_nt(qb, c_ref[idx].astype(BF16))[b * tq:(b + 1) * tq])
            n_parts.append(jnp.broadcast_to(n_ref[idx], (tq, MLSTM_DK)))
        qc = jnp.concatenate(qc_parts, axis=0)
        n_rows = jnp.concatenate(n_parts, axis=0)
        num = num + wi * qc
        den = ssum + wi * jnp.sum(qb.astype(F32) * n_rows, axis=-1, keepdims=True)
        og = og_ref[:, h * MLSTM_DV:(h + 1) * MLSTM_DV]
        g_head = gh_ref[:, h * MLSTM_DV:(h + 1) * MLSTM_DV]
        hm_ref[:, h * MLSTM_DV:(h + 1) * MLSTM_DV] = _mlstm_head_out(num, den, mt, og, g_head)

        b_last = last_of_batch(bt_c)
        m_new = last_of_batch(mt)
        wk = jnp.exp(b_last - bt_c + it_c - m_new)
        wc = jnp.exp(b_last + m_c - m_new)
        wkv = wk * vb.astype(F32)
        wkk = wk * kb.astype(F32)
        for b in range(nb):
            idx = b * MLSTM_HEADS + h
            wc_b = wc[b * tq:b * tq + 1]
            wkv_b = jnp.where(row_batch == b, wkv, 0.0).astype(BF16)
            co_ref[idx] = wc_b * c_ref[idx] + _dot_tn(wkv_b, kb)
            no_ref[idx] = wc_b * n_ref[idx] + jnp.sum(wkk[b * tq:(b + 1) * tq], axis=0, keepdims=True)
        mt_all = jnp.where(lane == h, mt, mt_all)
    mt_ref[...] = mt_all


def _mlstm_sample(mq, mk, mv, og, gcol, grow, m_rows, c_in, n_in, g_head, nbatch, nb):
    t = mq.shape[0]
    tq = t // nbatch
    assert tq & (tq - 1) == 0
    rows = nb * tq

    def tok(width):
        return pl.BlockSpec((rows, width), lambda i: (i, 0))

    def state(shape):
        return pl.BlockSpec((nb * MLSTM_HEADS,) + shape, lambda i: (i, 0, 0))

    return pl.pallas_call(
        functools.partial(_mlstm_sample_kernel, nb=nb, tq=tq),
        grid=(nbatch // nb,),
        in_specs=[
            tok(MQK_WIDTH), tok(MQK_WIDTH), tok(MLSTM_WIDTH), tok(MLSTM_WIDTH), tok(LANES),
            pl.BlockSpec((1, GATE_ROWS, rows), lambda i: (0, 0, i)),
            tok(LANES),
            state((MLSTM_DV, MLSTM_DK)), state((1, MLSTM_DK)),
            _const_spec((1, MLSTM_WIDTH)),
        ],
        out_specs=[tok(MLSTM_WIDTH), state((MLSTM_DV, MLSTM_DK)), state((1, MLSTM_DK)), tok(LANES)],
        out_shape=[
            jax.ShapeDtypeStruct((t, MLSTM_WIDTH), F32),
            jax.ShapeDtypeStruct(c_in.shape, F32),
            jax.ShapeDtypeStruct(n_in.shape, F32),
            jax.ShapeDtypeStruct((t, LANES), F32),
        ],
        compiler_params=pltpu.CompilerParams(
            dimension_semantics=("arbitrary",), vmem_limit_bytes=VMEM_LIMIT),
    )(mq, mk, mv, og, gcol, grow, m_rows, c_in, n_in, g_head)


def _memkv_kernel(mem_ref, g_ref, wk_ref, wv_ref, k_ref, v_ref):
    mn = _rmsnorm(mem_ref[...], g_ref[...]).astype(BF16)
    k_ref[...] = _dot(mn, wk_ref[...])
    v_ref[...] = _dot(mn, wv_ref[...])


def _memkv(mem, w, nbatch):
    t, d = mem.shape
    m = t // nbatch
    spec = pl.BlockSpec((m, CROSS_WIDTH), lambda b: (b, 0))
    return pl.pallas_call(
        _memkv_kernel,
        grid=(nbatch,),
        in_specs=[
            pl.BlockSpec((m, d), lambda b: (b, 0)),
            _const_spec((1, d)), _const_spec((d, CROSS_WIDTH)), _const_spec((d, CROSS_WIDTH)),
        ],
        out_specs=[spec, spec],
        out_shape=[jax.ShapeDtypeStruct((t, CROSS_WIDTH), F32)] * 2,
        compiler_params=pltpu.CompilerParams(
            dimension_semantics=("arbitrary",), vmem_limit_bytes=VMEM_LIMIT),
    )(mem, w["g_mem"], w["w_ck"], w["w_cv"])


def _mix_stage(x, att, hm, wo_ref, gcr_ref, wcq_ref):
    cat = jnp.concatenate([att.astype(BF16), hm.astype(BF16)], axis=1)
    y1 = x + _dot(cat, wo_ref[...])
    qc = _dot(_rmsnorm(y1, gcr_ref[...]).astype(BF16), wcq_ref[...])
    return y1, qc


def _cross_attention(qc, mk, mv):
    slabs = []
    for j in range(CROSS_WIDTH // LANES):
        qs = qc[:, j * LANES:(j + 1) * LANES]
        ks = mk[:, j * LANES:(j + 1) * LANES]
        vs = mv[:, j * LANES:(j + 1) * LANES]
        lo = _low_half(qs.shape)
        halves = []
        for keep in (lo, jnp.logical_not(lo)):
            s = _dot_nt(jnp.where(keep, qs, jnp.zeros_like(qs)), ks) * (HEAD_DIM ** -0.5)
            p = jnp.exp(s - jnp.max(s, axis=-1, keepdims=True))
            halves.append(_dot(p.astype(BF16), vs) / jnp.sum(p, axis=-1, keepdims=True))
        slabs.append(jnp.where(lo, halves[0], halves[1]))
    return jnp.concatenate(slabs, axis=1)


def _ffn_chunks(hidden):
    step = 512
    return [(c, min(c + step, hidden)) for c in range(0, hidden, step)]


def _ffn_stage(y1, o, wco_ref, gf_ref, wg_ref, wu_ref, wd_ref, act_ref, gfin_ref, final):
    y2 = y1 + _dot(o.astype(BF16), wco_ref[...])
    xn = _rmsnorm(y2, gf_ref[...]).astype(BF16)
    for lo, hi in _ffn_chunks(wg_ref.shape[1]):
        g = _dot(xn, wg_ref[:, lo:hi])
        u = _dot(xn, wu_ref[:, lo:hi])
        act_ref[:, lo:hi] = (g * jax.nn.sigmoid(g) * u).astype(BF16)
    y3 = y2 + _dot(act_ref[...], wd_ref[...])
    return _rmsnorm(y3, gfin_ref[...]) if final else y3


def _post_prompt_kernel(x_ref, att_ref, hm_ref, mk_ref, mv_ref, wo_ref, gcr_ref, wcq_ref, wco_ref,
                        gf_ref, wg_ref, wu_ref, wd_ref, gfin_ref, o_ref, act_ref, *, final):
    y1, qc = _mix_stage(x_ref[...], att_ref[...], hm_ref[...], wo_ref, gcr_ref, wcq_ref)
    o = _cross_attention(qc.astype(BF16), mk_ref[...].astype(BF16), mv_ref[...].astype(BF16))
    o_ref[...] = _ffn_stage(y1, o, wco_ref, gf_ref, wg_ref, wu_ref, wd_ref, act_ref, gfin_ref, final)


def _weight_specs(w, names):
    return [_const_spec(w[n].shape) for n in names]


_MIX_WEIGHTS = ("w_out", "g_cross", "w_cq")
_FFN_WEIGHTS = ("w_co", "g_ffn", "w_gate", "w_up", "w_down", "g_final")


def _post_prompt(x, att, hm, memk, memv, w, nbatch, final, tm):
    t, d = x.shape
    tpb = t // nbatch // tm
    m = memk.shape[0] // nbatch
    hidden = w["w_gate"].shape[1]

    def rows(width):
        return pl.BlockSpec((tm, width), lambda i: (i, 0))

    mem_spec = pl.BlockSpec((m, CROSS_WIDTH), lambda i: (i // tpb, 0))
    names = _MIX_WEIGHTS + _FFN_WEIGHTS
    return pl.pallas_call(
        functools.partial(_post_prompt_kernel, final=final),
        grid=(t // tm,),
        in_specs=[rows(d), rows(ATT_WIDTH), rows(MLSTM_WIDTH), mem_spec, mem_spec]
        + _weight_specs(w, names),
        out_specs=rows(d),
        out_shape=jax.ShapeDtypeStruct((t, d), F32),
        scratch_shapes=[pltpu.VMEM((tm, hidden), BF16)],
        compiler_params=pltpu.CompilerParams(
            dimension_semantics=("arbitrary",), vmem_limit_bytes=VMEM_LIMIT),
    )(x, att, hm, memk, memv, *[w[n] for n in names])


def _mix_kernel(x_ref, att_ref, hm_ref, wo_ref, gcr_ref, wcq_ref, y_ref, qc_ref):
    y1, qc = _mix_stage(x_ref[...], att_ref[...], hm_ref[...], wo_ref, gcr_ref, wcq_ref)
    y_ref[...] = y1
    qc_ref[...] = qc


def _mix(x, att, hm, w, tm):
    t, d = x.shape

    def rows(width):
        return pl.BlockSpec((tm, width), lambda i: (i, 0))

    return pl.pallas_call(
        _mix_kernel,
        grid=(t // tm,),
        in_specs=[rows(d), rows(ATT_WIDTH), rows(MLSTM_WIDTH)] + _weight_specs(w, _MIX_WEIGHTS),
        out_specs=[rows(d), rows(CROSS_WIDTH)],
        out_shape=[jax.ShapeDtypeStruct((t, d), F32), jax.ShapeDtypeStruct((t, CROSS_WIDTH), F32)],
        compiler_params=pltpu.CompilerParams(
            dimension_semantics=("arbitrary",), vmem_limit_bytes=VMEM_LIMIT),
    )(x, att, hm, *[w[n] for n in _MIX_WEIGHTS])


def _cross_sample_kernel(qc_ref, mk_ref, mv_ref, o_ref, *, nb, tq):
    mem = mk_ref.shape[0] // nb
    n = CROSS_HEADS * tq
    row_head = lax.broadcasted_iota(jnp.int32, (n, CROSS_WIDTH), 0) >> _log2(tq)
    lane_head = lax.broadcasted_iota(jnp.int32, (n, CROSS_WIDTH), 1) >> _log2(HEAD_DIM)
    own = row_head == lane_head
    for b in range(nb):
        q = qc_ref[b * tq:(b + 1) * tq, :]
        qx = jnp.where(own, jnp.concatenate([q] * CROSS_HEADS, axis=0), 0.0).astype(BF16)
        mk = mk_ref[b * mem:(b + 1) * mem, :].astype(BF16)
        mv = mv_ref[b * mem:(b + 1) * mem, :].astype(BF16)
        s = _dot_nt(qx, mk) * (HEAD_DIM ** -0.5)
        p = jnp.exp(s - jnp.max(s, axis=-1, keepdims=True))
        pv = _dot(p.astype(BF16), mv) / jnp.sum(p, axis=-1, keepdims=True)
        pv = jnp.where(own, pv, 0.0)
        out = pv[0:tq]
        for h in range(1, CROSS_HEADS):
            out = out + pv[h * tq:(h + 1) * tq]
        o_ref[b * tq:(b + 1) * tq, :] = out


def _cross_sample(qc, memk, memv, nbatch, nb):
    t = qc.shape[0]
    tq = t // nbatch
    mem = memk.shape[0] // nbatch
    tok = pl.BlockSpec((nb * tq, CROSS_WIDTH), lambda i: (i, 0))
    mspec = pl.BlockSpec((nb * mem, CROSS_WIDTH), lambda i: (i, 0))
    return pl.pallas_call(
        functools.partial(_cross_sample_kernel, nb=nb, tq=tq),
        grid=(nbatch // nb,),
        in_specs=[tok, mspec, mspec],
        out_specs=tok,
        out_shape=jax.ShapeDtypeStruct((t, CROSS_WIDTH), F32),
        compiler_params=pltpu.CompilerParams(
            dimension_semantics=("arbitrary",), vmem_limit_bytes=VMEM_LIMIT),
    )(qc, memk, memv)


def _ffn_kernel(y_ref, o_ref, wco_ref, gf_ref, wg_ref, wu_ref, wd_ref, gfin_ref, out_ref, act_ref,
                *, final):
    out_ref[...] = _ffn_stage(y_ref[...], o_ref[...], wco_ref, gf_ref, wg_ref, wu_ref, wd_ref,
                              act_ref, gfin_ref, final)


def _ffn(y1, o, w, final, tm):
    t, d = y1.shape
    hidden = w["w_gate"].shape[1]

    def rows(width):
        return pl.BlockSpec((tm, width), lambda i: (i, 0))

    return pl.pallas_call(
        functools.partial(_ffn_kernel, final=final),
        grid=(t // tm,),
        in_specs=[rows(d), rows(CROSS_WIDTH)] + _weight_specs(w, _FFN_WEIGHTS),
        out_specs=rows(d),
        out_shape=jax.ShapeDtypeStruct((t, d), F32),
        scratch_shapes=[pltpu.VMEM((tm, hidden), BF16)],
        compiler_params=pltpu.CompilerParams(
            dimension_semantics=("arbitrary",), vmem_limit_bytes=VMEM_LIMIT),
    )(y1, o, *[w[n] for n in _FFN_WEIGHTS])


def _layer_weights(l, w_in, b_igate, b_fgate, g_mlstm_head, w_out, g_mix, g_cross, g_mem, w_cq,
                   w_ck, w_cv, w_co, g_ffn, w_gate, w_up, w_down, g_final):
    d = w_in.shape[1]
    gate_w = w_in[l][:, MAIN_WIDTH:]
    gate_b = jnp.concatenate([b_igate[l], b_fgate[l]]).astype(F32)
    ngate = gate_b.shape[0]
    row = lambda a: a.astype(F32).reshape(1, -1)
    return {
        "w_main": w_in[l][:, :MAIN_WIDTH].astype(BF16),
        "w_gate_cols": jnp.pad(gate_w, ((0, 0), (0, LANES - ngate))).astype(BF16),
        "w_gate_rows": jnp.pad(gate_w.T, ((0, GATE_ROWS - ngate), (0, 0))).astype(BF16),
        "b_gate_cols": jnp.pad(gate_b, (0, LANES - ngate)).reshape(1, LANES),
        "b_gate_rows": jnp.pad(gate_b, (0, GATE_ROWS - ngate)).reshape(GATE_ROWS, 1),
        "g_mix": row(g_mix[l]), "g_cross": row(g_cross[l]), "g_mem": row(g_mem[l]),
        "g_ffn": row(g_ffn[l]), "g_final": row(g_final), "g_head": row(g_mlstm_head[l]),
        "w_out": w_out[l].astype(BF16), "w_cq": w_cq[l].astype(BF16), "w_ck": w_ck[l].astype(BF16),
        "w_cv": w_cv[l].astype(BF16), "w_co": w_co[l].astype(BF16), "w_gate": w_gate[l].astype(BF16),
        "w_up": w_up[l].astype(BF16), "w_down": w_down[l].astype(BF16),
    }


TOKEN_TILE = 512
SWA_QUERY_BLOCK = 512
MLSTM_CHUNK_ROWS = 256
SWA_SAMPLE_BATCH = 8
MLSTM_SAMPLE_BATCH = 16
CROSS_SAMPLE_BATCH = 8


def kernel(x_prompt, x_sample, mem_prompt, cache_swa_k, cache_swa_v, state_mlstm_C, state_mlstm_n,
           state_mlstm_m, cache_mem_k, cache_mem_v, w_in, b_igate, b_fgate, attn_sinks,
           g_mlstm_head, w_out, g_mix, g_cross, g_mem, w_cq, w_ck, w_cv, w_co, g_ffn, w_gate,
           w_up, w_down, g_final):
    depth = w_in.shape[0]
    bp, sp, d = x_prompt.shape
    bs, ss, _ = x_sample.shape
    mem_tokens = mem_prompt.shape[1]
    past = cache_swa_k.shape[2]
    yp = x_prompt.reshape(bp * sp, d)
    ys = x_sample.reshape(bs * ss, d)
    mem = mem_prompt.reshape(bp * mem_tokens, d)
    outs = [[] for _ in range(12)]

    for l in range(depth):
        final = l == depth - 1
        w = _layer_weights(l, w_in, b_igate, b_fgate, g_mlstm_head, w_out, g_mix, g_cross, g_mem,
                           w_cq, w_ck, w_cv, w_co, g_ffn, w_gate, w_up, w_down, g_final)
        sinks = attn_sinks[l].astype(F32)

        q, k, v, mq, mk, mv, og, gcol, grow = _proj(yp, w, bp, BF16, TOKEN_TILE)
        att = _swa_prompt(q, k, v, sinks, bp, SWA_QUERY_BLOCK)
        hm, c_p, n_p, m_p = _mlstm_prompt(mq, mk, mv, og, gcol, grow, w["g_head"], bp, MLSTM_CHUNK_ROWS)
        memk, memv = _memkv(mem, w, bp)
        yp = _post_prompt(yp, att, hm, memk, memv, w, bp, final, TOKEN_TILE)
        kv_shape = (bp, sp, ATT_KV_HEADS, HEAD_DIM)
        outs[0].append(k.reshape(kv_shape)[:, sp - WINDOW:])
        outs[1].append(v.reshape(kv_shape)[:, sp - WINDOW:])
        outs[2].append(c_p)
        outs[3].append(n_p)
        outs[4].append(m_p)
        mem_shape = (bp, mem_tokens, CROSS_HEADS, HEAD_DIM)
        outs[5].append(memk.reshape(mem_shape))
        outs[6].append(memv.reshape(mem_shape))

        q, k, v, mq, mk, mv, og, gcol, grow = _proj(ys, w, 1, F32, TOKEN_TILE)
        att, kbuf, vbuf = _swa_sample(
            q, k, v, cache_swa_k[l].reshape(bs * past, KV_WIDTH),
            cache_swa_v[l].reshape(bs * past, KV_WIDTH), sinks, bs, SWA_SAMPLE_BATCH)
        m_rows = jnp.pad(jnp.repeat(state_mlstm_m[l].astype(F32), ss, axis=0),
                         ((0, 0), (0, LANES - MLSTM_HEADS)))
        hm, c_s, n_s, mt = _mlstm_sample(
            mq, mk, mv, og, gcol, grow, m_rows,
            state_mlstm_C[l].astype(F32).reshape(bs * MLSTM_HEADS, MLSTM_DV, MLSTM_DK),
            state_mlstm_n[l].astype(F32).reshape(bs * MLSTM_HEADS, 1, MLSTM_DK),
            w["g_head"], bs, MLSTM_SAMPLE_BATCH)
        y1, qc = _mix(ys, att, hm, w, TOKEN_TILE)
        o = _cross_sample(
            qc, cache_mem_k[l].reshape(bs * mem_tokens, CROSS_WIDTH),
            cache_mem_v[l].reshape(bs * mem_tokens, CROSS_WIDTH), bs, CROSS_SAMPLE_BATCH)
        ys = _ffn(y1, o, w, final, TOKEN_TILE)
        buf_shape = (bs, past, ATT_KV_HEADS, HEAD_DIM)
        outs[7].append(kbuf.reshape(buf_shape))
        outs[8].append(vbuf.reshape(buf_shape))
        outs[9].append(c_s.reshape(bs, MLSTM_HEADS, MLSTM_DV, MLSTM_DK))
        outs[10].append(n_s.reshape(bs, MLSTM_HEADS, MLSTM_DK))
        outs[11].append(mt.reshape(bs, ss, LANES)[:, ss - 1, :MLSTM_HEADS])

    return (yp.reshape(bp, sp, d), ys.reshape(bs, ss, d)) + tuple(jnp.stack(o) for o in outs)
```

```python
import functools

import jax
import jax.numpy as jnp
from jax import lax
from jax.experimental import pallas as pl
from jax.experimental.pallas import tpu as pltpu

F32 = jnp.float32
BF16 = jnp.bfloat16

HEAD_DIM = 64
ATT_GROUP = 4
ATT_KV_HEADS = 2
ATT_WIDTH = 512
KV_WIDTH = ATT_KV_HEADS * HEAD_DIM
WINDOW = 128
LOG_WINDOW = 7
MLSTM_HEADS = 4
MLSTM_DV = 128
MLSTM_DK = 64
MQK_WIDTH = MLSTM_HEADS * MLSTM_DK
MLSTM_WIDTH = MLSTM_HEADS * MLSTM_DV
MAIN_WIDTH = ATT_WIDTH + 2 * KV_WIDTH + 2 * MQK_WIDTH + 2 * MLSTM_WIDTH
CROSS_HEADS = 4
CROSS_WIDTH = CROSS_HEADS * HEAD_DIM
EPS = 1e-6
NEG_INF = float("-inf")

LANES = 128
GATE_ROWS = 16
VMEM_LIMIT = 56 * 1024 * 1024

NT = (((1,), (1,)), ((), ()))
TN = (((0,), (0,)), ((), ()))


def _dot(a, b):
    return jnp.dot(a, b, preferred_element_type=F32)


def _dot_nt(a, b):
    return lax.dot_general(a, b, NT, preferred_element_type=F32)


def _dot_tn(a, b):
    return lax.dot_general(a, b, TN, preferred_element_type=F32)


def _rmsnorm(x, g):
    return x * lax.rsqrt(jnp.mean(x * x, axis=-1, keepdims=True) + EPS) * g


def _log_sigmoid(x):
    return jnp.minimum(x, 0.0) - jnp.log1p(jnp.exp(-jnp.abs(x)))


def _split3(x):
    hi = x.astype(BF16)
    r1 = x - hi.astype(F32)
    mid = r1.astype(BF16)
    lo = (r1 - mid.astype(F32)).astype(BF16)
    return hi, mid, lo


def _cumsum_cols(tri, x):
    hi, mid, lo = _split3(x)
    return _dot(tri, hi) + _dot(tri, mid) + _dot(tri, lo)


def _cumsum_rows(x, tri):
    hi, mid, lo = _split3(x)
    return _dot(hi, tri) + _dot(mid, tri) + _dot(lo, tri)


def _log2(n):
    assert n > 0 and n & (n - 1) == 0, n
    return n.bit_length() - 1


def _low_half(shape):
    return lax.broadcasted_iota(jnp.int32, shape, 1) < HEAD_DIM


def _proj_kernel(x_ref, g_ref, w_ref, wgc_ref, wgr_ref, bc_ref, br_ref, *rest, transposed):
    if transposed:
        wt_ref, rest = rest[0], rest[1:]
    q_ref, k_ref, v_ref, mq_ref, mk_ref, mv_ref, og_ref, gc_ref, gr_ref = rest[:9]
    xn = _rmsnorm(x_ref[...], g_ref[...]).astype(BF16)
    if transposed:
        kt_ref, vt_ref, mkt_ref = rest[9:]
        zt = _dot_nt(wt_ref[...], xn)
        kt_ref[...] = zt[:KV_WIDTH]
        vt_ref[...] = zt[KV_WIDTH:2 * KV_WIDTH]
        mkt_ref[...] = zt[2 * KV_WIDTH:] * (MLSTM_DK ** -0.5)

    def mm(lo, hi):
        return _dot(xn, w_ref[:, lo:hi])

    o = 0
    q_ref[...] = mm(o, o + ATT_WIDTH).astype(q_ref.dtype)
    o += ATT_WIDTH
    kv = mm(o, o + 2 * KV_WIDTH)
    k_ref[...] = kv[:, :KV_WIDTH]
    v_ref[...] = kv[:, KV_WIDTH:]
    o += 2 * KV_WIDTH
    mqk = mm(o, o + 2 * MQK_WIDTH)
    mq_ref[...] = mqk[:, :MQK_WIDTH].astype(mq_ref.dtype)
    mk_ref[...] = (mqk[:, MQK_WIDTH:] * (MLSTM_DK ** -0.5)).astype(mk_ref.dtype)
    o += 2 * MQK_WIDTH
    mv_ref[...] = mm(o, o + MLSTM_WIDTH).astype(mv_ref.dtype)
    o += MLSTM_WIDTH
    og_ref[...] = mm(o, o + MLSTM_WIDTH).astype(og_ref.dtype)

    zc = _dot(xn, wgc_ref[...]) + bc_ref[...]
    lane = lax.broadcasted_iota(jnp.int32, zc.shape, 1)
    gc_ref[...] = jnp.where(lane < MLSTM_HEADS, zc, _log_sigmoid(zc))
    zr = _dot_nt(wgr_ref[...], xn) + br_ref[...]
    row = lax.broadcasted_iota(jnp.int32, zr.shape, 0)
    gr_ref[0] = jnp.where(row < MLSTM_HEADS, zr, _log_sigmoid(zr))


def _const_spec(shape):
    nd = len(shape)
    return pl.BlockSpec(shape, lambda *_: (0,) * nd, pipeline_mode=pl.Buffered(1))


def _weight_specs(w, names):
    return [_const_spec(w[n].shape) for n in names]


def _proj(x, w, nseg, act_dtype, tm, transposed=False):
    t, d = x.shape
    seg = t // nseg
    tpb = seg // tm
    grid = (t // tm,)

    def rows(width):
        return pl.BlockSpec((tm, width), lambda i: (i, 0))

    out_shape = [
        jax.ShapeDtypeStruct((t, ATT_WIDTH), act_dtype),
        jax.ShapeDtypeStruct((t, KV_WIDTH), F32),
        jax.ShapeDtypeStruct((t, KV_WIDTH), F32),
        jax.ShapeDtypeStruct((t, MQK_WIDTH), act_dtype),
        jax.ShapeDtypeStruct((t, MQK_WIDTH), act_dtype),
        jax.ShapeDtypeStruct((t, MLSTM_WIDTH), act_dtype),
        jax.ShapeDtypeStruct((t, MLSTM_WIDTH), act_dtype),
        jax.ShapeDtypeStruct((t, LANES), F32),
        jax.ShapeDtypeStruct((nseg, GATE_ROWS, seg), F32),
    ]
    out_specs = [
        rows(ATT_WIDTH), rows(KV_WIDTH), rows(KV_WIDTH), rows(MQK_WIDTH), rows(MQK_WIDTH),
        rows(MLSTM_WIDTH), rows(MLSTM_WIDTH), rows(LANES),
        pl.BlockSpec((1, GATE_ROWS, tm), lambda i: (i // tpb, 0, i % tpb)),
    ]
    names = ["g_mix", "w_main", "w_gate_cols", "w_gate_rows", "b_gate_cols", "b_gate_rows"]
    if transposed:
        names.append("w_kvk_t")
        for width in (KV_WIDTH, KV_WIDTH, MQK_WIDTH):
            out_shape.append(jax.ShapeDtypeStruct((width, t), F32))
            out_specs.append(pl.BlockSpec((width, tm), lambda i: (0, i)))
    in_specs = [rows(d)] + _weight_specs(w, names)
    return pl.pallas_call(
        functools.partial(_proj_kernel, transposed=transposed),
        grid=grid,
        in_specs=in_specs,
        out_specs=out_specs,
        out_shape=out_shape,
        compiler_params=pltpu.CompilerParams(
            dimension_semantics=("arbitrary",), vmem_limit_bytes=VMEM_LIMIT),
    )(x, *[w[n] for n in names])


def _dup_heads(x):
    swapped = pltpu.roll(x, HEAD_DIM, axis=1)
    lo = _low_half(x.shape)
    return (jnp.where(lo, x, swapped).astype(BF16), jnp.where(lo, swapped, x).astype(BF16))


def _stack_query_heads(q, h):
    parts = []
    for g in range(ATT_GROUP):
        hq = h * ATT_GROUP + g
        slab = q[:, (hq // 2) * LANES:(hq // 2 + 1) * LANES]
        lo = _low_half(slab.shape)
        keep = lo if hq % 2 == 0 else jnp.logical_not(lo)
        parts.append(jnp.where(keep, slab, jnp.zeros_like(slab)))
    return jnp.concatenate(parts, axis=0).astype(BF16)


def _sink_column(sink_ref, h, rows_per_head):
    n = ATT_GROUP * rows_per_head
    grp = lax.broadcasted_iota(jnp.int32, (n, 1), 0) >> _log2(rows_per_head)
    col = jnp.full((n, 1), sink_ref[h * ATT_GROUP], F32)
    for g in range(1, ATT_GROUP):
        col = jnp.where(grp == g, sink_ref[h * ATT_GROUP + g], col)
    return col


def _sink_attention(qh, kd, vd, valid, sink):
    s = _dot_nt(qh, kd) * (HEAD_DIM ** -0.5)
    s = jnp.where(valid, s, NEG_INF)
    m = jnp.maximum(jnp.max(s, axis=-1, keepdims=True), sink)
    p = jnp.exp(s - m)
    denom = jnp.sum(p, axis=-1, keepdims=True) + jnp.exp(sink - m)
    return _dot(p.astype(BF16), vd) / denom


def _merge_query_heads(outs, rows_per_head):
    slabs = []
    for h in range(ATT_KV_HEADS):
        for j in range(ATT_GROUP // 2):
            even = outs[h][(2 * j) * rows_per_head:(2 * j + 1) * rows_per_head]
            odd = outs[h][(2 * j + 1) * rows_per_head:(2 * j + 2) * rows_per_head]
            slabs.append(jnp.where(_low_half(even.shape), even, odd))
    return jnp.concatenate(slabs, axis=1)


def _swa_prompt_kernel(sink_ref, q_ref, kc_ref, kp_ref, vc_ref, vp_ref, o_ref, *, qb):
    i = pl.program_id(1)
    k_all = jnp.concatenate([kp_ref[...], kc_ref[...]], axis=0)
    v_all = jnp.concatenate([vp_ref[...], vc_ref[...]], axis=0)
    kd = _dup_heads(k_all)
    vd = _dup_heads(v_all)

    n = ATT_GROUP * WINDOW
    row = lax.broadcasted_iota(jnp.int32, (n, 2 * WINDOW), 0) & (WINDOW - 1)
    col = lax.broadcasted_iota(jnp.int32, (n, 2 * WINDOW), 1)
    delta = col - row
    in_band = (delta >= 0) & (delta <= WINDOW)
    sinks = [_sink_column(sink_ref, h, WINDOW) for h in range(ATT_KV_HEADS)]

    for j in range(qb // WINDOW):
        valid = in_band & (col >= jnp.where(i == 0, WINDOW, 0)) if j == 0 else in_band
        q = q_ref[j * WINDOW:(j + 1) * WINDOW, :]
        outs = []
        for h in range(ATT_KV_HEADS):
            qh = _stack_query_heads(q, h)
            kband = kd[h][j * WINDOW:(j + 2) * WINDOW]
            vband = vd[h][j * WINDOW:(j + 2) * WINDOW]
            outs.append(_sink_attention(qh, kband, vband, valid, sinks[h]))
        o_ref[j * WINDOW:(j + 1) * WINDOW, :] = _merge_query_heads(outs, WINDOW).astype(o_ref.dtype)


def _swa_prompt(q, k, v, sinks, nbatch, qb):
    t = q.shape[0]
    seq = t // nbatch
    nq = seq // qb
    wpq = qb // WINDOW
    wps = seq // WINDOW

    def cur(width):
        return pl.BlockSpec((qb, width), lambda b, i: (b * nq + i, 0))

    def prev(width):
        return pl.BlockSpec(
            (WINDOW, width), lambda b, i: (b * wps + jnp.maximum(i * wpq - 1, 0), 0))

    return pl.pallas_call(
        functools.partial(_swa_prompt_kernel, qb=qb),
        grid=(nbatch, nq),
        in_specs=[
            pl.BlockSpec(memory_space=pltpu.SMEM),
            cur(ATT_WIDTH), cur(KV_WIDTH), prev(KV_WIDTH), cur(KV_WIDTH), prev(KV_WIDTH),
        ],
        out_specs=cur(ATT_WIDTH),
        out_shape=jax.ShapeDtypeStruct((t, ATT_WIDTH), BF16),
        compiler_params=pltpu.CompilerParams(
            dimension_semantics=("arbitrary", "arbitrary"), vmem_limit_bytes=VMEM_LIMIT),
    )(sinks, q, k, k, v, v)


def _swa_sample_kernel(sink_ref, q_ref, knt_ref, vnt_ref, ck_ref, cv_ref,
                       o_ref, kbuf_ref, vbuf_ref, *, nb, tq):
    past = ck_ref.shape[1]
    hw = ATT_KV_HEADS * HEAD_DIM
    heads = ATT_KV_HEADS * ATT_GROUP
    n = heads * tq
    fresh = past - tq
    t = lax.broadcasted_iota(jnp.int32, (n, 2 * past), 0) & (tq - 1)
    col = lax.broadcasted_iota(jnp.int32, (n, 2 * past), 1)
    valid = ((col < past) & (col >= t)) | ((col >= past + fresh) & (col - (past + fresh) <= t))
    sink = jnp.concatenate([_sink_column(sink_ref, h, tq) for h in range(ATT_KV_HEADS)], axis=0)
    lane = lax.broadcasted_iota(jnp.int32, (hw, past), 1)
    knt = knt_ref[...]
    vnt = vnt_ref[...]
    zero_slab = jnp.zeros((tq, LANES), F32)
    lo = _low_half((tq, LANES))

    def shifted(old, new_t, b):
        return jnp.where(lane >= fresh, pltpu.roll(new_t, (fresh - b * tq) % past, axis=1),
                         pltpu.roll(old, fresh, axis=1))

    def dup_rows(x):
        h0, h1 = x[:HEAD_DIM], x[HEAD_DIM:]
        return jnp.concatenate([h0, h0, h1, h1], axis=0)

    for b in range(nb):
        k_old = ck_ref[b * hw:(b + 1) * hw, :]
        v_old = cv_ref[b * hw:(b + 1) * hw, :]
        k_new = shifted(k_old, knt, b)
        v_new = shifted(v_old, vnt, b)
        kbuf_ref[b * hw:(b + 1) * hw, :] = k_new
        vbuf_ref[b * hw:(b + 1) * hw, :] = v_new
        kop = jnp.concatenate([dup_rows(k_old), dup_rows(k_new)], axis=1).astype(BF16)
        vop = jnp.concatenate([dup_rows(v_old), dup_rows(v_new)], axis=1).astype(BF16)

        q = q_ref[b * tq:(b + 1) * tq, :]
        blocks = []
        for hq in range(heads):
            slab = q[:, (hq // 2) * LANES:(hq // 2 + 1) * LANES]
            slab = jnp.where(lo if hq % 2 == 0 else jnp.logical_not(lo), slab, 0.0)
            pair = [slab, zero_slab] if hq // ATT_GROUP == 0 else [zero_slab, slab]
            blocks.append(jnp.concatenate(pair, axis=1))
        qx = jnp.concatenate(blocks, axis=0).astype(BF16)

        s = _dot(qx, kop) * (HEAD_DIM ** -0.5)
        s = jnp.where(valid, s, NEG_INF)
        m = jnp.maximum(jnp.max(s, axis=-1, keepdims=True), sink)
        p = jnp.exp(s - m)
        denom = jnp.sum(p, axis=-1, keepdims=True) + jnp.exp(sink - m)
        o = _dot_nt(p.astype(BF16), vop) / denom

        slabs = []
        for j in range(heads // 2):
            h = (2 * j) // ATT_GROUP
            even = o[(2 * j) * tq:(2 * j + 1) * tq, h * LANES:(h + 1) * LANES]
            odd = o[(2 * j + 1) * tq:(2 * j + 2) * tq, h * LANES:(h + 1) * LANES]
            slabs.append(jnp.where(lo, even, odd))
        o_ref[b * tq:(b + 1) * tq, :] = jnp.concatenate(slabs, axis=1)


def _swa_sample(q, knt, vnt, cache_kt, cache_vt, sinks, nbatch, nb):
    t = q.shape[0]
    tq = t // nbatch
    past = cache_kt.shape[1]
    hw = ATT_KV_HEADS * HEAD_DIM
    assert past == WINDOW and tq & (tq - 1) == 0 and nb * tq == past

    def rows(r, width):
        return pl.BlockSpec((r, width), lambda i: (i, 0))

    new_t = pl.BlockSpec((hw, nb * tq), lambda i: (0, i))
    return pl.pallas_call(
        functools.partial(_swa_sample_kernel, nb=nb, tq=tq),
        grid=(nbatch // nb,),
        in_specs=[
            pl.BlockSpec(memory_space=pltpu.SMEM),
            rows(nb * tq, ATT_WIDTH), new_t, new_t, rows(nb * hw, past), rows(nb * hw, past),
        ],
        out_specs=[rows(nb * tq, ATT_WIDTH), rows(nb * hw, past), rows(nb * hw, past)],
        out_shape=[
            jax.ShapeDtypeStruct((t, ATT_WIDTH), F32),
            jax.ShapeDtypeStruct(cache_kt.shape, F32),
            jax.ShapeDtypeStruct(cache_vt.shape, F32),
        ],
        compiler_params=pltpu.CompilerParams(
            dimension_semantics=("arbitrary",), vmem_limit_bytes=VMEM_LIMIT),
    )(sinks, q, knt, vnt, cache_kt, cache_vt)


def _mlstm_intra(q, k, v, it_c, bt_c, it_r, bt_r, m_c, valid):
    d = jnp.where(valid, bt_c - bt_r + it_r, NEG_INF)
    inter = bt_c + m_c
    mt = jnp.maximum(jnp.max(d, axis=-1, keepdims=True), inter)
    sm = _dot_nt(q, k) * jnp.exp(d - mt)
    wi = jnp.exp(inter - mt)
    return _dot(sm.astype(BF16), v), jnp.sum(sm, axis=-1, keepdims=True), mt, wi


def _mlstm_head_out(num, den, mt, og, g_head):
    lower = jnp.maximum(jnp.abs(den), jnp.exp(-mt))
    h = num / lower
    hn = h * lax.rsqrt(jnp.mean(h * h, axis=-1, keepdims=True) + EPS)
    return hn * g_head * jax.nn.sigmoid(og)


def _mlstm_prompt_kernel(mq_ref, mk_ref, mv_ref, og_ref, gc_ref, gr_ref, gh_ref,
                         hm_ref, c_ref, n_ref, m_ref, *, nbatch, chunk):
    @pl.when(pl.program_id(0) == 0)
    def _():
        c_ref[...] = jnp.zeros_like(c_ref)
        n_ref[...] = jnp.zeros_like(n_ref)
        m_ref[...] = jnp.zeros_like(m_ref)

    r = lax.broadcasted_iota(jnp.int32, (chunk, chunk), 0)
    s = lax.broadcasted_iota(jnp.int32, (chunk, chunk), 1)
    causal = s <= r
    tril = jnp.where(causal, 1.0, 0.0).astype(BF16)
    triu = jnp.where(r <= s, 1.0, 0.0).astype(BF16)
    lo = _low_half((chunk, LANES))

    for b in range(nbatch):
        gc = gc_ref[b]
        gr = gr_ref[b]
        btc = _cumsum_cols(tril, gc)
        btr = _cumsum_rows(gr, triu)
        for h in range(MLSTM_HEADS):
            idx = b * MLSTM_HEADS + h
            slab = slice((h // 2) * LANES, (h // 2 + 1) * LANES)
            keep = lo if h % 2 == 0 else jnp.logical_not(lo)
            q = mq_ref[b, :, slab]
            q = jnp.where(keep, q, jnp.zeros_like(q))
            k = mk_ref[b, :, slab]
            k = jnp.where(keep, k, jnp.zeros_like(k))
            v = mv_ref[b, :, h * MLSTM_DV:(h + 1) * MLSTM_DV]
            it_c = gc[:, h:h + 1]
            bt_c = btc[:, MLSTM_HEADS + h:MLSTM_HEADS + h + 1]
            it_r = gr[h:h + 1, :]
            bt_r = btr[MLSTM_HEADS + h:MLSTM_HEADS + h + 1, :]
            c_old = c_ref[idx]
            n_old = n_ref[idx]
            m_old = m_ref[idx][:, 0:1]

            num, ssum, mt, wi = _mlstm_intra(q, k, v, it_c, bt_c, it_r, bt_r, m_old, causal)
            num = num + wi * _dot_nt(q, c_old.astype(BF16))
            den = ssum + wi * jnp.sum(q.astype(F32) * n_old, axis=-1, keepdims=True)
            og = og_ref[b, :, h * MLSTM_DV:(h + 1) * MLSTM_DV].astype(F32)
            g_head = gh_ref[:, h * MLSTM_DV:(h + 1) * MLSTM_DV]
            hm_ref[b, :, h * MLSTM_DV:(h + 1) * MLSTM_DV] = _mlstm_head_out(
                num, den, mt, og, g_head).astype(hm_ref.dtype)

            b_last = bt_c[chunk - 1:chunk]
            m_new = mt[chunk - 1:chunk]
            wk = jnp.exp(b_last - bt_c + it_c - m_new)
            wc = jnp.exp(b_last + m_old - m_new)
            c_ref[idx] = wc * c_old + _dot_tn((wk * v.astype(F32)).astype(BF16), k)
            n_ref[idx] = wc * n_old + jnp.sum(wk * k.astype(F32), axis=0, keepdims=True)
            m_ref[idx] = jnp.broadcast_to(m_new, (1, LANES))


def _mlstm_prompt(mq, mk, mv, og, gcol, grow, g_head, nbatch, chunk):
    t = mq.shape[0]
    seq = t // nbatch
    nstate = nbatch * MLSTM_HEADS

    def seq3(a):
        return a.reshape(nbatch, seq, a.shape[-1])

    def chunks(width):
        return pl.BlockSpec((nbatch, chunk, width), lambda c: (0, c, 0))

    def state(shape):
        return pl.BlockSpec(shape, lambda c: (0, 0, 0))

    hm, c_st, n_st, m_st = pl.pallas_call(
        functools.partial(_mlstm_prompt_kernel, nbatch=nbatch, chunk=chunk),
        grid=(seq // chunk,),
        in_specs=[
            chunks(MQK_WIDTH), chunks(MQK_WIDTH), chunks(MLSTM_WIDTH), chunks(MLSTM_WIDTH),
            chunks(LANES),
            pl.BlockSpec((nbatch, GATE_ROWS, chunk), lambda c: (0, 0, c)),
            _const_spec((1, MLSTM_WIDTH)),
        ],
        out_specs=[
            chunks(MLSTM_WIDTH),
            state((nstate, MLSTM_DV, LANES)), state((nstate, 1, LANES)), state((nstate, 1, LANES)),
        ],
        out_shape=[
            jax.ShapeDtypeStruct((nbatch, seq, MLSTM_WIDTH), BF16),
            jax.ShapeDtypeStruct((nstate, MLSTM_DV, LANES), F32),
            jax.ShapeDtypeStruct((nstate, 1, LANES), F32),
            jax.ShapeDtypeStruct((nstate, 1, LANES), F32),
        ],
        compiler_params=pltpu.CompilerParams(
            dimension_semantics=("arbitrary",), vmem_limit_bytes=VMEM_LIMIT),
    )(seq3(mq), seq3(mk), seq3(mv), seq3(og), seq3(gcol), grow, g_head)

    c_st = c_st.reshape(nbatch, MLSTM_HEADS, MLSTM_DV, 2, MLSTM_DK)
    n_st = n_st.reshape(nbatch, MLSTM_HEADS, 2, MLSTM_DK)
    c_fin = jnp.stack([c_st[:, h, :, h % 2, :] for h in range(MLSTM_HEADS)], axis=1)
    n_fin = jnp.stack([n_st[:, h, h % 2, :] for h in range(MLSTM_HEADS)], axis=1)
    m_fin = m_st[:, 0, 0].reshape(nbatch, MLSTM_HEADS)
    return hm.reshape(t, MLSTM_WIDTH), c_fin, n_fin, m_fin


def _mlstm_sample_kernel(mq_ref, mk_ref, mkt_ref, mv_ref, og_ref, gc_ref, gr_ref, mrow_ref, c_ref,
                         n_ref, gh_ref, hm_ref, co_ref, no_ref, mt_ref, *, nb, tq):
    rows = nb * tq
    shift = tq.bit_length() - 1
    r = lax.broadcasted_iota(jnp.int32, (rows, rows), 0)
    s = lax.broadcasted_iota(jnp.int32, (rows, rows), 1)
    same = (r >> shift) == (s >> shift)
    valid = same & (s <= r)
    tril = jnp.where(valid, 1.0, 0.0).astype(BF16)
    triu = jnp.where(same & (r <= s), 1.0, 0.0).astype(BF16)
    row_batch = lax.broadcasted_iota(jnp.int32, (rows, 1), 0) >> shift
    lane = lax.broadcasted_iota(jnp.int32, (rows, LANES), 1)

    gc = gc_ref[...]
    gr = gr_ref[0]
    btc = _cumsum_cols(tril, gc)
    btr = _cumsum_rows(gr, triu)
    m_rows = mrow_ref[...]
    mt_all = jnp.zeros((rows, LANES), F32)

    def last_of_batch(col):
        parts = [jnp.broadcast_to(col[(b + 1) * tq - 1:(b + 1) * tq], (tq, 1)) for b in range(nb)]
        return jnp.concatenate(parts, axis=0)

    dk = MLSTM_DK
    lo = _low_half((rows, LANES))
    own_q = row_batch == (lax.broadcasted_iota(jnp.int32, (rows, nb * dk), 1) >> _log2(dk))
    lane_batch = lax.broadcasted_iota(jnp.int32, (dk, rows), 1) >> shift

    def state_rows(b, h):
        return slice((b * MLSTM_HEADS + h) * dk, (b * MLSTM_HEADS + h + 1) * dk)

    for h in range(MLSTM_HEADS):
        slab = slice((h // 2) * LANES, (h // 2 + 1) * LANES)
        qs, ks = mq_ref[:, slab], mk_ref[:, slab]
        qsw, ksw = pltpu.roll(qs, dk, axis=1), pltpu.roll(ks, dk, axis=1)
        q2 = jnp.where(lo, qs, qsw) if h % 2 == 0 else jnp.where(lo, qsw, qs)
        k2 = jnp.where(lo, ks, ksw) if h % 2 == 0 else jnp.where(lo, ksw, ks)
        qb, kb = q2[:, :dk].astype(BF16), k2[:, :dk].astype(BF16)
        vb = mv_ref[:, h * MLSTM_DV:(h + 1) * MLSTM_DV].astype(BF16)
        it_c = gc[:, h:h + 1]
        bt_c = btc[:, MLSTM_HEADS + h:MLSTM_HEADS + h + 1]
        it_r = gr[h:h + 1, :]
        bt_r = btr[MLSTM_HEADS + h:MLSTM_HEADS + h + 1, :]
        m_c = m_rows[:, h:h + 1]

        num, ssum, mt, wi = _mlstm_intra(qb, kb, vb, it_c, bt_c, it_r, bt_r, m_c, valid)
        qx = jnp.where(own_q, jnp.concatenate([q2] * (nb // 2), axis=1), 0.0).astype(BF16)
        c_stack = jnp.concatenate([c_ref[state_rows(b, h), :] for b in range(nb)], axis=0)
        qc = _dot(qx, c_stack.astype(BF16))
        n_rows = jnp.concatenate(
            [jnp.broadcast_to(n_ref[b * MLSTM_HEADS + h], (tq, dk)) for b in range(nb)], axis=0)
        num = num + wi * qc
        den = ssum + wi * jnp.sum(qb.astype(F32) * n_rows, axis=-1, keepdims=True)
        og = og_ref[:, h * MLSTM_DV:(h + 1) * MLSTM_DV]
        g_head = gh_ref[:, h * MLSTM_DV:(h + 1) * MLSTM_DV]
        hm_ref[:, h * MLSTM_DV:(h + 1) * MLSTM_DV] = _mlstm_head_out(num, den, mt, og, g_head)

        b_last = last_of_batch(bt_c)
        m_new = last_of_batch(mt)
        wk = jnp.exp(b_last - bt_c + it_c - m_new)
        wc = jnp.exp(b_last + m_c - m_new)
        wkv = (wk * vb.astype(F32)).astype(BF16)
        wkk = wk * kb.astype(F32)
        kt = mkt_ref[h * dk:(h + 1) * dk, :]
        kx = jnp.concatenate([jnp.where(lane_batch == b, kt, 0.0) for b in range(nb)], axis=0)
        upd = _dot(kx.astype(BF16), wkv)
        for b in range(nb):
            idx = b * MLSTM_HEADS + h
            wc_b = wc[b * tq:b * tq + 1]
            co_ref[state_rows(b, h), :] = wc_b * c_ref[state_rows(b, h), :] + upd[b * dk:(b + 1) * dk]
            no_ref[idx] = wc_b * n_ref[idx] + jnp.sum(wkk[b * tq:(b + 1) * tq], axis=0, keepdims=True)
        mt_all = jnp.where(lane == h, mt, mt_all)
    mt_ref[...] = mt_all


def _mlstm_sample(mq, mk, mkt, mv, og, gcol, grow, m_rows, ct_in, n_in, g_head, nbatch, nb):
    t = mq.shape[0]
    tq = t // nbatch
    rows = nb * tq
    assert tq & (tq - 1) == 0 and rows == LANES and nb % 2 == 0

    def tok(width):
        return pl.BlockSpec((rows, width), lambda i: (i, 0))

    ct_spec = pl.BlockSpec((nb * MLSTM_HEADS * MLSTM_DK, MLSTM_DV), lambda i: (i, 0))
    n_spec = pl.BlockSpec((nb * MLSTM_HEADS, 1, MLSTM_DK), lambda i: (i, 0, 0))
    return pl.pallas_call(
        functools.partial(_mlstm_sample_kernel, nb=nb, tq=tq),
        grid=(nbatch // nb,),
        in_specs=[
            tok(MQK_WIDTH), tok(MQK_WIDTH),
            pl.BlockSpec((MQK_WIDTH, rows), lambda i: (0, i)),
            tok(MLSTM_WIDTH), tok(MLSTM_WIDTH), tok(LANES),
            pl.BlockSpec((1, GATE_ROWS, rows), lambda i: (0, 0, i)),
            tok(LANES),
            ct_spec, n_spec,
            _const_spec((1, MLSTM_WIDTH)),
        ],
        out_specs=[tok(MLSTM_WIDTH), ct_spec, n_spec, tok(LANES)],
        out_shape=[
            jax.ShapeDtypeStruct((t, MLSTM_WIDTH), F32),
            jax.ShapeDtypeStruct(ct_in.shape, F32),
            jax.ShapeDtypeStruct(n_in.shape, F32),
            jax.ShapeDtypeStruct((t, LANES), F32),
        ],
        compiler_params=pltpu.CompilerParams(
            dimension_semantics=("arbitrary",), vmem_limit_bytes=VMEM_LIMIT),
    )(mq, mk, mkt, mv, og, gcol, grow, m_rows, ct_in, n_in, g_head)


def _memkv_kernel(mem_ref, g_ref, wk_ref, wv_ref, k_ref, v_ref):
    mn = _rmsnorm(mem_ref[...], g_ref[...]).astype(BF16)
    k_ref[...] = _dot(mn, wk_ref[...])
    v_ref[...] = _dot(mn, wv_ref[...])


def _memkv(mem, w, nbatch):
    t, d = mem.shape
    m = t // nbatch
    spec = pl.BlockSpec((m, CROSS_WIDTH), lambda b: (b, 0))
    return pl.pallas_call(
        _memkv_kernel,
        grid=(nbatch,),
        in_specs=[
            pl.BlockSpec((m, d), lambda b: (b, 0)),
            _const_spec((1, d)), _const_spec((d, CROSS_WIDTH)), _const_spec((d, CROSS_WIDTH)),
        ],
        out_specs=[spec, spec],
        out_shape=[jax.ShapeDtypeStruct((t, CROSS_WIDTH), F32)] * 2,
        compiler_params=pltpu.CompilerParams(
            dimension_semantics=("arbitrary",), vmem_limit_bytes=VMEM_LIMIT),
    )(mem, w["g_mem"], w["w_ck"], w["w_cv"])


def _mix_stage(x, att, hm, wo_ref, gcr_ref, wcq_ref):
    cat = jnp.concatenate([att.astype(BF16), hm.astype(BF16)], axis=1)
    y1 = x + _dot(cat, wo_ref[...])
    qc = _dot(_rmsnorm(y1, gcr_ref[...]).astype(BF16), wcq_ref[...])
    return y1, qc


def _cross_attention(qc, mk, mv):
    slabs = []
    for j in range(CROSS_WIDTH // LANES):
        qs = qc[:, j * LANES:(j + 1) * LANES]
        ks = mk[:, j * LANES:(j + 1) * LANES]
        vs = mv[:, j * LANES:(j + 1) * LANES]
        lo = _low_half(qs.shape)
        halves = []
        for keep in (lo, jnp.logical_not(lo)):
            s = _dot_nt(jnp.where(keep, qs, jnp.zeros_like(qs)), ks) * (HEAD_DIM ** -0.5)
            p = jnp.exp(s - jnp.max(s, axis=-1, keepdims=True))
            halves.append(_dot(p.astype(BF16), vs) / jnp.sum(p, axis=-1, keepdims=True))
        slabs.append(jnp.where(lo, halves[0], halves[1]))
    return jnp.concatenate(slabs, axis=1)


def _ffn_chunks(hidden):
    step = 512
    return [(c, min(c + step, hidden)) for c in range(0, hidden, step)]


def _ffn_stage(y1, o, wco_ref, gf_ref, wg_ref, wu_ref, wd_ref, act_ref, gfin_ref, final):
    y2 = y1 + _dot(o.astype(BF16), wco_ref[...])
    xn = _rmsnorm(y2, gf_ref[...]).astype(BF16)
    for lo, hi in _ffn_chunks(wg_ref.shape[1]):
        g = _dot(xn, wg_ref[:, lo:hi])
        u = _dot(xn, wu_ref[:, lo:hi])
        act_ref[:, lo:hi] = (g * jax.nn.sigmoid(g) * u).astype(BF16)
    y3 = y2 + _dot(act_ref[...], wd_ref[...])
    return _rmsnorm(y3, gfin_ref[...]) if final else y3


def _post_prompt_kernel(x_ref, att_ref, hm_ref, mk_ref, mv_ref, wo_ref, gcr_ref, wcq_ref, wco_ref,
                        gf_ref, wg_ref, wu_ref, wd_ref, gfin_ref, o_ref, act_ref, *, final):
    y1, qc = _mix_stage(x_ref[...], att_ref[...], hm_ref[...], wo_ref, gcr_ref, wcq_ref)
    o = _cross_attention(qc.astype(BF16), mk_ref[...].astype(BF16), mv_ref[...].astype(BF16))
    o_ref[...] = _ffn_stage(y1, o, wco_ref, gf_ref, wg_ref, wu_ref, wd_ref, act_ref, gfin_ref, final)


_MIX_WEIGHTS = ("w_out", "g_cross", "w_cq")
_FFN_WEIGHTS = ("w_co", "g_ffn", "w_gate", "w_up", "w_down", "g_final")


def _post_prompt(x, att, hm, memk, memv, w, nbatch, final, tm):
    t, d = x.shape
    tpb = t // nbatch // tm
    m = memk.shape[0] // nbatch
    hidden = w["w_gate"].shape[1]

    def rows(width):
        return pl.BlockSpec((tm, width), lambda i: (i, 0))

    mem_spec = pl.BlockSpec((m, CROSS_WIDTH), lambda i: (i // tpb, 0))
    names = _MIX_WEIGHTS + _FFN_WEIGHTS
    return pl.pallas_call(
        functools.partial(_post_prompt_kernel, final=final),
        grid=(t // tm,),
        in_specs=[rows(d), rows(ATT_WIDTH), rows(MLSTM_WIDTH), mem_spec, mem_spec]
        + _weight_specs(w, names),
        out_specs=rows(d),
        out_shape=jax.ShapeDtypeStruct((t, d), F32),
        scratch_shapes=[pltpu.VMEM((tm, hidden), BF16)],
        compiler_params=pltpu.CompilerParams(
            dimension_semantics=("arbitrary",), vmem_limit_bytes=VMEM_LIMIT),
    )(x, att, hm, memk, memv, *[w[n] for n in names])


def _mix_kernel(x_ref, att_ref, hm_ref, wo_ref, gcr_ref, wcq_ref, y_ref, qc_ref):
    y1, qc = _mix_stage(x_ref[...], att_ref[...], hm_ref[...], wo_ref, gcr_ref, wcq_ref)
    y_ref[...] = y1
    qc_ref[...] = qc


def _mix(x, att, hm, w, tm):
    t, d = x.shape

    def rows(width):
        return pl.BlockSpec((tm, width), lambda i: (i, 0))

    return pl.pallas_call(
        _mix_kernel,
        grid=(t // tm,),
        in_specs=[rows(d), rows(ATT_WIDTH), rows(MLSTM_WIDTH)] + _weight_specs(w, _MIX_WEIGHTS),
        out_specs=[rows(d), rows(CROSS_WIDTH)],
        out_shape=[jax.ShapeDtypeStruct((t, d), F32), jax.ShapeDtypeStruct((t, CROSS_WIDTH), F32)],
        compiler_params=pltpu.CompilerParams(
            dimension_semantics=("arbitrary",), vmem_limit_bytes=VMEM_LIMIT),
    )(x, att, hm, *[w[n] for n in _MIX_WEIGHTS])


def _cross_sample_kernel(qc_ref, mk_ref, mv_ref, o_ref, *, nb, tq):
    mem = CROSS_WIDTH
    n = CROSS_HEADS * tq
    row_head = lax.broadcasted_iota(jnp.int32, (n, CROSS_WIDTH), 0) >> _log2(tq)
    lane_head = lax.broadcasted_iota(jnp.int32, (n, CROSS_WIDTH), 1) >> _log2(HEAD_DIM)
    own = row_head == lane_head
    for b in range(nb):
        q = qc_ref[b * tq:(b + 1) * tq, :]
        qx = jnp.where(own, jnp.concatenate([q] * CROSS_HEADS, axis=0), 0.0).astype(BF16)
        mk = mk_ref[b * mem:(b + 1) * mem, :].astype(BF16)
        mv = mv_ref[b * mem:(b + 1) * mem, :].astype(BF16)
        s = _dot(qx, mk) * (HEAD_DIM ** -0.5)
        p = jnp.exp(s - jnp.max(s, axis=-1, keepdims=True))
        pv = _dot_nt(p.astype(BF16), mv) / jnp.sum(p, axis=-1, keepdims=True)
        pv = jnp.where(own, pv, 0.0)
        out = pv[0:tq]
        for h in range(1, CROSS_HEADS):
            out = out + pv[h * tq:(h + 1) * tq]
        o_ref[b * tq:(b + 1) * tq, :] = out


def _cross_sample(qc, memk_t, memv_t, nbatch, nb):
    t = qc.shape[0]
    tq = t // nbatch
    mem = memk_t.shape[1]
    tok = pl.BlockSpec((nb * tq, CROSS_WIDTH), lambda i: (i, 0))
    mspec = pl.BlockSpec((nb * CROSS_WIDTH, mem), lambda i: (i, 0))
    return pl.pallas_call(
        functools.partial(_cross_sample_kernel, nb=nb, tq=tq),
        grid=(nbatch // nb,),
        in_specs=[tok, mspec, mspec],
        out_specs=tok,
        out_shape=jax.ShapeDtypeStruct((t, CROSS_WIDTH), F32),
        compiler_params=pltpu.CompilerParams(
            dimension_semantics=("arbitrary",), vmem_limit_bytes=VMEM_LIMIT),
    )(qc, memk_t, memv_t)


def _ffn_kernel(y_ref, o_ref, wco_ref, gf_ref, wg_ref, wu_ref, wd_ref, gfin_ref, out_ref, act_ref,
                *, final):
    out_ref[...] = _ffn_stage(y_ref[...], o_ref[...], wco_ref, gf_ref, wg_ref, wu_ref, wd_ref,
                              act_ref, gfin_ref, final)


def _ffn(y1, o, w, final, tm):
    t, d = y1.shape
    hidden = w["w_gate"].shape[1]

    def rows(width):
        return pl.BlockSpec((tm, width), lambda i: (i, 0))

    return pl.pallas_call(
        functools.partial(_ffn_kernel, final=final),
        grid=(t // tm,),
        in_specs=[rows(d), rows(CROSS_WIDTH)] + _weight_specs(w, _FFN_WEIGHTS),
        out_specs=rows(d),
        out_shape=jax.ShapeDtypeStruct((t, d), F32),
        scratch_shapes=[pltpu.VMEM((tm, hidden), BF16)],
        compiler_params=pltpu.CompilerParams(
            dimension_semantics=("arbitrary",), vmem_limit_bytes=VMEM_LIMIT),
    )(y1, o, *[w[n] for n in _FFN_WEIGHTS])


def _layer_weights(l, w_in, b_igate, b_fgate, g_mlstm_head, w_out, g_mix, g_cross, g_mem, w_cq,
                   w_ck, w_cv, w_co, g_ffn, w_gate, w_up, w_down, g_final):
    d = w_in.shape[1]
    gate_w = w_in[l][:, MAIN_WIDTH:]
    gate_b = jnp.concatenate([b_igate[l], b_fgate[l]]).astype(F32)
    ngate = gate_b.shape[0]
    row = lambda a: a.astype(F32).reshape(1, -1)
    w_t = w_in[l].T
    k0 = ATT_WIDTH
    mk0 = ATT_WIDTH + 2 * KV_WIDTH + MQK_WIDTH
    return {
        "w_kvk_t": jnp.concatenate([w_t[k0:k0 + 2 * KV_WIDTH], w_t[mk0:mk0 + MQK_WIDTH]]).astype(BF16),
        "w_main": w_in[l][:, :MAIN_WIDTH].astype(BF16),
        "w_gate_cols": jnp.pad(gate_w, ((0, 0), (0, LANES - ngate))).astype(BF16),
        "w_gate_rows": jnp.pad(gate_w.T, ((0, GATE_ROWS - ngate), (0, 0))).astype(BF16),
        "b_gate_cols": jnp.pad(gate_b, (0, LANES - ngate)).reshape(1, LANES),
        "b_gate_rows": jnp.pad(gate_b, (0, GATE_ROWS - ngate)).reshape(GATE_ROWS, 1),
        "g_mix": row(g_mix[l]), "g_cross": row(g_cross[l]), "g_mem": row(g_mem[l]),
        "g_ffn": row(g_ffn[l]), "g_final": row(g_final), "g_head": row(g_mlstm_head[l]),
        "w_out": w_out[l].astype(BF16), "w_cq": w_cq[l].astype(BF16), "w_ck": w_ck[l].astype(BF16),
        "w_cv": w_cv[l].astype(BF16), "w_co": w_co[l].astype(BF16), "w_gate": w_gate[l].astype(BF16),
        "w_up": w_up[l].astype(BF16), "w_down": w_down[l].astype(BF16),
    }


TOKEN_TILE = 512
SWA_QUERY_BLOCK = 512
MLSTM_CHUNK_ROWS = 256
SWA_SAMPLE_BATCH = 16
MLSTM_SAMPLE_BATCH = 16
CROSS_SAMPLE_BATCH = 8


def kernel(x_prompt, x_sample, mem_prompt, cache_swa_k, cache_swa_v, state_mlstm_C, state_mlstm_n,
           state_mlstm_m, cache_mem_k, cache_mem_v, w_in, b_igate, b_fgate, attn_sinks,
           g_mlstm_head, w_out, g_mix, g_cross, g_mem, w_cq, w_ck, w_cv, w_co, g_ffn, w_gate,
           w_up, w_down, g_final):
    depth = w_in.shape[0]
    bp, sp, d = x_prompt.shape
    bs, ss, _ = x_sample.shape
    mem_tokens = mem_prompt.shape[1]
    past = cache_swa_k.shape[2]
    yp = x_prompt.reshape(bp * sp, d)
    ys = x_sample.reshape(bs * ss, d)
    mem = mem_prompt.reshape(bp * mem_tokens, d)
    outs = [[] for _ in range(12)]

    for l in range(depth):
        final = l == depth - 1
        w = _layer_weights(l, w_in, b_igate, b_fgate, g_mlstm_head, w_out, g_mix, g_cross, g_mem,
                           w_cq, w_ck, w_cv, w_co, g_ffn, w_gate, w_up, w_down, g_final)
        sinks = attn_sinks[l].astype(F32)

        q, k, v, mq, mk, mv, og, gcol, grow = _proj(yp, w, bp, BF16, TOKEN_TILE)
        att = _swa_prompt(q, k, v, sinks, bp, SWA_QUERY_BLOCK)
        hm, c_p, n_p, m_p = _mlstm_prompt(mq, mk, mv, og, gcol, grow, w["g_head"], bp, MLSTM_CHUNK_ROWS)
        memk, memv = _memkv(mem, w, bp)
        yp = _post_prompt(yp, att, hm, memk, memv, w, bp, final, TOKEN_TILE)
        win_shape = (bp, WINDOW, ATT_KV_HEADS, HEAD_DIM)
        outs[0].append(k.reshape(bp, sp, KV_WIDTH)[:, sp - WINDOW:].reshape(win_shape))
        outs[1].append(v.reshape(bp, sp, KV_WIDTH)[:, sp - WINDOW:].reshape(win_shape))
        outs[2].append(c_p)
        outs[3].append(n_p)
        outs[4].append(m_p)
        mem_shape = (bp, mem_tokens, CROSS_HEADS, HEAD_DIM)
        outs[5].append(memk.reshape(mem_shape))
        outs[6].append(memv.reshape(mem_shape))

        def tokens_last(a):
            return jnp.transpose(a.astype(F32), (0, 2, 3, 1)).reshape(-1, a.shape[1])

        q, _, _, mq, mk, mv, og, gcol, grow, knt, vnt, mkt = _proj(
            ys, w, 1, F32, TOKEN_TILE, transposed=True)
        att, kbuf_t, vbuf_t = _swa_sample(
            q, knt, vnt, tokens_last(cache_swa_k[l]), tokens_last(cache_swa_v[l]), sinks, bs,
            SWA_SAMPLE_BATCH)
        m_rows = jnp.pad(jnp.repeat(state_mlstm_m[l].astype(F32), ss, axis=0),
                         ((0, 0), (0, LANES - MLSTM_HEADS)))
        ct_in = jnp.swapaxes(state_mlstm_C[l].astype(F32), 2, 3).reshape(-1, MLSTM_DV)
        hm, ct_s, n_s, mt = _mlstm_sample(
            mq, mk, mkt, mv, og, gcol, grow, m_rows, ct_in,
            state_mlstm_n[l].astype(F32).reshape(bs * MLSTM_HEADS, 1, MLSTM_DK),
            w["g_head"], bs, MLSTM_SAMPLE_BATCH)
        y1, qc = _mix(ys, att, hm, w, TOKEN_TILE)
        o = _cross_sample(qc, tokens_last(cache_mem_k[l]), tokens_last(cache_mem_v[l]), bs,
                          CROSS_SAMPLE_BATCH)
        ys = _ffn(y1, o, w, final, TOKEN_TILE)
        buf_t_shape = (bs, ATT_KV_HEADS, HEAD_DIM, past)
        outs[7].append(jnp.transpose(kbuf_t.reshape(buf_t_shape), (0, 3, 1, 2)))
        outs[8].append(jnp.transpose(vbuf_t.reshape(buf_t_shape), (0, 3, 1, 2)))
        outs[9].append(jnp.swapaxes(ct_s.reshape(bs, MLSTM_HEADS, MLSTM_DK, MLSTM_DV), 2, 3))
        outs[10].append(n_s.reshape(bs, MLSTM_HEADS, MLSTM_DK))
        outs[11].append(mt.reshape(bs, ss, LANES)[:, ss - 1, :MLSTM_HEADS])

    return (yp.reshape(bp, sp, d), ys.reshape(bs, ss, d)) + tuple(jnp.stack(o) for o in outs)
```

```python
import functools

import jax
import jax.numpy as jnp
from jax import lax
from jax.experimental import pallas as pl
from jax.experimental.pallas import tpu as pltpu

F32 = jnp.float32
BF16 = jnp.bfloat16

HEAD_DIM = 64
ATT_GROUP = 4
ATT_KV_HEADS = 2
ATT_WIDTH = 512
KV_WIDTH = ATT_KV_HEADS * HEAD_DIM
WINDOW = 128
LOG_WINDOW = 7
MLSTM_HEADS = 4
MLSTM_DV = 128
MLSTM_DK = 64
MQK_WIDTH = MLSTM_HEADS * MLSTM_DK
MLSTM_WIDTH = MLSTM_HEADS * MLSTM_DV
MAIN_WIDTH = ATT_WIDTH + 2 * KV_WIDTH + 2 * MQK_WIDTH + 2 * MLSTM_WIDTH
CROSS_HEADS = 4
CROSS_WIDTH = CROSS_HEADS * HEAD_DIM
EPS = 1e-6
NEG_INF = float("-inf")

LANES = 128
GATE_ROWS = 16
VMEM_LIMIT = 56 * 1024 * 1024

NT = (((1,), (1,)), ((), ()))
TN = (((0,), (0,)), ((), ()))


def _dot(a, b):
    return jnp.dot(a, b, preferred_element_type=F32)


def _dot_nt(a, b):
    return lax.dot_general(a, b, NT, preferred_element_type=F32)


def _dot_tn(a, b):
    return lax.dot_general(a, b, TN, preferred_element_type=F32)


def _rmsnorm(x, g):
    return x * lax.rsqrt(jnp.mean(x * x, axis=-1, keepdims=True) + EPS) * g


def _log_sigmoid(x):
    return jnp.minimum(x, 0.0) - jnp.log1p(jnp.exp(-jnp.abs(x)))


def _split3(x):
    hi = x.astype(BF16)
    r1 = x - hi.astype(F32)
    mid = r1.astype(BF16)
    lo = (r1 - mid.astype(F32)).astype(BF16)
    return hi, mid, lo


def _cumsum_cols(tri, x):
    hi, mid, lo = _split3(x)
    return _dot(tri, hi) + _dot(tri, mid) + _dot(tri, lo)


def _cumsum_rows(x, tri):
    hi, mid, lo = _split3(x)
    return _dot(hi, tri) + _dot(mid, tri) + _dot(lo, tri)


def _log2(n):
    assert n > 0 and n & (n - 1) == 0, n
    return n.bit_length() - 1


def _low_half(shape):
    return lax.broadcasted_iota(jnp.int32, shape, 1) < HEAD_DIM


def _proj_kernel(x_ref, g_ref, w_ref, wgc_ref, wgr_ref, bc_ref, br_ref, *rest, transposed):
    if transposed:
        wt_ref, rest = rest[0], rest[1:]
    q_ref, k_ref, v_ref, mq_ref, mk_ref, mv_ref, og_ref, gc_ref, gr_ref = rest[:9]
    xn = _rmsnorm(x_ref[...], g_ref[...]).astype(BF16)
    if transposed:
        kt_ref, vt_ref, mkt_ref = rest[9:]
        zt = _dot_nt(wt_ref[...], xn)
        kt_ref[...] = zt[:KV_WIDTH]
        vt_ref[...] = zt[KV_WIDTH:2 * KV_WIDTH]
        mkt_ref[...] = zt[2 * KV_WIDTH:] * (MLSTM_DK ** -0.5)

    def mm(lo, hi):
        return _dot(xn, w_ref[:, lo:hi])

    o = 0
    q_ref[...] = mm(o, o + ATT_WIDTH).astype(q_ref.dtype)
    o += ATT_WIDTH
    kv = mm(o, o + 2 * KV_WIDTH)
    k_ref[...] = kv[:, :KV_WIDTH]
    v_ref[...] = kv[:, KV_WIDTH:]
    o += 2 * KV_WIDTH
    mqk = mm(o, o + 2 * MQK_WIDTH)
    mq_ref[...] = mqk[:, :MQK_WIDTH].astype(mq_ref.dtype)
    mk_ref[...] = (mqk[:, MQK_WIDTH:] * (MLSTM_DK ** -0.5)).astype(mk_ref.dtype)
    o += 2 * MQK_WIDTH
    mv_ref[...] = mm(o, o + MLSTM_WIDTH).astype(mv_ref.dtype)
    o += MLSTM_WIDTH
    og_ref[...] = mm(o, o + MLSTM_WIDTH).astype(og_ref.dtype)

    zc = _dot(xn, wgc_ref[...]) + bc_ref[...]
    lane = lax.broadcasted_iota(jnp.int32, zc.shape, 1)
    gc_ref[...] = jnp.where(lane < MLSTM_HEADS, zc, _log_sigmoid(zc))
    zr = _dot_nt(wgr_ref[...], xn) + br_ref[...]
    row = lax.broadcasted_iota(jnp.int32, zr.shape, 0)
    gr_ref[0] = jnp.where(row < MLSTM_HEADS, zr, _log_sigmoid(zr))


def _const_spec(shape):
    nd = len(shape)
    return pl.BlockSpec(shape, lambda *_: (0,) * nd, pipeline_mode=pl.Buffered(1))


def _weight_specs(w, names):
    return [_const_spec(w[n].shape) for n in names]


def _proj(x, w, nseg, act_dtype, tm, transposed=False):
    t, d = x.shape
    seg = t // nseg
    tpb = seg // tm
    grid = (t // tm,)

    def rows(width):
        return pl.BlockSpec((tm, width), lambda i: (i, 0))

    out_shape = [
        jax.ShapeDtypeStruct((t, ATT_WIDTH), act_dtype),
        jax.ShapeDtypeStruct((t, KV_WIDTH), F32),
        jax.ShapeDtypeStruct((t, KV_WIDTH), F32),
        jax.ShapeDtypeStruct((t, MQK_WIDTH), act_dtype),
        jax.ShapeDtypeStruct((t, MQK_WIDTH), act_dtype),
        jax.ShapeDtypeStruct((t, MLSTM_WIDTH), act_dtype),
        jax.ShapeDtypeStruct((t, MLSTM_WIDTH), act_dtype),
        jax.ShapeDtypeStruct((t, LANES), F32),
        jax.ShapeDtypeStruct((nseg, GATE_ROWS, seg), F32),
    ]
    out_specs = [
        rows(ATT_WIDTH), rows(KV_WIDTH), rows(KV_WIDTH), rows(MQK_WIDTH), rows(MQK_WIDTH),
        rows(MLSTM_WIDTH), rows(MLSTM_WIDTH), rows(LANES),
        pl.BlockSpec((1, GATE_ROWS, tm), lambda i: (i // tpb, 0, i % tpb)),
    ]
    names = ["g_mix", "w_main", "w_gate_cols", "w_gate_rows", "b_gate_cols", "b_gate_rows"]
    if transposed:
        names.append("w_kvk_t")
        for width in (KV_WIDTH, KV_WIDTH, MQK_WIDTH):
            out_shape.append(jax.ShapeDtypeStruct((width, t), F32))
            out_specs.append(pl.BlockSpec((width, tm), lambda i: (0, i)))
    in_specs = [rows(d)] + _weight_specs(w, names)
    return pl.pallas_call(
        functools.partial(_proj_kernel, transposed=transposed),
        grid=grid,
        in_specs=in_specs,
        out_specs=out_specs,
        out_shape=out_shape,
        compiler_params=pltpu.CompilerParams(
            dimension_semantics=("arbitrary",), vmem_limit_bytes=VMEM_LIMIT),
    )(x, *[w[n] for n in names])


def _dup_heads(x):
    swapped = pltpu.roll(x, HEAD_DIM, axis=1)
    lo = _low_half(x.shape)
    return (jnp.where(lo, x, swapped).astype(BF16), jnp.where(lo, swapped, x).astype(BF16))


def _stack_query_heads(q, h):
    parts = []
    for g in range(ATT_GROUP):
        hq = h * ATT_GROUP + g
        slab = q[:, (hq // 2) * LANES:(hq // 2 + 1) * LANES]
        lo = _low_half(slab.shape)
        keep = lo if hq % 2 == 0 else jnp.logical_not(lo)
        parts.append(jnp.where(keep, slab, jnp.zeros_like(slab)))
    return jnp.concatenate(parts, axis=0).astype(BF16)


def _sink_column(sink_ref, h, rows_per_head):
    n = ATT_GROUP * rows_per_head
    grp = lax.broadcasted_iota(jnp.int32, (n, 1), 0) >> _log2(rows_per_head)
    col = jnp.full((n, 1), sink_ref[h * ATT_GROUP], F32)
    for g in range(1, ATT_GROUP):
        col = jnp.where(grp == g, sink_ref[h * ATT_GROUP + g], col)
    return col


def _sink_attention(qh, kd, vd, valid, sink):
    s = _dot_nt(qh, kd) * (HEAD_DIM ** -0.5)
    s = jnp.where(valid, s, NEG_INF)
    m = jnp.maximum(jnp.max(s, axis=-1, keepdims=True), sink)
    p = jnp.exp(s - m)
    denom = jnp.sum(p, axis=-1, keepdims=True) + jnp.exp(sink - m)
    return _dot(p.astype(BF16), vd) / denom


def _merge_query_heads(outs, rows_per_head):
    slabs = []
    for h in range(ATT_KV_HEADS):
        for j in range(ATT_GROUP // 2):
            even = outs[h][(2 * j) * rows_per_head:(2 * j + 1) * rows_per_head]
            odd = outs[h][(2 * j + 1) * rows_per_head:(2 * j + 2) * rows_per_head]
            slabs.append(jnp.where(_low_half(even.shape), even, odd))
    return jnp.concatenate(slabs, axis=1)


def _swa_prompt_kernel(sink_ref, q_ref, kc_ref, kp_ref, vc_ref, vp_ref, o_ref, *, qb):
    i = pl.program_id(1)
    k_all = jnp.concatenate([kp_ref[...], kc_ref[...]], axis=0)
    v_all = jnp.concatenate([vp_ref[...], vc_ref[...]], axis=0)
    kd = _dup_heads(k_all)
    vd = _dup_heads(v_all)

    n = ATT_GROUP * WINDOW
    row = lax.broadcasted_iota(jnp.int32, (n, 2 * WINDOW), 0) & (WINDOW - 1)
    col = lax.broadcasted_iota(jnp.int32, (n, 2 * WINDOW), 1)
    delta = col - row
    in_band = (delta >= 0) & (delta <= WINDOW)
    sinks = [_sink_column(sink_ref, h, WINDOW) for h in range(ATT_KV_HEADS)]

    for j in range(qb // WINDOW):
        valid = in_band & (col >= jnp.where(i == 0, WINDOW, 0)) if j == 0 else in_band
        q = q_ref[j * WINDOW:(j + 1) * WINDOW, :]
        outs = []
        for h in range(ATT_KV_HEADS):
            qh = _stack_query_heads(q, h)
            kband = kd[h][j * WINDOW:(j + 2) * WINDOW]
            vband = vd[h][j * WINDOW:(j + 2) * WINDOW]
            outs.append(_sink_attention(qh, kband, vband, valid, sinks[h]))
        o_ref[j * WINDOW:(j + 1) * WINDOW, :] = _merge_query_heads(outs, WINDOW).astype(o_ref.dtype)


def _swa_prompt(q, k, v, sinks, nbatch, qb):
    t = q.shape[0]
    seq = t // nbatch
    nq = seq // qb
    wpq = qb // WINDOW
    wps = seq // WINDOW

    def cur(width):
        return pl.BlockSpec((qb, width), lambda b, i: (b * nq + i, 0))

    def prev(width):
        return pl.BlockSpec(
            (WINDOW, width), lambda b, i: (b * wps + jnp.maximum(i * wpq - 1, 0), 0))

    return pl.pallas_call(
        functools.partial(_swa_prompt_kernel, qb=qb),
        grid=(nbatch, nq),
        in_specs=[
            pl.BlockSpec(memory_space=pltpu.SMEM),
            cur(ATT_WIDTH), cur(KV_WIDTH), prev(KV_WIDTH), cur(KV_WIDTH), prev(KV_WIDTH),
        ],
        out_specs=cur(ATT_WIDTH),
        out_shape=jax.ShapeDtypeStruct((t, ATT_WIDTH), BF16),
        compiler_params=pltpu.CompilerParams(
            dimension_semantics=("arbitrary", "arbitrary"), vmem_limit_bytes=VMEM_LIMIT),
    )(sinks, q, k, k, v, v)


def _swa_sample_kernel(sink_ref, q_ref, knt_ref, vnt_ref, ck_ref, cv_ref,
                       o_ref, kbuf_ref, vbuf_ref, *, nb, tq):
    past = ck_ref.shape[1]
    hw = ATT_KV_HEADS * HEAD_DIM
    heads = ATT_KV_HEADS * ATT_GROUP
    n = heads * tq
    fresh = past - tq
    t = lax.broadcasted_iota(jnp.int32, (n, 2 * past), 0) & (tq - 1)
    col = lax.broadcasted_iota(jnp.int32, (n, 2 * past), 1)
    valid = ((col < past) & (col >= t)) | ((col >= past + fresh) & (col - (past + fresh) <= t))
    sink = jnp.concatenate([_sink_column(sink_ref, h, tq) for h in range(ATT_KV_HEADS)], axis=0)
    lane = lax.broadcasted_iota(jnp.int32, (hw, past), 1)
    knt = knt_ref[...]
    vnt = vnt_ref[...]
    zero_slab = jnp.zeros((tq, LANES), F32)
    lo = _low_half((tq, LANES))

    def shifted(old, new_t, b):
        return jnp.where(lane >= fresh, pltpu.roll(new_t, (fresh - b * tq) % past, axis=1),
                         pltpu.roll(old, fresh, axis=1))

    def dup_rows(x):
        h0, h1 = x[:HEAD_DIM], x[HEAD_DIM:]
        return jnp.concatenate([h0, h0, h1, h1], axis=0)

    for b in range(nb):
        k_old = ck_ref[b * hw:(b + 1) * hw, :]
        v_old = cv_ref[b * hw:(b + 1) * hw, :]
        k_new = shifted(k_old, knt, b)
        v_new = shifted(v_old, vnt, b)
        kbuf_ref[b * hw:(b + 1) * hw, :] = k_new
        vbuf_ref[b * hw:(b + 1) * hw, :] = v_new
        kop = jnp.concatenate([dup_rows(k_old), dup_rows(k_new)], axis=1).astype(BF16)
        vop = jnp.concatenate([dup_rows(v_old), dup_rows(v_new)], axis=1).astype(BF16)

        q = q_ref[b * tq:(b + 1) * tq, :]
        blocks = []
        for hq in range(heads):
            slab = q[:, (hq // 2) * LANES:(hq // 2 + 1) * LANES]
            slab = jnp.where(lo if hq % 2 == 0 else jnp.logical_not(lo), slab, 0.0)
            pair = [slab, zero_slab] if hq // ATT_GROUP == 0 else [zero_slab, slab]
            blocks.append(jnp.concatenate(pair, axis=1))
        qx = jnp.concatenate(blocks, axis=0).astype(BF16)

        s = _dot(qx, kop) * (HEAD_DIM ** -0.5)
        s = jnp.where(valid, s, NEG_INF)
        m = jnp.maximum(jnp.max(s, axis=-1, keepdims=True), sink)
        p = jnp.exp(s - m)
        denom = jnp.sum(p, axis=-1, keepdims=True) + jnp.exp(sink - m)
        o = _dot_nt(p.astype(BF16), vop) / denom

        slabs = []
        for j in range(heads // 2):
            h = (2 * j) // ATT_GROUP
            even = o[(2 * j) * tq:(2 * j + 1) * tq, h * LANES:(h + 1) * LANES]
            odd = o[(2 * j + 1) * tq:(2 * j + 2) * tq, h * LANES:(h + 1) * LANES]
            slabs.append(jnp.where(lo, even, odd))
        o_ref[b * tq:(b + 1) * tq, :] = jnp.concatenate(slabs, axis=1)


def _swa_sample(q, knt, vnt, cache_kt, cache_vt, sinks, nbatch, nb):
    t = q.shape[0]
    tq = t // nbatch
    past = cache_kt.shape[1]
    hw = ATT_KV_HEADS * HEAD_DIM
    assert past == WINDOW and tq & (tq - 1) == 0 and nb * tq == past

    def rows(r, width):
        return pl.BlockSpec((r, width), lambda i: (i, 0))

    new_t = pl.BlockSpec((hw, nb * tq), lambda i: (0, i))
    return pl.pallas_call(
        functools.partial(_swa_sample_kernel, nb=nb, tq=tq),
        grid=(nbatch // nb,),
        in_specs=[
            pl.BlockSpec(memory_space=pltpu.SMEM),
            rows(nb * tq, ATT_WIDTH), new_t, new_t, rows(nb * hw, past), rows(nb * hw, past),
        ],
        out_specs=[rows(nb * tq, ATT_WIDTH), rows(nb * hw, past), rows(nb * hw, past)],
        out_shape=[
            jax.ShapeDtypeStruct((t, ATT_WIDTH), F32),
            jax.ShapeDtypeStruct(cache_kt.shape, F32),
            jax.ShapeDtypeStruct(cache_vt.shape, F32),
        ],
        compiler_params=pltpu.CompilerParams(
            dimension_semantics=("arbitrary",), vmem_limit_bytes=VMEM_LIMIT),
    )(sinks, q, knt, vnt, cache_kt, cache_vt)


def _mlstm_intra(q, k, v, it_c, bt_c, it_r, bt_r, m_c, valid):
    d = jnp.where(valid, bt_c - bt_r + it_r, NEG_INF)
    inter = bt_c + m_c
    mt = jnp.maximum(jnp.max(d, axis=-1, keepdims=True), inter)
    sm = _dot_nt(q, k) * jnp.exp(d - mt)
    wi = jnp.exp(inter - mt)
    return _dot(sm.astype(BF16), v), jnp.sum(sm, axis=-1, keepdims=True), mt, wi


def _mlstm_head_out(num, den, mt, og, g_head):
    lower = jnp.maximum(jnp.abs(den), jnp.exp(-mt))
    h = num / lower
    hn = h * lax.rsqrt(jnp.mean(h * h, axis=-1, keepdims=True) + EPS)
    return hn * g_head * jax.nn.sigmoid(og)


def _mlstm_prompt_kernel(mq_ref, mk_ref, mv_ref, og_ref, gc_ref, gr_ref, gh_ref,
                         hm_ref, c_ref, n_ref, m_ref, *, nbatch, chunk):
    @pl.when(pl.program_id(0) == 0)
    def _():
        c_ref[...] = jnp.zeros_like(c_ref)
        n_ref[...] = jnp.zeros_like(n_ref)
        m_ref[...] = jnp.zeros_like(m_ref)

    r = lax.broadcasted_iota(jnp.int32, (chunk, chunk), 0)
    s = lax.broadcasted_iota(jnp.int32, (chunk, chunk), 1)
    causal = s <= r
    tril = jnp.where(causal, 1.0, 0.0).astype(BF16)
    triu = jnp.where(r <= s, 1.0, 0.0).astype(BF16)
    lo = _low_half((chunk, LANES))

    probs = []
    for b in range(nbatch):
        gc = gc_ref[b]
        gr = gr_ref[b]
        btc = _cumsum_cols(tril, gc)
        btr = _cumsum_rows(gr, triu)
        for h in range(MLSTM_HEADS):
            slab = slice((h // 2) * LANES, (h // 2 + 1) * LANES)
            keep = lo if h % 2 == 0 else jnp.logical_not(lo)
            q = mq_ref[b, :, slab]
            k = mk_ref[b, :, slab]
            probs.append(dict(
                b=b, h=h, idx=b * MLSTM_HEADS + h, cols=slice(h * MLSTM_DV, (h + 1) * MLSTM_DV),
                q=jnp.where(keep, q, jnp.zeros_like(q)), k=jnp.where(keep, k, jnp.zeros_like(k)),
                it_c=gc[:, h:h + 1], bt_c=btc[:, MLSTM_HEADS + h:MLSTM_HEADS + h + 1],
                it_r=gr[h:h + 1, :], bt_r=btr[MLSTM_HEADS + h:MLSTM_HEADS + h + 1, :]))

    for p in probs:
        p["m_old"] = m_ref[p["idx"]][:, 0:1]
        p["d"] = jnp.where(causal, p["bt_c"] - p["bt_r"] + p["it_r"], NEG_INF)
        p["inter"] = p["bt_c"] + p["m_old"]
        p["qk"] = _dot_nt(p["q"], p["k"])
    for p in probs:
        p["mt"] = jnp.maximum(jnp.max(p["d"], axis=-1, keepdims=True), p["inter"])
    for p in probs:
        p["sm"] = p["qk"] * jnp.exp(p["d"] - p["mt"])
        p["wi"] = jnp.exp(p["inter"] - p["mt"])
    for p in probs:
        v = mv_ref[p["b"], :, p["cols"]]
        c_old = c_ref[p["idx"]]
        n_old = n_ref[p["idx"]]
        p["num"] = _dot(p["sm"].astype(BF16), v) + p["wi"] * _dot_nt(p["q"], c_old.astype(BF16))
        p["den"] = (jnp.sum(p["sm"], axis=-1, keepdims=True)
                    + p["wi"] * jnp.sum(p["q"].astype(F32) * n_old, axis=-1, keepdims=True))
    for p in probs:
        og = og_ref[p["b"], :, p["cols"]].astype(F32)
        hm_ref[p["b"], :, p["cols"]] = _mlstm_head_out(
            p["num"], p["den"], p["mt"], og, gh_ref[:, p["cols"]]).astype(hm_ref.dtype)
    for p in probs:
        idx = p["idx"]
        v = mv_ref[p["b"], :, p["cols"]]
        b_last = p["bt_c"][chunk - 1:chunk]
        m_new = p["mt"][chunk - 1:chunk]
        wk = jnp.exp(b_last - p["bt_c"] + p["it_c"] - m_new)
        wc = jnp.exp(b_last + p["m_old"] - m_new)
        c_ref[idx] = wc * c_ref[idx] + _dot_tn((wk * v.astype(F32)).astype(BF16), p["k"])
        n_ref[idx] = wc * n_ref[idx] + jnp.sum(wk * p["k"].astype(F32), axis=0, keepdims=True)
        m_ref[idx] = jnp.broadcast_to(m_new, (1, LANES))


def _mlstm_prompt(mq, mk, mv, og, gcol, grow, g_head, nbatch, chunk):
    t = mq.shape[0]
    seq = t // nbatch
    nstate = nbatch * MLSTM_HEADS

    def seq3(a):
        return a.reshape(nbatch, seq, a.shape[-1])

    def chunks(width):
        return pl.BlockSpec((nbatch, chunk, width), lambda c: (0, c, 0))

    def state(shape):
        return pl.BlockSpec(shape, lambda c: (0, 0, 0))

    hm, c_st, n_st, m_st = pl.pallas_call(
        functools.partial(_mlstm_prompt_kernel, nbatch=nbatch, chunk=chunk),
        grid=(seq // chunk,),
        in_specs=[
            chunks(MQK_WIDTH), chunks(MQK_WIDTH), chunks(MLSTM_WIDTH), chunks(MLSTM_WIDTH),
            chunks(LANES),
            pl.BlockSpec((nbatch, GATE_ROWS, chunk), lambda c: (0, 0, c)),
            _const_spec((1, MLSTM_WIDTH)),
        ],
        out_specs=[
            chunks(MLSTM_WIDTH),
            state((nstate, MLSTM_DV, LANES)), state((nstate, 1, LANES)), state((nstate, 1, LANES)),
        ],
        out_shape=[
            jax.ShapeDtypeStruct((nbatch, seq, MLSTM_WIDTH), BF16),
            jax.ShapeDtypeStruct((nstate, MLSTM_DV, LANES), F32),
            jax.ShapeDtypeStruct((nstate, 1, LANES), F32),
            jax.ShapeDtypeStruct((nstate, 1, LANES), F32),
        ],
        compiler_params=pltpu.CompilerParams(
            dimension_semantics=("arbitrary",), vmem_limit_bytes=VMEM_LIMIT),
    )(seq3(mq), seq3(mk), seq3(mv), seq3(og), seq3(gcol), grow, g_head)

    c_st = c_st.reshape(nbatch, MLSTM_HEADS, MLSTM_DV, 2, MLSTM_DK)
    n_st = n_st.reshape(nbatch, MLSTM_HEADS, 2, MLSTM_DK)
    c_fin = jnp.stack([c_st[:, h, :, h % 2, :] for h in range(MLSTM_HEADS)], axis=1)
    n_fin = jnp.stack([n_st[:, h, h % 2, :] for h in range(MLSTM_HEADS)], axis=1)
    m_fin = m_st[:, 0, 0].reshape(nbatch, MLSTM_HEADS)
    return hm.reshape(t, MLSTM_WIDTH), c_fin, n_fin, m_fin


def _mlstm_sample_kernel(mq_ref, mk_ref, mkt_ref, mv_ref, og_ref, gc_ref, gr_ref, mrow_ref, c_ref,
                         n_ref, gh_ref, hm_ref, co_ref, no_ref, mt_ref, *, nb, tq):
    rows = nb * tq
    shift = tq.bit_length() - 1
    r = lax.broadcasted_iota(jnp.int32, (rows, rows), 0)
    s = lax.broadcasted_iota(jnp.int32, (rows, rows), 1)
    same = (r >> shift) == (s >> shift)
    valid = same & (s <= r)
    tril = jnp.where(valid, 1.0, 0.0).astype(BF16)
    triu = jnp.where(same & (r <= s), 1.0, 0.0).astype(BF16)
    row_batch = lax.broadcasted_iota(jnp.int32, (rows, 1), 0) >> shift
    lane = lax.broadcasted_iota(jnp.int32, (rows, LANES), 1)

    gc = gc_ref[...]
    gr = gr_ref[0]
    btc = _cumsum_cols(tril, gc)
    btr = _cumsum_rows(gr, triu)
    m_rows = mrow_ref[...]
    mt_all = jnp.zeros((rows, LANES), F32)

    def last_of_batch(col):
        parts = [jnp.broadcast_to(col[(b + 1) * tq - 1:(b + 1) * tq], (tq, 1)) for b in range(nb)]
        return jnp.concatenate(parts, axis=0)

    dk = MLSTM_DK
    lo = _low_half((rows, LANES))
    own_q = row_batch == (lax.broadcasted_iota(jnp.int32, (rows, nb * dk), 1) >> _log2(dk))
    lane_batch = lax.broadcasted_iota(jnp.int32, (dk, rows), 1) >> shift

    def state_rows(b, h):
        return slice((b * MLSTM_HEADS + h) * dk, (b * MLSTM_HEADS + h + 1) * dk)

    for h in range(MLSTM_HEADS):
        slab = slice((h // 2) * LANES, (h // 2 + 1) * LANES)
        qs, ks = mq_ref[:, slab], mk_ref[:, slab]
        qsw, ksw = pltpu.roll(qs, dk, axis=1), pltpu.roll(ks, dk, axis=1)
        q2 = jnp.where(lo, qs, qsw) if h % 2 == 0 else jnp.where(lo, qsw, qs)
        k2 = jnp.where(lo, ks, ksw) if h % 2 == 0 else jnp.where(lo, ksw, ks)
        qb, kb = q2[:, :dk].astype(BF16), k2[:, :dk].astype(BF16)
        vb = mv_ref[:, h * MLSTM_DV:(h + 1) * MLSTM_DV].astype(BF16)
        it_c = gc[:, h:h + 1]
        bt_c = btc[:, MLSTM_HEADS + h:MLSTM_HEADS + h + 1]
        it_r = gr[h:h + 1, :]
        bt_r = btr[MLSTM_HEADS + h:MLSTM_HEADS + h + 1, :]
        m_c = m_rows[:, h:h + 1]

        num, ssum, mt, wi = _mlstm_intra(qb, kb, vb, it_c, bt_c, it_r, bt_r, m_c, valid)
        qx = jnp.where(own_q, jnp.concatenate([q2] * (nb // 2), axis=1), 0.0).astype(BF16)
        c_stack = jnp.concatenate([c_ref[state_rows(b, h), :] for b in range(nb)], axis=0)
        qc = _dot(qx, c_stack.astype(BF16))
        n_rows = jnp.concatenate(
            [jnp.broadcast_to(n_ref[b * MLSTM_HEADS + h], (tq, dk)) for b in range(nb)], axis=0)
        num = num + wi * qc
        den = ssum + wi * jnp.sum(qb.astype(F32) * n_rows, axis=-1, keepdims=True)
        og = og_ref[:, h * MLSTM_DV:(h + 1) * MLSTM_DV]
        g_head = gh_ref[:, h * MLSTM_DV:(h + 1) * MLSTM_DV]
        hm_ref[:, h * MLSTM_DV:(h + 1) * MLSTM_DV] = _mlstm_head_out(num, den, mt, og, g_head)

        b_last = last_of_batch(bt_c)
        m_new = last_of_batch(mt)
        wk = jnp.exp(b_last - bt_c + it_c - m_new)
        wc = jnp.exp(b_last + m_c - m_new)
        wkv = (wk * vb.astype(F32)).astype(BF16)
        wkk = wk * kb.astype(F32)
        kt = mkt_ref[h * dk:(h + 1) * dk, :]
        kx = jnp.concatenate([jnp.where(lane_batch == b, kt, 0.0) for b in range(nb)], axis=0)
        upd = _dot(kx.astype(BF16), wkv)
        for b in range(nb):
            idx = b * MLSTM_HEADS + h
            wc_b = wc[b * tq:b * tq + 1]
            co_ref[state_rows(b, h), :] = wc_b * c_ref[state_rows(b, h), :] + upd[b * dk:(b + 1) * dk]
            no_ref[idx] = wc_b * n_ref[idx] + jnp.sum(wkk[b * tq:(b + 1) * tq], axis=0, keepdims=True)
        mt_all = jnp.where(lane == h, mt, mt_all)
    mt_ref[...] = mt_all


def _mlstm_sample(mq, mk, mkt, mv, og, gcol, grow, m_rows, ct_in, n_in, g_head, nbatch, nb):
    t = mq.shape[0]
    tq = t // nbatch
    rows = nb * tq
    assert tq & (tq - 1) == 0 and rows == LANES and nb % 2 == 0

    def tok(width):
        return pl.BlockSpec((rows, width), lambda i: (i, 0))

    ct_spec = pl.BlockSpec((nb * MLSTM_HEADS * MLSTM_DK, MLSTM_DV), lambda i: (i, 0))
    n_spec = pl.BlockSpec((nb * MLSTM_HEADS, 1, MLSTM_DK), lambda i: (i, 0, 0))
    return pl.pallas_call(
        functools.partial(_mlstm_sample_kernel, nb=nb, tq=tq),
        grid=(nbatch // nb,),
        in_specs=[
            tok(MQK_WIDTH), tok(MQK_WIDTH),
            pl.BlockSpec((MQK_WIDTH, rows), lambda i: (0, i)),
            tok(MLSTM_WIDTH), tok(MLSTM_WIDTH), tok(LANES),
            pl.BlockSpec((1, GATE_ROWS, rows), lambda i: (0, 0, i)),
            tok(LANES),
            ct_spec, n_spec,
            _const_spec((1, MLSTM_WIDTH)),
        ],
        out_specs=[tok(MLSTM_WIDTH), ct_spec, n_spec, tok(LANES)],
        out_shape=[
            jax.ShapeDtypeStruct((t, MLSTM_WIDTH), F32),
            jax.ShapeDtypeStruct(ct_in.shape, F32),
            jax.ShapeDtypeStruct(n_in.shape, F32),
            jax.ShapeDtypeStruct((t, LANES), F32),
        ],
        compiler_params=pltpu.CompilerParams(
            dimension_semantics=("arbitrary",), vmem_limit_bytes=VMEM_LIMIT),
    )(mq, mk, mkt, mv, og, gcol, grow, m_rows, ct_in, n_in, g_head)


def _memkv_kernel(mem_ref, g_ref, wk_ref, wv_ref, k_ref, v_ref):
    mn = _rmsnorm(mem_ref[...], g_ref[...]).astype(BF16)
    k_ref[...] = _dot(mn, wk_ref[...])
    v_ref[...] = _dot(mn, wv_ref[...])


def _memkv(mem, w, nbatch):
    t, d = mem.shape
    m = t // nbatch
    spec = pl.BlockSpec((m, CROSS_WIDTH), lambda b: (b, 0))
    return pl.pallas_call(
        _memkv_kernel,
        grid=(nbatch,),
        in_specs=[
            pl.BlockSpec((m, d), lambda b: (b, 0)),
            _const_spec((1, d)), _const_spec((d, CROSS_WIDTH)), _const_spec((d, CROSS_WIDTH)),
        ],
        out_specs=[spec, spec],
        out_shape=[jax.ShapeDtypeStruct((t, CROSS_WIDTH), F32)] * 2,
        compiler_params=pltpu.CompilerParams(
            dimension_semantics=("arbitrary",), vmem_limit_bytes=VMEM_LIMIT),
    )(mem, w["g_mem"], w["w_ck"], w["w_cv"])


def _mix_stage(x, att, hm, wo_ref, gcr_ref, wcq_ref):
    cat = jnp.concatenate([att.astype(BF16), hm.astype(BF16)], axis=1)
    y1 = x + _dot(cat, wo_ref[...])
    qc = _dot(_rmsnorm(y1, gcr_ref[...]).astype(BF16), wcq_ref[...])
    return y1, qc


def _cross_attention(qc, mk, mv):
    slabs = []
    for j in range(CROSS_WIDTH // LANES):
        qs = qc[:, j * LANES:(j + 1) * LANES]
        ks = mk[:, j * LANES:(j + 1) * LANES]
        vs = mv[:, j * LANES:(j + 1) * LANES]
        lo = _low_half(qs.shape)
        halves = []
        for keep in (lo, jnp.logical_not(lo)):
            s = _dot_nt(jnp.where(keep, qs, jnp.zeros_like(qs)), ks) * (HEAD_DIM ** -0.5)
            p = jnp.exp(s - jnp.max(s, axis=-1, keepdims=True))
            halves.append(_dot(p.astype(BF16), vs) / jnp.sum(p, axis=-1, keepdims=True))
        slabs.append(jnp.where(lo, halves[0], halves[1]))
    return jnp.concatenate(slabs, axis=1)


def _ffn_chunks(hidden):
    step = 512
    return [(c, min(c + step, hidden)) for c in range(0, hidden, step)]


def _ffn_stage(y1, o, wco_ref, gf_ref, wg_ref, wu_ref, wd_ref, act_ref, gfin_ref, final):
    y2 = y1 + _dot(o.astype(BF16), wco_ref[...])
    xn = _rmsnorm(y2, gf_ref[...]).astype(BF16)
    for lo, hi in _ffn_chunks(wg_ref.shape[1]):
        g = _dot(xn, wg_ref[:, lo:hi])
        u = _dot(xn, wu_ref[:, lo:hi])
        act_ref[:, lo:hi] = (g * jax.nn.sigmoid(g) * u).astype(BF16)
    y3 = y2 + _dot(act_ref[...], wd_ref[...])
    return _rmsnorm(y3, gfin_ref[...]) if final else y3


def _post_prompt_kernel(x_ref, att_ref, hm_ref, mk_ref, mv_ref, wo_ref, gcr_ref, wcq_ref, wco_ref,
                        gf_ref, wg_ref, wu_ref, wd_ref, gfin_ref, o_ref, act_ref, *, final):
    y1, qc = _mix_stage(x_ref[...], att_ref[...], hm_ref[...], wo_ref, gcr_ref, wcq_ref)
    o = _cross_attention(qc.astype(BF16), mk_ref[...].astype(BF16), mv_ref[...].astype(BF16))
    o_ref[...] = _ffn_stage(y1, o, wco_ref, gf_ref, wg_ref, wu_ref, wd_ref, act_ref, gfin_ref, final)


_MIX_WEIGHTS = ("w_out", "g_cross", "w_cq")
_FFN_WEIGHTS = ("w_co", "g_ffn", "w_gate", "w_up", "w_down", "g_final")


def _post_prompt(x, att, hm, memk, memv, w, nbatch, final, tm):
    t, d = x.shape
    tpb = t // nbatch // tm
    m = memk.shape[0] // nbatch
    hidden = w["w_gate"].shape[1]

    def rows(width):
        return pl.BlockSpec((tm, width), lambda i: (i, 0))

    mem_spec = pl.BlockSpec((m, CROSS_WIDTH), lambda i: (i // tpb, 0))
    names = _MIX_WEIGHTS + _FFN_WEIGHTS
    return pl.pallas_call(
        functools.partial(_post_prompt_kernel, final=final),
        grid=(t // tm,),
        in_specs=[rows(d), rows(ATT_WIDTH), rows(MLSTM_WIDTH), mem_spec, mem_spec]
        + _weight_specs(w, names),
        out_specs=rows(d),
        out_shape=jax.ShapeDtypeStruct((t, d), F32),
        scratch_shapes=[pltpu.VMEM((tm, hidden), BF16)],
        compiler_params=pltpu.CompilerParams(
            dimension_semantics=("arbitrary",), vmem_limit_bytes=VMEM_LIMIT),
    )(x, att, hm, memk, memv, *[w[n] for n in names])


def _mix_kernel(x_ref, att_ref, hm_ref, wo_ref, gcr_ref, wcq_ref, y_ref, qc_ref):
    y1, qc = _mix_stage(x_ref[...], att_ref[...], hm_ref[...], wo_ref, gcr_ref, wcq_ref)
    y_ref[...] = y1
    qc_ref[...] = qc


def _mix(x, att, hm, w, tm):
    t, d = x.shape

    def rows(width):
        return pl.BlockSpec((tm, width), lambda i: (i, 0))

    return pl.pallas_call(
        _mix_kernel,
        grid=(t // tm,),
        in_specs=[rows(d), rows(ATT_WIDTH), rows(MLSTM_WIDTH)] + _weight_specs(w, _MIX_WEIGHTS),
        out_specs=[rows(d), rows(CROSS_WIDTH)],
        out_shape=[jax.ShapeDtypeStruct((t, d), F32), jax.ShapeDtypeStruct((t, CROSS_WIDTH), F32)],
        compiler_params=pltpu.CompilerParams(
            dimension_semantics=("arbitrary",), vmem_limit_bytes=VMEM_LIMIT),
    )(x, att, hm, *[w[n] for n in _MIX_WEIGHTS])


def _cross_sample_kernel(qc_ref, mk_ref, mv_ref, o_ref, *, nb, tq):
    mem = CROSS_WIDTH
    n = CROSS_HEADS * tq
    row_head = lax.broadcasted_iota(jnp.int32, (n, CROSS_WIDTH), 0) >> _log2(tq)
    lane_head = lax.broadcasted_iota(jnp.int32, (n, CROSS_WIDTH), 1) >> _log2(HEAD_DIM)
    own = row_head == lane_head
    for b in range(nb):
        q = qc_ref[b * tq:(b + 1) * tq, :]
        qx = jnp.where(own, jnp.concatenate([q] * CROSS_HEADS, axis=0), 0.0).astype(BF16)
        mk = mk_ref[b * mem:(b + 1) * mem, :].astype(BF16)
        mv = mv_ref[b * mem:(b + 1) * mem, :].astype(BF16)
        s = _dot(qx, mk) * (HEAD_DIM ** -0.5)
        p = jnp.exp(s - jnp.max(s, axis=-1, keepdims=True))
        pv = _dot_nt(p.astype(BF16), mv) / jnp.sum(p, axis=-1, keepdims=True)
        pv = jnp.where(own, pv, 0.0)
        out = pv[0:tq]
        for h in range(1, CROSS_HEADS):
            out = out + pv[h * tq:(h + 1) * tq]
        o_ref[b * tq:(b + 1) * tq, :] = out


def _cross_sample(qc, memk_t, memv_t, nbatch, nb):
    t = qc.shape[0]
    tq = t // nbatch
    mem = memk_t.shape[1]
    tok = pl.BlockSpec((nb * tq, CROSS_WIDTH), lambda i: (i, 0))
    mspec = pl.BlockSpec((nb * CROSS_WIDTH, mem), lambda i: (i, 0))
    return pl.pallas_call(
        functools.partial(_cross_sample_kernel, nb=nb, tq=tq),
        grid=(nbatch // nb,),
        in_specs=[tok, mspec, mspec],
        out_specs=tok,
        out_shape=jax.ShapeDtypeStruct((t, CROSS_WIDTH), F32),
        compiler_params=pltpu.CompilerParams(
            dimension_semantics=("arbitrary",), vmem_limit_bytes=VMEM_LIMIT),
    )(qc, memk_t, memv_t)


def _ffn_kernel(y_ref, o_ref, wco_ref, gf_ref, wg_ref, wu_ref, wd_ref, gfin_ref, out_ref, act_ref,
                *, final):
    out_ref[...] = _ffn_stage(y_ref[...], o_ref[...], wco_ref, gf_ref, wg_ref, wu_ref, wd_ref,
                              act_ref, gfin_ref, final)


def _ffn(y1, o, w, final, tm):
    t, d = y1.shape
    hidden = w["w_gate"].shape[1]

    def rows(width):
        return pl.BlockSpec((tm, width), lambda i: (i, 0))

    return pl.pallas_call(
        functools.partial(_ffn_kernel, final=final),
        grid=(t // tm,),
        in_specs=[rows(d), rows(CROSS_WIDTH)] + _weight_specs(w, _FFN_WEIGHTS),
        out_specs=rows(d),
        out_shape=jax.ShapeDtypeStruct((t, d), F32),
        scratch_shapes=[pltpu.VMEM((tm, hidden), BF16)],
        compiler_params=pltpu.CompilerParams(
            dimension_semantics=("arbitrary",), vmem_limit_bytes=VMEM_LIMIT),
    )(y1, o, *[w[n] for n in _FFN_WEIGHTS])


def _layer_weights(l, w_in, b_igate, b_fgate, g_mlstm_head, w_out, g_mix, g_cross, g_mem, w_cq,
                   w_ck, w_cv, w_co, g_ffn, w_gate, w_up, w_down, g_final):
    d = w_in.shape[1]
    gate_w = w_in[l][:, MAIN_WIDTH:]
    gate_b = jnp.concatenate([b_igate[l], b_fgate[l]]).astype(F32)
    ngate = gate_b.shape[0]
    row = lambda a: a.astype(F32).reshape(1, -1)
    w_t = w_in[l].T
    k0 = ATT_WIDTH
    mk0 = ATT_WIDTH + 2 * KV_WIDTH + MQK_WIDTH
    return {
        "w_kvk_t": jnp.concatenate([w_t[k0:k0 + 2 * KV_WIDTH], w_t[mk0:mk0 + MQK_WIDTH]]).astype(BF16),
        "w_main": w_in[l][:, :MAIN_WIDTH].astype(BF16),
        "w_gate_cols": jnp.pad(gate_w, ((0, 0), (0, LANES - ngate))).astype(BF16),
        "w_gate_rows": jnp.pad(gate_w.T, ((0, GATE_ROWS - ngate), (0, 0))).astype(BF16),
        "b_gate_cols": jnp.pad(gate_b, (0, LANES - ngate)).reshape(1, LANES),
        "b_gate_rows": jnp.pad(gate_b, (0, GATE_ROWS - ngate)).reshape(GATE_ROWS, 1),
        "g_mix": row(g_mix[l]), "g_cross": row(g_cross[l]), "g_mem": row(g_mem[l]),
        "g_ffn": row(g_ffn[l]), "g_final": row(g_final), "g_head": row(g_mlstm_head[l]),
        "w_out": w_out[l].astype(BF16), "w_cq": w_cq[l].astype(BF16), "w_ck": w_ck[l].astype(BF16),
        "w_cv": w_cv[l].astype(BF16), "w_co": w_co[l].astype(BF16), "w_gate": w_gate[l].astype(BF16),
        "w_up": w_up[l].astype(BF16), "w_down": w_down[l].astype(BF16),
    }


TOKEN_TILE = 512
SWA_QUERY_BLOCK = 512
MLSTM_CHUNK_ROWS = 256
SWA_SAMPLE_BATCH = 16
MLSTM_SAMPLE_BATCH = 16
CROSS_SAMPLE_BATCH = 8


def kernel(x_prompt, x_sample, mem_prompt, cache_swa_k, cache_swa_v, state_mlstm_C, state_mlstm_n,
           state_mlstm_m, cache_mem_k, cache_mem_v, w_in, b_igate, b_fgate, attn_sinks,
           g_mlstm_head, w_out, g_mix, g_cross, g_mem, w_cq, w_ck, w_cv, w_co, g_ffn, w_gate,
           w_up, w_down, g_final):
    depth = w_in.shape[0]
    bp, sp, d = x_prompt.shape
    bs, ss, _ = x_sample.shape
    mem_tokens = mem_prompt.shape[1]
    past = cache_swa_k.shape[2]
    yp = x_prompt.reshape(bp * sp, d)
    ys = x_sample.reshape(bs * ss, d)
    mem = mem_prompt.reshape(bp * mem_tokens, d)
    outs = [[] for _ in range(12)]

    for l in range(depth):
        final = l == depth - 1
        w = _layer_weights(l, w_in, b_igate, b_fgate, g_mlstm_head, w_out, g_mix, g_cross, g_mem,
                           w_cq, w_ck, w_cv, w_co, g_ffn, w_gate, w_up, w_down, g_final)
        sinks = attn_sinks[l].astype(F32)

        q, k, v, mq, mk, mv, og, gcol, grow = _proj(yp, w, bp, BF16, TOKEN_TILE)
        att = _swa_prompt(q, k, v, sinks, bp, SWA_QUERY_BLOCK)
        hm, c_p, n_p, m_p = _mlstm_prompt(mq, mk, mv, og, gcol, grow, w["g_head"], bp, MLSTM_CHUNK_ROWS)
        memk, memv = _memkv(mem, w, bp)
        yp = _post_prompt(yp, att, hm, memk, memv, w, bp, final, TOKEN_TILE)
        win_shape = (bp, WINDOW, ATT_KV_HEADS, HEAD_DIM)
        outs[0].append(k.reshape(bp, sp, KV_WIDTH)[:, sp - WINDOW:].reshape(win_shape))
        outs[1].append(v.reshape(bp, sp, KV_WIDTH)[:, sp - WINDOW:].reshape(win_shape))
        outs[2].append(c_p)
        outs[3].append(n_p)
        outs[4].append(m_p)
        mem_shape = (bp, mem_tokens, CROSS_HEADS, HEAD_DIM)
        outs[5].append(memk.reshape(mem_shape))
        outs[6].append(memv.reshape(mem_shape))

        def tokens_last(a):
            return jnp.transpose(a.astype(F32), (0, 2, 3, 1)).reshape(-1, a.shape[1])

        q, _, _, mq, mk, mv, og, gcol, grow, knt, vnt, mkt = _proj(
            ys, w, 1, F32, TOKEN_TILE, transposed=True)
        att, kbuf_t, vbuf_t = _swa_sample(
            q, knt, vnt, tokens_last(cache_swa_k[l]), tokens_last(cache_swa_v[l]), sinks, bs,
            SWA_SAMPLE_BATCH)
        m_rows = jnp.pad(jnp.repeat(state_mlstm_m[l].astype(F32), ss, axis=0),
                         ((0, 0), (0, LANES - MLSTM_HEADS)))
        ct_in = jnp.swapaxes(state_mlstm_C[l].astype(F32), 2, 3).reshape(-1, MLSTM_DV)
        hm, ct_s, n_s, mt = _mlstm_sample(
            mq, mk, mkt, mv, og, gcol, grow, m_rows, ct_in,
            state_mlstm_n[l].astype(F32).reshape(bs * MLSTM_HEADS, 1, MLSTM_DK),
            w["g_head"], bs, MLSTM_SAMPLE_BATCH)
        y1, qc = _mix(ys, att, hm, w, TOKEN_TILE)
        o = _cross_sample(qc, tokens_last(cache_mem_k[l]), tokens_last(cache_mem_v[l]), bs,
                          CROSS_SAMPLE_BATCH)
        ys = _ffn(y1, o, w, final, TOKEN_TILE)
        buf_t_shape = (bs, ATT_KV_HEADS, HEAD_DIM, past)
        outs[7].append(jnp.transpose(kbuf_t.reshape(buf_t_shape), (0, 3, 1, 2)))
        outs[8].append(jnp.transpose(vbuf_t.reshape(buf_t_shape), (0, 3, 1, 2)))
        outs[9].append(jnp.swapaxes(ct_s.reshape(bs, MLSTM_HEADS, MLSTM_DK, MLSTM_DV), 2, 3))
        outs[10].append(n_s.reshape(bs, MLSTM_HEADS, MLSTM_DK))
        outs[11].append(mt.reshape(bs, ss, LANES)[:, ss - 1, :MLSTM_HEADS])

    return (yp.reshape(bp, sp, d), ys.reshape(bs, ss, d)) + tuple(jnp.stack(o) for o in outs)
```

```python
import functools

import jax
import jax.numpy as jnp
from jax import lax
from jax.experimental import pallas as pl
from jax.experimental.pallas import tpu as pltpu

F32 = jnp.float32
BF16 = jnp.bfloat16

HEAD_DIM = 64
ATT_GROUP = 4
ATT_KV_HEADS = 2
ATT_WIDTH = 512
KV_WIDTH = ATT_KV_HEADS * HEAD_DIM
WINDOW = 128
LOG_WINDOW = 7
MLSTM_HEADS = 4
MLSTM_DV = 128
MLSTM_DK = 64
MQK_WIDTH = MLSTM_HEADS * MLSTM_DK
MLSTM_WIDTH = MLSTM_HEADS * MLSTM_DV
MAIN_WIDTH = ATT_WIDTH + 2 * KV_WIDTH + 2 * MQK_WIDTH + 2 * MLSTM_WIDTH
CROSS_HEADS = 4
CROSS_WIDTH = CROSS_HEADS * HEAD_DIM
EPS = 1e-6
NEG_INF = float("-inf")

LANES = 128
GATE_ROWS = 16
VMEM_LIMIT = 56 * 1024 * 1024

NT = (((1,), (1,)), ((), ()))
TN = (((0,), (0,)), ((), ()))


def _dot(a, b):
    return jnp.dot(a, b, preferred_element_type=F32)


def _dot_nt(a, b):
    return lax.dot_general(a, b, NT, preferred_element_type=F32)


def _dot_tn(a, b):
    return lax.dot_general(a, b, TN, preferred_element_type=F32)


def _rmsnorm(x, g):
    return x * lax.rsqrt(jnp.mean(x * x, axis=-1, keepdims=True) + EPS) * g


def _log_sigmoid(x):
    return jnp.minimum(x, 0.0) - jnp.log1p(jnp.exp(-jnp.abs(x)))


def _split3(x):
    hi = x.astype(BF16)
    r1 = x - hi.astype(F32)
    mid = r1.astype(BF16)
    lo = (r1 - mid.astype(F32)).astype(BF16)
    return hi, mid, lo


def _cumsum_cols(tri, x):
    hi, mid, lo = _split3(x)
    return _dot(tri, hi) + _dot(tri, mid) + _dot(tri, lo)


def _cumsum_rows(x, tri):
    hi, mid, lo = _split3(x)
    return _dot(hi, tri) + _dot(mid, tri) + _dot(lo, tri)


def _log2(n):
    assert n > 0 and n & (n - 1) == 0, n
    return n.bit_length() - 1


def _low_half(shape):
    return lax.broadcasted_iota(jnp.int32, shape, 1) < HEAD_DIM


def _proj_kernel(x_ref, g_ref, w_ref, wgc_ref, wgr_ref, bc_ref, br_ref, *rest, transposed):
    if transposed:
        wt_ref, rest = rest[0], rest[1:]
    q_ref, k_ref, v_ref, mq_ref, mk_ref, mv_ref, og_ref, gc_ref, gr_ref = rest[:9]
    xn = _rmsnorm(x_ref[...], g_ref[...]).astype(BF16)
    if transposed:
        kt_ref, vt_ref, mkt_ref = rest[9:]
        zt = _dot_nt(wt_ref[...], xn)
        kt_ref[...] = zt[:KV_WIDTH]
        vt_ref[...] = zt[KV_WIDTH:2 * KV_WIDTH]
        mkt_ref[...] = zt[2 * KV_WIDTH:] * (MLSTM_DK ** -0.5)

    def mm(lo, hi):
        return _dot(xn, w_ref[:, lo:hi])

    o = 0
    q_ref[...] = mm(o, o + ATT_WIDTH).astype(q_ref.dtype)
    o += ATT_WIDTH
    kv = mm(o, o + 2 * KV_WIDTH)
    k_ref[...] = kv[:, :KV_WIDTH]
    v_ref[...] = kv[:, KV_WIDTH:]
    o += 2 * KV_WIDTH
    mqk = mm(o, o + 2 * MQK_WIDTH)
    mq_ref[...] = mqk[:, :MQK_WIDTH].astype(mq_ref.dtype)
    mk_ref[...] = (mqk[:, MQK_WIDTH:] * (MLSTM_DK ** -0.5)).astype(mk_ref.dtype)
    o += 2 * MQK_WIDTH
    mv_ref[...] = mm(o, o + MLSTM_WIDTH).astype(mv_ref.dtype)
    o += MLSTM_WIDTH
    og_ref[...] = mm(o, o + MLSTM_WIDTH).astype(og_ref.dtype)

    zc = _dot(xn, wgc_ref[...]) + bc_ref[...]
    lane = lax.broadcasted_iota(jnp.int32, zc.shape, 1)
    gc_ref[...] = jnp.where(lane < MLSTM_HEADS, zc, _log_sigmoid(zc))
    zr = _dot_nt(wgr_ref[...], xn) + br_ref[...]
    row = lax.broadcasted_iota(jnp.int32, zr.shape, 0)
    gr_ref[0] = jnp.where(row < MLSTM_HEADS, zr, _log_sigmoid(zr))


def _const_spec(shape):
    nd = len(shape)
    return pl.BlockSpec(shape, lambda *_: (0,) * nd, pipeline_mode=pl.Buffered(1))


def _weight_specs(w, names):
    return [_const_spec(w[n].shape) for n in names]


def _proj(x, w, nseg, act_dtype, tm, transposed=False):
    t, d = x.shape
    seg = t // nseg
    tpb = seg // tm
    grid = (t // tm,)

    def rows(width):
        return pl.BlockSpec((tm, width), lambda i: (i, 0))

    out_shape = [
        jax.ShapeDtypeStruct((t, ATT_WIDTH), act_dtype),
        jax.ShapeDtypeStruct((t, KV_WIDTH), F32),
        jax.ShapeDtypeStruct((t, KV_WIDTH), F32),
        jax.ShapeDtypeStruct((t, MQK_WIDTH), act_dtype),
        jax.ShapeDtypeStruct((t, MQK_WIDTH), act_dtype),
        jax.ShapeDtypeStruct((t, MLSTM_WIDTH), act_dtype),
        jax.ShapeDtypeStruct((t, MLSTM_WIDTH), act_dtype),
        jax.ShapeDtypeStruct((t, LANES), F32),
        jax.ShapeDtypeStruct((nseg, GATE_ROWS, seg), F32),
    ]
    out_specs = [
        rows(ATT_WIDTH), rows(KV_WIDTH), rows(KV_WIDTH), rows(MQK_WIDTH), rows(MQK_WIDTH),
        rows(MLSTM_WIDTH), rows(MLSTM_WIDTH), rows(LANES),
        pl.BlockSpec((1, GATE_ROWS, tm), lambda i: (i // tpb, 0, i % tpb)),
    ]
    names = ["g_mix", "w_main", "w_gate_cols", "w_gate_rows", "b_gate_cols", "b_gate_rows"]
    if transposed:
        names.append("w_kvk_t")
        for width in (KV_WIDTH, KV_WIDTH, MQK_WIDTH):
            out_shape.append(jax.ShapeDtypeStruct((width, t), F32))
            out_specs.append(pl.BlockSpec((width, tm), lambda i: (0, i)))
    in_specs = [rows(d)] + _weight_specs(w, names)
    return pl.pallas_call(
        functools.partial(_proj_kernel, transposed=transposed),
        grid=grid,
        in_specs=in_specs,
        out_specs=out_specs,
        out_shape=out_shape,
        compiler_params=pltpu.CompilerParams(
            dimension_semantics=("arbitrary",), vmem_limit_bytes=VMEM_LIMIT),
    )(x, *[w[n] for n in names])


def _dup_heads(x):
    swapped = pltpu.roll(x, HEAD_DIM, axis=1)
    lo = _low_half(x.shape)
    return (jnp.where(lo, x, swapped).astype(BF16), jnp.where(lo, swapped, x).astype(BF16))


def _stack_query_heads(q, h):
    parts = []
    for g in range(ATT_GROUP):
        hq = h * ATT_GROUP + g
        slab = q[:, (hq // 2) * LANES:(hq // 2 + 1) * LANES]
        lo = _low_half(slab.shape)
        keep = lo if hq % 2 == 0 else jnp.logical_not(lo)
        parts.append(jnp.where(keep, slab, jnp.zeros_like(slab)))
    return jnp.concatenate(parts, axis=0).astype(BF16)


def _sink_column(sink_ref, h, rows_per_head):
    n = ATT_GROUP * rows_per_head
    grp = lax.broadcasted_iota(jnp.int32, (n, 1), 0) >> _log2(rows_per_head)
    col = jnp.full((n, 1), sink_ref[h * ATT_GROUP], F32)
    for g in range(1, ATT_GROUP):
        col = jnp.where(grp == g, sink_ref[h * ATT_GROUP + g], col)
    return col


def _sink_attention(qh, kd, vd, valid, sink):
    s = _dot_nt(qh, kd) * (HEAD_DIM ** -0.5)
    s = jnp.where(valid, s, NEG_INF)
    m = jnp.maximum(jnp.max(s, axis=-1, keepdims=True), sink)
    p = jnp.exp(s - m)
    denom = jnp.sum(p, axis=-1, keepdims=True) + jnp.exp(sink - m)
    return _dot(p.astype(BF16), vd) / denom


def _merge_query_heads(outs, rows_per_head):
    slabs = []
    for h in range(ATT_KV_HEADS):
        for j in range(ATT_GROUP // 2):
            even = outs[h][(2 * j) * rows_per_head:(2 * j + 1) * rows_per_head]
            odd = outs[h][(2 * j + 1) * rows_per_head:(2 * j + 2) * rows_per_head]
            slabs.append(jnp.where(_low_half(even.shape), even, odd))
    return jnp.concatenate(slabs, axis=1)


def _swa_prompt_kernel(sink_ref, q_ref, kc_ref, kp_ref, vc_ref, vp_ref, o_ref, *, qb):
    i = pl.program_id(1)
    k_all = jnp.concatenate([kp_ref[...], kc_ref[...]], axis=0)
    v_all = jnp.concatenate([vp_ref[...], vc_ref[...]], axis=0)
    kd = _dup_heads(k_all)
    vd = _dup_heads(v_all)

    n = ATT_GROUP * WINDOW
    row = lax.broadcasted_iota(jnp.int32, (n, 2 * WINDOW), 0) & (WINDOW - 1)
    col = lax.broadcasted_iota(jnp.int32, (n, 2 * WINDOW), 1)
    delta = col - row
    in_band = (delta >= 0) & (delta <= WINDOW)
    sinks = [_sink_column(sink_ref, h, WINDOW) for h in range(ATT_KV_HEADS)]

    for j in range(qb // WINDOW):
        valid = in_band & (col >= jnp.where(i == 0, WINDOW, 0)) if j == 0 else in_band
        q = q_ref[j * WINDOW:(j + 1) * WINDOW, :]
        outs = []
        for h in range(ATT_KV_HEADS):
            qh = _stack_query_heads(q, h)
            kband = kd[h][j * WINDOW:(j + 2) * WINDOW]
            vband = vd[h][j * WINDOW:(j + 2) * WINDOW]
            outs.append(_sink_attention(qh, kband, vband, valid, sinks[h]))
        o_ref[j * WINDOW:(j + 1) * WINDOW, :] = _merge_query_heads(outs, WINDOW).astype(o_ref.dtype)


def _swa_prompt(q, k, v, sinks, nbatch, qb):
    t = q.shape[0]
    seq = t // nbatch
    nq = seq // qb
    wpq = qb // WINDOW
    wps = seq // WINDOW

    def cur(width):
        return pl.BlockSpec((qb, width), lambda b, i: (b * nq + i, 0))

    def prev(width):
        return pl.BlockSpec(
            (WINDOW, width), lambda b, i: (b * wps + jnp.maximum(i * wpq - 1, 0), 0))

    return pl.pallas_call(
        functools.partial(_swa_prompt_kernel, qb=qb),
        grid=(nbatch, nq),
        in_specs=[
            pl.BlockSpec(memory_space=pltpu.SMEM),
            cur(ATT_WIDTH), cur(KV_WIDTH), prev(KV_WIDTH), cur(KV_WIDTH), prev(KV_WIDTH),
        ],
        out_specs=cur(ATT_WIDTH),
        out_shape=jax.ShapeDtypeStruct((t, ATT_WIDTH), BF16),
        compiler_params=pltpu.CompilerParams(
            dimension_semantics=("arbitrary", "arbitrary"), vmem_limit_bytes=VMEM_LIMIT),
    )(sinks, q, k, k, v, v)


def _swa_sample_kernel(sink_ref, q_ref, knt_ref, vnt_ref, ck_ref, cv_ref,
                       o_ref, kbuf_ref, vbuf_ref, *, nb, tq):
    past = ck_ref.shape[1]
    hw = ATT_KV_HEADS * HEAD_DIM
    heads = ATT_KV_HEADS * ATT_GROUP
    n = heads * tq
    fresh = past - tq
    t = lax.broadcasted_iota(jnp.int32, (n, 2 * past), 0) & (tq - 1)
    col = lax.broadcasted_iota(jnp.int32, (n, 2 * past), 1)
    valid = ((col < past) & (col >= t)) | ((col >= past + fresh) & (col - (past + fresh) <= t))
    sink = jnp.concatenate([_sink_column(sink_ref, h, tq) for h in range(ATT_KV_HEADS)], axis=0)
    lane = lax.broadcasted_iota(jnp.int32, (hw, past), 1)
    knt = knt_ref[...]
    vnt = vnt_ref[...]
    zero_slab = jnp.zeros((tq, LANES), F32)
    lo = _low_half((tq, LANES))

    def shifted(old, new_t, b):
        return jnp.where(lane >= fresh, pltpu.roll(new_t, (fresh - b * tq) % past, axis=1),
                         pltpu.roll(old, fresh, axis=1))

    def dup_rows(x):
        h0, h1 = x[:HEAD_DIM], x[HEAD_DIM:]
        return jnp.concatenate([h0, h0, h1, h1], axis=0)

    for b in range(nb):
        k_old = ck_ref[b * hw:(b + 1) * hw, :]
        v_old = cv_ref[b * hw:(b + 1) * hw, :]
        k_new = shifted(k_old, knt, b)
        v_new = shifted(v_old, vnt, b)
        kbuf_ref[b * hw:(b + 1) * hw, :] = k_new
        vbuf_ref[b * hw:(b + 1) * hw, :] = v_new
        kop = jnp.concatenate([dup_rows(k_old), dup_rows(k_new)], axis=1).astype(BF16)
        vop = jnp.concatenate([dup_rows(v_old), dup_rows(v_new)], axis=1).astype(BF16)

        q = q_ref[b * tq:(b + 1) * tq, :]
        blocks = []
        for hq in range(heads):
            slab = q[:, (hq // 2) * LANES:(hq // 2 + 1) * LANES]
            slab = jnp.where(lo if hq % 2 == 0 else jnp.logical_not(lo), slab, 0.0)
            pair = [slab, zero_slab] if hq // ATT_GROUP == 0 else [zero_slab, slab]
            blocks.append(jnp.concatenate(pair, axis=1))
        qx = jnp.concatenate(blocks, axis=0).astype(BF16)

        s = _dot(qx, kop) * (HEAD_DIM ** -0.5)
        s = jnp.where(valid, s, NEG_INF)
        m = jnp.maximum(jnp.max(s, axis=-1, keepdims=True), sink)
        p = jnp.exp(s - m)
        denom = jnp.sum(p, axis=-1, keepdims=True) + jnp.exp(sink - m)
        o = _dot_nt(p.astype(BF16), vop) / denom

        slabs = []
        for j in range(heads // 2):
            h = (2 * j) // ATT_GROUP
            even = o[(2 * j) * tq:(2 * j + 1) * tq, h * LANES:(h + 1) * LANES]
            odd = o[(2 * j + 1) * tq:(2 * j + 2) * tq, h * LANES:(h + 1) * LANES]
            slabs.append(jnp.where(lo, even, odd))
        o_ref[b * tq:(b + 1) * tq, :] = jnp.concatenate(slabs, axis=1)


def _swa_sample(q, knt, vnt, cache_kt, cache_vt, sinks, nbatch, nb):
    t = q.shape[0]
    tq = t // nbatch
    past = cache_kt.shape[1]
    hw = ATT_KV_HEADS * HEAD_DIM
    assert past == WINDOW and tq & (tq - 1) == 0 and nb * tq == past

    def rows(r, width):
        return pl.BlockSpec((r, width), lambda i: (i, 0))

    new_t = pl.BlockSpec((hw, nb * tq), lambda i: (0, i))
    return pl.pallas_call(
        functools.partial(_swa_sample_kernel, nb=nb, tq=tq),
        grid=(nbatch // nb,),
        in_specs=[
            pl.BlockSpec(memory_space=pltpu.SMEM),
            rows(nb * tq, ATT_WIDTH), new_t, new_t, rows(nb * hw, past), rows(nb * hw, past),
        ],
        out_specs=[rows(nb * tq, ATT_WIDTH), rows(nb * hw, past), rows(nb * hw, past)],
        out_shape=[
            jax.ShapeDtypeStruct((t, ATT_WIDTH), F32),
            jax.ShapeDtypeStruct(cache_kt.shape, F32),
            jax.ShapeDtypeStruct(cache_vt.shape, F32),
        ],
        compiler_params=pltpu.CompilerParams(
            dimension_semantics=("arbitrary",), vmem_limit_bytes=VMEM_LIMIT),
    )(sinks, q, knt, vnt, cache_kt, cache_vt)


def _mlstm_intra(q, k, v, it_c, bt_c, it_r, bt_r, m_c, valid):
    d = jnp.where(valid, bt_c - bt_r + it_r, NEG_INF)
    inter = bt_c + m_c
    mt = jnp.maximum(jnp.max(d, axis=-1, keepdims=True), inter)
    sm = _dot_nt(q, k) * jnp.exp(d - mt)
    wi = jnp.exp(inter - mt)
    return _dot(sm.astype(BF16), v), jnp.sum(sm, axis=-1, keepdims=True), mt, wi


def _mlstm_head_out(num, den, mt, og, g_head):
    lower = jnp.maximum(jnp.abs(den), jnp.exp(-mt))
    h = num / lower
    hn = h * lax.rsqrt(jnp.mean(h * h, axis=-1, keepdims=True) + EPS)
    return hn * g_head * jax.nn.sigmoid(og)


def _mlstm_prompt_kernel(mq_ref, mk_ref, mv_ref, og_ref, gc_ref, gr_ref, gh_ref,
                         hm_ref, c_ref, n_ref, m_ref, *, nbatch, chunk):
    @pl.when(pl.program_id(0) == 0)
    def _():
        c_ref[...] = jnp.zeros_like(c_ref)
        n_ref[...] = jnp.zeros_like(n_ref)
        m_ref[...] = jnp.zeros_like(m_ref)

    s_idx = lax.broadcasted_iota(jnp.int32, (chunk, chunk), 0)
    l_idx = lax.broadcasted_iota(jnp.int32, (chunk, chunk), 1)
    causal = s_idx <= l_idx
    triu = jnp.where(causal, 1.0, 0.0).astype(BF16)
    lo = _low_half((chunk, LANES))
    pad_rows = jnp.zeros((LANES - MLSTM_HEADS, chunk), F32)
    sub8 = (GATE_ROWS, chunk)

    probs = []
    for b in range(nbatch):
        gr = gr_ref[b]
        btr = _cumsum_rows(gr, triu)
        it_rows = gr[:MLSTM_HEADS]
        bt_rows = btr[MLSTM_HEADS:2 * MLSTM_HEADS]
        u_cols = jnp.concatenate([it_rows - bt_rows, pad_rows], axis=0).T
        for h in range(MLSTM_HEADS):
            slab = slice((h // 2) * LANES, (h // 2 + 1) * LANES)
            keep = lo if h % 2 == 0 else jnp.logical_not(lo)
            q = mq_ref[b, :, slab]
            k = mk_ref[b, :, slab]
            probs.append(dict(
                b=b, h=h, idx=b * MLSTM_HEADS + h, cols=slice(h * MLSTM_DV, (h + 1) * MLSTM_DV),
                q=jnp.where(keep, q, jnp.zeros_like(q)), k=jnp.where(keep, k, jnp.zeros_like(k)),
                u_c=u_cols[:, h:h + 1], it_r=it_rows[h:h + 1, :], bt_r=bt_rows[h:h + 1, :]))

    for p in probs:
        p["m_old"] = m_ref[p["idx"]][:, 0:1]
        p["d"] = jnp.where(causal, p["bt_r"] + p["u_c"], NEG_INF)
        p["inter"] = p["bt_r"] + p["m_old"]
        p["kq"] = _dot_nt(p["k"], p["q"])
        p["q_t"] = p["q"].astype(F32).T.astype(BF16)
        p["v_t"] = mv_ref[p["b"], :, p["cols"]].astype(F32).T
    for p in probs:
        p["mt"] = jnp.maximum(jnp.max(p["d"], axis=0, keepdims=True), p["inter"])
    for p in probs:
        p["sm"] = p["kq"] * jnp.exp(p["d"] - p["mt"])
        p["wi"] = jnp.exp(p["inter"] - p["mt"])
    for p in probs:
        c_old = c_ref[p["idx"]]
        n_old = jnp.broadcast_to(n_ref[p["idx"]], (sub8[0], LANES))
        num = (_dot(p["v_t"].astype(BF16), p["sm"].astype(BF16))
               + p["wi"] * _dot(c_old.astype(BF16), p["q_t"]))
        den = (jnp.sum(p["sm"], axis=0, keepdims=True)
               + p["wi"] * _dot(n_old.astype(BF16), p["q_t"])[0:1])
        lower = jnp.maximum(jnp.abs(den), jnp.exp(-p["mt"]))
        h_t = num / lower
        p["hn"] = (h_t * lax.rsqrt(jnp.mean(h_t * h_t, axis=0, keepdims=True) + EPS)).T
    for p in probs:
        og = og_ref[p["b"], :, p["cols"]].astype(F32)
        hm_ref[p["b"], :, p["cols"]] = (
            p["hn"] * gh_ref[:, p["cols"]] * jax.nn.sigmoid(og)).astype(hm_ref.dtype)
    for p in probs:
        idx = p["idx"]
        b_last = p["bt_r"][:, chunk - 1:chunk]
        m_new = p["mt"][:, chunk - 1:chunk]
        wk = jnp.exp(b_last - p["bt_r"] + p["it_r"] - m_new)
        wc = jnp.exp(b_last + p["m_old"] - m_new)
        c_ref[idx] = wc * c_ref[idx] + _dot((p["v_t"] * wk).astype(BF16), p["k"])
        n_ref[idx] = wc * n_ref[idx] + _dot(jnp.broadcast_to(wk, sub8).astype(BF16), p["k"])[0:1]
        m_ref[idx] = jnp.broadcast_to(m_new, (1, LANES))


def _mlstm_prompt(mq, mk, mv, og, gcol, grow, g_head, nbatch, chunk):
    t = mq.shape[0]
    seq = t // nbatch
    nstate = nbatch * MLSTM_HEADS

    def seq3(a):
        return a.reshape(nbatch, seq, a.shape[-1])

    def chunks(width):
        return pl.BlockSpec((nbatch, chunk, width), lambda c: (0, c, 0))

    def state(shape):
        return pl.BlockSpec(shape, lambda c: (0, 0, 0))

    hm, c_st, n_st, m_st = pl.pallas_call(
        functools.partial(_mlstm_prompt_kernel, nbatch=nbatch, chunk=chunk),
        grid=(seq // chunk,),
        in_specs=[
            chunks(MQK_WIDTH), chunks(MQK_WIDTH), chunks(MLSTM_WIDTH), chunks(MLSTM_WIDTH),
            chunks(LANES),
            pl.BlockSpec((nbatch, GATE_ROWS, chunk), lambda c: (0, 0, c)),
            _const_spec((1, MLSTM_WIDTH)),
        ],
        out_specs=[
            chunks(MLSTM_WIDTH),
            state((nstate, MLSTM_DV, LANES)), state((nstate, 1, LANES)), state((nstate, 1, LANES)),
        ],
        out_shape=[
            jax.ShapeDtypeStruct((nbatch, seq, MLSTM_WIDTH), BF16),
            jax.ShapeDtypeStruct((nstate, MLSTM_DV, LANES), F32),
            jax.ShapeDtypeStruct((nstate, 1, LANES), F32),
            jax.ShapeDtypeStruct((nstate, 1, LANES), F32),
        ],
        compiler_params=pltpu.CompilerParams(
            dimension_semantics=("arbitrary",), vmem_limit_bytes=VMEM_LIMIT),
    )(seq3(mq), seq3(mk), seq3(mv), seq3(og), seq3(gcol), grow, g_head)

    c_st = c_st.reshape(nbatch, MLSTM_HEADS, MLSTM_DV, 2, MLSTM_DK)
    n_st = n_st.reshape(nbatch, MLSTM_HEADS, 2, MLSTM_DK)
    c_fin = jnp.stack([c_st[:, h, :, h % 2, :] for h in range(MLSTM_HEADS)], axis=1)
    n_fin = jnp.stack([n_st[:, h, h % 2, :] for h in range(MLSTM_HEADS)], axis=1)
    m_fin = m_st[:, 0, 0].reshape(nbatch, MLSTM_HEADS)
    return hm.reshape(t, MLSTM_WIDTH), c_fin, n_fin, m_fin


def _mlstm_sample_kernel(mq_ref, mk_ref, mkt_ref, mv_ref, og_ref, gc_ref, gr_ref, mrow_ref, c_ref,
                         n_ref, gh_ref, hm_ref, co_ref, no_ref, mt_ref, *, nb, tq):
    rows = nb * tq
    shift = tq.bit_length() - 1
    r = lax.broadcasted_iota(jnp.int32, (rows, rows), 0)
    s = lax.broadcasted_iota(jnp.int32, (rows, rows), 1)
    same = (r >> shift) == (s >> shift)
    valid = same & (s <= r)
    tril = jnp.where(valid, 1.0, 0.0).astype(BF16)
    triu = jnp.where(same & (r <= s), 1.0, 0.0).astype(BF16)
    row_batch = lax.broadcasted_iota(jnp.int32, (rows, 1), 0) >> shift
    lane = lax.broadcasted_iota(jnp.int32, (rows, LANES), 1)

    gc = gc_ref[...]
    gr = gr_ref[0]
    btc = _cumsum_cols(tril, gc)
    btr = _cumsum_rows(gr, triu)
    m_rows = mrow_ref[...]
    mt_all = jnp.zeros((rows, LANES), F32)

    def last_of_batch(col):
        parts = [jnp.broadcast_to(col[(b + 1) * tq - 1:(b + 1) * tq], (tq, 1)) for b in range(nb)]
        return jnp.concatenate(parts, axis=0)

    dk = MLSTM_DK
    lo = _low_half((rows, LANES))
    own_q = row_batch == (lax.broadcasted_iota(jnp.int32, (rows, nb * dk), 1) >> _log2(dk))
    lane_batch = lax.broadcasted_iota(jnp.int32, (dk, rows), 1) >> shift

    def state_rows(b, h):
        return slice((b * MLSTM_HEADS + h) * dk, (b * MLSTM_HEADS + h + 1) * dk)

    for h in range(MLSTM_HEADS):
        slab = slice((h // 2) * LANES, (h // 2 + 1) * LANES)
        qs, ks = mq_ref[:, slab], mk_ref[:, slab]
        qsw, ksw = pltpu.roll(qs, dk, axis=1), pltpu.roll(ks, dk, axis=1)
        q2 = jnp.where(lo, qs, qsw) if h % 2 == 0 else jnp.where(lo, qsw, qs)
        k2 = jnp.where(lo, ks, ksw) if h % 2 == 0 else jnp.where(lo, ksw, ks)
        qb, kb = q2[:, :dk].astype(BF16), k2[:, :dk].astype(BF16)
        vb = mv_ref[:, h * MLSTM_DV:(h + 1) * MLSTM_DV].astype(BF16)
        it_c = gc[:, h:h + 1]
        bt_c = btc[:, MLSTM_HEADS + h:MLSTM_HEADS + h + 1]
        it_r = gr[h:h + 1, :]
        bt_r = btr[MLSTM_HEADS + h:MLSTM_HEADS + h + 1, :]
        m_c = m_rows[:, h:h + 1]

        num, ssum, mt, wi = _mlstm_intra(qb, kb, vb, it_c, bt_c, it_r, bt_r, m_c, valid)
        qx = jnp.where(own_q, jnp.concatenate([q2] * (nb // 2), axis=1), 0.0).astype(BF16)
        c_stack = jnp.concatenate([c_ref[state_rows(b, h), :] for b in range(nb)], axis=0)
        qc = _dot(qx, c_stack.astype(BF16))
        n_rows = jnp.concatenate(
            [jnp.broadcast_to(n_ref[b * MLSTM_HEADS + h], (tq, dk)) for b in range(nb)], axis=0)
        num = num + wi * qc
        den = ssum + wi * jnp.sum(qb.astype(F32) * n_rows, axis=-1, keepdims=True)
        og = og_ref[:, h * MLSTM_DV:(h + 1) * MLSTM_DV]
        g_head = gh_ref[:, h * MLSTM_DV:(h + 1) * MLSTM_DV]
        hm_ref[:, h * MLSTM_DV:(h + 1) * MLSTM_DV] = _mlstm_head_out(num, den, mt, og, g_head)

        b_last = last_of_batch(bt_c)
        m_new = last_of_batch(mt)
        wk = jnp.exp(b_last - bt_c + it_c - m_new)
        wc = jnp.exp(b_last + m_c - m_new)
        wkv = (wk * vb.astype(F32)).astype(BF16)
        wkk = wk * kb.astype(F32)
        kt = mkt_ref[h * dk:(h + 1) * dk, :]
        kx = jnp.concatenate([jnp.where(lane_batch == b, kt, 0.0) for b in range(nb)], axis=0)
        upd = _dot(kx.astype(BF16), wkv)
        for b in range(nb):
            idx = b * MLSTM_HEADS + h
            wc_b = wc[b * tq:b * tq + 1]
            co_ref[state_rows(b, h), :] = wc_b * c_ref[state_rows(b, h), :] + upd[b * dk:(b + 1) * dk]
            no_ref[idx] = wc_b * n_ref[idx] + jnp.sum(wkk[b * tq:(b + 1) * tq], axis=0, keepdims=True)
        mt_all = jnp.where(lane == h, mt, mt_all)
    mt_ref[...] = mt_all


def _mlstm_sample(mq, mk, mkt, mv, og, gcol, grow, m_rows, ct_in, n_in, g_head, nbatch, nb):
    t = mq.shape[0]
    tq = t // nbatch
    rows = nb * tq
    assert tq & (tq - 1) == 0 and rows == LANES and nb % 2 == 0

    def tok(width):
        return pl.BlockSpec((rows, width), lambda i: (i, 0))

    ct_spec = pl.BlockSpec((nb * MLSTM_HEADS * MLSTM_DK, MLSTM_DV), lambda i: (i, 0))
    n_spec = pl.BlockSpec((nb * MLSTM_HEADS, 1, MLSTM_DK), lambda i: (i, 0, 0))
    return pl.pallas_call(
        functools.partial(_mlstm_sample_kernel, nb=nb, tq=tq),
        grid=(nbatch // nb,),
        in_specs=[
            tok(MQK_WIDTH), tok(MQK_WIDTH),
            pl.BlockSpec((MQK_WIDTH, rows), lambda i: (0, i)),
            tok(MLSTM_WIDTH), tok(MLSTM_WIDTH), tok(LANES),
            pl.BlockSpec((1, GATE_ROWS, rows), lambda i: (0, 0, i)),
            tok(LANES),
            ct_spec, n_spec,
            _const_spec((1, MLSTM_WIDTH)),
        ],
        out_specs=[tok(MLSTM_WIDTH), ct_spec, n_spec, tok(LANES)],
        out_shape=[
            jax.ShapeDtypeStruct((t, MLSTM_WIDTH), F32),
            jax.ShapeDtypeStruct(ct_in.shape, F32),
            jax.ShapeDtypeStruct(n_in.shape, F32),
            jax.ShapeDtypeStruct((t, LANES), F32),
        ],
        compiler_params=pltpu.CompilerParams(
            dimension_semantics=("arbitrary",), vmem_limit_bytes=VMEM_LIMIT),
    )(mq, mk, mkt, mv, og, gcol, grow, m_rows, ct_in, n_in, g_head)


def _memkv_kernel(mem_ref, g_ref, wk_ref, wv_ref, k_ref, v_ref):
    mn = _rmsnorm(mem_ref[...], g_ref[...]).astype(BF16)
    k_ref[...] = _dot(mn, wk_ref[...])
    v_ref[...] = _dot(mn, wv_ref[...])


def _memkv(mem, w, nbatch):
    t, d = mem.shape
    m = t // nbatch
    spec = pl.BlockSpec((m, CROSS_WIDTH), lambda b: (b, 0))
    return pl.pallas_call(
        _memkv_kernel,
        grid=(nbatch,),
        in_specs=[
            pl.BlockSpec((m, d), lambda b: (b, 0)),
            _const_spec((1, d)), _const_spec((d, CROSS_WIDTH)), _const_spec((d, CROSS_WIDTH)),
        ],
        out_specs=[spec, spec],
        out_shape=[jax.ShapeDtypeStruct((t, CROSS_WIDTH), F32)] * 2,
        compiler_params=pltpu.CompilerParams(
            dimension_semantics=("arbitrary",), vmem_limit_bytes=VMEM_LIMIT),
    )(mem, w["g_mem"], w["w_ck"], w["w_cv"])


def _mix_stage(x, att, hm, wo_ref, gcr_ref, wcq_ref):
    cat = jnp.concatenate([att.astype(BF16), hm.astype(BF16)], axis=1)
    y1 = x + _dot(cat, wo_ref[...])
    qc = _dot(_rmsnorm(y1, gcr_ref[...]).astype(BF16), wcq_ref[...])
    return y1, qc


def _cross_attention(qc, mk, mv):
    slabs = []
    for j in range(CROSS_WIDTH // LANES):
        qs = qc[:, j * LANES:(j + 1) * LANES]
        ks = mk[:, j * LANES:(j + 1) * LANES]
        vs = mv[:, j * LANES:(j + 1) * LANES]
        lo = _low_half(qs.shape)
        halves = []
        for keep in (lo, jnp.logical_not(lo)):
            s = _dot_nt(jnp.where(keep, qs, jnp.zeros_like(qs)), ks) * (HEAD_DIM ** -0.5)
            p = jnp.exp(s - jnp.max(s, axis=-1, keepdims=True))
            halves.append(_dot(p.astype(BF16), vs) / jnp.sum(p, axis=-1, keepdims=True))
        slabs.append(jnp.where(lo, halves[0], halves[1]))
    return jnp.concatenate(slabs, axis=1)


def _ffn_chunks(hidden):
    step = 512
    return [(c, min(c + step, hidden)) for c in range(0, hidden, step)]


def _ffn_stage(y1, o, wco_ref, gf_ref, wg_ref, wu_ref, wd_ref, act_ref, gfin_ref, final):
    y2 = y1 + _dot(o.astype(BF16), wco_ref[...])
    xn = _rmsnorm(y2, gf_ref[...]).astype(BF16)
    for lo, hi in _ffn_chunks(wg_ref.shape[1]):
        g = _dot(xn, wg_ref[:, lo:hi])
        u = _dot(xn, wu_ref[:, lo:hi])
        act_ref[:, lo:hi] = (g * jax.nn.sigmoid(g) * u).astype(BF16)
    y3 = y2 + _dot(act_ref[...], wd_ref[...])
    return _rmsnorm(y3, gfin_ref[...]) if final else y3


def _post_prompt_kernel(x_ref, att_ref, hm_ref, mk_ref, mv_ref, wo_ref, gcr_ref, wcq_ref, wco_ref,
                        gf_ref, wg_ref, wu_ref, wd_ref, gfin_ref, o_ref, act_ref, *, final):
    y1, qc = _mix_stage(x_ref[...], att_ref[...], hm_ref[...], wo_ref, gcr_ref, wcq_ref)
    o = _cross_attention(qc.astype(BF16), mk_ref[...].astype(BF16), mv_ref[...].astype(BF16))
    o_ref[...] = _ffn_stage(y1, o, wco_ref, gf_ref, wg_ref, wu_ref, wd_ref, act_ref, gfin_ref, final)


_MIX_WEIGHTS = ("w_out", "g_cross", "w_cq")
_FFN_WEIGHTS = ("w_co", "g_ffn", "w_gate", "w_up", "w_down", "g_final")


def _post_prompt(x, att, hm, memk, memv, w, nbatch, final, tm):
    t, d = x.shape
    tpb = t // nbatch // tm
    m = memk.shape[0] // nbatch
    hidden = w["w_gate"].shape[1]

    def rows(width):
        return pl.BlockSpec((tm, width), lambda i: (i, 0))

    mem_spec = pl.BlockSpec((m, CROSS_WIDTH), lambda i: (i // tpb, 0))
    names = _MIX_WEIGHTS + _FFN_WEIGHTS
    return pl.pallas_call(
        functools.partial(_post_prompt_kernel, final=final),
        grid=(t // tm,),
        in_specs=[rows(d), rows(ATT_WIDTH), rows(MLSTM_WIDTH), mem_spec, mem_spec]
        + _weight_specs(w, names),
        out_specs=rows(d),
        out_shape=jax.ShapeDtypeStruct((t, d), F32),
        scratch_shapes=[pltpu.VMEM((tm, hidden), BF16)],
        compiler_params=pltpu.CompilerParams(
            dimension_semantics=("arbitrary",), vmem_limit_bytes=VMEM_LIMIT),
    )(x, att, hm, memk, memv, *[w[n] for n in names])


def _mix_kernel(x_ref, att_ref, hm_ref, wo_ref, gcr_ref, wcq_ref, y_ref, qc_ref):
    y1, qc = _mix_stage(x_ref[...], att_ref[...], hm_ref[...], wo_ref, gcr_ref, wcq_ref)
    y_ref[...] = y1
    qc_ref[...] = qc


def _mix(x, att, hm, w, tm):
    t, d = x.shape

    def rows(width):
        return pl.BlockSpec((tm, width), lambda i: (i, 0))

    return pl.pallas_call(
        _mix_kernel,
        grid=(t // tm,),
        in_specs=[rows(d), rows(ATT_WIDTH), rows(MLSTM_WIDTH)] + _weight_specs(w, _MIX_WEIGHTS),
        out_specs=[rows(d), rows(CROSS_WIDTH)],
        out_shape=[jax.ShapeDtypeStruct((t, d), F32), jax.ShapeDtypeStruct((t, CROSS_WIDTH), F32)],
        compiler_params=pltpu.CompilerParams(
            dimension_semantics=("arbitrary",), vmem_limit_bytes=VMEM_LIMIT),
    )(x, att, hm, *[w[n] for n in _MIX_WEIGHTS])


def _cross_sample_kernel(qc_ref, mk_ref, mv_ref, o_ref, *, nb, tq):
    mem = CROSS_WIDTH
    n = CROSS_HEADS * tq
    row_head = lax.broadcasted_iota(jnp.int32, (n, CROSS_WIDTH), 0) >> _log2(tq)
    lane_head = lax.broadcasted_iota(jnp.int32, (n, CROSS_WIDTH), 1) >> _log2(HEAD_DIM)
    own = row_head == lane_head
    for b in range(nb):
        q = qc_ref[b * tq:(b + 1) * tq, :]
        qx = jnp.where(own, jnp.concatenate([q] * CROSS_HEADS, axis=0), 0.0).astype(BF16)
        mk = mk_ref[b * mem:(b + 1) * mem, :].astype(BF16)
        mv = mv_ref[b * mem:(b + 1) * mem, :].astype(BF16)
        s = _dot(qx, mk) * (HEAD_DIM ** -0.5)
        p = jnp.exp(s - jnp.max(s, axis=-1, keepdims=True))
        pv = _dot_nt(p.astype(BF16), mv) / jnp.sum(p, axis=-1, keepdims=True)
        pv = jnp.where(own, pv, 0.0)
        out = pv[0:tq]
        for h in range(1, CROSS_HEADS):
            out = out + pv[h * tq:(h + 1) * tq]
        o_ref[b * tq:(b + 1) * tq, :] = out


def _cross_sample(qc, memk_t, memv_t, nbatch, nb):
    t = qc.shape[0]
    tq = t // nbatch
    mem = memk_t.shape[1]
    tok = pl.BlockSpec((nb * tq, CROSS_WIDTH), lambda i: (i, 0))
    mspec = pl.BlockSpec((nb * CROSS_WIDTH, mem), lambda i: (i, 0))
    return pl.pallas_call(
        functools.partial(_cross_sample_kernel, nb=nb, tq=tq),
        grid=(nbatch // nb,),
        in_specs=[tok, mspec, mspec],
        out_specs=tok,
        out_shape=jax.ShapeDtypeStruct((t, CROSS_WIDTH), F32),
        compiler_params=pltpu.CompilerParams(
            dimension_semantics=("arbitrary",), vmem_limit_bytes=VMEM_LIMIT),
    )(qc, memk_t, memv_t)


def _ffn_kernel(y_ref, o_ref, wco_ref, gf_ref, wg_ref, wu_ref, wd_ref, gfin_ref, out_ref, act_ref,
                *, final):
    out_ref[...] = _ffn_stage(y_ref[...], o_ref[...], wco_ref, gf_ref, wg_ref, wu_ref, wd_ref,
                              act_ref, gfin_ref, final)


def _ffn(y1, o, w, final, tm):
    t, d = y1.shape
    hidden = w["w_gate"].shape[1]

    def rows(width):
        return pl.BlockSpec((tm, width), lambda i: (i, 0))

    return pl.pallas_call(
        functools.partial(_ffn_kernel, final=final),
        grid=(t // tm,),
        in_specs=[rows(d), rows(CROSS_WIDTH)] + _weight_specs(w, _FFN_WEIGHTS),
        out_specs=rows(d),
        out_shape=jax.ShapeDtypeStruct((t, d), F32),
        scratch_shapes=[pltpu.VMEM((tm, hidden), BF16)],
        compiler_params=pltpu.CompilerParams(
            dimension_semantics=("arbitrary",), vmem_limit_bytes=VMEM_LIMIT),
    )(y1, o, *[w[n] for n in _FFN_WEIGHTS])


def _layer_weights(l, w_in, b_igate, b_fgate, g_mlstm_head, w_out, g_mix, g_cross, g_mem, w_cq,
                   w_ck, w_cv, w_co, g_ffn, w_gate, w_up, w_down, g_final):
    d = w_in.shape[1]
    gate_w = w_in[l][:, MAIN_WIDTH:]
    gate_b = jnp.concatenate([b_igate[l], b_fgate[l]]).astype(F32)
    ngate = gate_b.shape[0]
    row = lambda a: a.astype(F32).reshape(1, -1)
    w_t = w_in[l].T
    k0 = ATT_WIDTH
    mk0 = ATT_WIDTH + 2 * KV_WIDTH + MQK_WIDTH
    return {
        "w_kvk_t": jnp.concatenate([w_t[k0:k0 + 2 * KV_WIDTH], w_t[mk0:mk0 + MQK_WIDTH]]).astype(BF16),
        "w_main": w_in[l][:, :MAIN_WIDTH].astype(BF16),
        "w_gate_cols": jnp.pad(gate_w, ((0, 0), (0, LANES - ngate))).astype(BF16),
        "w_gate_rows": jnp.pad(gate_w.T, ((0, GATE_ROWS - ngate), (0, 0))).astype(BF16),
        "b_gate_cols": jnp.pad(gate_b, (0, LANES - ngate)).reshape(1, LANES),
        "b_gate_rows": jnp.pad(gate_b, (0, GATE_ROWS - ngate)).reshape(GATE_ROWS, 1),
        "g_mix": row(g_mix[l]), "g_cross": row(g_cross[l]), "g_mem": row(g_mem[l]),
        "g_ffn": row(g_ffn[l]), "g_final": row(g_final), "g_head": row(g_mlstm_head[l]),
        "w_out": w_out[l].astype(BF16), "w_cq": w_cq[l].astype(BF16), "w_ck": w_ck[l].astype(BF16),
        "w_cv": w_cv[l].astype(BF16), "w_co": w_co[l].astype(BF16), "w_gate": w_gate[l].astype(BF16),
        "w_up": w_up[l].astype(BF16), "w_down": w_down[l].astype(BF16),
    }


TOKEN_TILE = 512
SWA_QUERY_BLOCK = 512
MLSTM_CHUNK_ROWS = 256
SWA_SAMPLE_BATCH = 16
MLSTM_SAMPLE_BATCH = 16
CROSS_SAMPLE_BATCH = 8


def kernel(x_prompt, x_sample, mem_prompt, cache_swa_k, cache_swa_v, state_mlstm_C, state_mlstm_n,
           state_mlstm_m, cache_mem_k, cache_mem_v, w_in, b_igate, b_fgate, attn_sinks,
           g_mlstm_head, w_out, g_mix, g_cross, g_mem, w_cq, w_ck, w_cv, w_co, g_ffn, w_gate,
           w_up, w_down, g_final):
    depth = w_in.shape[0]
    bp, sp, d = x_prompt.shape
    bs, ss, _ = x_sample.shape
    mem_tokens = mem_prompt.shape[1]
    past = cache_swa_k.shape[2]
    yp = x_prompt.reshape(bp * sp, d)
    ys = x_sample.reshape(bs * ss, d)
    mem = mem_prompt.reshape(bp * mem_tokens, d)
    outs = [[] for _ in range(12)]

    for l in range(depth):
        final = l == depth - 1
        w = _layer_weights(l, w_in, b_igate, b_fgate, g_mlstm_head, w_out, g_mix, g_cross, g_mem,
                           w_cq, w_ck, w_cv, w_co, g_ffn, w_gate, w_up, w_down, g_final)
        sinks = attn_sinks[l].astype(F32)

        q, k, v, mq, mk, mv, og, gcol, grow = _proj(yp, w, bp, BF16, TOKEN_TILE)
        att = _swa_prompt(q, k, v, sinks, bp, SWA_QUERY_BLOCK)
        hm, c_p, n_p, m_p = _mlstm_prompt(mq, mk, mv, og, gcol, grow, w["g_head"], bp, MLSTM_CHUNK_ROWS)
        memk, memv = _memkv(mem, w, bp)
        yp = _post_prompt(yp, att, hm, memk, memv, w, bp, final, TOKEN_TILE)
        win_shape = (bp, WINDOW, ATT_KV_HEADS, HEAD_DIM)
        outs[0].append(k.reshape(bp, sp, KV_WIDTH)[:, sp - WINDOW:].reshape(win_shape))
        outs[1].append(v.reshape(bp, sp, KV_WIDTH)[:, sp - WINDOW:].reshape(win_shape))
        outs[2].append(c_p)
        outs[3].append(n_p)
        outs[4].append(m_p)
        mem_shape = (bp, mem_tokens, CROSS_HEADS, HEAD_DIM)
        outs[5].append(memk.reshape(mem_shape))
        outs[6].append(memv.reshape(mem_shape))

        def tokens_last(a):
            return jnp.transpose(a.astype(F32), (0, 2, 3, 1)).reshape(-1, a.shape[1])

        q, _, _, mq, mk, mv, og, gcol, grow, knt, vnt, mkt = _proj(
            ys, w, 1, F32, TOKEN_TILE, transposed=True)
        att, kbuf_t, vbuf_t = _swa_sample(
            q, knt, vnt, tokens_last(cache_swa_k[l]), tokens_last(cache_swa_v[l]), sinks, bs,
            SWA_SAMPLE_BATCH)
        m_rows = jnp.pad(jnp.repeat(state_mlstm_m[l].astype(F32), ss, axis=0),
                         ((0, 0), (0, LANES - MLSTM_HEADS)))
        ct_in = jnp.swapaxes(state_mlstm_C[l].astype(F32), 2, 3).reshape(-1, MLSTM_DV)
        hm, ct_s, n_s, mt = _mlstm_sample(
            mq, mk, mkt, mv, og, gcol, grow, m_rows, ct_in,
            state_mlstm_n[l].astype(F32).reshape(bs * MLSTM_HEADS, 1, MLSTM_DK),
            w["g_head"], bs, MLSTM_SAMPLE_BATCH)
        y1, qc = _mix(ys, att, hm, w, TOKEN_TILE)
        o = _cross_sample(qc, tokens_last(cache_mem_k[l]), tokens_last(cache_mem_v[l]), bs,
                          CROSS_SAMPLE_BATCH)
        ys = _ffn(y1, o, w, final, TOKEN_TILE)
        buf_t_shape = (bs, ATT_KV_HEADS, HEAD_DIM, past)
        outs[7].append(jnp.transpose(kbuf_t.reshape(buf_t_shape), (0, 3, 1, 2)))
        outs[8].append(jnp.transpose(vbuf_t.reshape(buf_t_shape), (0, 3, 1, 2)))
        outs[9].append(jnp.swapaxes(ct_s.reshape(bs, MLSTM_HEADS, MLSTM_DK, MLSTM_DV), 2, 3))
        outs[10].append(n_s.reshape(bs, MLSTM_HEADS, MLSTM_DK))
        outs[11].append(mt.reshape(bs, ss, LANES)[:, ss - 1, :MLSTM_HEADS])

    return (yp.reshape(bp, sp, d), ys.reshape(bs, ss, d)) + tuple(jnp.stack(o) for o in outs)
```

```python
import functools

import jax
import jax.numpy as jnp
from jax import lax
from jax.experimental import pallas as pl
from jax.experimental.pallas import tpu as pltpu

F32 = jnp.float32
BF16 = jnp.bfloat16

HEAD_DIM = 64
ATT_GROUP = 4
ATT_KV_HEADS = 2
ATT_WIDTH = 512
KV_WIDTH = ATT_KV_HEADS * HEAD_DIM
WINDOW = 128
LOG_WINDOW = 7
MLSTM_HEADS = 4
MLSTM_DV = 128
MLSTM_DK = 64
MQK_WIDTH = MLSTM_HEADS * MLSTM_DK
MLSTM_WIDTH = MLSTM_HEADS * MLSTM_DV
MAIN_WIDTH = ATT_WIDTH + 2 * KV_WIDTH + 2 * MQK_WIDTH + 2 * MLSTM_WIDTH
CROSS_HEADS = 4
CROSS_WIDTH = CROSS_HEADS * HEAD_DIM
EPS = 1e-6
NEG_INF = float("-inf")
assert HEAD_DIM == 4 ** 3

LANES = 128
GATE_ROWS = 16
VMEM_LIMIT = 56 * 1024 * 1024

NT = (((1,), (1,)), ((), ()))
TN = (((0,), (0,)), ((), ()))


def _dot(a, b):
    return jnp.dot(a, b, preferred_element_type=F32)


def _dot_nt(a, b):
    return lax.dot_general(a, b, NT, preferred_element_type=F32)


def _dot_tn(a, b):
    return lax.dot_general(a, b, TN, preferred_element_type=F32)


def _rmsnorm(x, g):
    return x * lax.rsqrt(jnp.mean(x * x, axis=-1, keepdims=True) + EPS) * g


def _log_sigmoid(x):
    return jnp.minimum(x, 0.0) - jnp.log1p(jnp.exp(-jnp.abs(x)))


def _split3(x):
    hi = x.astype(BF16)
    r1 = x - hi.astype(F32)
    mid = r1.astype(BF16)
    lo = (r1 - mid.astype(F32)).astype(BF16)
    return hi, mid, lo


def _cumsum_cols(tri, x):
    hi, mid, lo = _split3(x)
    return _dot(tri, hi) + _dot(tri, mid) + _dot(tri, lo)


def _cumsum_rows(x, tri):
    hi, mid, lo = _split3(x)
    return _dot(hi, tri) + _dot(mid, tri) + _dot(lo, tri)


def _log2(n):
    assert n > 0 and n & (n - 1) == 0, n
    return n.bit_length() - 1


def _low_half(shape):
    return lax.broadcasted_iota(jnp.int32, shape, 1) < HEAD_DIM


def _proj_kernel(x_ref, g_ref, w_ref, wgc_ref, wgr_ref, bc_ref, br_ref, *rest, transposed):
    if transposed:
        wt_ref, rest = rest[0], rest[1:]
    q_ref, k_ref, v_ref, mq_ref, mk_ref, mv_ref, og_ref, gc_ref, gr_ref = rest[:9]
    xn = _rmsnorm(x_ref[...], g_ref[...]).astype(BF16)
    if transposed:
        kt_ref, vt_ref, mkt_ref = rest[9:]
        zt = _dot_nt(wt_ref[...], xn)
        kt_ref[...] = zt[:KV_WIDTH]
        vt_ref[...] = zt[KV_WIDTH:2 * KV_WIDTH]
        mkt_ref[...] = zt[2 * KV_WIDTH:] * (MLSTM_DK ** -0.5)

    def mm(lo, hi):
        return _dot(xn, w_ref[:, lo:hi])

    o = 0
    q_ref[...] = mm(o, o + ATT_WIDTH).astype(q_ref.dtype)
    o += ATT_WIDTH
    kv = mm(o, o + 2 * KV_WIDTH)
    k_ref[...] = kv[:, :KV_WIDTH]
    v_ref[...] = kv[:, KV_WIDTH:]
    o += 2 * KV_WIDTH
    mqk = mm(o, o + 2 * MQK_WIDTH)
    mq_ref[...] = mqk[:, :MQK_WIDTH].astype(mq_ref.dtype)
    mk_ref[...] = (mqk[:, MQK_WIDTH:] * (MLSTM_DK ** -0.5)).astype(mk_ref.dtype)
    o += 2 * MQK_WIDTH
    mv_ref[...] = mm(o, o + MLSTM_WIDTH).astype(mv_ref.dtype)
    o += MLSTM_WIDTH
    og_ref[...] = mm(o, o + MLSTM_WIDTH).astype(og_ref.dtype)

    zc = _dot(xn, wgc_ref[...]) + bc_ref[...]
    lane = lax.broadcasted_iota(jnp.int32, zc.shape, 1)
    gc_ref[...] = jnp.where(lane < MLSTM_HEADS, zc, _log_sigmoid(zc))
    zr = _dot_nt(wgr_ref[...], xn) + br_ref[...]
    row = lax.broadcasted_iota(jnp.int32, zr.shape, 0)
    gr_ref[0] = jnp.where(row < MLSTM_HEADS, zr, _log_sigmoid(zr))


def _const_spec(shape):
    nd = len(shape)
    return pl.BlockSpec(shape, lambda *_: (0,) * nd, pipeline_mode=pl.Buffered(1))


def _weight_specs(w, names):
    return [_const_spec(w[n].shape) for n in names]


def _proj(x, w, nseg, act_dtype, tm, transposed=False):
    t, d = x.shape
    seg = t // nseg
    tpb = seg // tm
    grid = (t // tm,)

    def rows(width):
        return pl.BlockSpec((tm, width), lambda i: (i, 0))

    out_shape = [
        jax.ShapeDtypeStruct((t, ATT_WIDTH), act_dtype),
        jax.ShapeDtypeStruct((t, KV_WIDTH), F32),
        jax.ShapeDtypeStruct((t, KV_WIDTH), F32),
        jax.ShapeDtypeStruct((t, MQK_WIDTH), act_dtype),
        jax.ShapeDtypeStruct((t, MQK_WIDTH), act_dtype),
        jax.ShapeDtypeStruct((t, MLSTM_WIDTH), act_dtype),
        jax.ShapeDtypeStruct((t, MLSTM_WIDTH), act_dtype),
        jax.ShapeDtypeStruct((t, LANES), F32),
        jax.ShapeDtypeStruct((nseg, GATE_ROWS, seg), F32),
    ]
    out_specs = [
        rows(ATT_WIDTH), rows(KV_WIDTH), rows(KV_WIDTH), rows(MQK_WIDTH), rows(MQK_WIDTH),
        rows(MLSTM_WIDTH), rows(MLSTM_WIDTH), rows(LANES),
        pl.BlockSpec((1, GATE_ROWS, tm), lambda i: (i // tpb, 0, i % tpb)),
    ]
    names = ["g_mix", "w_main", "w_gate_cols", "w_gate_rows", "b_gate_cols", "b_gate_rows"]
    if transposed:
        names.append("w_kvk_t")
        for width in (KV_WIDTH, KV_WIDTH, MQK_WIDTH):
            out_shape.append(jax.ShapeDtypeStruct((width, t), F32))
            out_specs.append(pl.BlockSpec((width, tm), lambda i: (0, i)))
    in_specs = [rows(d)] + _weight_specs(w, names)
    return pl.pallas_call(
        functools.partial(_proj_kernel, transposed=transposed),
        grid=grid,
        in_specs=in_specs,
        out_specs=out_specs,
        out_shape=out_shape,
        compiler_params=pltpu.CompilerParams(
            dimension_semantics=("arbitrary",), vmem_limit_bytes=VMEM_LIMIT),
    )(x, *[w[n] for n in names])


def _dup_heads(x):
    swapped = pltpu.roll(x, HEAD_DIM, axis=1)
    lo = _low_half(x.shape)
    return (jnp.where(lo, x, swapped).astype(BF16), jnp.where(lo, swapped, x).astype(BF16))


def _stack_query_heads(q, h):
    parts = []
    for g in range(ATT_GROUP):
        hq = h * ATT_GROUP + g
        slab = q[:, (hq // 2) * LANES:(hq // 2 + 1) * LANES]
        lo = _low_half(slab.shape)
        keep = lo if hq % 2 == 0 else jnp.logical_not(lo)
        parts.append(jnp.where(keep, slab, jnp.zeros_like(slab)))
    return jnp.concatenate(parts, axis=0).astype(BF16)


def _sink_column(sink_ref, h, rows_per_head):
    n = ATT_GROUP * rows_per_head
    grp = lax.broadcasted_iota(jnp.int32, (n, 1), 0) >> _log2(rows_per_head)
    col = jnp.full((n, 1), sink_ref[h * ATT_GROUP], F32)
    for g in range(1, ATT_GROUP):
        col = jnp.where(grp == g, sink_ref[h * ATT_GROUP + g], col)
    return col


def _swa_prompt_kernel(sink_ref, q_ref, kc_ref, kp_ref, vc_ref, vp_ref, o_ref, *, qb):
    i = pl.program_id(1)
    k_all = jnp.concatenate([kp_ref[...], kc_ref[...]], axis=0)
    v_all = jnp.concatenate([vp_ref[...], vc_ref[...]], axis=0)
    kd = _dup_heads(k_all)
    v_t = v_all.T.astype(BF16)

    n = ATT_GROUP * WINDOW
    key = lax.broadcasted_iota(jnp.int32, (2 * WINDOW, n), 0)
    qry = lax.broadcasted_iota(jnp.int32, (2 * WINDOW, n), 1) & (WINDOW - 1)
    delta = key - qry
    in_band = (delta >= 0) & (delta <= WINDOW)
    grp = lax.broadcasted_iota(jnp.int32, (1, n), 1) >> LOG_WINDOW
    sinks = []
    for h in range(ATT_KV_HEADS):
        row = jnp.full((1, n), sink_ref[h * ATT_GROUP], F32)
        for g in range(1, ATT_GROUP):
            row = jnp.where(grp == g, sink_ref[h * ATT_GROUP + g], row)
        sinks.append(row)

    probs = []
    for j in range(qb // WINDOW):
        valid = in_band & (key >= jnp.where(i == 0, WINDOW, 0)) if j == 0 else in_band
        q = q_ref[j * WINDOW:(j + 1) * WINDOW, :] * (HEAD_DIM ** -0.5)
        band = slice(j * WINDOW, (j + 2) * WINDOW)
        for h in range(ATT_KV_HEADS):
            qh = _stack_query_heads(q, h)
            s = _dot_nt(kd[h][band], qh)
            probs.append(dict(h=h, band=band, s=jnp.where(valid, s, NEG_INF)))
    for p in probs:
        p["m"] = jnp.maximum(jnp.max(p["s"], axis=0, keepdims=True), sinks[p["h"]])
    for p in probs:
        p["p"] = jnp.exp(p["s"] - p["m"])
    for p in probs:
        h = p["h"]
        denom = jnp.sum(p["p"], axis=0, keepdims=True) + jnp.exp(sinks[h] - p["m"])
        p["o_t"] = _dot(v_t[h * HEAD_DIM:(h + 1) * HEAD_DIM, p["band"]], p["p"].astype(BF16)) / denom
    for j in range(qb // WINDOW):
        pieces = [p["o_t"][:, g * WINDOW:(g + 1) * WINDOW]
                  for p in probs[j * ATT_KV_HEADS:(j + 1) * ATT_KV_HEADS] for g in range(ATT_GROUP)]
        o_ref[j * WINDOW:(j + 1) * WINDOW, :] = jnp.concatenate(pieces, axis=0).T.astype(o_ref.dtype)


def _swa_prompt(q, k, v, sinks, nbatch, qb):
    t = q.shape[0]
    seq = t // nbatch
    nq = seq // qb
    wpq = qb // WINDOW
    wps = seq // WINDOW

    def cur(width):
        return pl.BlockSpec((qb, width), lambda b, i: (b * nq + i, 0))

    def prev(width):
        return pl.BlockSpec(
            (WINDOW, width), lambda b, i: (b * wps + jnp.maximum(i * wpq - 1, 0), 0))

    return pl.pallas_call(
        functools.partial(_swa_prompt_kernel, qb=qb),
        grid=(nbatch, nq),
        in_specs=[
            pl.BlockSpec(memory_space=pltpu.SMEM),
            cur(ATT_WIDTH), cur(KV_WIDTH), prev(KV_WIDTH), cur(KV_WIDTH), prev(KV_WIDTH),
        ],
        out_specs=cur(ATT_WIDTH),
        out_shape=jax.ShapeDtypeStruct((t, ATT_WIDTH), BF16),
        compiler_params=pltpu.CompilerParams(
            dimension_semantics=("arbitrary", "arbitrary"), vmem_limit_bytes=VMEM_LIMIT),
    )(sinks, q, k, k, v, v)


def _swa_sample_kernel(sink_ref, q_ref, knt_ref, vnt_ref, ck_ref, cv_ref,
                       o_ref, kbuf_ref, vbuf_ref, *, nb, tq):
    past = ck_ref.shape[1]
    hw = ATT_KV_HEADS * HEAD_DIM
    heads = ATT_KV_HEADS * ATT_GROUP
    n = heads * tq
    fresh = past - tq
    t = lax.broadcasted_iota(jnp.int32, (n, 2 * past), 0) & (tq - 1)
    col = lax.broadcasted_iota(jnp.int32, (n, 2 * past), 1)
    valid = ((col < past) & (col >= t)) | ((col >= past + fresh) & (col - (past + fresh) <= t))
    sink = jnp.concatenate([_sink_column(sink_ref, h, tq) for h in range(ATT_KV_HEADS)], axis=0)
    lane = lax.broadcasted_iota(jnp.int32, (hw, past), 1)
    knt = knt_ref[...]
    vnt = vnt_ref[...]
    zero_slab = jnp.zeros((tq, LANES), F32)
    lo = _low_half((tq, LANES))

    def shifted(old, new_t, b):
        return jnp.where(lane >= fresh, pltpu.roll(new_t, (fresh - b * tq) % past, axis=1),
                         pltpu.roll(old, fresh, axis=1))

    def dup_rows(x):
        h0, h1 = x[:HEAD_DIM], x[HEAD_DIM:]
        return jnp.concatenate([h0, h0, h1, h1], axis=0)

    for b in range(nb):
        k_old = ck_ref[b * hw:(b + 1) * hw, :]
        v_old = cv_ref[b * hw:(b + 1) * hw, :]
        k_new = shifted(k_old, knt, b)
        v_new = shifted(v_old, vnt, b)
        kbuf_ref[b * hw:(b + 1) * hw, :] = k_new
        vbuf_ref[b * hw:(b + 1) * hw, :] = v_new
        kop = jnp.concatenate([dup_rows(k_old), dup_rows(k_new)], axis=1).astype(BF16)
        vop = jnp.concatenate([dup_rows(v_old), dup_rows(v_new)], axis=1).astype(BF16)

        q = q_ref[b * tq:(b + 1) * tq, :]
        blocks = []
        for hq in range(heads):
            slab = q[:, (hq // 2) * LANES:(hq // 2 + 1) * LANES]
            slab = jnp.where(lo if hq % 2 == 0 else jnp.logical_not(lo), slab, 0.0)
            pair = [slab, zero_slab] if hq // ATT_GROUP == 0 else [zero_slab, slab]
            blocks.append(jnp.concatenate(pair, axis=1))
        qx = jnp.concatenate(blocks, axis=0).astype(BF16)

        s = _dot(qx, kop) * (HEAD_DIM ** -0.5)
        s = jnp.where(valid, s, NEG_INF)
        m = jnp.maximum(jnp.max(s, axis=-1, keepdims=True), sink)
        p = jnp.exp(s - m)
        denom = jnp.sum(p, axis=-1, keepdims=True) + jnp.exp(sink - m)
        o = _dot_nt(p.astype(BF16), vop) / denom

        slabs = []
        for j in range(heads // 2):
            h = (2 * j) // ATT_GROUP
            even = o[(2 * j) * tq:(2 * j + 1) * tq, h * LANES:(h + 1) * LANES]
            odd = o[(2 * j + 1) * tq:(2 * j + 2) * tq, h * LANES:(h + 1) * LANES]
            slabs.append(jnp.where(lo, even, odd))
        o_ref[b * tq:(b + 1) * tq, :] = jnp.concatenate(slabs, axis=1)


def _swa_sample(q, knt, vnt, cache_kt, cache_vt, sinks, nbatch, nb):
    t = q.shape[0]
    tq = t // nbatch
    past = cache_kt.shape[1]
    hw = ATT_KV_HEADS * HEAD_DIM
    assert past == WINDOW and tq & (tq - 1) == 0 and nb * tq == past

    def rows(r, width):
        return pl.BlockSpec((r, width), lambda i: (i, 0))

    new_t = pl.BlockSpec((hw, nb * tq), lambda i: (0, i))
    return pl.pallas_call(
        functools.partial(_swa_sample_kernel, nb=nb, tq=tq),
        grid=(nbatch // nb,),
        in_specs=[
            pl.BlockSpec(memory_space=pltpu.SMEM),
            rows(nb * tq, ATT_WIDTH), new_t, new_t, rows(nb * hw, past), rows(nb * hw, past),
        ],
        out_specs=[rows(nb * tq, ATT_WIDTH), rows(nb * hw, past), rows(nb * hw, past)],
        out_shape=[
            jax.ShapeDtypeStruct((t, ATT_WIDTH), F32),
            jax.ShapeDtypeStruct(cache_kt.shape, F32),
            jax.ShapeDtypeStruct(cache_vt.shape, F32),
        ],
        compiler_params=pltpu.CompilerParams(
            dimension_semantics=("arbitrary",), vmem_limit_bytes=VMEM_LIMIT),
    )(sinks, q, knt, vnt, cache_kt, cache_vt)


def _mlstm_intra(q, k, v, it_c, bt_c, it_r, bt_r, m_c, valid):
    d = jnp.where(valid, bt_c - bt_r + it_r, NEG_INF)
    inter = bt_c + m_c
    mt = jnp.maximum(jnp.max(d, axis=-1, keepdims=True), inter)
    sm = _dot_nt(q, k) * jnp.exp(d - mt)
    wi = jnp.exp(inter - mt)
    return _dot(sm.astype(BF16), v), jnp.sum(sm, axis=-1, keepdims=True), mt, wi


def _mlstm_head_out(num, den, mt, og, g_head):
    lower = jnp.maximum(jnp.abs(den), jnp.exp(-mt))
    h = num / lower
    hn = h * lax.rsqrt(jnp.mean(h * h, axis=-1, keepdims=True) + EPS)
    return hn * g_head * jax.nn.sigmoid(og)


def _mlstm_prompt_kernel(mq_ref, mk_ref, mv_ref, og_ref, gr_ref, gh_ref,
                         hm_ref, c_ref, n_ref, m_ref, *, nbatch, chunk):
    @pl.when(pl.program_id(0) == 0)
    def _():
        c_ref[...] = jnp.zeros_like(c_ref)
        n_ref[...] = jnp.zeros_like(n_ref)
        m_ref[...] = jnp.zeros_like(m_ref)

    s_idx = lax.broadcasted_iota(jnp.int32, (chunk, chunk), 0)
    l_idx = lax.broadcasted_iota(jnp.int32, (chunk, chunk), 1)
    causal = s_idx <= l_idx
    triu = jnp.where(causal, 1.0, 0.0).astype(BF16)
    lo = _low_half((chunk, LANES))
    pad_rows = jnp.zeros((LANES - MLSTM_HEADS, chunk), F32)
    sub8 = (GATE_ROWS, chunk)

    probs = []
    for b in range(nbatch):
        gr = gr_ref[b]
        btr = _cumsum_rows(gr, triu)
        it_rows = gr[:MLSTM_HEADS]
        bt_rows = btr[MLSTM_HEADS:2 * MLSTM_HEADS]
        u_cols = jnp.concatenate([it_rows - bt_rows, pad_rows], axis=0).T
        for h in range(MLSTM_HEADS):
            slab = slice((h // 2) * LANES, (h // 2 + 1) * LANES)
            keep = lo if h % 2 == 0 else jnp.logical_not(lo)
            q = mq_ref[b, :, slab]
            k = mk_ref[b, :, slab]
            probs.append(dict(
                b=b, h=h, idx=b * MLSTM_HEADS + h, cols=slice(h * MLSTM_DV, (h + 1) * MLSTM_DV),
                q=jnp.where(keep, q, jnp.zeros_like(q)), k=jnp.where(keep, k, jnp.zeros_like(k)),
                u_c=u_cols[:, h:h + 1], it_r=it_rows[h:h + 1, :], bt_r=bt_rows[h:h + 1, :]))

    for p in probs:
        p["m_old"] = m_ref[p["idx"]][:, 0:1]
        p["d"] = jnp.where(causal, p["bt_r"] + p["u_c"], NEG_INF)
        p["inter"] = p["bt_r"] + p["m_old"]
        p["kq"] = _dot_nt(p["k"], p["q"])
        p["q_t"] = p["q"].astype(F32).T.astype(BF16)
        p["v_t"] = mv_ref[p["b"], :, p["cols"]].astype(F32).T
    for p in probs:
        p["mt"] = jnp.maximum(jnp.max(p["d"], axis=0, keepdims=True), p["inter"])
    for p in probs:
        p["sm"] = p["kq"] * jnp.exp(p["d"] - p["mt"])
        p["wi"] = jnp.exp(p["inter"] - p["mt"])
    for p in probs:
        c_old = c_ref[p["idx"]]
        n_old = jnp.broadcast_to(n_ref[p["idx"]], (sub8[0], LANES))
        num = (_dot(p["v_t"].astype(BF16), p["sm"].astype(BF16))
               + p["wi"] * _dot(c_old.astype(BF16), p["q_t"]))
        den = (jnp.sum(p["sm"], axis=0, keepdims=True)
               + p["wi"] * _dot(n_old.astype(BF16), p["q_t"])[0:1])
        lower = jnp.maximum(jnp.abs(den), jnp.exp(-p["mt"]))
        h_t = num / lower
        p["hn"] = (h_t * lax.rsqrt(jnp.mean(h_t * h_t, axis=0, keepdims=True) + EPS)).T
    for p in probs:
        og = og_ref[p["b"], :, p["cols"]].astype(F32)
        hm_ref[p["b"], :, p["cols"]] = (
            p["hn"] * gh_ref[:, p["cols"]] * jax.nn.sigmoid(og)).astype(hm_ref.dtype)
    for p in probs:
        idx = p["idx"]
        b_last = p["bt_r"][:, chunk - 1:chunk]
        m_new = p["mt"][:, chunk - 1:chunk]
        wk = jnp.exp(b_last - p["bt_r"] + p["it_r"] - m_new)
        wc = jnp.exp(b_last + p["m_old"] - m_new)
        c_ref[idx] = wc * c_ref[idx] + _dot((p["v_t"] * wk).astype(BF16), p["k"])
        n_ref[idx] = wc * n_ref[idx] + _dot(jnp.broadcast_to(wk, sub8).astype(BF16), p["k"])[0:1]
        m_ref[idx] = jnp.broadcast_to(m_new, (1, LANES))


def _mlstm_prompt(mq, mk, mv, og, grow, g_head, nbatch, chunk):
    t = mq.shape[0]
    seq = t // nbatch
    nstate = nbatch * MLSTM_HEADS

    def seq3(a):
        return a.reshape(nbatch, seq, a.shape[-1])

    def chunks(width):
        return pl.BlockSpec((nbatch, chunk, width), lambda c: (0, c, 0))

    def state(shape):
        return pl.BlockSpec(shape, lambda c: (0, 0, 0))

    hm, c_st, n_st, m_st = pl.pallas_call(
        functools.partial(_mlstm_prompt_kernel, nbatch=nbatch, chunk=chunk),
        grid=(seq // chunk,),
        in_specs=[
            chunks(MQK_WIDTH), chunks(MQK_WIDTH), chunks(MLSTM_WIDTH), chunks(MLSTM_WIDTH),
            pl.BlockSpec((nbatch, GATE_ROWS, chunk), lambda c: (0, 0, c)),
            _const_spec((1, MLSTM_WIDTH)),
        ],
        out_specs=[
            chunks(MLSTM_WIDTH),
            state((nstate, MLSTM_DV, LANES)), state((nstate, 1, LANES)), state((nstate, 1, LANES)),
        ],
        out_shape=[
            jax.ShapeDtypeStruct((nbatch, seq, MLSTM_WIDTH), BF16),
            jax.ShapeDtypeStruct((nstate, MLSTM_DV, LANES), F32),
            jax.ShapeDtypeStruct((nstate, 1, LANES), F32),
            jax.ShapeDtypeStruct((nstate, 1, LANES), F32),
        ],
        compiler_params=pltpu.CompilerParams(
            dimension_semantics=("arbitrary",), vmem_limit_bytes=VMEM_LIMIT),
    )(seq3(mq), seq3(mk), seq3(mv), seq3(og), grow, g_head)

    c_st = c_st.reshape(nbatch, MLSTM_HEADS, MLSTM_DV, 2, MLSTM_DK)
    n_st = n_st.reshape(nbatch, MLSTM_HEADS, 2, MLSTM_DK)
    c_fin = jnp.stack([c_st[:, h, :, h % 2, :] for h in range(MLSTM_HEADS)], axis=1)
    n_fin = jnp.stack([n_st[:, h, h % 2, :] for h in range(MLSTM_HEADS)], axis=1)
    m_fin = m_st[:, 0, 0].reshape(nbatch, MLSTM_HEADS)
    return hm.reshape(t, MLSTM_WIDTH), c_fin, n_fin, m_fin


def _mlstm_sample_kernel(mq_ref, mk_ref, mkt_ref, mv_ref, og_ref, gc_ref, gr_ref, mrow_ref, c_ref,
                         n_ref, gh_ref, hm_ref, co_ref, no_ref, mt_ref, *, nb, tq):
    rows = nb * tq
    shift = tq.bit_length() - 1
    r = lax.broadcasted_iota(jnp.int32, (rows, rows), 0)
    s = lax.broadcasted_iota(jnp.int32, (rows, rows), 1)
    same = (r >> shift) == (s >> shift)
    valid = same & (s <= r)
    tril = jnp.where(valid, 1.0, 0.0).astype(BF16)
    triu = jnp.where(same & (r <= s), 1.0, 0.0).astype(BF16)
    row_batch = lax.broadcasted_iota(jnp.int32, (rows, 1), 0) >> shift
    lane = lax.broadcasted_iota(jnp.int32, (rows, LANES), 1)

    gc = gc_ref[...]
    gr = gr_ref[0]
    btc = _cumsum_cols(tril, gc)
    btr = _cumsum_rows(gr, triu)
    m_rows = mrow_ref[...]
    mt_all = jnp.zeros((rows, LANES), F32)

    def last_of_batch(col):
        parts = [jnp.broadcast_to(col[(b + 1) * tq - 1:(b + 1) * tq], (tq, 1)) for b in range(nb)]
        return jnp.concatenate(parts, axis=0)

    dk = MLSTM_DK
    lo = _low_half((rows, LANES))
    own_q = row_batch == (lax.broadcasted_iota(jnp.int32, (rows, nb * dk), 1) >> _log2(dk))
    lane_batch = lax.broadcasted_iota(jnp.int32, (dk, rows), 1) >> shift

    def state_rows(b, h):
        return slice((b * MLSTM_HEADS + h) * dk, (b * MLSTM_HEADS + h + 1) * dk)

    for h in range(MLSTM_HEADS):
        slab = slice((h // 2) * LANES, (h // 2 + 1) * LANES)
        qs, ks = mq_ref[:, slab], mk_ref[:, slab]
        qsw, ksw = pltpu.roll(qs, dk, axis=1), pltpu.roll(ks, dk, axis=1)
        q2 = jnp.where(lo, qs, qsw) if h % 2 == 0 else jnp.where(lo, qsw, qs)
        k2 = jnp.where(lo, ks, ksw) if h % 2 == 0 else jnp.where(lo, ksw, ks)
        qb, kb = q2[:, :dk].astype(BF16), k2[:, :dk].astype(BF16)
        vb = mv_ref[:, h * MLSTM_DV:(h + 1) * MLSTM_DV].astype(BF16)
        it_c = gc[:, h:h + 1]
        bt_c = btc[:, MLSTM_HEADS + h:MLSTM_HEADS + h + 1]
        it_r = gr[h:h + 1, :]
        bt_r = btr[MLSTM_HEADS + h:MLSTM_HEADS + h + 1, :]
        m_c = m_rows[:, h:h + 1]

        num, ssum, mt, wi = _mlstm_intra(qb, kb, vb, it_c, bt_c, it_r, bt_r, m_c, valid)
        qx = jnp.where(own_q, jnp.concatenate([q2] * (nb // 2), axis=1), 0.0).astype(BF16)
        c_stack = jnp.concatenate([c_ref[state_rows(b, h), :] for b in range(nb)], axis=0)
        qc = _dot(qx, c_stack.astype(BF16))
        n_rows = jnp.concatenate(
            [jnp.broadcast_to(n_ref[b * MLSTM_HEADS + h], (tq, dk)) for b in range(nb)], axis=0)
        num = num + wi * qc
        den = ssum + wi * jnp.sum(qb.astype(F32) * n_rows, axis=-1, keepdims=True)
        og = og_ref[:, h * MLSTM_DV:(h + 1) * MLSTM_DV]
        g_head = gh_ref[:, h * MLSTM_DV:(h + 1) * MLSTM_DV]
        hm_ref[:, h * MLSTM_DV:(h + 1) * MLSTM_DV] = _mlstm_head_out(num, den, mt, og, g_head)

        b_last = last_of_batch(bt_c)
        m_new = last_of_batch(mt)
        wk = jnp.exp(b_last - bt_c + it_c - m_new)
        wc = jnp.exp(b_last + m_c - m_new)
        wkv = (wk * vb.astype(F32)).astype(BF16)
        wkk = wk * kb.astype(F32)
        kt = mkt_ref[h * dk:(h + 1) * dk, :]
        kx = jnp.concatenate([jnp.where(lane_batch == b, kt, 0.0) for b in range(nb)], axis=0)
        upd = _dot(kx.astype(BF16), wkv)
        for b in range(nb):
            idx = b * MLSTM_HEADS + h
            wc_b = wc[b * tq:b * tq + 1]
            co_ref[state_rows(b, h), :] = wc_b * c_ref[state_rows(b, h), :] + upd[b * dk:(b + 1) * dk]
            no_ref[idx] = wc_b * n_ref[idx] + jnp.sum(wkk[b * tq:(b + 1) * tq], axis=0, keepdims=True)
        mt_all = jnp.where(lane == h, mt, mt_all)
    mt_ref[...] = mt_all


def _mlstm_sample(mq, mk, mkt, mv, og, gcol, grow, m_rows, ct_in, n_in, g_head, nbatch, nb):
    t = mq.shape[0]
    tq = t // nbatch
    rows = nb * tq
    assert tq & (tq - 1) == 0 and rows == LANES and nb % 2 == 0

    def tok(width):
        return pl.BlockSpec((rows, width), lambda i: (i, 0))

    ct_spec = pl.BlockSpec((nb * MLSTM_HEADS * MLSTM_DK, MLSTM_DV), lambda i: (i, 0))
    n_spec = pl.BlockSpec((nb * MLSTM_HEADS, 1, MLSTM_DK), lambda i: (i, 0, 0))
    return pl.pallas_call(
        functools.partial(_mlstm_sample_kernel, nb=nb, tq=tq),
        grid=(nbatch // nb,),
        in_specs=[
            tok(MQK_WIDTH), tok(MQK_WIDTH),
            pl.BlockSpec((MQK_WIDTH, rows), lambda i: (0, i)),
            tok(MLSTM_WIDTH), tok(MLSTM_WIDTH), tok(LANES),
            pl.BlockSpec((1, GATE_ROWS, rows), lambda i: (0, 0, i)),
            tok(LANES),
            ct_spec, n_spec,
            _const_spec((1, MLSTM_WIDTH)),
        ],
        out_specs=[tok(MLSTM_WIDTH), ct_spec, n_spec, tok(LANES)],
        out_shape=[
            jax.ShapeDtypeStruct((t, MLSTM_WIDTH), F32),
            jax.ShapeDtypeStruct(ct_in.shape, F32),
            jax.ShapeDtypeStruct(n_in.shape, F32),
            jax.ShapeDtypeStruct((t, LANES), F32),
        ],
        compiler_params=pltpu.CompilerParams(
            dimension_semantics=("arbitrary",), vmem_limit_bytes=VMEM_LIMIT),
    )(mq, mk, mkt, mv, og, gcol, grow, m_rows, ct_in, n_in, g_head)


def _memkv_kernel(mem_ref, g_ref, wk_ref, wv_ref, k_ref, v_ref):
    mn = _rmsnorm(mem_ref[...], g_ref[...]).astype(BF16)
    k_ref[...] = _dot(mn, wk_ref[...])
    v_ref[...] = _dot(mn, wv_ref[...])


def _memkv(mem, w, nbatch):
    t, d = mem.shape
    m = t // nbatch
    spec = pl.BlockSpec((m, CROSS_WIDTH), lambda b: (b, 0))
    return pl.pallas_call(
        _memkv_kernel,
        grid=(nbatch,),
        in_specs=[
            pl.BlockSpec((m, d), lambda b: (b, 0)),
            _const_spec((1, d)), _const_spec((d, CROSS_WIDTH)), _const_spec((d, CROSS_WIDTH)),
        ],
        out_specs=[spec, spec],
        out_shape=[jax.ShapeDtypeStruct((t, CROSS_WIDTH), F32)] * 2,
        compiler_params=pltpu.CompilerParams(
            dimension_semantics=("arbitrary",), vmem_limit_bytes=VMEM_LIMIT),
    )(mem, w["g_mem"], w["w_ck"], w["w_cv"])


def _mix_stage(x, att, hm, wo_ref, gcr_ref, wcq_ref):
    cat = jnp.concatenate([att.astype(BF16), hm.astype(BF16)], axis=1)
    y1 = x + _dot(cat, wo_ref[...])
    qc = _dot(_rmsnorm(y1, gcr_ref[...]).astype(BF16), wcq_ref[...])
    return y1, qc


def _cross_attention(qc, mk, mv):
    slabs = []
    for j in range(CROSS_WIDTH // LANES):
        qs = qc[:, j * LANES:(j + 1) * LANES]
        ks = mk[:, j * LANES:(j + 1) * LANES]
        vs = mv[:, j * LANES:(j + 1) * LANES]
        lo = _low_half(qs.shape)
        halves = []
        for keep in (lo, jnp.logical_not(lo)):
            s = _dot_nt(jnp.where(keep, qs, jnp.zeros_like(qs)), ks) * (HEAD_DIM ** -0.5)
            p = jnp.exp(s - jnp.max(s, axis=-1, keepdims=True))
            halves.append(_dot(p.astype(BF16), vs) / jnp.sum(p, axis=-1, keepdims=True))
        slabs.append(jnp.where(lo, halves[0], halves[1]))
    return jnp.concatenate(slabs, axis=1)


def _ffn_chunks(hidden):
    step = 512
    return [(c, min(c + step, hidden)) for c in range(0, hidden, step)]


def _ffn_stage(y1, o, wco_ref, gf_ref, wg_ref, wu_ref, wd_ref, act_ref, gfin_ref, final):
    y2 = y1 + _dot(o.astype(BF16), wco_ref[...])
    xn = _rmsnorm(y2, gf_ref[...]).astype(BF16)
    for lo, hi in _ffn_chunks(wg_ref.shape[1]):
        g = _dot(xn, wg_ref[:, lo:hi])
        u = _dot(xn, wu_ref[:, lo:hi])
        act_ref[:, lo:hi] = (g * jax.nn.sigmoid(g) * u).astype(BF16)
    y3 = y2 + _dot(act_ref[...], wd_ref[...])
    return _rmsnorm(y3, gfin_ref[...]) if final else y3


def _post_prompt_kernel(x_ref, att_ref, hm_ref, mk_ref, mv_ref, wo_ref, gcr_ref, wcq_ref, wco_ref,
                        gf_ref, wg_ref, wu_ref, wd_ref, gfin_ref, o_ref, act_ref, *, final):
    y1, qc = _mix_stage(x_ref[...], att_ref[...], hm_ref[...], wo_ref, gcr_ref, wcq_ref)
    o = _cross_attention(qc.astype(BF16), mk_ref[...].astype(BF16), mv_ref[...].astype(BF16))
    o_ref[...] = _ffn_stage(y1, o, wco_ref, gf_ref, wg_ref, wu_ref, wd_ref, act_ref, gfin_ref, final)


_MIX_WEIGHTS = ("w_out", "g_cross", "w_cq")
_FFN_WEIGHTS = ("w_co", "g_ffn", "w_gate", "w_up", "w_down", "g_final")


def _post_prompt(x, att, hm, memk, memv, w, nbatch, final, tm):
    t, d = x.shape
    tpb = t // nbatch // tm
    m = memk.shape[0] // nbatch
    hidden = w["w_gate"].shape[1]

    def rows(width):
        return pl.BlockSpec((tm, width), lambda i: (i, 0))

    mem_spec = pl.BlockSpec((m, CROSS_WIDTH), lambda i: (i // tpb, 0))
    names = _MIX_WEIGHTS + _FFN_WEIGHTS
    return pl.pallas_call(
        functools.partial(_post_prompt_kernel, final=final),
        grid=(t // tm,),
        in_specs=[rows(d), rows(ATT_WIDTH), rows(MLSTM_WIDTH), mem_spec, mem_spec]
        + _weight_specs(w, names),
        out_specs=rows(d),
        out_shape=jax.ShapeDtypeStruct((t, d), F32),
        scratch_shapes=[pltpu.VMEM((tm, hidden), BF16)],
        compiler_params=pltpu.CompilerParams(
            dimension_semantics=("arbitrary",), vmem_limit_bytes=VMEM_LIMIT),
    )(x, att, hm, memk, memv, *[w[n] for n in names])


def _mix_kernel(x_ref, att_ref, hm_ref, wo_ref, gcr_ref, wcq_ref, y_ref, qc_ref):
    y1, qc = _mix_stage(x_ref[...], att_ref[...], hm_ref[...], wo_ref, gcr_ref, wcq_ref)
    y_ref[...] = y1
    qc_ref[...] = qc


def _mix(x, att, hm, w, tm):
    t, d = x.shape

    def rows(width):
        return pl.BlockSpec((tm, width), lambda i: (i, 0))

    return pl.pallas_call(
        _mix_kernel,
        grid=(t // tm,),
        in_specs=[rows(d), rows(ATT_WIDTH), rows(MLSTM_WIDTH)] + _weight_specs(w, _MIX_WEIGHTS),
        out_specs=[rows(d), rows(CROSS_WIDTH)],
        out_shape=[jax.ShapeDtypeStruct((t, d), F32), jax.ShapeDtypeStruct((t, CROSS_WIDTH), F32)],
        compiler_params=pltpu.CompilerParams(
            dimension_semantics=("arbitrary",), vmem_limit_bytes=VMEM_LIMIT),
    )(x, att, hm, *[w[n] for n in _MIX_WEIGHTS])


def _cross_sample_kernel(qc_ref, mk_ref, mv_ref, o_ref, *, nb, tq):
    mem = CROSS_WIDTH
    n = CROSS_HEADS * tq
    row_head = lax.broadcasted_iota(jnp.int32, (n, CROSS_WIDTH), 0) >> _log2(tq)
    lane_head = lax.broadcasted_iota(jnp.int32, (n, CROSS_WIDTH), 1) >> _log2(HEAD_DIM)
    own = row_head == lane_head
    for b in range(nb):
        q = qc_ref[b * tq:(b + 1) * tq, :]
        qx = jnp.where(own, jnp.concatenate([q] * CROSS_HEADS, axis=0), 0.0).astype(BF16)
        mk = mk_ref[b * mem:(b + 1) * mem, :].astype(BF16)
        mv = mv_ref[b * mem:(b + 1) * mem, :].astype(BF16)
        s = _dot(qx, mk) * (HEAD_DIM ** -0.5)
        p = jnp.exp(s - jnp.max(s, axis=-1, keepdims=True))
        pv = _dot_nt(p.astype(BF16), mv) / jnp.sum(p, axis=-1, keepdims=True)
        pv = jnp.where(own, pv, 0.0)
        out = pv[0:tq]
        for h in range(1, CROSS_HEADS):
            out = out + pv[h * tq:(h + 1) * tq]
        o_ref[b * tq:(b + 1) * tq, :] = out


def _cross_sample(qc, memk_t, memv_t, nbatch, nb):
    t = qc.shape[0]
    tq = t // nbatch
    mem = memk_t.shape[1]
    tok = pl.BlockSpec((nb * tq, CROSS_WIDTH), lambda i: (i, 0))
    mspec = pl.BlockSpec((nb * CROSS_WIDTH, mem), lambda i: (i, 0))
    return pl.pallas_call(
        functools.partial(_cross_sample_kernel, nb=nb, tq=tq),
        grid=(nbatch // nb,),
        in_specs=[tok, mspec, mspec],
        out_specs=tok,
        out_shape=jax.ShapeDtypeStruct((t, CROSS_WIDTH), F32),
        compiler_params=pltpu.CompilerParams(
            dimension_semantics=("arbitrary",), vmem_limit_bytes=VMEM_LIMIT),
    )(qc, memk_t, memv_t)


def _ffn_kernel(y_ref, o_ref, wco_ref, gf_ref, wg_ref, wu_ref, wd_ref, gfin_ref, out_ref, act_ref,
                *, final):
    out_ref[...] = _ffn_stage(y_ref[...], o_ref[...], wco_ref, gf_ref, wg_ref, wu_ref, wd_ref,
                              act_ref, gfin_ref, final)


def _ffn(y1, o, w, final, tm):
    t, d = y1.shape
    hidden = w["w_gate"].shape[1]

    def rows(width):
        return pl.BlockSpec((tm, width), lambda i: (i, 0))

    return pl.pallas_call(
        functools.partial(_ffn_kernel, final=final),
        grid=(t // tm,),
        in_specs=[rows(d), rows(CROSS_WIDTH)] + _weight_specs(w, _FFN_WEIGHTS),
        out_specs=rows(d),
        out_shape=jax.ShapeDtypeStruct((t, d), F32),
        scratch_shapes=[pltpu.VMEM((tm, hidden), BF16)],
        compiler_params=pltpu.CompilerParams(
            dimension_semantics=("arbitrary",), vmem_limit_bytes=VMEM_LIMIT),
    )(y1, o, *[w[n] for n in _FFN_WEIGHTS])


def _layer_weights(l, w_in, b_igate, b_fgate, g_mlstm_head, w_out, g_mix, g_cross, g_mem, w_cq,
                   w_ck, w_cv, w_co, g_ffn, w_gate, w_up, w_down, g_final):
    d = w_in.shape[1]
    gate_w = w_in[l][:, MAIN_WIDTH:]
    gate_b = jnp.concatenate([b_igate[l], b_fgate[l]]).astype(F32)
    ngate = gate_b.shape[0]
    row = lambda a: a.astype(F32).reshape(1, -1)
    w_t = w_in[l].T
    k0 = ATT_WIDTH
    mk0 = ATT_WIDTH + 2 * KV_WIDTH + MQK_WIDTH
    return {
        "w_kvk_t": jnp.concatenate([w_t[k0:k0 + 2 * KV_WIDTH], w_t[mk0:mk0 + MQK_WIDTH]]).astype(BF16),
        "w_main": w_in[l][:, :MAIN_WIDTH].astype(BF16),
        "w_gate_cols": jnp.pad(gate_w, ((0, 0), (0, LANES - ngate))).astype(BF16),
        "w_gate_rows": jnp.pad(gate_w.T, ((0, GATE_ROWS - ngate), (0, 0))).astype(BF16),
        "b_gate_cols": jnp.pad(gate_b, (0, LANES - ngate)).reshape(1, LANES),
        "b_gate_rows": jnp.pad(gate_b, (0, GATE_ROWS - ngate)).reshape(GATE_ROWS, 1),
        "g_mix": row(g_mix[l]), "g_cross": row(g_cross[l]), "g_mem": row(g_mem[l]),
        "g_ffn": row(g_ffn[l]), "g_final": row(g_final), "g_head": row(g_mlstm_head[l]),
        "w_out": w_out[l].astype(BF16), "w_cq": w_cq[l].astype(BF16), "w_ck": w_ck[l].astype(BF16),
        "w_cv": w_cv[l].astype(BF16), "w_co": w_co[l].astype(BF16), "w_gate": w_gate[l].astype(BF16),
        "w_up": w_up[l].astype(BF16), "w_down": w_down[l].astype(BF16),
    }


TOKEN_TILE = 512
SWA_QUERY_BLOCK = 512
MLSTM_CHUNK_ROWS = 256
SWA_SAMPLE_BATCH = 16
MLSTM_SAMPLE_BATCH = 16
CROSS_SAMPLE_BATCH = 8


def kernel(x_prompt, x_sample, mem_prompt, cache_swa_k, cache_swa_v, state_mlstm_C, state_mlstm_n,
           state_mlstm_m, cache_mem_k, cache_mem_v, w_in, b_igate, b_fgate, attn_sinks,
           g_mlstm_head, w_out, g_mix, g_cross, g_mem, w_cq, w_ck, w_cv, w_co, g_ffn, w_gate,
           w_up, w_down, g_final):
    depth = w_in.shape[0]
    bp, sp, d = x_prompt.shape
    bs, ss, _ = x_sample.shape
    mem_tokens = mem_prompt.shape[1]
    past = cache_swa_k.shape[2]
    yp = x_prompt.reshape(bp * sp, d)
    ys = x_sample.reshape(bs * ss, d)
    mem = mem_prompt.reshape(bp * mem_tokens, d)
    outs = [[] for _ in range(12)]

    for l in range(depth):
        final = l == depth - 1
        w = _layer_weights(l, w_in, b_igate, b_fgate, g_mlstm_head, w_out, g_mix, g_cross, g_mem,
                           w_cq, w_ck, w_cv, w_co, g_ffn, w_gate, w_up, w_down, g_final)
        sinks = attn_sinks[l].astype(F32)

        q, k, v, mq, mk, mv, og, gcol, grow = _proj(yp, w, bp, BF16, TOKEN_TILE)
        att = _swa_prompt(q, k, v, sinks, bp, SWA_QUERY_BLOCK)
        hm, c_p, n_p, m_p = _mlstm_prompt(mq, mk, mv, og, grow, w["g_head"], bp, MLSTM_CHUNK_ROWS)
        memk, memv = _memkv(mem, w, bp)
        yp = _post_prompt(yp, att, hm, memk, memv, w, bp, final, TOKEN_TILE)
        win_shape = (bp, WINDOW, ATT_KV_HEADS, HEAD_DIM)
        outs[0].append(k.reshape(bp, sp, KV_WIDTH)[:, sp - WINDOW:].reshape(win_shape))
        outs[1].append(v.reshape(bp, sp, KV_WIDTH)[:, sp - WINDOW:].reshape(win_shape))
        outs[2].append(c_p)
        outs[3].append(n_p)
        outs[4].append(m_p)
        mem_shape = (bp, mem_tokens, CROSS_HEADS, HEAD_DIM)
        outs[5].append(memk.reshape(mem_shape))
        outs[6].append(memv.reshape(mem_shape))

        def tokens_last(a):
            return jnp.transpose(a.astype(F32), (0, 2, 3, 1)).reshape(-1, a.shape[1])

        q, _, _, mq, mk, mv, og, gcol, grow, knt, vnt, mkt = _proj(
            ys, w, 1, F32, TOKEN_TILE, transposed=True)
        att, kbuf_t, vbuf_t = _swa_sample(
            q, knt, vnt, tokens_last(cache_swa_k[l]), tokens_last(cache_swa_v[l]), sinks, bs,
            SWA_SAMPLE_BATCH)
        m_rows = jnp.pad(jnp.repeat(state_mlstm_m[l].astype(F32), ss, axis=0),
                         ((0, 0), (0, LANES - MLSTM_HEADS)))
        ct_in = jnp.swapaxes(state_mlstm_C[l].astype(F32), 2, 3).reshape(-1, MLSTM_DV)
        hm, ct_s, n_s, mt = _mlstm_sample(
            mq, mk, mkt, mv, og, gcol, grow, m_rows, ct_in,
            state_mlstm_n[l].astype(F32).reshape(bs * MLSTM_HEADS, 1, MLSTM_DK),
            w["g_head"], bs, MLSTM_SAMPLE_BATCH)
        y1, qc = _mix(ys, att, hm, w, TOKEN_TILE)
        o = _cross_sample(qc, tokens_last(cache_mem_k[l]), tokens_last(cache_mem_v[l]), bs,
                          CROSS_SAMPLE_BATCH)
        ys = _ffn(y1, o, w, final, TOKEN_TILE)
        buf_t_shape = (bs, ATT_KV_HEADS, HEAD_DIM, past)
        outs[7].append(jnp.transpose(kbuf_t.reshape(buf_t_shape), (0, 3, 1, 2)))
        outs[8].append(jnp.transpose(vbuf_t.reshape(buf_t_shape), (0, 3, 1, 2)))
        outs[9].append(jnp.swapaxes(ct_s.reshape(bs, MLSTM_HEADS, MLSTM_DK, MLSTM_DV), 2, 3))
        outs[10].append(n_s.reshape(bs, MLSTM_HEADS, MLSTM_DK))
        outs[11].append(mt.reshape(bs, ss, LANES)[:, ss - 1, :MLSTM_HEADS])

    return (yp.reshape(bp, sp, d), ys.reshape(bs, ss, d)) + tuple(jnp.stack(o) for o in outs)
```

```python
import functools

import jax
import jax.numpy as jnp
from jax import lax
from jax.experimental import pallas as pl
from jax.experimental.pallas import tpu as pltpu

F32 = jnp.float32
BF16 = jnp.bfloat16

HEAD_DIM = 64
ATT_GROUP = 4
ATT_KV_HEADS = 2
ATT_WIDTH = 512
KV_WIDTH = ATT_KV_HEADS * HEAD_DIM
WINDOW = 128
LOG_WINDOW = 7
MLSTM_HEADS = 4
MLSTM_DV = 128
MLSTM_DK = 64
MQK_WIDTH = MLSTM_HEADS * MLSTM_DK
MLSTM_WIDTH = MLSTM_HEADS * MLSTM_DV
MAIN_WIDTH = ATT_WIDTH + 2 * KV_WIDTH + 2 * MQK_WIDTH + 2 * MLSTM_WIDTH
CROSS_HEADS = 4
CROSS_WIDTH = CROSS_HEADS * HEAD_DIM
EPS = 1e-6
NEG_INF = float("-inf")
assert HEAD_DIM == 4 ** 3

LANES = 128
GATE_ROWS = 16
VMEM_LIMIT = 56 * 1024 * 1024

NT = (((1,), (1,)), ((), ()))
TN = (((0,), (0,)), ((), ()))


def _dot(a, b):
    return jnp.dot(a, b, preferred_element_type=F32)


def _dot_nt(a, b):
    return lax.dot_general(a, b, NT, preferred_element_type=F32)


def _dot_tn(a, b):
    return lax.dot_general(a, b, TN, preferred_element_type=F32)


def _rmsnorm(x, g):
    return x * lax.rsqrt(jnp.mean(x * x, axis=-1, keepdims=True) + EPS) * g


def _log_sigmoid(x):
    return jnp.minimum(x, 0.0) - jnp.log1p(jnp.exp(-jnp.abs(x)))


def _split3(x):
    hi = x.astype(BF16)
    r1 = x - hi.astype(F32)
    mid = r1.astype(BF16)
    lo = (r1 - mid.astype(F32)).astype(BF16)
    return hi, mid, lo


def _cumsum_cols(tri, x):
    hi, mid, lo = _split3(x)
    return _dot(tri, hi) + _dot(tri, mid) + _dot(tri, lo)


def _cumsum_rows(x, tri):
    hi, mid, lo = _split3(x)
    return _dot(hi, tri) + _dot(mid, tri) + _dot(lo, tri)


def _log2(n):
    assert n > 0 and n & (n - 1) == 0, n
    return n.bit_length() - 1


def _low_half(shape):
    return lax.broadcasted_iota(jnp.int32, shape, 1) < HEAD_DIM


def _proj_kernel(x_ref, g_ref, w_ref, wgr_ref, br_ref, *rest, decode):
    if decode:
        wgc_ref, bc_ref, wt_ref = rest[:3]
        rest = rest[3:]
    q_ref, k_ref, v_ref, mq_ref, mk_ref, mv_ref, og_ref, gr_ref = rest[:8]
    xn = _rmsnorm(x_ref[...], g_ref[...]).astype(BF16)
    if decode:
        gc_ref, kt_ref, vt_ref, mkt_ref = rest[8:]
        zt = _dot_nt(wt_ref[...], xn)
        kt_ref[...] = zt[:KV_WIDTH]
        vt_ref[...] = zt[KV_WIDTH:2 * KV_WIDTH]
        mkt_ref[...] = zt[2 * KV_WIDTH:] * (MLSTM_DK ** -0.5)
        zc = _dot(xn, wgc_ref[...]) + bc_ref[...]
        lane = lax.broadcasted_iota(jnp.int32, zc.shape, 1)
        gc_ref[...] = jnp.where(lane < MLSTM_HEADS, zc, _log_sigmoid(zc))

    def mm(lo, hi):
        return _dot(xn, w_ref[:, lo:hi])

    o = 0
    q_ref[...] = mm(o, o + ATT_WIDTH).astype(q_ref.dtype)
    o += ATT_WIDTH
    kv = mm(o, o + 2 * KV_WIDTH)
    k_ref[...] = kv[:, :KV_WIDTH]
    v_ref[...] = kv[:, KV_WIDTH:]
    o += 2 * KV_WIDTH
    mqk = mm(o, o + 2 * MQK_WIDTH)
    mq_ref[...] = mqk[:, :MQK_WIDTH].astype(mq_ref.dtype)
    mk_ref[...] = (mqk[:, MQK_WIDTH:] * (MLSTM_DK ** -0.5)).astype(mk_ref.dtype)
    o += 2 * MQK_WIDTH
    mv_ref[...] = mm(o, o + MLSTM_WIDTH).astype(mv_ref.dtype)
    o += MLSTM_WIDTH
    og_ref[...] = mm(o, o + MLSTM_WIDTH).astype(og_ref.dtype)

    zr = _dot_nt(wgr_ref[...], xn) + br_ref[...]
    row = lax.broadcasted_iota(jnp.int32, zr.shape, 0)
    gr_ref[0] = jnp.where(row < MLSTM_HEADS, zr, _log_sigmoid(zr))


def _const_spec(shape):
    nd = len(shape)
    return pl.BlockSpec(shape, lambda *_: (0,) * nd, pipeline_mode=pl.Buffered(1))


def _weight_specs(w, names):
    return [_const_spec(w[n].shape) for n in names]


def _proj(x, w, nseg, act_dtype, tm, decode=False):
    t, d = x.shape
    seg = t // nseg
    tpb = seg // tm
    grid = (t // tm,)

    def rows(width):
        return pl.BlockSpec((tm, width), lambda i: (i, 0))

    out_shape = [
        jax.ShapeDtypeStruct((t, ATT_WIDTH), act_dtype),
        jax.ShapeDtypeStruct((t, KV_WIDTH), F32),
        jax.ShapeDtypeStruct((t, KV_WIDTH), F32),
        jax.ShapeDtypeStruct((t, MQK_WIDTH), act_dtype),
        jax.ShapeDtypeStruct((t, MQK_WIDTH), act_dtype),
        jax.ShapeDtypeStruct((t, MLSTM_WIDTH), act_dtype),
        jax.ShapeDtypeStruct((t, MLSTM_WIDTH), act_dtype),
        jax.ShapeDtypeStruct((nseg, GATE_ROWS, seg), F32),
    ]
    out_specs = [
        rows(ATT_WIDTH), rows(KV_WIDTH), rows(KV_WIDTH), rows(MQK_WIDTH), rows(MQK_WIDTH),
        rows(MLSTM_WIDTH), rows(MLSTM_WIDTH),
        pl.BlockSpec((1, GATE_ROWS, tm), lambda i: (i // tpb, 0, i % tpb)),
    ]
    names = ["g_mix", "w_main", "w_gate_rows", "b_gate_rows"]
    if decode:
        names += ["w_gate_cols", "b_gate_cols", "w_kvk_t"]
        out_shape.append(jax.ShapeDtypeStruct((t, LANES), F32))
        out_specs.append(rows(LANES))
        for width in (KV_WIDTH, KV_WIDTH, MQK_WIDTH):
            out_shape.append(jax.ShapeDtypeStruct((width, t), F32))
            out_specs.append(pl.BlockSpec((width, tm), lambda i: (0, i)))
    in_specs = [rows(d)] + _weight_specs(w, names)
    return pl.pallas_call(
        functools.partial(_proj_kernel, decode=decode),
        grid=grid,
        in_specs=in_specs,
        out_specs=out_specs,
        out_shape=out_shape,
        compiler_params=pltpu.CompilerParams(
            dimension_semantics=("arbitrary",), vmem_limit_bytes=VMEM_LIMIT),
    )(x, *[w[n] for n in names])


def _dup_heads(x):
    swapped = pltpu.roll(x, HEAD_DIM, axis=1)
    lo = _low_half(x.shape)
    return (jnp.where(lo, x, swapped).astype(BF16), jnp.where(lo, swapped, x).astype(BF16))


def _stack_query_heads(q, h):
    parts = []
    for g in range(ATT_GROUP):
        hq = h * ATT_GROUP + g
        slab = q[:, (hq // 2) * LANES:(hq // 2 + 1) * LANES]
        lo = _low_half(slab.shape)
        keep = lo if hq % 2 == 0 else jnp.logical_not(lo)
        parts.append(jnp.where(keep, slab, jnp.zeros_like(slab)))
    return jnp.concatenate(parts, axis=0).astype(BF16)


def _sink_column(sink_ref, h, rows_per_head):
    n = ATT_GROUP * rows_per_head
    grp = lax.broadcasted_iota(jnp.int32, (n, 1), 0) >> _log2(rows_per_head)
    col = jnp.full((n, 1), sink_ref[h * ATT_GROUP], F32)
    for g in range(1, ATT_GROUP):
        col = jnp.where(grp == g, sink_ref[h * ATT_GROUP + g], col)
    return col


def _swa_prompt_kernel(sink_ref, q_ref, kc_ref, kp_ref, vc_ref, vp_ref, o_ref, *, qb):
    i = pl.program_id(1)
    k_all = jnp.concatenate([kp_ref[...], kc_ref[...]], axis=0)
    v_all = jnp.concatenate([vp_ref[...], vc_ref[...]], axis=0)
    kd = _dup_heads(k_all)
    v_t = v_all.T.astype(BF16)

    n = ATT_GROUP * WINDOW
    key = lax.broadcasted_iota(jnp.int32, (2 * WINDOW, n), 0)
    qry = lax.broadcasted_iota(jnp.int32, (2 * WINDOW, n), 1) & (WINDOW - 1)
    delta = key - qry
    in_band = (delta >= 0) & (delta <= WINDOW)
    grp = lax.broadcasted_iota(jnp.int32, (1, n), 1) >> LOG_WINDOW
    sinks = []
    for h in range(ATT_KV_HEADS):
        row = jnp.full((1, n), sink_ref[h * ATT_GROUP], F32)
        for g in range(1, ATT_GROUP):
            row = jnp.where(grp == g, sink_ref[h * ATT_GROUP + g], row)
        sinks.append(row)

    probs = []
    for j in range(qb // WINDOW):
        valid = in_band & (key >= jnp.where(i == 0, WINDOW, 0)) if j == 0 else in_band
        q = q_ref[j * WINDOW:(j + 1) * WINDOW, :] * (HEAD_DIM ** -0.5)
        band = slice(j * WINDOW, (j + 2) * WINDOW)
        for h in range(ATT_KV_HEADS):
            qh = _stack_query_heads(q, h)
            s = _dot_nt(kd[h][band], qh)
            probs.append(dict(h=h, band=band, s=jnp.where(valid, s, NEG_INF)))
    for p in probs:
        p["m"] = jnp.maximum(jnp.max(p["s"], axis=0, keepdims=True), sinks[p["h"]])
    for p in probs:
        p["p"] = jnp.exp(p["s"] - p["m"])
    for p in probs:
        h = p["h"]
        denom = jnp.sum(p["p"], axis=0, keepdims=True) + jnp.exp(sinks[h] - p["m"])
        p["o_t"] = _dot(v_t[h * HEAD_DIM:(h + 1) * HEAD_DIM, p["band"]], p["p"].astype(BF16)) / denom
    for j in range(qb // WINDOW):
        pieces = [p["o_t"][:, g * WINDOW:(g + 1) * WINDOW]
                  for p in probs[j * ATT_KV_HEADS:(j + 1) * ATT_KV_HEADS] for g in range(ATT_GROUP)]
        o_ref[j * WINDOW:(j + 1) * WINDOW, :] = jnp.concatenate(pieces, axis=0).T.astype(o_ref.dtype)


def _swa_prompt(q, k, v, sinks, nbatch, qb):
    t = q.shape[0]
    seq = t // nbatch
    nq = seq // qb
    wpq = qb // WINDOW
    wps = seq // WINDOW

    def cur(width):
        return pl.BlockSpec((qb, width), lambda b, i: (b * nq + i, 0))

    def prev(width):
        return pl.BlockSpec(
            (WINDOW, width), lambda b, i: (b * wps + jnp.maximum(i * wpq - 1, 0), 0))

    return pl.pallas_call(
        functools.partial(_swa_prompt_kernel, qb=qb),
        grid=(nbatch, nq),
        in_specs=[
            pl.BlockSpec(memory_space=pltpu.SMEM),
            cur(ATT_WIDTH), cur(KV_WIDTH), prev(KV_WIDTH), cur(KV_WIDTH), prev(KV_WIDTH),
        ],
        out_specs=cur(ATT_WIDTH),
        out_shape=jax.ShapeDtypeStruct((t, ATT_WIDTH), BF16),
        compiler_params=pltpu.CompilerParams(
            dimension_semantics=("arbitrary", "arbitrary"), vmem_limit_bytes=VMEM_LIMIT),
    )(sinks, q, k, k, v, v)


def _swa_sample_kernel(sink_ref, q_ref, knt_ref, vnt_ref, ck_ref, cv_ref,
                       o_ref, kbuf_ref, vbuf_ref, *, nb, tq):
    past = ck_ref.shape[1]
    hw = ATT_KV_HEADS * HEAD_DIM
    heads = ATT_KV_HEADS * ATT_GROUP
    n = heads * tq
    fresh = past - tq
    t = lax.broadcasted_iota(jnp.int32, (n, 2 * past), 0) & (tq - 1)
    col = lax.broadcasted_iota(jnp.int32, (n, 2 * past), 1)
    valid = ((col < past) & (col >= t)) | ((col >= past + fresh) & (col - (past + fresh) <= t))
    sink = jnp.concatenate([_sink_column(sink_ref, h, tq) for h in range(ATT_KV_HEADS)], axis=0)
    lane = lax.broadcasted_iota(jnp.int32, (hw, past), 1)
    knt = knt_ref[...]
    vnt = vnt_ref[...]
    zero_slab = jnp.zeros((tq, LANES), F32)
    lo = _low_half((tq, LANES))

    def shifted(old, new_t, b):
        return jnp.where(lane >= fresh, pltpu.roll(new_t, (fresh - b * tq) % past, axis=1),
                         pltpu.roll(old, fresh, axis=1))

    def dup_rows(x):
        h0, h1 = x[:HEAD_DIM], x[HEAD_DIM:]
        return jnp.concatenate([h0, h0, h1, h1], axis=0)

    for b in range(nb):
        k_old = ck_ref[b * hw:(b + 1) * hw, :]
        v_old = cv_ref[b * hw:(b + 1) * hw, :]
        k_new = shifted(k_old, knt, b)
        v_new = shifted(v_old, vnt, b)
        kbuf_ref[b * hw:(b + 1) * hw, :] = k_new
        vbuf_ref[b * hw:(b + 1) * hw, :] = v_new
        kop = jnp.concatenate([dup_rows(k_old), dup_rows(k_new)], axis=1).astype(BF16)
        vop = jnp.concatenate([dup_rows(v_old), dup_rows(v_new)], axis=1).astype(BF16)

        q = q_ref[b * tq:(b + 1) * tq, :]
        blocks = []
        for hq in range(heads):
            slab = q[:, (hq // 2) * LANES:(hq // 2 + 1) * LANES]
            slab = jnp.where(lo if hq % 2 == 0 else jnp.logical_not(lo), slab, 0.0)
            pair = [slab, zero_slab] if hq // ATT_GROUP == 0 else [zero_slab, slab]
            blocks.append(jnp.concatenate(pair, axis=1))
        qx = jnp.concatenate(blocks, axis=0).astype(BF16)

        s = _dot(qx, kop) * (HEAD_DIM ** -0.5)
        s = jnp.where(valid, s, NEG_INF)
        m = jnp.maximum(jnp.max(s, axis=-1, keepdims=True), sink)
        p = jnp.exp(s - m)
        denom = jnp.sum(p, axis=-1, keepdims=True) + jnp.exp(sink - m)
        o = _dot_nt(p.astype(BF16), vop) / denom

        slabs = []
        for j in range(heads // 2):
            h = (2 * j) // ATT_GROUP
            even = o[(2 * j) * tq:(2 * j + 1) * tq, h * LANES:(h + 1) * LANES]
            odd = o[(2 * j + 1) * tq:(2 * j + 2) * tq, h * LANES:(h + 1) * LANES]
            slabs.append(jnp.where(lo, even, odd))
        o_ref[b * tq:(b + 1) * tq, :] = jnp.concatenate(slabs, axis=1)


def _swa_sample(q, knt, vnt, cache_kt, cache_vt, sinks, nbatch, nb):
    t = q.shape[0]
    tq = t // nbatch
    past = cache_kt.shape[1]
    hw = ATT_KV_HEADS * HEAD_DIM
    assert past == WINDOW and tq & (tq - 1) == 0 and nb * tq == past

    def rows(r, width):
        return pl.BlockSpec((r, width), lambda i: (i, 0))

    new_t = pl.BlockSpec((hw, nb * tq), lambda i: (0, i))
    return pl.pallas_call(
        functools.partial(_swa_sample_kernel, nb=nb, tq=tq),
        grid=(nbatch // nb,),
        in_specs=[
            pl.BlockSpec(memory_space=pltpu.SMEM),
            rows(nb * tq, ATT_WIDTH), new_t, new_t, rows(nb * hw, past), rows(nb * hw, past),
        ],
        out_specs=[rows(nb * tq, ATT_WIDTH), rows(nb * hw, past), rows(nb * hw, past)],
        out_shape=[
            jax.ShapeDtypeStruct((t, ATT_WIDTH), F32),
            jax.ShapeDtypeStruct(cache_kt.shape, F32),
            jax.ShapeDtypeStruct(cache_vt.shape, F32),
        ],
        compiler_params=pltpu.CompilerParams(
            dimension_semantics=("arbitrary",), vmem_limit_bytes=VMEM_LIMIT),
    )(sinks, q, knt, vnt, cache_kt, cache_vt)


def _mlstm_intra(q, k, v, it_c, bt_c, it_r, bt_r, m_c, valid):
    d = jnp.where(valid, bt_c - bt_r + it_r, NEG_INF)
    inter = bt_c + m_c
    mt = jnp.maximum(jnp.max(d, axis=-1, keepdims=True), inter)
    sm = _dot_nt(q, k) * jnp.exp(d - mt)
    wi = jnp.exp(inter - mt)
    return _dot(sm.astype(BF16), v), jnp.sum(sm, axis=-1, keepdims=True), mt, wi


def _mlstm_head_out(num, den, mt, og, g_head):
    lower = jnp.maximum(jnp.abs(den), jnp.exp(-mt))
    h = num / lower
    hn = h * lax.rsqrt(jnp.mean(h * h, axis=-1, keepdims=True) + EPS)
    return hn * g_head * jax.nn.sigmoid(og)


def _mlstm_prompt_kernel(mq_ref, mk_ref, mv_ref, og_ref, gr_ref, gh_ref,
                         hm_ref, c_ref, n_ref, m_ref, *, nbatch, chunk):
    @pl.when(pl.program_id(0) == 0)
    def _():
        c_ref[...] = jnp.zeros_like(c_ref)
        n_ref[...] = jnp.zeros_like(n_ref)
        m_ref[...] = jnp.zeros_like(m_ref)

    s_idx = lax.broadcasted_iota(jnp.int32, (chunk, chunk), 0)
    l_idx = lax.broadcasted_iota(jnp.int32, (chunk, chunk), 1)
    causal = s_idx <= l_idx
    triu = jnp.where(causal, 1.0, 0.0).astype(BF16)
    lo = _low_half((chunk, LANES))
    pad_rows = jnp.zeros((LANES - MLSTM_HEADS, chunk), F32)
    sub8 = (GATE_ROWS, chunk)

    probs = []
    for b in range(nbatch):
        gr = gr_ref[b]
        btr = _cumsum_rows(gr, triu)
        it_rows = gr[:MLSTM_HEADS]
        bt_rows = btr[MLSTM_HEADS:2 * MLSTM_HEADS]
        u_cols = jnp.concatenate([it_rows - bt_rows, pad_rows], axis=0).T
        for h in range(MLSTM_HEADS):
            slab = slice((h // 2) * LANES, (h // 2 + 1) * LANES)
            keep = lo if h % 2 == 0 else jnp.logical_not(lo)
            q = mq_ref[b, :, slab]
            k = mk_ref[b, :, slab]
            probs.append(dict(
                b=b, h=h, idx=b * MLSTM_HEADS + h, cols=slice(h * MLSTM_DV, (h + 1) * MLSTM_DV),
                q=jnp.where(keep, q, jnp.zeros_like(q)), k=jnp.where(keep, k, jnp.zeros_like(k)),
                u_c=u_cols[:, h:h + 1], it_r=it_rows[h:h + 1, :], bt_r=bt_rows[h:h + 1, :]))

    for p in probs:
        p["m_old"] = m_ref[p["idx"]][:, 0:1]
        p["d"] = jnp.where(causal, p["bt_r"] + p["u_c"], NEG_INF)
        p["inter"] = p["bt_r"] + p["m_old"]
        p["kq"] = _dot_nt(p["k"], p["q"])
        p["q_t"] = p["q"].astype(F32).T.astype(BF16)
        p["v_t"] = mv_ref[p["b"], :, p["cols"]].astype(F32).T
    for p in probs:
        p["mt"] = jnp.maximum(jnp.max(p["d"], axis=0, keepdims=True), p["inter"])
    for p in probs:
        p["sm"] = p["kq"] * jnp.exp(p["d"] - p["mt"])
        p["wi"] = jnp.exp(p["inter"] - p["mt"])
    for p in probs:
        c_old = c_ref[p["idx"]]
        n_old = jnp.broadcast_to(n_ref[p["idx"]], (sub8[0], LANES))
        num = (_dot(p["v_t"].astype(BF16), p["sm"].astype(BF16))
               + p["wi"] * _dot(c_old.astype(BF16), p["q_t"]))
        den = (jnp.sum(p["sm"], axis=0, keepdims=True)
               + p["wi"] * _dot(n_old.astype(BF16), p["q_t"])[0:1])
        lower = jnp.maximum(jnp.abs(den), jnp.exp(-p["mt"]))
        h_t = num / lower
        p["hn"] = (h_t * lax.rsqrt(jnp.mean(h_t * h_t, axis=0, keepdims=True) + EPS)).T
    for p in probs:
        og = og_ref[p["b"], :, p["cols"]].astype(F32)
        hm_ref[p["b"], :, p["cols"]] = (
            p["hn"] * gh_ref[:, p["cols"]] * jax.nn.sigmoid(og)).astype(hm_ref.dtype)
    for p in probs:
        idx = p["idx"]
        b_last = p["bt_r"][:, chunk - 1:chunk]
        m_new = p["mt"][:, chunk - 1:chunk]
        wk = jnp.exp(b_last - p["bt_r"] + p["it_r"] - m_new)
        wc = jnp.exp(b_last + p["m_old"] - m_new)
        c_ref[idx] = wc * c_ref[idx] + _dot((p["v_t"] * wk).astype(BF16), p["k"])
        n_ref[idx] = wc * n_ref[idx] + _dot(jnp.broadcast_to(wk, sub8).astype(BF16), p["k"])[0:1]
        m_ref[idx] = jnp.broadcast_to(m_new, (1, LANES))


def _mlstm_prompt(mq, mk, mv, og, grow, g_head, nbatch, chunk):
    t = mq.shape[0]
    seq = t // nbatch
    nstate = nbatch * MLSTM_HEADS

    def seq3(a):
        return a.reshape(nbatch, seq, a.shape[-1])

    def chunks(width):
        return pl.BlockSpec((nbatch, chunk, width), lambda c: (0, c, 0))

    def state(shape):
        return pl.BlockSpec(shape, lambda c: (0, 0, 0))

    hm, c_st, n_st, m_st = pl.pallas_call(
        functools.partial(_mlstm_prompt_kernel, nbatch=nbatch, chunk=chunk),
        grid=(seq // chunk,),
        in_specs=[
            chunks(MQK_WIDTH), chunks(MQK_WIDTH), chunks(MLSTM_WIDTH), chunks(MLSTM_WIDTH),
            pl.BlockSpec((nbatch, GATE_ROWS, chunk), lambda c: (0, 0, c)),
            _const_spec((1, MLSTM_WIDTH)),
        ],
        out_specs=[
            chunks(MLSTM_WIDTH),
            state((nstate, MLSTM_DV, LANES)), state((nstate, 1, LANES)), state((nstate, 1, LANES)),
        ],
        out_shape=[
            jax.ShapeDtypeStruct((nbatch, seq, MLSTM_WIDTH), BF16),
            jax.ShapeDtypeStruct((nstate, MLSTM_DV, LANES), F32),
            jax.ShapeDtypeStruct((nstate, 1, LANES), F32),
            jax.ShapeDtypeStruct((nstate, 1, LANES), F32),
        ],
        compiler_params=pltpu.CompilerParams(
            dimension_semantics=("arbitrary",), vmem_limit_bytes=VMEM_LIMIT),
    )(seq3(mq), seq3(mk), seq3(mv), seq3(og), grow, g_head)

    c_st = c_st.reshape(nbatch, MLSTM_HEADS, MLSTM_DV, 2, MLSTM_DK)
    n_st = n_st.reshape(nbatch, MLSTM_HEADS, 2, MLSTM_DK)
    c_fin = jnp.stack([c_st[:, h, :, h % 2, :] for h in range(MLSTM_HEADS)], axis=1)
    n_fin = jnp.stack([n_st[:, h, h % 2, :] for h in range(MLSTM_HEADS)], axis=1)
    m_fin = m_st[:, 0, 0].reshape(nbatch, MLSTM_HEADS)
    return hm.reshape(t, MLSTM_WIDTH), c_fin, n_fin, m_fin


def _mlstm_sample_kernel(mq_ref, mk_ref, mkt_ref, mv_ref, og_ref, gc_ref, gr_ref, mrow_ref, c_ref,
                         n_ref, gh_ref, hm_ref, co_ref, no_ref, mt_ref, *, nb, tq):
    rows = nb * tq
    shift = tq.bit_length() - 1
    r = lax.broadcasted_iota(jnp.int32, (rows, rows), 0)
    s = lax.broadcasted_iota(jnp.int32, (rows, rows), 1)
    same = (r >> shift) == (s >> shift)
    valid = same & (s <= r)
    tril = jnp.where(valid, 1.0, 0.0).astype(BF16)
    triu = jnp.where(same & (r <= s), 1.0, 0.0).astype(BF16)
    row_batch = lax.broadcasted_iota(jnp.int32, (rows, 1), 0) >> shift
    lane = lax.broadcasted_iota(jnp.int32, (rows, LANES), 1)

    gc = gc_ref[...]
    gr = gr_ref[0]
    btc = _cumsum_cols(tril, gc)
    btr = _cumsum_rows(gr, triu)
    m_rows = mrow_ref[...]
    mt_all = jnp.zeros((rows, LANES), F32)

    def last_of_batch(col):
        parts = [jnp.broadcast_to(col[(b + 1) * tq - 1:(b + 1) * tq], (tq, 1)) for b in range(nb)]
        return jnp.concatenate(parts, axis=0)

    dk = MLSTM_DK
    lo = _low_half((rows, LANES))
    own_q = row_batch == (lax.broadcasted_iota(jnp.int32, (rows, nb * dk), 1) >> _log2(dk))
    lane_batch = lax.broadcasted_iota(jnp.int32, (dk, rows), 1) >> shift

    def state_rows(b, h):
        return slice((b * MLSTM_HEADS + h) * dk, (b * MLSTM_HEADS + h + 1) * dk)

    for h in range(MLSTM_HEADS):
        slab = slice((h // 2) * LANES, (h // 2 + 1) * LANES)
        qs, ks = mq_ref[:, slab], mk_ref[:, slab]
        qsw, ksw = pltpu.roll(qs, dk, axis=1), pltpu.roll(ks, dk, axis=1)
        q2 = jnp.where(lo, qs, qsw) if h % 2 == 0 else jnp.where(lo, qsw, qs)
        k2 = jnp.where(lo, ks, ksw) if h % 2 == 0 else jnp.where(lo, ksw, ks)
        qb, kb = q2[:, :dk].astype(BF16), k2[:, :dk].astype(BF16)
        vb = mv_ref[:, h * MLSTM_DV:(h + 1) * MLSTM_DV].astype(BF16)
        it_c = gc[:, h:h + 1]
        bt_c = btc[:, MLSTM_HEADS + h:MLSTM_HEADS + h + 1]
        it_r = gr[h:h + 1, :]
        bt_r = btr[MLSTM_HEADS + h:MLSTM_HEADS + h + 1, :]
        m_c = m_rows[:, h:h + 1]

        num, ssum, mt, wi = _mlstm_intra(qb, kb, vb, it_c, bt_c, it_r, bt_r, m_c, valid)
        qx = jnp.where(own_q, jnp.concatenate([q2] * (nb // 2), axis=1), 0.0).astype(BF16)
        c_stack = jnp.concatenate([c_ref[state_rows(b, h), :] for b in range(nb)], axis=0)
        qc = _dot(qx, c_stack.astype(BF16))
        n_rows = jnp.concatenate(
            [jnp.broadcast_to(n_ref[b * MLSTM_HEADS + h], (tq, dk)) for b in range(nb)], axis=0)
        num = num + wi * qc
        den = ssum + wi * jnp.sum(qb.astype(F32) * n_rows, axis=-1, keepdims=True)
        og = og_ref[:, h * MLSTM_DV:(h + 1) * MLSTM_DV]
        g_head = gh_ref[:, h * MLSTM_DV:(h + 1) * MLSTM_DV]
        hm_ref[:, h * MLSTM_DV:(h + 1) * MLSTM_DV] = _mlstm_head_out(num, den, mt, og, g_head)

        b_last = last_of_batch(bt_c)
        m_new = last_of_batch(mt)
        wk = jnp.exp(b_last - bt_c + it_c - m_new)
        wc = jnp.exp(b_last + m_c - m_new)
        wkv = (wk * vb.astype(F32)).astype(BF16)
        wkk = wk * kb.astype(F32)
        kt = mkt_ref[h * dk:(h + 1) * dk, :]
        kx = jnp.concatenate([jnp.where(lane_batch == b, kt, 0.0) for b in range(nb)], axis=0)
        upd = _dot(kx.astype(BF16), wkv)
        for b in range(nb):
            idx = b * MLSTM_HEADS + h
            wc_b = wc[b * tq:b * tq + 1]
            co_ref[state_rows(b, h), :] = wc_b * c_ref[state_rows(b, h), :] + upd[b * dk:(b + 1) * dk]
            no_ref[idx] = wc_b * n_ref[idx] + jnp.sum(wkk[b * tq:(b + 1) * tq], axis=0, keepdims=True)
        mt_all = jnp.where(lane == h, mt, mt_all)
    mt_ref[...] = mt_all


def _mlstm_sample(mq, mk, mkt, mv, og, gcol, grow, m_rows, ct_in, n_in, g_head, nbatch, nb):
    t = mq.shape[0]
    tq = t // nbatch
    rows = nb * tq
    assert tq & (tq - 1) == 0 and rows == LANES and nb % 2 == 0

    def tok(width):
        return pl.BlockSpec((rows, width), lambda i: (i, 0))

    ct_spec = pl.BlockSpec((nb * MLSTM_HEADS * MLSTM_DK, MLSTM_DV), lambda i: (i, 0))
    n_spec = pl.BlockSpec((nb * MLSTM_HEADS, 1, MLSTM_DK), lambda i: (i, 0, 0))
    return pl.pallas_call(
        functools.partial(_mlstm_sample_kernel, nb=nb, tq=tq),
        grid=(nbatch // nb,),
        in_specs=[
            tok(MQK_WIDTH), tok(MQK_WIDTH),
            pl.BlockSpec((MQK_WIDTH, rows), lambda i: (0, i)),
            tok(MLSTM_WIDTH), tok(MLSTM_WIDTH), tok(LANES),
            pl.BlockSpec((1, GATE_ROWS, rows), lambda i: (0, 0, i)),
            tok(LANES),
            ct_spec, n_spec,
            _const_spec((1, MLSTM_WIDTH)),
        ],
        out_specs=[tok(MLSTM_WIDTH), ct_spec, n_spec, tok(LANES)],
        out_shape=[
            jax.ShapeDtypeStruct((t, MLSTM_WIDTH), F32),
            jax.ShapeDtypeStruct(ct_in.shape, F32),
            jax.ShapeDtypeStruct(n_in.shape, F32),
            jax.ShapeDtypeStruct((t, LANES), F32),
        ],
        compiler_params=pltpu.CompilerParams(
            dimension_semantics=("arbitrary",), vmem_limit_bytes=VMEM_LIMIT),
    )(mq, mk, mkt, mv, og, gcol, grow, m_rows, ct_in, n_in, g_head)


def _memkv_kernel(mem_ref, g_ref, wk_ref, wv_ref, k_ref, v_ref):
    mn = _rmsnorm(mem_ref[...], g_ref[...]).astype(BF16)
    k_ref[...] = _dot(mn, wk_ref[...])
    v_ref[...] = _dot(mn, wv_ref[...])


def _memkv(mem, w, nbatch):
    t, d = mem.shape
    m = t // nbatch
    spec = pl.BlockSpec((m, CROSS_WIDTH), lambda b: (b, 0))
    return pl.pallas_call(
        _memkv_kernel,
        grid=(nbatch,),
        in_specs=[
            pl.BlockSpec((m, d), lambda b: (b, 0)),
            _const_spec((1, d)), _const_spec((d, CROSS_WIDTH)), _const_spec((d, CROSS_WIDTH)),
        ],
        out_specs=[spec, spec],
        out_shape=[jax.ShapeDtypeStruct((t, CROSS_WIDTH), F32)] * 2,
        compiler_params=pltpu.CompilerParams(
            dimension_semantics=("arbitrary",), vmem_limit_bytes=VMEM_LIMIT),
    )(mem, w["g_mem"], w["w_ck"], w["w_cv"])


def _mix_stage(x, att, hm, wo_ref, gcr_ref, wcq_ref):
    cat = jnp.concatenate([att.astype(BF16), hm.astype(BF16)], axis=1)
    y1 = x + _dot(cat, wo_ref[...])
    qc = _dot(_rmsnorm(y1, gcr_ref[...]).astype(BF16), wcq_ref[...])
    return y1, qc


def _cross_attention(qc, mk, mv):
    slabs = []
    for j in range(CROSS_WIDTH // LANES):
        qs = qc[:, j * LANES:(j + 1) * LANES]
        ks = mk[:, j * LANES:(j + 1) * LANES]
        vs = mv[:, j * LANES:(j + 1) * LANES]
        n = qs.shape[0]
        lo = _low_half(qs.shape)
        zero = jnp.zeros_like(qs)
        q2 = jnp.concatenate([jnp.where(lo, qs, zero), jnp.where(lo, zero, qs)], axis=0)
        s = _dot_nt(q2 * (HEAD_DIM ** -0.5), ks)
        p = jnp.exp(s - jnp.max(s, axis=-1, keepdims=True))
        o2 = _dot(p.astype(BF16), vs) / jnp.sum(p, axis=-1, keepdims=True)
        slabs.append(jnp.where(lo, o2[:n], o2[n:]))
    return jnp.concatenate(slabs, axis=1)


FFN_CHUNK = 256


def _interleave_gate_up(w_gate, w_up):
    parts = []
    for c in range(0, w_gate.shape[1], FFN_CHUNK):
        parts += [w_gate[:, c:c + FFN_CHUNK], w_up[:, c:c + FFN_CHUNK]]
    return jnp.concatenate(parts, axis=1)


def _ffn_stage(y1, o, wco_ref, gf_ref, wgu_ref, wd_ref, act_ref, gfin_ref, final, rows=slice(None)):
    y2 = y1 + _dot(o.astype(BF16), wco_ref[...])
    xn = _rmsnorm(y2, gf_ref[...]).astype(BF16)
    for c in range(wd_ref.shape[0] // FFN_CHUNK):
        gu = _dot(xn, wgu_ref[:, 2 * c * FFN_CHUNK:2 * (c + 1) * FFN_CHUNK])
        g, u = gu[:, :FFN_CHUNK], gu[:, FFN_CHUNK:]
        act_ref[rows, c * FFN_CHUNK:(c + 1) * FFN_CHUNK] = (g * jax.nn.sigmoid(g) * u).astype(BF16)
    y3 = y2 + _dot(act_ref[rows, :], wd_ref[...])
    return _rmsnorm(y3, gfin_ref[...]) if final else y3


def _post_prompt_kernel(x_ref, att_ref, hm_ref, mk_ref, mv_ref, wo_ref, gcr_ref, wcq_ref, wco_ref,
                        gf_ref, wgu_ref, wd_ref, gfin_ref, o_ref, act_ref, *, final):
    y1, qc = _mix_stage(x_ref[...], att_ref[...], hm_ref[...], wo_ref, gcr_ref, wcq_ref)
    o = _cross_attention(qc.astype(BF16), mk_ref[...].astype(BF16), mv_ref[...].astype(BF16))
    o_ref[...] = _ffn_stage(y1, o, wco_ref, gf_ref, wgu_ref, wd_ref, act_ref, gfin_ref, final)


_MIX_WEIGHTS = ("w_out", "g_cross", "w_cq")
_FFN_WEIGHTS = ("w_co", "g_ffn", "w_gate_up", "w_down", "g_final")


def _post_prompt(x, att, hm, memk, memv, w, nbatch, final, tm):
    t, d = x.shape
    tpb = t // nbatch // tm
    m = memk.shape[0] // nbatch
    hidden = w["w_down"].shape[0]

    def rows(width):
        return pl.BlockSpec((tm, width), lambda i: (i, 0))

    mem_spec = pl.BlockSpec((m, CROSS_WIDTH), lambda i: (i // tpb, 0))
    names = _MIX_WEIGHTS + _FFN_WEIGHTS
    return pl.pallas_call(
        functools.partial(_post_prompt_kernel, final=final),
        grid=(t // tm,),
        in_specs=[rows(d), rows(ATT_WIDTH), rows(MLSTM_WIDTH), mem_spec, mem_spec]
        + _weight_specs(w, names),
        out_specs=rows(d),
        out_shape=jax.ShapeDtypeStruct((t, d), F32),
        scratch_shapes=[pltpu.VMEM((tm, hidden), BF16)],
        compiler_params=pltpu.CompilerParams(
            dimension_semantics=("arbitrary",), vmem_limit_bytes=VMEM_LIMIT),
    )(x, att, hm, memk, memv, *[w[n] for n in names])


def _mix_kernel(x_ref, att_ref, hm_ref, wo_ref, gcr_ref, wcq_ref, y_ref, qc_ref):
    y1, qc = _mix_stage(x_ref[...], att_ref[...], hm_ref[...], wo_ref, gcr_ref, wcq_ref)
    y_ref[...] = y1
    qc_ref[...] = qc


def _mix(x, att, hm, w, tm):
    t, d = x.shape

    def rows(width):
        return pl.BlockSpec((tm, width), lambda i: (i, 0))

    return pl.pallas_call(
        _mix_kernel,
        grid=(t // tm,),
        in_specs=[rows(d), rows(ATT_WIDTH), rows(MLSTM_WIDTH)] + _weight_specs(w, _MIX_WEIGHTS),
        out_specs=[rows(d), rows(CROSS_WIDTH)],
        out_shape=[jax.ShapeDtypeStruct((t, d), F32), jax.ShapeDtypeStruct((t, CROSS_WIDTH), F32)],
        compiler_params=pltpu.CompilerParams(
            dimension_semantics=("arbitrary",), vmem_limit_bytes=VMEM_LIMIT),
    )(x, att, hm, *[w[n] for n in _MIX_WEIGHTS])


def _cross_sample_kernel(qc_ref, mk_ref, mv_ref, o_ref, *, nb, tq):
    mem = CROSS_WIDTH
    n = CROSS_HEADS * tq
    row_head = lax.broadcasted_iota(jnp.int32, (n, CROSS_WIDTH), 0) >> _log2(tq)
    lane_head = lax.broadcasted_iota(jnp.int32, (n, CROSS_WIDTH), 1) >> _log2(HEAD_DIM)
    own = row_head == lane_head
    for b in range(nb):
        q = qc_ref[b * tq:(b + 1) * tq, :]
        qx = jnp.where(own, jnp.concatenate([q] * CROSS_HEADS, axis=0), 0.0).astype(BF16)
        mk = mk_ref[b * mem:(b + 1) * mem, :].astype(BF16)
        mv = mv_ref[b * mem:(b + 1) * mem, :].astype(BF16)
        s = _dot(qx, mk) * (HEAD_DIM ** -0.5)
        p = jnp.exp(s - jnp.max(s, axis=-1, keepdims=True))
        pv = _dot_nt(p.astype(BF16), mv) / jnp.sum(p, axis=-1, keepdims=True)
        pv = jnp.where(own, pv, 0.0)
        out = pv[0:tq]
        for h in range(1, CROSS_HEADS):
            out = out + pv[h * tq:(h + 1) * tq]
        o_ref[b * tq:(b + 1) * tq, :] = out


def _cross_sample(qc, memk_t, memv_t, nbatch, nb):
    t = qc.shape[0]
    tq = t // nbatch
    mem = memk_t.shape[1]
    tok = pl.BlockSpec((nb * tq, CROSS_WIDTH), lambda i: (i, 0))
    mspec = pl.BlockSpec((nb * CROSS_WIDTH, mem), lambda i: (i, 0))
    return pl.pallas_call(
        functools.partial(_cross_sample_kernel, nb=nb, tq=tq),
        grid=(nbatch // nb,),
        in_specs=[tok, mspec, mspec],
        out_specs=tok,
        out_shape=jax.ShapeDtypeStruct((t, CROSS_WIDTH), F32),
        compiler_params=pltpu.CompilerParams(
            dimension_semantics=("arbitrary",), vmem_limit_bytes=VMEM_LIMIT),
    )(qc, memk_t, memv_t)


def _ffn_kernel(y_ref, o_ref, wco_ref, gf_ref, wgu_ref, wd_ref, gfin_ref, out_ref, act_ref, *, final):
    out_ref[...] = _ffn_stage(y_ref[...], o_ref[...], wco_ref, gf_ref, wgu_ref, wd_ref, act_ref,
                              gfin_ref, final)


def _ffn(y1, o, w, final, tm):
    t, d = y1.shape
    hidden = w["w_down"].shape[0]

    def rows(width):
        return pl.BlockSpec((tm, width), lambda i: (i, 0))

    return pl.pallas_call(
        functools.partial(_ffn_kernel, final=final),
        grid=(t // tm,),
        in_specs=[rows(d), rows(CROSS_WIDTH)] + _weight_specs(w, _FFN_WEIGHTS),
        out_specs=rows(d),
        out_shape=jax.ShapeDtypeStruct((t, d), F32),
        scratch_shapes=[pltpu.VMEM((tm, hidden), BF16)],
        compiler_params=pltpu.CompilerParams(
            dimension_semantics=("arbitrary",), vmem_limit_bytes=VMEM_LIMIT),
    )(y1, o, *[w[n] for n in _FFN_WEIGHTS])


def _layer_weights(l, w_in, b_igate, b_fgate, g_mlstm_head, w_out, g_mix, g_cross, g_mem, w_cq,
                   w_ck, w_cv, w_co, g_ffn, w_gate, w_up, w_down, g_final):
    d = w_in.shape[1]
    gate_w = w_in[l][:, MAIN_WIDTH:]
    gate_b = jnp.concatenate([b_igate[l], b_fgate[l]]).astype(F32)
    ngate = gate_b.shape[0]
    row = lambda a: a.astype(F32).reshape(1, -1)
    w_t = w_in[l].T
    k0 = ATT_WIDTH
    mk0 = ATT_WIDTH + 2 * KV_WIDTH + MQK_WIDTH
    return {
        "w_kvk_t": jnp.concatenate([w_t[k0:k0 + 2 * KV_WIDTH], w_t[mk0:mk0 + MQK_WIDTH]]).astype(BF16),
        "w_main": w_in[l][:, :MAIN_WIDTH].astype(BF16),
        "w_gate_cols": jnp.pad(gate_w, ((0, 0), (0, LANES - ngate))).astype(BF16),
        "w_gate_rows": jnp.pad(gate_w.T, ((0, GATE_ROWS - ngate), (0, 0))).astype(BF16),
        "b_gate_cols": jnp.pad(gate_b, (0, LANES - ngate)).reshape(1, LANES),
        "b_gate_rows": jnp.pad(gate_b, (0, GATE_ROWS - ngate)).reshape(GATE_ROWS, 1),
        "g_mix": row(g_mix[l]), "g_cross": row(g_cross[l]), "g_mem": row(g_mem[l]),
        "g_ffn": row(g_ffn[l]), "g_final": row(g_final), "g_head": row(g_mlstm_head[l]),
        "w_out": w_out[l].astype(BF16), "w_cq": w_cq[l].astype(BF16), "w_ck": w_ck[l].astype(BF16),
        "w_cv": w_cv[l].astype(BF16), "w_co": w_co[l].astype(BF16),
        "w_gate_up": _interleave_gate_up(w_gate[l], w_up[l]).astype(BF16),
        "w_down": w_down[l].astype(BF16),
    }


TOKEN_TILE = 512
PROJ_TILE = 1024
SWA_QUERY_BLOCK = 512
MLSTM_CHUNK_ROWS = 256
SWA_SAMPLE_BATCH = 16
MLSTM_SAMPLE_BATCH = 16
CROSS_SAMPLE_BATCH = 8


def kernel(x_prompt, x_sample, mem_prompt, cache_swa_k, cache_swa_v, state_mlstm_C, state_mlstm_n,
           state_mlstm_m, cache_mem_k, cache_mem_v, w_in, b_igate, b_fgate, attn_sinks,
           g_mlstm_head, w_out, g_mix, g_cross, g_mem, w_cq, w_ck, w_cv, w_co, g_ffn, w_gate,
           w_up, w_down, g_final):
    depth = w_in.shape[0]
    bp, sp, d = x_prompt.shape
    bs, ss, _ = x_sample.shape
    mem_tokens = mem_prompt.shape[1]
    past = cache_swa_k.shape[2]
    yp = x_prompt.reshape(bp * sp, d)
    ys = x_sample.reshape(bs * ss, d)
    mem = mem_prompt.reshape(bp * mem_tokens, d)
    outs = [[] for _ in range(12)]

    for l in range(depth):
        final = l == depth - 1
        w = _layer_weights(l, w_in, b_igate, b_fgate, g_mlstm_head, w_out, g_mix, g_cross, g_mem,
                           w_cq, w_ck, w_cv, w_co, g_ffn, w_gate, w_up, w_down, g_final)
        sinks = attn_sinks[l].astype(F32)

        q, k, v, mq, mk, mv, og, grow = _proj(yp, w, bp, BF16, PROJ_TILE)
        att = _swa_prompt(q, k, v, sinks, bp, SWA_QUERY_BLOCK)
        hm, c_p, n_p, m_p = _mlstm_prompt(mq, mk, mv, og, grow, w["g_head"], bp, MLSTM_CHUNK_ROWS)
        memk, memv = _memkv(mem, w, bp)
        yp = _post_prompt(yp, att, hm, memk, memv, w, bp, final, TOKEN_TILE)
        win_shape = (bp, WINDOW, ATT_KV_HEADS, HEAD_DIM)
        outs[0].append(k.reshape(bp, sp, KV_WIDTH)[:, sp - WINDOW:].reshape(win_shape))
        outs[1].append(v.reshape(bp, sp, KV_WIDTH)[:, sp - WINDOW:].reshape(win_shape))
        outs[2].append(c_p)
        outs[3].append(n_p)
        outs[4].append(m_p)
        mem_shape = (bp, mem_tokens, CROSS_HEADS, HEAD_DIM)
        outs[5].append(memk.reshape(mem_shape))
        outs[6].append(memv.reshape(mem_shape))

        def tokens_last(a):
            return jnp.transpose(a.astype(F32), (0, 2, 3, 1)).reshape(-1, a.shape[1])

        q, _, _, mq, mk, mv, og, grow, gcol, knt, vnt, mkt = _proj(
            ys, w, 1, F32, TOKEN_TILE, decode=True)
        att, kbuf_t, vbuf_t = _swa_sample(
            q, knt, vnt, tokens_last(cache_swa_k[l]), tokens_last(cache_swa_v[l]), sinks, bs,
            SWA_SAMPLE_BATCH)
        m_rows = jnp.pad(jnp.repeat(state_mlstm_m[l].astype(F32), ss, axis=0),
                         ((0, 0), (0, LANES - MLSTM_HEADS)))
        ct_in = jnp.swapaxes(state_mlstm_C[l].astype(F32), 2, 3).reshape(-1, MLSTM_DV)
        hm, ct_s, n_s, mt = _mlstm_sample(
            mq, mk, mkt, mv, og, gcol, grow, m_rows, ct_in,
            state_mlstm_n[l].astype(F32).reshape(bs * MLSTM_HEADS, 1, MLSTM_DK),
            w["g_head"], bs, MLSTM_SAMPLE_BATCH)
        y1, qc = _mix(ys, att, hm, w, TOKEN_TILE)
        o = _cross_sample(qc, tokens_last(cache_mem_k[l]), tokens_last(cache_mem_v[l]), bs,
                          CROSS_SAMPLE_BATCH)
        ys = _ffn(y1, o, w, final, TOKEN_TILE)
        buf_t_shape = (bs, ATT_KV_HEADS, HEAD_DIM, past)
        outs[7].append(jnp.transpose(kbuf_t.reshape(buf_t_shape), (0, 3, 1, 2)))
        outs[8].append(jnp.transpose(vbuf_t.reshape(buf_t_shape), (0, 3, 1, 2)))
        outs[9].append(jnp.swapaxes(ct_s.reshape(bs, MLSTM_HEADS, MLSTM_DK, MLSTM_DV), 2, 3))
        outs[10].append(n_s.reshape(bs, MLSTM_HEADS, MLSTM_DK))
        outs[11].append(mt.reshape(bs, ss, LANES)[:, ss - 1, :MLSTM_HEADS])

    return (yp.reshape(bp, sp, d), ys.reshape(bs, ss, d)) + tuple(jnp.stack(o) for o in outs)
```

```python
import functools

import jax
import jax.numpy as jnp
from jax import lax
from jax.experimental import pallas as pl
from jax.experimental.pallas import tpu as pltpu

F32 = jnp.float32
BF16 = jnp.bfloat16

HEAD_DIM = 64
ATT_GROUP = 4
ATT_KV_HEADS = 2
ATT_WIDTH = 512
KV_WIDTH = ATT_KV_HEADS * HEAD_DIM
WINDOW = 128
LOG_WINDOW = 7
MLSTM_HEADS = 4
MLSTM_DV = 128
MLSTM_DK = 64
MQK_WIDTH = MLSTM_HEADS * MLSTM_DK
MLSTM_WIDTH = MLSTM_HEADS * MLSTM_DV
MAIN_WIDTH = ATT_WIDTH + 2 * KV_WIDTH + 2 * MQK_WIDTH + 2 * MLSTM_WIDTH
CROSS_HEADS = 4
CROSS_WIDTH = CROSS_HEADS * HEAD_DIM
EPS = 1e-6
NEG_INF = float("-inf")
assert HEAD_DIM == 4 ** 3

LANES = 128
GATE_ROWS = 16
VMEM_LIMIT = 56 * 1024 * 1024

NT = (((1,), (1,)), ((), ()))
TN = (((0,), (0,)), ((), ()))


def _dot(a, b):
    return jnp.dot(a, b, preferred_element_type=F32)


def _dot_nt(a, b):
    return lax.dot_general(a, b, NT, preferred_element_type=F32)


def _dot_tn(a, b):
    return lax.dot_general(a, b, TN, preferred_element_type=F32)


def _rmsnorm(x, g):
    return x * lax.rsqrt(jnp.mean(x * x, axis=-1, keepdims=True) + EPS) * g


def _log_sigmoid(x):
    return jnp.minimum(x, 0.0) - jnp.log1p(jnp.exp(-jnp.abs(x)))


def _split3(x):
    hi = x.astype(BF16)
    r1 = x - hi.astype(F32)
    mid = r1.astype(BF16)
    lo = (r1 - mid.astype(F32)).astype(BF16)
    return hi, mid, lo


def _cumsum_cols(tri, x):
    hi, mid, lo = _split3(x)
    return _dot(tri, hi) + _dot(tri, mid) + _dot(tri, lo)


def _cumsum_rows(x, tri):
    hi, mid, lo = _split3(x)
    return _dot(hi, tri) + _dot(mid, tri) + _dot(lo, tri)


def _log2(n):
    assert n > 0 and n & (n - 1) == 0, n
    return n.bit_length() - 1


def _low_half(shape):
    return lax.broadcasted_iota(jnp.int32, shape, 1) < HEAD_DIM


def _proj_kernel(x_ref, g_ref, w_ref, wgr_ref, br_ref, *rest, decode):
    if decode:
        wgc_ref, bc_ref, wt_ref = rest[:3]
        rest = rest[3:]
    q_ref, k_ref, v_ref, mq_ref, mk_ref, mv_ref, og_ref, gr_ref = rest[:8]
    xn = _rmsnorm(x_ref[...], g_ref[...]).astype(BF16)
    if decode:
        gc_ref, kt_ref, vt_ref, mkt_ref = rest[8:]
        zt = _dot_nt(wt_ref[...], xn)
        kt_ref[...] = zt[:KV_WIDTH]
        vt_ref[...] = zt[KV_WIDTH:2 * KV_WIDTH]
        mkt_ref[...] = zt[2 * KV_WIDTH:] * (MLSTM_DK ** -0.5)
        zc = _dot(xn, wgc_ref[...]) + bc_ref[...]
        lane = lax.broadcasted_iota(jnp.int32, zc.shape, 1)
        gc_ref[...] = jnp.where(lane < MLSTM_HEADS, zc, _log_sigmoid(zc))

    def mm(lo, hi):
        return _dot(xn, w_ref[:, lo:hi])

    o = 0
    q_ref[...] = mm(o, o + ATT_WIDTH).astype(q_ref.dtype)
    o += ATT_WIDTH
    kv = mm(o, o + 2 * KV_WIDTH)
    k_ref[...] = kv[:, :KV_WIDTH]
    v_ref[...] = kv[:, KV_WIDTH:]
    o += 2 * KV_WIDTH
    mqk = mm(o, o + 2 * MQK_WIDTH)
    mq_ref[...] = mqk[:, :MQK_WIDTH].astype(mq_ref.dtype)
    mk_ref[...] = (mqk[:, MQK_WIDTH:] * (MLSTM_DK ** -0.5)).astype(mk_ref.dtype)
    o += 2 * MQK_WIDTH
    mv_ref[...] = mm(o, o + MLSTM_WIDTH).astype(mv_ref.dtype)
    o += MLSTM_WIDTH
    og_ref[...] = mm(o, o + MLSTM_WIDTH).astype(og_ref.dtype)

    zr = _dot_nt(wgr_ref[...], xn) + br_ref[...]
    row = lax.broadcasted_iota(jnp.int32, zr.shape, 0)
    gr_ref[0] = jnp.where(row < MLSTM_HEADS, zr, _log_sigmoid(zr))


def _const_spec(shape):
    nd = len(shape)
    return pl.BlockSpec(shape, lambda *_: (0,) * nd, pipeline_mode=pl.Buffered(1))


def _weight_specs(w, names):
    return [_const_spec(w[n].shape) for n in names]


def _proj(x, w, nseg, act_dtype, tm, decode=False):
    t, d = x.shape
    seg = t // nseg
    tpb = seg // tm
    grid = (t // tm,)

    def rows(width):
        return pl.BlockSpec((tm, width), lambda i: (i, 0))

    out_shape = [
        jax.ShapeDtypeStruct((t, ATT_WIDTH), act_dtype),
        jax.ShapeDtypeStruct((t, KV_WIDTH), F32),
        jax.ShapeDtypeStruct((t, KV_WIDTH), F32),
        jax.ShapeDtypeStruct((t, MQK_WIDTH), act_dtype),
        jax.ShapeDtypeStruct((t, MQK_WIDTH), act_dtype),
        jax.ShapeDtypeStruct((t, MLSTM_WIDTH), act_dtype),
        jax.ShapeDtypeStruct((t, MLSTM_WIDTH), act_dtype),
        jax.ShapeDtypeStruct((nseg, GATE_ROWS, seg), F32),
    ]
    out_specs = [
        rows(ATT_WIDTH), rows(KV_WIDTH), rows(KV_WIDTH), rows(MQK_WIDTH), rows(MQK_WIDTH),
        rows(MLSTM_WIDTH), rows(MLSTM_WIDTH),
        pl.BlockSpec((1, GATE_ROWS, tm), lambda i: (i // tpb, 0, i % tpb)),
    ]
    names = ["g_mix", "w_main", "w_gate_rows", "b_gate_rows"]
    if decode:
        names += ["w_gate_cols", "b_gate_cols", "w_kvk_t"]
        out_shape.append(jax.ShapeDtypeStruct((t, LANES), F32))
        out_specs.append(rows(LANES))
        for width in (KV_WIDTH, KV_WIDTH, MQK_WIDTH):
            out_shape.append(jax.ShapeDtypeStruct((width, t), F32))
            out_specs.append(pl.BlockSpec((width, tm), lambda i: (0, i)))
    in_specs = [rows(d)] + _weight_specs(w, names)
    return pl.pallas_call(
        functools.partial(_proj_kernel, decode=decode),
        grid=grid,
        in_specs=in_specs,
        out_specs=out_specs,
        out_shape=out_shape,
        compiler_params=pltpu.CompilerParams(
            dimension_semantics=("arbitrary",), vmem_limit_bytes=VMEM_LIMIT),
    )(x, *[w[n] for n in names])


def _dup_heads(x):
    swapped = pltpu.roll(x, HEAD_DIM, axis=1)
    lo = _low_half(x.shape)
    return (jnp.where(lo, x, swapped).astype(BF16), jnp.where(lo, swapped, x).astype(BF16))


def _stack_query_heads(q, h):
    parts = []
    for g in range(ATT_GROUP):
        hq = h * ATT_GROUP + g
        slab = q[:, (hq // 2) * LANES:(hq // 2 + 1) * LANES]
        lo = _low_half(slab.shape)
        keep = lo if hq % 2 == 0 else jnp.logical_not(lo)
        parts.append(jnp.where(keep, slab, jnp.zeros_like(slab)))
    return jnp.concatenate(parts, axis=0).astype(BF16)


def _sink_column(sink_ref, h, rows_per_head):
    n = ATT_GROUP * rows_per_head
    grp = lax.broadcasted_iota(jnp.int32, (n, 1), 0) >> _log2(rows_per_head)
    col = jnp.full((n, 1), sink_ref[h * ATT_GROUP], F32)
    for g in range(1, ATT_GROUP):
        col = jnp.where(grp == g, sink_ref[h * ATT_GROUP + g], col)
    return col


def _swa_prompt_kernel(sink_ref, q_ref, kc_ref, kp_ref, vc_ref, vp_ref, o_ref, *, qb):
    i = pl.program_id(1)
    k_all = jnp.concatenate([kp_ref[...], kc_ref[...]], axis=0)
    v_all = jnp.concatenate([vp_ref[...], vc_ref[...]], axis=0)
    kd = _dup_heads(k_all)
    v_t = v_all.T.astype(BF16)

    n = ATT_GROUP * WINDOW
    key = lax.broadcasted_iota(jnp.int32, (2 * WINDOW, n), 0)
    qry = lax.broadcasted_iota(jnp.int32, (2 * WINDOW, n), 1) & (WINDOW - 1)
    delta = key - qry
    in_band = (delta >= 0) & (delta <= WINDOW)
    grp = lax.broadcasted_iota(jnp.int32, (1, n), 1) >> LOG_WINDOW
    sinks = []
    for h in range(ATT_KV_HEADS):
        row = jnp.full((1, n), sink_ref[h * ATT_GROUP], F32)
        for g in range(1, ATT_GROUP):
            row = jnp.where(grp == g, sink_ref[h * ATT_GROUP + g], row)
        sinks.append(row)

    probs = []
    for j in range(qb // WINDOW):
        valid = in_band & (key >= jnp.where(i == 0, WINDOW, 0)) if j == 0 else in_band
        q = q_ref[j * WINDOW:(j + 1) * WINDOW, :] * (HEAD_DIM ** -0.5)
        band = slice(j * WINDOW, (j + 2) * WINDOW)
        for h in range(ATT_KV_HEADS):
            qh = _stack_query_heads(q, h)
            s = _dot_nt(kd[h][band], qh)
            probs.append(dict(h=h, band=band, s=jnp.where(valid, s, NEG_INF)))
    for p in probs:
        p["m"] = jnp.maximum(jnp.max(p["s"], axis=0, keepdims=True), sinks[p["h"]])
    for p in probs:
        p["p"] = jnp.exp(p["s"] - p["m"])
    for p in probs:
        h = p["h"]
        denom = jnp.sum(p["p"], axis=0, keepdims=True) + jnp.exp(sinks[h] - p["m"])
        p["o_t"] = _dot(v_t[h * HEAD_DIM:(h + 1) * HEAD_DIM, p["band"]], p["p"].astype(BF16)) / denom
    for j in range(qb // WINDOW):
        pieces = [p["o_t"][:, g * WINDOW:(g + 1) * WINDOW]
                  for p in probs[j * ATT_KV_HEADS:(j + 1) * ATT_KV_HEADS] for g in range(ATT_GROUP)]
        o_ref[j * WINDOW:(j + 1) * WINDOW, :] = jnp.concatenate(pieces, axis=0).T.astype(o_ref.dtype)


def _swa_prompt(q, k, v, sinks, nbatch, qb):
    t = q.shape[0]
    seq = t // nbatch
    nq = seq // qb
    wpq = qb // WINDOW
    wps = seq // WINDOW

    def cur(width):
        return pl.BlockSpec((qb, width), lambda b, i: (b * nq + i, 0))

    def prev(width):
        return pl.BlockSpec(
            (WINDOW, width), lambda b, i: (b * wps + jnp.maximum(i * wpq - 1, 0), 0))

    return pl.pallas_call(
        functools.partial(_swa_prompt_kernel, qb=qb),
        grid=(nbatch, nq),
        in_specs=[
            pl.BlockSpec(memory_space=pltpu.SMEM),
            cur(ATT_WIDTH), cur(KV_WIDTH), prev(KV_WIDTH), cur(KV_WIDTH), prev(KV_WIDTH),
        ],
        out_specs=cur(ATT_WIDTH),
        out_shape=jax.ShapeDtypeStruct((t, ATT_WIDTH), BF16),
        compiler_params=pltpu.CompilerParams(
            dimension_semantics=("arbitrary", "arbitrary"), vmem_limit_bytes=VMEM_LIMIT),
    )(sinks, q, k, k, v, v)


def _swa_sample_kernel(sink_ref, q_ref, knt_ref, vnt_ref, ck_ref, cv_ref,
                       o_ref, kbuf_ref, vbuf_ref, *, nb, tq):
    past = ck_ref.shape[1]
    hw = ATT_KV_HEADS * HEAD_DIM
    heads = ATT_KV_HEADS * ATT_GROUP
    n = heads * tq
    fresh = past - tq
    t = lax.broadcasted_iota(jnp.int32, (n, 2 * past), 0) & (tq - 1)
    col = lax.broadcasted_iota(jnp.int32, (n, 2 * past), 1)
    valid = ((col < past) & (col >= t)) | ((col >= past + fresh) & (col - (past + fresh) <= t))
    sink = jnp.concatenate([_sink_column(sink_ref, h, tq) for h in range(ATT_KV_HEADS)], axis=0)
    lane = lax.broadcasted_iota(jnp.int32, (hw, past), 1)
    knt = knt_ref[...]
    vnt = vnt_ref[...]
    zero_slab = jnp.zeros((tq, LANES), F32)
    lo = _low_half((tq, LANES))

    def shifted(old, new_t, b):
        return jnp.where(lane >= fresh, pltpu.roll(new_t, (fresh - b * tq) % past, axis=1),
                         pltpu.roll(old, fresh, axis=1))

    def dup_rows(x):
        h0, h1 = x[:HEAD_DIM], x[HEAD_DIM:]
        return jnp.concatenate([h0, h0, h1, h1], axis=0)

    for b in range(nb):
        k_old = ck_ref[b * hw:(b + 1) * hw, :]
        v_old = cv_ref[b * hw:(b + 1) * hw, :]
        k_new = shifted(k_old, knt, b)
        v_new = shifted(v_old, vnt, b)
        kbuf_ref[b * hw:(b + 1) * hw, :] = k_new
        vbuf_ref[b * hw:(b + 1) * hw, :] = v_new
        kop = jnp.concatenate([dup_rows(k_old), dup_rows(k_new)], axis=1).astype(BF16)
        vop = jnp.concatenate([dup_rows(v_old), dup_rows(v_new)], axis=1).astype(BF16)

        q = q_ref[b * tq:(b + 1) * tq, :]
        blocks = []
        for hq in range(heads):
            slab = q[:, (hq // 2) * LANES:(hq // 2 + 1) * LANES]
            slab = jnp.where(lo if hq % 2 == 0 else jnp.logical_not(lo), slab, 0.0)
            pair = [slab, zero_slab] if hq // ATT_GROUP == 0 else [zero_slab, slab]
            blocks.append(jnp.concatenate(pair, axis=1))
        qx = jnp.concatenate(blocks, axis=0).astype(BF16)

        s = _dot(qx, kop) * (HEAD_DIM ** -0.5)
        s = jnp.where(valid, s, NEG_INF)
        m = jnp.maximum(jnp.max(s, axis=-1, keepdims=True), sink)
        p = jnp.exp(s - m)
        denom = jnp.sum(p, axis=-1, keepdims=True) + jnp.exp(sink - m)
        o = _dot_nt(p.astype(BF16), vop) / denom

        slabs = []
        for j in range(heads // 2):
            h = (2 * j) // ATT_GROUP
            even = o[(2 * j) * tq:(2 * j + 1) * tq, h * LANES:(h + 1) * LANES]
            odd = o[(2 * j + 1) * tq:(2 * j + 2) * tq, h * LANES:(h + 1) * LANES]
            slabs.append(jnp.where(lo, even, odd))
        o_ref[b * tq:(b + 1) * tq, :] = jnp.concatenate(slabs, axis=1)


def _swa_sample(q, knt, vnt, cache_kt, cache_vt, sinks, nbatch, nb):
    t = q.shape[0]
    tq = t // nbatch
    past = cache_kt.shape[1]
    hw = ATT_KV_HEADS * HEAD_DIM
    assert past == WINDOW and tq & (tq - 1) == 0 and nb * tq == past

    def rows(r, width):
        return pl.BlockSpec((r, width), lambda i: (i, 0))

    new_t = pl.BlockSpec((hw, nb * tq), lambda i: (0, i))
    return pl.pallas_call(
        functools.partial(_swa_sample_kernel, nb=nb, tq=tq),
        grid=(nbatch // nb,),
        in_specs=[
            pl.BlockSpec(memory_space=pltpu.SMEM),
            rows(nb * tq, ATT_WIDTH), new_t, new_t, rows(nb * hw, past), rows(nb * hw, past),
        ],
        out_specs=[rows(nb * tq, ATT_WIDTH), rows(nb * hw, past), rows(nb * hw, past)],
        out_shape=[
            jax.ShapeDtypeStruct((t, ATT_WIDTH), F32),
            jax.ShapeDtypeStruct(cache_kt.shape, F32),
            jax.ShapeDtypeStruct(cache_vt.shape, F32),
        ],
        compiler_params=pltpu.CompilerParams(
            dimension_semantics=("arbitrary",), vmem_limit_bytes=VMEM_LIMIT),
    )(sinks, q, knt, vnt, cache_kt, cache_vt)


def _mlstm_intra(q, k, v, it_c, bt_c, it_r, bt_r, m_c, valid):
    d = jnp.where(valid, bt_c - bt_r + it_r, NEG_INF)
    inter = bt_c + m_c
    mt = jnp.maximum(jnp.max(d, axis=-1, keepdims=True), inter)
    sm = _dot_nt(q, k) * jnp.exp(d - mt)
    wi = jnp.exp(inter - mt)
    return _dot(sm.astype(BF16), v), jnp.sum(sm, axis=-1, keepdims=True), mt, wi


def _mlstm_head_out(num, den, mt, og, g_head):
    lower = jnp.maximum(jnp.abs(den), jnp.exp(-mt))
    h = num / lower
    hn = h * lax.rsqrt(jnp.mean(h * h, axis=-1, keepdims=True) + EPS)
    return hn * g_head * jax.nn.sigmoid(og)


def _mlstm_prompt_kernel(mq_ref, mk_ref, mv_ref, og_ref, gr_ref, gh_ref,
                         hm_ref, c_ref, n_ref, m_ref, *, nbatch, chunk):
    @pl.when(pl.program_id(0) == 0)
    def _():
        c_ref[...] = jnp.zeros_like(c_ref)
        n_ref[...] = jnp.zeros_like(n_ref)
        m_ref[...] = jnp.zeros_like(m_ref)

    s_idx = lax.broadcasted_iota(jnp.int32, (chunk, chunk), 0)
    l_idx = lax.broadcasted_iota(jnp.int32, (chunk, chunk), 1)
    causal = s_idx <= l_idx
    triu = jnp.where(causal, 1.0, 0.0).astype(BF16)
    lo = _low_half((chunk, LANES))
    pad_rows = jnp.zeros((LANES - MLSTM_HEADS, chunk), F32)
    sub8 = (GATE_ROWS, chunk)

    probs = []
    for b in range(nbatch):
        gr = gr_ref[b]
        btr = _cumsum_rows(gr, triu)
        it_rows = gr[:MLSTM_HEADS]
        bt_rows = btr[MLSTM_HEADS:2 * MLSTM_HEADS]
        u_cols = jnp.concatenate([it_rows - bt_rows, pad_rows], axis=0).T
        for h in range(MLSTM_HEADS):
            slab = slice((h // 2) * LANES, (h // 2 + 1) * LANES)
            keep = lo if h % 2 == 0 else jnp.logical_not(lo)
            q = mq_ref[b, :, slab]
            k = mk_ref[b, :, slab]
            probs.append(dict(
                b=b, h=h, idx=b * MLSTM_HEADS + h, cols=slice(h * MLSTM_DV, (h + 1) * MLSTM_DV),
                q=jnp.where(keep, q, jnp.zeros_like(q)), k=jnp.where(keep, k, jnp.zeros_like(k)),
                u_c=u_cols[:, h:h + 1], it_r=it_rows[h:h + 1, :], bt_r=bt_rows[h:h + 1, :]))

    for p in probs:
        p["m_old"] = m_ref[p["idx"]][:, 0:1]
        p["d"] = jnp.where(causal, p["bt_r"] + p["u_c"], NEG_INF)
        p["inter"] = p["bt_r"] + p["m_old"]
        p["kq"] = _dot_nt(p["k"], p["q"])
        p["q_t"] = p["q"].astype(F32).T.astype(BF16)
        p["v_t"] = mv_ref[p["b"], :, p["cols"]].astype(F32).T
    for p in probs:
        p["mt"] = jnp.maximum(jnp.max(p["d"], axis=0, keepdims=True), p["inter"])
    for p in probs:
        p["sm"] = p["kq"] * jnp.exp(p["d"] - p["mt"])
        p["wi"] = jnp.exp(p["inter"] - p["mt"])
    for p in probs:
        c_old = c_ref[p["idx"]]
        n_old = jnp.broadcast_to(n_ref[p["idx"]], (sub8[0], LANES))
        num = (_dot(p["v_t"].astype(BF16), p["sm"].astype(BF16))
               + p["wi"] * _dot(c_old.astype(BF16), p["q_t"]))
        den = (jnp.sum(p["sm"], axis=0, keepdims=True)
               + p["wi"] * _dot(n_old.astype(BF16), p["q_t"])[0:1])
        lower = jnp.maximum(jnp.abs(den), jnp.exp(-p["mt"]))
        h_t = num / lower
        p["hn"] = (h_t * lax.rsqrt(jnp.mean(h_t * h_t, axis=0, keepdims=True) + EPS)).T
    for p in probs:
        og = og_ref[p["b"], :, p["cols"]].astype(F32)
        hm_ref[p["b"], :, p["cols"]] = (
            p["hn"] * gh_ref[:, p["cols"]] * jax.nn.sigmoid(og)).astype(hm_ref.dtype)
    for p in probs:
        idx = p["idx"]
        b_last = p["bt_r"][:, chunk - 1:chunk]
        m_new = p["mt"][:, chunk - 1:chunk]
        wk = jnp.exp(b_last - p["bt_r"] + p["it_r"] - m_new)
        wc = jnp.exp(b_last + p["m_old"] - m_new)
        c_ref[idx] = wc * c_ref[idx] + _dot((p["v_t"] * wk).astype(BF16), p["k"])
        n_ref[idx] = wc * n_ref[idx] + _dot(jnp.broadcast_to(wk, sub8).astype(BF16), p["k"])[0:1]
        m_ref[idx] = jnp.broadcast_to(m_new, (1, LANES))


def _mlstm_prompt(mq, mk, mv, og, grow, g_head, nbatch, chunk):
    t = mq.shape[0]
    seq = t // nbatch
    nstate = nbatch * MLSTM_HEADS

    def seq3(a):
        return a.reshape(nbatch, seq, a.shape[-1])

    def chunks(width):
        return pl.BlockSpec((nbatch, chunk, width), lambda c: (0, c, 0))

    def state(shape):
        return pl.BlockSpec(shape, lambda c: (0, 0, 0))

    hm, c_st, n_st, m_st = pl.pallas_call(
        functools.partial(_mlstm_prompt_kernel, nbatch=nbatch, chunk=chunk),
        grid=(seq // chunk,),
        in_specs=[
            chunks(MQK_WIDTH), chunks(MQK_WIDTH), chunks(MLSTM_WIDTH), chunks(MLSTM_WIDTH),
            pl.BlockSpec((nbatch, GATE_ROWS, chunk), lambda c: (0, 0, c)),
            _const_spec((1, MLSTM_WIDTH)),
        ],
        out_specs=[
            chunks(MLSTM_WIDTH),
            state((nstate, MLSTM_DV, LANES)), state((nstate, 1, LANES)), state((nstate, 1, LANES)),
        ],
        out_shape=[
            jax.ShapeDtypeStruct((nbatch, seq, MLSTM_WIDTH), BF16),
            jax.ShapeDtypeStruct((nstate, MLSTM_DV, LANES), F32),
            jax.ShapeDtypeStruct((nstate, 1, LANES), F32),
            jax.ShapeDtypeStruct((nstate, 1, LANES), F32),
        ],
        compiler_params=pltpu.CompilerParams(
            dimension_semantics=("arbitrary",), vmem_limit_bytes=VMEM_LIMIT),
    )(seq3(mq), seq3(mk), seq3(mv), seq3(og), grow, g_head)

    c_st = c_st.reshape(nbatch, MLSTM_HEADS, MLSTM_DV, 2, MLSTM_DK)
    n_st = n_st.reshape(nbatch, MLSTM_HEADS, 2, MLSTM_DK)
    c_fin = jnp.stack([c_st[:, h, :, h % 2, :] for h in range(MLSTM_HEADS)], axis=1)
    n_fin = jnp.stack([n_st[:, h, h % 2, :] for h in range(MLSTM_HEADS)], axis=1)
    m_fin = m_st[:, 0, 0].reshape(nbatch, MLSTM_HEADS)
    return hm.reshape(t, MLSTM_WIDTH), c_fin, n_fin, m_fin


def _mlstm_sample_kernel(mq_ref, mk_ref, mkt_ref, mv_ref, og_ref, gc_ref, gr_ref, mrow_ref, c_ref,
                         n_ref, gh_ref, hm_ref, co_ref, no_ref, mt_ref, *, nb, tq):
    rows = nb * tq
    shift = tq.bit_length() - 1
    r = lax.broadcasted_iota(jnp.int32, (rows, rows), 0)
    s = lax.broadcasted_iota(jnp.int32, (rows, rows), 1)
    same = (r >> shift) == (s >> shift)
    valid = same & (s <= r)
    tril = jnp.where(valid, 1.0, 0.0).astype(BF16)
    triu = jnp.where(same & (r <= s), 1.0, 0.0).astype(BF16)
    row_batch = lax.broadcasted_iota(jnp.int32, (rows, 1), 0) >> shift
    lane = lax.broadcasted_iota(jnp.int32, (rows, LANES), 1)

    gc = gc_ref[...]
    gr = gr_ref[0]
    btc = _cumsum_cols(tril, gc)
    btr = _cumsum_rows(gr, triu)
    m_rows = mrow_ref[...]
    mt_all = jnp.zeros((rows, LANES), F32)

    def last_of_batch(col):
        parts = [jnp.broadcast_to(col[(b + 1) * tq - 1:(b + 1) * tq], (tq, 1)) for b in range(nb)]
        return jnp.concatenate(parts, axis=0)

    dk = MLSTM_DK
    lo = _low_half((rows, LANES))
    own_q = row_batch == (lax.broadcasted_iota(jnp.int32, (rows, nb * dk), 1) >> _log2(dk))
    lane_batch = lax.broadcasted_iota(jnp.int32, (dk, rows), 1) >> shift

    def state_rows(b, h):
        return slice((b * MLSTM_HEADS + h) * dk, (b * MLSTM_HEADS + h + 1) * dk)

    for h in range(MLSTM_HEADS):
        slab = slice((h // 2) * LANES, (h // 2 + 1) * LANES)
        qs, ks = mq_ref[:, slab], mk_ref[:, slab]
        qsw, ksw = pltpu.roll(qs, dk, axis=1), pltpu.roll(ks, dk, axis=1)
        q2 = jnp.where(lo, qs, qsw) if h % 2 == 0 else jnp.where(lo, qsw, qs)
        k2 = jnp.where(lo, ks, ksw) if h % 2 == 0 else jnp.where(lo, ksw, ks)
        qb, kb = q2[:, :dk].astype(BF16), k2[:, :dk].astype(BF16)
        vb = mv_ref[:, h * MLSTM_DV:(h + 1) * MLSTM_DV].astype(BF16)
        it_c = gc[:, h:h + 1]
        bt_c = btc[:, MLSTM_HEADS + h:MLSTM_HEADS + h + 1]
        it_r = gr[h:h + 1, :]
        bt_r = btr[MLSTM_HEADS + h:MLSTM_HEADS + h + 1, :]
        m_c = m_rows[:, h:h + 1]

        num, ssum, mt, wi = _mlstm_intra(qb, kb, vb, it_c, bt_c, it_r, bt_r, m_c, valid)
        qx = jnp.where(own_q, jnp.concatenate([q2] * (nb // 2), axis=1), 0.0).astype(BF16)
        c_stack = jnp.concatenate([c_ref[state_rows(b, h), :] for b in range(nb)], axis=0)
        qc = _dot(qx, c_stack.astype(BF16))
        n_rows = jnp.concatenate(
            [jnp.broadcast_to(n_ref[b * MLSTM_HEADS + h], (tq, dk)) for b in range(nb)], axis=0)
        num = num + wi * qc
        den = ssum + wi * jnp.sum(qb.astype(F32) * n_rows, axis=-1, keepdims=True)
        og = og_ref[:, h * MLSTM_DV:(h + 1) * MLSTM_DV]
        g_head = gh_ref[:, h * MLSTM_DV:(h + 1) * MLSTM_DV]
        hm_ref[:, h * MLSTM_DV:(h + 1) * MLSTM_DV] = _mlstm_head_out(num, den, mt, og, g_head)

        b_last = last_of_batch(bt_c)
        m_new = last_of_batch(mt)
        wk = jnp.exp(b_last - bt_c + it_c - m_new)
        wc = jnp.exp(b_last + m_c - m_new)
        wkv = (wk * vb.astype(F32)).astype(BF16)
        wkk = wk * kb.astype(F32)
        kt = mkt_ref[h * dk:(h + 1) * dk, :]
        kx = jnp.concatenate([jnp.where(lane_batch == b, kt, 0.0) for b in range(nb)], axis=0)
        upd = _dot(kx.astype(BF16), wkv)
        for b in range(nb):
            idx = b * MLSTM_HEADS + h
            wc_b = wc[b * tq:b * tq + 1]
            co_ref[state_rows(b, h), :] = wc_b * c_ref[state_rows(b, h), :] + upd[b * dk:(b + 1) * dk]
            no_ref[idx] = wc_b * n_ref[idx] + jnp.sum(wkk[b * tq:(b + 1) * tq], axis=0, keepdims=True)
        mt_all = jnp.where(lane == h, mt, mt_all)
    mt_ref[...] = mt_all


def _mlstm_sample(mq, mk, mkt, mv, og, gcol, grow, m_rows, ct_in, n_in, g_head, nbatch, nb):
    t = mq.shape[0]
    tq = t // nbatch
    rows = nb * tq
    assert tq & (tq - 1) == 0 and rows == LANES and nb % 2 == 0

    def tok(width):
        return pl.BlockSpec((rows, width), lambda i: (i, 0))

    ct_spec = pl.BlockSpec((nb * MLSTM_HEADS * MLSTM_DK, MLSTM_DV), lambda i: (i, 0))
    n_spec = pl.BlockSpec((nb * MLSTM_HEADS, 1, MLSTM_DK), lambda i: (i, 0, 0))
    return pl.pallas_call(
        functools.partial(_mlstm_sample_kernel, nb=nb, tq=tq),
        grid=(nbatch // nb,),
        in_specs=[
            tok(MQK_WIDTH), tok(MQK_WIDTH),
            pl.BlockSpec((MQK_WIDTH, rows), lambda i: (0, i)),
            tok(MLSTM_WIDTH), tok(MLSTM_WIDTH), tok(LANES),
            pl.BlockSpec((1, GATE_ROWS, rows), lambda i: (0, 0, i)),
            tok(LANES),
            ct_spec, n_spec,
            _const_spec((1, MLSTM_WIDTH)),
        ],
        out_specs=[tok(MLSTM_WIDTH), ct_spec, n_spec, tok(LANES)],
        out_shape=[
            jax.ShapeDtypeStruct((t, MLSTM_WIDTH), F32),
            jax.ShapeDtypeStruct(ct_in.shape, F32),
            jax.ShapeDtypeStruct(n_in.shape, F32),
            jax.ShapeDtypeStruct((t, LANES), F32),
        ],
        compiler_params=pltpu.CompilerParams(
            dimension_semantics=("arbitrary",), vmem_limit_bytes=VMEM_LIMIT),
    )(mq, mk, mkt, mv, og, gcol, grow, m_rows, ct_in, n_in, g_head)


def _memkv_kernel(mem_ref, g_ref, wk_ref, wv_ref, k_ref, v_ref):
    mn = _rmsnorm(mem_ref[...], g_ref[...]).astype(BF16)
    k_ref[...] = _dot(mn, wk_ref[...])
    v_ref[...] = _dot(mn, wv_ref[...])


def _memkv(mem, w, nbatch):
    t, d = mem.shape
    m = t // nbatch
    spec = pl.BlockSpec((m, CROSS_WIDTH), lambda b: (b, 0))
    return pl.pallas_call(
        _memkv_kernel,
        grid=(nbatch,),
        in_specs=[
            pl.BlockSpec((m, d), lambda b: (b, 0)),
            _const_spec((1, d)), _const_spec((d, CROSS_WIDTH)), _const_spec((d, CROSS_WIDTH)),
        ],
        out_specs=[spec, spec],
        out_shape=[jax.ShapeDtypeStruct((t, CROSS_WIDTH), F32)] * 2,
        compiler_params=pltpu.CompilerParams(
            dimension_semantics=("arbitrary",), vmem_limit_bytes=VMEM_LIMIT),
    )(mem, w["g_mem"], w["w_ck"], w["w_cv"])


def _mix_stage(x, att, hm, wo_ref, gcr_ref, wcq_ref):
    cat = jnp.concatenate([att.astype(BF16), hm.astype(BF16)], axis=1)
    y1 = x + _dot(cat, wo_ref[...])
    qc = _dot(_rmsnorm(y1, gcr_ref[...]).astype(BF16), wcq_ref[...])
    return y1, qc


def _cross_attention(qc, mk, mv):
    slabs = []
    for j in range(CROSS_WIDTH // LANES):
        qs = qc[:, j * LANES:(j + 1) * LANES]
        ks = mk[:, j * LANES:(j + 1) * LANES]
        vs = mv[:, j * LANES:(j + 1) * LANES]
        n = qs.shape[0]
        lo = _low_half(qs.shape)
        zero = jnp.zeros_like(qs)
        q2 = jnp.concatenate([jnp.where(lo, qs, zero), jnp.where(lo, zero, qs)], axis=0)
        s = _dot_nt(q2 * (HEAD_DIM ** -0.5), ks)
        p = jnp.exp(s - jnp.max(s, axis=-1, keepdims=True))
        o2 = _dot(p.astype(BF16), vs) / jnp.sum(p, axis=-1, keepdims=True)
        slabs.append(jnp.where(lo, o2[:n], o2[n:]))
    return jnp.concatenate(slabs, axis=1)


FFN_CHUNK = 256


def _interleave_kernel(wg_ref, wu_ref, o_ref):
    o_ref[:, :FFN_CHUNK] = wg_ref[...].astype(o_ref.dtype)
    o_ref[:, FFN_CHUNK:] = wu_ref[...].astype(o_ref.dtype)


def _interleave_gate_up(w_gate, w_up):
    d, hidden = w_gate.shape
    chunk = pl.BlockSpec((d, FFN_CHUNK), lambda c: (0, c))
    return pl.pallas_call(
        _interleave_kernel,
        grid=(hidden // FFN_CHUNK,),
        in_specs=[chunk, chunk],
        out_specs=pl.BlockSpec((d, 2 * FFN_CHUNK), lambda c: (0, c)),
        out_shape=jax.ShapeDtypeStruct((d, 2 * hidden), BF16),
        compiler_params=pltpu.CompilerParams(
            dimension_semantics=("arbitrary",), vmem_limit_bytes=VMEM_LIMIT),
    )(w_gate, w_up)


def _ffn_stage(y1, o, wco_ref, gf_ref, wgu_ref, wd_ref, act_ref, gfin_ref, final, rows=slice(None)):
    y2 = y1 + _dot(o.astype(BF16), wco_ref[...])
    xn = _rmsnorm(y2, gf_ref[...]).astype(BF16)
    for c in range(wd_ref.shape[0] // FFN_CHUNK):
        gu = _dot(xn, wgu_ref[:, 2 * c * FFN_CHUNK:2 * (c + 1) * FFN_CHUNK])
        g, u = gu[:, :FFN_CHUNK], gu[:, FFN_CHUNK:]
        act_ref[rows, c * FFN_CHUNK:(c + 1) * FFN_CHUNK] = (g * jax.nn.sigmoid(g) * u).astype(BF16)
    y3 = y2 + _dot(act_ref[rows, :], wd_ref[...])
    return _rmsnorm(y3, gfin_ref[...]) if final else y3


def _post_prompt_kernel(x_ref, att_ref, hm_ref, mk_ref, mv_ref, wo_ref, gcr_ref, wcq_ref, wco_ref,
                        gf_ref, wgu_ref, wd_ref, gfin_ref, o_ref, act_ref, *, final):
    y1, qc = _mix_stage(x_ref[...], att_ref[...], hm_ref[...], wo_ref, gcr_ref, wcq_ref)
    o = _cross_attention(qc.astype(BF16), mk_ref[...].astype(BF16), mv_ref[...].astype(BF16))
    o_ref[...] = _ffn_stage(y1, o, wco_ref, gf_ref, wgu_ref, wd_ref, act_ref, gfin_ref, final)


_MIX_WEIGHTS = ("w_out", "g_cross", "w_cq")
_FFN_WEIGHTS = ("w_co", "g_ffn", "w_gate_up", "w_down", "g_final")


def _post_prompt(x, att, hm, memk, memv, w, nbatch, final, tm):
    t, d = x.shape
    tpb = t // nbatch // tm
    m = memk.shape[0] // nbatch
    hidden = w["w_down"].shape[0]

    def rows(width):
        return pl.BlockSpec((tm, width), lambda i: (i, 0))

    mem_spec = pl.BlockSpec((m, CROSS_WIDTH), lambda i: (i // tpb, 0))
    names = _MIX_WEIGHTS + _FFN_WEIGHTS
    return pl.pallas_call(
        functools.partial(_post_prompt_kernel, final=final),
        grid=(t // tm,),
        in_specs=[rows(d), rows(ATT_WIDTH), rows(MLSTM_WIDTH), mem_spec, mem_spec]
        + _weight_specs(w, names),
        out_specs=rows(d),
        out_shape=jax.ShapeDtypeStruct((t, d), F32),
        scratch_shapes=[pltpu.VMEM((tm, hidden), BF16)],
        compiler_params=pltpu.CompilerParams(
            dimension_semantics=("arbitrary",), vmem_limit_bytes=VMEM_LIMIT),
    )(x, att, hm, memk, memv, *[w[n] for n in names])


def _mix_kernel(x_ref, att_ref, hm_ref, wo_ref, gcr_ref, wcq_ref, y_ref, qc_ref):
    y1, qc = _mix_stage(x_ref[...], att_ref[...], hm_ref[...], wo_ref, gcr_ref, wcq_ref)
    y_ref[...] = y1
    qc_ref[...] = qc


def _mix(x, att, hm, w, tm):
    t, d = x.shape

    def rows(width):
        return pl.BlockSpec((tm, width), lambda i: (i, 0))

    return pl.pallas_call(
        _mix_kernel,
        grid=(t // tm,),
        in_specs=[rows(d), rows(ATT_WIDTH), rows(MLSTM_WIDTH)] + _weight_specs(w, _MIX_WEIGHTS),
        out_specs=[rows(d), rows(CROSS_WIDTH)],
        out_shape=[jax.ShapeDtypeStruct((t, d), F32), jax.ShapeDtypeStruct((t, CROSS_WIDTH), F32)],
        compiler_params=pltpu.CompilerParams(
            dimension_semantics=("arbitrary",), vmem_limit_bytes=VMEM_LIMIT),
    )(x, att, hm, *[w[n] for n in _MIX_WEIGHTS])


def _cross_sample_kernel(qc_ref, mk_ref, mv_ref, o_ref, *, nb, tq):
    mem = CROSS_WIDTH
    n = CROSS_HEADS * tq
    row_head = lax.broadcasted_iota(jnp.int32, (n, CROSS_WIDTH), 0) >> _log2(tq)
    lane_head = lax.broadcasted_iota(jnp.int32, (n, CROSS_WIDTH), 1) >> _log2(HEAD_DIM)
    own = row_head == lane_head
    for b in range(nb):
        q = qc_ref[b * tq:(b + 1) * tq, :]
        qx = jnp.where(own, jnp.concatenate([q] * CROSS_HEADS, axis=0), 0.0).astype(BF16)
        mk = mk_ref[b * mem:(b + 1) * mem, :].astype(BF16)
        mv = mv_ref[b * mem:(b + 1) * mem, :].astype(BF16)
        s = _dot(qx, mk) * (HEAD_DIM ** -0.5)
        p = jnp.exp(s - jnp.max(s, axis=-1, keepdims=True))
        pv = _dot_nt(p.astype(BF16), mv) / jnp.sum(p, axis=-1, keepdims=True)
        pv = jnp.where(own, pv, 0.0)
        out = pv[0:tq]
        for h in range(1, CROSS_HEADS):
            out = out + pv[h * tq:(h + 1) * tq]
        o_ref[b * tq:(b + 1) * tq, :] = out


def _cross_sample(qc, memk_t, memv_t, nbatch, nb):
    t = qc.shape[0]
    tq = t // nbatch
    mem = memk_t.shape[1]
    tok = pl.BlockSpec((nb * tq, CROSS_WIDTH), lambda i: (i, 0))
    mspec = pl.BlockSpec((nb * CROSS_WIDTH, mem), lambda i: (i, 0))
    return pl.pallas_call(
        functools.partial(_cross_sample_kernel, nb=nb, tq=tq),
        grid=(nbatch // nb,),
        in_specs=[tok, mspec, mspec],
        out_specs=tok,
        out_shape=jax.ShapeDtypeStruct((t, CROSS_WIDTH), F32),
        compiler_params=pltpu.CompilerParams(
            dimension_semantics=("arbitrary",), vmem_limit_bytes=VMEM_LIMIT),
    )(qc, memk_t, memv_t)


def _ffn_kernel(y_ref, o_ref, wco_ref, gf_ref, wgu_ref, wd_ref, gfin_ref, out_ref, act_ref, *, final):
    out_ref[...] = _ffn_stage(y_ref[...], o_ref[...], wco_ref, gf_ref, wgu_ref, wd_ref, act_ref,
                              gfin_ref, final)


def _ffn(y1, o, w, final, tm):
    t, d = y1.shape
    hidden = w["w_down"].shape[0]

    def rows(width):
        return pl.BlockSpec((tm, width), lambda i: (i, 0))

    return pl.pallas_call(
        functools.partial(_ffn_kernel, final=final),
        grid=(t // tm,),
        in_specs=[rows(d), rows(CROSS_WIDTH)] + _weight_specs(w, _FFN_WEIGHTS),
        out_specs=rows(d),
        out_shape=jax.ShapeDtypeStruct((t, d), F32),
        scratch_shapes=[pltpu.VMEM((tm, hidden), BF16)],
        compiler_params=pltpu.CompilerParams(
            dimension_semantics=("arbitrary",), vmem_limit_bytes=VMEM_LIMIT),
    )(y1, o, *[w[n] for n in _FFN_WEIGHTS])


def _layer_weights(l, w_in, b_igate, b_fgate, g_mlstm_head, w_out, g_mix, g_cross, g_mem, w_cq,
                   w_ck, w_cv, w_co, g_ffn, w_gate, w_up, w_down, g_final):
    d = w_in.shape[1]
    gate_w = w_in[l][:, MAIN_WIDTH:]
    gate_b = jnp.concatenate([b_igate[l], b_fgate[l]]).astype(F32)
    ngate = gate_b.shape[0]
    row = lambda a: a.astype(F32).reshape(1, -1)
    w_t = w_in[l].T
    k0 = ATT_WIDTH
    mk0 = ATT_WIDTH + 2 * KV_WIDTH + MQK_WIDTH
    return {
        "w_kvk_t": jnp.concatenate([w_t[k0:k0 + 2 * KV_WIDTH], w_t[mk0:mk0 + MQK_WIDTH]]).astype(BF16),
        "w_main": w_in[l][:, :MAIN_WIDTH].astype(BF16),
        "w_gate_cols": jnp.pad(gate_w, ((0, 0), (0, LANES - ngate))).astype(BF16),
        "w_gate_rows": jnp.pad(gate_w.T, ((0, GATE_ROWS - ngate), (0, 0))).astype(BF16),
        "b_gate_cols": jnp.pad(gate_b, (0, LANES - ngate)).reshape(1, LANES),
        "b_gate_rows": jnp.pad(gate_b, (0, GATE_ROWS - ngate)).reshape(GATE_ROWS, 1),
        "g_mix": row(g_mix[l]), "g_cross": row(g_cross[l]), "g_mem": row(g_mem[l]),
        "g_ffn": row(g_ffn[l]), "g_final": row(g_final), "g_head": row(g_mlstm_head[l]),
        "w_out": w_out[l].astype(BF16), "w_cq": w_cq[l].astype(BF16), "w_ck": w_ck[l].astype(BF16),
        "w_cv": w_cv[l].astype(BF16), "w_co": w_co[l].astype(BF16),
        "w_gate_up": _interleave_gate_up(w_gate[l].astype(F32), w_up[l].astype(F32)),
        "w_down": w_down[l].astype(BF16),
    }


TOKEN_TILE = 512
PROJ_TILE = 1024
SWA_QUERY_BLOCK = 512
MLSTM_CHUNK_ROWS = 256
SWA_SAMPLE_BATCH = 16
MLSTM_SAMPLE_BATCH = 16
CROSS_SAMPLE_BATCH = 8


def kernel(x_prompt, x_sample, mem_prompt, cache_swa_k, cache_swa_v, state_mlstm_C, state_mlstm_n,
           state_mlstm_m, cache_mem_k, cache_mem_v, w_in, b_igate, b_fgate, attn_sinks,
           g_mlstm_head, w_out, g_mix, g_cross, g_mem, w_cq, w_ck, w_cv, w_co, g_ffn, w_gate,
           w_up, w_down, g_final):
    depth = w_in.shape[0]
    bp, sp, d = x_prompt.shape
    bs, ss, _ = x_sample.shape
    mem_tokens = mem_prompt.shape[1]
    past = cache_swa_k.shape[2]
    yp = x_prompt.reshape(bp * sp, d)
    ys = x_sample.reshape(bs * ss, d)
    mem = mem_prompt.reshape(bp * mem_tokens, d)
    outs = [[] for _ in range(12)]

    for l in range(depth):
        final = l == depth - 1
        w = _layer_weights(l, w_in, b_igate, b_fgate, g_mlstm_head, w_out, g_mix, g_cross, g_mem,
                           w_cq, w_ck, w_cv, w_co, g_ffn, w_gate, w_up, w_down, g_final)
        sinks = attn_sinks[l].astype(F32)

        q, k, v, mq, mk, mv, og, grow = _proj(yp, w, bp, BF16, PROJ_TILE)
        att = _swa_prompt(q, k, v, sinks, bp, SWA_QUERY_BLOCK)
        hm, c_p, n_p, m_p = _mlstm_prompt(mq, mk, mv, og, grow, w["g_head"], bp, MLSTM_CHUNK_ROWS)
        memk, memv = _memkv(mem, w, bp)
        yp = _post_prompt(yp, att, hm, memk, memv, w, bp, final, TOKEN_TILE)
        win_shape = (bp, WINDOW, ATT_KV_HEADS, HEAD_DIM)
        outs[0].append(k.reshape(bp, sp, KV_WIDTH)[:, sp - WINDOW:].reshape(win_shape))
        outs[1].append(v.reshape(bp, sp, KV_WIDTH)[:, sp - WINDOW:].reshape(win_shape))
        outs[2].append(c_p)
        outs[3].append(n_p)
        outs[4].append(m_p)
        mem_shape = (bp, mem_tokens, CROSS_HEADS, HEAD_DIM)
        outs[5].append(memk.reshape(mem_shape))
        outs[6].append(memv.reshape(mem_shape))

        def tokens_last(a):
            return jnp.transpose(a.astype(F32), (0, 2, 3, 1)).reshape(-1, a.shape[1])

        q, _, _, mq, mk, mv, og, grow, gcol, knt, vnt, mkt = _proj(
            ys, w, 1, F32, TOKEN_TILE, decode=True)
        att, kbuf_t, vbuf_t = _swa_sample(
            q, knt, vnt, tokens_last(cache_swa_k[l]), tokens_last(cache_swa_v[l]), sinks, bs,
            SWA_SAMPLE_BATCH)
        m_rows = jnp.pad(jnp.repeat(state_mlstm_m[l].astype(F32), ss, axis=0),
                         ((0, 0), (0, LANES - MLSTM_HEADS)))
        ct_in = jnp.swapaxes(state_mlstm_C[l].astype(F32), 2, 3).reshape(-1, MLSTM_DV)
        hm, ct_s, n_s, mt = _mlstm_sample(
            mq, mk, mkt, mv, og, gcol, grow, m_rows, ct_in,
            state_mlstm_n[l].astype(F32).reshape(bs * MLSTM_HEADS, 1, MLSTM_DK),
            w["g_head"], bs, MLSTM_SAMPLE_BATCH)
        y1, qc = _mix(ys, att, hm, w, TOKEN_TILE)
        o = _cross_sample(qc, tokens_last(cache_mem_k[l]), tokens_last(cache_mem_v[l]), bs,
                          CROSS_SAMPLE_BATCH)
        ys = _ffn(y1, o, w, final, TOKEN_TILE)
        buf_t_shape = (bs, ATT_KV_HEADS, HEAD_DIM, past)
        outs[7].append(jnp.transpose(kbuf_t.reshape(buf_t_shape), (0, 3, 1, 2)))
        outs[8].append(jnp.transpose(vbuf_t.reshape(buf_t_shape), (0, 3, 1, 2)))
        outs[9].append(jnp.swapaxes(ct_s.reshape(bs, MLSTM_HEADS, MLSTM_DK, MLSTM_DV), 2, 3))
        outs[10].append(n_s.reshape(bs, MLSTM_HEADS, MLSTM_DK))
        outs[11].append(mt.reshape(bs, ss, LANES)[:, ss - 1, :MLSTM_HEADS])

    return (yp.reshape(bp, sp, d), ys.reshape(bs, ss, d)) + tuple(jnp.stack(o) for o in outs)
```

```python
import functools

import jax
import jax.numpy as jnp
from jax import lax
from jax.experimental import pallas as pl
from jax.experimental.pallas import tpu as pltpu

F32 = jnp.float32
BF16 = jnp.bfloat16

HEAD_DIM = 64
ATT_GROUP = 4
ATT_KV_HEADS = 2
ATT_WIDTH = 512
KV_WIDTH = ATT_KV_HEADS * HEAD_DIM
WINDOW = 128
LOG_WINDOW = 7
MLSTM_HEADS = 4
MLSTM_DV = 128
MLSTM_DK = 64
MQK_WIDTH = MLSTM_HEADS * MLSTM_DK
MLSTM_WIDTH = MLSTM_HEADS * MLSTM_DV
MAIN_WIDTH = ATT_WIDTH + 2 * KV_WIDTH + 2 * MQK_WIDTH + 2 * MLSTM_WIDTH
CROSS_HEADS = 4
CROSS_WIDTH = CROSS_HEADS * HEAD_DIM
EPS = 1e-6
NEG_INF = float("-inf")
assert HEAD_DIM == 4 ** 3

LANES = 128
GATE_ROWS = 16
VMEM_LIMIT = 56 * 1024 * 1024

NT = (((1,), (1,)), ((), ()))
TN = (((0,), (0,)), ((), ()))


def _dot(a, b):
    return jnp.dot(a, b, preferred_element_type=F32)


def _dot_nt(a, b):
    return lax.dot_general(a, b, NT, preferred_element_type=F32)


def _dot_tn(a, b):
    return lax.dot_general(a, b, TN, preferred_element_type=F32)


def _rmsnorm(x, g):
    return x * lax.rsqrt(jnp.mean(x * x, axis=-1, keepdims=True) + EPS) * g


def _log_sigmoid(x):
    return jnp.minimum(x, 0.0) - jnp.log1p(jnp.exp(-jnp.abs(x)))


def _split3(x):
    hi = x.astype(BF16)
    r1 = x - hi.astype(F32)
    mid = r1.astype(BF16)
    lo = (r1 - mid.astype(F32)).astype(BF16)
    return hi, mid, lo


def _cumsum_cols(tri, x):
    hi, mid, lo = _split3(x)
    return _dot(tri, hi) + _dot(tri, mid) + _dot(tri, lo)


def _cumsum_rows(x, tri):
    hi, mid, lo = _split3(x)
    return _dot(hi, tri) + _dot(mid, tri) + _dot(lo, tri)


def _log2(n):
    assert n > 0 and n & (n - 1) == 0, n
    return n.bit_length() - 1


def _low_half(shape):
    return lax.broadcasted_iota(jnp.int32, shape, 1) < HEAD_DIM


def _proj_kernel(x_ref, g_ref, w_ref, wgr_ref, br_ref, *rest, decode):
    if decode:
        wgc_ref, bc_ref, wt_ref = rest[:3]
        rest = rest[3:]
    q_ref, k_ref, v_ref, mq_ref, mk_ref, mv_ref, og_ref, gr_ref = rest[:8]
    xn = _rmsnorm(x_ref[...], g_ref[...]).astype(BF16)
    if decode:
        gc_ref, kt_ref, vt_ref, mkt_ref = rest[8:]
        zt = _dot_nt(wt_ref[...], xn)
        kt_ref[...] = zt[:KV_WIDTH]
        vt_ref[...] = zt[KV_WIDTH:2 * KV_WIDTH]
        mkt_ref[...] = zt[2 * KV_WIDTH:] * (MLSTM_DK ** -0.5)
        zc = _dot(xn, wgc_ref[...]) + bc_ref[...]
        lane = lax.broadcasted_iota(jnp.int32, zc.shape, 1)
        gc_ref[...] = jnp.where(lane < MLSTM_HEADS, zc, _log_sigmoid(zc))

    def mm(lo, hi):
        return _dot(xn, w_ref[:, lo:hi])

    o = 0
    q_ref[...] = mm(o, o + ATT_WIDTH).astype(q_ref.dtype)
    o += ATT_WIDTH
    kv = mm(o, o + 2 * KV_WIDTH)
    k_ref[...] = kv[:, :KV_WIDTH]
    v_ref[...] = kv[:, KV_WIDTH:]
    o += 2 * KV_WIDTH
    mqk = mm(o, o + 2 * MQK_WIDTH)
    mq_ref[...] = mqk[:, :MQK_WIDTH].astype(mq_ref.dtype)
    mk_ref[...] = (mqk[:, MQK_WIDTH:] * (MLSTM_DK ** -0.5)).astype(mk_ref.dtype)
    o += 2 * MQK_WIDTH
    mv_ref[...] = mm(o, o + MLSTM_WIDTH).astype(mv_ref.dtype)
    o += MLSTM_WIDTH
    og_ref[...] = mm(o, o + MLSTM_WIDTH).astype(og_ref.dtype)

    zr = _dot_nt(wgr_ref[...], xn) + br_ref[...]
    row = lax.broadcasted_iota(jnp.int32, zr.shape, 0)
    gr_ref[0] = jnp.where(row < MLSTM_HEADS, zr, _log_sigmoid(zr))


def _const_spec(shape):
    nd = len(shape)
    return pl.BlockSpec(shape, lambda *_: (0,) * nd, pipeline_mode=pl.Buffered(1))


def _weight_specs(w, names):
    return [_const_spec(w[n].shape) for n in names]


def _proj(x, w, nseg, act_dtype, tm, decode=False):
    t, d = x.shape
    seg = t // nseg
    tpb = seg // tm
    grid = (t // tm,)

    def rows(width):
        return pl.BlockSpec((tm, width), lambda i: (i, 0))

    out_shape = [
        jax.ShapeDtypeStruct((t, ATT_WIDTH), act_dtype),
        jax.ShapeDtypeStruct((t, KV_WIDTH), F32),
        jax.ShapeDtypeStruct((t, KV_WIDTH), F32),
        jax.ShapeDtypeStruct((t, MQK_WIDTH), act_dtype),
        jax.ShapeDtypeStruct((t, MQK_WIDTH), act_dtype),
        jax.ShapeDtypeStruct((t, MLSTM_WIDTH), act_dtype),
        jax.ShapeDtypeStruct((t, MLSTM_WIDTH), act_dtype),
        jax.ShapeDtypeStruct((nseg, GATE_ROWS, seg), F32),
    ]
    out_specs = [
        rows(ATT_WIDTH), rows(KV_WIDTH), rows(KV_WIDTH), rows(MQK_WIDTH), rows(MQK_WIDTH),
        rows(MLSTM_WIDTH), rows(MLSTM_WIDTH),
        pl.BlockSpec((1, GATE_ROWS, tm), lambda i: (i // tpb, 0, i % tpb)),
    ]
    names = ["g_mix", "w_main", "w_gate_rows", "b_gate_rows"]
    if decode:
        names += ["w_gate_cols", "b_gate_cols", "w_kvk_t"]
        out_shape.append(jax.ShapeDtypeStruct((t, LANES), F32))
        out_specs.append(rows(LANES))
        for width in (KV_WIDTH, KV_WIDTH, MQK_WIDTH):
            out_shape.append(jax.ShapeDtypeStruct((width, t), F32))
            out_specs.append(pl.BlockSpec((width, tm), lambda i: (0, i)))
    in_specs = [rows(d)] + _weight_specs(w, names)
    return pl.pallas_call(
        functools.partial(_proj_kernel, decode=decode),
        grid=grid,
        in_specs=in_specs,
        out_specs=out_specs,
        out_shape=out_shape,
        compiler_params=pltpu.CompilerParams(
            dimension_semantics=("arbitrary",), vmem_limit_bytes=VMEM_LIMIT),
    )(x, *[w[n] for n in names])


def _dup_heads(x):
    swapped = pltpu.roll(x, HEAD_DIM, axis=1)
    lo = _low_half(x.shape)
    return (jnp.where(lo, x, swapped).astype(BF16), jnp.where(lo, swapped, x).astype(BF16))


def _stack_query_heads(q, h):
    parts = []
    for g in range(ATT_GROUP):
        hq = h * ATT_GROUP + g
        slab = q[:, (hq // 2) * LANES:(hq // 2 + 1) * LANES]
        lo = _low_half(slab.shape)
        keep = lo if hq % 2 == 0 else jnp.logical_not(lo)
        parts.append(jnp.where(keep, slab, jnp.zeros_like(slab)))
    return jnp.concatenate(parts, axis=0).astype(BF16)


def _sink_column(sink_ref, h, rows_per_head):
    n = ATT_GROUP * rows_per_head
    grp = lax.broadcasted_iota(jnp.int32, (n, 1), 0) >> _log2(rows_per_head)
    col = jnp.full((n, 1), sink_ref[h * ATT_GROUP], F32)
    for g in range(1, ATT_GROUP):
        col = jnp.where(grp == g, sink_ref[h * ATT_GROUP + g], col)
    return col


def _swa_prompt_stages(sink_ref, q_ref, kc_ref, kp_ref, vc_ref, vp_ref, first_block, store):
    qb = q_ref.shape[0]
    k_all = jnp.concatenate([kp_ref[...], kc_ref[...]], axis=0)
    v_all = jnp.concatenate([vp_ref[...], vc_ref[...]], axis=0)
    kd = _dup_heads(k_all)
    v_t = v_all.T.astype(BF16)

    n = ATT_GROUP * WINDOW
    key = lax.broadcasted_iota(jnp.int32, (2 * WINDOW, n), 0)
    qry = lax.broadcasted_iota(jnp.int32, (2 * WINDOW, n), 1) & (WINDOW - 1)
    delta = key - qry
    in_band = (delta >= 0) & (delta <= WINDOW)
    grp = lax.broadcasted_iota(jnp.int32, (1, n), 1) >> LOG_WINDOW
    sinks = []
    for h in range(ATT_KV_HEADS):
        row = jnp.full((1, n), sink_ref[h * ATT_GROUP], F32)
        for g in range(1, ATT_GROUP):
            row = jnp.where(grp == g, sink_ref[h * ATT_GROUP + g], row)
        sinks.append(row)
    yield

    probs = []
    for j in range(qb // WINDOW):
        valid = in_band & (key >= jnp.where(first_block, WINDOW, 0)) if j == 0 else in_band
        q = q_ref[j * WINDOW:(j + 1) * WINDOW, :] * (HEAD_DIM ** -0.5)
        band = slice(j * WINDOW, (j + 2) * WINDOW)
        for h in range(ATT_KV_HEADS):
            qh = _stack_query_heads(q, h)
            s = _dot_nt(kd[h][band], qh)
            probs.append(dict(h=h, band=band, s=jnp.where(valid, s, NEG_INF)))
            yield
    for p in probs:
        p["m"] = jnp.maximum(jnp.max(p["s"], axis=0, keepdims=True), sinks[p["h"]])
        yield
    for p in probs:
        p["p"] = jnp.exp(p["s"] - p["m"])
        yield
    for p in probs:
        h = p["h"]
        denom = jnp.sum(p["p"], axis=0, keepdims=True) + jnp.exp(sinks[h] - p["m"])
        p["o_t"] = _dot(v_t[h * HEAD_DIM:(h + 1) * HEAD_DIM, p["band"]], p["p"].astype(BF16)) / denom
        yield
    for j in range(qb // WINDOW):
        pieces = [p["o_t"][:, g * WINDOW:(g + 1) * WINDOW]
                  for p in probs[j * ATT_KV_HEADS:(j + 1) * ATT_KV_HEADS] for g in range(ATT_GROUP)]
        store(j, jnp.concatenate(pieces, axis=0).T)
        yield


def _swa_sample_kernel(sink_ref, q_ref, knt_ref, vnt_ref, ck_ref, cv_ref,
                       o_ref, kbuf_ref, vbuf_ref, *, nb, tq):
    past = ck_ref.shape[1]
    hw = ATT_KV_HEADS * HEAD_DIM
    heads = ATT_KV_HEADS * ATT_GROUP
    n = heads * tq
    fresh = past - tq
    t = lax.broadcasted_iota(jnp.int32, (n, 2 * past), 0) & (tq - 1)
    col = lax.broadcasted_iota(jnp.int32, (n, 2 * past), 1)
    valid = ((col < past) & (col >= t)) | ((col >= past + fresh) & (col - (past + fresh) <= t))
    sink = jnp.concatenate([_sink_column(sink_ref, h, tq) for h in range(ATT_KV_HEADS)], axis=0)
    lane = lax.broadcasted_iota(jnp.int32, (hw, past), 1)
    knt = knt_ref[...]
    vnt = vnt_ref[...]
    zero_slab = jnp.zeros((tq, LANES), F32)
    lo = _low_half((tq, LANES))

    def shifted(old, new_t, b):
        return jnp.where(lane >= fresh, pltpu.roll(new_t, (fresh - b * tq) % past, axis=1),
                         pltpu.roll(old, fresh, axis=1))

    def dup_rows(x):
        h0, h1 = x[:HEAD_DIM], x[HEAD_DIM:]
        return jnp.concatenate([h0, h0, h1, h1], axis=0)

    for b in range(nb):
        k_old = ck_ref[b * hw:(b + 1) * hw, :]
        v_old = cv_ref[b * hw:(b + 1) * hw, :]
        k_new = shifted(k_old, knt, b)
        v_new = shifted(v_old, vnt, b)
        kbuf_ref[b * hw:(b + 1) * hw, :] = k_new
        vbuf_ref[b * hw:(b + 1) * hw, :] = v_new
        kop = jnp.concatenate([dup_rows(k_old), dup_rows(k_new)], axis=1).astype(BF16)
        vop = jnp.concatenate([dup_rows(v_old), dup_rows(v_new)], axis=1).astype(BF16)

        q = q_ref[b * tq:(b + 1) * tq, :]
        blocks = []
        for hq in range(heads):
            slab = q[:, (hq // 2) * LANES:(hq // 2 + 1) * LANES]
            slab = jnp.where(lo if hq % 2 == 0 else jnp.logical_not(lo), slab, 0.0)
            pair = [slab, zero_slab] if hq // ATT_GROUP == 0 else [zero_slab, slab]
            blocks.append(jnp.concatenate(pair, axis=1))
        qx = jnp.concatenate(blocks, axis=0).astype(BF16)

        s = _dot(qx, kop) * (HEAD_DIM ** -0.5)
        s = jnp.where(valid, s, NEG_INF)
        m = jnp.maximum(jnp.max(s, axis=-1, keepdims=True), sink)
        p = jnp.exp(s - m)
        denom = jnp.sum(p, axis=-1, keepdims=True) + jnp.exp(sink - m)
        o = _dot_nt(p.astype(BF16), vop) / denom

        slabs = []
        for j in range(heads // 2):
            h = (2 * j) // ATT_GROUP
            even = o[(2 * j) * tq:(2 * j + 1) * tq, h * LANES:(h + 1) * LANES]
            odd = o[(2 * j + 1) * tq:(2 * j + 2) * tq, h * LANES:(h + 1) * LANES]
            slabs.append(jnp.where(lo, even, odd))
        o_ref[b * tq:(b + 1) * tq, :] = jnp.concatenate(slabs, axis=1)


def _swa_sample(q, knt, vnt, cache_kt, cache_vt, sinks, nbatch, nb):
    t = q.shape[0]
    tq = t // nbatch
    past = cache_kt.shape[1]
    hw = ATT_KV_HEADS * HEAD_DIM
    assert past == WINDOW and tq & (tq - 1) == 0 and nb * tq == past

    def rows(r, width):
        return pl.BlockSpec((r, width), lambda i: (i, 0))

    new_t = pl.BlockSpec((hw, nb * tq), lambda i: (0, i))
    return pl.pallas_call(
        functools.partial(_swa_sample_kernel, nb=nb, tq=tq),
        grid=(nbatch // nb,),
        in_specs=[
            pl.BlockSpec(memory_space=pltpu.SMEM),
            rows(nb * tq, ATT_WIDTH), new_t, new_t, rows(nb * hw, past), rows(nb * hw, past),
        ],
        out_specs=[rows(nb * tq, ATT_WIDTH), rows(nb * hw, past), rows(nb * hw, past)],
        out_shape=[
            jax.ShapeDtypeStruct((t, ATT_WIDTH), F32),
            jax.ShapeDtypeStruct(cache_kt.shape, F32),
            jax.ShapeDtypeStruct(cache_vt.shape, F32),
        ],
        compiler_params=pltpu.CompilerParams(
            dimension_semantics=("arbitrary",), vmem_limit_bytes=VMEM_LIMIT),
    )(sinks, q, knt, vnt, cache_kt, cache_vt)


def _mlstm_intra(q, k, v, it_c, bt_c, it_r, bt_r, m_c, valid):
    d = jnp.where(valid, bt_c - bt_r + it_r, NEG_INF)
    inter = bt_c + m_c
    mt = jnp.maximum(jnp.max(d, axis=-1, keepdims=True), inter)
    sm = _dot_nt(q, k) * jnp.exp(d - mt)
    wi = jnp.exp(inter - mt)
    return _dot(sm.astype(BF16), v), jnp.sum(sm, axis=-1, keepdims=True), mt, wi


def _mlstm_head_out(num, den, mt, og, g_head):
    lower = jnp.maximum(jnp.abs(den), jnp.exp(-mt))
    h = num / lower
    hn = h * lax.rsqrt(jnp.mean(h * h, axis=-1, keepdims=True) + EPS)
    return hn * g_head * jax.nn.sigmoid(og)


def _mlstm_prompt_kernel(mq_ref, mk_ref, mv_ref, og_ref, gr_ref, gh_ref,
                         hm_ref, c_ref, n_ref, m_ref, *, nbatch, chunk):
    @pl.when(pl.program_id(0) == 0)
    def _():
        c_ref[...] = jnp.zeros_like(c_ref)
        n_ref[...] = jnp.zeros_like(n_ref)
        m_ref[...] = jnp.zeros_like(m_ref)

    s_idx = lax.broadcasted_iota(jnp.int32, (chunk, chunk), 0)
    l_idx = lax.broadcasted_iota(jnp.int32, (chunk, chunk), 1)
    causal = s_idx <= l_idx
    triu = jnp.where(causal, 1.0, 0.0).astype(BF16)
    lo = _low_half((chunk, LANES))
    pad_rows = jnp.zeros((LANES - MLSTM_HEADS, chunk), F32)
    sub8 = (GATE_ROWS, chunk)

    probs = []
    for b in range(nbatch):
        gr = gr_ref[b]
        btr = _cumsum_rows(gr, triu)
        it_rows = gr[:MLSTM_HEADS]
        bt_rows = btr[MLSTM_HEADS:2 * MLSTM_HEADS]
        u_cols = jnp.concatenate([it_rows - bt_rows, pad_rows], axis=0).T
        for h in range(MLSTM_HEADS):
            slab = slice((h // 2) * LANES, (h // 2 + 1) * LANES)
            keep = lo if h % 2 == 0 else jnp.logical_not(lo)
            q = mq_ref[b, :, slab]
            k = mk_ref[b, :, slab]
            probs.append(dict(
                b=b, h=h, idx=b * MLSTM_HEADS + h, cols=slice(h * MLSTM_DV, (h + 1) * MLSTM_DV),
                q=jnp.where(keep, q, jnp.zeros_like(q)), k=jnp.where(keep, k, jnp.zeros_like(k)),
                u_c=u_cols[:, h:h + 1], it_r=it_rows[h:h + 1, :], bt_r=bt_rows[h:h + 1, :]))

    for p in probs:
        p["m_old"] = m_ref[p["idx"]][:, 0:1]
        p["d"] = jnp.where(causal, p["bt_r"] + p["u_c"], NEG_INF)
        p["inter"] = p["bt_r"] + p["m_old"]
        p["kq"] = _dot_nt(p["k"], p["q"])
        p["q_t"] = p["q"].astype(F32).T.astype(BF16)
        p["v_t"] = mv_ref[p["b"], :, p["cols"]].astype(F32).T
    for p in probs:
        p["mt"] = jnp.maximum(jnp.max(p["d"], axis=0, keepdims=True), p["inter"])
    for p in probs:
        p["sm"] = p["kq"] * jnp.exp(p["d"] - p["mt"])
        p["wi"] = jnp.exp(p["inter"] - p["mt"])
    for p in probs:
        c_old = c_ref[p["idx"]]
        n_old = jnp.broadcast_to(n_ref[p["idx"]], (sub8[0], LANES))
        num = (_dot(p["v_t"].astype(BF16), p["sm"].astype(BF16))
               + p["wi"] * _dot(c_old.astype(BF16), p["q_t"]))
        den = (jnp.sum(p["sm"], axis=0, keepdims=True)
               + p["wi"] * _dot(n_old.astype(BF16), p["q_t"])[0:1])
        lower = jnp.maximum(jnp.abs(den), jnp.exp(-p["mt"]))
        h_t = num / lower
        p["hn"] = (h_t * lax.rsqrt(jnp.mean(h_t * h_t, axis=0, keepdims=True) + EPS)).T
    for p in probs:
        og = og_ref[p["b"], :, p["cols"]].astype(F32)
        hm_ref[p["b"], :, p["cols"]] = (
            p["hn"] * gh_ref[:, p["cols"]] * jax.nn.sigmoid(og)).astype(hm_ref.dtype)
    for p in probs:
        idx = p["idx"]
        b_last = p["bt_r"][:, chunk - 1:chunk]
        m_new = p["mt"][:, chunk - 1:chunk]
        wk = jnp.exp(b_last - p["bt_r"] + p["it_r"] - m_new)
        wc = jnp.exp(b_last + p["m_old"] - m_new)
        c_ref[idx] = wc * c_ref[idx] + _dot((p["v_t"] * wk).astype(BF16), p["k"])
        n_ref[idx] = wc * n_ref[idx] + _dot(jnp.broadcast_to(wk, sub8).astype(BF16), p["k"])[0:1]
        m_ref[idx] = jnp.broadcast_to(m_new, (1, LANES))


def _mlstm_prompt(mq, mk, mv, og, grow, g_head, nbatch, chunk):
    t = mq.shape[0]
    seq = t // nbatch
    nstate = nbatch * MLSTM_HEADS

    def seq3(a):
        return a.reshape(nbatch, seq, a.shape[-1])

    def chunks(width):
        return pl.BlockSpec((nbatch, chunk, width), lambda c: (0, c, 0))

    def state(shape):
        return pl.BlockSpec(shape, lambda c: (0, 0, 0))

    hm, c_st, n_st, m_st = pl.pallas_call(
        functools.partial(_mlstm_prompt_kernel, nbatch=nbatch, chunk=chunk),
        grid=(seq // chunk,),
        in_specs=[
            chunks(MQK_WIDTH), chunks(MQK_WIDTH), chunks(MLSTM_WIDTH), chunks(MLSTM_WIDTH),
            pl.BlockSpec((nbatch, GATE_ROWS, chunk), lambda c: (0, 0, c)),
            _const_spec((1, MLSTM_WIDTH)),
        ],
        out_specs=[
            chunks(MLSTM_WIDTH),
            state((nstate, MLSTM_DV, LANES)), state((nstate, 1, LANES)), state((nstate, 1, LANES)),
        ],
        out_shape=[
            jax.ShapeDtypeStruct((nbatch, seq, MLSTM_WIDTH), BF16),
            jax.ShapeDtypeStruct((nstate, MLSTM_DV, LANES), F32),
            jax.ShapeDtypeStruct((nstate, 1, LANES), F32),
            jax.ShapeDtypeStruct((nstate, 1, LANES), F32),
        ],
        compiler_params=pltpu.CompilerParams(
            dimension_semantics=("arbitrary",), vmem_limit_bytes=VMEM_LIMIT),
    )(seq3(mq), seq3(mk), seq3(mv), seq3(og), grow, g_head)

    c_st = c_st.reshape(nbatch, MLSTM_HEADS, MLSTM_DV, 2, MLSTM_DK)
    n_st = n_st.reshape(nbatch, MLSTM_HEADS, 2, MLSTM_DK)
    c_fin = jnp.stack([c_st[:, h, :, h % 2, :] for h in range(MLSTM_HEADS)], axis=1)
    n_fin = jnp.stack([n_st[:, h, h % 2, :] for h in range(MLSTM_HEADS)], axis=1)
    m_fin = m_st[:, 0, 0].reshape(nbatch, MLSTM_HEADS)
    return hm.reshape(t, MLSTM_WIDTH), c_fin, n_fin, m_fin


def _mlstm_sample_kernel(mq_ref, mk_ref, mkt_ref, mv_ref, og_ref, gc_ref, gr_ref, mrow_ref, c_ref,
                         n_ref, gh_ref, hm_ref, co_ref, no_ref, mt_ref, *, nb, tq):
    rows = nb * tq
    shift = tq.bit_length() - 1
    r = lax.broadcasted_iota(jnp.int32, (rows, rows), 0)
    s = lax.broadcasted_iota(jnp.int32, (rows, rows), 1)
    same = (r >> shift) == (s >> shift)
    valid = same & (s <= r)
    tril = jnp.where(valid, 1.0, 0.0).astype(BF16)
    triu = jnp.where(same & (r <= s), 1.0, 0.0).astype(BF16)
    row_batch = lax.broadcasted_iota(jnp.int32, (rows, 1), 0) >> shift
    lane = lax.broadcasted_iota(jnp.int32, (rows, LANES), 1)

    gc = gc_ref[...]
    gr = gr_ref[0]
    btc = _cumsum_cols(tril, gc)
    btr = _cumsum_rows(gr, triu)
    m_rows = mrow_ref[...]
    mt_all = jnp.zeros((rows, LANES), F32)

    def last_of_batch(col):
        parts = [jnp.broadcast_to(col[(b + 1) * tq - 1:(b + 1) * tq], (tq, 1)) for b in range(nb)]
        return jnp.concatenate(parts, axis=0)

    dk = MLSTM_DK
    lo = _low_half((rows, LANES))
    own_q = row_batch == (lax.broadcasted_iota(jnp.int32, (rows, nb * dk), 1) >> _log2(dk))
    lane_batch = lax.broadcasted_iota(jnp.int32, (dk, rows), 1) >> shift

    def state_rows(b, h):
        return slice((b * MLSTM_HEADS + h) * dk, (b * MLSTM_HEADS + h + 1) * dk)

    for h in range(MLSTM_HEADS):
        slab = slice((h // 2) * LANES, (h // 2 + 1) * LANES)
        qs, ks = mq_ref[:, slab], mk_ref[:, slab]
        qsw, ksw = pltpu.roll(qs, dk, axis=1), pltpu.roll(ks, dk, axis=1)
        q2 = jnp.where(lo, qs, qsw) if h % 2 == 0 else jnp.where(lo, qsw, qs)
        k2 = jnp.where(lo, ks, ksw) if h % 2 == 0 else jnp.where(lo, ksw, ks)
        qb, kb = q2[:, :dk].astype(BF16), k2[:, :dk].astype(BF16)
        vb = mv_ref[:, h * MLSTM_DV:(h + 1) * MLSTM_DV].astype(BF16)
        it_c = gc[:, h:h + 1]
        bt_c = btc[:, MLSTM_HEADS + h:MLSTM_HEADS + h + 1]
        it_r = gr[h:h + 1, :]
        bt_r = btr[MLSTM_HEADS + h:MLSTM_HEADS + h + 1, :]
        m_c = m_rows[:, h:h + 1]

        num, ssum, mt, wi = _mlstm_intra(qb, kb, vb, it_c, bt_c, it_r, bt_r, m_c, valid)
        qx = jnp.where(own_q, jnp.concatenate([q2] * (nb // 2), axis=1), 0.0).astype(BF16)
        c_stack = jnp.concatenate([c_ref[state_rows(b, h), :] for b in range(nb)], axis=0)
        qc = _dot(qx, c_stack.astype(BF16))
        n_rows = jnp.concatenate(
            [jnp.broadcast_to(n_ref[b * MLSTM_HEADS + h], (tq, dk)) for b in range(nb)], axis=0)
        num = num + wi * qc
        den = ssum + wi * jnp.sum(qb.astype(F32) * n_rows, axis=-1, keepdims=True)
        og = og_ref[:, h * MLSTM_DV:(h + 1) * MLSTM_DV]
        g_head = gh_ref[:, h * MLSTM_DV:(h + 1) * MLSTM_DV]
        hm_ref[:, h * MLSTM_DV:(h + 1) * MLSTM_DV] = _mlstm_head_out(num, den, mt, og, g_head)

        b_last = last_of_batch(bt_c)
        m_new = last_of_batch(mt)
        wk = jnp.exp(b_last - bt_c + it_c - m_new)
        wc = jnp.exp(b_last + m_c - m_new)
        wkv = (wk * vb.astype(F32)).astype(BF16)
        wkk = wk * kb.astype(F32)
        kt = mkt_ref[h * dk:(h + 1) * dk, :]
        kx = jnp.concatenate([jnp.where(lane_batch == b, kt, 0.0) for b in range(nb)], axis=0)
        upd = _dot(kx.astype(BF16), wkv)
        for b in range(nb):
            idx = b * MLSTM_HEADS + h
            wc_b = wc[b * tq:b * tq + 1]
            co_ref[state_rows(b, h), :] = wc_b * c_ref[state_rows(b, h), :] + upd[b * dk:(b + 1) * dk]
            no_ref[idx] = wc_b * n_ref[idx] + jnp.sum(wkk[b * tq:(b + 1) * tq], axis=0, keepdims=True)
        mt_all = jnp.where(lane == h, mt, mt_all)
    mt_ref[...] = mt_all


def _mlstm_sample(mq, mk, mkt, mv, og, gcol, grow, m_rows, ct_in, n_in, g_head, nbatch, nb):
    t = mq.shape[0]
    tq = t // nbatch
    rows = nb * tq
    assert tq & (tq - 1) == 0 and rows == LANES and nb % 2 == 0

    def tok(width):
        return pl.BlockSpec((rows, width), lambda i: (i, 0))

    ct_spec = pl.BlockSpec((nb * MLSTM_HEADS * MLSTM_DK, MLSTM_DV), lambda i: (i, 0))
    n_spec = pl.BlockSpec((nb * MLSTM_HEADS, 1, MLSTM_DK), lambda i: (i, 0, 0))
    return pl.pallas_call(
        functools.partial(_mlstm_sample_kernel, nb=nb, tq=tq),
        grid=(nbatch // nb,),
        in_specs=[
            tok(MQK_WIDTH), tok(MQK_WIDTH),
            pl.BlockSpec((MQK_WIDTH, rows), lambda i: (0, i)),
            tok(MLSTM_WIDTH), tok(MLSTM_WIDTH), tok(LANES),
            pl.BlockSpec((1, GATE_ROWS, rows), lambda i: (0, 0, i)),
            tok(LANES),
            ct_spec, n_spec,
            _const_spec((1, MLSTM_WIDTH)),
        ],
        out_specs=[tok(MLSTM_WIDTH), ct_spec, n_spec, tok(LANES)],
        out_shape=[
            jax.ShapeDtypeStruct((t, MLSTM_WIDTH), F32),
            jax.ShapeDtypeStruct(ct_in.shape, F32),
            jax.ShapeDtypeStruct(n_in.shape, F32),
            jax.ShapeDtypeStruct((t, LANES), F32),
        ],
        compiler_params=pltpu.CompilerParams(
            dimension_semantics=("arbitrary",), vmem_limit_bytes=VMEM_LIMIT),
    )(mq, mk, mkt, mv, og, gcol, grow, m_rows, ct_in, n_in, g_head)


def _memkv_kernel(mem_ref, g_ref, wk_ref, wv_ref, k_ref, v_ref):
    mn = _rmsnorm(mem_ref[...], g_ref[...]).astype(BF16)
    k_ref[...] = _dot(mn, wk_ref[...])
    v_ref[...] = _dot(mn, wv_ref[...])


def _memkv(mem, w, nbatch):
    t, d = mem.shape
    m = t // nbatch
    spec = pl.BlockSpec((m, CROSS_WIDTH), lambda b: (b, 0))
    return pl.pallas_call(
        _memkv_kernel,
        grid=(nbatch,),
        in_specs=[
            pl.BlockSpec((m, d), lambda b: (b, 0)),
            _const_spec((1, d)), _const_spec((d, CROSS_WIDTH)), _const_spec((d, CROSS_WIDTH)),
        ],
        out_specs=[spec, spec],
        out_shape=[jax.ShapeDtypeStruct((t, CROSS_WIDTH), F32)] * 2,
        compiler_params=pltpu.CompilerParams(
            dimension_semantics=("arbitrary",), vmem_limit_bytes=VMEM_LIMIT),
    )(mem, w["g_mem"], w["w_ck"], w["w_cv"])


def _mix_stage(x, att, hm, wo_ref, gcr_ref, wcq_ref):
    cat = jnp.concatenate([att.astype(BF16), hm.astype(BF16)], axis=1)
    y1 = x + _dot(cat, wo_ref[...])
    qc = _dot(_rmsnorm(y1, gcr_ref[...]).astype(BF16), wcq_ref[...])
    return y1, qc


def _cross_attention(qc, mk, mv):
    slabs = []
    for j in range(CROSS_WIDTH // LANES):
        qs = qc[:, j * LANES:(j + 1) * LANES]
        ks = mk[:, j * LANES:(j + 1) * LANES]
        vs = mv[:, j * LANES:(j + 1) * LANES]
        n = qs.shape[0]
        lo = _low_half(qs.shape)
        zero = jnp.zeros_like(qs)
        q2 = jnp.concatenate([jnp.where(lo, qs, zero), jnp.where(lo, zero, qs)], axis=0)
        s = _dot_nt(q2 * (HEAD_DIM ** -0.5), ks)
        p = jnp.exp(s - jnp.max(s, axis=-1, keepdims=True))
        o2 = _dot(p.astype(BF16), vs) / jnp.sum(p, axis=-1, keepdims=True)
        slabs.append(jnp.where(lo, o2[:n], o2[n:]))
    return jnp.concatenate(slabs, axis=1)


FFN_CHUNK = 256


def _interleave_kernel(wg_ref, wu_ref, o_ref):
    o_ref[:, :FFN_CHUNK] = wg_ref[...].astype(o_ref.dtype)
    o_ref[:, FFN_CHUNK:] = wu_ref[...].astype(o_ref.dtype)


def _interleave_gate_up(w_gate, w_up):
    d, hidden = w_gate.shape
    chunk = pl.BlockSpec((d, FFN_CHUNK), lambda c: (0, c))
    return pl.pallas_call(
        _interleave_kernel,
        grid=(hidden // FFN_CHUNK,),
        in_specs=[chunk, chunk],
        out_specs=pl.BlockSpec((d, 2 * FFN_CHUNK), lambda c: (0, c)),
        out_shape=jax.ShapeDtypeStruct((d, 2 * hidden), BF16),
        compiler_params=pltpu.CompilerParams(
            dimension_semantics=("arbitrary",), vmem_limit_bytes=VMEM_LIMIT),
    )(w_gate, w_up)


def _ffn_stage(y1, o, wco_ref, gf_ref, wgu_ref, wd_ref, act_ref, gfin_ref, final, between=None):
    y2 = y1 + _dot(o.astype(BF16), wco_ref[...])
    xn = _rmsnorm(y2, gf_ref[...]).astype(BF16)
    for c in range(wd_ref.shape[0] // FFN_CHUNK):
        gu = _dot(xn, wgu_ref[:, 2 * c * FFN_CHUNK:2 * (c + 1) * FFN_CHUNK])
        g, u = gu[:, :FFN_CHUNK], gu[:, FFN_CHUNK:]
        act_ref[:, c * FFN_CHUNK:(c + 1) * FFN_CHUNK] = (g * jax.nn.sigmoid(g) * u).astype(BF16)
        if between is not None:
            between()
    y3 = y2 + _dot(act_ref[...], wd_ref[...])
    return _rmsnorm(y3, gfin_ref[...]) if final else y3


def _post_prompt_kernel(sink_ref, q_ref, kc_ref, kp_ref, vc_ref, vp_ref, x_ref, hm_ref, mk_ref, mv_ref,
                        wo_ref, gcr_ref, wcq_ref, wco_ref, gf_ref, wgu_ref, wd_ref, gfin_ref,
                        o_ref, act_ref, att_ref, *, final, ntiles, tiles_per_seq):
    s = pl.program_id(0)
    tile = jnp.minimum(s, ntiles - 1)
    first_block = (tile % tiles_per_seq) == 0
    slot = s % 2

    def swa_stages():
        def store(j, value):
            att_ref[slot, j * WINDOW:(j + 1) * WINDOW, :] = value.astype(att_ref.dtype)
        return _swa_prompt_stages(sink_ref, q_ref, kc_ref, kp_ref, vc_ref, vp_ref, first_block, store)

    @pl.when(s == 0)
    def _():
        for _ in swa_stages():
            pass

    @pl.when(s > 0)
    def _():
        swa = swa_stages()

        def advance(n=SWA_STAGES_PER_FFN_CHUNK):
            for _ in range(n):
                next(swa, None)

        y1, qc = _mix_stage(x_ref[...], att_ref[1 - slot], hm_ref[...], wo_ref, gcr_ref, wcq_ref)
        advance()
        o = _cross_attention(qc.astype(BF16), mk_ref[...].astype(BF16), mv_ref[...].astype(BF16))
        advance()
        o_ref[...] = _ffn_stage(y1, o, wco_ref, gf_ref, wgu_ref, wd_ref, act_ref, gfin_ref, final,
                                between=advance)
        for _ in swa:
            pass


_MIX_WEIGHTS = ("w_out", "g_cross", "w_cq")
_FFN_WEIGHTS = ("w_co", "g_ffn", "w_gate_up", "w_down", "g_final")


def _post_prompt(x, q, k, v, hm, memk, memv, sinks, w, nbatch, final, tm):
    t, d = x.shape
    ntiles = t // tm
    tpb = ntiles // nbatch
    wpt = tm // WINDOW
    m = memk.shape[0] // nbatch
    hidden = w["w_down"].shape[0]

    def attn_tile(s):
        return jnp.minimum(s, ntiles - 1)

    def post_tile(s):
        return jnp.maximum(s - 1, 0)

    def attn_rows(width):
        return pl.BlockSpec((tm, width), lambda s: (attn_tile(s), 0))

    def prev_window(width):
        return pl.BlockSpec((WINDOW, width), lambda s: (jnp.maximum(attn_tile(s) * wpt - 1, 0), 0))

    def post_rows(width):
        return pl.BlockSpec((tm, width), lambda s: (post_tile(s), 0))

    mem_spec = pl.BlockSpec((m, CROSS_WIDTH), lambda s: (post_tile(s) // tpb, 0))
    names = _MIX_WEIGHTS + _FFN_WEIGHTS
    return pl.pallas_call(
        functools.partial(_post_prompt_kernel, final=final, ntiles=ntiles, tiles_per_seq=tpb),
        grid=(ntiles + 1,),
        in_specs=[
            pl.BlockSpec(memory_space=pltpu.SMEM),
            attn_rows(ATT_WIDTH), attn_rows(KV_WIDTH), prev_window(KV_WIDTH),
            attn_rows(KV_WIDTH), prev_window(KV_WIDTH),
            post_rows(d), post_rows(MLSTM_WIDTH), mem_spec, mem_spec,
        ] + _weight_specs(w, names),
        out_specs=post_rows(d),
        out_shape=jax.ShapeDtypeStruct((t, d), F32),
        scratch_shapes=[pltpu.VMEM((tm, hidden), BF16), pltpu.VMEM((2, tm, ATT_WIDTH), BF16)],
        compiler_params=pltpu.CompilerParams(
            dimension_semantics=("arbitrary",), vmem_limit_bytes=VMEM_LIMIT),
    )(sinks, q, k, k, v, v, x, hm, memk, memv, *[w[n] for n in names])


def _mix_kernel(x_ref, att_ref, hm_ref, wo_ref, gcr_ref, wcq_ref, y_ref, qc_ref):
    y1, qc = _mix_stage(x_ref[...], att_ref[...], hm_ref[...], wo_ref, gcr_ref, wcq_ref)
    y_ref[...] = y1
    qc_ref[...] = qc


def _mix(x, att, hm, w, tm):
    t, d = x.shape

    def rows(width):
        return pl.BlockSpec((tm, width), lambda i: (i, 0))

    return pl.pallas_call(
        _mix_kernel,
        grid=(t // tm,),
        in_specs=[rows(d), rows(ATT_WIDTH), rows(MLSTM_WIDTH)] + _weight_specs(w, _MIX_WEIGHTS),
        out_specs=[rows(d), rows(CROSS_WIDTH)],
        out_shape=[jax.ShapeDtypeStruct((t, d), F32), jax.ShapeDtypeStruct((t, CROSS_WIDTH), F32)],
        compiler_params=pltpu.CompilerParams(
            dimension_semantics=("arbitrary",), vmem_limit_bytes=VMEM_LIMIT),
    )(x, att, hm, *[w[n] for n in _MIX_WEIGHTS])


def _cross_sample_kernel(qc_ref, mk_ref, mv_ref, o_ref, *, nb, tq):
    mem = CROSS_WIDTH
    n = CROSS_HEADS * tq
    row_head = lax.broadcasted_iota(jnp.int32, (n, CROSS_WIDTH), 0) >> _log2(tq)
    lane_head = lax.broadcasted_iota(jnp.int32, (n, CROSS_WIDTH), 1) >> _log2(HEAD_DIM)
    own = row_head == lane_head
    for b in range(nb):
        q = qc_ref[b * tq:(b + 1) * tq, :]
        qx = jnp.where(own, jnp.concatenate([q] * CROSS_HEADS, axis=0), 0.0).astype(BF16)
        mk = mk_ref[b * mem:(b + 1) * mem, :].astype(BF16)
        mv = mv_ref[b * mem:(b + 1) * mem, :].astype(BF16)
        s = _dot(qx, mk) * (HEAD_DIM ** -0.5)
        p = jnp.exp(s - jnp.max(s, axis=-1, keepdims=True))
        pv = _dot_nt(p.astype(BF16), mv) / jnp.sum(p, axis=-1, keepdims=True)
        pv = jnp.where(own, pv, 0.0)
        out = pv[0:tq]
        for h in range(1, CROSS_HEADS):
            out = out + pv[h * tq:(h + 1) * tq]
        o_ref[b * tq:(b + 1) * tq, :] = out


def _cross_sample(qc, memk_t, memv_t, nbatch, nb):
    t = qc.shape[0]
    tq = t // nbatch
    mem = memk_t.shape[1]
    tok = pl.BlockSpec((nb * tq, CROSS_WIDTH), lambda i: (i, 0))
    mspec = pl.BlockSpec((nb * CROSS_WIDTH, mem), lambda i: (i, 0))
    return pl.pallas_call(
        functools.partial(_cross_sample_kernel, nb=nb, tq=tq),
        grid=(nbatch // nb,),
        in_specs=[tok, mspec, mspec],
        out_specs=tok,
        out_shape=jax.ShapeDtypeStruct((t, CROSS_WIDTH), F32),
        compiler_params=pltpu.CompilerParams(
            dimension_semantics=("arbitrary",), vmem_limit_bytes=VMEM_LIMIT),
    )(qc, memk_t, memv_t)


def _ffn_kernel(y_ref, o_ref, wco_ref, gf_ref, wgu_ref, wd_ref, gfin_ref, out_ref, act_ref, *, final):
    out_ref[...] = _ffn_stage(y_ref[...], o_ref[...], wco_ref, gf_ref, wgu_ref, wd_ref, act_ref,
                              gfin_ref, final)


def _ffn(y1, o, w, final, tm):
    t, d = y1.shape
    hidden = w["w_down"].shape[0]

    def rows(width):
        return pl.BlockSpec((tm, width), lambda i: (i, 0))

    return pl.pallas_call(
        functools.partial(_ffn_kernel, final=final),
        grid=(t // tm,),
        in_specs=[rows(d), rows(CROSS_WIDTH)] + _weight_specs(w, _FFN_WEIGHTS),
        out_specs=rows(d),
        out_shape=jax.ShapeDtypeStruct((t, d), F32),
        scratch_shapes=[pltpu.VMEM((tm, hidden), BF16)],
        compiler_params=pltpu.CompilerParams(
            dimension_semantics=("arbitrary",), vmem_limit_bytes=VMEM_LIMIT),
    )(y1, o, *[w[n] for n in _FFN_WEIGHTS])


def _layer_weights(l, w_in, b_igate, b_fgate, g_mlstm_head, w_out, g_mix, g_cross, g_mem, w_cq,
                   w_ck, w_cv, w_co, g_ffn, w_gate, w_up, w_down, g_final):
    d = w_in.shape[1]
    gate_w = w_in[l][:, MAIN_WIDTH:]
    gate_b = jnp.concatenate([b_igate[l], b_fgate[l]]).astype(F32)
    ngate = gate_b.shape[0]
    row = lambda a: a.astype(F32).reshape(1, -1)
    w_t = w_in[l].T
    k0 = ATT_WIDTH
    mk0 = ATT_WIDTH + 2 * KV_WIDTH + MQK_WIDTH
    return {
        "w_kvk_t": jnp.concatenate([w_t[k0:k0 + 2 * KV_WIDTH], w_t[mk0:mk0 + MQK_WIDTH]]).astype(BF16),
        "w_main": w_in[l][:, :MAIN_WIDTH].astype(BF16),
        "w_gate_cols": jnp.pad(gate_w, ((0, 0), (0, LANES - ngate))).astype(BF16),
        "w_gate_rows": jnp.pad(gate_w.T, ((0, GATE_ROWS - ngate), (0, 0))).astype(BF16),
        "b_gate_cols": jnp.pad(gate_b, (0, LANES - ngate)).reshape(1, LANES),
        "b_gate_rows": jnp.pad(gate_b, (0, GATE_ROWS - ngate)).reshape(GATE_ROWS, 1),
        "g_mix": row(g_mix[l]), "g_cross": row(g_cross[l]), "g_mem": row(g_mem[l]),
        "g_ffn": row(g_ffn[l]), "g_final": row(g_final), "g_head": row(g_mlstm_head[l]),
        "w_out": w_out[l].astype(BF16), "w_cq": w_cq[l].astype(BF16), "w_ck": w_ck[l].astype(BF16),
        "w_cv": w_cv[l].astype(BF16), "w_co": w_co[l].astype(BF16),
        "w_gate_up": _interleave_gate_up(w_gate[l].astype(F32), w_up[l].astype(F32)),
        "w_down": w_down[l].astype(BF16),
    }


TOKEN_TILE = 512
PROJ_TILE = 1024
SWA_STAGES_PER_FFN_CHUNK = 3
MLSTM_CHUNK_ROWS = 256
SWA_SAMPLE_BATCH = 16
MLSTM_SAMPLE_BATCH = 16
CROSS_SAMPLE_BATCH = 8


def kernel(x_prompt, x_sample, mem_prompt, cache_swa_k, cache_swa_v, state_mlstm_C, state_mlstm_n,
           state_mlstm_m, cache_mem_k, cache_mem_v, w_in, b_igate, b_fgate, attn_sinks,
           g_mlstm_head, w_out, g_mix, g_cross, g_mem, w_cq, w_ck, w_cv, w_co, g_ffn, w_gate,
           w_up, w_down, g_final):
    depth = w_in.shape[0]
    bp, sp, d = x_prompt.shape
    bs, ss, _ = x_sample.shape
    mem_tokens = mem_prompt.shape[1]
    past = cache_swa_k.shape[2]
    yp = x_prompt.reshape(bp * sp, d)
    ys = x_sample.reshape(bs * ss, d)
    mem = mem_prompt.reshape(bp * mem_tokens, d)
    outs = [[] for _ in range(12)]

    for l in range(depth):
        final = l == depth - 1
        w = _layer_weights(l, w_in, b_igate, b_fgate, g_mlstm_head, w_out, g_mix, g_cross, g_mem,
                           w_cq, w_ck, w_cv, w_co, g_ffn, w_gate, w_up, w_down, g_final)
        sinks = attn_sinks[l].astype(F32)

        q, k, v, mq, mk, mv, og, grow = _proj(yp, w, bp, BF16, PROJ_TILE)
        hm, c_p, n_p, m_p = _mlstm_prompt(mq, mk, mv, og, grow, w["g_head"], bp, MLSTM_CHUNK_ROWS)
        memk, memv = _memkv(mem, w, bp)
        yp = _post_prompt(yp, q, k, v, hm, memk, memv, sinks, w, bp, final, TOKEN_TILE)
        win_shape = (bp, WINDOW, ATT_KV_HEADS, HEAD_DIM)
        outs[0].append(k.reshape(bp, sp, KV_WIDTH)[:, sp - WINDOW:].reshape(win_shape))
        outs[1].append(v.reshape(bp, sp, KV_WIDTH)[:, sp - WINDOW:].reshape(win_shape))
        outs[2].append(c_p)
        outs[3].append(n_p)
        outs[4].append(m_p)
        mem_shape = (bp, mem_tokens, CROSS_HEADS, HEAD_DIM)
        outs[5].append(memk.reshape(mem_shape))
        outs[6].append(memv.reshape(mem_shape))

        def tokens_last(a):
            return jnp.transpose(a.astype(F32), (0, 2, 3, 1)).reshape(-1, a.shape[1])

        q, _, _, mq, mk, mv, og, grow, gcol, knt, vnt, mkt = _proj(
            ys, w, 1, F32, TOKEN_TILE, decode=True)
        att, kbuf_t, vbuf_t = _swa_sample(
            q, knt, vnt, tokens_last(cache_swa_k[l]), tokens_last(cache_swa_v[l]), sinks, bs,
            SWA_SAMPLE_BATCH)
        m_rows = jnp.pad(jnp.repeat(state_mlstm_m[l].astype(F32), ss, axis=0),
                         ((0, 0), (0, LANES - MLSTM_HEADS)))
        ct_in = jnp.swapaxes(state_mlstm_C[l].astype(F32), 2, 3).reshape(-1, MLSTM_DV)
        hm, ct_s, n_s, mt = _mlstm_sample(
            mq, mk, mkt, mv, og, gcol, grow, m_rows, ct_in,
            state_mlstm_n[l].astype(F32).reshape(bs * MLSTM_HEADS, 1, MLSTM_DK),
            w["g_head"], bs, MLSTM_SAMPLE_BATCH)
        y1, qc = _mix(ys, att, hm, w, TOKEN_TILE)
        o = _cross_sample(qc, tokens_last(cache_mem_k[l]), tokens_last(cache_mem_v[l]), bs,
                          CROSS_SAMPLE_BATCH)
        ys = _ffn(y1, o, w, final, TOKEN_TILE)
        buf_t_shape = (bs, ATT_KV_HEADS, HEAD_DIM, past)
        outs[7].append(jnp.transpose(kbuf_t.reshape(buf_t_shape), (0, 3, 1, 2)))
        outs[8].append(jnp.transpose(vbuf_t.reshape(buf_t_shape), (0, 3, 1, 2)))
        outs[9].append(jnp.swapaxes(ct_s.reshape(bs, MLSTM_HEADS, MLSTM_DK, MLSTM_DV), 2, 3))
        outs[10].append(n_s.reshape(bs, MLSTM_HEADS, MLSTM_DK))
        outs[11].append(mt.reshape(bs, ss, LANES)[:, ss - 1, :MLSTM_HEADS])

    return (yp.reshape(bp, sp, d), ys.reshape(bs, ss, d)) + tuple(jnp.stack(o) for o in outs)
```

```python
import functools

import jax
import jax.numpy as jnp
from jax import lax
from jax.experimental import pallas as pl
from jax.experimental.pallas import tpu as pltpu

F32 = jnp.float32
BF16 = jnp.bfloat16

HEAD_DIM = 64
ATT_GROUP = 4
ATT_KV_HEADS = 2
ATT_WIDTH = 512
KV_WIDTH = ATT_KV_HEADS * HEAD_DIM
WINDOW = 128
LOG_WINDOW = 7
MLSTM_HEADS = 4
MLSTM_DV = 128
MLSTM_DK = 64
MQK_WIDTH = MLSTM_HEADS * MLSTM_DK
MLSTM_WIDTH = MLSTM_HEADS * MLSTM_DV
MAIN_WIDTH = ATT_WIDTH + 2 * KV_WIDTH + 2 * MQK_WIDTH + 2 * MLSTM_WIDTH
CROSS_HEADS = 4
CROSS_WIDTH = CROSS_HEADS * HEAD_DIM
EPS = 1e-6
NEG_INF = float("-inf")
assert HEAD_DIM == 4 ** 3

LANES = 128
GATE_ROWS = 16
VMEM_LIMIT = 56 * 1024 * 1024

NT = (((1,), (1,)), ((), ()))
TN = (((0,), (0,)), ((), ()))


def _dot(a, b):
    return jnp.dot(a, b, preferred_element_type=F32)


def _dot_nt(a, b):
    return lax.dot_general(a, b, NT, preferred_element_type=F32)


def _dot_tn(a, b):
    return lax.dot_general(a, b, TN, preferred_element_type=F32)


def _rmsnorm(x, g):
    return x * lax.rsqrt(jnp.mean(x * x, axis=-1, keepdims=True) + EPS) * g


def _log_sigmoid(x):
    return jnp.minimum(x, 0.0) - jnp.log1p(jnp.exp(-jnp.abs(x)))


def _split3(x):
    hi = x.astype(BF16)
    r1 = x - hi.astype(F32)
    mid = r1.astype(BF16)
    lo = (r1 - mid.astype(F32)).astype(BF16)
    return hi, mid, lo


def _cumsum_cols(tri, x):
    hi, mid, lo = _split3(x)
    return _dot(tri, hi) + _dot(tri, mid) + _dot(tri, lo)


def _cumsum_rows(x, tri):
    hi, mid, lo = _split3(x)
    return _dot(hi, tri) + _dot(mid, tri) + _dot(lo, tri)


def _log2(n):
    assert n > 0 and n & (n - 1) == 0, n
    return n.bit_length() - 1


def _low_half(shape):
    return lax.broadcasted_iota(jnp.int32, shape, 1) < HEAD_DIM


def _proj_kernel(x_ref, g_ref, w_ref, wgr_ref, br_ref, *rest, decode):
    if decode:
        wgc_ref, bc_ref, wt_ref = rest[:3]
        rest = rest[3:]
    q_ref, k_ref, v_ref, mq_ref, mk_ref, mv_ref, og_ref, gr_ref = rest[:8]
    xn = _rmsnorm(x_ref[...], g_ref[...]).astype(BF16)
    if decode:
        gc_ref, kt_ref, vt_ref, mkt_ref = rest[8:]
        zt = _dot_nt(wt_ref[...], xn)
        kt_ref[...] = zt[:KV_WIDTH]
        vt_ref[...] = zt[KV_WIDTH:2 * KV_WIDTH]
        mkt_ref[...] = zt[2 * KV_WIDTH:] * (MLSTM_DK ** -0.5)
        zc = _dot(xn, wgc_ref[...]) + bc_ref[...]
        lane = lax.broadcasted_iota(jnp.int32, zc.shape, 1)
        gc_ref[...] = jnp.where(lane < MLSTM_HEADS, zc, _log_sigmoid(zc))

    def mm(lo, hi):
        return _dot(xn, w_ref[:, lo:hi])

    o = 0
    q_ref[...] = mm(o, o + ATT_WIDTH).astype(q_ref.dtype)
    o += ATT_WIDTH
    kv = mm(o, o + 2 * KV_WIDTH)
    k_ref[...] = kv[:, :KV_WIDTH]
    v_ref[...] = kv[:, KV_WIDTH:]
    o += 2 * KV_WIDTH
    mqk = mm(o, o + 2 * MQK_WIDTH)
    mq_ref[...] = mqk[:, :MQK_WIDTH].astype(mq_ref.dtype)
    mk_ref[...] = (mqk[:, MQK_WIDTH:] * (MLSTM_DK ** -0.5)).astype(mk_ref.dtype)
    o += 2 * MQK_WIDTH
    mv_ref[...] = mm(o, o + MLSTM_WIDTH).astype(mv_ref.dtype)
    o += MLSTM_WIDTH
    og_ref[...] = mm(o, o + MLSTM_WIDTH).astype(og_ref.dtype)

    zr = _dot_nt(wgr_ref[...], xn) + br_ref[...]
    row = lax.broadcasted_iota(jnp.int32, zr.shape, 0)
    gr_ref[0] = jnp.where(row < MLSTM_HEADS, zr, _log_sigmoid(zr))


def _const_spec(shape):
    nd = len(shape)
    return pl.BlockSpec(shape, lambda *_: (0,) * nd, pipeline_mode=pl.Buffered(1))


def _weight_specs(w, names):
    return [_const_spec(w[n].shape) for n in names]


def _proj(x, w, nseg, act_dtype, tm, decode=False):
    t, d = x.shape
    seg = t // nseg
    tpb = seg // tm
    grid = (t // tm,)

    def rows(width):
        return pl.BlockSpec((tm, width), lambda i: (i, 0))

    out_shape = [
        jax.ShapeDtypeStruct((t, ATT_WIDTH), act_dtype),
        jax.ShapeDtypeStruct((t, KV_WIDTH), F32),
        jax.ShapeDtypeStruct((t, KV_WIDTH), F32),
        jax.ShapeDtypeStruct((t, MQK_WIDTH), act_dtype),
        jax.ShapeDtypeStruct((t, MQK_WIDTH), act_dtype),
        jax.ShapeDtypeStruct((t, MLSTM_WIDTH), act_dtype),
        jax.ShapeDtypeStruct((t, MLSTM_WIDTH), act_dtype),
        jax.ShapeDtypeStruct((nseg, GATE_ROWS, seg), F32),
    ]
    out_specs = [
        rows(ATT_WIDTH), rows(KV_WIDTH), rows(KV_WIDTH), rows(MQK_WIDTH), rows(MQK_WIDTH),
        rows(MLSTM_WIDTH), rows(MLSTM_WIDTH),
        pl.BlockSpec((1, GATE_ROWS, tm), lambda i: (i // tpb, 0, i % tpb)),
    ]
    names = ["g_mix", "w_main", "w_gate_rows", "b_gate_rows"]
    if decode:
        names += ["w_gate_cols", "b_gate_cols", "w_kvk_t"]
        out_shape.append(jax.ShapeDtypeStruct((t, LANES), F32))
        out_specs.append(rows(LANES))
        for width in (KV_WIDTH, KV_WIDTH, MQK_WIDTH):
            out_shape.append(jax.ShapeDtypeStruct((width, t), F32))
            out_specs.append(pl.BlockSpec((width, tm), lambda i: (0, i)))
    in_specs = [rows(d)] + _weight_specs(w, names)
    return pl.pallas_call(
        functools.partial(_proj_kernel, decode=decode),
        grid=grid,
        in_specs=in_specs,
        out_specs=out_specs,
        out_shape=out_shape,
        compiler_params=pltpu.CompilerParams(
            dimension_semantics=("arbitrary",), vmem_limit_bytes=VMEM_LIMIT),
    )(x, *[w[n] for n in names])


def _dup_heads(x):
    swapped = pltpu.roll(x, HEAD_DIM, axis=1)
    lo = _low_half(x.shape)
    return (jnp.where(lo, x, swapped).astype(BF16), jnp.where(lo, swapped, x).astype(BF16))


def _stack_query_heads(q, h):
    parts = []
    for g in range(ATT_GROUP):
        hq = h * ATT_GROUP + g
        slab = q[:, (hq // 2) * LANES:(hq // 2 + 1) * LANES]
        lo = _low_half(slab.shape)
        keep = lo if hq % 2 == 0 else jnp.logical_not(lo)
        parts.append(jnp.where(keep, slab, jnp.zeros_like(slab)))
    return jnp.concatenate(parts, axis=0).astype(BF16)


def _sink_column(sink_ref, h, rows_per_head):
    n = ATT_GROUP * rows_per_head
    grp = lax.broadcasted_iota(jnp.int32, (n, 1), 0) >> _log2(rows_per_head)
    col = jnp.full((n, 1), sink_ref[h * ATT_GROUP], F32)
    for g in range(1, ATT_GROUP):
        col = jnp.where(grp == g, sink_ref[h * ATT_GROUP + g], col)
    return col


def _swa_prompt_stages(sink_ref, q_ref, kc_ref, kp_ref, vc_ref, vp_ref, first_block, store):
    qb = q_ref.shape[0]
    k_all = jnp.concatenate([kp_ref[...], kc_ref[...]], axis=0)
    v_all = jnp.concatenate([vp_ref[...], vc_ref[...]], axis=0)
    kd = _dup_heads(k_all)
    v_t = v_all.T.astype(BF16)

    n = ATT_GROUP * WINDOW
    key = lax.broadcasted_iota(jnp.int32, (2 * WINDOW, n), 0)
    qry = lax.broadcasted_iota(jnp.int32, (2 * WINDOW, n), 1) & (WINDOW - 1)
    delta = key - qry
    in_band = (delta >= 0) & (delta <= WINDOW)
    grp = lax.broadcasted_iota(jnp.int32, (1, n), 1) >> LOG_WINDOW
    sinks = []
    for h in range(ATT_KV_HEADS):
        row = jnp.full((1, n), sink_ref[h * ATT_GROUP], F32)
        for g in range(1, ATT_GROUP):
            row = jnp.where(grp == g, sink_ref[h * ATT_GROUP + g], row)
        sinks.append(row)
    yield

    probs = []
    for j in range(qb // WINDOW):
        valid = in_band & (key >= jnp.where(first_block, WINDOW, 0)) if j == 0 else in_band
        q = q_ref[j * WINDOW:(j + 1) * WINDOW, :] * (HEAD_DIM ** -0.5)
        band = slice(j * WINDOW, (j + 2) * WINDOW)
        for h in range(ATT_KV_HEADS):
            qh = _stack_query_heads(q, h)
            s = _dot_nt(kd[h][band], qh)
            probs.append(dict(h=h, band=band, s=jnp.where(valid, s, NEG_INF)))
            yield
    for p in probs:
        p["m"] = jnp.maximum(jnp.max(p["s"], axis=0, keepdims=True), sinks[p["h"]])
        yield
    for p in probs:
        p["p"] = jnp.exp(p["s"] - p["m"])
        yield
    for p in probs:
        h = p["h"]
        denom = jnp.sum(p["p"], axis=0, keepdims=True) + jnp.exp(sinks[h] - p["m"])
        p["o_t"] = _dot(v_t[h * HEAD_DIM:(h + 1) * HEAD_DIM, p["band"]], p["p"].astype(BF16)) / denom
        yield
    for j in range(qb // WINDOW):
        pieces = [p["o_t"][:, g * WINDOW:(g + 1) * WINDOW]
                  for p in probs[j * ATT_KV_HEADS:(j + 1) * ATT_KV_HEADS] for g in range(ATT_GROUP)]
        store(j, jnp.concatenate(pieces, axis=0).T)
        yield


def _swa_sample_kernel(sink_ref, q_ref, knt_ref, vnt_ref, ck_ref, cv_ref,
                       o_ref, kbuf_ref, vbuf_ref, *, nb, tq):
    past = ck_ref.shape[1]
    hw = ATT_KV_HEADS * HEAD_DIM
    heads = ATT_KV_HEADS * ATT_GROUP
    n = heads * tq
    fresh = past - tq
    t = lax.broadcasted_iota(jnp.int32, (n, 2 * past), 0) & (tq - 1)
    col = lax.broadcasted_iota(jnp.int32, (n, 2 * past), 1)
    valid = ((col < past) & (col >= t)) | ((col >= past + fresh) & (col - (past + fresh) <= t))
    sink = jnp.concatenate([_sink_column(sink_ref, h, tq) for h in range(ATT_KV_HEADS)], axis=0)
    lane = lax.broadcasted_iota(jnp.int32, (hw, past), 1)
    knt = knt_ref[...]
    vnt = vnt_ref[...]
    zero_slab = jnp.zeros((tq, LANES), F32)
    lo = _low_half((tq, LANES))

    def shifted(old, new_t, b):
        return jnp.where(lane >= fresh, pltpu.roll(new_t, (fresh - b * tq) % past, axis=1),
                         pltpu.roll(old, fresh, axis=1))

    def dup_rows(x):
        h0, h1 = x[:HEAD_DIM], x[HEAD_DIM:]
        return jnp.concatenate([h0, h0, h1, h1], axis=0)

    for b in range(nb):
        k_old = ck_ref[b * hw:(b + 1) * hw, :]
        v_old = cv_ref[b * hw:(b + 1) * hw, :]
        k_new = shifted(k_old, knt, b)
        v_new = shifted(v_old, vnt, b)
        kbuf_ref[b * hw:(b + 1) * hw, :] = k_new
        vbuf_ref[b * hw:(b + 1) * hw, :] = v_new
        kop = jnp.concatenate([dup_rows(k_old), dup_rows(k_new)], axis=1).astype(BF16)
        vop = jnp.concatenate([dup_rows(v_old), dup_rows(v_new)], axis=1).astype(BF16)

        q = q_ref[b * tq:(b + 1) * tq, :]
        blocks = []
        for hq in range(heads):
            slab = q[:, (hq // 2) * LANES:(hq // 2 + 1) * LANES]
            slab = jnp.where(lo if hq % 2 == 0 else jnp.logical_not(lo), slab, 0.0)
            pair = [slab, zero_slab] if hq // ATT_GROUP == 0 else [zero_slab, slab]
            blocks.append(jnp.concatenate(pair, axis=1))
        qx = jnp.concatenate(blocks, axis=0).astype(BF16)

        s = _dot(qx, kop) * (HEAD_DIM ** -0.5)
        s = jnp.where(valid, s, NEG_INF)
        m = jnp.maximum(jnp.max(s, axis=-1, keepdims=True), sink)
        p = jnp.exp(s - m)
        denom = jnp.sum(p, axis=-1, keepdims=True) + jnp.exp(sink - m)
        o = _dot_nt(p.astype(BF16), vop) / denom

        slabs = []
        for j in range(heads // 2):
            h = (2 * j) // ATT_GROUP
            even = o[(2 * j) * tq:(2 * j + 1) * tq, h * LANES:(h + 1) * LANES]
            odd = o[(2 * j + 1) * tq:(2 * j + 2) * tq, h * LANES:(h + 1) * LANES]
            slabs.append(jnp.where(lo, even, odd))
        o_ref[b * tq:(b + 1) * tq, :] = jnp.concatenate(slabs, axis=1)


def _swa_sample(q, knt, vnt, cache_kt, cache_vt, sinks, nbatch, nb):
    t = q.shape[0]
    tq = t // nbatch
    past = cache_kt.shape[1]
    hw = ATT_KV_HEADS * HEAD_DIM
    assert past == WINDOW and tq & (tq - 1) == 0 and nb * tq == past

    def rows(r, width):
        return pl.BlockSpec((r, width), lambda i: (i, 0))

    new_t = pl.BlockSpec((hw, nb * tq), lambda i: (0, i))
    return pl.pallas_call(
        functools.partial(_swa_sample_kernel, nb=nb, tq=tq),
        grid=(nbatch // nb,),
        in_specs=[
            pl.BlockSpec(memory_space=pltpu.SMEM),
            rows(nb * tq, ATT_WIDTH), new_t, new_t, rows(nb * hw, past), rows(nb * hw, past),
        ],
        out_specs=[rows(nb * tq, ATT_WIDTH), rows(nb * hw, past), rows(nb * hw, past)],
        out_shape=[
            jax.ShapeDtypeStruct((t, ATT_WIDTH), F32),
            jax.ShapeDtypeStruct(cache_kt.shape, F32),
            jax.ShapeDtypeStruct(cache_vt.shape, F32),
        ],
        compiler_params=pltpu.CompilerParams(
            dimension_semantics=("arbitrary",), vmem_limit_bytes=VMEM_LIMIT),
    )(sinks, q, knt, vnt, cache_kt, cache_vt)


def _mlstm_intra(q, k, v, it_c, bt_c, it_r, bt_r, m_c, valid):
    d = jnp.where(valid, bt_c - bt_r + it_r, NEG_INF)
    inter = bt_c + m_c
    mt = jnp.maximum(jnp.max(d, axis=-1, keepdims=True), inter)
    sm = _dot_nt(q, k) * jnp.exp(d - mt)
    wi = jnp.exp(inter - mt)
    return _dot(sm.astype(BF16), v), jnp.sum(sm, axis=-1, keepdims=True), mt, wi


def _mlstm_head_out(num, den, mt, og, g_head):
    lower = jnp.maximum(jnp.abs(den), jnp.exp(-mt))
    h = num / lower
    hn = h * lax.rsqrt(jnp.mean(h * h, axis=-1, keepdims=True) + EPS)
    return hn * g_head * jax.nn.sigmoid(og)


def _mlstm_prompt_stages(mq, mk, mv, og, gr, gh_ref, hm_ref, c_ref, n_ref, m_ref, nbatch, chunk):
    s_idx = lax.broadcasted_iota(jnp.int32, (chunk, chunk), 0)
    l_idx = lax.broadcasted_iota(jnp.int32, (chunk, chunk), 1)
    causal = s_idx <= l_idx
    triu = jnp.where(causal, 1.0, 0.0).astype(BF16)
    lo = _low_half((chunk, LANES))
    pad_rows = jnp.zeros((LANES - MLSTM_HEADS, chunk), F32)
    sub8 = (GATE_ROWS, chunk)

    probs = []
    for b in range(nbatch):
        g = gr(b)
        btr = _cumsum_rows(g, triu)
        it_rows = g[:MLSTM_HEADS]
        bt_rows = btr[MLSTM_HEADS:2 * MLSTM_HEADS]
        u_cols = jnp.concatenate([it_rows - bt_rows, pad_rows], axis=0).T
        for h in range(MLSTM_HEADS):
            slab = slice((h // 2) * LANES, (h // 2 + 1) * LANES)
            keep = lo if h % 2 == 0 else jnp.logical_not(lo)
            q = mq(b, slab)
            k = mk(b, slab)
            probs.append(dict(
                b=b, h=h, idx=b * MLSTM_HEADS + h, cols=slice(h * MLSTM_DV, (h + 1) * MLSTM_DV),
                q=jnp.where(keep, q, jnp.zeros_like(q)), k=jnp.where(keep, k, jnp.zeros_like(k)),
                u_c=u_cols[:, h:h + 1], it_r=it_rows[h:h + 1, :], bt_r=bt_rows[h:h + 1, :]))
        yield

    for p in probs:
        p["m_old"] = m_ref[p["idx"]][:, 0:1]
        p["d"] = jnp.where(causal, p["bt_r"] + p["u_c"], NEG_INF)
        p["inter"] = p["bt_r"] + p["m_old"]
        p["kq"] = _dot_nt(p["k"], p["q"])
        p["q_t"] = p["q"].astype(F32).T.astype(BF16)
        p["v_t"] = mv(p["b"], p["cols"]).astype(F32).T
        yield
    for p in probs:
        p["mt"] = jnp.maximum(jnp.max(p["d"], axis=0, keepdims=True), p["inter"])
        yield
    for p in probs:
        p["sm"] = p["kq"] * jnp.exp(p["d"] - p["mt"])
        p["wi"] = jnp.exp(p["inter"] - p["mt"])
        yield
    for p in probs:
        c_old = c_ref[p["idx"]]
        n_old = jnp.broadcast_to(n_ref[p["idx"]], (sub8[0], LANES))
        num = (_dot(p["v_t"].astype(BF16), p["sm"].astype(BF16))
               + p["wi"] * _dot(c_old.astype(BF16), p["q_t"]))
        den = (jnp.sum(p["sm"], axis=0, keepdims=True)
               + p["wi"] * _dot(n_old.astype(BF16), p["q_t"])[0:1])
        lower = jnp.maximum(jnp.abs(den), jnp.exp(-p["mt"]))
        h_t = num / lower
        p["hn"] = (h_t * lax.rsqrt(jnp.mean(h_t * h_t, axis=0, keepdims=True) + EPS)).T
        yield
    for p in probs:
        gate = jax.nn.sigmoid(og(p["b"], p["cols"]).astype(F32))
        hm_ref[p["b"], :, p["cols"]] = (p["hn"] * gh_ref[:, p["cols"]] * gate).astype(hm_ref.dtype)
        yield
    for p in probs:
        idx = p["idx"]
        b_last = p["bt_r"][:, chunk - 1:chunk]
        m_new = p["mt"][:, chunk - 1:chunk]
        wk = jnp.exp(b_last - p["bt_r"] + p["it_r"] - m_new)
        wc = jnp.exp(b_last + p["m_old"] - m_new)
        c_ref[idx] = wc * c_ref[idx] + _dot((p["v_t"] * wk).astype(BF16), p["k"])
        n_ref[idx] = wc * n_ref[idx] + _dot(jnp.broadcast_to(wk, sub8).astype(BF16), p["k"])[0:1]
        m_ref[idx] = jnp.broadcast_to(m_new, (1, LANES))
        yield


def _mixer_prompt_kernel(x_ref, g_ref, w_ref, wgr_ref, br_ref, gh_ref,
                         q_ref, k_ref, v_ref, hm_ref, c_ref, n_ref, m_ref,
                         mq_scr, mk_scr, mv_scr, og_scr, gr_scr, *, nbatch, chunk):
    s = pl.program_id(0)
    wr = s % 2
    rd = 1 - wr
    rows = nbatch * chunk

    def proj_stages():
        xn = _rmsnorm(x_ref[...].reshape(rows, x_ref.shape[-1]), g_ref[...]).astype(BF16)
        yield

        def mm(lo, hi):
            return _dot(xn, w_ref[:, lo:hi])

        o = 0
        q_ref[...] = mm(o, o + ATT_WIDTH).astype(q_ref.dtype).reshape(q_ref.shape)
        yield
        o += ATT_WIDTH
        kv = mm(o, o + 2 * KV_WIDTH)
        k_ref[...] = kv[:, :KV_WIDTH].reshape(k_ref.shape)
        v_ref[...] = kv[:, KV_WIDTH:].reshape(v_ref.shape)
        yield
        o += 2 * KV_WIDTH
        mqk = mm(o, o + 2 * MQK_WIDTH)
        mq_scr[wr] = mqk[:, :MQK_WIDTH].astype(BF16)
        mk_scr[wr] = (mqk[:, MQK_WIDTH:] * (MLSTM_DK ** -0.5)).astype(BF16)
        yield
        o += 2 * MQK_WIDTH
        mv_scr[wr] = mm(o, o + MLSTM_WIDTH).astype(BF16)
        yield
        o += MLSTM_WIDTH
        og_scr[wr] = mm(o, o + MLSTM_WIDTH).astype(BF16)
        yield
        zr = _dot_nt(wgr_ref[...], xn) + br_ref[...]
        row = lax.broadcasted_iota(jnp.int32, zr.shape, 0)
        gr_scr[wr] = jnp.where(row < MLSTM_HEADS, zr, _log_sigmoid(zr))
        yield

    def mlstm_stages():
        def rows_of(scr):
            return lambda b, lanes: scr[rd, b * chunk:(b + 1) * chunk, lanes]
        return _mlstm_prompt_stages(
            rows_of(mq_scr), rows_of(mk_scr), rows_of(mv_scr), rows_of(og_scr),
            lambda b: gr_scr[rd, :, b * chunk:(b + 1) * chunk],
            gh_ref, hm_ref, c_ref, n_ref, m_ref, nbatch, chunk)

    @pl.when(s == 0)
    def _():
        c_ref[...] = jnp.zeros_like(c_ref)
        n_ref[...] = jnp.zeros_like(n_ref)
        m_ref[...] = jnp.zeros_like(m_ref)
        for _ in proj_stages():
            pass

    @pl.when(s > 0)
    def _():
        mlstm = mlstm_stages()
        for _ in proj_stages():
            for _ in range(MLSTM_STAGES_PER_PROJ_STAGE):
                next(mlstm, None)
        for _ in mlstm:
            pass


def _mixer_prompt(x, w, nbatch, chunk):
    t, d = x.shape
    seq = t // nbatch
    nchunks = seq // chunk
    nstate = nbatch * MLSTM_HEADS
    rows = nbatch * chunk

    def proj_chunk(width):
        return pl.BlockSpec((nbatch, chunk, width), lambda s: (0, jnp.minimum(s, nchunks - 1), 0))

    def state(shape):
        return pl.BlockSpec(shape, lambda s: (0, 0, 0))

    names = ["g_mix", "w_main", "w_gate_rows", "b_gate_rows", "g_head"]
    q, k, v, hm, c_st, n_st, m_st = pl.pallas_call(
        functools.partial(_mixer_prompt_kernel, nbatch=nbatch, chunk=chunk),
        grid=(nchunks + 1,),
        in_specs=[proj_chunk(d)] + _weight_specs(w, names),
        out_specs=[
            proj_chunk(ATT_WIDTH), proj_chunk(KV_WIDTH), proj_chunk(KV_WIDTH),
            pl.BlockSpec((nbatch, chunk, MLSTM_WIDTH), lambda s: (0, jnp.maximum(s - 1, 0), 0)),
            state((nstate, MLSTM_DV, LANES)), state((nstate, 1, LANES)), state((nstate, 1, LANES)),
        ],
        out_shape=[
            jax.ShapeDtypeStruct((nbatch, seq, ATT_WIDTH), BF16),
            jax.ShapeDtypeStruct((nbatch, seq, KV_WIDTH), F32),
            jax.ShapeDtypeStruct((nbatch, seq, KV_WIDTH), F32),
            jax.ShapeDtypeStruct((nbatch, seq, MLSTM_WIDTH), BF16),
            jax.ShapeDtypeStruct((nstate, MLSTM_DV, LANES), F32),
            jax.ShapeDtypeStruct((nstate, 1, LANES), F32),
            jax.ShapeDtypeStruct((nstate, 1, LANES), F32),
        ],
        scratch_shapes=[
            pltpu.VMEM((2, rows, MQK_WIDTH), BF16), pltpu.VMEM((2, rows, MQK_WIDTH), BF16),
            pltpu.VMEM((2, rows, MLSTM_WIDTH), BF16), pltpu.VMEM((2, rows, MLSTM_WIDTH), BF16),
            pltpu.VMEM((2, GATE_ROWS, rows), F32),
        ],
        compiler_params=pltpu.CompilerParams(
            dimension_semantics=("arbitrary",), vmem_limit_bytes=VMEM_LIMIT),
    )(x.reshape(nbatch, seq, d), *[w[n] for n in names])
    q = q.reshape(t, ATT_WIDTH)
    k = k.reshape(t, KV_WIDTH)
    v = v.reshape(t, KV_WIDTH)

    c_st = c_st.reshape(nbatch, MLSTM_HEADS, MLSTM_DV, 2, MLSTM_DK)
    n_st = n_st.reshape(nbatch, MLSTM_HEADS, 2, MLSTM_DK)
    c_fin = jnp.stack([c_st[:, h, :, h % 2, :] for h in range(MLSTM_HEADS)], axis=1)
    n_fin = jnp.stack([n_st[:, h, h % 2, :] for h in range(MLSTM_HEADS)], axis=1)
    m_fin = m_st[:, 0, 0].reshape(nbatch, MLSTM_HEADS)
    return q, k, v, hm.reshape(t, MLSTM_WIDTH), c_fin, n_fin, m_fin


def _mlstm_sample_kernel(mq_ref, mk_ref, mkt_ref, mv_ref, og_ref, gc_ref, gr_ref, mrow_ref, c_ref,
                         n_ref, gh_ref, hm_ref, co_ref, no_ref, mt_ref, *, nb, tq):
    rows = nb * tq
    shift = tq.bit_length() - 1
    r = lax.broadcasted_iota(jnp.int32, (rows, rows), 0)
    s = lax.broadcasted_iota(jnp.int32, (rows, rows), 1)
    same = (r >> shift) == (s >> shift)
    valid = same & (s <= r)
    tril = jnp.where(valid, 1.0, 0.0).astype(BF16)
    triu = jnp.where(same & (r <= s), 1.0, 0.0).astype(BF16)
    row_batch = lax.broadcasted_iota(jnp.int32, (rows, 1), 0) >> shift
    lane = lax.broadcasted_iota(jnp.int32, (rows, LANES), 1)

    gc = gc_ref[...]
    gr = gr_ref[0]
    btc = _cumsum_cols(tril, gc)
    btr = _cumsum_rows(gr, triu)
    m_rows = mrow_ref[...]
    mt_all = jnp.zeros((rows, LANES), F32)

    def last_of_batch(col):
        parts = [jnp.broadcast_to(col[(b + 1) * tq - 1:(b + 1) * tq], (tq, 1)) for b in range(nb)]
        return jnp.concatenate(parts, axis=0)

    dk = MLSTM_DK
    lo = _low_half((rows, LANES))
    own_q = row_batch == (lax.broadcasted_iota(jnp.int32, (rows, nb * dk), 1) >> _log2(dk))
    lane_batch = lax.broadcasted_iota(jnp.int32, (dk, rows), 1) >> shift

    def state_rows(b, h):
        return slice((b * MLSTM_HEADS + h) * dk, (b * MLSTM_HEADS + h + 1) * dk)

    for h in range(MLSTM_HEADS):
        slab = slice((h // 2) * LANES, (h // 2 + 1) * LANES)
        qs, ks = mq_ref[:, slab], mk_ref[:, slab]
        qsw, ksw = pltpu.roll(qs, dk, axis=1), pltpu.roll(ks, dk, axis=1)
        q2 = jnp.where(lo, qs, qsw) if h % 2 == 0 else jnp.where(lo, qsw, qs)
        k2 = jnp.where(lo, ks, ksw) if h % 2 == 0 else jnp.where(lo, ksw, ks)
        qb, kb = q2[:, :dk].astype(BF16), k2[:, :dk].astype(BF16)
        vb = mv_ref[:, h * MLSTM_DV:(h + 1) * MLSTM_DV].astype(BF16)
        it_c = gc[:, h:h + 1]
        bt_c = btc[:, MLSTM_HEADS + h:MLSTM_HEADS + h + 1]
        it_r = gr[h:h + 1, :]
        bt_r = btr[MLSTM_HEADS + h:MLSTM_HEADS + h + 1, :]
        m_c = m_rows[:, h:h + 1]

        num, ssum, mt, wi = _mlstm_intra(qb, kb, vb, it_c, bt_c, it_r, bt_r, m_c, valid)
        qx = jnp.where(own_q, jnp.concatenate([q2] * (nb // 2), axis=1), 0.0).astype(BF16)
        c_stack = jnp.concatenate([c_ref[state_rows(b, h), :] for b in range(nb)], axis=0)
        qc = _dot(qx, c_stack.astype(BF16))
        n_rows = jnp.concatenate(
            [jnp.broadcast_to(n_ref[b * MLSTM_HEADS + h], (tq, dk)) for b in range(nb)], axis=0)
        num = num + wi * qc
        den = ssum + wi * jnp.sum(qb.astype(F32) * n_rows, axis=-1, keepdims=True)
        og = og_ref[:, h * MLSTM_DV:(h + 1) * MLSTM_DV]
        g_head = gh_ref[:, h * MLSTM_DV:(h + 1) * MLSTM_DV]
        hm_ref[:, h * MLSTM_DV:(h + 1) * MLSTM_DV] = _mlstm_head_out(num, den, mt, og, g_head)

        b_last = last_of_batch(bt_c)
        m_new = last_of_batch(mt)
        wk = jnp.exp(b_last - bt_c + it_c - m_new)
        wc = jnp.exp(b_last + m_c - m_new)
        wkv = (wk * vb.astype(F32)).astype(BF16)
        wkk = wk * kb.astype(F32)
        kt = mkt_ref[h * dk:(h + 1) * dk, :]
        kx = jnp.concatenate([jnp.where(lane_batch == b, kt, 0.0) for b in range(nb)], axis=0)
        upd = _dot(kx.astype(BF16), wkv)
        for b in range(nb):
            idx = b * MLSTM_HEADS + h
            wc_b = wc[b * tq:b * tq + 1]
            co_ref[state_rows(b, h), :] = wc_b * c_ref[state_rows(b, h), :] + upd[b * dk:(b + 1) * dk]
            no_ref[idx] = wc_b * n_ref[idx] + jnp.sum(wkk[b * tq:(b + 1) * tq], axis=0, keepdims=True)
        mt_all = jnp.where(lane == h, mt, mt_all)
    mt_ref[...] = mt_all


def _mlstm_sample(mq, mk, mkt, mv, og, gcol, grow, m_rows, ct_in, n_in, g_head, nbatch, nb):
    t = mq.shape[0]
    tq = t // nbatch
    rows = nb * tq
    assert tq & (tq - 1) == 0 and rows == LANES and nb % 2 == 0

    def tok(width):
        return pl.BlockSpec((rows, width), lambda i: (i, 0))

    ct_spec = pl.BlockSpec((nb * MLSTM_HEADS * MLSTM_DK, MLSTM_DV), lambda i: (i, 0))
    n_spec = pl.BlockSpec((nb * MLSTM_HEADS, 1, MLSTM_DK), lambda i: (i, 0, 0))
    return pl.pallas_call(
        functools.partial(_mlstm_sample_kernel, nb=nb, tq=tq),
        grid=(nbatch // nb,),
        in_specs=[
            tok(MQK_WIDTH), tok(MQK_WIDTH),
            pl.BlockSpec((MQK_WIDTH, rows), lambda i: (0, i)),
            tok(MLSTM_WIDTH), tok(MLSTM_WIDTH), tok(LANES),
            pl.BlockSpec((1, GATE_ROWS, rows), lambda i: (0, 0, i)),
            tok(LANES),
            ct_spec, n_spec,
            _const_spec((1, MLSTM_WIDTH)),
        ],
        out_specs=[tok(MLSTM_WIDTH), ct_spec, n_spec, tok(LANES)],
        out_shape=[
            jax.ShapeDtypeStruct((t, MLSTM_WIDTH), F32),
            jax.ShapeDtypeStruct(ct_in.shape, F32),
            jax.ShapeDtypeStruct(n_in.shape, F32),
            jax.ShapeDtypeStruct((t, LANES), F32),
        ],
        compiler_params=pltpu.CompilerParams(
            dimension_semantics=("arbitrary",), vmem_limit_bytes=VMEM_LIMIT),
    )(mq, mk, mkt, mv, og, gcol, grow, m_rows, ct_in, n_in, g_head)


def _memkv_kernel(mem_ref, g_ref, wk_ref, wv_ref, k_ref, v_ref):
    mn = _rmsnorm(mem_ref[...], g_ref[...]).astype(BF16)
    k_ref[...] = _dot(mn, wk_ref[...])
    v_ref[...] = _dot(mn, wv_ref[...])


def _memkv(mem, w, nbatch):
    t, d = mem.shape
    m = t // nbatch
    spec = pl.BlockSpec((m, CROSS_WIDTH), lambda b: (b, 0))
    return pl.pallas_call(
        _memkv_kernel,
        grid=(nbatch,),
        in_specs=[
            pl.BlockSpec((m, d), lambda b: (b, 0)),
            _const_spec((1, d)), _const_spec((d, CROSS_WIDTH)), _const_spec((d, CROSS_WIDTH)),
        ],
        out_specs=[spec, spec],
        out_shape=[jax.ShapeDtypeStruct((t, CROSS_WIDTH), F32)] * 2,
        compiler_params=pltpu.CompilerParams(
            dimension_semantics=("arbitrary",), vmem_limit_bytes=VMEM_LIMIT),
    )(mem, w["g_mem"], w["w_ck"], w["w_cv"])


def _mix_stage(x, att, hm, wo_ref, gcr_ref, wcq_ref):
    cat = jnp.concatenate([att.astype(BF16), hm.astype(BF16)], axis=1)
    y1 = x + _dot(cat, wo_ref[...])
    qc = _dot(_rmsnorm(y1, gcr_ref[...]).astype(BF16), wcq_ref[...])
    return y1, qc


def _cross_attention(qc, mk, mv):
    slabs = []
    for j in range(CROSS_WIDTH // LANES):
        qs = qc[:, j * LANES:(j + 1) * LANES]
        ks = mk[:, j * LANES:(j + 1) * LANES]
        vs = mv[:, j * LANES:(j + 1) * LANES]
        n = qs.shape[0]
        lo = _low_half(qs.shape)
        zero = jnp.zeros_like(qs)
        q2 = jnp.concatenate([jnp.where(lo, qs, zero), jnp.where(lo, zero, qs)], axis=0)
        s = _dot_nt(q2 * (HEAD_DIM ** -0.5), ks)
        p = jnp.exp(s - jnp.max(s, axis=-1, keepdims=True))
        o2 = _dot(p.astype(BF16), vs) / jnp.sum(p, axis=-1, keepdims=True)
        slabs.append(jnp.where(lo, o2[:n], o2[n:]))
    return jnp.concatenate(slabs, axis=1)


FFN_CHUNK = 256


def _interleave_kernel(wg_ref, wu_ref, o_ref):
    o_ref[:, :FFN_CHUNK] = wg_ref[...].astype(o_ref.dtype)
    o_ref[:, FFN_CHUNK:] = wu_ref[...].astype(o_ref.dtype)


def _interleave_gate_up(w_gate, w_up):
    d, hidden = w_gate.shape
    chunk = pl.BlockSpec((d, FFN_CHUNK), lambda c: (0, c))
    return pl.pallas_call(
        _interleave_kernel,
        grid=(hidden // FFN_CHUNK,),
        in_specs=[chunk, chunk],
        out_specs=pl.BlockSpec((d, 2 * FFN_CHUNK), lambda c: (0, c)),
        out_shape=jax.ShapeDtypeStruct((d, 2 * hidden), BF16),
        compiler_params=pltpu.CompilerParams(
            dimension_semantics=("arbitrary",), vmem_limit_bytes=VMEM_LIMIT),
    )(w_gate, w_up)


def _ffn_stage(y1, o, wco_ref, gf_ref, wgu_ref, wd_ref, act_ref, gfin_ref, final, between=None):
    y2 = y1 + _dot(o.astype(BF16), wco_ref[...])
    xn = _rmsnorm(y2, gf_ref[...]).astype(BF16)
    for c in range(wd_ref.shape[0] // FFN_CHUNK):
        gu = _dot(xn, wgu_ref[:, 2 * c * FFN_CHUNK:2 * (c + 1) * FFN_CHUNK])
        g, u = gu[:, :FFN_CHUNK], gu[:, FFN_CHUNK:]
        act_ref[:, c * FFN_CHUNK:(c + 1) * FFN_CHUNK] = (g * jax.nn.sigmoid(g) * u).astype(BF16)
        if between is not None:
            between()
    y3 = y2 + _dot(act_ref[...], wd_ref[...])
    return _rmsnorm(y3, gfin_ref[...]) if final else y3


def _post_prompt_kernel(sink_ref, q_ref, kc_ref, kp_ref, vc_ref, vp_ref, x_ref, hm_ref, mk_ref, mv_ref,
                        wo_ref, gcr_ref, wcq_ref, wco_ref, gf_ref, wgu_ref, wd_ref, gfin_ref,
                        o_ref, act_ref, att_ref, *, final, ntiles, tiles_per_seq):
    s = pl.program_id(0)
    tile = jnp.minimum(s, ntiles - 1)
    first_block = (tile % tiles_per_seq) == 0
    slot = s % 2

    def swa_stages():
        def store(j, value):
            att_ref[slot, j * WINDOW:(j + 1) * WINDOW, :] = value.astype(att_ref.dtype)
        return _swa_prompt_stages(sink_ref, q_ref, kc_ref, kp_ref, vc_ref, vp_ref, first_block, store)

    @pl.when(s == 0)
    def _():
        for _ in swa_stages():
            pass

    @pl.when(s > 0)
    def _():
        swa = swa_stages()

        def advance(n=SWA_STAGES_PER_FFN_CHUNK):
            for _ in range(n):
                next(swa, None)

        y1, qc = _mix_stage(x_ref[...], att_ref[1 - slot], hm_ref[...], wo_ref, gcr_ref, wcq_ref)
        advance()
        o = _cross_attention(qc.astype(BF16), mk_ref[...].astype(BF16), mv_ref[...].astype(BF16))
        advance()
        o_ref[...] = _ffn_stage(y1, o, wco_ref, gf_ref, wgu_ref, wd_ref, act_ref, gfin_ref, final,
                                between=advance)
        for _ in swa:
            pass


_MIX_WEIGHTS = ("w_out", "g_cross", "w_cq")
_FFN_WEIGHTS = ("w_co", "g_ffn", "w_gate_up", "w_down", "g_final")


def _post_prompt(x, q, k, v, hm, memk, memv, sinks, w, nbatch, final, tm):
    t, d = x.shape
    ntiles = t // tm
    tpb = ntiles // nbatch
    wpt = tm // WINDOW
    m = memk.shape[0] // nbatch
    hidden = w["w_down"].shape[0]

    def attn_tile(s):
        return jnp.minimum(s, ntiles - 1)

    def post_tile(s):
        return jnp.maximum(s - 1, 0)

    def attn_rows(width):
        return pl.BlockSpec((tm, width), lambda s: (attn_tile(s), 0))

    def prev_window(width):
        return pl.BlockSpec((WINDOW, width), lambda s: (jnp.maximum(attn_tile(s) * wpt - 1, 0), 0))

    def post_rows(width):
        return pl.BlockSpec((tm, width), lambda s: (post_tile(s), 0))

    mem_spec = pl.BlockSpec((m, CROSS_WIDTH), lambda s: (post_tile(s) // tpb, 0))
    names = _MIX_WEIGHTS + _FFN_WEIGHTS
    return pl.pallas_call(
        functools.partial(_post_prompt_kernel, final=final, ntiles=ntiles, tiles_per_seq=tpb),
        grid=(ntiles + 1,),
        in_specs=[
            pl.BlockSpec(memory_space=pltpu.SMEM),
            attn_rows(ATT_WIDTH), attn_rows(KV_WIDTH), prev_window(KV_WIDTH),
            attn_rows(KV_WIDTH), prev_window(KV_WIDTH),
            post_rows(d), post_rows(MLSTM_WIDTH), mem_spec, mem_spec,
        ] + _weight_specs(w, names),
        out_specs=post_rows(d),
        out_shape=jax.ShapeDtypeStruct((t, d), F32),
        scratch_shapes=[pltpu.VMEM((tm, hidden), BF16), pltpu.VMEM((2, tm, ATT_WIDTH), BF16)],
        compiler_params=pltpu.CompilerParams(
            dimension_semantics=("arbitrary",), vmem_limit_bytes=VMEM_LIMIT),
    )(sinks, q, k, k, v, v, x, hm, memk, memv, *[w[n] for n in names])


def _mix_kernel(x_ref, att_ref, hm_ref, wo_ref, gcr_ref, wcq_ref, y_ref, qc_ref):
    y1, qc = _mix_stage(x_ref[...], att_ref[...], hm_ref[...], wo_ref, gcr_ref, wcq_ref)
    y_ref[...] = y1
    qc_ref[...] = qc


def _mix(x, att, hm, w, tm):
    t, d = x.shape

    def rows(width):
        return pl.BlockSpec((tm, width), lambda i: (i, 0))

    return pl.pallas_call(
        _mix_kernel,
        grid=(t // tm,),
        in_specs=[rows(d), rows(ATT_WIDTH), rows(MLSTM_WIDTH)] + _weight_specs(w, _MIX_WEIGHTS),
        out_specs=[rows(d), rows(CROSS_WIDTH)],
        out_shape=[jax.ShapeDtypeStruct((t, d), F32), jax.ShapeDtypeStruct((t, CROSS_WIDTH), F32)],
        compiler_params=pltpu.CompilerParams(
            dimension_semantics=("arbitrary",), vmem_limit_bytes=VMEM_LIMIT),
    )(x, att, hm, *[w[n] for n in _MIX_WEIGHTS])


def _cross_sample_kernel(qc_ref, mk_ref, mv_ref, o_ref, *, nb, tq):
    mem = CROSS_WIDTH
    n = CROSS_HEADS * tq
    row_head = lax.broadcasted_iota(jnp.int32, (n, CROSS_WIDTH), 0) >> _log2(tq)
    lane_head = lax.broadcasted_iota(jnp.int32, (n, CROSS_WIDTH), 1) >> _log2(HEAD_DIM)
    own = row_head == lane_head
    for b in range(nb):
        q = qc_ref[b * tq:(b + 1) * tq, :]
        qx = jnp.where(own, jnp.concatenate([q] * CROSS_HEADS, axis=0), 0.0).astype(BF16)
        mk = mk_ref[b * mem:(b + 1) * mem, :].astype(BF16)
        mv = mv_ref[b * mem:(b + 1) * mem, :].astype(BF16)
        s = _dot(qx, mk) * (HEAD_DIM ** -0.5)
        p = jnp.exp(s - jnp.max(s, axis=-1, keepdims=True))
        pv = _dot_nt(p.astype(BF16), mv) / jnp.sum(p, axis=-1, keepdims=True)
        pv = jnp.where(own, pv, 0.0)
        out = pv[0:tq]
        for h in range(1, CROSS_HEADS):
            out = out + pv[h * tq:(h + 1) * tq]
        o_ref[b * tq:(b + 1) * tq, :] = out


def _cross_sample(qc, memk_t, memv_t, nbatch, nb):
    t = qc.shape[0]
    tq = t // nbatch
    mem = memk_t.shape[1]
    tok = pl.BlockSpec((nb * tq, CROSS_WIDTH), lambda i: (i, 0))
    mspec = pl.BlockSpec((nb * CROSS_WIDTH, mem), lambda i: (i, 0))
    return pl.pallas_call(
        functools.partial(_cross_sample_kernel, nb=nb, tq=tq),
        grid=(nbatch // nb,),
        in_specs=[tok, mspec, mspec],
        out_specs=tok,
        out_shape=jax.ShapeDtypeStruct((t, CROSS_WIDTH), F32),
        compiler_params=pltpu.CompilerParams(
            dimension_semantics=("arbitrary",), vmem_limit_bytes=VMEM_LIMIT),
    )(qc, memk_t, memv_t)


def _ffn_kernel(y_ref, o_ref, wco_ref, gf_ref, wgu_ref, wd_ref, gfin_ref, out_ref, act_ref, *, final):
    out_ref[...] = _ffn_stage(y_ref[...], o_ref[...], wco_ref, gf_ref, wgu_ref, wd_ref, act_ref,
                              gfin_ref, final)


def _ffn(y1, o, w, final, tm):
    t, d = y1.shape
    hidden = w["w_down"].shape[0]

    def rows(width):
        return pl.BlockSpec((tm, width), lambda i: (i, 0))

    return pl.pallas_call(
        functools.partial(_ffn_kernel, final=final),
        grid=(t // tm,),
        in_specs=[rows(d), rows(CROSS_WIDTH)] + _weight_specs(w, _FFN_WEIGHTS),
        out_specs=rows(d),
        out_shape=jax.ShapeDtypeStruct((t, d), F32),
        scratch_shapes=[pltpu.VMEM((tm, hidden), BF16)],
        compiler_params=pltpu.CompilerParams(
            dimension_semantics=("arbitrary",), vmem_limit_bytes=VMEM_LIMIT),
    )(y1, o, *[w[n] for n in _FFN_WEIGHTS])


def _layer_weights(l, w_in, b_igate, b_fgate, g_mlstm_head, w_out, g_mix, g_cross, g_mem, w_cq,
                   w_ck, w_cv, w_co, g_ffn, w_gate, w_up, w_down, g_final):
    d = w_in.shape[1]
    gate_w = w_in[l][:, MAIN_WIDTH:]
    gate_b = jnp.concatenate([b_igate[l], b_fgate[l]]).astype(F32)
    ngate = gate_b.shape[0]
    row = lambda a: a.astype(F32).reshape(1, -1)
    w_t = w_in[l].T
    k0 = ATT_WIDTH
    mk0 = ATT_WIDTH + 2 * KV_WIDTH + MQK_WIDTH
    return {
        "w_kvk_t": jnp.concatenate([w_t[k0:k0 + 2 * KV_WIDTH], w_t[mk0:mk0 + MQK_WIDTH]]).astype(BF16),
        "w_main": w_in[l][:, :MAIN_WIDTH].astype(BF16),
        "w_gate_cols": jnp.pad(gate_w, ((0, 0), (0, LANES - ngate))).astype(BF16),
        "w_gate_rows": jnp.pad(gate_w.T, ((0, GATE_ROWS - ngate), (0, 0))).astype(BF16),
        "b_gate_cols": jnp.pad(gate_b, (0, LANES - ngate)).reshape(1, LANES),
        "b_gate_rows": jnp.pad(gate_b, (0, GATE_ROWS - ngate)).reshape(GATE_ROWS, 1),
        "g_mix": row(g_mix[l]), "g_cross": row(g_cross[l]), "g_mem": row(g_mem[l]),
        "g_ffn": row(g_ffn[l]), "g_final": row(g_final), "g_head": row(g_mlstm_head[l]),
        "w_out": w_out[l].astype(BF16), "w_cq": w_cq[l].astype(BF16), "w_ck": w_ck[l].astype(BF16),
        "w_cv": w_cv[l].astype(BF16), "w_co": w_co[l].astype(BF16),
        "w_gate_up": _interleave_gate_up(w_gate[l].astype(F32), w_up[l].astype(F32)),
        "w_down": w_down[l].astype(BF16),
    }


TOKEN_TILE = 512
MLSTM_STAGES_PER_PROJ_STAGE = 15
SWA_STAGES_PER_FFN_CHUNK = 3
MLSTM_CHUNK_ROWS = 256
SWA_SAMPLE_BATCH = 16
MLSTM_SAMPLE_BATCH = 16
CROSS_SAMPLE_BATCH = 16


def kernel(x_prompt, x_sample, mem_prompt, cache_swa_k, cache_swa_v, state_mlstm_C, state_mlstm_n,
           state_mlstm_m, cache_mem_k, cache_mem_v, w_in, b_igate, b_fgate, attn_sinks,
           g_mlstm_head, w_out, g_mix, g_cross, g_mem, w_cq, w_ck, w_cv, w_co, g_ffn, w_gate,
           w_up, w_down, g_final):
    depth = w_in.shape[0]
    bp, sp, d = x_prompt.shape
    bs, ss, _ = x_sample.shape
    mem_tokens = mem_prompt.shape[1]
    past = cache_swa_k.shape[2]
    yp = x_prompt.reshape(bp * sp, d)
    ys = x_sample.reshape(bs * ss, d)
    mem = mem_prompt.reshape(bp * mem_tokens, d)
    outs = [[] for _ in range(12)]

    for l in range(depth):
        final = l == depth - 1
        w = _layer_weights(l, w_in, b_igate, b_fgate, g_mlstm_head, w_out, g_mix, g_cross, g_mem,
                           w_cq, w_ck, w_cv, w_co, g_ffn, w_gate, w_up, w_down, g_final)
        sinks = attn_sinks[l].astype(F32)

        q, k, v, hm, c_p, n_p, m_p = _mixer_prompt(yp, w, bp, MLSTM_CHUNK_ROWS)
        memk, memv = _memkv(mem, w, bp)
        yp = _post_prompt(yp, q, k, v, hm, memk, memv, sinks, w, bp, final, TOKEN_TILE)
        win_shape = (bp, WINDOW, ATT_KV_HEADS, HEAD_DIM)
        outs[0].append(k.reshape(bp, sp, KV_WIDTH)[:, sp - WINDOW:].reshape(win_shape))
        outs[1].append(v.reshape(bp, sp, KV_WIDTH)[:, sp - WINDOW:].reshape(win_shape))
        outs[2].append(c_p)
        outs[3].append(n_p)
        outs[4].append(m_p)
        mem_shape = (bp, mem_tokens, CROSS_HEADS, HEAD_DIM)
        outs[5].append(memk.reshape(mem_shape))
        outs[6].append(memv.reshape(mem_shape))

        def tokens_last(a):
            return jnp.transpose(a.astype(F32), (0, 2, 3, 1)).reshape(-1, a.shape[1])

        q, _, _, mq, mk, mv, og, grow, gcol, knt, vnt, mkt = _proj(
            ys, w, 1, F32, TOKEN_TILE, decode=True)
        att, kbuf_t, vbuf_t = _swa_sample(
            q, knt, vnt, tokens_last(cache_swa_k[l]), tokens_last(cache_swa_v[l]), sinks, bs,
            SWA_SAMPLE_BATCH)
        m_rows = jnp.pad(jnp.repeat(state_mlstm_m[l].astype(F32), ss, axis=0),
                         ((0, 0), (0, LANES - MLSTM_HEADS)))
        ct_in = jnp.swapaxes(state_mlstm_C[l].astype(F32), 2, 3).reshape(-1, MLSTM_DV)
        hm, ct_s, n_s, mt = _mlstm_sample(
            mq, mk, mkt, mv, og, gcol, grow, m_rows, ct_in,
            state_mlstm_n[l].astype(F32).reshape(bs * MLSTM_HEADS, 1, MLSTM_DK),
            w["g_head"], bs, MLSTM_SAMPLE_BATCH)
        y1, qc = _mix(ys, att, hm, w, TOKEN_TILE)
        o = _cross_sample(qc, tokens_last(cache_mem_k[l]), tokens_last(cache_mem_v[l]), bs,
                          CROSS_SAMPLE_BATCH)
        ys = _ffn(y1, o, w, final, TOKEN_TILE)
        buf_t_shape = (bs, ATT_KV_HEADS, HEAD_DIM, past)
        outs[7].append(jnp.transpose(kbuf_t.reshape(buf_t_shape), (0, 3, 1, 2)))
        outs[8].append(jnp.transpose(vbuf_t.reshape(buf_t_shape), (0, 3, 1, 2)))
        outs[9].append(jnp.swapaxes(ct_s.reshape(bs, MLSTM_HEADS, MLSTM_DK, MLSTM_DV), 2, 3))
        outs[10].append(n_s.reshape(bs, MLSTM_HEADS, MLSTM_DK))
        outs[11].append(mt.reshape(bs, ss, LANES)[:, ss - 1, :MLSTM_HEADS])

    return (yp.reshape(bp, sp, d), ys.reshape(bs, ss, d)) + tuple(jnp.stack(o) for o in outs)
```

```python
import functools

import jax
import jax.numpy as jnp
from jax import lax
from jax.experimental import pallas as pl
from jax.experimental.pallas import tpu as pltpu

F32 = jnp.float32
BF16 = jnp.bfloat16

HEAD_DIM = 64
ATT_GROUP = 4
ATT_KV_HEADS = 2
ATT_WIDTH = 512
KV_WIDTH = ATT_KV_HEADS * HEAD_DIM
WINDOW = 128
LOG_WINDOW = 7
MLSTM_HEADS = 4
MLSTM_DV = 128
MLSTM_DK = 64
MQK_WIDTH = MLSTM_HEADS * MLSTM_DK
MLSTM_WIDTH = MLSTM_HEADS * MLSTM_DV
MAIN_WIDTH = ATT_WIDTH + 2 * KV_WIDTH + 2 * MQK_WIDTH + 2 * MLSTM_WIDTH
CROSS_HEADS = 4
CROSS_WIDTH = CROSS_HEADS * HEAD_DIM
EPS = 1e-6
NEG_INF = float("-inf")
assert HEAD_DIM == 4 ** 3

LANES = 128
GATE_ROWS = 16
VMEM_LIMIT = 56 * 1024 * 1024

NT = (((1,), (1,)), ((), ()))
TN = (((0,), (0,)), ((), ()))


def _dot(a, b):
    return jnp.dot(a, b, preferred_element_type=F32)


def _dot_nt(a, b):
    return lax.dot_general(a, b, NT, preferred_element_type=F32)


def _dot_tn(a, b):
    return lax.dot_general(a, b, TN, preferred_element_type=F32)


def _rmsnorm(x, g):
    return x * lax.rsqrt(jnp.mean(x * x, axis=-1, keepdims=True) + EPS) * g


def _log_sigmoid(x):
    return jnp.minimum(x, 0.0) - jnp.log1p(jnp.exp(-jnp.abs(x)))


def _split3(x):
    hi = x.astype(BF16)
    r1 = x - hi.astype(F32)
    mid = r1.astype(BF16)
    lo = (r1 - mid.astype(F32)).astype(BF16)
    return hi, mid, lo


def _cumsum_cols(tri, x):
    hi, mid, lo = _split3(x)
    return _dot(tri, hi) + _dot(tri, mid) + _dot(tri, lo)


def _cumsum_rows(x, tri):
    hi, mid, lo = _split3(x)
    return _dot(hi, tri) + _dot(mid, tri) + _dot(lo, tri)


def _log2(n):
    assert n > 0 and n & (n - 1) == 0, n
    return n.bit_length() - 1


def _low_half(shape):
    return lax.broadcasted_iota(jnp.int32, shape, 1) < HEAD_DIM


def _proj_kernel(x_ref, g_ref, w_ref, wgr_ref, br_ref, *rest, decode):
    if decode:
        wgc_ref, bc_ref, wt_ref = rest[:3]
        rest = rest[3:]
    q_ref, k_ref, v_ref, mq_ref, mk_ref, mv_ref, og_ref, gr_ref = rest[:8]
    xn = _rmsnorm(x_ref[...], g_ref[...]).astype(BF16)
    if decode:
        gc_ref, kt_ref, vt_ref, mkt_ref = rest[8:]
        zt = _dot_nt(wt_ref[...], xn)
        kt_ref[...] = zt[:KV_WIDTH]
        vt_ref[...] = zt[KV_WIDTH:2 * KV_WIDTH]
        mkt_ref[...] = zt[2 * KV_WIDTH:] * (MLSTM_DK ** -0.5)
        zc = _dot(xn, wgc_ref[...]) + bc_ref[...]
        lane = lax.broadcasted_iota(jnp.int32, zc.shape, 1)
        gc_ref[...] = jnp.where(lane < MLSTM_HEADS, zc, _log_sigmoid(zc))

    def mm(lo, hi):
        return _dot(xn, w_ref[:, lo:hi])

    o = 0
    q_ref[...] = mm(o, o + ATT_WIDTH).astype(q_ref.dtype)
    o += ATT_WIDTH
    kv = mm(o, o + 2 * KV_WIDTH)
    k_ref[...] = kv[:, :KV_WIDTH]
    v_ref[...] = kv[:, KV_WIDTH:]
    o += 2 * KV_WIDTH
    mqk = mm(o, o + 2 * MQK_WIDTH)
    mq_ref[...] = mqk[:, :MQK_WIDTH].astype(mq_ref.dtype)
    mk_ref[...] = (mqk[:, MQK_WIDTH:] * (MLSTM_DK ** -0.5)).astype(mk_ref.dtype)
    o += 2 * MQK_WIDTH
    mv_ref[...] = mm(o, o + MLSTM_WIDTH).astype(mv_ref.dtype)
    o += MLSTM_WIDTH
    og_ref[...] = mm(o, o + MLSTM_WIDTH).astype(og_ref.dtype)

    zr = _dot_nt(wgr_ref[...], xn) + br_ref[...]
    row = lax.broadcasted_iota(jnp.int32, zr.shape, 0)
    gr_ref[0] = jnp.where(row < MLSTM_HEADS, zr, _log_sigmoid(zr))


def _const_spec(shape):
    nd = len(shape)
    return pl.BlockSpec(shape, lambda *_: (0,) * nd, pipeline_mode=pl.Buffered(1))


def _weight_specs(w, names):
    return [_const_spec(w[n].shape) for n in names]


def _proj(x, w, nseg, act_dtype, tm, decode=False):
    t, d = x.shape
    seg = t // nseg
    tpb = seg // tm
    grid = (t // tm,)

    def rows(width):
        return pl.BlockSpec((tm, width), lambda i: (i, 0))

    out_shape = [
        jax.ShapeDtypeStruct((t, ATT_WIDTH), act_dtype),
        jax.ShapeDtypeStruct((t, KV_WIDTH), F32),
        jax.ShapeDtypeStruct((t, KV_WIDTH), F32),
        jax.ShapeDtypeStruct((t, MQK_WIDTH), act_dtype),
        jax.ShapeDtypeStruct((t, MQK_WIDTH), act_dtype),
        jax.ShapeDtypeStruct((t, MLSTM_WIDTH), act_dtype),
        jax.ShapeDtypeStruct((t, MLSTM_WIDTH), act_dtype),
        jax.ShapeDtypeStruct((nseg, GATE_ROWS, seg), F32),
    ]
    out_specs = [
        rows(ATT_WIDTH), rows(KV_WIDTH), rows(KV_WIDTH), rows(MQK_WIDTH), rows(MQK_WIDTH),
        rows(MLSTM_WIDTH), rows(MLSTM_WIDTH),
        pl.BlockSpec((1, GATE_ROWS, tm), lambda i: (i // tpb, 0, i % tpb)),
    ]
    names = ["g_mix", "w_main", "w_gate_rows", "b_gate_rows"]
    if decode:
        names += ["w_gate_cols", "b_gate_cols", "w_kvk_t"]
        out_shape.append(jax.ShapeDtypeStruct((t, LANES), F32))
        out_specs.append(rows(LANES))
        for width in (KV_WIDTH, KV_WIDTH, MQK_WIDTH):
            out_shape.append(jax.ShapeDtypeStruct((width, t), F32))
            out_specs.append(pl.BlockSpec((width, tm), lambda i: (0, i)))
    in_specs = [rows(d)] + _weight_specs(w, names)
    return pl.pallas_call(
        functools.partial(_proj_kernel, decode=decode),
        grid=grid,
        in_specs=in_specs,
        out_specs=out_specs,
        out_shape=out_shape,
        compiler_params=pltpu.CompilerParams(
            dimension_semantics=("arbitrary",), vmem_limit_bytes=VMEM_LIMIT),
    )(x, *[w[n] for n in names])


def _dup_heads(x):
    swapped = pltpu.roll(x, HEAD_DIM, axis=1)
    lo = _low_half(x.shape)
    return (jnp.where(lo, x, swapped).astype(BF16), jnp.where(lo, swapped, x).astype(BF16))


def _stack_query_heads(q, h):
    parts = []
    for g in range(ATT_GROUP):
        hq = h * ATT_GROUP + g
        slab = q[:, (hq // 2) * LANES:(hq // 2 + 1) * LANES]
        lo = _low_half(slab.shape)
        keep = lo if hq % 2 == 0 else jnp.logical_not(lo)
        parts.append(jnp.where(keep, slab, jnp.zeros_like(slab)))
    return jnp.concatenate(parts, axis=0).astype(BF16)


def _sink_column(sink_ref, h, rows_per_head):
    n = ATT_GROUP * rows_per_head
    grp = lax.broadcasted_iota(jnp.int32, (n, 1), 0) >> _log2(rows_per_head)
    col = jnp.full((n, 1), sink_ref[h * ATT_GROUP], F32)
    for g in range(1, ATT_GROUP):
        col = jnp.where(grp == g, sink_ref[h * ATT_GROUP + g], col)
    return col


def _swa_prompt_stages(sink_ref, q_ref, kc_ref, kp_ref, vc_ref, vp_ref, first_block, store):
    qb = q_ref.shape[0]
    k_all = jnp.concatenate([kp_ref[...], kc_ref[...]], axis=0)
    v_all = jnp.concatenate([vp_ref[...], vc_ref[...]], axis=0)
    kd = _dup_heads(k_all)
    v_t = v_all.T.astype(BF16)

    n = ATT_GROUP * WINDOW
    key = lax.broadcasted_iota(jnp.int32, (2 * WINDOW, n), 0)
    qry = lax.broadcasted_iota(jnp.int32, (2 * WINDOW, n), 1) & (WINDOW - 1)
    delta = key - qry
    in_band = (delta >= 0) & (delta <= WINDOW)
    grp = lax.broadcasted_iota(jnp.int32, (1, n), 1) >> LOG_WINDOW
    sinks = []
    for h in range(ATT_KV_HEADS):
        row = jnp.full((1, n), sink_ref[h * ATT_GROUP], F32)
        for g in range(1, ATT_GROUP):
            row = jnp.where(grp == g, sink_ref[h * ATT_GROUP + g], row)
        sinks.append(row)
    yield

    probs = []
    for j in range(qb // WINDOW):
        valid = in_band & (key >= jnp.where(first_block, WINDOW, 0)) if j == 0 else in_band
        q = q_ref[j * WINDOW:(j + 1) * WINDOW, :] * (HEAD_DIM ** -0.5)
        band = slice(j * WINDOW, (j + 2) * WINDOW)
        for h in range(ATT_KV_HEADS):
            qh = _stack_query_heads(q, h)
            s = _dot_nt(kd[h][band], qh)
            probs.append(dict(h=h, band=band, s=jnp.where(valid, s, NEG_INF)))
            yield
    for p in probs:
        p["m"] = jnp.maximum(jnp.max(p["s"], axis=0, keepdims=True), sinks[p["h"]])
        yield
    for p in probs:
        p["p"] = jnp.exp(p["s"] - p["m"])
        yield
    for p in probs:
        h = p["h"]
        denom = jnp.sum(p["p"], axis=0, keepdims=True) + jnp.exp(sinks[h] - p["m"])
        p["o_t"] = _dot(v_t[h * HEAD_DIM:(h + 1) * HEAD_DIM, p["band"]], p["p"].astype(BF16)) / denom
        yield
    for j in range(qb // WINDOW):
        pieces = [p["o_t"][:, g * WINDOW:(g + 1) * WINDOW]
                  for p in probs[j * ATT_KV_HEADS:(j + 1) * ATT_KV_HEADS] for g in range(ATT_GROUP)]
        store(j, jnp.concatenate(pieces, axis=0).T)
        yield


def _swa_sample_kernel(sink_ref, q_ref, knt_ref, vnt_ref, ck_ref, cv_ref,
                       o_ref, kbuf_ref, vbuf_ref, *, nb, tq):
    past = ck_ref.shape[1]
    hw = ATT_KV_HEADS * HEAD_DIM
    heads = ATT_KV_HEADS * ATT_GROUP
    n = heads * tq
    fresh = past - tq
    t = lax.broadcasted_iota(jnp.int32, (n, 2 * past), 0) & (tq - 1)
    col = lax.broadcasted_iota(jnp.int32, (n, 2 * past), 1)
    valid = ((col < past) & (col >= t)) | ((col >= past + fresh) & (col - (past + fresh) <= t))
    sink = jnp.concatenate([_sink_column(sink_ref, h, tq) for h in range(ATT_KV_HEADS)], axis=0)
    lane = lax.broadcasted_iota(jnp.int32, (hw, past), 1)
    knt = knt_ref[...]
    vnt = vnt_ref[...]
    zero_slab = jnp.zeros((tq, LANES), F32)
    lo = _low_half((tq, LANES))

    def shifted(old, new_t, b):
        return jnp.where(lane >= fresh, pltpu.roll(new_t, (fresh - b * tq) % past, axis=1),
                         pltpu.roll(old, fresh, axis=1))

    def dup_rows(x):
        h0, h1 = x[:HEAD_DIM], x[HEAD_DIM:]
        return jnp.concatenate([h0, h0, h1, h1], axis=0)

    for b in range(nb):
        k_old = ck_ref[b * hw:(b + 1) * hw, :]
        v_old = cv_ref[b * hw:(b + 1) * hw, :]
        k_new = shifted(k_old, knt, b)
        v_new = shifted(v_old, vnt, b)
        kbuf_ref[b * hw:(b + 1) * hw, :] = k_new
        vbuf_ref[b * hw:(b + 1) * hw, :] = v_new
        kop = jnp.concatenate([dup_rows(k_old), dup_rows(k_new)], axis=1).astype(BF16)
        vop = jnp.concatenate([dup_rows(v_old), dup_rows(v_new)], axis=1).astype(BF16)

        q = q_ref[b * tq:(b + 1) * tq, :]
        blocks = []
        for hq in range(heads):
            slab = q[:, (hq // 2) * LANES:(hq // 2 + 1) * LANES]
            slab = jnp.where(lo if hq % 2 == 0 else jnp.logical_not(lo), slab, 0.0)
            pair = [slab, zero_slab] if hq // ATT_GROUP == 0 else [zero_slab, slab]
            blocks.append(jnp.concatenate(pair, axis=1))
        qx = jnp.concatenate(blocks, axis=0).astype(BF16)

        s = _dot(qx, kop) * (HEAD_DIM ** -0.5)
        s = jnp.where(valid, s, NEG_INF)
        m = jnp.maximum(jnp.max(s, axis=-1, keepdims=True), sink)
        p = jnp.exp(s - m)
        denom = jnp.sum(p, axis=-1, keepdims=True) + jnp.exp(sink - m)
        o = _dot_nt(p.astype(BF16), vop) / denom

        slabs = []
        for j in range(heads // 2):
            h = (2 * j) // ATT_GROUP
            even = o[(2 * j) * tq:(2 * j + 1) * tq, h * LANES:(h + 1) * LANES]
            odd = o[(2 * j + 1) * tq:(2 * j + 2) * tq, h * LANES:(h + 1) * LANES]
            slabs.append(jnp.where(lo, even, odd))
        o_ref[b * tq:(b + 1) * tq, :] = jnp.concatenate(slabs, axis=1)


def _swa_sample(q, knt, vnt, cache_kt, cache_vt, sinks, nbatch, nb):
    t = q.shape[0]
    tq = t // nbatch
    past = cache_kt.shape[1]
    hw = ATT_KV_HEADS * HEAD_DIM
    assert past == WINDOW and tq & (tq - 1) == 0 and nb * tq == past

    def rows(r, width):
        return pl.BlockSpec((r, width), lambda i: (i, 0))

    new_t = pl.BlockSpec((hw, nb * tq), lambda i: (0, i))
    return pl.pallas_call(
        functools.partial(_swa_sample_kernel, nb=nb, tq=tq),
        grid=(nbatch // nb,),
        in_specs=[
            pl.BlockSpec(memory_space=pltpu.SMEM),
            rows(nb * tq, ATT_WIDTH), new_t, new_t, rows(nb * hw, past), rows(nb * hw, past),
        ],
        out_specs=[rows(nb * tq, ATT_WIDTH), rows(nb * hw, past), rows(nb * hw, past)],
        out_shape=[
            jax.ShapeDtypeStruct((t, ATT_WIDTH), F32),
            jax.ShapeDtypeStruct(cache_kt.shape, F32),
            jax.ShapeDtypeStruct(cache_vt.shape, F32),
        ],
        compiler_params=pltpu.CompilerParams(
            dimension_semantics=("arbitrary",), vmem_limit_bytes=VMEM_LIMIT),
    )(sinks, q, knt, vnt, cache_kt, cache_vt)


def _mlstm_intra(q, k, v, it_c, bt_c, it_r, bt_r, m_c, valid):
    d = jnp.where(valid, bt_c - bt_r + it_r, NEG_INF)
    inter = bt_c + m_c
    mt = jnp.maximum(jnp.max(d, axis=-1, keepdims=True), inter)
    sm = _dot_nt(q, k) * jnp.exp(d - mt)
    wi = jnp.exp(inter - mt)
    return _dot(sm.astype(BF16), v), jnp.sum(sm, axis=-1, keepdims=True), mt, wi


def _mlstm_head_out(num, den, mt, og, g_head):
    lower = jnp.maximum(jnp.abs(den), jnp.exp(-mt))
    h = num / lower
    hn = h * lax.rsqrt(jnp.mean(h * h, axis=-1, keepdims=True) + EPS)
    return hn * g_head * jax.nn.sigmoid(og)


def _mlstm_prompt_stages(mq, mk, mv, og, gr, gh_ref, hm_ref, c_ref, n_ref, m_ref, nbatch, chunk):
    s_idx = lax.broadcasted_iota(jnp.int32, (chunk, chunk), 0)
    l_idx = lax.broadcasted_iota(jnp.int32, (chunk, chunk), 1)
    causal = s_idx <= l_idx
    triu = jnp.where(causal, 1.0, 0.0).astype(BF16)
    lo = _low_half((chunk, LANES))
    pad_rows = jnp.zeros((LANES - MLSTM_HEADS, chunk), F32)
    sub8 = (GATE_ROWS, chunk)

    probs = []
    for b in range(nbatch):
        g = gr(b)
        btr = _cumsum_rows(g, triu)
        it_rows = g[:MLSTM_HEADS]
        bt_rows = btr[MLSTM_HEADS:2 * MLSTM_HEADS]
        u_cols = jnp.concatenate([it_rows - bt_rows, pad_rows], axis=0).T
        for h in range(MLSTM_HEADS):
            slab = slice((h // 2) * LANES, (h // 2 + 1) * LANES)
            keep = lo if h % 2 == 0 else jnp.logical_not(lo)
            q = mq(b, slab)
            k = mk(b, slab)
            probs.append(dict(
                b=b, h=h, idx=b * MLSTM_HEADS + h, cols=slice(h * MLSTM_DV, (h + 1) * MLSTM_DV),
                q=jnp.where(keep, q, jnp.zeros_like(q)), k=jnp.where(keep, k, jnp.zeros_like(k)),
                u_c=u_cols[:, h:h + 1], it_r=it_rows[h:h + 1, :], bt_r=bt_rows[h:h + 1, :]))
        yield

    for p in probs:
        p["m_old"] = m_ref[p["idx"]][:, 0:1]
        p["d"] = jnp.where(causal, p["bt_r"] + p["u_c"], NEG_INF)
        p["inter"] = p["bt_r"] + p["m_old"]
        p["kq"] = _dot_nt(p["k"], p["q"])
        p["q_t"] = p["q"].astype(F32).T.astype(BF16)
        p["v_t"] = mv(p["b"], p["cols"]).astype(F32).T
        yield
    for p in probs:
        p["mt"] = jnp.maximum(jnp.max(p["d"], axis=0, keepdims=True), p["inter"])
        yield
    for p in probs:
        p["sm"] = p["kq"] * jnp.exp(p["d"] - p["mt"])
        p["wi"] = jnp.exp(p["inter"] - p["mt"])
        yield
    for p in probs:
        c_old = c_ref[p["idx"]]
        n_old = jnp.broadcast_to(n_ref[p["idx"]], (sub8[0], LANES))
        num = (_dot(p["v_t"].astype(BF16), p["sm"].astype(BF16))
               + p["wi"] * _dot(c_old.astype(BF16), p["q_t"]))
        den = (jnp.sum(p["sm"], axis=0, keepdims=True)
               + p["wi"] * _dot(n_old.astype(BF16), p["q_t"])[0:1])
        lower = jnp.maximum(jnp.abs(den), jnp.exp(-p["mt"]))
        h_t = num / lower
        p["hn"] = (h_t * lax.rsqrt(jnp.mean(h_t * h_t, axis=0, keepdims=True) + EPS)).T
        yield
    for p in probs:
        gate = jax.nn.sigmoid(og(p["b"], p["cols"]).astype(F32))
        hm_ref[p["b"], :, p["cols"]] = (p["hn"] * gh_ref[:, p["cols"]] * gate).astype(hm_ref.dtype)
        yield
    for p in probs:
        idx = p["idx"]
        b_last = p["bt_r"][:, chunk - 1:chunk]
        m_new = p["mt"][:, chunk - 1:chunk]
        wk = jnp.exp(b_last - p["bt_r"] + p["it_r"] - m_new)
        wc = jnp.exp(b_last + p["m_old"] - m_new)
        c_ref[idx] = wc * c_ref[idx] + _dot((p["v_t"] * wk).astype(BF16), p["k"])
        n_ref[idx] = wc * n_ref[idx] + _dot(jnp.broadcast_to(wk, sub8).astype(BF16), p["k"])[0:1]
        m_ref[idx] = jnp.broadcast_to(m_new, (1, LANES))
        yield


def _mixer_prompt_kernel(x_ref, g_ref, w_ref, wgr_ref, br_ref, gh_ref, wg_ref, wu_ref, wd_ref, wo_ref,
                         q_ref, k_ref, v_ref, hm_ref, c_ref, n_ref, m_ref, wgu_out, wd_out, wo_out,
                         mq_scr, mk_scr, mv_scr, og_scr, gr_scr, *, nbatch, chunk):
    s = pl.program_id(0)
    wr = s % 2
    rd = 1 - wr
    rows = nbatch * chunk

    def cast_weight_chunks():
        for c in range(0, wg_ref.shape[1], FFN_CHUNK):
            wgu_out[:, 2 * c:2 * c + FFN_CHUNK] = wg_ref[:, c:c + FFN_CHUNK].astype(BF16)
            wgu_out[:, 2 * c + FFN_CHUNK:2 * (c + FFN_CHUNK)] = wu_ref[:, c:c + FFN_CHUNK].astype(BF16)
        wd_out[...] = wd_ref[...].astype(BF16)
        wo_out[...] = wo_ref[...].astype(BF16)

    def proj_stages():
        xn = _rmsnorm(x_ref[...].reshape(rows, x_ref.shape[-1]), g_ref[...]).astype(BF16)
        yield

        def mm(lo, hi):
            return _dot(xn, w_ref[:, lo:hi])

        o = 0
        q_ref[...] = mm(o, o + ATT_WIDTH).astype(q_ref.dtype).reshape(q_ref.shape)
        yield
        o += ATT_WIDTH
        kv = mm(o, o + 2 * KV_WIDTH)
        k_ref[...] = kv[:, :KV_WIDTH].reshape(k_ref.shape)
        v_ref[...] = kv[:, KV_WIDTH:].reshape(v_ref.shape)
        yield
        o += 2 * KV_WIDTH
        mqk = mm(o, o + 2 * MQK_WIDTH)
        mq_scr[wr] = mqk[:, :MQK_WIDTH].astype(BF16)
        mk_scr[wr] = (mqk[:, MQK_WIDTH:] * (MLSTM_DK ** -0.5)).astype(BF16)
        yield
        o += 2 * MQK_WIDTH
        mv_scr[wr] = mm(o, o + MLSTM_WIDTH).astype(BF16)
        yield
        o += MLSTM_WIDTH
        og_scr[wr] = mm(o, o + MLSTM_WIDTH).astype(BF16)
        yield
        zr = _dot_nt(wgr_ref[...], xn) + br_ref[...]
        row = lax.broadcasted_iota(jnp.int32, zr.shape, 0)
        gr_scr[wr] = jnp.where(row < MLSTM_HEADS, zr, _log_sigmoid(zr))
        yield

    def mlstm_stages():
        def rows_of(scr):
            return lambda b, lanes: scr[rd, b * chunk:(b + 1) * chunk, lanes]
        return _mlstm_prompt_stages(
            rows_of(mq_scr), rows_of(mk_scr), rows_of(mv_scr), rows_of(og_scr),
            lambda b: gr_scr[rd, :, b * chunk:(b + 1) * chunk],
            gh_ref, hm_ref, c_ref, n_ref, m_ref, nbatch, chunk)

    @pl.when(s == 0)
    def _():
        c_ref[...] = jnp.zeros_like(c_ref)
        n_ref[...] = jnp.zeros_like(n_ref)
        m_ref[...] = jnp.zeros_like(m_ref)
        cast_weight_chunks()
        for _ in proj_stages():
            pass

    @pl.when(s > 0)
    def _():
        cast_weight_chunks()
        mlstm = mlstm_stages()
        for _ in proj_stages():
            for _ in range(MLSTM_STAGES_PER_PROJ_STAGE):
                next(mlstm, None)
        for _ in mlstm:
            pass


def _mixer_prompt(x, w, w_gate, w_up, w_down, w_out, nbatch, chunk):
    t, d = x.shape
    seq = t // nbatch
    nchunks = seq // chunk
    nstate = nbatch * MLSTM_HEADS
    rows = nbatch * chunk
    hidden = w_down.shape[0]
    assert hidden % FFN_CHUNK == 0 and w_out.shape[0] % nchunks == 0
    per_step = next(c for c in range(1, hidden // FFN_CHUNK + 1)
                    if (hidden // FFN_CHUNK) % c == 0 and hidden // FFN_CHUNK <= c * (nchunks + 1))
    ffn_cols = per_step * FFN_CHUNK
    nffn = hidden // ffn_cols
    out_rows = w_out.shape[0] // nchunks

    def proj_chunk(width):
        return pl.BlockSpec((nbatch, chunk, width), lambda s: (0, jnp.minimum(s, nchunks - 1), 0))

    def state(shape):
        return pl.BlockSpec(shape, lambda s: (0, 0, 0))

    def ffn_chunk(s):
        return jnp.minimum(s, nffn - 1)

    def out_chunk(s):
        return jnp.minimum(s, nchunks - 1)

    names = ["g_mix", "w_main", "w_gate_rows", "b_gate_rows", "g_head"]
    q, k, v, hm, c_st, n_st, m_st, w_gate_up_bf, w_down_bf, w_out_bf = pl.pallas_call(
        functools.partial(_mixer_prompt_kernel, nbatch=nbatch, chunk=chunk),
        grid=(nchunks + 1,),
        in_specs=[proj_chunk(d)] + _weight_specs(w, names) + [
            pl.BlockSpec((d, ffn_cols), lambda s: (0, ffn_chunk(s))),
            pl.BlockSpec((d, ffn_cols), lambda s: (0, ffn_chunk(s))),
            pl.BlockSpec((ffn_cols, d), lambda s: (ffn_chunk(s), 0)),
            pl.BlockSpec((out_rows, d), lambda s: (out_chunk(s), 0)),
        ],
        out_specs=[
            proj_chunk(ATT_WIDTH), proj_chunk(KV_WIDTH), proj_chunk(KV_WIDTH),
            pl.BlockSpec((nbatch, chunk, MLSTM_WIDTH), lambda s: (0, jnp.maximum(s - 1, 0), 0)),
            state((nstate, MLSTM_DV, LANES)), state((nstate, 1, LANES)), state((nstate, 1, LANES)),
            pl.BlockSpec((d, 2 * ffn_cols), lambda s: (0, ffn_chunk(s))),
            pl.BlockSpec((ffn_cols, d), lambda s: (ffn_chunk(s), 0)),
            pl.BlockSpec((out_rows, d), lambda s: (out_chunk(s), 0)),
        ],
        out_shape=[
            jax.ShapeDtypeStruct((nbatch, seq, ATT_WIDTH), BF16),
            jax.ShapeDtypeStruct((nbatch, seq, KV_WIDTH), F32),
            jax.ShapeDtypeStruct((nbatch, seq, KV_WIDTH), F32),
            jax.ShapeDtypeStruct((nbatch, seq, MLSTM_WIDTH), BF16),
            jax.ShapeDtypeStruct((nstate, MLSTM_DV, LANES), F32),
            jax.ShapeDtypeStruct((nstate, 1, LANES), F32),
            jax.ShapeDtypeStruct((nstate, 1, LANES), F32),
            jax.ShapeDtypeStruct((d, 2 * hidden), BF16),
            jax.ShapeDtypeStruct((hidden, d), BF16),
            jax.ShapeDtypeStruct(w_out.shape, BF16),
        ],
        scratch_shapes=[
            pltpu.VMEM((2, rows, MQK_WIDTH), BF16), pltpu.VMEM((2, rows, MQK_WIDTH), BF16),
            pltpu.VMEM((2, rows, MLSTM_WIDTH), BF16), pltpu.VMEM((2, rows, MLSTM_WIDTH), BF16),
            pltpu.VMEM((2, GATE_ROWS, rows), F32),
        ],
        compiler_params=pltpu.CompilerParams(
            dimension_semantics=("arbitrary",), vmem_limit_bytes=VMEM_LIMIT),
    )(x.reshape(nbatch, seq, d), *[w[n] for n in names], w_gate, w_up, w_down, w_out)
    q = q.reshape(t, ATT_WIDTH)
    k = k.reshape(t, KV_WIDTH)
    v = v.reshape(t, KV_WIDTH)

    c_st = c_st.reshape(nbatch, MLSTM_HEADS, MLSTM_DV, 2, MLSTM_DK)
    n_st = n_st.reshape(nbatch, MLSTM_HEADS, 2, MLSTM_DK)
    c_fin = jnp.stack([c_st[:, h, :, h % 2, :] for h in range(MLSTM_HEADS)], axis=1)
    n_fin = jnp.stack([n_st[:, h, h % 2, :] for h in range(MLSTM_HEADS)], axis=1)
    m_fin = m_st[:, 0, 0].reshape(nbatch, MLSTM_HEADS)
    bf16_weights = {"w_gate_up": w_gate_up_bf, "w_down": w_down_bf, "w_out": w_out_bf}
    return q, k, v, hm.reshape(t, MLSTM_WIDTH), c_fin, n_fin, m_fin, bf16_weights


def _mlstm_sample_kernel(mq_ref, mk_ref, mkt_ref, mv_ref, og_ref, gc_ref, gr_ref, mrow_ref, c_ref,
                         n_ref, gh_ref, hm_ref, co_ref, no_ref, mt_ref, *, nb, tq):
    rows = nb * tq
    shift = tq.bit_length() - 1
    r = lax.broadcasted_iota(jnp.int32, (rows, rows), 0)
    s = lax.broadcasted_iota(jnp.int32, (rows, rows), 1)
    same = (r >> shift) == (s >> shift)
    valid = same & (s <= r)
    tril = jnp.where(valid, 1.0, 0.0).astype(BF16)
    triu = jnp.where(same & (r <= s), 1.0, 0.0).astype(BF16)
    row_batch = lax.broadcasted_iota(jnp.int32, (rows, 1), 0) >> shift
    lane = lax.broadcasted_iota(jnp.int32, (rows, LANES), 1)

    gc = gc_ref[...]
    gr = gr_ref[0]
    btc = _cumsum_cols(tril, gc)
    btr = _cumsum_rows(gr, triu)
    m_rows = mrow_ref[...]
    mt_all = jnp.zeros((rows, LANES), F32)

    def last_of_batch(col):
        parts = [jnp.broadcast_to(col[(b + 1) * tq - 1:(b + 1) * tq], (tq, 1)) for b in range(nb)]
        return jnp.concatenate(parts, axis=0)

    dk = MLSTM_DK
    lo = _low_half((rows, LANES))
    own_q = row_batch == (lax.broadcasted_iota(jnp.int32, (rows, nb * dk), 1) >> _log2(dk))
    lane_batch = lax.broadcasted_iota(jnp.int32, (dk, rows), 1) >> shift

    def state_rows(b, h):
        return slice((b * MLSTM_HEADS + h) * dk, (b * MLSTM_HEADS + h + 1) * dk)

    for h in range(MLSTM_HEADS):
        slab = slice((h // 2) * LANES, (h // 2 + 1) * LANES)
        qs, ks = mq_ref[:, slab], mk_ref[:, slab]
        qsw, ksw = pltpu.roll(qs, dk, axis=1), pltpu.roll(ks, dk, axis=1)
        q2 = jnp.where(lo, qs, qsw) if h % 2 == 0 else jnp.where(lo, qsw, qs)
        k2 = jnp.where(lo, ks, ksw) if h % 2 == 0 else jnp.where(lo, ksw, ks)
        qb, kb = q2[:, :dk].astype(BF16), k2[:, :dk].astype(BF16)
        vb = mv_ref[:, h * MLSTM_DV:(h + 1) * MLSTM_DV].astype(BF16)
        it_c = gc[:, h:h + 1]
        bt_c = btc[:, MLSTM_HEADS + h:MLSTM_HEADS + h + 1]
        it_r = gr[h:h + 1, :]
        bt_r = btr[MLSTM_HEADS + h:MLSTM_HEADS + h + 1, :]
        m_c = m_rows[:, h:h + 1]

        num, ssum, mt, wi = _mlstm_intra(qb, kb, vb, it_c, bt_c, it_r, bt_r, m_c, valid)
        qx = jnp.where(own_q, jnp.concatenate([q2] * (nb // 2), axis=1), 0.0).astype(BF16)
        c_stack = jnp.concatenate([c_ref[state_rows(b, h), :] for b in range(nb)], axis=0)
        qc = _dot(qx, c_stack.astype(BF16))
        n_rows = jnp.concatenate(
            [jnp.broadcast_to(n_ref[b * MLSTM_HEADS + h], (tq, dk)) for b in range(nb)], axis=0)
        num = num + wi * qc
        den = ssum + wi * jnp.sum(qb.astype(F32) * n_rows, axis=-1, keepdims=True)
        og = og_ref[:, h * MLSTM_DV:(h + 1) * MLSTM_DV]
        g_head = gh_ref[:, h * MLSTM_DV:(h + 1) * MLSTM_DV]
        hm_ref[:, h * MLSTM_DV:(h + 1) * MLSTM_DV] = _mlstm_head_out(num, den, mt, og, g_head)

        b_last = last_of_batch(bt_c)
        m_new = last_of_batch(mt)
        wk = jnp.exp(b_last - bt_c + it_c - m_new)
        wc = jnp.exp(b_last + m_c - m_new)
        wkv = (wk * vb.astype(F32)).astype(BF16)
        wkk = wk * kb.astype(F32)
        kt = mkt_ref[h * dk:(h + 1) * dk, :]
        kx = jnp.concatenate([jnp.where(lane_batch == b, kt, 0.0) for b in range(nb)], axis=0)
        upd = _dot(kx.astype(BF16), wkv)
        for b in range(nb):
            idx = b * MLSTM_HEADS + h
            wc_b = wc[b * tq:b * tq + 1]
            co_ref[state_rows(b, h), :] = wc_b * c_ref[state_rows(b, h), :] + upd[b * dk:(b + 1) * dk]
            no_ref[idx] = wc_b * n_ref[idx] + jnp.sum(wkk[b * tq:(b + 1) * tq], axis=0, keepdims=True)
        mt_all = jnp.where(lane == h, mt, mt_all)
    mt_ref[...] = mt_all


def _mlstm_sample(mq, mk, mkt, mv, og, gcol, grow, m_rows, ct_in, n_in, g_head, nbatch, nb):
    t = mq.shape[0]
    tq = t // nbatch
    rows = nb * tq
    assert tq & (tq - 1) == 0 and rows == LANES and nb % 2 == 0

    def tok(width):
        return pl.BlockSpec((rows, width), lambda i: (i, 0))

    ct_spec = pl.BlockSpec((nb * MLSTM_HEADS * MLSTM_DK, MLSTM_DV), lambda i: (i, 0))
    n_spec = pl.BlockSpec((nb * MLSTM_HEADS, 1, MLSTM_DK), lambda i: (i, 0, 0))
    return pl.pallas_call(
        functools.partial(_mlstm_sample_kernel, nb=nb, tq=tq),
        grid=(nbatch // nb,),
        in_specs=[
            tok(MQK_WIDTH), tok(MQK_WIDTH),
            pl.BlockSpec((MQK_WIDTH, rows), lambda i: (0, i)),
            tok(MLSTM_WIDTH), tok(MLSTM_WIDTH), tok(LANES),
            pl.BlockSpec((1, GATE_ROWS, rows), lambda i: (0, 0, i)),
            tok(LANES),
            ct_spec, n_spec,
            _const_spec((1, MLSTM_WIDTH)),
        ],
        out_specs=[tok(MLSTM_WIDTH), ct_spec, n_spec, tok(LANES)],
        out_shape=[
            jax.ShapeDtypeStruct((t, MLSTM_WIDTH), F32),
            jax.ShapeDtypeStruct(ct_in.shape, F32),
            jax.ShapeDtypeStruct(n_in.shape, F32),
            jax.ShapeDtypeStruct((t, LANES), F32),
        ],
        compiler_params=pltpu.CompilerParams(
            dimension_semantics=("arbitrary",), vmem_limit_bytes=VMEM_LIMIT),
    )(mq, mk, mkt, mv, og, gcol, grow, m_rows, ct_in, n_in, g_head)


def _memkv_kernel(mem_ref, g_ref, wk_ref, wv_ref, k_ref, v_ref):
    mn = _rmsnorm(mem_ref[...], g_ref[...]).astype(BF16)
    k_ref[...] = _dot(mn, wk_ref[...])
    v_ref[...] = _dot(mn, wv_ref[...])


def _memkv(mem, w, nbatch):
    t, d = mem.shape
    m = t // nbatch
    spec = pl.BlockSpec((m, CROSS_WIDTH), lambda b: (b, 0))
    return pl.pallas_call(
        _memkv_kernel,
        grid=(nbatch,),
        in_specs=[
            pl.BlockSpec((m, d), lambda b: (b, 0)),
            _const_spec((1, d)), _const_spec((d, CROSS_WIDTH)), _const_spec((d, CROSS_WIDTH)),
        ],
        out_specs=[spec, spec],
        out_shape=[jax.ShapeDtypeStruct((t, CROSS_WIDTH), F32)] * 2,
        compiler_params=pltpu.CompilerParams(
            dimension_semantics=("arbitrary",), vmem_limit_bytes=VMEM_LIMIT),
    )(mem, w["g_mem"], w["w_ck"], w["w_cv"])


def _mix_stage(x, att, hm, wo_ref, gcr_ref, wcq_ref):
    cat = jnp.concatenate([att.astype(BF16), hm.astype(BF16)], axis=1)
    y1 = x + _dot(cat, wo_ref[...])
    qc = _dot(_rmsnorm(y1, gcr_ref[...]).astype(BF16), wcq_ref[...])
    return y1, qc


def _cross_attention(qc, mk, mv):
    slabs = []
    for j in range(CROSS_WIDTH // LANES):
        qs = qc[:, j * LANES:(j + 1) * LANES]
        ks = mk[:, j * LANES:(j + 1) * LANES]
        vs = mv[:, j * LANES:(j + 1) * LANES]
        n = qs.shape[0]
        lo = _low_half(qs.shape)
        zero = jnp.zeros_like(qs)
        q2 = jnp.concatenate([jnp.where(lo, qs, zero), jnp.where(lo, zero, qs)], axis=0)
        s = _dot_nt(q2 * (HEAD_DIM ** -0.5), ks)
        p = jnp.exp(s - jnp.max(s, axis=-1, keepdims=True))
        o2 = _dot(p.astype(BF16), vs) / jnp.sum(p, axis=-1, keepdims=True)
        slabs.append(jnp.where(lo, o2[:n], o2[n:]))
    return jnp.concatenate(slabs, axis=1)


FFN_CHUNK = 256


def _ffn_stage(y1, o, wco_ref, gf_ref, wgu_ref, wd_ref, act_ref, gfin_ref, final, between=None):
    y2 = y1 + _dot(o.astype(BF16), wco_ref[...])
    xn = _rmsnorm(y2, gf_ref[...]).astype(BF16)
    for c in range(wd_ref.shape[0] // FFN_CHUNK):
        gu = _dot(xn, wgu_ref[:, 2 * c * FFN_CHUNK:2 * (c + 1) * FFN_CHUNK])
        g, u = gu[:, :FFN_CHUNK], gu[:, FFN_CHUNK:]
        act_ref[:, c * FFN_CHUNK:(c + 1) * FFN_CHUNK] = (g * jax.nn.sigmoid(g) * u).astype(BF16)
        if between is not None:
            between()
    y3 = y2 + _dot(act_ref[...], wd_ref[...])
    return _rmsnorm(y3, gfin_ref[...]) if final else y3


def _post_prompt_kernel(sink_ref, q_ref, kc_ref, kp_ref, vc_ref, vp_ref, x_ref, hm_ref, mk_ref, mv_ref,
                        wo_ref, gcr_ref, wcq_ref, wco_ref, gf_ref, wgu_ref, wd_ref, gfin_ref,
                        o_ref, act_ref, att_ref, *, final, ntiles, tiles_per_seq):
    s = pl.program_id(0)
    tile = jnp.minimum(s, ntiles - 1)
    first_block = (tile % tiles_per_seq) == 0
    slot = s % 2

    def swa_stages():
        def store(j, value):
            att_ref[slot, j * WINDOW:(j + 1) * WINDOW, :] = value.astype(att_ref.dtype)
        return _swa_prompt_stages(sink_ref, q_ref, kc_ref, kp_ref, vc_ref, vp_ref, first_block, store)

    @pl.when(s == 0)
    def _():
        for _ in swa_stages():
            pass

    @pl.when(s > 0)
    def _():
        swa = swa_stages()

        def advance(n=SWA_STAGES_PER_FFN_CHUNK):
            for _ in range(n):
                next(swa, None)

        y1, qc = _mix_stage(x_ref[...], att_ref[1 - slot], hm_ref[...], wo_ref, gcr_ref, wcq_ref)
        advance()
        o = _cross_attention(qc.astype(BF16), mk_ref[...].astype(BF16), mv_ref[...].astype(BF16))
        advance()
        o_ref[...] = _ffn_stage(y1, o, wco_ref, gf_ref, wgu_ref, wd_ref, act_ref, gfin_ref, final,
                                between=advance)
        for _ in swa:
            pass


_MIX_WEIGHTS = ("w_out", "g_cross", "w_cq")
_FFN_WEIGHTS = ("w_co", "g_ffn", "w_gate_up", "w_down", "g_final")


def _post_prompt(x, q, k, v, hm, memk, memv, sinks, w, nbatch, final, tm):
    t, d = x.shape
    ntiles = t // tm
    tpb = ntiles // nbatch
    wpt = tm // WINDOW
    m = memk.shape[0] // nbatch
    hidden = w["w_down"].shape[0]

    def attn_tile(s):
        return jnp.minimum(s, ntiles - 1)

    def post_tile(s):
        return jnp.maximum(s - 1, 0)

    def attn_rows(width):
        return pl.BlockSpec((tm, width), lambda s: (attn_tile(s), 0))

    def prev_window(width):
        return pl.BlockSpec((WINDOW, width), lambda s: (jnp.maximum(attn_tile(s) * wpt - 1, 0), 0))

    def post_rows(width):
        return pl.BlockSpec((tm, width), lambda s: (post_tile(s), 0))

    mem_spec = pl.BlockSpec((m, CROSS_WIDTH), lambda s: (post_tile(s) // tpb, 0))
    names = _MIX_WEIGHTS + _FFN_WEIGHTS
    return pl.pallas_call(
        functools.partial(_post_prompt_kernel, final=final, ntiles=ntiles, tiles_per_seq=tpb),
        grid=(ntiles + 1,),
        in_specs=[
            pl.BlockSpec(memory_space=pltpu.SMEM),
            attn_rows(ATT_WIDTH), attn_rows(KV_WIDTH), prev_window(KV_WIDTH),
            attn_rows(KV_WIDTH), prev_window(KV_WIDTH),
            post_rows(d), post_rows(MLSTM_WIDTH), mem_spec, mem_spec,
        ] + _weight_specs(w, names),
        out_specs=post_rows(d),
        out_shape=jax.ShapeDtypeStruct((t, d), F32),
        scratch_shapes=[pltpu.VMEM((tm, hidden), BF16), pltpu.VMEM((2, tm, ATT_WIDTH), BF16)],
        compiler_params=pltpu.CompilerParams(
            dimension_semantics=("arbitrary",), vmem_limit_bytes=VMEM_LIMIT),
    )(sinks, q, k, k, v, v, x, hm, memk, memv, *[w[n] for n in names])


def _mix_kernel(x_ref, att_ref, hm_ref, wo_ref, gcr_ref, wcq_ref, y_ref, qc_ref):
    y1, qc = _mix_stage(x_ref[...], att_ref[...], hm_ref[...], wo_ref, gcr_ref, wcq_ref)
    y_ref[...] = y1
    qc_ref[...] = qc


def _mix(x, att, hm, w, tm):
    t, d = x.shape

    def rows(width):
        return pl.BlockSpec((tm, width), lambda i: (i, 0))

    return pl.pallas_call(
        _mix_kernel,
        grid=(t // tm,),
        in_specs=[rows(d), rows(ATT_WIDTH), rows(MLSTM_WIDTH)] + _weight_specs(w, _MIX_WEIGHTS),
        out_specs=[rows(d), rows(CROSS_WIDTH)],
        out_shape=[jax.ShapeDtypeStruct((t, d), F32), jax.ShapeDtypeStruct((t, CROSS_WIDTH), F32)],
        compiler_params=pltpu.CompilerParams(
            dimension_semantics=("arbitrary",), vmem_limit_bytes=VMEM_LIMIT),
    )(x, att, hm, *[w[n] for n in _MIX_WEIGHTS])


def _cross_sample_kernel(qc_ref, mk_ref, mv_ref, o_ref, *, nb, tq):
    mem = CROSS_WIDTH
    n = CROSS_HEADS * tq
    row_head = lax.broadcasted_iota(jnp.int32, (n, CROSS_WIDTH), 0) >> _log2(tq)
    lane_head = lax.broadcasted_iota(jnp.int32, (n, CROSS_WIDTH), 1) >> _log2(HEAD_DIM)
    own = row_head == lane_head
    for b in range(nb):
        q = qc_ref[b * tq:(b + 1) * tq, :]
        qx = jnp.where(own, jnp.concatenate([q] * CROSS_HEADS, axis=0), 0.0).astype(BF16)
        mk = mk_ref[b * mem:(b + 1) * mem, :].astype(BF16)
        mv = mv_ref[b * mem:(b + 1) * mem, :].astype(BF16)
        s = _dot(qx, mk) * (HEAD_DIM ** -0.5)
        p = jnp.exp(s - jnp.max(s, axis=-1, keepdims=True))
        pv = _dot_nt(p.astype(BF16), mv) / jnp.sum(p, axis=-1, keepdims=True)
        pv = jnp.where(own, pv, 0.0)
        out = pv[0:tq]
        for h in range(1, CROSS_HEADS):
            out = out + pv[h * tq:(h + 1) * tq]
        o_ref[b * tq:(b + 1) * tq, :] = out


def _cross_sample(qc, memk_t, memv_t, nbatch, nb):
    t = qc.shape[0]
    tq = t // nbatch
    mem = memk_t.shape[1]
    tok = pl.BlockSpec((nb * tq, CROSS_WIDTH), lambda i: (i, 0))
    mspec = pl.BlockSpec((nb * CROSS_WIDTH, mem), lambda i: (i, 0))
    return pl.pallas_call(
        functools.partial(_cross_sample_kernel, nb=nb, tq=tq),
        grid=(nbatch // nb,),
        in_specs=[tok, mspec, mspec],
        out_specs=tok,
        out_shape=jax.ShapeDtypeStruct((t, CROSS_WIDTH), F32),
        compiler_params=pltpu.CompilerParams(
            dimension_semantics=("arbitrary",), vmem_limit_bytes=VMEM_LIMIT),
    )(qc, memk_t, memv_t)


def _ffn_kernel(y_ref, o_ref, wco_ref, gf_ref, wgu_ref, wd_ref, gfin_ref, out_ref, act_ref, *, final):
    out_ref[...] = _ffn_stage(y_ref[...], o_ref[...], wco_ref, gf_ref, wgu_ref, wd_ref, act_ref,
                              gfin_ref, final)


def _ffn(y1, o, w, final, tm):
    t, d = y1.shape
    hidden = w["w_down"].shape[0]

    def rows(width):
        return pl.BlockSpec((tm, width), lambda i: (i, 0))

    return pl.pallas_call(
        functools.partial(_ffn_kernel, final=final),
        grid=(t // tm,),
        in_specs=[rows(d), rows(CROSS_WIDTH)] + _weight_specs(w, _FFN_WEIGHTS),
        out_specs=rows(d),
        out_shape=jax.ShapeDtypeStruct((t, d), F32),
        scratch_shapes=[pltpu.VMEM((tm, hidden), BF16)],
        compiler_params=pltpu.CompilerParams(
            dimension_semantics=("arbitrary",), vmem_limit_bytes=VMEM_LIMIT),
    )(y1, o, *[w[n] for n in _FFN_WEIGHTS])


def _layer_weights(l, w_in, b_igate, b_fgate, g_mlstm_head, w_out, g_mix, g_cross, g_mem, w_cq,
                   w_ck, w_cv, w_co, g_ffn, w_gate, w_up, w_down, g_final):
    d = w_in.shape[1]
    gate_w = w_in[l][:, MAIN_WIDTH:]
    gate_b = jnp.concatenate([b_igate[l], b_fgate[l]]).astype(F32)
    ngate = gate_b.shape[0]
    row = lambda a: a.astype(F32).reshape(1, -1)
    w_t = w_in[l].T
    k0 = ATT_WIDTH
    mk0 = ATT_WIDTH + 2 * KV_WIDTH + MQK_WIDTH
    return {
        "w_kvk_t": jnp.concatenate([w_t[k0:k0 + 2 * KV_WIDTH], w_t[mk0:mk0 + MQK_WIDTH]]).astype(BF16),
        "w_main": w_in[l][:, :MAIN_WIDTH].astype(BF16),
        "w_gate_cols": jnp.pad(gate_w, ((0, 0), (0, LANES - ngate))).astype(BF16),
        "w_gate_rows": jnp.pad(gate_w.T, ((0, GATE_ROWS - ngate), (0, 0))).astype(BF16),
        "b_gate_cols": jnp.pad(gate_b, (0, LANES - ngate)).reshape(1, LANES),
        "b_gate_rows": jnp.pad(gate_b, (0, GATE_ROWS - ngate)).reshape(GATE_ROWS, 1),
        "g_mix": row(g_mix[l]), "g_cross": row(g_cross[l]), "g_mem": row(g_mem[l]),
        "g_ffn": row(g_ffn[l]), "g_final": row(g_final), "g_head": row(g_mlstm_head[l]),
        "w_cq": w_cq[l].astype(BF16), "w_ck": w_ck[l].astype(BF16),
        "w_cv": w_cv[l].astype(BF16), "w_co": w_co[l].astype(BF16),
    }


TOKEN_TILE = 512
MLSTM_STAGES_PER_PROJ_STAGE = 15
SWA_STAGES_PER_FFN_CHUNK = 3
MLSTM_CHUNK_ROWS = 256
SWA_SAMPLE_BATCH = 16
MLSTM_SAMPLE_BATCH = 16
CROSS_SAMPLE_BATCH = 16


def kernel(x_prompt, x_sample, mem_prompt, cache_swa_k, cache_swa_v, state_mlstm_C, state_mlstm_n,
           state_mlstm_m, cache_mem_k, cache_mem_v, w_in, b_igate, b_fgate, attn_sinks,
           g_mlstm_head, w_out, g_mix, g_cross, g_mem, w_cq, w_ck, w_cv, w_co, g_ffn, w_gate,
           w_up, w_down, g_final):
    depth = w_in.shape[0]
    bp, sp, d = x_prompt.shape
    bs, ss, _ = x_sample.shape
    mem_tokens = mem_prompt.shape[1]
    past = cache_swa_k.shape[2]
    yp = x_prompt.reshape(bp * sp, d)
    ys = x_sample.reshape(bs * ss, d)
    mem = mem_prompt.reshape(bp * mem_tokens, d)
    outs = [[] for _ in range(12)]

    for l in range(depth):
        final = l == depth - 1
        w = _layer_weights(l, w_in, b_igate, b_fgate, g_mlstm_head, w_out, g_mix, g_cross, g_mem,
                           w_cq, w_ck, w_cv, w_co, g_ffn, w_gate, w_up, w_down, g_final)
        sinks = attn_sinks[l].astype(F32)

        q, k, v, hm, c_p, n_p, m_p, cast = _mixer_prompt(
            yp, w, w_gate[l].astype(F32), w_up[l].astype(F32), w_down[l].astype(F32),
            w_out[l].astype(F32), bp, MLSTM_CHUNK_ROWS)
        w.update(cast)
        memk, memv = _memkv(mem, w, bp)
        yp = _post_prompt(yp, q, k, v, hm, memk, memv, sinks, w, bp, final, TOKEN_TILE)
        win_shape = (bp, WINDOW, ATT_KV_HEADS, HEAD_DIM)
        outs[0].append(k.reshape(bp, sp, KV_WIDTH)[:, sp - WINDOW:].reshape(win_shape))
        outs[1].append(v.reshape(bp, sp, KV_WIDTH)[:, sp - WINDOW:].reshape(win_shape))
        outs[2].append(c_p)
        outs[3].append(n_p)
        outs[4].append(m_p)
        mem_shape = (bp, mem_tokens, CROSS_HEADS, HEAD_DIM)
        outs[5].append(memk.reshape(mem_shape))
        outs[6].append(memv.reshape(mem_shape))

        def tokens_last(a):
            return jnp.transpose(a.astype(F32), (0, 2, 3, 1)).reshape(-1, a.shape[1])

        q, _, _, mq, mk, mv, og, grow, gcol, knt, vnt, mkt = _proj(
            ys, w, 1, F32, TOKEN_TILE, decode=True)
        att, kbuf_t, vbuf_t = _swa_sample(
            q, knt, vnt, tokens_last(cache_swa_k[l]), tokens_last(cache_swa_v[l]), sinks, bs,
            SWA_SAMPLE_BATCH)
        m_rows = jnp.pad(jnp.repeat(state_mlstm_m[l].astype(F32), ss, axis=0),
                         ((0, 0), (0, LANES - MLSTM_HEADS)))
        ct_in = jnp.swapaxes(state_mlstm_C[l].astype(F32), 2, 3).reshape(-1, MLSTM_DV)
        hm, ct_s, n_s, mt = _mlstm_sample(
            mq, mk, mkt, mv, og, gcol, grow, m_rows, ct_in,
            state_mlstm_n[l].astype(F32).reshape(bs * MLSTM_HEADS, 1, MLSTM_DK),
            w["g_head"], bs, MLSTM_SAMPLE_BATCH)
        y1, qc = _mix(ys, att, hm, w, TOKEN_TILE)
        o = _cross_sample(qc, tokens_last(cache_mem_k[l]), tokens_last(cache_mem_v[l]), bs,
                          CROSS_SAMPLE_BATCH)
        ys = _ffn(y1, o, w, final, TOKEN_TILE)
        buf_t_shape = (bs, ATT_KV_HEADS, HEAD_DIM, past)
        outs[7].append(jnp.transpose(kbuf_t.reshape(buf_t_shape), (0, 3, 1, 2)))
        outs[8].append(jnp.transpose(vbuf_t.reshape(buf_t_shape), (0, 3, 1, 2)))
        outs[9].append(jnp.swapaxes(ct_s.reshape(bs, MLSTM_HEADS, MLSTM_DK, MLSTM_DV), 2, 3))
        outs[10].append(n_s.reshape(bs, MLSTM_HEADS, MLSTM_DK))
        outs[11].append(mt.reshape(bs, ss, LANES)[:, ss - 1, :MLSTM_HEADS])

    return (yp.reshape(bp, sp, d), ys.reshape(bs, ss, d)) + tuple(jnp.stack(o) for o in outs)
```

```python
import functools

import jax
import jax.numpy as jnp
from jax import lax
from jax.experimental import pallas as pl
from jax.experimental.pallas import tpu as pltpu

F32 = jnp.float32
BF16 = jnp.bfloat16

HEAD_DIM = 64
ATT_GROUP = 4
ATT_KV_HEADS = 2
ATT_WIDTH = 512
KV_WIDTH = ATT_KV_HEADS * HEAD_DIM
WINDOW = 128
LOG_WINDOW = 7
MLSTM_HEADS = 4
MLSTM_DV = 128
MLSTM_DK = 64
MQK_WIDTH = MLSTM_HEADS * MLSTM_DK
MLSTM_WIDTH = MLSTM_HEADS * MLSTM_DV
MAIN_WIDTH = ATT_WIDTH + 2 * KV_WIDTH + 2 * MQK_WIDTH + 2 * MLSTM_WIDTH
CROSS_HEADS = 4
CROSS_WIDTH = CROSS_HEADS * HEAD_DIM
EPS = 1e-6
NEG_INF = float("-inf")
assert HEAD_DIM == 4 ** 3

LANES = 128
GATE_ROWS = 16
VMEM_LIMIT = 56 * 1024 * 1024

NT = (((1,), (1,)), ((), ()))
TN = (((0,), (0,)), ((), ()))


def _dot(a, b):
    return jnp.dot(a, b, preferred_element_type=F32)


def _dot_nt(a, b):
    return lax.dot_general(a, b, NT, preferred_element_type=F32)


def _dot_tn(a, b):
    return lax.dot_general(a, b, TN, preferred_element_type=F32)


def _rmsnorm(x, g):
    return x * lax.rsqrt(jnp.mean(x * x, axis=-1, keepdims=True) + EPS) * g


def _log_sigmoid(x):
    return jnp.minimum(x, 0.0) - jnp.log1p(jnp.exp(-jnp.abs(x)))


def _split3(x):
    hi = x.astype(BF16)
    r1 = x - hi.astype(F32)
    mid = r1.astype(BF16)
    lo = (r1 - mid.astype(F32)).astype(BF16)
    return hi, mid, lo


def _cumsum_cols(tri, x):
    hi, mid, lo = _split3(x)
    return _dot(tri, hi) + _dot(tri, mid) + _dot(tri, lo)


def _cumsum_rows(x, tri):
    hi, mid, lo = _split3(x)
    return _dot(hi, tri) + _dot(mid, tri) + _dot(lo, tri)


def _log2(n):
    assert n > 0 and n & (n - 1) == 0, n
    return n.bit_length() - 1


def _low_half(shape):
    return lax.broadcasted_iota(jnp.int32, shape, 1) < HEAD_DIM


def _decode_proj_kernel(x_ref, g_ref, w_ref, wgr_ref, br_ref, wgc_ref, bc_ref, wt_ref,
                        q_ref, mq_ref, mk_ref, mv_ref, og_ref, gr_ref, gc_ref, kt_ref, vt_ref, mkt_ref):
    xn = _rmsnorm(x_ref[...], g_ref[...]).astype(BF16)
    zt = _dot_nt(wt_ref[...], xn)
    kt_ref[...] = zt[:KV_WIDTH]
    vt_ref[...] = zt[KV_WIDTH:2 * KV_WIDTH]
    mkt_ref[...] = zt[2 * KV_WIDTH:] * (MLSTM_DK ** -0.5)

    def mm(lo, hi):
        return _dot(xn, w_ref[:, lo:hi])

    o = 0
    q_ref[...] = mm(o, o + ATT_WIDTH)
    o += ATT_WIDTH + 2 * KV_WIDTH
    mqk = mm(o, o + 2 * MQK_WIDTH)
    mq_ref[...] = mqk[:, :MQK_WIDTH]
    mk_ref[...] = mqk[:, MQK_WIDTH:] * (MLSTM_DK ** -0.5)
    o += 2 * MQK_WIDTH
    mv_ref[...] = mm(o, o + MLSTM_WIDTH)
    o += MLSTM_WIDTH
    og_ref[...] = mm(o, o + MLSTM_WIDTH)

    zc = _dot(xn, wgc_ref[...]) + bc_ref[...]
    lane = lax.broadcasted_iota(jnp.int32, zc.shape, 1)
    gc_ref[...] = jnp.where(lane < MLSTM_HEADS, zc, _log_sigmoid(zc))
    zr = _dot_nt(wgr_ref[...], xn) + br_ref[...]
    row = lax.broadcasted_iota(jnp.int32, zr.shape, 0)
    gr_ref[...] = jnp.where(row < MLSTM_HEADS, zr, _log_sigmoid(zr))


def _const_spec(shape):
    nd = len(shape)
    return pl.BlockSpec(shape, lambda *_: (0,) * nd, pipeline_mode=pl.Buffered(1))


def _weight_specs(w, names):
    return [_const_spec(w[n].shape) for n in names]


def _decode_proj(x, w, tm):
    t, d = x.shape

    def rows(width):
        return pl.BlockSpec((tm, width), lambda i: (i, 0))

    def lanes(height):
        return pl.BlockSpec((height, tm), lambda i: (0, i))

    outs = [(rows, ATT_WIDTH), (rows, MQK_WIDTH), (rows, MQK_WIDTH), (rows, MLSTM_WIDTH),
            (rows, MLSTM_WIDTH), (lanes, GATE_ROWS), (rows, LANES), (lanes, KV_WIDTH),
            (lanes, KV_WIDTH), (lanes, MQK_WIDTH)]
    names = ["g_mix", "w_main", "w_gate_rows", "b_gate_rows", "w_gate_cols", "b_gate_cols", "w_kvk_t"]
    return pl.pallas_call(
        _decode_proj_kernel,
        grid=(t // tm,),
        in_specs=[rows(d)] + _weight_specs(w, names),
        out_specs=[kind(size) for kind, size in outs],
        out_shape=[jax.ShapeDtypeStruct((t, size) if kind is rows else (size, t), F32)
                   for kind, size in outs],
        compiler_params=pltpu.CompilerParams(
            dimension_semantics=("arbitrary",), vmem_limit_bytes=VMEM_LIMIT),
    )(x, *[w[n] for n in names])


def _dup_heads(x):
    swapped = pltpu.roll(x, HEAD_DIM, axis=1)
    lo = _low_half(x.shape)
    return (jnp.where(lo, x, swapped).astype(BF16), jnp.where(lo, swapped, x).astype(BF16))


def _stack_query_heads(q, h):
    parts = []
    for g in range(ATT_GROUP):
        hq = h * ATT_GROUP + g
        slab = q[:, (hq // 2) * LANES:(hq // 2 + 1) * LANES]
        lo = _low_half(slab.shape)
        keep = lo if hq % 2 == 0 else jnp.logical_not(lo)
        parts.append(jnp.where(keep, slab, jnp.zeros_like(slab)))
    return jnp.concatenate(parts, axis=0).astype(BF16)


def _sink_column(sink_ref, h, rows_per_head):
    n = ATT_GROUP * rows_per_head
    grp = lax.broadcasted_iota(jnp.int32, (n, 1), 0) >> _log2(rows_per_head)
    col = jnp.full((n, 1), sink_ref[h * ATT_GROUP], F32)
    for g in range(1, ATT_GROUP):
        col = jnp.where(grp == g, sink_ref[h * ATT_GROUP + g], col)
    return col


def _swa_prompt_stages(sink_ref, q_ref, kc_ref, kp_ref, vc_ref, vp_ref, first_block, store):
    qb = q_ref.shape[0]
    k_all = jnp.concatenate([kp_ref[...], kc_ref[...]], axis=0)
    v_all = jnp.concatenate([vp_ref[...], vc_ref[...]], axis=0)
    kd = _dup_heads(k_all)
    v_t = v_all.T.astype(BF16)

    n = ATT_GROUP * WINDOW
    key = lax.broadcasted_iota(jnp.int32, (2 * WINDOW, n), 0)
    qry = lax.broadcasted_iota(jnp.int32, (2 * WINDOW, n), 1) & (WINDOW - 1)
    delta = key - qry
    in_band = (delta >= 0) & (delta <= WINDOW)
    grp = lax.broadcasted_iota(jnp.int32, (1, n), 1) >> LOG_WINDOW
    sinks = []
    for h in range(ATT_KV_HEADS):
        row = jnp.full((1, n), sink_ref[h * ATT_GROUP], F32)
        for g in range(1, ATT_GROUP):
            row = jnp.where(grp == g, sink_ref[h * ATT_GROUP + g], row)
        sinks.append(row)
    yield

    probs = []
    for j in range(qb // WINDOW):
        valid = in_band & (key >= jnp.where(first_block, WINDOW, 0)) if j == 0 else in_band
        q = q_ref[j * WINDOW:(j + 1) * WINDOW, :] * (HEAD_DIM ** -0.5)
        band = slice(j * WINDOW, (j + 2) * WINDOW)
        for h in range(ATT_KV_HEADS):
            qh = _stack_query_heads(q, h)
            s = _dot_nt(kd[h][band], qh)
            probs.append(dict(h=h, band=band, s=jnp.where(valid, s, NEG_INF)))
            yield
    for p in probs:
        p["m"] = jnp.maximum(jnp.max(p["s"], axis=0, keepdims=True), sinks[p["h"]])
        yield
    for p in probs:
        p["p"] = jnp.exp(p["s"] - p["m"])
        yield
    for p in probs:
        h = p["h"]
        denom = jnp.sum(p["p"], axis=0, keepdims=True) + jnp.exp(sinks[h] - p["m"])
        p["o_t"] = _dot(v_t[h * HEAD_DIM:(h + 1) * HEAD_DIM, p["band"]], p["p"].astype(BF16)) / denom
        yield
    for j in range(qb // WINDOW):
        pieces = [p["o_t"][:, g * WINDOW:(g + 1) * WINDOW]
                  for p in probs[j * ATT_KV_HEADS:(j + 1) * ATT_KV_HEADS] for g in range(ATT_GROUP)]
        store(j, jnp.concatenate(pieces, axis=0).T)
        yield


def _swa_sample_kernel(sink_ref, q_ref, knt_ref, vnt_ref, ck_ref, cv_ref,
                       o_ref, kbuf_ref, vbuf_ref, *, nb, tq):
    past = ck_ref.shape[1]
    hw = ATT_KV_HEADS * HEAD_DIM
    heads = ATT_KV_HEADS * ATT_GROUP
    n = heads * tq
    fresh = past - tq
    t = lax.broadcasted_iota(jnp.int32, (n, 2 * past), 0) & (tq - 1)
    col = lax.broadcasted_iota(jnp.int32, (n, 2 * past), 1)
    valid = ((col < past) & (col >= t)) | ((col >= past + fresh) & (col - (past + fresh) <= t))
    sink = jnp.concatenate([_sink_column(sink_ref, h, tq) for h in range(ATT_KV_HEADS)], axis=0)
    lane = lax.broadcasted_iota(jnp.int32, (hw, past), 1)
    knt = knt_ref[...]
    vnt = vnt_ref[...]
    zero_slab = jnp.zeros((tq, LANES), F32)
    lo = _low_half((tq, LANES))

    def shifted(old, new_t, b):
        return jnp.where(lane >= fresh, pltpu.roll(new_t, (fresh - b * tq) % past, axis=1),
                         pltpu.roll(old, fresh, axis=1))

    def dup_rows(x):
        h0, h1 = x[:HEAD_DIM], x[HEAD_DIM:]
        return jnp.concatenate([h0, h0, h1, h1], axis=0)

    for b in range(nb):
        k_old = ck_ref[b * hw:(b + 1) * hw, :]
        v_old = cv_ref[b * hw:(b + 1) * hw, :]
        k_new = shifted(k_old, knt, b)
        v_new = shifted(v_old, vnt, b)
        kbuf_ref[b * hw:(b + 1) * hw, :] = k_new
        vbuf_ref[b * hw:(b + 1) * hw, :] = v_new
        kop = jnp.concatenate([dup_rows(k_old), dup_rows(k_new)], axis=1).astype(BF16)
        vop = jnp.concatenate([dup_rows(v_old), dup_rows(v_new)], axis=1).astype(BF16)

        q = q_ref[b * tq:(b + 1) * tq, :]
        blocks = []
        for hq in range(heads):
            slab = q[:, (hq // 2) * LANES:(hq // 2 + 1) * LANES]
            slab = jnp.where(lo if hq % 2 == 0 else jnp.logical_not(lo), slab, 0.0)
            pair = [slab, zero_slab] if hq // ATT_GROUP == 0 else [zero_slab, slab]
            blocks.append(jnp.concatenate(pair, axis=1))
        qx = jnp.concatenate(blocks, axis=0).astype(BF16)

        s = _dot(qx, kop) * (HEAD_DIM ** -0.5)
        s = jnp.where(valid, s, NEG_INF)
        m = jnp.maximum(jnp.max(s, axis=-1, keepdims=True), sink)
        p = jnp.exp(s - m)
        denom = jnp.sum(p, axis=-1, keepdims=True) + jnp.exp(sink - m)
        o = _dot_nt(p.astype(BF16), vop) / denom

        slabs = []
        for j in range(heads // 2):
            h = (2 * j) // ATT_GROUP
            even = o[(2 * j) * tq:(2 * j + 1) * tq, h * LANES:(h + 1) * LANES]
            odd = o[(2 * j + 1) * tq:(2 * j + 2) * tq, h * LANES:(h + 1) * LANES]
            slabs.append(jnp.where(lo, even, odd))
        o_ref[b * tq:(b + 1) * tq, :] = jnp.concatenate(slabs, axis=1)


def _swa_sample(q, knt, vnt, cache_kt, cache_vt, sinks, nbatch, nb):
    t = q.shape[0]
    tq = t // nbatch
    past = cache_kt.shape[1]
    hw = ATT_KV_HEADS * HEAD_DIM
    assert past == WINDOW and tq & (tq - 1) == 0 and nb * tq == past

    def rows(r, width):
        return pl.BlockSpec((r, width), lambda i: (i, 0))

    new_t = pl.BlockSpec((hw, nb * tq), lambda i: (0, i))
    return pl.pallas_call(
        functools.partial(_swa_sample_kernel, nb=nb, tq=tq),
        grid=(nbatch // nb,),
        in_specs=[
            pl.BlockSpec(memory_space=pltpu.SMEM),
            rows(nb * tq, ATT_WIDTH), new_t, new_t, rows(nb * hw, past), rows(nb * hw, past),
        ],
        out_specs=[rows(nb * tq, ATT_WIDTH), rows(nb * hw, past), rows(nb * hw, past)],
        out_shape=[
            jax.ShapeDtypeStruct((t, ATT_WIDTH), F32),
            jax.ShapeDtypeStruct(cache_kt.shape, F32),
            jax.ShapeDtypeStruct(cache_vt.shape, F32),
        ],
        compiler_params=pltpu.CompilerParams(
            dimension_semantics=("arbitrary",), vmem_limit_bytes=VMEM_LIMIT),
    )(sinks, q, knt, vnt, cache_kt, cache_vt)


def _mlstm_intra(q, k, v, it_c, bt_c, it_r, bt_r, m_c, valid):
    d = jnp.where(valid, bt_c - bt_r + it_r, NEG_INF)
    inter = bt_c + m_c
    mt = jnp.maximum(jnp.max(d, axis=-1, keepdims=True), inter)
    sm = _dot_nt(q, k) * jnp.exp(d - mt)
    wi = jnp.exp(inter - mt)
    return _dot(sm.astype(BF16), v), jnp.sum(sm, axis=-1, keepdims=True), mt, wi


def _mlstm_head_out(num, den, mt, og, g_head):
    lower = jnp.maximum(jnp.abs(den), jnp.exp(-mt))
    h = num / lower
    hn = h * lax.rsqrt(jnp.mean(h * h, axis=-1, keepdims=True) + EPS)
    return hn * g_head * jax.nn.sigmoid(og)


def _mlstm_prompt_stages(mq, mk, mv, og, gr, gh_ref, hm_ref, c_ref, n_ref, m_ref, nbatch, chunk):
    s_idx = lax.broadcasted_iota(jnp.int32, (chunk, chunk), 0)
    l_idx = lax.broadcasted_iota(jnp.int32, (chunk, chunk), 1)
    causal = s_idx <= l_idx
    triu = jnp.where(causal, 1.0, 0.0).astype(BF16)
    lo = _low_half((chunk, LANES))
    pad_rows = jnp.zeros((LANES - MLSTM_HEADS, chunk), F32)
    sub8 = (GATE_ROWS, chunk)

    probs = []
    for b in range(nbatch):
        g = gr(b)
        btr = _cumsum_rows(g, triu)
        it_rows = g[:MLSTM_HEADS]
        bt_rows = btr[MLSTM_HEADS:2 * MLSTM_HEADS]
        u_cols = jnp.concatenate([it_rows - bt_rows, pad_rows], axis=0).T
        for h in range(MLSTM_HEADS):
            slab = slice((h // 2) * LANES, (h // 2 + 1) * LANES)
            keep = lo if h % 2 == 0 else jnp.logical_not(lo)
            q = mq(b, slab)
            k = mk(b, slab)
            probs.append(dict(
                b=b, h=h, idx=b * MLSTM_HEADS + h, cols=slice(h * MLSTM_DV, (h + 1) * MLSTM_DV),
                q=jnp.where(keep, q, jnp.zeros_like(q)), k=jnp.where(keep, k, jnp.zeros_like(k)),
                u_c=u_cols[:, h:h + 1], it_r=it_rows[h:h + 1, :], bt_r=bt_rows[h:h + 1, :]))
        yield

    for p in probs:
        p["m_old"] = m_ref[p["idx"]][:, 0:1]
        p["d"] = jnp.where(causal, p["bt_r"] + p["u_c"], NEG_INF)
        p["inter"] = p["bt_r"] + p["m_old"]
        p["q_t"] = p["q"].astype(F32).T.astype(BF16)
        p["kq"] = _dot(p["k"], p["q_t"])
        p["v_t"] = mv(p["b"], p["cols"]).astype(F32).T
        yield
    for p in probs:
        p["mt"] = jnp.maximum(jnp.max(p["d"], axis=0, keepdims=True), p["inter"])
        yield
    for p in probs:
        p["sm"] = p["kq"] * jnp.exp(p["d"] - p["mt"])
        p["wi"] = jnp.exp(p["inter"] - p["mt"])
        yield
    for p in probs:
        c_old = c_ref[p["idx"]]
        n_old = jnp.broadcast_to(n_ref[p["idx"]], (sub8[0], LANES))
        num = (_dot(p["v_t"].astype(BF16), p["sm"].astype(BF16))
               + p["wi"] * _dot(c_old.astype(BF16), p["q_t"]))
        den = (jnp.sum(p["sm"], axis=0, keepdims=True)
               + p["wi"] * _dot(n_old.astype(BF16), p["q_t"])[0:1])
        lower = jnp.maximum(jnp.abs(den), jnp.exp(-p["mt"]))
        h_t = num / lower
        p["hn"] = (h_t * lax.rsqrt(jnp.mean(h_t * h_t, axis=0, keepdims=True) + EPS)).T
        yield
    for p in probs:
        gate = jax.nn.sigmoid(og(p["b"], p["cols"]).astype(F32))
        hm_ref[p["b"], :, p["cols"]] = (p["hn"] * gh_ref[:, p["cols"]] * gate).astype(hm_ref.dtype)
        yield
    for p in probs:
        idx = p["idx"]
        b_last = p["bt_r"][:, chunk - 1:chunk]
        m_new = p["mt"][:, chunk - 1:chunk]
        wk = jnp.exp(b_last - p["bt_r"] + p["it_r"] - m_new)
        wc = jnp.exp(b_last + p["m_old"] - m_new)
        c_ref[idx] = wc * c_ref[idx] + _dot((p["v_t"] * wk).astype(BF16), p["k"])
        n_ref[idx] = wc * n_ref[idx] + _dot(jnp.broadcast_to(wk, sub8).astype(BF16), p["k"])[0:1]
        m_ref[idx] = jnp.broadcast_to(m_new, (1, LANES))
        yield


def _mixer_prompt_kernel(x_ref, g_ref, w_ref, wgr_ref, br_ref, gh_ref, wg_ref, wu_ref, wd_ref, wo_ref,
                         q_ref, k_ref, v_ref, hm_ref, c_ref, n_ref, m_ref, wgu_out, wd_out, wo_out,
                         mq_scr, mk_scr, mv_scr, og_scr, gr_scr, *, nbatch, chunk):
    s = pl.program_id(0)
    wr = s % 2
    rd = 1 - wr
    rows = nbatch * chunk

    def cast_weight_chunks():
        for c in range(0, wg_ref.shape[1], FFN_CHUNK):
            wgu_out[:, 2 * c:2 * c + FFN_CHUNK] = wg_ref[:, c:c + FFN_CHUNK].astype(BF16)
            wgu_out[:, 2 * c + FFN_CHUNK:2 * (c + FFN_CHUNK)] = wu_ref[:, c:c + FFN_CHUNK].astype(BF16)
        wd_out[...] = wd_ref[...].astype(BF16)
        wo_out[...] = wo_ref[...].astype(BF16)

    def proj_stages():
        xn = _rmsnorm(x_ref[...].reshape(rows, x_ref.shape[-1]), g_ref[...]).astype(BF16)
        yield

        def mm(lo, hi):
            return _dot(xn, w_ref[:, lo:hi])

        o = 0
        q_ref[...] = mm(o, o + ATT_WIDTH).astype(q_ref.dtype).reshape(q_ref.shape)
        yield
        o += ATT_WIDTH
        kv = mm(o, o + 2 * KV_WIDTH)
        k_ref[...] = kv[:, :KV_WIDTH].reshape(k_ref.shape)
        v_ref[...] = kv[:, KV_WIDTH:].reshape(v_ref.shape)
        yield
        o += 2 * KV_WIDTH
        mqk = mm(o, o + 2 * MQK_WIDTH)
        mq_scr[wr] = mqk[:, :MQK_WIDTH].astype(BF16)
        mk_scr[wr] = (mqk[:, MQK_WIDTH:] * (MLSTM_DK ** -0.5)).astype(BF16)
        yield
        o += 2 * MQK_WIDTH
        mv_scr[wr] = mm(o, o + MLSTM_WIDTH).astype(BF16)
        yield
        o += MLSTM_WIDTH
        og_scr[wr] = mm(o, o + MLSTM_WIDTH).astype(BF16)
        yield
        zr = _dot_nt(wgr_ref[...], xn) + br_ref[...]
        row = lax.broadcasted_iota(jnp.int32, zr.shape, 0)
        gr_scr[wr] = jnp.where(row < MLSTM_HEADS, zr, _log_sigmoid(zr))
        yield

    def mlstm_stages():
        def rows_of(scr):
            return lambda b, lanes: scr[rd, b * chunk:(b + 1) * chunk, lanes]
        return _mlstm_prompt_stages(
            rows_of(mq_scr), rows_of(mk_scr), rows_of(mv_scr), rows_of(og_scr),
            lambda b: gr_scr[rd, :, b * chunk:(b + 1) * chunk],
            gh_ref, hm_ref, c_ref, n_ref, m_ref, nbatch, chunk)

    @pl.when(s == 0)
    def _():
        c_ref[...] = jnp.zeros_like(c_ref)
        n_ref[...] = jnp.zeros_like(n_ref)
        m_ref[...] = jnp.zeros_like(m_ref)
        cast_weight_chunks()
        for _ in proj_stages():
            pass

    @pl.when(s > 0)
    def _():
        cast_weight_chunks()
        mlstm = mlstm_stages()
        for _ in proj_stages():
            for _ in range(MLSTM_STAGES_PER_PROJ_STAGE):
                next(mlstm, None)
        for _ in mlstm:
            pass


def _mixer_prompt(x, w, w_gate, w_up, w_down, w_out, nbatch, chunk):
    t, d = x.shape
    seq = t // nbatch
    nchunks = seq // chunk
    nstate = nbatch * MLSTM_HEADS
    rows = nbatch * chunk
    hidden = w_down.shape[0]
    assert hidden % FFN_CHUNK == 0 and w_out.shape[0] % nchunks == 0
    per_step = next(c for c in range(1, hidden // FFN_CHUNK + 1)
                    if (hidden // FFN_CHUNK) % c == 0 and hidden // FFN_CHUNK <= c * (nchunks + 1))
    ffn_cols = per_step * FFN_CHUNK
    nffn = hidden // ffn_cols
    out_rows = w_out.shape[0] // nchunks

    def proj_chunk(width):
        return pl.BlockSpec((nbatch, chunk, width), lambda s: (0, jnp.minimum(s, nchunks - 1), 0))

    def state(shape):
        return pl.BlockSpec(shape, lambda s: (0, 0, 0))

    def ffn_chunk(s):
        return jnp.minimum(s, nffn - 1)

    def out_chunk(s):
        return jnp.minimum(s, nchunks - 1)

    names = ["g_mix", "w_main", "w_gate_rows", "b_gate_rows", "g_head"]
    q, k, v, hm, c_st, n_st, m_st, w_gate_up_bf, w_down_bf, w_out_bf = pl.pallas_call(
        functools.partial(_mixer_prompt_kernel, nbatch=nbatch, chunk=chunk),
        grid=(nchunks + 1,),
        in_specs=[proj_chunk(d)] + _weight_specs(w, names) + [
            pl.BlockSpec((d, ffn_cols), lambda s: (0, ffn_chunk(s))),
            pl.BlockSpec((d, ffn_cols), lambda s: (0, ffn_chunk(s))),
            pl.BlockSpec((ffn_cols, d), lambda s: (ffn_chunk(s), 0)),
            pl.BlockSpec((out_rows, d), lambda s: (out_chunk(s), 0)),
        ],
        out_specs=[
            proj_chunk(ATT_WIDTH), proj_chunk(KV_WIDTH), proj_chunk(KV_WIDTH),
            pl.BlockSpec((nbatch, chunk, MLSTM_WIDTH), lambda s: (0, jnp.maximum(s - 1, 0), 0)),
            state((nstate, MLSTM_DV, LANES)), state((nstate, 1, LANES)), state((nstate, 1, LANES)),
            pl.BlockSpec((d, 2 * ffn_cols), lambda s: (0, ffn_chunk(s))),
            pl.BlockSpec((ffn_cols, d), lambda s: (ffn_chunk(s), 0)),
            pl.BlockSpec((out_rows, d), lambda s: (out_chunk(s), 0)),
        ],
        out_shape=[
            jax.ShapeDtypeStruct((nbatch, seq, ATT_WIDTH), BF16),
            jax.ShapeDtypeStruct((nbatch, seq, KV_WIDTH), F32),
            jax.ShapeDtypeStruct((nbatch, seq, KV_WIDTH), F32),
            jax.ShapeDtypeStruct((nbatch, seq, MLSTM_WIDTH), BF16),
            jax.ShapeDtypeStruct((nstate, MLSTM_DV, LANES), F32),
            jax.ShapeDtypeStruct((nstate, 1, LANES), F32),
            jax.ShapeDtypeStruct((nstate, 1, LANES), F32),
            jax.ShapeDtypeStruct((d, 2 * hidden), BF16),
            jax.ShapeDtypeStruct((hidden, d), BF16),
            jax.ShapeDtypeStruct(w_out.shape, BF16),
        ],
        scratch_shapes=[
            pltpu.VMEM((2, rows, MQK_WIDTH), BF16), pltpu.VMEM((2, rows, MQK_WIDTH), BF16),
            pltpu.VMEM((2, rows, MLSTM_WIDTH), BF16), pltpu.VMEM((2, rows, MLSTM_WIDTH), BF16),
            pltpu.VMEM((2, GATE_ROWS, rows), F32),
        ],
        compiler_params=pltpu.CompilerParams(
            dimension_semantics=("arbitrary",), vmem_limit_bytes=VMEM_LIMIT),
    )(x.reshape(nbatch, seq, d), *[w[n] for n in names], w_gate, w_up, w_down, w_out)
    q = q.reshape(t, ATT_WIDTH)
    k = k.reshape(t, KV_WIDTH)
    v = v.reshape(t, KV_WIDTH)

    c_st = c_st.reshape(nbatch, MLSTM_HEADS, MLSTM_DV, 2, MLSTM_DK)
    n_st = n_st.reshape(nbatch, MLSTM_HEADS, 2, MLSTM_DK)
    c_fin = jnp.stack([c_st[:, h, :, h % 2, :] for h in range(MLSTM_HEADS)], axis=1)
    n_fin = jnp.stack([n_st[:, h, h % 2, :] for h in range(MLSTM_HEADS)], axis=1)
    m_fin = m_st[:, 0, 0].reshape(nbatch, MLSTM_HEADS)
    bf16_weights = {"w_gate_up": w_gate_up_bf, "w_down": w_down_bf, "w_out": w_out_bf}
    return q, k, v, hm.reshape(t, MLSTM_WIDTH), c_fin, n_fin, m_fin, bf16_weights


def _mlstm_sample_kernel(mq_ref, mk_ref, mkt_ref, mv_ref, og_ref, gc_ref, gr_ref, mrow_ref, c_ref,
                         n_ref, gh_ref, hm_ref, co_ref, no_ref, mt_ref, *, nb, tq):
    rows = nb * tq
    shift = tq.bit_length() - 1
    r = lax.broadcasted_iota(jnp.int32, (rows, rows), 0)
    s = lax.broadcasted_iota(jnp.int32, (rows, rows), 1)
    same = (r >> shift) == (s >> shift)
    valid = same & (s <= r)
    tril = jnp.where(valid, 1.0, 0.0).astype(BF16)
    triu = jnp.where(same & (r <= s), 1.0, 0.0).astype(BF16)
    row_batch = lax.broadcasted_iota(jnp.int32, (rows, 1), 0) >> shift
    lane = lax.broadcasted_iota(jnp.int32, (rows, LANES), 1)

    gc = gc_ref[...]
    gr = gr_ref[...]
    btc = _cumsum_cols(tril, gc)
    btr = _cumsum_rows(gr, triu)
    m_rows = mrow_ref[...]
    mt_all = jnp.zeros((rows, LANES), F32)

    def last_of_batch(col):
        parts = [jnp.broadcast_to(col[(b + 1) * tq - 1:(b + 1) * tq], (tq, 1)) for b in range(nb)]
        return jnp.concatenate(parts, axis=0)

    dk = MLSTM_DK
    lo = _low_half((rows, LANES))
    own_q = row_batch == (lax.broadcasted_iota(jnp.int32, (rows, nb * dk), 1) >> _log2(dk))
    lane_batch = lax.broadcasted_iota(jnp.int32, (dk, rows), 1) >> shift

    def state_rows(b, h):
        return slice((b * MLSTM_HEADS + h) * dk, (b * MLSTM_HEADS + h + 1) * dk)

    for h in range(MLSTM_HEADS):
        slab = slice((h // 2) * LANES, (h // 2 + 1) * LANES)
        qs, ks = mq_ref[:, slab], mk_ref[:, slab]
        qsw, ksw = pltpu.roll(qs, dk, axis=1), pltpu.roll(ks, dk, axis=1)
        q2 = jnp.where(lo, qs, qsw) if h % 2 == 0 else jnp.where(lo, qsw, qs)
        k2 = jnp.where(lo, ks, ksw) if h % 2 == 0 else jnp.where(lo, ksw, ks)
        qb, kb = q2[:, :dk].astype(BF16), k2[:, :dk].astype(BF16)
        vb = mv_ref[:, h * MLSTM_DV:(h + 1) * MLSTM_DV].astype(BF16)
        it_c = gc[:, h:h + 1]
        bt_c = btc[:, MLSTM_HEADS + h:MLSTM_HEADS + h + 1]
        it_r = gr[h:h + 1, :]
        bt_r = btr[MLSTM_HEADS + h:MLSTM_HEADS + h + 1, :]
        m_c = m_rows[:, h:h + 1]

        num, ssum, mt, wi = _mlstm_intra(qb, kb, vb, it_c, bt_c, it_r, bt_r, m_c, valid)
        qx = jnp.where(own_q, jnp.concatenate([q2] * (nb // 2), axis=1), 0.0).astype(BF16)
        c_stack = jnp.concatenate([c_ref[state_rows(b, h), :] for b in range(nb)], axis=0)
        qc = _dot(qx, c_stack.astype(BF16))
        n_rows = jnp.concatenate(
            [jnp.broadcast_to(n_ref[b * MLSTM_HEADS + h], (tq, dk)) for b in range(nb)], axis=0)
        num = num + wi * qc
        den = ssum + wi * jnp.sum(qb.astype(F32) * n_rows, axis=-1, keepdims=True)
        og = og_ref[:, h * MLSTM_DV:(h + 1) * MLSTM_DV]
        g_head = gh_ref[:, h * MLSTM_DV:(h + 1) * MLSTM_DV]
        hm_ref[:, h * MLSTM_DV:(h + 1) * MLSTM_DV] = _mlstm_head_out(num, den, mt, og, g_head)

        b_last = last_of_batch(bt_c)
        m_new = last_of_batch(mt)
        wk = jnp.exp(b_last - bt_c + it_c - m_new)
        wc = jnp.exp(b_last + m_c - m_new)
        wkv = (wk * vb.astype(F32)).astype(BF16)
        wkk = wk * kb.astype(F32)
        kt = mkt_ref[h * dk:(h + 1) * dk, :]
        kx = jnp.concatenate([jnp.where(lane_batch == b, kt, 0.0) for b in range(nb)], axis=0)
        upd = _dot(kx.astype(BF16), wkv)
        for b in range(nb):
            idx = b * MLSTM_HEADS + h
            wc_b = wc[b * tq:b * tq + 1]
            co_ref[state_rows(b, h), :] = wc_b * c_ref[state_rows(b, h), :] + upd[b * dk:(b + 1) * dk]
            no_ref[idx] = wc_b * n_ref[idx] + jnp.sum(wkk[b * tq:(b + 1) * tq], axis=0, keepdims=True)
        mt_all = jnp.where(lane == h, mt, mt_all)
    mt_ref[...] = mt_all


def _mlstm_sample(mq, mk, mkt, mv, og, gcol, grow, m_rows, ct_in, n_in, g_head, nbatch, nb):
    t = mq.shape[0]
    tq = t // nbatch
    rows = nb * tq
    assert tq & (tq - 1) == 0 and rows == LANES and nb % 2 == 0

    def tok(width):
        return pl.BlockSpec((rows, width), lambda i: (i, 0))

    ct_spec = pl.BlockSpec((nb * MLSTM_HEADS * MLSTM_DK, MLSTM_DV), lambda i: (i, 0))
    n_spec = pl.BlockSpec((nb * MLSTM_HEADS, 1, MLSTM_DK), lambda i: (i, 0, 0))
    return pl.pallas_call(
        functools.partial(_mlstm_sample_kernel, nb=nb, tq=tq),
        grid=(nbatch // nb,),
        in_specs=[
            tok(MQK_WIDTH), tok(MQK_WIDTH),
            pl.BlockSpec((MQK_WIDTH, rows), lambda i: (0, i)),
            tok(MLSTM_WIDTH), tok(MLSTM_WIDTH), tok(LANES),
            pl.BlockSpec((GATE_ROWS, rows), lambda i: (0, i)),
            tok(LANES),
            ct_spec, n_spec,
            _const_spec((1, MLSTM_WIDTH)),
        ],
        out_specs=[tok(MLSTM_WIDTH), ct_spec, n_spec, tok(LANES)],
        out_shape=[
            jax.ShapeDtypeStruct((t, MLSTM_WIDTH), F32),
            jax.ShapeDtypeStruct(ct_in.shape, F32),
            jax.ShapeDtypeStruct(n_in.shape, F32),
            jax.ShapeDtypeStruct((t, LANES), F32),
        ],
        compiler_params=pltpu.CompilerParams(
            dimension_semantics=("arbitrary",), vmem_limit_bytes=VMEM_LIMIT),
    )(mq, mk, mkt, mv, og, gcol, grow, m_rows, ct_in, n_in, g_head)


def _memkv_kernel(mem_ref, g_ref, wk_ref, wv_ref, k_ref, v_ref):
    mn = _rmsnorm(mem_ref[...], g_ref[...]).astype(BF16)
    k_ref[...] = _dot(mn, wk_ref[...])
    v_ref[...] = _dot(mn, wv_ref[...])


def _memkv(mem, w, nbatch):
    t, d = mem.shape
    m = t // nbatch
    spec = pl.BlockSpec((m, CROSS_WIDTH), lambda b: (b, 0))
    return pl.pallas_call(
        _memkv_kernel,
        grid=(nbatch,),
        in_specs=[
            pl.BlockSpec((m, d), lambda b: (b, 0)),
            _const_spec((1, d)), _const_spec((d, CROSS_WIDTH)), _const_spec((d, CROSS_WIDTH)),
        ],
        out_specs=[spec, spec],
        out_shape=[jax.ShapeDtypeStruct((t, CROSS_WIDTH), F32)] * 2,
        compiler_params=pltpu.CompilerParams(
            dimension_semantics=("arbitrary",), vmem_limit_bytes=VMEM_LIMIT),
    )(mem, w["g_mem"], w["w_ck"], w["w_cv"])


def _mix_stage(x, att, hm, wo_ref, gcr_ref, wcq_ref):
    cat = jnp.concatenate([att.astype(BF16), hm.astype(BF16)], axis=1)
    y1 = x + _dot(cat, wo_ref[...])
    qc = _dot(_rmsnorm(y1, gcr_ref[...]).astype(BF16), wcq_ref[...])
    return y1, qc


def _cross_attention(qc, mk, mv):
    slabs = []
    for j in range(CROSS_WIDTH // LANES):
        qs = qc[:, j * LANES:(j + 1) * LANES]
        ks = mk[:, j * LANES:(j + 1) * LANES]
        vs = mv[:, j * LANES:(j + 1) * LANES]
        n = qs.shape[0]
        lo = _low_half(qs.shape)
        zero = jnp.zeros_like(qs)
        q2 = jnp.concatenate([jnp.where(lo, qs, zero), jnp.where(lo, zero, qs)], axis=0)
        s = _dot_nt(q2 * (HEAD_DIM ** -0.5), ks)
        p = jnp.exp(s - jnp.max(s, axis=-1, keepdims=True))
        o2 = _dot(p.astype(BF16), vs) / jnp.sum(p, axis=-1, keepdims=True)
        slabs.append(jnp.where(lo, o2[:n], o2[n:]))
    return jnp.concatenate(slabs, axis=1)


FFN_CHUNK = 256


def _ffn_stage(y1, o, wco_ref, gf_ref, wgu_ref, wd_ref, act_ref, gfin_ref, final, between=None):
    y2 = y1 + _dot(o.astype(BF16), wco_ref[...])
    xn = _rmsnorm(y2, gf_ref[...]).astype(BF16)
    for c in range(wd_ref.shape[0] // FFN_CHUNK):
        gu = _dot(xn, wgu_ref[:, 2 * c * FFN_CHUNK:2 * (c + 1) * FFN_CHUNK])
        g, u = gu[:, :FFN_CHUNK], gu[:, FFN_CHUNK:]
        act_ref[:, c * FFN_CHUNK:(c + 1) * FFN_CHUNK] = (g * jax.nn.sigmoid(g) * u).astype(BF16)
        if between is not None:
            between()
    y3 = y2 + _dot(act_ref[...], wd_ref[...])
    return _rmsnorm(y3, gfin_ref[...]) if final else y3


def _post_prompt_kernel(sink_ref, q_ref, kc_ref, kp_ref, vc_ref, vp_ref, x_ref, hm_ref, mk_ref, mv_ref,
                        wo_ref, gcr_ref, wcq_ref, wco_ref, gf_ref, wgu_ref, wd_ref, gfin_ref,
                        o_ref, act_ref, att_ref, *, final, ntiles, tiles_per_seq):
    s = pl.program_id(0)
    tile = jnp.minimum(s, ntiles - 1)
    first_block = (tile % tiles_per_seq) == 0
    slot = s % 2

    def swa_stages():
        def store(j, value):
            att_ref[slot, j * WINDOW:(j + 1) * WINDOW, :] = value.astype(att_ref.dtype)
        return _swa_prompt_stages(sink_ref, q_ref, kc_ref, kp_ref, vc_ref, vp_ref, first_block, store)

    @pl.when(s == 0)
    def _():
        for _ in swa_stages():
            pass

    @pl.when(s > 0)
    def _():
        swa = swa_stages()

        def advance(n=SWA_STAGES_PER_FFN_CHUNK):
            for _ in range(n):
                next(swa, None)

        y1, qc = _mix_stage(x_ref[...], att_ref[1 - slot], hm_ref[...], wo_ref, gcr_ref, wcq_ref)
        advance()
        o = _cross_attention(qc.astype(BF16), mk_ref[...].astype(BF16), mv_ref[...].astype(BF16))
        advance()
        o_ref[...] = _ffn_stage(y1, o, wco_ref, gf_ref, wgu_ref, wd_ref, act_ref, gfin_ref, final,
                                between=advance)
        for _ in swa:
            pass


_MIX_WEIGHTS = ("w_out", "g_cross", "w_cq")
_FFN_WEIGHTS = ("w_co", "g_ffn", "w_gate_up", "w_down", "g_final")


def _post_prompt(x, q, k, v, hm, memk, memv, sinks, w, nbatch, final, tm):
    t, d = x.shape
    ntiles = t // tm
    tpb = ntiles // nbatch
    wpt = tm // WINDOW
    m = memk.shape[0] // nbatch
    hidden = w["w_down"].shape[0]

    def attn_tile(s):
        return jnp.minimum(s, ntiles - 1)

    def post_tile(s):
        return jnp.maximum(s - 1, 0)

    def attn_rows(width):
        return pl.BlockSpec((tm, width), lambda s: (attn_tile(s), 0))

    def prev_window(width):
        return pl.BlockSpec((WINDOW, width), lambda s: (jnp.maximum(attn_tile(s) * wpt - 1, 0), 0))

    def post_rows(width):
        return pl.BlockSpec((tm, width), lambda s: (post_tile(s), 0))

    mem_spec = pl.BlockSpec((m, CROSS_WIDTH), lambda s: (post_tile(s) // tpb, 0))
    names = _MIX_WEIGHTS + _FFN_WEIGHTS
    return pl.pallas_call(
        functools.partial(_post_prompt_kernel, final=final, ntiles=ntiles, tiles_per_seq=tpb),
        grid=(ntiles + 1,),
        in_specs=[
            pl.BlockSpec(memory_space=pltpu.SMEM),
            attn_rows(ATT_WIDTH), attn_rows(KV_WIDTH), prev_window(KV_WIDTH),
            attn_rows(KV_WIDTH), prev_window(KV_WIDTH),
            post_rows(d), post_rows(MLSTM_WIDTH), mem_spec, mem_spec,
        ] + _weight_specs(w, names),
        out_specs=post_rows(d),
        out_shape=jax.ShapeDtypeStruct((t, d), F32),
        scratch_shapes=[pltpu.VMEM((tm, hidden), BF16), pltpu.VMEM((2, tm, ATT_WIDTH), BF16)],
        compiler_params=pltpu.CompilerParams(
            dimension_semantics=("arbitrary",), vmem_limit_bytes=VMEM_LIMIT),
    )(sinks, q, k, k, v, v, x, hm, memk, memv, *[w[n] for n in names])


def _mix_kernel(x_ref, att_ref, hm_ref, wo_ref, gcr_ref, wcq_ref, y_ref, qc_ref):
    y1, qc = _mix_stage(x_ref[...], att_ref[...], hm_ref[...], wo_ref, gcr_ref, wcq_ref)
    y_ref[...] = y1
    qc_ref[...] = qc


def _mix(x, att, hm, w, tm):
    t, d = x.shape

    def rows(width):
        return pl.BlockSpec((tm, width), lambda i: (i, 0))

    return pl.pallas_call(
        _mix_kernel,
        grid=(t // tm,),
        in_specs=[rows(d), rows(ATT_WIDTH), rows(MLSTM_WIDTH)] + _weight_specs(w, _MIX_WEIGHTS),
        out_specs=[rows(d), rows(CROSS_WIDTH)],
        out_shape=[jax.ShapeDtypeStruct((t, d), F32), jax.ShapeDtypeStruct((t, CROSS_WIDTH), F32)],
        compiler_params=pltpu.CompilerParams(
            dimension_semantics=("arbitrary",), vmem_limit_bytes=VMEM_LIMIT),
    )(x, att, hm, *[w[n] for n in _MIX_WEIGHTS])


def _cross_sample_kernel(qc_ref, mk_ref, mv_ref, o_ref, *, nb, tq):
    mem = CROSS_WIDTH
    n = CROSS_HEADS * tq
    row_head = lax.broadcasted_iota(jnp.int32, (n, CROSS_WIDTH), 0) >> _log2(tq)
    lane_head = lax.broadcasted_iota(jnp.int32, (n, CROSS_WIDTH), 1) >> _log2(HEAD_DIM)
    own = row_head == lane_head
    for b in range(nb):
        q = qc_ref[b * tq:(b + 1) * tq, :]
        qx = jnp.where(own, jnp.concatenate([q] * CROSS_HEADS, axis=0), 0.0).astype(BF16)
        mk = mk_ref[b * mem:(b + 1) * mem, :].astype(BF16)
        mv = mv_ref[b * mem:(b + 1) * mem, :].astype(BF16)
        s = _dot(qx, mk) * (HEAD_DIM ** -0.5)
        p = jnp.exp(s - jnp.max(s, axis=-1, keepdims=True))
        pv = _dot_nt(p.astype(BF16), mv) / jnp.sum(p, axis=-1, keepdims=True)
        pv = jnp.where(own, pv, 0.0)
        out = pv[0:tq]
        for h in range(1, CROSS_HEADS):
            out = out + pv[h * tq:(h + 1) * tq]
        o_ref[b * tq:(b + 1) * tq, :] = out


def _cross_sample(qc, memk_t, memv_t, nbatch, nb):
    t = qc.shape[0]
    tq = t // nbatch
    mem = memk_t.shape[1]
    tok = pl.BlockSpec((nb * tq, CROSS_WIDTH), lambda i: (i, 0))
    mspec = pl.BlockSpec((nb * CROSS_WIDTH, mem), lambda i: (i, 0))
    return pl.pallas_call(
        functools.partial(_cross_sample_kernel, nb=nb, tq=tq),
        grid=(nbatch // nb,),
        in_specs=[tok, mspec, mspec],
        out_specs=tok,
        out_shape=jax.ShapeDtypeStruct((t, CROSS_WIDTH), F32),
        compiler_params=pltpu.CompilerParams(
            dimension_semantics=("arbitrary",), vmem_limit_bytes=VMEM_LIMIT),
    )(qc, memk_t, memv_t)


def _ffn_kernel(y_ref, o_ref, wco_ref, gf_ref, wgu_ref, wd_ref, gfin_ref, out_ref, act_ref, *, final):
    out_ref[...] = _ffn_stage(y_ref[...], o_ref[...], wco_ref, gf_ref, wgu_ref, wd_ref, act_ref,
                              gfin_ref, final)


def _ffn(y1, o, w, final, tm):
    t, d = y1.shape
    hidden = w["w_down"].shape[0]

    def rows(width):
        return pl.BlockSpec((tm, width), lambda i: (i, 0))

    return pl.pallas_call(
        functools.partial(_ffn_kernel, final=final),
        grid=(t // tm,),
        in_specs=[rows(d), rows(CROSS_WIDTH)] + _weight_specs(w, _FFN_WEIGHTS),
        out_specs=rows(d),
        out_shape=jax.ShapeDtypeStruct((t, d), F32),
        scratch_shapes=[pltpu.VMEM((tm, hidden), BF16)],
        compiler_params=pltpu.CompilerParams(
            dimension_semantics=("arbitrary",), vmem_limit_bytes=VMEM_LIMIT),
    )(y1, o, *[w[n] for n in _FFN_WEIGHTS])


def _layer_weights(l, w_in, b_igate, b_fgate, g_mlstm_head, g_mix, g_cross, g_mem, w_cq, w_ck, w_cv,
                   w_co, g_ffn, g_final):
    gate_w = w_in[l][:, MAIN_WIDTH:]
    gate_b = jnp.concatenate([b_igate[l], b_fgate[l]]).astype(F32)
    ngate = gate_b.shape[0]
    row = lambda a: a.astype(F32).reshape(1, -1)
    w_t = w_in[l].T
    k0 = ATT_WIDTH
    mk0 = ATT_WIDTH + 2 * KV_WIDTH + MQK_WIDTH
    return {
        "w_kvk_t": jnp.concatenate([w_t[k0:k0 + 2 * KV_WIDTH], w_t[mk0:mk0 + MQK_WIDTH]]).astype(BF16),
        "w_main": w_in[l][:, :MAIN_WIDTH].astype(BF16),
        "w_gate_cols": jnp.pad(gate_w, ((0, 0), (0, LANES - ngate))).astype(BF16),
        "w_gate_rows": jnp.pad(gate_w.T, ((0, GATE_ROWS - ngate), (0, 0))).astype(BF16),
        "b_gate_cols": jnp.pad(gate_b, (0, LANES - ngate)).reshape(1, LANES),
        "b_gate_rows": jnp.pad(gate_b, (0, GATE_ROWS - ngate)).reshape(GATE_ROWS, 1),
        "g_mix": row(g_mix[l]), "g_cross": row(g_cross[l]), "g_mem": row(g_mem[l]),
        "g_ffn": row(g_ffn[l]), "g_final": row(g_final), "g_head": row(g_mlstm_head[l]),
        "w_cq": w_cq[l].astype(BF16), "w_ck": w_ck[l].astype(BF16),
        "w_cv": w_cv[l].astype(BF16), "w_co": w_co[l].astype(BF16),
    }


TOKEN_TILE = 512
MLSTM_STAGES_PER_PROJ_STAGE = 15
SWA_STAGES_PER_FFN_CHUNK = 3
MLSTM_CHUNK_ROWS = 256
SWA_SAMPLE_BATCH = 16
MLSTM_SAMPLE_BATCH = 16
CROSS_SAMPLE_BATCH = 16


def kernel(x_prompt, x_sample, mem_prompt, cache_swa_k, cache_swa_v, state_mlstm_C, state_mlstm_n,
           state_mlstm_m, cache_mem_k, cache_mem_v, w_in, b_igate, b_fgate, attn_sinks,
           g_mlstm_head, w_out, g_mix, g_cross, g_mem, w_cq, w_ck, w_cv, w_co, g_ffn, w_gate,
           w_up, w_down, g_final):
    depth = w_in.shape[0]
    bp, sp, d = x_prompt.shape
    bs, ss, _ = x_sample.shape
    mem_tokens = mem_prompt.shape[1]
    past = cache_swa_k.shape[2]
    yp = x_prompt.reshape(bp * sp, d)
    ys = x_sample.reshape(bs * ss, d)
    mem = mem_prompt.reshape(bp * mem_tokens, d)
    outs = [[] for _ in range(12)]

    for l in range(depth):
        final = l == depth - 1
        w = _layer_weights(l, w_in, b_igate, b_fgate, g_mlstm_head, g_mix, g_cross, g_mem, w_cq,
                           w_ck, w_cv, w_co, g_ffn, g_final)
        sinks = attn_sinks[l].astype(F32)

        q, k, v, hm, c_p, n_p, m_p, cast = _mixer_prompt(
            yp, w, w_gate[l].astype(F32), w_up[l].astype(F32), w_down[l].astype(F32),
            w_out[l].astype(F32), bp, MLSTM_CHUNK_ROWS)
        w.update(cast)
        memk, memv = _memkv(mem, w, bp)
        yp = _post_prompt(yp, q, k, v, hm, memk, memv, sinks, w, bp, final, TOKEN_TILE)
        win_shape = (bp, WINDOW, ATT_KV_HEADS, HEAD_DIM)
        outs[0].append(k.reshape(bp, sp, KV_WIDTH)[:, sp - WINDOW:].reshape(win_shape))
        outs[1].append(v.reshape(bp, sp, KV_WIDTH)[:, sp - WINDOW:].reshape(win_shape))
        outs[2].append(c_p)
        outs[3].append(n_p)
        outs[4].append(m_p)
        mem_shape = (bp, mem_tokens, CROSS_HEADS, HEAD_DIM)
        outs[5].append(memk.reshape(mem_shape))
        outs[6].append(memv.reshape(mem_shape))

        def tokens_last(a):
            return jnp.transpose(a.astype(F32), (0, 2, 3, 1)).reshape(-1, a.shape[1])

        q, mq, mk, mv, og, grow, gcol, knt, vnt, mkt = _decode_proj(ys, w, TOKEN_TILE)
        att, kbuf_t, vbuf_t = _swa_sample(
            q, knt, vnt, tokens_last(cache_swa_k[l]), tokens_last(cache_swa_v[l]), sinks, bs,
            SWA_SAMPLE_BATCH)
        m_rows = jnp.pad(jnp.repeat(state_mlstm_m[l].astype(F32), ss, axis=0),
                         ((0, 0), (0, LANES - MLSTM_HEADS)))
        ct_in = jnp.swapaxes(state_mlstm_C[l].astype(F32), 2, 3).reshape(-1, MLSTM_DV)
        hm, ct_s, n_s, mt = _mlstm_sample(
            mq, mk, mkt, mv, og, gcol, grow, m_rows, ct_in,
            state_mlstm_n[l].astype(F32).reshape(bs * MLSTM_HEADS, 1, MLSTM_DK),
            w["g_head"], bs, MLSTM_SAMPLE_BATCH)
        y1, qc = _mix(ys, att, hm, w, TOKEN_TILE)
        o = _cross_sample(qc, tokens_last(cache_mem_k[l]), tokens_last(cache_mem_v[l]), bs,
                          CROSS_SAMPLE_BATCH)
        ys = _ffn(y1, o, w, final, TOKEN_TILE)
        buf_t_shape = (bs, ATT_KV_HEADS, HEAD_DIM, past)
        outs[7].append(jnp.transpose(kbuf_t.reshape(buf_t_shape), (0, 3, 1, 2)))
        outs[8].append(jnp.transpose(vbuf_t.reshape(buf_t_shape), (0, 3, 1, 2)))
        outs[9].append(jnp.swapaxes(ct_s.reshape(bs, MLSTM_HEADS, MLSTM_DK, MLSTM_DV), 2, 3))
        outs[10].append(n_s.reshape(bs, MLSTM_HEADS, MLSTM_DK))
        outs[11].append(mt.reshape(bs, ss, LANES)[:, ss - 1, :MLSTM_HEADS])

    return (yp.reshape(bp, sp, d), ys.reshape(bs, ss, d)) + tuple(jnp.stack(o) for o in outs)
```

```python
import functools

import jax
import jax.numpy as jnp
from jax import lax
from jax.experimental import pallas as pl
from jax.experimental.pallas import tpu as pltpu

F32 = jnp.float32
BF16 = jnp.bfloat16

HEAD_DIM = 64
ATT_GROUP = 4
ATT_KV_HEADS = 2
ATT_WIDTH = 512
KV_WIDTH = ATT_KV_HEADS * HEAD_DIM
WINDOW = 128
LOG_WINDOW = 7
MLSTM_HEADS = 4
MLSTM_DV = 128
MLSTM_DK = 64
MQK_WIDTH = MLSTM_HEADS * MLSTM_DK
MLSTM_WIDTH = MLSTM_HEADS * MLSTM_DV
MAIN_WIDTH = ATT_WIDTH + 2 * KV_WIDTH + 2 * MQK_WIDTH + 2 * MLSTM_WIDTH
CROSS_HEADS = 4
CROSS_WIDTH = CROSS_HEADS * HEAD_DIM
EPS = 1e-6
NEG_INF = float("-inf")
assert HEAD_DIM == 4 ** 3

LANES = 128
GATE_ROWS = 16
VMEM_LIMIT = 56 * 1024 * 1024

NT = (((1,), (1,)), ((), ()))
TN = (((0,), (0,)), ((), ()))


def _dot(a, b):
    return jnp.dot(a, b, preferred_element_type=F32)


def _dot_nt(a, b):
    return lax.dot_general(a, b, NT, preferred_element_type=F32)


def _dot_tn(a, b):
    return lax.dot_general(a, b, TN, preferred_element_type=F32)


def _rmsnorm(x, g):
    return x * lax.rsqrt(jnp.mean(x * x, axis=-1, keepdims=True) + EPS) * g


def _log_sigmoid(x):
    return jnp.minimum(x, 0.0) - jnp.log1p(jnp.exp(-jnp.abs(x)))


def _split3(x):
    hi = x.astype(BF16)
    r1 = x - hi.astype(F32)
    mid = r1.astype(BF16)
    lo = (r1 - mid.astype(F32)).astype(BF16)
    return hi, mid, lo


def _cumsum_cols(tri, x):
    hi, mid, lo = _split3(x)
    return _dot(tri, hi) + _dot(tri, mid) + _dot(tri, lo)


def _cumsum_rows(x, tri):
    hi, mid, lo = _split3(x)
    return _dot(hi, tri) + _dot(mid, tri) + _dot(lo, tri)


def _log2(n):
    assert n > 0 and n & (n - 1) == 0, n
    return n.bit_length() - 1


def _low_half(shape):
    return lax.broadcasted_iota(jnp.int32, shape, 1) < HEAD_DIM


def _decode_proj_kernel(x_ref, g_ref, w_ref, wgr_ref, br_ref, wgc_ref, bc_ref, wt_ref,
                        q_ref, mq_ref, mk_ref, mv_ref, og_ref, gr_ref, gc_ref, kt_ref, vt_ref, mkt_ref):
    xn = _rmsnorm(x_ref[...], g_ref[...]).astype(BF16)
    zt = _dot_nt(wt_ref[...], xn)
    kt_ref[...] = zt[:KV_WIDTH]
    vt_ref[...] = zt[KV_WIDTH:2 * KV_WIDTH]
    mkt_ref[...] = zt[2 * KV_WIDTH:] * (MLSTM_DK ** -0.5)

    def mm(lo, hi):
        return _dot(xn, w_ref[:, lo:hi])

    o = 0
    q_ref[...] = mm(o, o + ATT_WIDTH)
    o += ATT_WIDTH + 2 * KV_WIDTH
    mqk = mm(o, o + 2 * MQK_WIDTH)
    mq_ref[...] = mqk[:, :MQK_WIDTH]
    mk_ref[...] = mqk[:, MQK_WIDTH:] * (MLSTM_DK ** -0.5)
    o += 2 * MQK_WIDTH
    mv_ref[...] = mm(o, o + MLSTM_WIDTH)
    o += MLSTM_WIDTH
    og_ref[...] = mm(o, o + MLSTM_WIDTH)

    zc = _dot(xn, wgc_ref[...]) + bc_ref[...]
    lane = lax.broadcasted_iota(jnp.int32, zc.shape, 1)
    gc_ref[...] = jnp.where(lane < MLSTM_HEADS, zc, _log_sigmoid(zc))
    zr = _dot_nt(wgr_ref[...], xn) + br_ref[...]
    row = lax.broadcasted_iota(jnp.int32, zr.shape, 0)
    gr_ref[...] = jnp.where(row < MLSTM_HEADS, zr, _log_sigmoid(zr))


def _const_spec(shape):
    nd = len(shape)
    return pl.BlockSpec(shape, lambda *_: (0,) * nd, pipeline_mode=pl.Buffered(1))


def _weight_specs(w, names):
    return [_const_spec(w[n].shape) for n in names]


def _decode_proj(x, w, tm):
    t, d = x.shape

    def rows(width):
        return pl.BlockSpec((tm, width), lambda i: (i, 0))

    def lanes(height):
        return pl.BlockSpec((height, tm), lambda i: (0, i))

    outs = [(rows, ATT_WIDTH), (rows, MQK_WIDTH), (rows, MQK_WIDTH), (rows, MLSTM_WIDTH),
            (rows, MLSTM_WIDTH), (lanes, GATE_ROWS), (rows, LANES), (lanes, KV_WIDTH),
            (lanes, KV_WIDTH), (lanes, MQK_WIDTH)]
    names = ["g_mix", "w_main", "w_gate_rows", "b_gate_rows", "w_gate_cols", "b_gate_cols", "w_kvk_t"]
    return pl.pallas_call(
        _decode_proj_kernel,
        grid=(t // tm,),
        in_specs=[rows(d)] + _weight_specs(w, names),
        out_specs=[kind(size) for kind, size in outs],
        out_shape=[jax.ShapeDtypeStruct((t, size) if kind is rows else (size, t), F32)
                   for kind, size in outs],
        compiler_params=pltpu.CompilerParams(
            dimension_semantics=("arbitrary",), vmem_limit_bytes=VMEM_LIMIT),
    )(x, *[w[n] for n in names])


def _dup_heads(x):
    swapped = pltpu.roll(x, HEAD_DIM, axis=1)
    lo = _low_half(x.shape)
    return (jnp.where(lo, x, swapped).astype(BF16), jnp.where(lo, swapped, x).astype(BF16))


def _stack_query_heads(q, h):
    parts = []
    for g in range(ATT_GROUP):
        hq = h * ATT_GROUP + g
        slab = q[:, (hq // 2) * LANES:(hq // 2 + 1) * LANES]
        lo = _low_half(slab.shape)
        keep = lo if hq % 2 == 0 else jnp.logical_not(lo)
        parts.append(jnp.where(keep, slab, jnp.zeros_like(slab)))
    return jnp.concatenate(parts, axis=0).astype(BF16)


def _sink_column(sink_ref, h, rows_per_head):
    n = ATT_GROUP * rows_per_head
    grp = lax.broadcasted_iota(jnp.int32, (n, 1), 0) >> _log2(rows_per_head)
    col = jnp.full((n, 1), sink_ref[h * ATT_GROUP], F32)
    for g in range(1, ATT_GROUP):
        col = jnp.where(grp == g, sink_ref[h * ATT_GROUP + g], col)
    return col


def _swa_prompt_stages(sink_ref, q_ref, kc_ref, kp_ref, vc_ref, vp_ref, first_block, store):
    qb = q_ref.shape[0]
    k_all = jnp.concatenate([kp_ref[...], kc_ref[...]], axis=0)
    v_all = jnp.concatenate([vp_ref[...], vc_ref[...]], axis=0)
    kd = _dup_heads(k_all)
    v_t = v_all.T.astype(BF16)

    n = ATT_GROUP * WINDOW
    key = lax.broadcasted_iota(jnp.int32, (2 * WINDOW, n), 0)
    qry = lax.broadcasted_iota(jnp.int32, (2 * WINDOW, n), 1) & (WINDOW - 1)
    delta = key - qry
    in_band = (delta >= 0) & (delta <= WINDOW)
    grp = lax.broadcasted_iota(jnp.int32, (1, n), 1) >> LOG_WINDOW
    sinks = []
    for h in range(ATT_KV_HEADS):
        row = jnp.full((1, n), sink_ref[h * ATT_GROUP], F32)
        for g in range(1, ATT_GROUP):
            row = jnp.where(grp == g, sink_ref[h * ATT_GROUP + g], row)
        sinks.append(row)
    yield

    probs = []
    for j in range(qb // WINDOW):
        valid = in_band & (key >= jnp.where(first_block, WINDOW, 0)) if j == 0 else in_band
        q = q_ref[j * WINDOW:(j + 1) * WINDOW, :] * (HEAD_DIM ** -0.5)
        band = slice(j * WINDOW, (j + 2) * WINDOW)
        for h in range(ATT_KV_HEADS):
            qh = _stack_query_heads(q, h)
            s = _dot_nt(kd[h][band], qh)
            probs.append(dict(h=h, band=band, s=jnp.where(valid, s, NEG_INF)))
            yield
    for p in probs:
        p["m"] = jnp.maximum(jnp.max(p["s"], axis=0, keepdims=True), sinks[p["h"]])
        yield
    for p in probs:
        p["p"] = jnp.exp(p["s"] - p["m"])
        yield
    for p in probs:
        h = p["h"]
        denom = jnp.sum(p["p"], axis=0, keepdims=True) + jnp.exp(sinks[h] - p["m"])
        p["o_t"] = _dot(v_t[h * HEAD_DIM:(h + 1) * HEAD_DIM, p["band"]], p["p"].astype(BF16)) / denom
        yield
    for j in range(qb // WINDOW):
        pieces = [p["o_t"][:, g * WINDOW:(g + 1) * WINDOW]
                  for p in probs[j * ATT_KV_HEADS:(j + 1) * ATT_KV_HEADS] for g in range(ATT_GROUP)]
        store(j, jnp.concatenate(pieces, axis=0).T)
        yield


def _swa_sample_kernel(sink_ref, q_ref, knt_ref, vnt_ref, ck_ref, cv_ref,
                       o_ref, kbuf_ref, vbuf_ref, *, nb, tq):
    past = ck_ref.shape[1]
    hw = ATT_KV_HEADS * HEAD_DIM
    heads = ATT_KV_HEADS * ATT_GROUP
    n = heads * tq
    fresh = past - tq
    t = lax.broadcasted_iota(jnp.int32, (n, 2 * past), 0) & (tq - 1)
    col = lax.broadcasted_iota(jnp.int32, (n, 2 * past), 1)
    valid = ((col < past) & (col >= t)) | ((col >= past + fresh) & (col - (past + fresh) <= t))
    sink = jnp.concatenate([_sink_column(sink_ref, h, tq) for h in range(ATT_KV_HEADS)], axis=0)
    lane = lax.broadcasted_iota(jnp.int32, (hw, past), 1)
    knt = knt_ref[...]
    vnt = vnt_ref[...]
    zero_slab = jnp.zeros((tq, LANES), F32)
    lo = _low_half((tq, LANES))

    def shifted(old, new_t, b):
        return jnp.where(lane >= fresh, pltpu.roll(new_t, (fresh - b * tq) % past, axis=1),
                         pltpu.roll(old, fresh, axis=1))

    def dup_rows(x):
        h0, h1 = x[:HEAD_DIM], x[HEAD_DIM:]
        return jnp.concatenate([h0, h0, h1, h1], axis=0)

    for b in range(nb):
        k_old = ck_ref[b * hw:(b + 1) * hw, :]
        v_old = cv_ref[b * hw:(b + 1) * hw, :]
        k_new = shifted(k_old, knt, b)
        v_new = shifted(v_old, vnt, b)
        kbuf_ref[b * hw:(b + 1) * hw, :] = k_new
        vbuf_ref[b * hw:(b + 1) * hw, :] = v_new
        kop = jnp.concatenate([dup_rows(k_old), dup_rows(k_new)], axis=1).astype(BF16)
        vop = jnp.concatenate([dup_rows(v_old), dup_rows(v_new)], axis=1).astype(BF16)

        q = q_ref[b * tq:(b + 1) * tq, :]
        blocks = []
        for hq in range(heads):
            slab = q[:, (hq // 2) * LANES:(hq // 2 + 1) * LANES]
            slab = jnp.where(lo if hq % 2 == 0 else jnp.logical_not(lo), slab, 0.0)
            pair = [slab, zero_slab] if hq // ATT_GROUP == 0 else [zero_slab, slab]
            blocks.append(jnp.concatenate(pair, axis=1))
        qx = jnp.concatenate(blocks, axis=0).astype(BF16)

        s = _dot(qx, kop) * (HEAD_DIM ** -0.5)
        s = jnp.where(valid, s, NEG_INF)
        m = jnp.maximum(jnp.max(s, axis=-1, keepdims=True), sink)
        p = jnp.exp(s - m)
        denom = jnp.sum(p, axis=-1, keepdims=True) + jnp.exp(sink - m)
        o = _dot_nt(p.astype(BF16), vop) / denom

        slabs = []
        for j in range(heads // 2):
            h = (2 * j) // ATT_GROUP
            even = o[(2 * j) * tq:(2 * j + 1) * tq, h * LANES:(h + 1) * LANES]
            odd = o[(2 * j + 1) * tq:(2 * j + 2) * tq, h * LANES:(h + 1) * LANES]
            slabs.append(jnp.where(lo, even, odd))
        o_ref[b * tq:(b + 1) * tq, :] = jnp.concatenate(slabs, axis=1)


def _swa_sample(q, knt, vnt, cache_kt, cache_vt, sinks, nbatch, nb):
    t = q.shape[0]
    tq = t // nbatch
    past = cache_kt.shape[1]
    hw = ATT_KV_HEADS * HEAD_DIM
    assert past == WINDOW and tq & (tq - 1) == 0 and nb * tq == past

    def rows(r, width):
        return pl.BlockSpec((r, width), lambda i: (i, 0))

    new_t = pl.BlockSpec((hw, nb * tq), lambda i: (0, i))
    return pl.pallas_call(
        functools.partial(_swa_sample_kernel, nb=nb, tq=tq),
        grid=(nbatch // nb,),
        in_specs=[
            pl.BlockSpec(memory_space=pltpu.SMEM),
            rows(nb * tq, ATT_WIDTH), new_t, new_t, rows(nb * hw, past), rows(nb * hw, past),
        ],
        out_specs=[rows(nb * tq, ATT_WIDTH), rows(nb * hw, past), rows(nb * hw, past)],
        out_shape=[
            jax.ShapeDtypeStruct((t, ATT_WIDTH), F32),
            jax.ShapeDtypeStruct(cache_kt.shape, F32),
            jax.ShapeDtypeStruct(cache_vt.shape, F32),
        ],
        compiler_params=pltpu.CompilerParams(
            dimension_semantics=("arbitrary",), vmem_limit_bytes=VMEM_LIMIT),
    )(sinks, q, knt, vnt, cache_kt, cache_vt)


def _mlstm_intra(q, k, v, it_c, bt_c, it_r, bt_r, m_c, valid):
    d = jnp.where(valid, bt_c - bt_r + it_r, NEG_INF)
    inter = bt_c + m_c
    mt = jnp.maximum(jnp.max(d, axis=-1, keepdims=True), inter)
    sm = _dot_nt(q, k) * jnp.exp(d - mt)
    wi = jnp.exp(inter - mt)
    return _dot(sm.astype(BF16), v), jnp.sum(sm, axis=-1, keepdims=True), mt, wi


def _mlstm_head_out(num, den, mt, og, g_head):
    lower = jnp.maximum(jnp.abs(den), jnp.exp(-mt))
    h = num / lower
    hn = h * lax.rsqrt(jnp.mean(h * h, axis=-1, keepdims=True) + EPS)
    return hn * g_head * jax.nn.sigmoid(og)


def _mlstm_prompt_stages(mq, mk, mv, og, gr, gh_ref, hm_ref, c_ref, n_ref, m_ref, nbatch, chunk):
    s_idx = lax.broadcasted_iota(jnp.int32, (chunk, chunk), 0)
    l_idx = lax.broadcasted_iota(jnp.int32, (chunk, chunk), 1)
    causal = s_idx <= l_idx
    triu = jnp.where(causal, 1.0, 0.0).astype(BF16)
    lo = _low_half((chunk, LANES))
    pad_rows = jnp.zeros((LANES - MLSTM_HEADS, chunk), F32)
    sub8 = (GATE_ROWS, chunk)

    probs = []
    for b in range(nbatch):
        g = gr(b)
        btr = _cumsum_rows(g, triu)
        it_rows = g[:MLSTM_HEADS]
        bt_rows = btr[MLSTM_HEADS:2 * MLSTM_HEADS]
        u_cols = jnp.concatenate([it_rows - bt_rows, pad_rows], axis=0).T
        for h in range(MLSTM_HEADS):
            slab = slice((h // 2) * LANES, (h // 2 + 1) * LANES)
            keep = lo if h % 2 == 0 else jnp.logical_not(lo)
            q = mq(b, slab)
            k = mk(b, slab)
            probs.append(dict(
                b=b, h=h, idx=b * MLSTM_HEADS + h, cols=slice(h * MLSTM_DV, (h + 1) * MLSTM_DV),
                q=jnp.where(keep, q, jnp.zeros_like(q)), k=jnp.where(keep, k, jnp.zeros_like(k)),
                u_c=u_cols[:, h:h + 1], it_r=it_rows[h:h + 1, :], bt_r=bt_rows[h:h + 1, :]))
        yield

    for p in probs:
        p["m_old"] = m_ref[p["idx"]][:, 0:1]
        p["d"] = jnp.where(causal, p["bt_r"] + p["u_c"], NEG_INF)
        p["inter"] = p["bt_r"] + p["m_old"]
        p["q_t"] = p["q"].astype(F32).T.astype(BF16)
        p["kq"] = _dot(p["k"], p["q_t"])
        p["v_t"] = mv(p["b"], p["cols"]).astype(F32).T
        yield
    for p in probs:
        p["mt"] = jnp.maximum(jnp.max(p["d"], axis=0, keepdims=True), p["inter"])
        yield
    for p in probs:
        p["sm"] = p["kq"] * jnp.exp(p["d"] - p["mt"])
        p["wi"] = jnp.exp(p["inter"] - p["mt"])
        yield
    for p in probs:
        c_old = c_ref[p["idx"]]
        n_old = jnp.broadcast_to(n_ref[p["idx"]], (sub8[0], LANES))
        num = (_dot(p["v_t"].astype(BF16), p["sm"].astype(BF16))
               + p["wi"] * _dot(c_old.astype(BF16), p["q_t"]))
        den = (jnp.sum(p["sm"], axis=0, keepdims=True)
               + p["wi"] * _dot(n_old.astype(BF16), p["q_t"])[0:1])
        lower = jnp.maximum(jnp.abs(den), jnp.exp(-p["mt"]))
        h_t = num / lower
        p["hn"] = (h_t * lax.rsqrt(jnp.mean(h_t * h_t, axis=0, keepdims=True) + EPS)).T
        yield
    for p in probs:
        gate = jax.nn.sigmoid(og(p["b"], p["cols"]).astype(F32))
        hm_ref[p["b"], :, p["cols"]] = (p["hn"] * gh_ref[:, p["cols"]] * gate).astype(hm_ref.dtype)
        yield
    for p in probs:
        idx = p["idx"]
        b_last = p["bt_r"][:, chunk - 1:chunk]
        m_new = p["mt"][:, chunk - 1:chunk]
        wk = jnp.exp(b_last - p["bt_r"] + p["it_r"] - m_new)
        wc = jnp.exp(b_last + p["m_old"] - m_new)
        c_ref[idx] = wc * c_ref[idx] + _dot((p["v_t"] * wk).astype(BF16), p["k"])
        n_ref[idx] = wc * n_ref[idx] + _dot(jnp.broadcast_to(wk, sub8).astype(BF16), p["k"])[0:1]
        m_ref[idx] = jnp.broadcast_to(m_new, (1, LANES))
        yield


def _mixer_prompt_kernel(x_ref, g_ref, w_ref, wgr_ref, br_ref, gh_ref, wg_ref, wu_ref, wd_ref, wo_ref,
                         q_ref, k_ref, v_ref, hm_ref, c_ref, n_ref, m_ref, wgu_out, wd_out, wo_out,
                         mq_scr, mk_scr, mv_scr, og_scr, gr_scr, *, nbatch, chunk):
    s = pl.program_id(0)
    wr = s % 2
    rd = 1 - wr
    rows = nbatch * chunk

    def cast_weight_chunks():
        for c in range(0, wg_ref.shape[1], FFN_CHUNK):
            wgu_out[:, 2 * c:2 * c + FFN_CHUNK] = wg_ref[:, c:c + FFN_CHUNK].astype(BF16)
            wgu_out[:, 2 * c + FFN_CHUNK:2 * (c + FFN_CHUNK)] = wu_ref[:, c:c + FFN_CHUNK].astype(BF16)
        wd_out[...] = wd_ref[...].astype(BF16)
        wo_out[...] = wo_ref[...].astype(BF16)

    def proj_stages():
        xn = _rmsnorm(x_ref[...].reshape(rows, x_ref.shape[-1]), g_ref[...]).astype(BF16)
        yield

        def mm(lo, hi):
            return _dot(xn, w_ref[:, lo:hi])

        o = 0
        q_ref[...] = mm(o, o + ATT_WIDTH).astype(q_ref.dtype).reshape(q_ref.shape)
        yield
        o += ATT_WIDTH
        kv = mm(o, o + 2 * KV_WIDTH)
        k_ref[...] = kv[:, :KV_WIDTH].reshape(k_ref.shape)
        v_ref[...] = kv[:, KV_WIDTH:].reshape(v_ref.shape)
        yield
        o += 2 * KV_WIDTH
        mqk = mm(o, o + 2 * MQK_WIDTH)
        mq_scr[wr] = mqk[:, :MQK_WIDTH].astype(BF16)
        mk_scr[wr] = (mqk[:, MQK_WIDTH:] * (MLSTM_DK ** -0.5)).astype(BF16)
        yield
        o += 2 * MQK_WIDTH
        mv_scr[wr] = mm(o, o + MLSTM_WIDTH).astype(BF16)
        yield
        o += MLSTM_WIDTH
        og_scr[wr] = mm(o, o + MLSTM_WIDTH).astype(BF16)
        yield
        zr = _dot_nt(wgr_ref[...], xn) + br_ref[...]
        row = lax.broadcasted_iota(jnp.int32, zr.shape, 0)
        gr_scr[wr] = jnp.where(row < MLSTM_HEADS, zr, _log_sigmoid(zr))
        yield

    def mlstm_stages():
        def rows_of(scr):
            return lambda b, lanes: scr[rd, b * chunk:(b + 1) * chunk, lanes]
        return _mlstm_prompt_stages(
            rows_of(mq_scr), rows_of(mk_scr), rows_of(mv_scr), rows_of(og_scr),
            lambda b: gr_scr[rd, :, b * chunk:(b + 1) * chunk],
            gh_ref, hm_ref, c_ref, n_ref, m_ref, nbatch, chunk)

    @pl.when(s == 0)
    def _():
        c_ref[...] = jnp.zeros_like(c_ref)
        n_ref[...] = jnp.zeros_like(n_ref)
        m_ref[...] = jnp.zeros_like(m_ref)
        cast_weight_chunks()
        for _ in proj_stages():
            pass

    @pl.when(s > 0)
    def _():
        cast_weight_chunks()
        mlstm = mlstm_stages()
        for _ in proj_stages():
            for _ in range(MLSTM_STAGES_PER_PROJ_STAGE):
                next(mlstm, None)
        for _ in mlstm:
            pass


def _mixer_prompt(x, w, w_gate, w_up, w_down, w_out, nbatch, chunk):
    t, d = x.shape
    seq = t // nbatch
    nchunks = seq // chunk
    nstate = nbatch * MLSTM_HEADS
    rows = nbatch * chunk
    hidden = w_down.shape[0]
    bf16_rows = 16
    assert hidden % FFN_CHUNK == 0 and d % (nchunks * bf16_rows) == 0 and hidden % (nchunks * bf16_rows) == 0

    def proj_chunk(width):
        return pl.BlockSpec((nbatch, chunk, width), lambda s: (0, jnp.minimum(s, nchunks - 1), 0))

    def state(shape):
        return pl.BlockSpec(shape, lambda s: (0, 0, 0))

    def weight_rows(a, width=None):
        return pl.BlockSpec((a.shape[0] // nchunks, width or a.shape[1]),
                            lambda s: (jnp.minimum(s, nchunks - 1), 0))

    names = ["g_mix", "w_main", "w_gate_rows", "b_gate_rows", "g_head"]
    q, k, v, hm, c_st, n_st, m_st, w_gate_up_bf, w_down_bf, w_out_bf = pl.pallas_call(
        functools.partial(_mixer_prompt_kernel, nbatch=nbatch, chunk=chunk),
        grid=(nchunks + 1,),
        in_specs=[proj_chunk(d)] + _weight_specs(w, names) + [
            weight_rows(w_gate), weight_rows(w_up), weight_rows(w_down), weight_rows(w_out),
        ],
        out_specs=[
            proj_chunk(ATT_WIDTH), proj_chunk(KV_WIDTH), proj_chunk(KV_WIDTH),
            pl.BlockSpec((nbatch, chunk, MLSTM_WIDTH), lambda s: (0, jnp.maximum(s - 1, 0), 0)),
            state((nstate, MLSTM_DV, LANES)), state((nstate, 1, LANES)), state((nstate, 1, LANES)),
            weight_rows(w_gate, 2 * hidden), weight_rows(w_down), weight_rows(w_out),
        ],
        out_shape=[
            jax.ShapeDtypeStruct((nbatch, seq, ATT_WIDTH), BF16),
            jax.ShapeDtypeStruct((nbatch, seq, KV_WIDTH), F32),
            jax.ShapeDtypeStruct((nbatch, seq, KV_WIDTH), F32),
            jax.ShapeDtypeStruct((nbatch, seq, MLSTM_WIDTH), BF16),
            jax.ShapeDtypeStruct((nstate, MLSTM_DV, LANES), F32),
            jax.ShapeDtypeStruct((nstate, 1, LANES), F32),
            jax.ShapeDtypeStruct((nstate, 1, LANES), F32),
            jax.ShapeDtypeStruct((d, 2 * hidden), BF16),
            jax.ShapeDtypeStruct((hidden, d), BF16),
            jax.ShapeDtypeStruct(w_out.shape, BF16),
        ],
        scratch_shapes=[
            pltpu.VMEM((2, rows, MQK_WIDTH), BF16), pltpu.VMEM((2, rows, MQK_WIDTH), BF16),
            pltpu.VMEM((2, rows, MLSTM_WIDTH), BF16), pltpu.VMEM((2, rows, MLSTM_WIDTH), BF16),
            pltpu.VMEM((2, GATE_ROWS, rows), F32),
        ],
        compiler_params=pltpu.CompilerParams(
            dimension_semantics=("arbitrary",), vmem_limit_bytes=VMEM_LIMIT),
    )(x.reshape(nbatch, seq, d), *[w[n] for n in names], w_gate, w_up, w_down, w_out)
    q = q.reshape(t, ATT_WIDTH)
    k = k.reshape(t, KV_WIDTH)
    v = v.reshape(t, KV_WIDTH)

    c_st = c_st.reshape(nbatch, MLSTM_HEADS, MLSTM_DV, 2, MLSTM_DK)
    n_st = n_st.reshape(nbatch, MLSTM_HEADS, 2, MLSTM_DK)
    c_fin = jnp.stack([c_st[:, h, :, h % 2, :] for h in range(MLSTM_HEADS)], axis=1)
    n_fin = jnp.stack([n_st[:, h, h % 2, :] for h in range(MLSTM_HEADS)], axis=1)
    m_fin = m_st[:, 0, 0].reshape(nbatch, MLSTM_HEADS)
    bf16_weights = {"w_gate_up": w_gate_up_bf, "w_down": w_down_bf, "w_out": w_out_bf}
    return q, k, v, hm.reshape(t, MLSTM_WIDTH), c_fin, n_fin, m_fin, bf16_weights


def _mlstm_sample_kernel(mq_ref, mk_ref, mkt_ref, mv_ref, og_ref, gc_ref, gr_ref, mrow_ref, c_ref,
                         n_ref, gh_ref, hm_ref, co_ref, no_ref, mt_ref, *, nb, tq):
    rows = nb * tq
    shift = tq.bit_length() - 1
    r = lax.broadcasted_iota(jnp.int32, (rows, rows), 0)
    s = lax.broadcasted_iota(jnp.int32, (rows, rows), 1)
    same = (r >> shift) == (s >> shift)
    valid = same & (s <= r)
    tril = jnp.where(valid, 1.0, 0.0).astype(BF16)
    triu = jnp.where(same & (r <= s), 1.0, 0.0).astype(BF16)
    row_batch = lax.broadcasted_iota(jnp.int32, (rows, 1), 0) >> shift
    lane = lax.broadcasted_iota(jnp.int32, (rows, LANES), 1)

    gc = gc_ref[...]
    gr = gr_ref[...]
    btc = _cumsum_cols(tril, gc)
    btr = _cumsum_rows(gr, triu)
    m_rows = mrow_ref[...]
    mt_all = jnp.zeros((rows, LANES), F32)

    def last_of_batch(col):
        parts = [jnp.broadcast_to(col[(b + 1) * tq - 1:(b + 1) * tq], (tq, 1)) for b in range(nb)]
        return jnp.concatenate(parts, axis=0)

    dk = MLSTM_DK
    lo = _low_half((rows, LANES))
    own_q = row_batch == (lax.broadcasted_iota(jnp.int32, (rows, nb * dk), 1) >> _log2(dk))
    lane_batch = lax.broadcasted_iota(jnp.int32, (dk, rows), 1) >> shift

    def state_rows(b, h):
        return slice((b * MLSTM_HEADS + h) * dk, (b * MLSTM_HEADS + h + 1) * dk)

    for h in range(MLSTM_HEADS):
        slab = slice((h // 2) * LANES, (h // 2 + 1) * LANES)
        qs, ks = mq_ref[:, slab], mk_ref[:, slab]
        qsw, ksw = pltpu.roll(qs, dk, axis=1), pltpu.roll(ks, dk, axis=1)
        q2 = jnp.where(lo, qs, qsw) if h % 2 == 0 else jnp.where(lo, qsw, qs)
        k2 = jnp.where(lo, ks, ksw) if h % 2 == 0 else jnp.where(lo, ksw, ks)
        qb, kb = q2[:, :dk].astype(BF16), k2[:, :dk].astype(BF16)
        vb = mv_ref[:, h * MLSTM_DV:(h + 1) * MLSTM_DV].astype(BF16)
        it_c = gc[:, h:h + 1]
        bt_c = btc[:, MLSTM_HEADS + h:MLSTM_HEADS + h + 1]
        it_r = gr[h:h + 1, :]
        bt_r = btr[MLSTM_HEADS + h:MLSTM_HEADS + h + 1, :]
        m_c = m_rows[:, h:h + 1]

        num, ssum, mt, wi = _mlstm_intra(qb, kb, vb, it_c, bt_c, it_r, bt_r, m_c, valid)
        qx = jnp.where(own_q, jnp.concatenate([q2] * (nb // 2), axis=1), 0.0).astype(BF16)
        c_stack = jnp.concatenate([c_ref[state_rows(b, h), :] for b in range(nb)], axis=0)
        qc = _dot(qx, c_stack.astype(BF16))
        n_rows = jnp.concatenate(
            [jnp.broadcast_to(n_ref[b * MLSTM_HEADS + h], (tq, dk)) for b in range(nb)], axis=0)
        num = num + wi * qc
        den = ssum + wi * jnp.sum(qb.astype(F32) * n_rows, axis=-1, keepdims=True)
        og = og_ref[:, h * MLSTM_DV:(h + 1) * MLSTM_DV]
        g_head = gh_ref[:, h * MLSTM_DV:(h + 1) * MLSTM_DV]
        hm_ref[:, h * MLSTM_DV:(h + 1) * MLSTM_DV] = _mlstm_head_out(num, den, mt, og, g_head)

        b_last = last_of_batch(bt_c)
        m_new = last_of_batch(mt)
        wk = jnp.exp(b_last - bt_c + it_c - m_new)
        wc = jnp.exp(b_last + m_c - m_new)
        wkv = (wk * vb.astype(F32)).astype(BF16)
        wkk = wk * kb.astype(F32)
        kt = mkt_ref[h * dk:(h + 1) * dk, :]
        kx = jnp.concatenate([jnp.where(lane_batch == b, kt, 0.0) for b in range(nb)], axis=0)
        upd = _dot(kx.astype(BF16), wkv)
        for b in range(nb):
            idx = b * MLSTM_HEADS + h
            wc_b = wc[b * tq:b * tq + 1]
            co_ref[state_rows(b, h), :] = wc_b * c_ref[state_rows(b, h), :] + upd[b * dk:(b + 1) * dk]
            no_ref[idx] = wc_b * n_ref[idx] + jnp.sum(wkk[b * tq:(b + 1) * tq], axis=0, keepdims=True)
        mt_all = jnp.where(lane == h, mt, mt_all)
    mt_ref[...] = mt_all


def _mlstm_sample(mq, mk, mkt, mv, og, gcol, grow, m_rows, ct_in, n_in, g_head, nbatch, nb):
    t = mq.shape[0]
    tq = t // nbatch
    rows = nb * tq
    assert tq & (tq - 1) == 0 and rows == LANES and nb % 2 == 0

    def tok(width):
        return pl.BlockSpec((rows, width), lambda i: (i, 0))

    ct_spec = pl.BlockSpec((nb * MLSTM_HEADS * MLSTM_DK, MLSTM_DV), lambda i: (i, 0))
    n_spec = pl.BlockSpec((nb * MLSTM_HEADS, 1, MLSTM_DK), lambda i: (i, 0, 0))
    return pl.pallas_call(
        functools.partial(_mlstm_sample_kernel, nb=nb, tq=tq),
        grid=(nbatch // nb,),
        in_specs=[
            tok(MQK_WIDTH), tok(MQK_WIDTH),
            pl.BlockSpec((MQK_WIDTH, rows), lambda i: (0, i)),
            tok(MLSTM_WIDTH), tok(MLSTM_WIDTH), tok(LANES),
            pl.BlockSpec((GATE_ROWS, rows), lambda i: (0, i)),
            tok(LANES),
            ct_spec, n_spec,
            _const_spec((1, MLSTM_WIDTH)),
        ],
        out_specs=[tok(MLSTM_WIDTH), ct_spec, n_spec, tok(LANES)],
        out_shape=[
            jax.ShapeDtypeStruct((t, MLSTM_WIDTH), F32),
            jax.ShapeDtypeStruct(ct_in.shape, F32),
            jax.ShapeDtypeStruct(n_in.shape, F32),
            jax.ShapeDtypeStruct((t, LANES), F32),
        ],
        compiler_params=pltpu.CompilerParams(
            dimension_semantics=("arbitrary",), vmem_limit_bytes=VMEM_LIMIT),
    )(mq, mk, mkt, mv, og, gcol, grow, m_rows, ct_in, n_in, g_head)


def _memkv_kernel(mem_ref, g_ref, wk_ref, wv_ref, k_ref, v_ref):
    mn = _rmsnorm(mem_ref[...], g_ref[...]).astype(BF16)
    k_ref[...] = _dot(mn, wk_ref[...])
    v_ref[...] = _dot(mn, wv_ref[...])


def _memkv(mem, w, nbatch):
    t, d = mem.shape
    m = t // nbatch
    spec = pl.BlockSpec((m, CROSS_WIDTH), lambda b: (b, 0))
    return pl.pallas_call(
        _memkv_kernel,
        grid=(nbatch,),
        in_specs=[
            pl.BlockSpec((m, d), lambda b: (b, 0)),
            _const_spec((1, d)), _const_spec((d, CROSS_WIDTH)), _const_spec((d, CROSS_WIDTH)),
        ],
        out_specs=[spec, spec],
        out_shape=[jax.ShapeDtypeStruct((t, CROSS_WIDTH), F32)] * 2,
        compiler_params=pltpu.CompilerParams(
            dimension_semantics=("arbitrary",), vmem_limit_bytes=VMEM_LIMIT),
    )(mem, w["g_mem"], w["w_ck"], w["w_cv"])


def _mix_stage(x, att, hm, wo_ref, gcr_ref, wcq_ref):
    cat = jnp.concatenate([att.astype(BF16), hm.astype(BF16)], axis=1)
    y1 = x + _dot(cat, wo_ref[...])
    qc = _dot(_rmsnorm(y1, gcr_ref[...]).astype(BF16), wcq_ref[...])
    return y1, qc


def _cross_attention(qc, mk, mv):
    slabs = []
    for j in range(CROSS_WIDTH // LANES):
        qs = qc[:, j * LANES:(j + 1) * LANES]
        ks = mk[:, j * LANES:(j + 1) * LANES]
        vs = mv[:, j * LANES:(j + 1) * LANES]
        n = qs.shape[0]
        lo = _low_half(qs.shape)
        zero = jnp.zeros_like(qs)
        q2 = jnp.concatenate([jnp.where(lo, qs, zero), jnp.where(lo, zero, qs)], axis=0)
        s = _dot_nt(q2 * (HEAD_DIM ** -0.5), ks)
        p = jnp.exp(s - jnp.max(s, axis=-1, keepdims=True))
        o2 = _dot(p.astype(BF16), vs) / jnp.sum(p, axis=-1, keepdims=True)
        slabs.append(jnp.where(lo, o2[:n], o2[n:]))
    return jnp.concatenate(slabs, axis=1)


FFN_CHUNK = 256


def _ffn_stage(y1, o, wco_ref, gf_ref, wgu_ref, wd_ref, act_ref, gfin_ref, final, between=None):
    y2 = y1 + _dot(o.astype(BF16), wco_ref[...])
    xn = _rmsnorm(y2, gf_ref[...]).astype(BF16)
    for c in range(wd_ref.shape[0] // FFN_CHUNK):
        gu = _dot(xn, wgu_ref[:, 2 * c * FFN_CHUNK:2 * (c + 1) * FFN_CHUNK])
        g, u = gu[:, :FFN_CHUNK], gu[:, FFN_CHUNK:]
        act_ref[:, c * FFN_CHUNK:(c + 1) * FFN_CHUNK] = (g * jax.nn.sigmoid(g) * u).astype(BF16)
        if between is not None:
            between()
    y3 = y2 + _dot(act_ref[...], wd_ref[...])
    return _rmsnorm(y3, gfin_ref[...]) if final else y3


def _post_prompt_kernel(sink_ref, q_ref, kc_ref, kp_ref, vc_ref, vp_ref, x_ref, hm_ref, mk_ref, mv_ref,
                        wo_ref, gcr_ref, wcq_ref, wco_ref, gf_ref, wgu_ref, wd_ref, gfin_ref,
                        o_ref, act_ref, att_ref, *, final, ntiles, tiles_per_seq):
    s = pl.program_id(0)
    tile = jnp.minimum(s, ntiles - 1)
    first_block = (tile % tiles_per_seq) == 0
    slot = s % 2

    def swa_stages():
        def store(j, value):
            att_ref[slot, j * WINDOW:(j + 1) * WINDOW, :] = value.astype(att_ref.dtype)
        return _swa_prompt_stages(sink_ref, q_ref, kc_ref, kp_ref, vc_ref, vp_ref, first_block, store)

    @pl.when(s == 0)
    def _():
        for _ in swa_stages():
            pass

    @pl.when(s > 0)
    def _():
        swa = swa_stages()

        def advance(n=SWA_STAGES_PER_FFN_CHUNK):
            for _ in range(n):
                next(swa, None)

        y1, qc = _mix_stage(x_ref[...], att_ref[1 - slot], hm_ref[...], wo_ref, gcr_ref, wcq_ref)
        advance()
        o = _cross_attention(qc.astype(BF16), mk_ref[...].astype(BF16), mv_ref[...].astype(BF16))
        advance()
        o_ref[...] = _ffn_stage(y1, o, wco_ref, gf_ref, wgu_ref, wd_ref, act_ref, gfin_ref, final,
                                between=advance)
        for _ in swa:
            pass


_MIX_WEIGHTS = ("w_out", "g_cross", "w_cq")
_FFN_WEIGHTS = ("w_co", "g_ffn", "w_gate_up", "w_down", "g_final")


def _post_prompt(x, q, k, v, hm, memk, memv, sinks, w, nbatch, final, tm):
    t, d = x.shape
    ntiles = t // tm
    tpb = ntiles // nbatch
    wpt = tm // WINDOW
    m = memk.shape[0] // nbatch
    hidden = w["w_down"].shape[0]

    def attn_tile(s):
        return jnp.minimum(s, ntiles - 1)

    def post_tile(s):
        return jnp.maximum(s - 1, 0)

    def attn_rows(width):
        return pl.BlockSpec((tm, width), lambda s: (attn_tile(s), 0))

    def prev_window(width):
        return pl.BlockSpec((WINDOW, width), lambda s: (jnp.maximum(attn_tile(s) * wpt - 1, 0), 0))

    def post_rows(width):
        return pl.BlockSpec((tm, width), lambda s: (post_tile(s), 0))

    mem_spec = pl.BlockSpec((m, CROSS_WIDTH), lambda s: (post_tile(s) // tpb, 0))
    names = _MIX_WEIGHTS + _FFN_WEIGHTS
    return pl.pallas_call(
        functools.partial(_post_prompt_kernel, final=final, ntiles=ntiles, tiles_per_seq=tpb),
        grid=(ntiles + 1,),
        in_specs=[
            pl.BlockSpec(memory_space=pltpu.SMEM),
            attn_rows(ATT_WIDTH), attn_rows(KV_WIDTH), prev_window(KV_WIDTH),
            attn_rows(KV_WIDTH), prev_window(KV_WIDTH),
            post_rows(d), post_rows(MLSTM_WIDTH), mem_spec, mem_spec,
        ] + _weight_specs(w, names),
        out_specs=post_rows(d),
        out_shape=jax.ShapeDtypeStruct((t, d), F32),
        scratch_shapes=[pltpu.VMEM((tm, hidden), BF16), pltpu.VMEM((2, tm, ATT_WIDTH), BF16)],
        compiler_params=pltpu.CompilerParams(
            dimension_semantics=("arbitrary",), vmem_limit_bytes=VMEM_LIMIT),
    )(sinks, q, k, k, v, v, x, hm, memk, memv, *[w[n] for n in names])


def _mix_kernel(x_ref, att_ref, hm_ref, wo_ref, gcr_ref, wcq_ref, y_ref, qc_ref):
    y1, qc = _mix_stage(x_ref[...], att_ref[...], hm_ref[...], wo_ref, gcr_ref, wcq_ref)
    y_ref[...] = y1
    qc_ref[...] = qc


def _mix(x, att, hm, w, tm):
    t, d = x.shape

    def rows(width):
        return pl.BlockSpec((tm, width), lambda i: (i, 0))

    return pl.pallas_call(
        _mix_kernel,
        grid=(t // tm,),
        in_specs=[rows(d), rows(ATT_WIDTH), rows(MLSTM_WIDTH)] + _weight_specs(w, _MIX_WEIGHTS),
        out_specs=[rows(d), rows(CROSS_WIDTH)],
        out_shape=[jax.ShapeDtypeStruct((t, d), F32), jax.ShapeDtypeStruct((t, CROSS_WIDTH), F32)],
        compiler_params=pltpu.CompilerParams(
            dimension_semantics=("arbitrary",), vmem_limit_bytes=VMEM_LIMIT),
    )(x, att, hm, *[w[n] for n in _MIX_WEIGHTS])


def _cross_sample_kernel(qc_ref, mk_ref, mv_ref, o_ref, *, nb, tq):
    mem = CROSS_WIDTH
    n = CROSS_HEADS * tq
    row_head = lax.broadcasted_iota(jnp.int32, (n, CROSS_WIDTH), 0) >> _log2(tq)
    lane_head = lax.broadcasted_iota(jnp.int32, (n, CROSS_WIDTH), 1) >> _log2(HEAD_DIM)
    own = row_head == lane_head
    for b in range(nb):
        q = qc_ref[b * tq:(b + 1) * tq, :]
        qx = jnp.where(own, jnp.concatenate([q] * CROSS_HEADS, axis=0), 0.0).astype(BF16)
        mk = mk_ref[b * mem:(b + 1) * mem, :].astype(BF16)
        mv = mv_ref[b * mem:(b + 1) * mem, :].astype(BF16)
        s = _dot(qx, mk) * (HEAD_DIM ** -0.5)
        p = jnp.exp(s - jnp.max(s, axis=-1, keepdims=True))
        pv = _dot_nt(p.astype(BF16), mv) / jnp.sum(p, axis=-1, keepdims=True)
        pv = jnp.where(own, pv, 0.0)
        out = pv[0:tq]
        for h in range(1, CROSS_HEADS):
            out = out + pv[h * tq:(h + 1) * tq]
        o_ref[b * tq:(b + 1) * tq, :] = out


def _cross_sample(qc, memk_t, memv_t, nbatch, nb):
    t = qc.shape[0]
    tq = t // nbatch
    mem = memk_t.shape[1]
    tok = pl.BlockSpec((nb * tq, CROSS_WIDTH), lambda i: (i, 0))
    mspec = pl.BlockSpec((nb * CROSS_WIDTH, mem), lambda i: (i, 0))
    return pl.pallas_call(
        functools.partial(_cross_sample_kernel, nb=nb, tq=tq),
        grid=(nbatch // nb,),
        in_specs=[tok, mspec, mspec],
        out_specs=tok,
        out_shape=jax.ShapeDtypeStruct((t, CROSS_WIDTH), F32),
        compiler_params=pltpu.CompilerParams(
            dimension_semantics=("arbitrary",), vmem_limit_bytes=VMEM_LIMIT),
    )(qc, memk_t, memv_t)


def _ffn_kernel(y_ref, o_ref, wco_ref, gf_ref, wgu_ref, wd_ref, gfin_ref, out_ref, act_ref, *, final):
    out_ref[...] = _ffn_stage(y_ref[...], o_ref[...], wco_ref, gf_ref, wgu_ref, wd_ref, act_ref,
                              gfin_ref, final)


def _ffn(y1, o, w, final, tm):
    t, d = y1.shape
    hidden = w["w_down"].shape[0]

    def rows(width):
        return pl.BlockSpec((tm, width), lambda i: (i, 0))

    return pl.pallas_call(
        functools.partial(_ffn_kernel, final=final),
        grid=(t // tm,),
        in_specs=[rows(d), rows(CROSS_WIDTH)] + _weight_specs(w, _FFN_WEIGHTS),
        out_specs=rows(d),
        out_shape=jax.ShapeDtypeStruct((t, d), F32),
        scratch_shapes=[pltpu.VMEM((tm, hidden), BF16)],
        compiler_params=pltpu.CompilerParams(
            dimension_semantics=("arbitrary",), vmem_limit_bytes=VMEM_LIMIT),
    )(y1, o, *[w[n] for n in _FFN_WEIGHTS])


def _layer_weights(l, w_in, b_igate, b_fgate, g_mlstm_head, g_mix, g_cross, g_mem, w_cq, w_ck, w_cv,
                   w_co, g_ffn, g_final):
    gate_w = w_in[l][:, MAIN_WIDTH:]
    gate_b = jnp.concatenate([b_igate[l], b_fgate[l]]).astype(F32)
    ngate = gate_b.shape[0]
    row = lambda a: a.astype(F32).reshape(1, -1)
    w_t = w_in[l].T
    k0 = ATT_WIDTH
    mk0 = ATT_WIDTH + 2 * KV_WIDTH + MQK_WIDTH
    return {
        "w_kvk_t": jnp.concatenate([w_t[k0:k0 + 2 * KV_WIDTH], w_t[mk0:mk0 + MQK_WIDTH]]).astype(BF16),
        "w_main": w_in[l][:, :MAIN_WIDTH].astype(BF16),
        "w_gate_cols": jnp.pad(gate_w, ((0, 0), (0, LANES - ngate))).astype(BF16),
        "w_gate_rows": jnp.pad(gate_w.T, ((0, GATE_ROWS - ngate), (0, 0))).astype(BF16),
        "b_gate_cols": jnp.pad(gate_b, (0, LANES - ngate)).reshape(1, LANES),
        "b_gate_rows": jnp.pad(gate_b, (0, GATE_ROWS - ngate)).reshape(GATE_ROWS, 1),
        "g_mix": row(g_mix[l]), "g_cross": row(g_cross[l]), "g_mem": row(g_mem[l]),
        "g_ffn": row(g_ffn[l]), "g_final": row(g_final), "g_head": row(g_mlstm_head[l]),
        "w_cq": w_cq[l].astype(BF16), "w_ck": w_ck[l].astype(BF16),
        "w_cv": w_cv[l].astype(BF16), "w_co": w_co[l].astype(BF16),
    }


TOKEN_TILE = 512
MLSTM_STAGES_PER_PROJ_STAGE = 15
SWA_STAGES_PER_FFN_CHUNK = 3
MLSTM_CHUNK_ROWS = 256
SWA_SAMPLE_BATCH = 16
MLSTM_SAMPLE_BATCH = 16
CROSS_SAMPLE_BATCH = 16


def kernel(x_prompt, x_sample, mem_prompt, cache_swa_k, cache_swa_v, state_mlstm_C, state_mlstm_n,
           state_mlstm_m, cache_mem_k, cache_mem_v, w_in, b_igate, b_fgate, attn_sinks,
           g_mlstm_head, w_out, g_mix, g_cross, g_mem, w_cq, w_ck, w_cv, w_co, g_ffn, w_gate,
           w_up, w_down, g_final):
    depth = w_in.shape[0]
    bp, sp, d = x_prompt.shape
    bs, ss, _ = x_sample.shape
    mem_tokens = mem_prompt.shape[1]
    past = cache_swa_k.shape[2]
    yp = x_prompt.reshape(bp * sp, d)
    ys = x_sample.reshape(bs * ss, d)
    mem = mem_prompt.reshape(bp * mem_tokens, d)
    outs = [[] for _ in range(12)]

    for l in range(depth):
        final = l == depth - 1
        w = _layer_weights(l, w_in, b_igate, b_fgate, g_mlstm_head, g_mix, g_cross, g_mem, w_cq,
                           w_ck, w_cv, w_co, g_ffn, g_final)
        sinks = attn_sinks[l].astype(F32)

        q, k, v, hm, c_p, n_p, m_p, cast = _mixer_prompt(
            yp, w, w_gate[l].astype(F32), w_up[l].astype(F32), w_down[l].astype(F32),
            w_out[l].astype(F32), bp, MLSTM_CHUNK_ROWS)
        w.update(cast)
        memk, memv = _memkv(mem, w, bp)
        yp = _post_prompt(yp, q, k, v, hm, memk, memv, sinks, w, bp, final, TOKEN_TILE)
        win_shape = (bp, WINDOW, ATT_KV_HEADS, HEAD_DIM)
        outs[0].append(k.reshape(bp, sp, KV_WIDTH)[:, sp - WINDOW:].reshape(win_shape))
        outs[1].append(v.reshape(bp, sp, KV_WIDTH)[:, sp - WINDOW:].reshape(win_shape))
        outs[2].append(c_p)
        outs[3].append(n_p)
        outs[4].append(m_p)
        mem_shape = (bp, mem_tokens, CROSS_HEADS, HEAD_DIM)
        outs[5].append(memk.reshape(mem_shape))
        outs[6].append(memv.reshape(mem_shape))

        def tokens_last(a):
            return jnp.transpose(a.astype(F32), (0, 2, 3, 1)).reshape(-1, a.shape[1])

        q, mq, mk, mv, og, grow, gcol, knt, vnt, mkt = _decode_proj(ys, w, TOKEN_TILE)
        att, kbuf_t, vbuf_t = _swa_sample(
            q, knt, vnt, tokens_last(cache_swa_k[l]), tokens_last(cache_swa_v[l]), sinks, bs,
            SWA_SAMPLE_BATCH)
        m_rows = jnp.pad(jnp.repeat(state_mlstm_m[l].astype(F32), ss, axis=0),
                         ((0, 0), (0, LANES - MLSTM_HEADS)))
        ct_in = jnp.swapaxes(state_mlstm_C[l].astype(F32), 2, 3).reshape(-1, MLSTM_DV)
        hm, ct_s, n_s, mt = _mlstm_sample(
            mq, mk, mkt, mv, og, gcol, grow, m_rows, ct_in,
            state_mlstm_n[l].astype(F32).reshape(bs * MLSTM_HEADS, 1, MLSTM_DK),
            w["g_head"], bs, MLSTM_SAMPLE_BATCH)
        y1, qc = _mix(ys, att, hm, w, TOKEN_TILE)
        o = _cross_sample(qc, tokens_last(cache_mem_k[l]), tokens_last(cache_mem_v[l]), bs,
                          CROSS_SAMPLE_BATCH)
        ys = _ffn(y1, o, w, final, TOKEN_TILE)
        buf_t_shape = (bs, ATT_KV_HEADS, HEAD_DIM, past)
        outs[7].append(jnp.transpose(kbuf_t.reshape(buf_t_shape), (0, 3, 1, 2)))
        outs[8].append(jnp.transpose(vbuf_t.reshape(buf_t_shape), (0, 3, 1, 2)))
        outs[9].append(jnp.swapaxes(ct_s.reshape(bs, MLSTM_HEADS, MLSTM_DK, MLSTM_DV), 2, 3))
        outs[10].append(n_s.reshape(bs, MLSTM_HEADS, MLSTM_DK))
        outs[11].append(mt.reshape(bs, ss, LANES)[:, ss - 1, :MLSTM_HEADS])

    return (yp.reshape(bp, sp, d), ys.reshape(bs, ss, d)) + tuple(jnp.stack(o) for o in outs)
```

```python
import functools

import jax
import jax.numpy as jnp
from jax import lax
from jax.experimental import pallas as pl
from jax.experimental.pallas import tpu as pltpu

F32 = jnp.float32
BF16 = jnp.bfloat16

HEAD_DIM = 64
ATT_GROUP = 4
ATT_KV_HEADS = 2
ATT_WIDTH = 512
KV_WIDTH = ATT_KV_HEADS * HEAD_DIM
WINDOW = 128
LOG_WINDOW = 7
MLSTM_HEADS = 4
MLSTM_DV = 128
MLSTM_DK = 64
MQK_WIDTH = MLSTM_HEADS * MLSTM_DK
MLSTM_WIDTH = MLSTM_HEADS * MLSTM_DV
MAIN_WIDTH = ATT_WIDTH + 2 * KV_WIDTH + 2 * MQK_WIDTH + 2 * MLSTM_WIDTH
CROSS_HEADS = 4
CROSS_WIDTH = CROSS_HEADS * HEAD_DIM
EPS = 1e-6
NEG_INF = float("-inf")
assert HEAD_DIM == 4 ** 3

LANES = 128
GATE_ROWS = 16
VMEM_LIMIT = 56 * 1024 * 1024

NT = (((1,), (1,)), ((), ()))
TN = (((0,), (0,)), ((), ()))


def _dot(a, b):
    return jnp.dot(a, b, preferred_element_type=F32)


def _dot_nt(a, b):
    return lax.dot_general(a, b, NT, preferred_element_type=F32)


def _dot_tn(a, b):
    return lax.dot_general(a, b, TN, preferred_element_type=F32)


def _rmsnorm(x, g):
    return x * lax.rsqrt(jnp.mean(x * x, axis=-1, keepdims=True) + EPS) * g


def _log_sigmoid(x):
    return jnp.minimum(x, 0.0) - jnp.log1p(jnp.exp(-jnp.abs(x)))


def _split3(x):
    hi = x.astype(BF16)
    r1 = x - hi.astype(F32)
    mid = r1.astype(BF16)
    lo = (r1 - mid.astype(F32)).astype(BF16)
    return hi, mid, lo


def _cumsum_cols(tri, x):
    hi, mid, lo = _split3(x)
    return _dot(tri, hi) + _dot(tri, mid) + _dot(tri, lo)


def _cumsum_rows(x, tri):
    hi, mid, lo = _split3(x)
    return _dot(hi, tri) + _dot(mid, tri) + _dot(lo, tri)


def _log2(n):
    assert n > 0 and n & (n - 1) == 0, n
    return n.bit_length() - 1


def _low_half(shape):
    return lax.broadcasted_iota(jnp.int32, shape, 1) < HEAD_DIM


def _decode_proj_kernel(x_ref, g_ref, w_ref, wgr_ref, br_ref, wgc_ref, bc_ref, wt_ref,
                        q_ref, mq_ref, mk_ref, mv_ref, og_ref, gr_ref, gc_ref, kt_ref, vt_ref, mkt_ref):
    xn = _rmsnorm(x_ref[...], g_ref[...]).astype(BF16)
    zt = _dot_nt(wt_ref[...], xn)
    kt_ref[...] = zt[:KV_WIDTH]
    vt_ref[...] = zt[KV_WIDTH:2 * KV_WIDTH]
    mkt_ref[...] = zt[2 * KV_WIDTH:] * (MLSTM_DK ** -0.5)

    def mm(lo, hi):
        return _dot(xn, w_ref[:, lo:hi])

    o = 0
    q_ref[...] = mm(o, o + ATT_WIDTH)
    o += ATT_WIDTH + 2 * KV_WIDTH
    mqk = mm(o, o + 2 * MQK_WIDTH)
    mq_ref[...] = mqk[:, :MQK_WIDTH]
    mk_ref[...] = mqk[:, MQK_WIDTH:] * (MLSTM_DK ** -0.5)
    o += 2 * MQK_WIDTH
    mv_ref[...] = mm(o, o + MLSTM_WIDTH)
    o += MLSTM_WIDTH
    og_ref[...] = mm(o, o + MLSTM_WIDTH)

    zc = _dot(xn, wgc_ref[...]) + bc_ref[...]
    lane = lax.broadcasted_iota(jnp.int32, zc.shape, 1)
    gc_ref[...] = jnp.where(lane < MLSTM_HEADS, zc, _log_sigmoid(zc))
    zr = _dot_nt(wgr_ref[...], xn) + br_ref[...]
    row = lax.broadcasted_iota(jnp.int32, zr.shape, 0)
    gr_ref[...] = jnp.where(row < MLSTM_HEADS, zr, _log_sigmoid(zr))


def _const_spec(shape):
    nd = len(shape)
    return pl.BlockSpec(shape, lambda *_: (0,) * nd, pipeline_mode=pl.Buffered(1))


def _weight_specs(w, names):
    return [_const_spec(w[n].shape) for n in names]


def _decode_proj(x, w, tm):
    t, d = x.shape

    def rows(width):
        return pl.BlockSpec((tm, width), lambda i: (i, 0))

    def lanes(height):
        return pl.BlockSpec((height, tm), lambda i: (0, i))

    outs = [(rows, ATT_WIDTH), (rows, MQK_WIDTH), (rows, MQK_WIDTH), (rows, MLSTM_WIDTH),
            (rows, MLSTM_WIDTH), (lanes, GATE_ROWS), (rows, LANES), (lanes, KV_WIDTH),
            (lanes, KV_WIDTH), (lanes, MQK_WIDTH)]
    names = ["g_mix", "w_main", "w_gate_rows", "b_gate_rows", "w_gate_cols", "b_gate_cols", "w_kvk_t"]
    return pl.pallas_call(
        _decode_proj_kernel,
        grid=(t // tm,),
        in_specs=[rows(d)] + _weight_specs(w, names),
        out_specs=[kind(size) for kind, size in outs],
        out_shape=[jax.ShapeDtypeStruct((t, size) if kind is rows else (size, t), F32)
                   for kind, size in outs],
        compiler_params=pltpu.CompilerParams(
            dimension_semantics=("arbitrary",), vmem_limit_bytes=VMEM_LIMIT),
    )(x, *[w[n] for n in names])


def _dup_heads(x):
    swapped = pltpu.roll(x, HEAD_DIM, axis=1)
    lo = _low_half(x.shape)
    return (jnp.where(lo, x, swapped).astype(BF16), jnp.where(lo, swapped, x).astype(BF16))


def _stack_query_heads(q, h):
    parts = []
    for g in range(ATT_GROUP):
        hq = h * ATT_GROUP + g
        slab = q[:, (hq // 2) * LANES:(hq // 2 + 1) * LANES]
        lo = _low_half(slab.shape)
        keep = lo if hq % 2 == 0 else jnp.logical_not(lo)
        parts.append(jnp.where(keep, slab, jnp.zeros_like(slab)))
    return jnp.concatenate(parts, axis=0).astype(BF16)


def _sink_column(sink_ref, h, rows_per_head):
    n = ATT_GROUP * rows_per_head
    grp = lax.broadcasted_iota(jnp.int32, (n, 1), 0) >> _log2(rows_per_head)
    col = jnp.full((n, 1), sink_ref[h * ATT_GROUP], F32)
    for g in range(1, ATT_GROUP):
        col = jnp.where(grp == g, sink_ref[h * ATT_GROUP + g], col)
    return col


def _swa_prompt_stages(sink_ref, q_ref, kc_ref, kp_ref, vc_ref, vp_ref, first_block, store):
    qb = q_ref.shape[0]
    k_all = jnp.concatenate([kp_ref[...], kc_ref[...]], axis=0)
    v_all = jnp.concatenate([vp_ref[...], vc_ref[...]], axis=0)
    kd = _dup_heads(k_all)
    v_t = v_all.T.astype(BF16)

    n = ATT_GROUP * WINDOW
    key = lax.broadcasted_iota(jnp.int32, (2 * WINDOW, n), 0)
    qry = lax.broadcasted_iota(jnp.int32, (2 * WINDOW, n), 1) & (WINDOW - 1)
    delta = key - qry
    in_band = (delta >= 0) & (delta <= WINDOW)
    grp = lax.broadcasted_iota(jnp.int32, (1, n), 1) >> LOG_WINDOW
    sinks = []
    for h in range(ATT_KV_HEADS):
        row = jnp.full((1, n), sink_ref[h * ATT_GROUP], F32)
        for g in range(1, ATT_GROUP):
            row = jnp.where(grp == g, sink_ref[h * ATT_GROUP + g], row)
        sinks.append(row)
    yield

    probs = []
    for j in range(qb // WINDOW):
        valid = in_band & (key >= jnp.where(first_block, WINDOW, 0)) if j == 0 else in_band
        q = q_ref[j * WINDOW:(j + 1) * WINDOW, :] * (HEAD_DIM ** -0.5)
        band = slice(j * WINDOW, (j + 2) * WINDOW)
        for h in range(ATT_KV_HEADS):
            qh = _stack_query_heads(q, h)
            s = _dot_nt(kd[h][band], qh)
            probs.append(dict(h=h, band=band, s=jnp.where(valid, s, NEG_INF)))
            yield
    for p in probs:
        p["m"] = jnp.maximum(jnp.max(p["s"], axis=0, keepdims=True), sinks[p["h"]])
        yield
    for p in probs:
        p["p"] = jnp.exp(p["s"] - p["m"])
        yield
    for p in probs:
        h = p["h"]
        denom = jnp.sum(p["p"], axis=0, keepdims=True) + jnp.exp(sinks[h] - p["m"])
        p["o_t"] = _dot(v_t[h * HEAD_DIM:(h + 1) * HEAD_DIM, p["band"]], p["p"].astype(BF16)) / denom
        yield
    for j in range(qb // WINDOW):
        pieces = [p["o_t"][:, g * WINDOW:(g + 1) * WINDOW]
                  for p in probs[j * ATT_KV_HEADS:(j + 1) * ATT_KV_HEADS] for g in range(ATT_GROUP)]
        store(j, jnp.concatenate(pieces, axis=0).T)
        yield


def _swa_sample_kernel(sink_ref, q_ref, knt_ref, vnt_ref, ck_ref, cv_ref,
                       o_ref, kbuf_ref, vbuf_ref, *, nb, tq):
    past = ck_ref.shape[1]
    hw = ATT_KV_HEADS * HEAD_DIM
    heads = ATT_KV_HEADS * ATT_GROUP
    n = heads * tq
    fresh = past - tq
    t = lax.broadcasted_iota(jnp.int32, (n, 2 * past), 0) & (tq - 1)
    col = lax.broadcasted_iota(jnp.int32, (n, 2 * past), 1)
    valid = ((col < past) & (col >= t)) | ((col >= past + fresh) & (col - (past + fresh) <= t))
    sink = jnp.concatenate([_sink_column(sink_ref, h, tq) for h in range(ATT_KV_HEADS)], axis=0)
    lane = lax.broadcasted_iota(jnp.int32, (hw, past), 1)
    knt = knt_ref[...]
    vnt = vnt_ref[...]
    zero_slab = jnp.zeros((tq, LANES), F32)
    lo = _low_half((tq, LANES))

    def shifted(old, new_t, b):
        return jnp.where(lane >= fresh, pltpu.roll(new_t, (fresh - b * tq) % past, axis=1),
                         pltpu.roll(old, fresh, axis=1))

    def dup_rows(x):
        h0, h1 = x[:HEAD_DIM], x[HEAD_DIM:]
        return jnp.concatenate([h0, h0, h1, h1], axis=0)

    for b in range(nb):
        k_old = ck_ref[b * hw:(b + 1) * hw, :]
        v_old = cv_ref[b * hw:(b + 1) * hw, :]
        k_new = shifted(k_old, knt, b)
        v_new = shifted(v_old, vnt, b)
        kbuf_ref[b * hw:(b + 1) * hw, :] = k_new
        vbuf_ref[b * hw:(b + 1) * hw, :] = v_new
        kop = jnp.concatenate([dup_rows(k_old), dup_rows(k_new)], axis=1).astype(BF16)
        vop = jnp.concatenate([dup_rows(v_old), dup_rows(v_new)], axis=1).astype(BF16)

        q = q_ref[b * tq:(b + 1) * tq, :]
        blocks = []
        for hq in range(heads):
            slab = q[:, (hq // 2) * LANES:(hq // 2 + 1) * LANES]
            slab = jnp.where(lo if hq % 2 == 0 else jnp.logical_not(lo), slab, 0.0)
            pair = [slab, zero_slab] if hq // ATT_GROUP == 0 else [zero_slab, slab]
            blocks.append(jnp.concatenate(pair, axis=1))
        qx = jnp.concatenate(blocks, axis=0).astype(BF16)

        s = _dot(qx, kop) * (HEAD_DIM ** -0.5)
        s = jnp.where(valid, s, NEG_INF)
        m = jnp.maximum(jnp.max(s, axis=-1, keepdims=True), sink)
        p = jnp.exp(s - m)
        denom = jnp.sum(p, axis=-1, keepdims=True) + jnp.exp(sink - m)
        o = _dot_nt(p.astype(BF16), vop) / denom

        slabs = []
        for j in range(heads // 2):
            h = (2 * j) // ATT_GROUP
            even = o[(2 * j) * tq:(2 * j + 1) * tq, h * LANES:(h + 1) * LANES]
            odd = o[(2 * j + 1) * tq:(2 * j + 2) * tq, h * LANES:(h + 1) * LANES]
            slabs.append(jnp.where(lo, even, odd))
        o_ref[b * tq:(b + 1) * tq, :] = jnp.concatenate(slabs, axis=1)


def _swa_sample(q, knt, vnt, cache_kt, cache_vt, sinks, nbatch, nb):
    t = q.shape[0]
    tq = t // nbatch
    past = cache_kt.shape[1]
    hw = ATT_KV_HEADS * HEAD_DIM
    assert past == WINDOW and tq & (tq - 1) == 0 and nb * tq == past

    def rows(r, width):
        return pl.BlockSpec((r, width), lambda i: (i, 0))

    new_t = pl.BlockSpec((hw, nb * tq), lambda i: (0, i))
    return pl.pallas_call(
        functools.partial(_swa_sample_kernel, nb=nb, tq=tq),
        grid=(nbatch // nb,),
        in_specs=[
            pl.BlockSpec(memory_space=pltpu.SMEM),
            rows(nb * tq, ATT_WIDTH), new_t, new_t, rows(nb * hw, past), rows(nb * hw, past),
        ],
        out_specs=[rows(nb * tq, ATT_WIDTH), rows(nb * hw, past), rows(nb * hw, past)],
        out_shape=[
            jax.ShapeDtypeStruct((t, ATT_WIDTH), F32),
            jax.ShapeDtypeStruct(cache_kt.shape, F32),
            jax.ShapeDtypeStruct(cache_vt.shape, F32),
        ],
        compiler_params=pltpu.CompilerParams(
            dimension_semantics=("arbitrary",), vmem_limit_bytes=VMEM_LIMIT),
    )(sinks, q, knt, vnt, cache_kt, cache_vt)


def _mlstm_intra(q, k, v, it_c, bt_c, it_r, bt_r, m_c, valid):
    d = jnp.where(valid, bt_c - bt_r + it_r, NEG_INF)
    inter = bt_c + m_c
    mt = jnp.maximum(jnp.max(d, axis=-1, keepdims=True), inter)
    sm = _dot_nt(q, k) * jnp.exp(d - mt)
    wi = jnp.exp(inter - mt)
    return _dot(sm.astype(BF16), v), jnp.sum(sm, axis=-1, keepdims=True), mt, wi


def _mlstm_head_out(num, den, mt, og, g_head):
    lower = jnp.maximum(jnp.abs(den), jnp.exp(-mt))
    h = num / lower
    hn = h * lax.rsqrt(jnp.mean(h * h, axis=-1, keepdims=True) + EPS)
    return hn * g_head * jax.nn.sigmoid(og)


def _mlstm_prompt_stages(mq, mk, mv, og, gr, gh_ref, hm_ref, c_ref, n_ref, m_ref, nbatch, chunk):
    s_idx = lax.broadcasted_iota(jnp.int32, (chunk, chunk), 0)
    l_idx = lax.broadcasted_iota(jnp.int32, (chunk, chunk), 1)
    causal = s_idx <= l_idx
    triu = jnp.where(causal, 1.0, 0.0).astype(BF16)
    lo = _low_half((chunk, LANES))
    pad_rows = jnp.zeros((LANES - MLSTM_HEADS, chunk), F32)
    sub8 = (GATE_ROWS, chunk)

    probs = []
    for b in range(nbatch):
        g = gr(b)
        btr = _cumsum_rows(g, triu)
        it_rows = g[:MLSTM_HEADS]
        bt_rows = btr[MLSTM_HEADS:2 * MLSTM_HEADS]
        u_cols = jnp.concatenate([it_rows - bt_rows, pad_rows], axis=0).T
        for h in range(MLSTM_HEADS):
            slab = slice((h // 2) * LANES, (h // 2 + 1) * LANES)
            keep = lo if h % 2 == 0 else jnp.logical_not(lo)
            q = mq(b, slab)
            k = mk(b, slab)
            probs.append(dict(
                b=b, h=h, idx=b * MLSTM_HEADS + h, cols=slice(h * MLSTM_DV, (h + 1) * MLSTM_DV),
                q=jnp.where(keep, q, jnp.zeros_like(q)), k=jnp.where(keep, k, jnp.zeros_like(k)),
                u_c=u_cols[:, h:h + 1], it_r=it_rows[h:h + 1, :], bt_r=bt_rows[h:h + 1, :]))
        yield

    for p in probs:
        p["m_old"] = m_ref[p["idx"]][:, 0:1]
        p["d"] = jnp.where(causal, p["bt_r"] + p["u_c"], NEG_INF)
        p["inter"] = p["bt_r"] + p["m_old"]
        p["q_t"] = p["q"].astype(F32).T.astype(BF16)
        p["kq"] = _dot(p["k"], p["q_t"])
        p["v_t"] = mv(p["b"], p["cols"]).astype(F32).T
        yield
    for p in probs:
        p["mt"] = jnp.maximum(jnp.max(p["d"], axis=0, keepdims=True), p["inter"])
        yield
    for p in probs:
        p["sm"] = p["kq"] * jnp.exp(p["d"] - p["mt"])
        p["wi"] = jnp.exp(p["inter"] - p["mt"])
        yield
    for p in probs:
        c_old = c_ref[p["idx"]]
        n_old = jnp.broadcast_to(n_ref[p["idx"]], (sub8[0], LANES))
        num = (_dot(p["v_t"].astype(BF16), p["sm"].astype(BF16))
               + p["wi"] * _dot(c_old.astype(BF16), p["q_t"]))
        den = (jnp.sum(p["sm"], axis=0, keepdims=True)
               + p["wi"] * _dot(n_old.astype(BF16), p["q_t"])[0:1])
        lower = jnp.maximum(jnp.abs(den), jnp.exp(-p["mt"]))
        h_t = num / lower
        p["hn"] = (h_t * lax.rsqrt(jnp.mean(h_t * h_t, axis=0, keepdims=True) + EPS)).T
        yield
    for p in probs:
        gate = jax.nn.sigmoid(og(p["b"], p["cols"]).astype(F32))
        hm_ref[p["b"], :, p["cols"]] = (p["hn"] * gh_ref[:, p["cols"]] * gate).astype(hm_ref.dtype)
        yield
    for p in probs:
        idx = p["idx"]
        b_last = p["bt_r"][:, chunk - 1:chunk]
        m_new = p["mt"][:, chunk - 1:chunk]
        wk = jnp.exp(b_last - p["bt_r"] + p["it_r"] - m_new)
        wc = jnp.exp(b_last + p["m_old"] - m_new)
        c_ref[idx] = wc * c_ref[idx] + _dot((p["v_t"] * wk).astype(BF16), p["k"])
        n_ref[idx] = wc * n_ref[idx] + _dot(jnp.broadcast_to(wk, sub8).astype(BF16), p["k"])[0:1]
        m_ref[idx] = jnp.broadcast_to(m_new, (1, LANES))
        yield


def _mixer_prompt_kernel(x_ref, g_ref, w_ref, wgr_ref, br_ref, gh_ref, gmem_ref, wck_ref, wcv_ref,
                         wg_ref, wu_ref, wd_ref, wo_ref, mem_ref,
                         q_ref, k_ref, v_ref, hm_ref, c_ref, n_ref, m_ref, wgu_out, wd_out, wo_out,
                         memk_out, memv_out,
                         mq_scr, mk_scr, mv_scr, og_scr, gr_scr, *, nbatch, chunk):
    s = pl.program_id(0)
    wr = s % 2
    rd = 1 - wr
    rows = nbatch * chunk

    def side_work():
        for c in range(0, wg_ref.shape[1], FFN_CHUNK):
            wgu_out[:, 2 * c:2 * c + FFN_CHUNK] = wg_ref[:, c:c + FFN_CHUNK].astype(BF16)
            wgu_out[:, 2 * c + FFN_CHUNK:2 * (c + FFN_CHUNK)] = wu_ref[:, c:c + FFN_CHUNK].astype(BF16)
        wd_out[...] = wd_ref[...].astype(BF16)
        wo_out[...] = wo_ref[...].astype(BF16)
        mn = _rmsnorm(mem_ref[...], gmem_ref[...]).astype(BF16)
        memk_out[...] = _dot(mn, wck_ref[...])
        memv_out[...] = _dot(mn, wcv_ref[...])

    def proj_stages():
        xn = _rmsnorm(x_ref[...].reshape(rows, x_ref.shape[-1]), g_ref[...]).astype(BF16)
        yield

        def mm(lo, hi):
            return _dot(xn, w_ref[:, lo:hi])

        o = 0
        q_ref[...] = mm(o, o + ATT_WIDTH).astype(q_ref.dtype).reshape(q_ref.shape)
        yield
        o += ATT_WIDTH
        kv = mm(o, o + 2 * KV_WIDTH)
        k_ref[...] = kv[:, :KV_WIDTH].reshape(k_ref.shape)
        v_ref[...] = kv[:, KV_WIDTH:].reshape(v_ref.shape)
        yield
        o += 2 * KV_WIDTH
        mqk = mm(o, o + 2 * MQK_WIDTH)
        mq_scr[wr] = mqk[:, :MQK_WIDTH].astype(BF16)
        mk_scr[wr] = (mqk[:, MQK_WIDTH:] * (MLSTM_DK ** -0.5)).astype(BF16)
        yield
        o += 2 * MQK_WIDTH
        mv_scr[wr] = mm(o, o + MLSTM_WIDTH).astype(BF16)
        yield
        o += MLSTM_WIDTH
        og_scr[wr] = mm(o, o + MLSTM_WIDTH).astype(BF16)
        yield
        zr = _dot_nt(wgr_ref[...], xn) + br_ref[...]
        row = lax.broadcasted_iota(jnp.int32, zr.shape, 0)
        gr_scr[wr] = jnp.where(row < MLSTM_HEADS, zr, _log_sigmoid(zr))
        yield

    def mlstm_stages():
        def rows_of(scr):
            return lambda b, lanes: scr[rd, b * chunk:(b + 1) * chunk, lanes]
        return _mlstm_prompt_stages(
            rows_of(mq_scr), rows_of(mk_scr), rows_of(mv_scr), rows_of(og_scr),
            lambda b: gr_scr[rd, :, b * chunk:(b + 1) * chunk],
            gh_ref, hm_ref, c_ref, n_ref, m_ref, nbatch, chunk)

    @pl.when(s == 0)
    def _():
        c_ref[...] = jnp.zeros_like(c_ref)
        n_ref[...] = jnp.zeros_like(n_ref)
        m_ref[...] = jnp.zeros_like(m_ref)
        side_work()
        for _ in proj_stages():
            pass

    @pl.when(s > 0)
    def _():
        side_work()
        mlstm = mlstm_stages()
        for _ in proj_stages():
            for _ in range(MLSTM_STAGES_PER_PROJ_STAGE):
                next(mlstm, None)
        for _ in mlstm:
            pass


def _mixer_prompt(x, mem, w, w_gate, w_up, w_down, w_out, nbatch, chunk):
    t, d = x.shape
    seq = t // nbatch
    nchunks = seq // chunk
    nstate = nbatch * MLSTM_HEADS
    rows = nbatch * chunk
    hidden = w_down.shape[0]
    bf16_rows = 16
    assert hidden % FFN_CHUNK == 0 and d % (nchunks * bf16_rows) == 0 and hidden % (nchunks * bf16_rows) == 0
    assert mem.shape[0] % (nchunks * 8) == 0

    def proj_chunk(width):
        return pl.BlockSpec((nbatch, chunk, width), lambda s: (0, jnp.minimum(s, nchunks - 1), 0))

    def state(shape):
        return pl.BlockSpec(shape, lambda s: (0, 0, 0))

    def weight_rows(a, width=None):
        return pl.BlockSpec((a.shape[0] // nchunks, width or a.shape[1]),
                            lambda s: (jnp.minimum(s, nchunks - 1), 0))

    names = ["g_mix", "w_main", "w_gate_rows", "b_gate_rows", "g_head", "g_mem", "w_ck", "w_cv"]
    q, k, v, hm, c_st, n_st, m_st, w_gate_up_bf, w_down_bf, w_out_bf, memk, memv = pl.pallas_call(
        functools.partial(_mixer_prompt_kernel, nbatch=nbatch, chunk=chunk),
        grid=(nchunks + 1,),
        in_specs=[proj_chunk(d)] + _weight_specs(w, names) + [
            weight_rows(w_gate), weight_rows(w_up), weight_rows(w_down), weight_rows(w_out),
            weight_rows(mem),
        ],
        out_specs=[
            proj_chunk(ATT_WIDTH), proj_chunk(KV_WIDTH), proj_chunk(KV_WIDTH),
            pl.BlockSpec((nbatch, chunk, MLSTM_WIDTH), lambda s: (0, jnp.maximum(s - 1, 0), 0)),
            state((nstate, MLSTM_DV, LANES)), state((nstate, 1, LANES)), state((nstate, 1, LANES)),
            weight_rows(w_gate, 2 * hidden), weight_rows(w_down), weight_rows(w_out),
            weight_rows(mem, CROSS_WIDTH), weight_rows(mem, CROSS_WIDTH),
        ],
        out_shape=[
            jax.ShapeDtypeStruct((nbatch, seq, ATT_WIDTH), BF16),
            jax.ShapeDtypeStruct((nbatch, seq, KV_WIDTH), F32),
            jax.ShapeDtypeStruct((nbatch, seq, KV_WIDTH), F32),
            jax.ShapeDtypeStruct((nbatch, seq, MLSTM_WIDTH), BF16),
            jax.ShapeDtypeStruct((nstate, MLSTM_DV, LANES), F32),
            jax.ShapeDtypeStruct((nstate, 1, LANES), F32),
            jax.ShapeDtypeStruct((nstate, 1, LANES), F32),
            jax.ShapeDtypeStruct((d, 2 * hidden), BF16),
            jax.ShapeDtypeStruct((hidden, d), BF16),
            jax.ShapeDtypeStruct(w_out.shape, BF16),
            jax.ShapeDtypeStruct((mem.shape[0], CROSS_WIDTH), F32),
            jax.ShapeDtypeStruct((mem.shape[0], CROSS_WIDTH), F32),
        ],
        scratch_shapes=[
            pltpu.VMEM((2, rows, MQK_WIDTH), BF16), pltpu.VMEM((2, rows, MQK_WIDTH), BF16),
            pltpu.VMEM((2, rows, MLSTM_WIDTH), BF16), pltpu.VMEM((2, rows, MLSTM_WIDTH), BF16),
            pltpu.VMEM((2, GATE_ROWS, rows), F32),
        ],
        compiler_params=pltpu.CompilerParams(
            dimension_semantics=("arbitrary",), vmem_limit_bytes=VMEM_LIMIT),
    )(x.reshape(nbatch, seq, d), *[w[n] for n in names], w_gate, w_up, w_down, w_out, mem)
    q = q.reshape(t, ATT_WIDTH)
    k = k.reshape(t, KV_WIDTH)
    v = v.reshape(t, KV_WIDTH)

    c_st = c_st.reshape(nbatch, MLSTM_HEADS, MLSTM_DV, 2, MLSTM_DK)
    n_st = n_st.reshape(nbatch, MLSTM_HEADS, 2, MLSTM_DK)
    c_fin = jnp.stack([c_st[:, h, :, h % 2, :] for h in range(MLSTM_HEADS)], axis=1)
    n_fin = jnp.stack([n_st[:, h, h % 2, :] for h in range(MLSTM_HEADS)], axis=1)
    m_fin = m_st[:, 0, 0].reshape(nbatch, MLSTM_HEADS)
    bf16_weights = {"w_gate_up": w_gate_up_bf, "w_down": w_down_bf, "w_out": w_out_bf}
    return q, k, v, hm.reshape(t, MLSTM_WIDTH), c_fin, n_fin, m_fin, bf16_weights, memk, memv


def _mlstm_sample_kernel(mq_ref, mk_ref, mkt_ref, mv_ref, og_ref, gc_ref, gr_ref, mrow_ref, c_ref,
                         n_ref, gh_ref, hm_ref, co_ref, no_ref, mt_ref, *, nb, tq):
    rows = nb * tq
    shift = tq.bit_length() - 1
    r = lax.broadcasted_iota(jnp.int32, (rows, rows), 0)
    s = lax.broadcasted_iota(jnp.int32, (rows, rows), 1)
    same = (r >> shift) == (s >> shift)
    valid = same & (s <= r)
    tril = jnp.where(valid, 1.0, 0.0).astype(BF16)
    triu = jnp.where(same & (r <= s), 1.0, 0.0).astype(BF16)
    row_batch = lax.broadcasted_iota(jnp.int32, (rows, 1), 0) >> shift
    lane = lax.broadcasted_iota(jnp.int32, (rows, LANES), 1)

    gc = gc_ref[...]
    gr = gr_ref[...]
    btc = _cumsum_cols(tril, gc)
    btr = _cumsum_rows(gr, triu)
    m_rows = mrow_ref[...]
    mt_all = jnp.zeros((rows, LANES), F32)

    def last_of_batch(col):
        parts = [jnp.broadcast_to(col[(b + 1) * tq - 1:(b + 1) * tq], (tq, 1)) for b in range(nb)]
        return jnp.concatenate(parts, axis=0)

    dk = MLSTM_DK
    lo = _low_half((rows, LANES))
    own_q = row_batch == (lax.broadcasted_iota(jnp.int32, (rows, nb * dk), 1) >> _log2(dk))
    lane_batch = lax.broadcasted_iota(jnp.int32, (dk, rows), 1) >> shift

    def state_rows(b, h):
        return slice((b * MLSTM_HEADS + h) * dk, (b * MLSTM_HEADS + h + 1) * dk)

    for h in range(MLSTM_HEADS):
        slab = slice((h // 2) * LANES, (h // 2 + 1) * LANES)
        qs, ks = mq_ref[:, slab], mk_ref[:, slab]
        qsw, ksw = pltpu.roll(qs, dk, axis=1), pltpu.roll(ks, dk, axis=1)
        q2 = jnp.where(lo, qs, qsw) if h % 2 == 0 else jnp.where(lo, qsw, qs)
        k2 = jnp.where(lo, ks, ksw) if h % 2 == 0 else jnp.where(lo, ksw, ks)
        qb, kb = q2[:, :dk].astype(BF16), k2[:, :dk].astype(BF16)
        vb = mv_ref[:, h * MLSTM_DV:(h + 1) * MLSTM_DV].astype(BF16)
        it_c = gc[:, h:h + 1]
        bt_c = btc[:, MLSTM_HEADS + h:MLSTM_HEADS + h + 1]
        it_r = gr[h:h + 1, :]
        bt_r = btr[MLSTM_HEADS + h:MLSTM_HEADS + h + 1, :]
        m_c = m_rows[:, h:h + 1]

        num, ssum, mt, wi = _mlstm_intra(qb, kb, vb, it_c, bt_c, it_r, bt_r, m_c, valid)
        qx = jnp.where(own_q, jnp.concatenate([q2] * (nb // 2), axis=1), 0.0).astype(BF16)
        c_stack = jnp.concatenate([c_ref[state_rows(b, h), :] for b in range(nb)], axis=0)
        qc = _dot(qx, c_stack.astype(BF16))
        n_rows = jnp.concatenate(
            [jnp.broadcast_to(n_ref[b * MLSTM_HEADS + h], (tq, dk)) for b in range(nb)], axis=0)
        num = num + wi * qc
        den = ssum + wi * jnp.sum(qb.astype(F32) * n_rows, axis=-1, keepdims=True)
        og = og_ref[:, h * MLSTM_DV:(h + 1) * MLSTM_DV]
        g_head = gh_ref[:, h * MLSTM_DV:(h + 1) * MLSTM_DV]
        hm_ref[:, h * MLSTM_DV:(h + 1) * MLSTM_DV] = _mlstm_head_out(num, den, mt, og, g_head)

        b_last = last_of_batch(bt_c)
        m_new = last_of_batch(mt)
        wk = jnp.exp(b_last - bt_c + it_c - m_new)
        wc = jnp.exp(b_last + m_c - m_new)
        wkv = (wk * vb.astype(F32)).astype(BF16)
        wkk = wk * kb.astype(F32)
        kt = mkt_ref[h * dk:(h + 1) * dk, :]
        kx = jnp.concatenate([jnp.where(lane_batch == b, kt, 0.0) for b in range(nb)], axis=0)
        upd = _dot(kx.astype(BF16), wkv)
        for b in range(nb):
            idx = b * MLSTM_HEADS + h
            wc_b = wc[b * tq:b * tq + 1]
            co_ref[state_rows(b, h), :] = wc_b * c_ref[state_rows(b, h), :] + upd[b * dk:(b + 1) * dk]
            no_ref[idx] = wc_b * n_ref[idx] + jnp.sum(wkk[b * tq:(b + 1) * tq], axis=0, keepdims=True)
        mt_all = jnp.where(lane == h, mt, mt_all)
    mt_ref[...] = mt_all


def _mlstm_sample(mq, mk, mkt, mv, og, gcol, grow, m_rows, ct_in, n_in, g_head, nbatch, nb):
    t = mq.shape[0]
    tq = t // nbatch
    rows = nb * tq
    assert tq & (tq - 1) == 0 and rows == LANES and nb % 2 == 0

    def tok(width):
        return pl.BlockSpec((rows, width), lambda i: (i, 0))

    ct_spec = pl.BlockSpec((nb * MLSTM_HEADS * MLSTM_DK, MLSTM_DV), lambda i: (i, 0))
    n_spec = pl.BlockSpec((nb * MLSTM_HEADS, 1, MLSTM_DK), lambda i: (i, 0, 0))
    return pl.pallas_call(
        functools.partial(_mlstm_sample_kernel, nb=nb, tq=tq),
        grid=(nbatch // nb,),
        in_specs=[
            tok(MQK_WIDTH), tok(MQK_WIDTH),
            pl.BlockSpec((MQK_WIDTH, rows), lambda i: (0, i)),
            tok(MLSTM_WIDTH), tok(MLSTM_WIDTH), tok(LANES),
            pl.BlockSpec((GATE_ROWS, rows), lambda i: (0, i)),
            tok(LANES),
            ct_spec, n_spec,
            _const_spec((1, MLSTM_WIDTH)),
        ],
        out_specs=[tok(MLSTM_WIDTH), ct_spec, n_spec, tok(LANES)],
        out_shape=[
            jax.ShapeDtypeStruct((t, MLSTM_WIDTH), F32),
            jax.ShapeDtypeStruct(ct_in.shape, F32),
            jax.ShapeDtypeStruct(n_in.shape, F32),
            jax.ShapeDtypeStruct((t, LANES), F32),
        ],
        compiler_params=pltpu.CompilerParams(
            dimension_semantics=("arbitrary",), vmem_limit_bytes=VMEM_LIMIT),
    )(mq, mk, mkt, mv, og, gcol, grow, m_rows, ct_in, n_in, g_head)


def _mix_stage(x, att, hm, wo_ref, gcr_ref, wcq_ref):
    cat = jnp.concatenate([att.astype(BF16), hm.astype(BF16)], axis=1)
    y1 = x + _dot(cat, wo_ref[...])
    qc = _dot(_rmsnorm(y1, gcr_ref[...]).astype(BF16), wcq_ref[...])
    return y1, qc


def _cross_attention(qc, mk, mv):
    slabs = []
    for j in range(CROSS_WIDTH // LANES):
        qs = qc[:, j * LANES:(j + 1) * LANES]
        ks = mk[:, j * LANES:(j + 1) * LANES]
        vs = mv[:, j * LANES:(j + 1) * LANES]
        n = qs.shape[0]
        lo = _low_half(qs.shape)
        zero = jnp.zeros_like(qs)
        q2 = jnp.concatenate([jnp.where(lo, qs, zero), jnp.where(lo, zero, qs)], axis=0)
        s = _dot_nt(q2 * (HEAD_DIM ** -0.5), ks)
        p = jnp.exp(s - jnp.max(s, axis=-1, keepdims=True))
        o2 = _dot(p.astype(BF16), vs) / jnp.sum(p, axis=-1, keepdims=True)
        slabs.append(jnp.where(lo, o2[:n], o2[n:]))
    return jnp.concatenate(slabs, axis=1)


FFN_CHUNK = 256


def _finish(y3, gfin_ref, final):
    return _rmsnorm(y3, gfin_ref[...]) if final else y3


def _ffn_stage(y1, o, wco_ref, gf_ref, wgu_ref, wd_ref, act_ref, between=None):
    y2 = y1 + _dot(o.astype(BF16), wco_ref[...])
    xn = _rmsnorm(y2, gf_ref[...]).astype(BF16)
    for c in range(wd_ref.shape[0] // FFN_CHUNK):
        gu = _dot(xn, wgu_ref[:, 2 * c * FFN_CHUNK:2 * (c + 1) * FFN_CHUNK])
        g, u = gu[:, :FFN_CHUNK], gu[:, FFN_CHUNK:]
        act_ref[:, c * FFN_CHUNK:(c + 1) * FFN_CHUNK] = (g * jax.nn.sigmoid(g) * u).astype(BF16)
        if between is not None:
            between()
    return y2 + _dot(act_ref[...], wd_ref[...])


def _post_prompt_kernel(sink_ref, q_ref, kc_ref, kp_ref, vc_ref, vp_ref, x_ref, hm_ref, mk_ref, mv_ref,
                        wo_ref, gcr_ref, wcq_ref, wco_ref, gf_ref, wgu_ref, wd_ref, gfin_ref,
                        o_ref, act_ref, att_ref, y3_ref, *, final, ntiles, tiles_per_seq):
    s = pl.program_id(0)
    tile = jnp.minimum(s, ntiles - 1)
    first_block = (tile % tiles_per_seq) == 0
    slot = s % 2

    def swa_stages():
        def store(j, value):
            att_ref[slot, j * WINDOW:(j + 1) * WINDOW, :] = value.astype(att_ref.dtype)
        return _swa_prompt_stages(sink_ref, q_ref, kc_ref, kp_ref, vc_ref, vp_ref, first_block, store)

    @pl.when(s == 0)
    def _():
        y3_ref[...] = jnp.zeros_like(y3_ref)
        for _ in swa_stages():
            pass

    @pl.when(jnp.logical_and(s > 0, s <= ntiles))
    def _():
        swa = swa_stages()

        def advance(n=SWA_STAGES_PER_FFN_CHUNK):
            for _ in range(n):
                next(swa, None)

        o_ref[...] = _finish(y3_ref[...], gfin_ref, final)
        y1, qc = _mix_stage(x_ref[...], att_ref[1 - slot], hm_ref[...], wo_ref, gcr_ref, wcq_ref)
        advance()
        o = _cross_attention(qc.astype(BF16), mk_ref[...].astype(BF16), mv_ref[...].astype(BF16))
        advance()
        y3_ref[...] = _ffn_stage(y1, o, wco_ref, gf_ref, wgu_ref, wd_ref, act_ref, between=advance)
        for _ in swa:
            pass

    @pl.when(s > ntiles)
    def _():
        o_ref[...] = _finish(y3_ref[...], gfin_ref, final)


_MIX_WEIGHTS = ("w_out", "g_cross", "w_cq")
_FFN_WEIGHTS = ("w_co", "g_ffn", "w_gate_up", "w_down", "g_final")


def _post_prompt(x, q, k, v, hm, memk, memv, sinks, w, nbatch, final, tm):
    t, d = x.shape
    ntiles = t // tm
    tpb = ntiles // nbatch
    wpt = tm // WINDOW
    m = memk.shape[0] // nbatch
    hidden = w["w_down"].shape[0]

    def attn_tile(s):
        return jnp.minimum(s, ntiles - 1)

    def post_tile(s):
        return jnp.clip(s - 1, 0, ntiles - 1)

    def done_tile(s):
        return jnp.maximum(s - 2, 0)

    def attn_rows(width):
        return pl.BlockSpec((tm, width), lambda s: (attn_tile(s), 0))

    def prev_window(width):
        return pl.BlockSpec((WINDOW, width), lambda s: (jnp.maximum(attn_tile(s) * wpt - 1, 0), 0))

    def post_rows(width):
        return pl.BlockSpec((tm, width), lambda s: (post_tile(s), 0))

    mem_spec = pl.BlockSpec((m, CROSS_WIDTH), lambda s: (post_tile(s) // tpb, 0))
    names = _MIX_WEIGHTS + _FFN_WEIGHTS
    return pl.pallas_call(
        functools.partial(_post_prompt_kernel, final=final, ntiles=ntiles, tiles_per_seq=tpb),
        grid=(ntiles + 2,),
        in_specs=[
            pl.BlockSpec(memory_space=pltpu.SMEM),
            attn_rows(ATT_WIDTH), attn_rows(KV_WIDTH), prev_window(KV_WIDTH),
            attn_rows(KV_WIDTH), prev_window(KV_WIDTH),
            post_rows(d), post_rows(MLSTM_WIDTH), mem_spec, mem_spec,
        ] + _weight_specs(w, names),
        out_specs=pl.BlockSpec((tm, d), lambda s: (done_tile(s), 0)),
        out_shape=jax.ShapeDtypeStruct((t, d), F32),
        scratch_shapes=[pltpu.VMEM((tm, hidden), BF16), pltpu.VMEM((2, tm, ATT_WIDTH), BF16),
                        pltpu.VMEM((tm, d), F32)],
        compiler_params=pltpu.CompilerParams(
            dimension_semantics=("arbitrary",), vmem_limit_bytes=VMEM_LIMIT),
    )(sinks, q, k, k, v, v, x, hm, memk, memv, *[w[n] for n in names])


def _mix_kernel(x_ref, att_ref, hm_ref, wo_ref, gcr_ref, wcq_ref, y_ref, qc_ref):
    y1, qc = _mix_stage(x_ref[...], att_ref[...], hm_ref[...], wo_ref, gcr_ref, wcq_ref)
    y_ref[...] = y1
    qc_ref[...] = qc


def _mix(x, att, hm, w, tm):
    t, d = x.shape

    def rows(width):
        return pl.BlockSpec((tm, width), lambda i: (i, 0))

    return pl.pallas_call(
        _mix_kernel,
        grid=(t // tm,),
        in_specs=[rows(d), rows(ATT_WIDTH), rows(MLSTM_WIDTH)] + _weight_specs(w, _MIX_WEIGHTS),
        out_specs=[rows(d), rows(CROSS_WIDTH)],
        out_shape=[jax.ShapeDtypeStruct((t, d), F32), jax.ShapeDtypeStruct((t, CROSS_WIDTH), F32)],
        compiler_params=pltpu.CompilerParams(
            dimension_semantics=("arbitrary",), vmem_limit_bytes=VMEM_LIMIT),
    )(x, att, hm, *[w[n] for n in _MIX_WEIGHTS])


def _cross_sample_kernel(qc_ref, mk_ref, mv_ref, o_ref, *, nb, tq):
    mem = CROSS_WIDTH
    n = CROSS_HEADS * tq
    row_head = lax.broadcasted_iota(jnp.int32, (n, CROSS_WIDTH), 0) >> _log2(tq)
    lane_head = lax.broadcasted_iota(jnp.int32, (n, CROSS_WIDTH), 1) >> _log2(HEAD_DIM)
    own = row_head == lane_head
    for b in range(nb):
        q = qc_ref[b * tq:(b + 1) * tq, :]
        qx = jnp.where(own, jnp.concatenate([q] * CROSS_HEADS, axis=0), 0.0).astype(BF16)
        mk = mk_ref[b * mem:(b + 1) * mem, :].astype(BF16)
        mv = mv_ref[b * mem:(b + 1) * mem, :].astype(BF16)
        s = _dot(qx, mk) * (HEAD_DIM ** -0.5)
        p = jnp.exp(s - jnp.max(s, axis=-1, keepdims=True))
        pv = _dot_nt(p.astype(BF16), mv) / jnp.sum(p, axis=-1, keepdims=True)
        pv = jnp.where(own, pv, 0.0)
        out = pv[0:tq]
        for h in range(1, CROSS_HEADS):
            out = out + pv[h * tq:(h + 1) * tq]
        o_ref[b * tq:(b + 1) * tq, :] = out


def _cross_sample(qc, memk_t, memv_t, nbatch, nb):
    t = qc.shape[0]
    tq = t // nbatch
    mem = memk_t.shape[1]
    tok = pl.BlockSpec((nb * tq, CROSS_WIDTH), lambda i: (i, 0))
    mspec = pl.BlockSpec((nb * CROSS_WIDTH, mem), lambda i: (i, 0))
    return pl.pallas_call(
        functools.partial(_cross_sample_kernel, nb=nb, tq=tq),
        grid=(nbatch // nb,),
        in_specs=[tok, mspec, mspec],
        out_specs=tok,
        out_shape=jax.ShapeDtypeStruct((t, CROSS_WIDTH), F32),
        compiler_params=pltpu.CompilerParams(
            dimension_semantics=("arbitrary",), vmem_limit_bytes=VMEM_LIMIT),
    )(qc, memk_t, memv_t)


def _ffn_kernel(y_ref, o_ref, wco_ref, gf_ref, wgu_ref, wd_ref, gfin_ref, out_ref, act_ref, *, final):
    y3 = _ffn_stage(y_ref[...], o_ref[...], wco_ref, gf_ref, wgu_ref, wd_ref, act_ref)
    out_ref[...] = _finish(y3, gfin_ref, final)


def _ffn(y1, o, w, final, tm):
    t, d = y1.shape
    hidden = w["w_down"].shape[0]

    def rows(width):
        return pl.BlockSpec((tm, width), lambda i: (i, 0))

    return pl.pallas_call(
        functools.partial(_ffn_kernel, final=final),
        grid=(t // tm,),
        in_specs=[rows(d), rows(CROSS_WIDTH)] + _weight_specs(w, _FFN_WEIGHTS),
        out_specs=rows(d),
        out_shape=jax.ShapeDtypeStruct((t, d), F32),
        scratch_shapes=[pltpu.VMEM((tm, hidden), BF16)],
        compiler_params=pltpu.CompilerParams(
            dimension_semantics=("arbitrary",), vmem_limit_bytes=VMEM_LIMIT),
    )(y1, o, *[w[n] for n in _FFN_WEIGHTS])


def _layer_weights(l, w_in, b_igate, b_fgate, g_mlstm_head, g_mix, g_cross, g_mem, w_cq, w_ck, w_cv,
                   w_co, g_ffn, g_final):
    gate_w = w_in[l][:, MAIN_WIDTH:]
    gate_b = jnp.concatenate([b_igate[l], b_fgate[l]]).astype(F32)
    ngate = gate_b.shape[0]
    row = lambda a: a.astype(F32).reshape(1, -1)
    w_t = w_in[l].T
    k0 = ATT_WIDTH
    mk0 = ATT_WIDTH + 2 * KV_WIDTH + MQK_WIDTH
    return {
        "w_kvk_t": jnp.concatenate([w_t[k0:k0 + 2 * KV_WIDTH], w_t[mk0:mk0 + MQK_WIDTH]]).astype(BF16),
        "w_main": w_in[l][:, :MAIN_WIDTH].astype(BF16),
        "w_gate_cols": jnp.pad(gate_w, ((0, 0), (0, LANES - ngate))).astype(BF16),
        "w_gate_rows": jnp.pad(gate_w.T, ((0, GATE_ROWS - ngate), (0, 0))).astype(BF16),
        "b_gate_cols": jnp.pad(gate_b, (0, LANES - ngate)).reshape(1, LANES),
        "b_gate_rows": jnp.pad(gate_b, (0, GATE_ROWS - ngate)).reshape(GATE_ROWS, 1),
        "g_mix": row(g_mix[l]), "g_cross": row(g_cross[l]), "g_mem": row(g_mem[l]),
        "g_ffn": row(g_ffn[l]), "g_final": row(g_final), "g_head": row(g_mlstm_head[l]),
        "w_cq": w_cq[l].astype(BF16), "w_ck": w_ck[l].astype(BF16),
        "w_cv": w_cv[l].astype(BF16), "w_co": w_co[l].astype(BF16),
    }


TOKEN_TILE = 512
MLSTM_STAGES_PER_PROJ_STAGE = 15
SWA_STAGES_PER_FFN_CHUNK = 3
MLSTM_CHUNK_ROWS = 256
SWA_SAMPLE_BATCH = 16
MLSTM_SAMPLE_BATCH = 16
CROSS_SAMPLE_BATCH = 16


def kernel(x_prompt, x_sample, mem_prompt, cache_swa_k, cache_swa_v, state_mlstm_C, state_mlstm_n,
           state_mlstm_m, cache_mem_k, cache_mem_v, w_in, b_igate, b_fgate, attn_sinks,
           g_mlstm_head, w_out, g_mix, g_cross, g_mem, w_cq, w_ck, w_cv, w_co, g_ffn, w_gate,
           w_up, w_down, g_final):
    depth = w_in.shape[0]
    bp, sp, d = x_prompt.shape
    bs, ss, _ = x_sample.shape
    mem_tokens = mem_prompt.shape[1]
    past = cache_swa_k.shape[2]
    yp = x_prompt.reshape(bp * sp, d)
    ys = x_sample.reshape(bs * ss, d)
    mem = mem_prompt.reshape(bp * mem_tokens, d)
    outs = [[] for _ in range(12)]

    for l in range(depth):
        final = l == depth - 1
        w = _layer_weights(l, w_in, b_igate, b_fgate, g_mlstm_head, g_mix, g_cross, g_mem, w_cq,
                           w_ck, w_cv, w_co, g_ffn, g_final)
        sinks = attn_sinks[l].astype(F32)

        q, k, v, hm, c_p, n_p, m_p, cast, memk, memv = _mixer_prompt(
            yp, mem, w, w_gate[l].astype(F32), w_up[l].astype(F32), w_down[l].astype(F32),
            w_out[l].astype(F32), bp, MLSTM_CHUNK_ROWS)
        w.update(cast)
        yp = _post_prompt(yp, q, k, v, hm, memk, memv, sinks, w, bp, final, TOKEN_TILE)
        win_shape = (bp, WINDOW, ATT_KV_HEADS, HEAD_DIM)
        outs[0].append(k.reshape(bp, sp, KV_WIDTH)[:, sp - WINDOW:].reshape(win_shape))
        outs[1].append(v.reshape(bp, sp, KV_WIDTH)[:, sp - WINDOW:].reshape(win_shape))
        outs[2].append(c_p)
        outs[3].append(n_p)
        outs[4].append(m_p)
        mem_shape = (bp, mem_tokens, CROSS_HEADS, HEAD_DIM)
        outs[5].append(memk.reshape(mem_shape))
        outs[6].append(memv.reshape(mem_shape))

        def tokens_last(a):
            return jnp.transpose(a.astype(F32), (0, 2, 3, 1)).reshape(-1, a.shape[1])

        q, mq, mk, mv, og, grow, gcol, knt, vnt, mkt = _decode_proj(ys, w, TOKEN_TILE)
        att, kbuf_t, vbuf_t = _swa_sample(
            q, knt, vnt, tokens_last(cache_swa_k[l]), tokens_last(cache_swa_v[l]), sinks, bs,
            SWA_SAMPLE_BATCH)
        m_rows = jnp.pad(jnp.repeat(state_mlstm_m[l].astype(F32), ss, axis=0),
                         ((0, 0), (0, LANES - MLSTM_HEADS)))
        ct_in = jnp.swapaxes(state_mlstm_C[l].astype(F32), 2, 3).reshape(-1, MLSTM_DV)
        hm, ct_s, n_s, mt = _mlstm_sample(
            mq, mk, mkt, mv, og, gcol, grow, m_rows, ct_in,
            state_mlstm_n[l].astype(F32).reshape(bs * MLSTM_HEADS, 1, MLSTM_DK),
            w["g_head"], bs, MLSTM_SAMPLE_BATCH)
        y1, qc = _mix(ys, att, hm, w, TOKEN_TILE)
        o = _cross_sample(qc, tokens_last(cache_mem_k[l]), tokens_last(cache_mem_v[l]), bs,
                          CROSS_SAMPLE_BATCH)
        ys = _ffn(y1, o, w, final, TOKEN_TILE)
        buf_t_shape = (bs, ATT_KV_HEADS, HEAD_DIM, past)
        outs[7].append(jnp.transpose(kbuf_t.reshape(buf_t_shape), (0, 3, 1, 2)))
        outs[8].append(jnp.transpose(vbuf_t.reshape(buf_t_shape), (0, 3, 1, 2)))
        outs[9].append(jnp.swapaxes(ct_s.reshape(bs, MLSTM_HEADS, MLSTM_DK, MLSTM_DV), 2, 3))
        outs[10].append(n_s.reshape(bs, MLSTM_HEADS, MLSTM_DK))
        outs[11].append(mt.reshape(bs, ss, LANES)[:, ss - 1, :MLSTM_HEADS])

    return (yp.reshape(bp, sp, d), ys.reshape(bs, ss, d)) + tuple(jnp.stack(o) for o in outs)
```

```python
import functools

import jax
import jax.numpy as jnp
from jax import lax
from jax.experimental import pallas as pl
from jax.experimental.pallas import tpu as pltpu

F32 = jnp.float32
BF16 = jnp.bfloat16

HEAD_DIM = 64
ATT_GROUP = 4
ATT_KV_HEADS = 2
ATT_WIDTH = 512
KV_WIDTH = ATT_KV_HEADS * HEAD_DIM
WINDOW = 128
LOG_WINDOW = 7
MLSTM_HEADS = 4
MLSTM_DV = 128
MLSTM_DK = 64
MQK_WIDTH = MLSTM_HEADS * MLSTM_DK
MLSTM_WIDTH = MLSTM_HEADS * MLSTM_DV
MAIN_WIDTH = ATT_WIDTH + 2 * KV_WIDTH + 2 * MQK_WIDTH + 2 * MLSTM_WIDTH
CROSS_HEADS = 4
CROSS_WIDTH = CROSS_HEADS * HEAD_DIM
EPS = 1e-6
NEG_INF = float("-inf")
assert HEAD_DIM == 4 ** 3

LANES = 128
GATE_ROWS = 16
VMEM_LIMIT = 56 * 1024 * 1024

NT = (((1,), (1,)), ((), ()))
TN = (((0,), (0,)), ((), ()))


def _dot(a, b):
    return jnp.dot(a, b, preferred_element_type=F32)


def _dot_nt(a, b):
    return lax.dot_general(a, b, NT, preferred_element_type=F32)


def _dot_tn(a, b):
    return lax.dot_general(a, b, TN, preferred_element_type=F32)


def _rmsnorm(x, g):
    return x * lax.rsqrt(jnp.mean(x * x, axis=-1, keepdims=True) + EPS) * g


def _log_sigmoid(x):
    return jnp.minimum(x, 0.0) - jnp.log1p(jnp.exp(-jnp.abs(x)))


def _split3(x):
    hi = x.astype(BF16)
    r1 = x - hi.astype(F32)
    mid = r1.astype(BF16)
    lo = (r1 - mid.astype(F32)).astype(BF16)
    return hi, mid, lo


def _cumsum_cols(tri, x):
    hi, mid, lo = _split3(x)
    return _dot(tri, hi) + _dot(tri, mid) + _dot(tri, lo)


def _cumsum_rows(x, tri):
    hi, mid, lo = _split3(x)
    return _dot(hi, tri) + _dot(mid, tri) + _dot(lo, tri)


def _log2(n):
    assert n > 0 and n & (n - 1) == 0, n
    return n.bit_length() - 1


def _low_half(shape):
    return lax.broadcasted_iota(jnp.int32, shape, 1) < HEAD_DIM


def _decode_proj_kernel(x_ref, g_ref, w_ref, wgr_ref, br_ref, wgc_ref, bc_ref, wt_ref,
                        q_ref, mq_ref, mk_ref, mv_ref, og_ref, gr_ref, gc_ref, kt_ref, vt_ref, mkt_ref):
    xn = _rmsnorm(x_ref[...], g_ref[...]).astype(BF16)
    zt = _dot_nt(wt_ref[...], xn)
    kt_ref[...] = zt[:KV_WIDTH]
    vt_ref[...] = zt[KV_WIDTH:2 * KV_WIDTH]
    mkt_ref[...] = zt[2 * KV_WIDTH:] * (MLSTM_DK ** -0.5)

    def mm(lo, hi):
        return _dot_nt(xn, w_ref[lo:hi, :])

    o = 0
    q_ref[...] = mm(o, o + ATT_WIDTH)
    o += ATT_WIDTH + 2 * KV_WIDTH
    mqk = mm(o, o + 2 * MQK_WIDTH)
    mq_ref[...] = mqk[:, :MQK_WIDTH]
    mk_ref[...] = mqk[:, MQK_WIDTH:] * (MLSTM_DK ** -0.5)
    o += 2 * MQK_WIDTH
    mv_ref[...] = mm(o, o + MLSTM_WIDTH)
    o += MLSTM_WIDTH
    og_ref[...] = mm(o, o + MLSTM_WIDTH)

    zc = _dot(xn, wgc_ref[...]) + bc_ref[...]
    lane = lax.broadcasted_iota(jnp.int32, zc.shape, 1)
    gc_ref[...] = jnp.where(lane < MLSTM_HEADS, zc, _log_sigmoid(zc))
    zr = _dot_nt(wgr_ref[...], xn) + br_ref[...]
    row = lax.broadcasted_iota(jnp.int32, zr.shape, 0)
    gr_ref[...] = jnp.where(row < MLSTM_HEADS, zr, _log_sigmoid(zr))


def _const_spec(shape):
    nd = len(shape)
    return pl.BlockSpec(shape, lambda *_: (0,) * nd, pipeline_mode=pl.Buffered(1))


def _weight_specs(w, names):
    return [_const_spec(w[n].shape) for n in names]


def _decode_proj(x, w, tm):
    t, d = x.shape

    def rows(width):
        return pl.BlockSpec((tm, width), lambda i: (i, 0))

    def lanes(height):
        return pl.BlockSpec((height, tm), lambda i: (0, i))

    outs = [(rows, ATT_WIDTH), (rows, MQK_WIDTH), (rows, MQK_WIDTH), (rows, MLSTM_WIDTH),
            (rows, MLSTM_WIDTH), (lanes, GATE_ROWS), (rows, LANES), (lanes, KV_WIDTH),
            (lanes, KV_WIDTH), (lanes, MQK_WIDTH)]
    names = ["g_mix", "w_main", "w_gate_rows", "b_gate_rows", "w_gate_cols", "b_gate_cols", "w_kvk_t"]
    return pl.pallas_call(
        _decode_proj_kernel,
        grid=(t // tm,),
        in_specs=[rows(d)] + _weight_specs(w, names),
        out_specs=[kind(size) for kind, size in outs],
        out_shape=[jax.ShapeDtypeStruct((t, size) if kind is rows else (size, t), F32)
                   for kind, size in outs],
        compiler_params=pltpu.CompilerParams(
            dimension_semantics=("arbitrary",), vmem_limit_bytes=VMEM_LIMIT),
    )(x, *[w[n] for n in names])


def _dup_heads(x):
    swapped = pltpu.roll(x, HEAD_DIM, axis=1)
    lo = _low_half(x.shape)
    return (jnp.where(lo, x, swapped).astype(BF16), jnp.where(lo, swapped, x).astype(BF16))


def _stack_query_heads(q, h):
    parts = []
    for g in range(ATT_GROUP):
        hq = h * ATT_GROUP + g
        slab = q[:, (hq // 2) * LANES:(hq // 2 + 1) * LANES]
        lo = _low_half(slab.shape)
        keep = lo if hq % 2 == 0 else jnp.logical_not(lo)
        parts.append(jnp.where(keep, slab, jnp.zeros_like(slab)))
    return jnp.concatenate(parts, axis=0).astype(BF16)


def _sink_column(sink_ref, h, rows_per_head):
    n = ATT_GROUP * rows_per_head
    grp = lax.broadcasted_iota(jnp.int32, (n, 1), 0) >> _log2(rows_per_head)
    col = jnp.full((n, 1), sink_ref[h * ATT_GROUP], F32)
    for g in range(1, ATT_GROUP):
        col = jnp.where(grp == g, sink_ref[h * ATT_GROUP + g], col)
    return col


def _swa_prompt_stages(sink_ref, q_ref, kc_ref, kp_ref, vc_ref, vp_ref, first_block, store):
    qb = q_ref.shape[0]
    k_all = jnp.concatenate([kp_ref[...], kc_ref[...]], axis=0)
    v_all = jnp.concatenate([vp_ref[...], vc_ref[...]], axis=0)
    kd = _dup_heads(k_all)
    v_t = v_all.T.astype(BF16)

    n = ATT_GROUP * WINDOW
    key = lax.broadcasted_iota(jnp.int32, (2 * WINDOW, n), 0)
    qry = lax.broadcasted_iota(jnp.int32, (2 * WINDOW, n), 1) & (WINDOW - 1)
    delta = key - qry
    in_band = (delta >= 0) & (delta <= WINDOW)
    grp = lax.broadcasted_iota(jnp.int32, (1, n), 1) >> LOG_WINDOW
    sinks = []
    for h in range(ATT_KV_HEADS):
        row = jnp.full((1, n), sink_ref[h * ATT_GROUP], F32)
        for g in range(1, ATT_GROUP):
            row = jnp.where(grp == g, sink_ref[h * ATT_GROUP + g], row)
        sinks.append(row)
    yield

    probs = []
    for j in range(qb // WINDOW):
        valid = in_band & (key >= jnp.where(first_block, WINDOW, 0)) if j == 0 else in_band
        q = q_ref[j * WINDOW:(j + 1) * WINDOW, :] * (HEAD_DIM ** -0.5)
        band = slice(j * WINDOW, (j + 2) * WINDOW)
        for h in range(ATT_KV_HEADS):
            qh = _stack_query_heads(q, h)
            s = _dot_nt(kd[h][band], qh)
            probs.append(dict(h=h, band=band, s=jnp.where(valid, s, NEG_INF)))
            yield
    for p in probs:
        p["m"] = jnp.maximum(jnp.max(p["s"], axis=0, keepdims=True), sinks[p["h"]])
        yield
    for p in probs:
        p["p"] = jnp.exp(p["s"] - p["m"])
        yield
    for p in probs:
        h = p["h"]
        denom = jnp.sum(p["p"], axis=0, keepdims=True) + jnp.exp(sinks[h] - p["m"])
        p["o_t"] = _dot(v_t[h * HEAD_DIM:(h + 1) * HEAD_DIM, p["band"]], p["p"].astype(BF16)) / denom
        yield
    for j in range(qb // WINDOW):
        pieces = [p["o_t"][:, g * WINDOW:(g + 1) * WINDOW]
                  for p in probs[j * ATT_KV_HEADS:(j + 1) * ATT_KV_HEADS] for g in range(ATT_GROUP)]
        store(j, jnp.concatenate(pieces, axis=0).T)
        yield


def _swa_sample_kernel(sink_ref, q_ref, knt_ref, vnt_ref, ck_ref, cv_ref,
                       o_ref, kbuf_ref, vbuf_ref, *, nb, tq):
    past = ck_ref.shape[1]
    hw = ATT_KV_HEADS * HEAD_DIM
    heads = ATT_KV_HEADS * ATT_GROUP
    n = heads * tq
    fresh = past - tq
    t = lax.broadcasted_iota(jnp.int32, (n, 2 * past), 0) & (tq - 1)
    col = lax.broadcasted_iota(jnp.int32, (n, 2 * past), 1)
    valid = ((col < past) & (col >= t)) | ((col >= past + fresh) & (col - (past + fresh) <= t))
    sink = jnp.concatenate([_sink_column(sink_ref, h, tq) for h in range(ATT_KV_HEADS)], axis=0)
    lane = lax.broadcasted_iota(jnp.int32, (hw, past), 1)
    knt = knt_ref[...]
    vnt = vnt_ref[...]
    zero_slab = jnp.zeros((tq, LANES), F32)
    lo = _low_half((tq, LANES))

    def shifted(old, new_t, b):
        return jnp.where(lane >= fresh, pltpu.roll(new_t, (fresh - b * tq) % past, axis=1),
                         pltpu.roll(old, fresh, axis=1))

    def dup_rows(x):
        h0, h1 = x[:HEAD_DIM], x[HEAD_DIM:]
        return jnp.concatenate([h0, h0, h1, h1], axis=0)

    for b in range(nb):
        k_old = ck_ref[b * hw:(b + 1) * hw, :]
        v_old = cv_ref[b * hw:(b + 1) * hw, :]
        k_new = shifted(k_old, knt, b)
        v_new = shifted(v_old, vnt, b)
        kbuf_ref[b * hw:(b + 1) * hw, :] = k_new
        vbuf_ref[b * hw:(b + 1) * hw, :] = v_new
        kop = jnp.concatenate([dup_rows(k_old), dup_rows(k_new)], axis=1).astype(BF16)
        vop = jnp.concatenate([dup_rows(v_old), dup_rows(v_new)], axis=1).astype(BF16)

        q = q_ref[b * tq:(b + 1) * tq, :]
        blocks = []
        for hq in range(heads):
            slab = q[:, (hq // 2) * LANES:(hq // 2 + 1) * LANES]
            slab = jnp.where(lo if hq % 2 == 0 else jnp.logical_not(lo), slab, 0.0)
            pair = [slab, zero_slab] if hq // ATT_GROUP == 0 else [zero_slab, slab]
            blocks.append(jnp.concatenate(pair, axis=1))
        qx = jnp.concatenate(blocks, axis=0).astype(BF16)

        s = _dot(qx, kop) * (HEAD_DIM ** -0.5)
        s = jnp.where(valid, s, NEG_INF)
        m = jnp.maximum(jnp.max(s, axis=-1, keepdims=True), sink)
        p = jnp.exp(s - m)
        denom = jnp.sum(p, axis=-1, keepdims=True) + jnp.exp(sink - m)
        o = _dot_nt(p.astype(BF16), vop) / denom

        slabs = []
        for j in range(heads // 2):
            h = (2 * j) // ATT_GROUP
            even = o[(2 * j) * tq:(2 * j + 1) * tq, h * LANES:(h + 1) * LANES]
            odd = o[(2 * j + 1) * tq:(2 * j + 2) * tq, h * LANES:(h + 1) * LANES]
            slabs.append(jnp.where(lo, even, odd))
        o_ref[b * tq:(b + 1) * tq, :] = jnp.concatenate(slabs, axis=1)


def _swa_sample(q, knt, vnt, cache_kt, cache_vt, sinks, nbatch, nb):
    t = q.shape[0]
    tq = t // nbatch
    past = cache_kt.shape[1]
    hw = ATT_KV_HEADS * HEAD_DIM
    assert past == WINDOW and tq & (tq - 1) == 0 and nb * tq == past

    def rows(r, width):
        return pl.BlockSpec((r, width), lambda i: (i, 0))

    new_t = pl.BlockSpec((hw, nb * tq), lambda i: (0, i))
    return pl.pallas_call(
        functools.partial(_swa_sample_kernel, nb=nb, tq=tq),
        grid=(nbatch // nb,),
        in_specs=[
            pl.BlockSpec(memory_space=pltpu.SMEM),
            rows(nb * tq, ATT_WIDTH), new_t, new_t, rows(nb * hw, past), rows(nb * hw, past),
        ],
        out_specs=[rows(nb * tq, ATT_WIDTH), rows(nb * hw, past), rows(nb * hw, past)],
        out_shape=[
            jax.ShapeDtypeStruct((t, ATT_WIDTH), F32),
            jax.ShapeDtypeStruct(cache_kt.shape, F32),
            jax.ShapeDtypeStruct(cache_vt.shape, F32),
        ],
        compiler_params=pltpu.CompilerParams(
            dimension_semantics=("arbitrary",), vmem_limit_bytes=VMEM_LIMIT),
    )(sinks, q, knt, vnt, cache_kt, cache_vt)


def _mlstm_intra(q, k, v, it_c, bt_c, it_r, bt_r, m_c, valid):
    d = jnp.where(valid, bt_c - bt_r + it_r, NEG_INF)
    inter = bt_c + m_c
    mt = jnp.maximum(jnp.max(d, axis=-1, keepdims=True), inter)
    sm = _dot_nt(q, k) * jnp.exp(d - mt)
    wi = jnp.exp(inter - mt)
    return _dot(sm.astype(BF16), v), jnp.sum(sm, axis=-1, keepdims=True), mt, wi


def _mlstm_head_out(num, den, mt, og, g_head):
    lower = jnp.maximum(jnp.abs(den), jnp.exp(-mt))
    h = num / lower
    hn = h * lax.rsqrt(jnp.mean(h * h, axis=-1, keepdims=True) + EPS)
    return hn * g_head * jax.nn.sigmoid(og)


def _mlstm_prompt_stages(mq, mk, mv, og, gr, gh_ref, hm_ref, c_ref, n_ref, m_ref, nbatch, chunk):
    s_idx = lax.broadcasted_iota(jnp.int32, (chunk, chunk), 0)
    l_idx = lax.broadcasted_iota(jnp.int32, (chunk, chunk), 1)
    causal = s_idx <= l_idx
    triu = jnp.where(causal, 1.0, 0.0).astype(BF16)
    lo = _low_half((chunk, LANES))
    pad_rows = jnp.zeros((LANES - MLSTM_HEADS, chunk), F32)
    sub8 = (GATE_ROWS, chunk)

    probs = []
    for b in range(nbatch):
        g = gr(b)
        btr = _cumsum_rows(g, triu)
        it_rows = g[:MLSTM_HEADS]
        bt_rows = btr[MLSTM_HEADS:2 * MLSTM_HEADS]
        u_cols = jnp.concatenate([it_rows - bt_rows, pad_rows], axis=0).T
        for h in range(MLSTM_HEADS):
            slab = slice((h // 2) * LANES, (h // 2 + 1) * LANES)
            keep = lo if h % 2 == 0 else jnp.logical_not(lo)
            q = mq(b, slab)
            k = mk(b, slab)
            probs.append(dict(
                b=b, h=h, idx=b * MLSTM_HEADS + h, cols=slice(h * MLSTM_DV, (h + 1) * MLSTM_DV),
                q=jnp.where(keep, q, jnp.zeros_like(q)), k=jnp.where(keep, k, jnp.zeros_like(k)),
                u_c=u_cols[:, h:h + 1], it_r=it_rows[h:h + 1, :], bt_r=bt_rows[h:h + 1, :]))
        yield

    for p in probs:
        p["m_old"] = m_ref[p["idx"]][:, 0:1]
        p["d"] = jnp.where(causal, p["bt_r"] + p["u_c"], NEG_INF)
        p["inter"] = p["bt_r"] + p["m_old"]
        p["q_t"] = p["q"].astype(F32).T.astype(BF16)
        p["kq"] = _dot(p["k"], p["q_t"])
        p["v_t"] = mv(p["b"], p["cols"]).astype(F32).T
        yield
    for p in probs:
        p["mt"] = jnp.maximum(jnp.max(p["d"], axis=0, keepdims=True), p["inter"])
        yield
    for p in probs:
        p["sm"] = p["kq"] * jnp.exp(p["d"] - p["mt"])
        p["wi"] = jnp.exp(p["inter"] - p["mt"])
        yield
    for p in probs:
        c_old = c_ref[p["idx"]]
        n_old = jnp.broadcast_to(n_ref[p["idx"]], (sub8[0], LANES))
        num = (_dot(p["v_t"].astype(BF16), p["sm"].astype(BF16))
               + p["wi"] * _dot(c_old.astype(BF16), p["q_t"]))
        den = (jnp.sum(p["sm"], axis=0, keepdims=True)
               + p["wi"] * _dot(n_old.astype(BF16), p["q_t"])[0:1])
        lower = jnp.maximum(jnp.abs(den), jnp.exp(-p["mt"]))
        h_t = num / lower
        p["hn"] = (h_t * lax.rsqrt(jnp.mean(h_t * h_t, axis=0, keepdims=True) + EPS)).T
        yield
    for p in probs:
        gate = jax.nn.sigmoid(og(p["b"], p["cols"]).astype(F32))
        hm_ref[p["b"], :, p["cols"]] = (p["hn"] * gh_ref[:, p["cols"]] * gate).astype(hm_ref.dtype)
        yield
    for p in probs:
        idx = p["idx"]
        b_last = p["bt_r"][:, chunk - 1:chunk]
        m_new = p["mt"][:, chunk - 1:chunk]
        wk = jnp.exp(b_last - p["bt_r"] + p["it_r"] - m_new)
        wc = jnp.exp(b_last + p["m_old"] - m_new)
        c_ref[idx] = wc * c_ref[idx] + _dot((p["v_t"] * wk).astype(BF16), p["k"])
        n_ref[idx] = wc * n_ref[idx] + _dot(jnp.broadcast_to(wk, sub8).astype(BF16), p["k"])[0:1]
        m_ref[idx] = jnp.broadcast_to(m_new, (1, LANES))
        yield


def _mixer_prompt_kernel(x_ref, g_ref, w_ref, wgr_ref, br_ref, gh_ref, gmem_ref, wck_ref, wcv_ref,
                         wg_ref, wu_ref, wd_ref, wo_ref, mem_ref,
                         q_ref, k_ref, v_ref, hm_ref, c_ref, n_ref, m_ref, wgu_out, wd_out, wo_out,
                         memk_out, memv_out,
                         mq_scr, mk_scr, mv_scr, og_scr, gr_scr, *, nbatch, chunk):
    s = pl.program_id(0)
    wr = s % 2
    rd = 1 - wr
    rows = nbatch * chunk

    def side_work():
        for c in range(0, wg_ref.shape[1], FFN_CHUNK):
            wgu_out[:, 2 * c:2 * c + FFN_CHUNK] = wg_ref[:, c:c + FFN_CHUNK].astype(BF16)
            wgu_out[:, 2 * c + FFN_CHUNK:2 * (c + FFN_CHUNK)] = wu_ref[:, c:c + FFN_CHUNK].astype(BF16)
        wd_out[...] = wd_ref[...].astype(BF16)
        wo_out[...] = wo_ref[...].astype(BF16)
        mn = _rmsnorm(mem_ref[...], gmem_ref[...]).astype(BF16)
        memk_out[...] = _dot(mn, wck_ref[...])
        memv_out[...] = _dot(mn, wcv_ref[...])

    def proj_stages():
        xn = _rmsnorm(x_ref[...].reshape(rows, x_ref.shape[-1]), g_ref[...]).astype(BF16)
        yield

        def mm(lo, hi):
            return _dot_nt(xn, w_ref[lo:hi, :])

        o = 0
        q_ref[...] = mm(o, o + ATT_WIDTH).astype(q_ref.dtype).reshape(q_ref.shape)
        yield
        o += ATT_WIDTH
        kv = mm(o, o + 2 * KV_WIDTH)
        k_ref[...] = kv[:, :KV_WIDTH].reshape(k_ref.shape)
        v_ref[...] = kv[:, KV_WIDTH:].reshape(v_ref.shape)
        yield
        o += 2 * KV_WIDTH
        mqk = mm(o, o + 2 * MQK_WIDTH)
        mq_scr[wr] = mqk[:, :MQK_WIDTH].astype(BF16)
        mk_scr[wr] = (mqk[:, MQK_WIDTH:] * (MLSTM_DK ** -0.5)).astype(BF16)
        yield
        o += 2 * MQK_WIDTH
        mv_scr[wr] = mm(o, o + MLSTM_WIDTH).astype(BF16)
        yield
        o += MLSTM_WIDTH
        og_scr[wr] = mm(o, o + MLSTM_WIDTH).astype(BF16)
        yield
        zr = _dot_nt(wgr_ref[...], xn) + br_ref[...]
        row = lax.broadcasted_iota(jnp.int32, zr.shape, 0)
        gr_scr[wr] = jnp.where(row < MLSTM_HEADS, zr, _log_sigmoid(zr))
        yield

    def mlstm_stages():
        def rows_of(scr):
            return lambda b, lanes: scr[rd, b * chunk:(b + 1) * chunk, lanes]
        return _mlstm_prompt_stages(
            rows_of(mq_scr), rows_of(mk_scr), rows_of(mv_scr), rows_of(og_scr),
            lambda b: gr_scr[rd, :, b * chunk:(b + 1) * chunk],
            gh_ref, hm_ref, c_ref, n_ref, m_ref, nbatch, chunk)

    @pl.when(s == 0)
    def _():
        c_ref[...] = jnp.zeros_like(c_ref)
        n_ref[...] = jnp.zeros_like(n_ref)
        m_ref[...] = jnp.zeros_like(m_ref)
        side_work()
        for _ in proj_stages():
            pass

    @pl.when(s > 0)
    def _():
        side_work()
        mlstm = mlstm_stages()
        for _ in proj_stages():
            for _ in range(MLSTM_STAGES_PER_PROJ_STAGE):
                next(mlstm, None)
        for _ in mlstm:
            pass


def _mixer_prompt(x, mem, w, w_gate, w_up, w_down, w_out, nbatch, chunk):
    t, d = x.shape
    seq = t // nbatch
    nchunks = seq // chunk
    nstate = nbatch * MLSTM_HEADS
    rows = nbatch * chunk
    hidden = w_down.shape[0]
    bf16_rows = 16
    assert hidden % FFN_CHUNK == 0 and d % (nchunks * bf16_rows) == 0 and hidden % (nchunks * bf16_rows) == 0
    assert mem.shape[0] % (nchunks * 8) == 0

    def proj_chunk(width):
        return pl.BlockSpec((nbatch, chunk, width), lambda s: (0, jnp.minimum(s, nchunks - 1), 0))

    def state(shape):
        return pl.BlockSpec(shape, lambda s: (0, 0, 0))

    def weight_rows(a, width=None):
        return pl.BlockSpec((a.shape[0] // nchunks, width or a.shape[1]),
                            lambda s: (jnp.minimum(s, nchunks - 1), 0))

    names = ["g_mix", "w_main", "w_gate_rows", "b_gate_rows", "g_head", "g_mem", "w_ck", "w_cv"]
    q, k, v, hm, c_st, n_st, m_st, w_gate_up_bf, w_down_bf, w_out_bf, memk, memv = pl.pallas_call(
        functools.partial(_mixer_prompt_kernel, nbatch=nbatch, chunk=chunk),
        grid=(nchunks + 1,),
        in_specs=[proj_chunk(d)] + _weight_specs(w, names) + [
            weight_rows(w_gate), weight_rows(w_up), weight_rows(w_down), weight_rows(w_out),
            weight_rows(mem),
        ],
        out_specs=[
            proj_chunk(ATT_WIDTH), proj_chunk(KV_WIDTH), proj_chunk(KV_WIDTH),
            pl.BlockSpec((nbatch, chunk, MLSTM_WIDTH), lambda s: (0, jnp.maximum(s - 1, 0), 0)),
            state((nstate, MLSTM_DV, LANES)), state((nstate, 1, LANES)), state((nstate, 1, LANES)),
            weight_rows(w_gate, 2 * hidden), weight_rows(w_down), weight_rows(w_out),
            weight_rows(mem, CROSS_WIDTH), weight_rows(mem, CROSS_WIDTH),
        ],
        out_shape=[
            jax.ShapeDtypeStruct((nbatch, seq, ATT_WIDTH), BF16),
            jax.ShapeDtypeStruct((nbatch, seq, KV_WIDTH), F32),
            jax.ShapeDtypeStruct((nbatch, seq, KV_WIDTH), F32),
            jax.ShapeDtypeStruct((nbatch, seq, MLSTM_WIDTH), BF16),
            jax.ShapeDtypeStruct((nstate, MLSTM_DV, LANES), F32),
            jax.ShapeDtypeStruct((nstate, 1, LANES), F32),
            jax.ShapeDtypeStruct((nstate, 1, LANES), F32),
            jax.ShapeDtypeStruct((d, 2 * hidden), BF16),
            jax.ShapeDtypeStruct((hidden, d), BF16),
            jax.ShapeDtypeStruct(w_out.shape, BF16),
            jax.ShapeDtypeStruct((mem.shape[0], CROSS_WIDTH), F32),
            jax.ShapeDtypeStruct((mem.shape[0], CROSS_WIDTH), F32),
        ],
        scratch_shapes=[
            pltpu.VMEM((2, rows, MQK_WIDTH), BF16), pltpu.VMEM((2, rows, MQK_WIDTH), BF16),
            pltpu.VMEM((2, rows, MLSTM_WIDTH), BF16), pltpu.VMEM((2, rows, MLSTM_WIDTH), BF16),
            pltpu.VMEM((2, GATE_ROWS, rows), F32),
        ],
        compiler_params=pltpu.CompilerParams(
            dimension_semantics=("arbitrary",), vmem_limit_bytes=VMEM_LIMIT),
    )(x.reshape(nbatch, seq, d), *[w[n] for n in names], w_gate, w_up, w_down, w_out, mem)
    q = q.reshape(t, ATT_WIDTH)
    k = k.reshape(t, KV_WIDTH)
    v = v.reshape(t, KV_WIDTH)

    c_st = c_st.reshape(nbatch, MLSTM_HEADS, MLSTM_DV, 2, MLSTM_DK)
    n_st = n_st.reshape(nbatch, MLSTM_HEADS, 2, MLSTM_DK)
    c_fin = jnp.stack([c_st[:, h, :, h % 2, :] for h in range(MLSTM_HEADS)], axis=1)
    n_fin = jnp.stack([n_st[:, h, h % 2, :] for h in range(MLSTM_HEADS)], axis=1)
    m_fin = m_st[:, 0, 0].reshape(nbatch, MLSTM_HEADS)
    bf16_weights = {"w_gate_up": w_gate_up_bf, "w_down": w_down_bf, "w_out": w_out_bf}
    return q, k, v, hm.reshape(t, MLSTM_WIDTH), c_fin, n_fin, m_fin, bf16_weights, memk, memv


def _mlstm_sample_kernel(mq_ref, mk_ref, mkt_ref, mv_ref, og_ref, gc_ref, gr_ref, mrow_ref, c_ref,
                         n_ref, gh_ref, hm_ref, co_ref, no_ref, mt_ref, *, nb, tq):
    rows = nb * tq
    shift = tq.bit_length() - 1
    r = lax.broadcasted_iota(jnp.int32, (rows, rows), 0)
    s = lax.broadcasted_iota(jnp.int32, (rows, rows), 1)
    same = (r >> shift) == (s >> shift)
    valid = same & (s <= r)
    tril = jnp.where(valid, 1.0, 0.0).astype(BF16)
    triu = jnp.where(same & (r <= s), 1.0, 0.0).astype(BF16)
    row_batch = lax.broadcasted_iota(jnp.int32, (rows, 1), 0) >> shift
    lane = lax.broadcasted_iota(jnp.int32, (rows, LANES), 1)

    gc = gc_ref[...]
    gr = gr_ref[...]
    btc = _cumsum_cols(tril, gc)
    btr = _cumsum_rows(gr, triu)
    m_rows = mrow_ref[...]
    mt_all = jnp.zeros((rows, LANES), F32)

    def last_of_batch(col):
        parts = [jnp.broadcast_to(col[(b + 1) * tq - 1:(b + 1) * tq], (tq, 1)) for b in range(nb)]
        return jnp.concatenate(parts, axis=0)

    dk = MLSTM_DK
    lo = _low_half((rows, LANES))
    own_q = row_batch == (lax.broadcasted_iota(jnp.int32, (rows, nb * dk), 1) >> _log2(dk))
    lane_batch = lax.broadcasted_iota(jnp.int32, (dk, rows), 1) >> shift

    def state_rows(b, h):
        return slice((b * MLSTM_HEADS + h) * dk, (b * MLSTM_HEADS + h + 1) * dk)

    for h in range(MLSTM_HEADS):
        slab = slice((h // 2) * LANES, (h // 2 + 1) * LANES)
        qs, ks = mq_ref[:, slab], mk_ref[:, slab]
        qsw, ksw = pltpu.roll(qs, dk, axis=1), pltpu.roll(ks, dk, axis=1)
        q2 = jnp.where(lo, qs, qsw) if h % 2 == 0 else jnp.where(lo, qsw, qs)
        k2 = jnp.where(lo, ks, ksw) if h % 2 == 0 else jnp.where(lo, ksw, ks)
        qb, kb = q2[:, :dk].astype(BF16), k2[:, :dk].astype(BF16)
        vb = mv_ref[:, h * MLSTM_DV:(h + 1) * MLSTM_DV].astype(BF16)
        it_c = gc[:, h:h + 1]
        bt_c = btc[:, MLSTM_HEADS + h:MLSTM_HEADS + h + 1]
        it_r = gr[h:h + 1, :]
        bt_r = btr[MLSTM_HEADS + h:MLSTM_HEADS + h + 1, :]
        m_c = m_rows[:, h:h + 1]

        num, ssum, mt, wi = _mlstm_intra(qb, kb, vb, it_c, bt_c, it_r, bt_r, m_c, valid)
        qx = jnp.where(own_q, jnp.concatenate([q2] * (nb // 2), axis=1), 0.0).astype(BF16)
        c_stack = jnp.concatenate([c_ref[state_rows(b, h), :] for b in range(nb)], axis=0)
        qc = _dot(qx, c_stack.astype(BF16))
        n_rows = jnp.concatenate(
            [jnp.broadcast_to(n_ref[b * MLSTM_HEADS + h], (tq, dk)) for b in range(nb)], axis=0)
        num = num + wi * qc
        den = ssum + wi * jnp.sum(qb.astype(F32) * n_rows, axis=-1, keepdims=True)
        og = og_ref[:, h * MLSTM_DV:(h + 1) * MLSTM_DV]
        g_head = gh_ref[:, h * MLSTM_DV:(h + 1) * MLSTM_DV]
        hm_ref[:, h * MLSTM_DV:(h + 1) * MLSTM_DV] = _mlstm_head_out(num, den, mt, og, g_head)

        b_last = last_of_batch(bt_c)
        m_new = last_of_batch(mt)
        wk = jnp.exp(b_last - bt_c + it_c - m_new)
        wc = jnp.exp(b_last + m_c - m_new)
        wkv = (wk * vb.astype(F32)).astype(BF16)
        wkk = wk * kb.astype(F32)
        kt = mkt_ref[h * dk:(h + 1) * dk, :]
        kx = jnp.concatenate([jnp.where(lane_batch == b, kt, 0.0) for b in range(nb)], axis=0)
        upd = _dot(kx.astype(BF16), wkv)
        for b in range(nb):
            idx = b * MLSTM_HEADS + h
            wc_b = wc[b * tq:b * tq + 1]
            co_ref[state_rows(b, h), :] = wc_b * c_ref[state_rows(b, h), :] + upd[b * dk:(b + 1) * dk]
            no_ref[idx] = wc_b * n_ref[idx] + jnp.sum(wkk[b * tq:(b + 1) * tq], axis=0, keepdims=True)
        mt_all = jnp.where(lane == h, mt, mt_all)
    mt_ref[...] = mt_all


def _mlstm_sample(mq, mk, mkt, mv, og, gcol, grow, m_rows, ct_in, n_in, g_head, nbatch, nb):
    t = mq.shape[0]
    tq = t // nbatch
    rows = nb * tq
    assert tq & (tq - 1) == 0 and rows == LANES and nb % 2 == 0

    def tok(width):
        return pl.BlockSpec((rows, width), lambda i: (i, 0))

    ct_spec = pl.BlockSpec((nb * MLSTM_HEADS * MLSTM_DK, MLSTM_DV), lambda i: (i, 0))
    n_spec = pl.BlockSpec((nb * MLSTM_HEADS, 1, MLSTM_DK), lambda i: (i, 0, 0))
    return pl.pallas_call(
        functools.partial(_mlstm_sample_kernel, nb=nb, tq=tq),
        grid=(nbatch // nb,),
        in_specs=[
            tok(MQK_WIDTH), tok(MQK_WIDTH),
            pl.BlockSpec((MQK_WIDTH, rows), lambda i: (0, i)),
            tok(MLSTM_WIDTH), tok(MLSTM_WIDTH), tok(LANES),
            pl.BlockSpec((GATE_ROWS, rows), lambda i: (0, i)),
            tok(LANES),
            ct_spec, n_spec,
            _const_spec((1, MLSTM_WIDTH)),
        ],
        out_specs=[tok(MLSTM_WIDTH), ct_spec, n_spec, tok(LANES)],
        out_shape=[
            jax.ShapeDtypeStruct((t, MLSTM_WIDTH), F32),
            jax.ShapeDtypeStruct(ct_in.shape, F32),
            jax.ShapeDtypeStruct(n_in.shape, F32),
            jax.ShapeDtypeStruct((t, LANES), F32),
        ],
        compiler_params=pltpu.CompilerParams(
            dimension_semantics=("arbitrary",), vmem_limit_bytes=VMEM_LIMIT),
    )(mq, mk, mkt, mv, og, gcol, grow, m_rows, ct_in, n_in, g_head)


def _mix_stage(x, att, hm, wo_ref, gcr_ref, wcq_ref):
    cat = jnp.concatenate([att.astype(BF16), hm.astype(BF16)], axis=1)
    y1 = x + _dot(cat, wo_ref[...])
    qc = _dot(_rmsnorm(y1, gcr_ref[...]).astype(BF16), wcq_ref[...])
    return y1, qc


def _cross_attention(qc, mk, mv):
    slabs = []
    for j in range(CROSS_WIDTH // LANES):
        qs = qc[:, j * LANES:(j + 1) * LANES]
        ks = mk[:, j * LANES:(j + 1) * LANES]
        vs = mv[:, j * LANES:(j + 1) * LANES]
        n = qs.shape[0]
        lo = _low_half(qs.shape)
        zero = jnp.zeros_like(qs)
        q2 = jnp.concatenate([jnp.where(lo, qs, zero), jnp.where(lo, zero, qs)], axis=0)
        s = _dot_nt(q2 * (HEAD_DIM ** -0.5), ks)
        p = jnp.exp(s - jnp.max(s, axis=-1, keepdims=True))
        o2 = _dot(p.astype(BF16), vs) / jnp.sum(p, axis=-1, keepdims=True)
        slabs.append(jnp.where(lo, o2[:n], o2[n:]))
    return jnp.concatenate(slabs, axis=1)


FFN_CHUNK = 256


def _finish(y3, gfin_ref, final):
    return _rmsnorm(y3, gfin_ref[...]) if final else y3


def _ffn_stage(y1, o, wco_ref, gf_ref, wgu_ref, wd_ref, act_ref, between=None):
    y2 = y1 + _dot(o.astype(BF16), wco_ref[...])
    xn = _rmsnorm(y2, gf_ref[...]).astype(BF16)
    for c in range(wd_ref.shape[0] // FFN_CHUNK):
        gu = _dot(xn, wgu_ref[:, 2 * c * FFN_CHUNK:2 * (c + 1) * FFN_CHUNK])
        g, u = gu[:, :FFN_CHUNK], gu[:, FFN_CHUNK:]
        act_ref[:, c * FFN_CHUNK:(c + 1) * FFN_CHUNK] = (g * jax.nn.sigmoid(g) * u).astype(BF16)
        if between is not None:
            between()
    return y2 + _dot(act_ref[...], wd_ref[...])


def _post_prompt_kernel(sink_ref, q_ref, kc_ref, kp_ref, vc_ref, vp_ref, x_ref, hm_ref, mk_ref, mv_ref,
                        wo_ref, gcr_ref, wcq_ref, wco_ref, gf_ref, wgu_ref, wd_ref, gfin_ref,
                        o_ref, act_ref, att_ref, y3_ref, *, final, ntiles, tiles_per_seq):
    s = pl.program_id(0)
    tile = jnp.minimum(s, ntiles - 1)
    first_block = (tile % tiles_per_seq) == 0
    slot = s % 2

    def swa_stages():
        def store(j, value):
            att_ref[slot, j * WINDOW:(j + 1) * WINDOW, :] = value.astype(att_ref.dtype)
        return _swa_prompt_stages(sink_ref, q_ref, kc_ref, kp_ref, vc_ref, vp_ref, first_block, store)

    @pl.when(s == 0)
    def _():
        y3_ref[...] = jnp.zeros_like(y3_ref)
        for _ in swa_stages():
            pass

    @pl.when(jnp.logical_and(s > 0, s <= ntiles))
    def _():
        swa = swa_stages()

        def advance(n=SWA_STAGES_PER_FFN_CHUNK):
            for _ in range(n):
                next(swa, None)

        o_ref[...] = _finish(y3_ref[...], gfin_ref, final)
        y1, qc = _mix_stage(x_ref[...], att_ref[1 - slot], hm_ref[...], wo_ref, gcr_ref, wcq_ref)
        advance()
        o = _cross_attention(qc.astype(BF16), mk_ref[...].astype(BF16), mv_ref[...].astype(BF16))
        advance()
        y3_ref[...] = _ffn_stage(y1, o, wco_ref, gf_ref, wgu_ref, wd_ref, act_ref, between=advance)
        for _ in swa:
            pass

    @pl.when(s > ntiles)
    def _():
        o_ref[...] = _finish(y3_ref[...], gfin_ref, final)


_MIX_WEIGHTS = ("w_out", "g_cross", "w_cq")
_FFN_WEIGHTS = ("w_co", "g_ffn", "w_gate_up", "w_down", "g_final")


def _post_prompt(x, q, k, v, hm, memk, memv, sinks, w, nbatch, final, tm):
    t, d = x.shape
    ntiles = t // tm
    tpb = ntiles // nbatch
    wpt = tm // WINDOW
    m = memk.shape[0] // nbatch
    hidden = w["w_down"].shape[0]

    def attn_tile(s):
        return jnp.minimum(s, ntiles - 1)

    def post_tile(s):
        return jnp.clip(s - 1, 0, ntiles - 1)

    def done_tile(s):
        return jnp.maximum(s - 2, 0)

    def attn_rows(width):
        return pl.BlockSpec((tm, width), lambda s: (attn_tile(s), 0))

    def prev_window(width):
        return pl.BlockSpec((WINDOW, width), lambda s: (jnp.maximum(attn_tile(s) * wpt - 1, 0), 0))

    def post_rows(width):
        return pl.BlockSpec((tm, width), lambda s: (post_tile(s), 0))

    mem_spec = pl.BlockSpec((m, CROSS_WIDTH), lambda s: (post_tile(s) // tpb, 0))
    names = _MIX_WEIGHTS + _FFN_WEIGHTS
    return pl.pallas_call(
        functools.partial(_post_prompt_kernel, final=final, ntiles=ntiles, tiles_per_seq=tpb),
        grid=(ntiles + 2,),
        in_specs=[
            pl.BlockSpec(memory_space=pltpu.SMEM),
            attn_rows(ATT_WIDTH), attn_rows(KV_WIDTH), prev_window(KV_WIDTH),
            attn_rows(KV_WIDTH), prev_window(KV_WIDTH),
            post_rows(d), post_rows(MLSTM_WIDTH), mem_spec, mem_spec,
        ] + _weight_specs(w, names),
        out_specs=pl.BlockSpec((tm, d), lambda s: (done_tile(s), 0)),
        out_shape=jax.ShapeDtypeStruct((t, d), F32),
        scratch_shapes=[pltpu.VMEM((tm, hidden), BF16), pltpu.VMEM((2, tm, ATT_WIDTH), BF16),
                        pltpu.VMEM((tm, d), F32)],
        compiler_params=pltpu.CompilerParams(
            dimension_semantics=("arbitrary",), vmem_limit_bytes=VMEM_LIMIT),
    )(sinks, q, k, k, v, v, x, hm, memk, memv, *[w[n] for n in names])


def _mix_kernel(x_ref, att_ref, hm_ref, wo_ref, gcr_ref, wcq_ref, y_ref, qc_ref):
    y1, qc = _mix_stage(x_ref[...], att_ref[...], hm_ref[...], wo_ref, gcr_ref, wcq_ref)
    y_ref[...] = y1
    qc_ref[...] = qc


def _mix(x, att, hm, w, tm):
    t, d = x.shape

    def rows(width):
        return pl.BlockSpec((tm, width), lambda i: (i, 0))

    return pl.pallas_call(
        _mix_kernel,
        grid=(t // tm,),
        in_specs=[rows(d), rows(ATT_WIDTH), rows(MLSTM_WIDTH)] + _weight_specs(w, _MIX_WEIGHTS),
        out_specs=[rows(d), rows(CROSS_WIDTH)],
        out_shape=[jax.ShapeDtypeStruct((t, d), F32), jax.ShapeDtypeStruct((t, CROSS_WIDTH), F32)],
        compiler_params=pltpu.CompilerParams(
            dimension_semantics=("arbitrary",), vmem_limit_bytes=VMEM_LIMIT),
    )(x, att, hm, *[w[n] for n in _MIX_WEIGHTS])


def _cross_sample_kernel(qc_ref, mk_ref, mv_ref, o_ref, *, nb, tq):
    mem = CROSS_WIDTH
    n = CROSS_HEADS * tq
    row_head = lax.broadcasted_iota(jnp.int32, (n, CROSS_WIDTH), 0) >> _log2(tq)
    lane_head = lax.broadcasted_iota(jnp.int32, (n, CROSS_WIDTH), 1) >> _log2(HEAD_DIM)
    own = row_head == lane_head
    for b in range(nb):
        q = qc_ref[b * tq:(b + 1) * tq, :]
        qx = jnp.where(own, jnp.concatenate([q] * CROSS_HEADS, axis=0), 0.0).astype(BF16)
        mk = mk_ref[b * mem:(b + 1) * mem, :].astype(BF16)
        mv = mv_ref[b * mem:(b + 1) * mem, :].astype(BF16)
        s = _dot(qx, mk) * (HEAD_DIM ** -0.5)
        p = jnp.exp(s - jnp.max(s, axis=-1, keepdims=True))
        pv = _dot_nt(p.astype(BF16), mv) / jnp.sum(p, axis=-1, keepdims=True)
        pv = jnp.where(own, pv, 0.0)
        out = pv[0:tq]
        for h in range(1, CROSS_HEADS):
            out = out + pv[h * tq:(h + 1) * tq]
        o_ref[b * tq:(b + 1) * tq, :] = out


def _cross_sample(qc, memk_t, memv_t, nbatch, nb):
    t = qc.shape[0]
    tq = t // nbatch
    mem = memk_t.shape[1]
    tok = pl.BlockSpec((nb * tq, CROSS_WIDTH), lambda i: (i, 0))
    mspec = pl.BlockSpec((nb * CROSS_WIDTH, mem), lambda i: (i, 0))
    return pl.pallas_call(
        functools.partial(_cross_sample_kernel, nb=nb, tq=tq),
        grid=(nbatch // nb,),
        in_specs=[tok, mspec, mspec],
        out_specs=tok,
        out_shape=jax.ShapeDtypeStruct((t, CROSS_WIDTH), F32),
        compiler_params=pltpu.CompilerParams(
            dimension_semantics=("arbitrary",), vmem_limit_bytes=VMEM_LIMIT),
    )(qc, memk_t, memv_t)


def _ffn_kernel(y_ref, o_ref, wco_ref, gf_ref, wgu_ref, wd_ref, gfin_ref, out_ref, act_ref, *, final):
    y3 = _ffn_stage(y_ref[...], o_ref[...], wco_ref, gf_ref, wgu_ref, wd_ref, act_ref)
    out_ref[...] = _finish(y3, gfin_ref, final)


def _ffn(y1, o, w, final, tm):
    t, d = y1.shape
    hidden = w["w_down"].shape[0]

    def rows(width):
        return pl.BlockSpec((tm, width), lambda i: (i, 0))

    return pl.pallas_call(
        functools.partial(_ffn_kernel, final=final),
        grid=(t // tm,),
        in_specs=[rows(d), rows(CROSS_WIDTH)] + _weight_specs(w, _FFN_WEIGHTS),
        out_specs=rows(d),
        out_shape=jax.ShapeDtypeStruct((t, d), F32),
        scratch_shapes=[pltpu.VMEM((tm, hidden), BF16)],
        compiler_params=pltpu.CompilerParams(
            dimension_semantics=("arbitrary",), vmem_limit_bytes=VMEM_LIMIT),
    )(y1, o, *[w[n] for n in _FFN_WEIGHTS])


def _layer_weights(l, w_in, b_igate, b_fgate, g_mlstm_head, g_mix, g_cross, g_mem, w_cq, w_ck, w_cv,
                   w_co, g_ffn, g_final):
    gate_w = w_in[l][:, MAIN_WIDTH:]
    gate_b = jnp.concatenate([b_igate[l], b_fgate[l]]).astype(F32)
    ngate = gate_b.shape[0]
    row = lambda a: a.astype(F32).reshape(1, -1)
    w_t = w_in[l].T
    k0 = ATT_WIDTH
    mk0 = ATT_WIDTH + 2 * KV_WIDTH + MQK_WIDTH
    return {
        "w_kvk_t": jnp.concatenate([w_t[k0:k0 + 2 * KV_WIDTH], w_t[mk0:mk0 + MQK_WIDTH]]).astype(BF16),
        "w_main": w_t[:MAIN_WIDTH].astype(BF16),
        "w_gate_cols": jnp.pad(gate_w, ((0, 0), (0, LANES - ngate))).astype(BF16),
        "w_gate_rows": jnp.pad(gate_w.T, ((0, GATE_ROWS - ngate), (0, 0))).astype(BF16),
        "b_gate_cols": jnp.pad(gate_b, (0, LANES - ngate)).reshape(1, LANES),
        "b_gate_rows": jnp.pad(gate_b, (0, GATE_ROWS - ngate)).reshape(GATE_ROWS, 1),
        "g_mix": row(g_mix[l]), "g_cross": row(g_cross[l]), "g_mem": row(g_mem[l]),
        "g_ffn": row(g_ffn[l]), "g_final": row(g_final), "g_head": row(g_mlstm_head[l]),
        "w_cq": w_cq[l].astype(BF16), "w_ck": w_ck[l].astype(BF16),
        "w_cv": w_cv[l].astype(BF16), "w_co": w_co[l].astype(BF16),
    }


TOKEN_TILE = 512
MLSTM_STAGES_PER_PROJ_STAGE = 15
SWA_STAGES_PER_FFN_CHUNK = 3
MLSTM_CHUNK_ROWS = 256
SWA_SAMPLE_BATCH = 16
MLSTM_SAMPLE_BATCH = 16
CROSS_SAMPLE_BATCH = 16


def kernel(x_prompt, x_sample, mem_prompt, cache_swa_k, cache_swa_v, state_mlstm_C, state_mlstm_n,
           state_mlstm_m, cache_mem_k, cache_mem_v, w_in, b_igate, b_fgate, attn_sinks,
           g_mlstm_head, w_out, g_mix, g_cross, g_mem, w_cq, w_ck, w_cv, w_co, g_ffn, w_gate,
           w_up, w_down, g_final):
    depth = w_in.shape[0]
    bp, sp, d = x_prompt.shape
    bs, ss, _ = x_sample.shape
    mem_tokens = mem_prompt.shape[1]
    past = cache_swa_k.shape[2]
    yp = x_prompt.reshape(bp * sp, d)
    ys = x_sample.reshape(bs * ss, d)
    mem = mem_prompt.reshape(bp * mem_tokens, d)
    outs = [[] for _ in range(12)]

    for l in range(depth):
        final = l == depth - 1
        w = _layer_weights(l, w_in, b_igate, b_fgate, g_mlstm_head, g_mix, g_cross, g_mem, w_cq,
                           w_ck, w_cv, w_co, g_ffn, g_final)
        sinks = attn_sinks[l].astype(F32)

        q, k, v, hm, c_p, n_p, m_p, cast, memk, memv = _mixer_prompt(
            yp, mem, w, w_gate[l].astype(F32), w_up[l].astype(F32), w_down[l].astype(F32),
            w_out[l].astype(F32), bp, MLSTM_CHUNK_ROWS)
        w.update(cast)
        yp = _post_prompt(yp, q, k, v, hm, memk, memv, sinks, w, bp, final, TOKEN_TILE)
        win_shape = (bp, WINDOW, ATT_KV_HEADS, HEAD_DIM)
        outs[0].append(k.reshape(bp, sp, KV_WIDTH)[:, sp - WINDOW:].reshape(win_shape))
        outs[1].append(v.reshape(bp, sp, KV_WIDTH)[:, sp - WINDOW:].reshape(win_shape))
        outs[2].append(c_p)
        outs[3].append(n_p)
        outs[4].append(m_p)
        mem_shape = (bp, mem_tokens, CROSS_HEADS, HEAD_DIM)
        outs[5].append(memk.reshape(mem_shape))
        outs[6].append(memv.reshape(mem_shape))

        def tokens_last(a):
            return jnp.transpose(a.astype(F32), (0, 2, 3, 1)).reshape(-1, a.shape[1])

        q, mq, mk, mv, og, grow, gcol, knt, vnt, mkt = _decode_proj(ys, w, TOKEN_TILE)
        att, kbuf_t, vbuf_t = _swa_sample(
            q, knt, vnt, tokens_last(cache_swa_k[l]), tokens_last(cache_swa_v[l]), sinks, bs,
            SWA_SAMPLE_BATCH)
        m_rows = jnp.pad(jnp.repeat(state_mlstm_m[l].astype(F32), ss, axis=0),
                         ((0, 0), (0, LANES - MLSTM_HEADS)))
        ct_in = jnp.swapaxes(state_mlstm_C[l].astype(F32), 2, 3).reshape(-1, MLSTM_DV)
        hm, ct_s, n_s, mt = _mlstm_sample(
            mq, mk, mkt, mv, og, gcol, grow, m_rows, ct_in,
            state_mlstm_n[l].astype(F32).reshape(bs * MLSTM_HEADS, 1, MLSTM_DK),
            w["g_head"], bs, MLSTM_SAMPLE_BATCH)
        y1, qc = _mix(ys, att, hm, w, TOKEN_TILE)
        o = _cross_sample(qc, tokens_last(cache_mem_k[l]), tokens_last(cache_mem_v[l]), bs,
                          CROSS_SAMPLE_BATCH)
        ys = _ffn(y1, o, w, final, TOKEN_TILE)
        buf_t_shape = (bs, ATT_KV_HEADS, HEAD_DIM, past)
        outs[7].append(jnp.transpose(kbuf_t.reshape(buf_t_shape), (0, 3, 1, 2)))
        outs[8].append(jnp.transpose(vbuf_t.reshape(buf_t_shape), (0, 3, 1, 2)))
        outs[9].append(jnp.swapaxes(ct_s.reshape(bs, MLSTM_HEADS, MLSTM_DK, MLSTM_DV), 2, 3))
        outs[10].append(n_s.reshape(bs, MLSTM_HEADS, MLSTM_DK))
        outs[11].append(mt.reshape(bs, ss, LANES)[:, ss - 1, :MLSTM_HEADS])

    return (yp.reshape(bp, sp, d), ys.reshape(bs, ss, d)) + tuple(jnp.stack(o) for o in outs)
```

```python
import functools

import jax
import jax.numpy as jnp
from jax import lax
from jax.experimental import pallas as pl
from jax.experimental.pallas import tpu as pltpu

F32 = jnp.float32
BF16 = jnp.bfloat16

HEAD_DIM = 64
ATT_GROUP = 4
ATT_KV_HEADS = 2
ATT_WIDTH = 512
KV_WIDTH = ATT_KV_HEADS * HEAD_DIM
WINDOW = 128
LOG_WINDOW = 7
MLSTM_HEADS = 4
MLSTM_DV = 128
MLSTM_DK = 64
MQK_WIDTH = MLSTM_HEADS * MLSTM_DK
MLSTM_WIDTH = MLSTM_HEADS * MLSTM_DV
MAIN_WIDTH = ATT_WIDTH + 2 * KV_WIDTH + 2 * MQK_WIDTH + 2 * MLSTM_WIDTH
CROSS_HEADS = 4
CROSS_WIDTH = CROSS_HEADS * HEAD_DIM
EPS = 1e-6
NEG_INF = float("-inf")
assert HEAD_DIM == 4 ** 3

LANES = 128
GATE_ROWS = 16
VMEM_LIMIT = 56 * 1024 * 1024

NT = (((1,), (1,)), ((), ()))
TN = (((0,), (0,)), ((), ()))


def _dot(a, b):
    return jnp.dot(a, b, preferred_element_type=F32)


def _dot_nt(a, b):
    return lax.dot_general(a, b, NT, preferred_element_type=F32)


def _dot_tn(a, b):
    return lax.dot_general(a, b, TN, preferred_element_type=F32)


def _rmsnorm(x, g):
    return x * lax.rsqrt(jnp.mean(x * x, axis=-1, keepdims=True) + EPS) * g


def _log_sigmoid(x):
    return jnp.minimum(x, 0.0) - jnp.log1p(jnp.exp(-jnp.abs(x)))


def _split3(x):
    hi = x.astype(BF16)
    r1 = x - hi.astype(F32)
    mid = r1.astype(BF16)
    lo = (r1 - mid.astype(F32)).astype(BF16)
    return hi, mid, lo


def _cumsum_cols(tri, x):
    hi, mid, lo = _split3(x)
    return _dot(tri, hi) + _dot(tri, mid) + _dot(tri, lo)


def _cumsum_rows(x, tri):
    hi, mid, lo = _split3(x)
    return _dot(hi, tri) + _dot(mid, tri) + _dot(lo, tri)


def _log2(n):
    assert n > 0 and n & (n - 1) == 0, n
    return n.bit_length() - 1


def _low_half(shape):
    return lax.broadcasted_iota(jnp.int32, shape, 1) < HEAD_DIM


def _decode_proj_kernel(x_ref, g_ref, w_ref, wgr_ref, br_ref, wgc_ref, bc_ref, wt_ref,
                        q_ref, mq_ref, mk_ref, mv_ref, og_ref, gr_ref, gc_ref, kt_ref, vt_ref, mkt_ref):
    xn = _rmsnorm(x_ref[...], g_ref[...]).astype(BF16)
    zt = _dot_nt(wt_ref[...], xn)
    kt_ref[...] = zt[:KV_WIDTH]
    vt_ref[...] = zt[KV_WIDTH:2 * KV_WIDTH]
    mkt_ref[...] = zt[2 * KV_WIDTH:] * (MLSTM_DK ** -0.5)

    def mm(lo, hi):
        return _dot_nt(xn, w_ref[lo:hi, :])

    o = 0
    q_ref[...] = mm(o, o + ATT_WIDTH)
    o += ATT_WIDTH + 2 * KV_WIDTH
    mqk = mm(o, o + 2 * MQK_WIDTH)
    mq_ref[...] = mqk[:, :MQK_WIDTH]
    mk_ref[...] = mqk[:, MQK_WIDTH:] * (MLSTM_DK ** -0.5)
    o += 2 * MQK_WIDTH
    mv_ref[...] = mm(o, o + MLSTM_WIDTH)
    o += MLSTM_WIDTH
    og_ref[...] = mm(o, o + MLSTM_WIDTH)

    zc = _dot(xn, wgc_ref[...]) + bc_ref[...]
    lane = lax.broadcasted_iota(jnp.int32, zc.shape, 1)
    gc_ref[...] = jnp.where(lane < MLSTM_HEADS, zc, _log_sigmoid(zc))
    zr = _dot_nt(wgr_ref[...], xn) + br_ref[...]
    row = lax.broadcasted_iota(jnp.int32, zr.shape, 0)
    gr_ref[...] = jnp.where(row < MLSTM_HEADS, zr, _log_sigmoid(zr))


def _const_spec(shape):
    nd = len(shape)
    return pl.BlockSpec(shape, lambda *_: (0,) * nd, pipeline_mode=pl.Buffered(1))


def _weight_specs(w, names):
    return [_const_spec(w[n].shape) for n in names]


def _decode_proj(x, w, tm):
    t, d = x.shape

    def rows(width):
        return pl.BlockSpec((tm, width), lambda i: (i, 0))

    def lanes(height):
        return pl.BlockSpec((height, tm), lambda i: (0, i))

    outs = [(rows, ATT_WIDTH), (rows, MQK_WIDTH), (rows, MQK_WIDTH), (rows, MLSTM_WIDTH),
            (rows, MLSTM_WIDTH), (lanes, GATE_ROWS), (rows, LANES), (lanes, KV_WIDTH),
            (lanes, KV_WIDTH), (lanes, MQK_WIDTH)]
    names = ["g_mix", "w_main", "w_gate_rows", "b_gate_rows", "w_gate_cols", "b_gate_cols", "w_kvk_t"]
    return pl.pallas_call(
        _decode_proj_kernel,
        grid=(t // tm,),
        in_specs=[rows(d)] + _weight_specs(w, names),
        out_specs=[kind(size) for kind, size in outs],
        out_shape=[jax.ShapeDtypeStruct((t, size) if kind is rows else (size, t), F32)
                   for kind, size in outs],
        compiler_params=pltpu.CompilerParams(
            dimension_semantics=("arbitrary",), vmem_limit_bytes=VMEM_LIMIT),
    )(x, *[w[n] for n in names])


def _dup_heads(x):
    swapped = pltpu.roll(x, HEAD_DIM, axis=1)
    lo = _low_half(x.shape)
    return (jnp.where(lo, x, swapped).astype(BF16), jnp.where(lo, swapped, x).astype(BF16))


def _stack_query_heads(q, h):
    parts = []
    for g in range(ATT_GROUP):
        hq = h * ATT_GROUP + g
        slab = q[:, (hq // 2) * LANES:(hq // 2 + 1) * LANES]
        lo = _low_half(slab.shape)
        keep = lo if hq % 2 == 0 else jnp.logical_not(lo)
        parts.append(jnp.where(keep, slab, jnp.zeros_like(slab)))
    return jnp.concatenate(parts, axis=0).astype(BF16)


def _sink_column(sink_ref, h, rows_per_head):
    n = ATT_GROUP * rows_per_head
    grp = lax.broadcasted_iota(jnp.int32, (n, 1), 0) >> _log2(rows_per_head)
    col = jnp.full((n, 1), sink_ref[h * ATT_GROUP], F32)
    for g in range(1, ATT_GROUP):
        col = jnp.where(grp == g, sink_ref[h * ATT_GROUP + g], col)
    return col


def _swa_prompt_stages(sink_ref, q_ref, kc_ref, kp_ref, vc_ref, vp_ref, first_block, store):
    qb = q_ref.shape[0]
    k_all = jnp.concatenate([kp_ref[...], kc_ref[...]], axis=0)
    v_all = jnp.concatenate([vp_ref[...], vc_ref[...]], axis=0)
    kd = _dup_heads(k_all)
    v_t = v_all.T.astype(BF16)

    n = ATT_GROUP * WINDOW
    key = lax.broadcasted_iota(jnp.int32, (2 * WINDOW, n), 0)
    qry = lax.broadcasted_iota(jnp.int32, (2 * WINDOW, n), 1) & (WINDOW - 1)
    delta = key - qry
    in_band = (delta >= 0) & (delta <= WINDOW)
    grp = lax.broadcasted_iota(jnp.int32, (1, n), 1) >> LOG_WINDOW
    sinks = []
    for h in range(ATT_KV_HEADS):
        row = jnp.full((1, n), sink_ref[h * ATT_GROUP], F32)
        for g in range(1, ATT_GROUP):
            row = jnp.where(grp == g, sink_ref[h * ATT_GROUP + g], row)
        sinks.append(row)
    yield

    probs = []
    for j in range(qb // WINDOW):
        valid = in_band & (key >= jnp.where(first_block, WINDOW, 0)) if j == 0 else in_band
        q = q_ref[j * WINDOW:(j + 1) * WINDOW, :] * (HEAD_DIM ** -0.5)
        band = slice(j * WINDOW, (j + 2) * WINDOW)
        for h in range(ATT_KV_HEADS):
            qh = _stack_query_heads(q, h)
            s = _dot_nt(kd[h][band], qh)
            probs.append(dict(h=h, band=band, s=jnp.where(valid, s, NEG_INF)))
            yield
    for p in probs:
        p["m"] = jnp.maximum(jnp.max(p["s"], axis=0, keepdims=True), sinks[p["h"]])
        yield
    for p in probs:
        p["p"] = jnp.exp(p["s"] - p["m"])
        yield
    for p in probs:
        h = p["h"]
        denom = jnp.sum(p["p"], axis=0, keepdims=True) + jnp.exp(sinks[h] - p["m"])
        p["o_t"] = _dot(v_t[h * HEAD_DIM:(h + 1) * HEAD_DIM, p["band"]], p["p"].astype(BF16)) / denom
        yield
    for j in range(qb // WINDOW):
        pieces = [p["o_t"][:, g * WINDOW:(g + 1) * WINDOW]
                  for p in probs[j * ATT_KV_HEADS:(j + 1) * ATT_KV_HEADS] for g in range(ATT_GROUP)]
        store(j, jnp.concatenate(pieces, axis=0).T)
        yield


def _swa_sample_kernel(sink_ref, q_ref, knt_ref, vnt_ref, ck_ref, cv_ref,
                       o_ref, kbuf_ref, vbuf_ref, *, nb, tq):
    past = ck_ref.shape[1]
    hw = ATT_KV_HEADS * HEAD_DIM
    heads = ATT_KV_HEADS * ATT_GROUP
    n = heads * tq
    fresh = past - tq
    t = lax.broadcasted_iota(jnp.int32, (n, 2 * past), 0) & (tq - 1)
    col = lax.broadcasted_iota(jnp.int32, (n, 2 * past), 1)
    valid = ((col < past) & (col >= t)) | ((col >= past + fresh) & (col - (past + fresh) <= t))
    sink = jnp.concatenate([_sink_column(sink_ref, h, tq) for h in range(ATT_KV_HEADS)], axis=0)
    lane = lax.broadcasted_iota(jnp.int32, (hw, past), 1)
    knt = knt_ref[...]
    vnt = vnt_ref[...]
    zero_slab = jnp.zeros((tq, LANES), F32)
    lo = _low_half((tq, LANES))

    def shifted(old, new_t, b):
        return jnp.where(lane >= fresh, pltpu.roll(new_t, (fresh - b * tq) % past, axis=1),
                         pltpu.roll(old, fresh, axis=1))

    def dup_rows(x):
        h0, h1 = x[:HEAD_DIM], x[HEAD_DIM:]
        return jnp.concatenate([h0, h0, h1, h1], axis=0)

    for b in range(nb):
        k_old = ck_ref[b * hw:(b + 1) * hw, :]
        v_old = cv_ref[b * hw:(b + 1) * hw, :]
        k_new = shifted(k_old, knt, b)
        v_new = shifted(v_old, vnt, b)
        kbuf_ref[b * hw:(b + 1) * hw, :] = k_new
        vbuf_ref[b * hw:(b + 1) * hw, :] = v_new
        kop = jnp.concatenate([dup_rows(k_old), dup_rows(k_new)], axis=1).astype(BF16)
        vop = jnp.concatenate([dup_rows(v_old), dup_rows(v_new)], axis=1).astype(BF16)

        q = q_ref[b * tq:(b + 1) * tq, :]
        blocks = []
        for hq in range(heads):
            slab = q[:, (hq // 2) * LANES:(hq // 2 + 1) * LANES]
            slab = jnp.where(lo if hq % 2 == 0 else jnp.logical_not(lo), slab, 0.0)
            pair = [slab, zero_slab] if hq // ATT_GROUP == 0 else [zero_slab, slab]
            blocks.append(jnp.concatenate(pair, axis=1))
        qx = jnp.concatenate(blocks, axis=0).astype(BF16)

        s = _dot(qx, kop) * (HEAD_DIM ** -0.5)
        s = jnp.where(valid, s, NEG_INF)
        m = jnp.maximum(jnp.max(s, axis=-1, keepdims=True), sink)
        p = jnp.exp(s - m)
        denom = jnp.sum(p, axis=-1, keepdims=True) + jnp.exp(sink - m)
        o = _dot_nt(p.astype(BF16), vop) / denom

        slabs = []
        for j in range(heads // 2):
            h = (2 * j) // ATT_GROUP
            even = o[(2 * j) * tq:(2 * j + 1) * tq, h * LANES:(h + 1) * LANES]
            odd = o[(2 * j + 1) * tq:(2 * j + 2) * tq, h * LANES:(h + 1) * LANES]
            slabs.append(jnp.where(lo, even, odd))
        o_ref[b * tq:(b + 1) * tq, :] = jnp.concatenate(slabs, axis=1)


def _swa_sample(q, knt, vnt, cache_kt, cache_vt, sinks, nbatch, nb):
    t = q.shape[0]
    tq = t // nbatch
    past = cache_kt.shape[1]
    hw = ATT_KV_HEADS * HEAD_DIM
    assert past == WINDOW and tq & (tq - 1) == 0 and nb * tq == past

    def rows(r, width):
        return pl.BlockSpec((r, width), lambda i: (i, 0))

    new_t = pl.BlockSpec((hw, nb * tq), lambda i: (0, i))
    return pl.pallas_call(
        functools.partial(_swa_sample_kernel, nb=nb, tq=tq),
        grid=(nbatch // nb,),
        in_specs=[
            pl.BlockSpec(memory_space=pltpu.SMEM),
            rows(nb * tq, ATT_WIDTH), new_t, new_t, rows(nb * hw, past), rows(nb * hw, past),
        ],
        out_specs=[rows(nb * tq, ATT_WIDTH), rows(nb * hw, past), rows(nb * hw, past)],
        out_shape=[
            jax.ShapeDtypeStruct((t, ATT_WIDTH), F32),
            jax.ShapeDtypeStruct(cache_kt.shape, F32),
            jax.ShapeDtypeStruct(cache_vt.shape, F32),
        ],
        compiler_params=pltpu.CompilerParams(
            dimension_semantics=("arbitrary",), vmem_limit_bytes=VMEM_LIMIT),
    )(sinks, q, knt, vnt, cache_kt, cache_vt)


def _mlstm_intra(q, k, v, it_c, bt_c, it_r, bt_r, m_c, valid):
    d = jnp.where(valid, bt_c - bt_r + it_r, NEG_INF)
    inter = bt_c + m_c
    mt = jnp.maximum(jnp.max(d, axis=-1, keepdims=True), inter)
    sm = _dot_nt(q, k) * jnp.exp(d - mt)
    wi = jnp.exp(inter - mt)
    return _dot(sm.astype(BF16), v), jnp.sum(sm, axis=-1, keepdims=True), mt, wi


def _mlstm_head_out(num, den, mt, og, g_head):
    lower = jnp.maximum(jnp.abs(den), jnp.exp(-mt))
    h = num / lower
    hn = h * lax.rsqrt(jnp.mean(h * h, axis=-1, keepdims=True) + EPS)
    return hn * g_head * jax.nn.sigmoid(og)


def _mlstm_prompt_stages(mq, mk, mv, og, gr, gh_ref, hm_ref, c_ref, n_ref, m_ref, nbatch, chunk):
    s_idx = lax.broadcasted_iota(jnp.int32, (chunk, chunk), 0)
    l_idx = lax.broadcasted_iota(jnp.int32, (chunk, chunk), 1)
    causal = s_idx <= l_idx
    triu = jnp.where(causal, 1.0, 0.0).astype(BF16)
    lo = _low_half((chunk, LANES))
    pad_rows = jnp.zeros((LANES - MLSTM_HEADS, chunk), F32)
    sub8 = (GATE_ROWS, chunk)

    probs = []
    for b in range(nbatch):
        g = gr(b)
        btr = _cumsum_rows(g, triu)
        it_rows = g[:MLSTM_HEADS]
        bt_rows = btr[MLSTM_HEADS:2 * MLSTM_HEADS]
        u_cols = jnp.concatenate([it_rows - bt_rows, pad_rows], axis=0).T
        for h in range(MLSTM_HEADS):
            slab = slice((h // 2) * LANES, (h // 2 + 1) * LANES)
            keep = lo if h % 2 == 0 else jnp.logical_not(lo)
            q = mq(b, slab)
            k = mk(b, slab)
            probs.append(dict(
                b=b, h=h, idx=b * MLSTM_HEADS + h, cols=slice(h * MLSTM_DV, (h + 1) * MLSTM_DV),
                q=jnp.where(keep, q, jnp.zeros_like(q)), k=jnp.where(keep, k, jnp.zeros_like(k)),
                u_c=u_cols[:, h:h + 1], it_r=it_rows[h:h + 1, :], bt_r=bt_rows[h:h + 1, :]))
        yield

    for p in probs:
        p["m_old"] = m_ref[p["idx"]][:, 0:1]
        p["d"] = jnp.where(causal, p["bt_r"] + p["u_c"], NEG_INF)
        p["inter"] = p["bt_r"] + p["m_old"]
        p["q_t"] = p["q"].astype(F32).T.astype(BF16)
        p["kq"] = _dot(p["k"], p["q_t"])
        p["v_t"] = mv(p["b"], p["cols"]).astype(F32).T
        yield
    for p in probs:
        p["mt"] = jnp.maximum(jnp.max(p["d"], axis=0, keepdims=True), p["inter"])
        yield
    for p in probs:
        p["sm"] = p["kq"] * jnp.exp(p["d"] - p["mt"])
        p["wi"] = jnp.exp(p["inter"] - p["mt"])
        yield
    for p in probs:
        c_old = c_ref[p["idx"]]
        n_old = jnp.broadcast_to(n_ref[p["idx"]], (sub8[0], LANES))
        num = (_dot(p["v_t"].astype(BF16), p["sm"].astype(BF16))
               + p["wi"] * _dot(c_old.astype(BF16), p["q_t"]))
        den = (jnp.sum(p["sm"], axis=0, keepdims=True)
               + p["wi"] * _dot(n_old.astype(BF16), p["q_t"])[0:1])
        lower = jnp.maximum(jnp.abs(den), jnp.exp(-p["mt"]))
        h_t = num / lower
        p["hn"] = (h_t * lax.rsqrt(jnp.mean(h_t * h_t, axis=0, keepdims=True) + EPS)).T
        yield
    for p in probs:
        gate = jax.nn.sigmoid(og(p["b"], p["cols"]).astype(F32))
        hm_ref[p["b"], :, p["cols"]] = (p["hn"] * gh_ref[:, p["cols"]] * gate).astype(hm_ref.dtype)
        yield
    for p in probs:
        idx = p["idx"]
        b_last = p["bt_r"][:, chunk - 1:chunk]
        m_new = p["mt"][:, chunk - 1:chunk]
        wk = jnp.exp(b_last - p["bt_r"] + p["it_r"] - m_new)
        wc = jnp.exp(b_last + p["m_old"] - m_new)
        c_ref[idx] = wc * c_ref[idx] + _dot((p["v_t"] * wk).astype(BF16), p["k"])
        n_ref[idx] = wc * n_ref[idx] + _dot(jnp.broadcast_to(wk, sub8).astype(BF16), p["k"])[0:1]
        m_ref[idx] = jnp.broadcast_to(m_new, (1, LANES))
        yield


def _mixer_prompt_kernel(x_ref, g_ref, w_ref, wgr_ref, br_ref, gh_ref, gmem_ref, wck_ref, wcv_ref,
                         wg_ref, wu_ref, wd_ref, wo_ref, mem_ref,
                         q_ref, k_ref, v_ref, hm_ref, c_ref, n_ref, m_ref, wgu_out, wd_out, wo_out,
                         memk_out, memv_out,
                         mq_scr, mk_scr, mv_scr, og_scr, gr_scr, *, nbatch, chunk):
    s = pl.program_id(0)
    wr = s % 2
    rd = 1 - wr
    rows = nbatch * chunk

    def side_work():
        for c in range(0, wg_ref.shape[1], FFN_CHUNK):
            wgu_out[:, 2 * c:2 * c + FFN_CHUNK] = wg_ref[:, c:c + FFN_CHUNK].astype(BF16)
            wgu_out[:, 2 * c + FFN_CHUNK:2 * (c + FFN_CHUNK)] = wu_ref[:, c:c + FFN_CHUNK].astype(BF16)
        wd_out[...] = wd_ref[...].astype(BF16)
        wo_out[...] = wo_ref[...].astype(BF16)
        mn = _rmsnorm(mem_ref[...], gmem_ref[...]).astype(BF16)
        memk_out[...] = _dot(mn, wck_ref[...])
        memv_out[...] = _dot(mn, wcv_ref[...])

    def proj_stages():
        xn = _rmsnorm(x_ref[...].reshape(rows, x_ref.shape[-1]), g_ref[...]).astype(BF16)
        yield

        def mm(lo, hi):
            return _dot_nt(xn, w_ref[lo:hi, :])

        o = 0
        q_ref[...] = mm(o, o + ATT_WIDTH).astype(q_ref.dtype).reshape(q_ref.shape)
        yield
        o += ATT_WIDTH
        kv = mm(o, o + 2 * KV_WIDTH)
        k_ref[...] = kv[:, :KV_WIDTH].reshape(k_ref.shape)
        v_ref[...] = kv[:, KV_WIDTH:].reshape(v_ref.shape)
        yield
        o += 2 * KV_WIDTH
        mqk = mm(o, o + 2 * MQK_WIDTH)
        mq_scr[wr] = mqk[:, :MQK_WIDTH].astype(BF16)
        mk_scr[wr] = (mqk[:, MQK_WIDTH:] * (MLSTM_DK ** -0.5)).astype(BF16)
        yield
        o += 2 * MQK_WIDTH
        mv_scr[wr] = mm(o, o + MLSTM_WIDTH).astype(BF16)
        yield
        o += MLSTM_WIDTH
        og_scr[wr] = mm(o, o + MLSTM_WIDTH).astype(BF16)
        yield
        zr = _dot_nt(wgr_ref[...], xn) + br_ref[...]
        row = lax.broadcasted_iota(jnp.int32, zr.shape, 0)
        gr_scr[wr] = jnp.where(row < MLSTM_HEADS, zr, _log_sigmoid(zr))
        yield

    def mlstm_stages():
        def rows_of(scr):
            return lambda b, lanes: scr[rd, b * chunk:(b + 1) * chunk, lanes]
        return _mlstm_prompt_stages(
            rows_of(mq_scr), rows_of(mk_scr), rows_of(mv_scr), rows_of(og_scr),
            lambda b: gr_scr[rd, :, b * chunk:(b + 1) * chunk],
            gh_ref, hm_ref, c_ref, n_ref, m_ref, nbatch, chunk)

    @pl.when(s == 0)
    def _():
        c_ref[...] = jnp.zeros_like(c_ref)
        n_ref[...] = jnp.zeros_like(n_ref)
        m_ref[...] = jnp.zeros_like(m_ref)
        side_work()
        for _ in proj_stages():
            pass

    @pl.when(s > 0)
    def _():
        side_work()
        mlstm = mlstm_stages()
        for _ in proj_stages():
            for _ in range(MLSTM_STAGES_PER_PROJ_STAGE):
                next(mlstm, None)
        for _ in mlstm:
            pass


def _mixer_prompt(x, mem, w, w_gate, w_up, w_down, w_out, nbatch, chunk):
    t, d = x.shape
    seq = t // nbatch
    nchunks = seq // chunk
    nstate = nbatch * MLSTM_HEADS
    rows = nbatch * chunk
    hidden = w_down.shape[0]
    bf16_rows = 16
    assert hidden % FFN_CHUNK == 0 and d % (nchunks * bf16_rows) == 0 and hidden % (nchunks * bf16_rows) == 0
    assert mem.shape[0] % (nchunks * 8) == 0

    def proj_chunk(width):
        return pl.BlockSpec((nbatch, chunk, width), lambda s: (0, jnp.minimum(s, nchunks - 1), 0))

    def state(shape):
        return pl.BlockSpec(shape, lambda s: (0, 0, 0))

    def weight_rows(a, width=None):
        return pl.BlockSpec((a.shape[0] // nchunks, width or a.shape[1]),
                            lambda s: (jnp.minimum(s, nchunks - 1), 0))

    names = ["g_mix", "w_main", "w_gate_rows", "b_gate_rows", "g_head", "g_mem", "w_ck", "w_cv"]
    q, k, v, hm, c_st, n_st, m_st, w_gate_up_bf, w_down_bf, w_out_bf, memk, memv = pl.pallas_call(
        functools.partial(_mixer_prompt_kernel, nbatch=nbatch, chunk=chunk),
        grid=(nchunks + 1,),
        in_specs=[proj_chunk(d)] + _weight_specs(w, names) + [
            weight_rows(w_gate), weight_rows(w_up), weight_rows(w_down), weight_rows(w_out),
            weight_rows(mem),
        ],
        out_specs=[
            proj_chunk(ATT_WIDTH), proj_chunk(KV_WIDTH), proj_chunk(KV_WIDTH),
            pl.BlockSpec((nbatch, chunk, MLSTM_WIDTH), lambda s: (0, jnp.maximum(s - 1, 0), 0)),
            state((nstate, MLSTM_DV, LANES)), state((nstate, 1, LANES)), state((nstate, 1, LANES)),
            weight_rows(w_gate, 2 * hidden), weight_rows(w_down), weight_rows(w_out),
            weight_rows(mem, CROSS_WIDTH), weight_rows(mem, CROSS_WIDTH),
        ],
        out_shape=[
            jax.ShapeDtypeStruct((nbatch, seq, ATT_WIDTH), BF16),
            jax.ShapeDtypeStruct((nbatch, seq, KV_WIDTH), F32),
            jax.ShapeDtypeStruct((nbatch, seq, KV_WIDTH), F32),
            jax.ShapeDtypeStruct((nbatch, seq, MLSTM_WIDTH), BF16),
            jax.ShapeDtypeStruct((nstate, MLSTM_DV, LANES), F32),
            jax.ShapeDtypeStruct((nstate, 1, LANES), F32),
            jax.ShapeDtypeStruct((nstate, 1, LANES), F32),
            jax.ShapeDtypeStruct((d, 2 * hidden), BF16),
            jax.ShapeDtypeStruct((hidden, d), BF16),
            jax.ShapeDtypeStruct(w_out.shape, BF16),
            jax.ShapeDtypeStruct((mem.shape[0], CROSS_WIDTH), F32),
            jax.ShapeDtypeStruct((mem.shape[0], CROSS_WIDTH), F32),
        ],
        scratch_shapes=[
            pltpu.VMEM((2, rows, MQK_WIDTH), BF16), pltpu.VMEM((2, rows, MQK_WIDTH), BF16),
            pltpu.VMEM((2, rows, MLSTM_WIDTH), BF16), pltpu.VMEM((2, rows, MLSTM_WIDTH), BF16),
            pltpu.VMEM((2, GATE_ROWS, rows), F32),
        ],
        compiler_params=pltpu.CompilerParams(
            dimension_semantics=("arbitrary",), vmem_limit_bytes=VMEM_LIMIT),
    )(x.reshape(nbatch, seq, d), *[w[n] for n in names], w_gate, w_up, w_down, w_out, mem)
    q = q.reshape(t, ATT_WIDTH)
    k = k.reshape(t, KV_WIDTH)
    v = v.reshape(t, KV_WIDTH)

    c_st = c_st.reshape(nbatch, MLSTM_HEADS, MLSTM_DV, 2, MLSTM_DK)
    n_st = n_st.reshape(nbatch, MLSTM_HEADS, 2, MLSTM_DK)
    c_fin = jnp.stack([c_st[:, h, :, h % 2, :] for h in range(MLSTM_HEADS)], axis=1)
    n_fin = jnp.stack([n_st[:, h, h % 2, :] for h in range(MLSTM_HEADS)], axis=1)
    m_fin = m_st[:, 0, 0].reshape(nbatch, MLSTM_HEADS)
    bf16_weights = {"w_gate_up": w_gate_up_bf, "w_down": w_down_bf, "w_out": w_out_bf}
    return q, k, v, hm.reshape(t, MLSTM_WIDTH), c_fin, n_fin, m_fin, bf16_weights, memk, memv


def _mlstm_sample_kernel(mq_ref, mk_ref, mkt_ref, mv_ref, og_ref, gc_ref, gr_ref, mrow_ref, c_ref,
                         n_ref, gh_ref, hm_ref, co_ref, no_ref, mt_ref, *, nb, tq):
    rows = nb * tq
    shift = tq.bit_length() - 1
    r = lax.broadcasted_iota(jnp.int32, (rows, rows), 0)
    s = lax.broadcasted_iota(jnp.int32, (rows, rows), 1)
    same = (r >> shift) == (s >> shift)
    valid = same & (s <= r)
    tril = jnp.where(valid, 1.0, 0.0).astype(BF16)
    triu = jnp.where(same & (r <= s), 1.0, 0.0).astype(BF16)
    row_batch = lax.broadcasted_iota(jnp.int32, (rows, 1), 0) >> shift
    lane = lax.broadcasted_iota(jnp.int32, (rows, LANES), 1)

    gc = gc_ref[...]
    gr = gr_ref[...]
    btc = _cumsum_cols(tril, gc)
    btr = _cumsum_rows(gr, triu)
    m_rows = mrow_ref[...]
    mt_all = jnp.zeros((rows, LANES), F32)

    def last_of_batch(col):
        parts = [jnp.broadcast_to(col[(b + 1) * tq - 1:(b + 1) * tq], (tq, 1)) for b in range(nb)]
        return jnp.concatenate(parts, axis=0)

    dk = MLSTM_DK
    lo = _low_half((rows, LANES))
    own_q = row_batch == (lax.broadcasted_iota(jnp.int32, (rows, nb * dk), 1) >> _log2(dk))
    lane_batch = lax.broadcasted_iota(jnp.int32, (dk, rows), 1) >> shift

    def state_rows(b, h):
        return slice((b * MLSTM_HEADS + h) * dk, (b * MLSTM_HEADS + h + 1) * dk)

    for h in range(MLSTM_HEADS):
        slab = slice((h // 2) * LANES, (h // 2 + 1) * LANES)
        qs, ks = mq_ref[:, slab], mk_ref[:, slab]
        qsw, ksw = pltpu.roll(qs, dk, axis=1), pltpu.roll(ks, dk, axis=1)
        q2 = jnp.where(lo, qs, qsw) if h % 2 == 0 else jnp.where(lo, qsw, qs)
        k2 = jnp.where(lo, ks, ksw) if h % 2 == 0 else jnp.where(lo, ksw, ks)
        qb, kb = q2[:, :dk].astype(BF16), k2[:, :dk].astype(BF16)
        vb = mv_ref[:, h * MLSTM_DV:(h + 1) * MLSTM_DV].astype(BF16)
        it_c = gc[:, h:h + 1]
        bt_c = btc[:, MLSTM_HEADS + h:MLSTM_HEADS + h + 1]
        it_r = gr[h:h + 1, :]
        bt_r = btr[MLSTM_HEADS + h:MLSTM_HEADS + h + 1, :]
        m_c = m_rows[:, h:h + 1]

        num, ssum, mt, wi = _mlstm_intra(qb, kb, vb, it_c, bt_c, it_r, bt_r, m_c, valid)
        qx = jnp.where(own_q, jnp.concatenate([q2] * (nb // 2), axis=1), 0.0).astype(BF16)
        c_stack = jnp.concatenate([c_ref[state_rows(b, h), :] for b in range(nb)], axis=0)
        qc = _dot(qx, c_stack.astype(BF16))
        n_rows = jnp.concatenate(
            [jnp.broadcast_to(n_ref[b * MLSTM_HEADS + h], (tq, dk)) for b in range(nb)], axis=0)
        num = num + wi * qc
        den = ssum + wi * jnp.sum(qb.astype(F32) * n_rows, axis=-1, keepdims=True)
        og = og_ref[:, h * MLSTM_DV:(h + 1) * MLSTM_DV]
        g_head = gh_ref[:, h * MLSTM_DV:(h + 1) * MLSTM_DV]
        hm_ref[:, h * MLSTM_DV:(h + 1) * MLSTM_DV] = _mlstm_head_out(num, den, mt, og, g_head)

        b_last = last_of_batch(bt_c)
        m_new = last_of_batch(mt)
        wk = jnp.exp(b_last - bt_c + it_c - m_new)
        wc = jnp.exp(b_last + m_c - m_new)
        wkv = (wk * vb.astype(F32)).astype(BF16)
        wkk = wk * kb.astype(F32)
        kt = mkt_ref[h * dk:(h + 1) * dk, :]
        kx = jnp.concatenate([jnp.where(lane_batch == b, kt, 0.0) for b in range(nb)], axis=0)
        upd = _dot(kx.astype(BF16), wkv)
        for b in range(nb):
            idx = b * MLSTM_HEADS + h
            wc_b = wc[b * tq:b * tq + 1]
            co_ref[state_rows(b, h), :] = wc_b * c_ref[state_rows(b, h), :] + upd[b * dk:(b + 1) * dk]
            no_ref[idx] = wc_b * n_ref[idx] + jnp.sum(wkk[b * tq:(b + 1) * tq], axis=0, keepdims=True)
        mt_all = jnp.where(lane == h, mt, mt_all)
    mt_ref[...] = mt_all


def _mlstm_sample(mq, mk, mkt, mv, og, gcol, grow, m_rows, ct_in, n_in, g_head, nbatch, nb):
    t = mq.shape[0]
    tq = t // nbatch
    rows = nb * tq
    assert tq & (tq - 1) == 0 and rows == LANES and nb % 2 == 0

    def tok(width):
        return pl.BlockSpec((rows, width), lambda i: (i, 0))

    ct_spec = pl.BlockSpec((nb * MLSTM_HEADS * MLSTM_DK, MLSTM_DV), lambda i: (i, 0))
    n_spec = pl.BlockSpec((nb * MLSTM_HEADS, 1, MLSTM_DK), lambda i: (i, 0, 0))
    return pl.pallas_call(
        functools.partial(_mlstm_sample_kernel, nb=nb, tq=tq),
        grid=(nbatch // nb,),
        in_specs=[
            tok(MQK_WIDTH), tok(MQK_WIDTH),
            pl.BlockSpec((MQK_WIDTH, rows), lambda i: (0, i)),
            tok(MLSTM_WIDTH), tok(MLSTM_WIDTH), tok(LANES),
            pl.BlockSpec((GATE_ROWS, rows), lambda i: (0, i)),
            tok(LANES),
            ct_spec, n_spec,
            _const_spec((1, MLSTM_WIDTH)),
        ],
        out_specs=[tok(MLSTM_WIDTH), ct_spec, n_spec, tok(LANES)],
        out_shape=[
            jax.ShapeDtypeStruct((t, MLSTM_WIDTH), F32),
            jax.ShapeDtypeStruct(ct_in.shape, F32),
            jax.ShapeDtypeStruct(n_in.shape, F32),
            jax.ShapeDtypeStruct((t, LANES), F32),
        ],
        compiler_params=pltpu.CompilerParams(
            dimension_semantics=("arbitrary",), vmem_limit_bytes=VMEM_LIMIT),
    )(mq, mk, mkt, mv, og, gcol, grow, m_rows, ct_in, n_in, g_head)


def _mix_stage(x, att, hm, wo_ref, gcr_ref, wcq_ref):
    cat = jnp.concatenate([att.astype(BF16), hm.astype(BF16)], axis=1)
    y1 = x + _dot(cat, wo_ref[...])
    qc = _dot(_rmsnorm(y1, gcr_ref[...]).astype(BF16), wcq_ref[...])
    return y1, qc


def _cross_attention(qc, mk, mv):
    slabs = []
    for j in range(CROSS_WIDTH // LANES):
        qs = qc[:, j * LANES:(j + 1) * LANES]
        ks = mk[:, j * LANES:(j + 1) * LANES]
        vs = mv[:, j * LANES:(j + 1) * LANES]
        n = qs.shape[0]
        lo = _low_half(qs.shape)
        zero = jnp.zeros_like(qs)
        q2 = jnp.concatenate([jnp.where(lo, qs, zero), jnp.where(lo, zero, qs)], axis=0)
        s = _dot_nt(q2 * (HEAD_DIM ** -0.5), ks)
        p = jnp.exp(s - jnp.max(s, axis=-1, keepdims=True))
        o2 = _dot(p.astype(BF16), vs) / jnp.sum(p, axis=-1, keepdims=True)
        slabs.append(jnp.where(lo, o2[:n], o2[n:]))
    return jnp.concatenate(slabs, axis=1)


FFN_CHUNK = 256


def _finish(y3, gfin_ref, final):
    return _rmsnorm(y3, gfin_ref[...]) if final else y3


def _ffn_stage(y1, o, wco_ref, gf_ref, wgu_ref, wd_ref, act_ref, between=None):
    y2 = y1 + _dot(o.astype(BF16), wco_ref[...])
    xn = _rmsnorm(y2, gf_ref[...]).astype(BF16)
    for c in range(wd_ref.shape[0] // FFN_CHUNK):
        gu = _dot(xn, wgu_ref[:, 2 * c * FFN_CHUNK:2 * (c + 1) * FFN_CHUNK])
        g, u = gu[:, :FFN_CHUNK], gu[:, FFN_CHUNK:]
        act_ref[:, c * FFN_CHUNK:(c + 1) * FFN_CHUNK] = (g * jax.nn.sigmoid(g) * u).astype(BF16)
        if between is not None:
            between()
    return y2 + _dot(act_ref[...], wd_ref[...])


def _post_kernel(sink_ref, q_ref, kc_ref, kp_ref, vc_ref, vp_ref, x_ref, hm_ref, mk_ref, mv_ref,
                 xd_ref, attd_ref, hmd_ref, mkd_ref, mvd_ref,
                 wo_ref, gcr_ref, wcq_ref, wco_ref, gf_ref, wgu_ref, wd_ref, gfin_ref,
                 o_ref, od_ref, act_ref, att_ref, y3_ref, *, final, ntiles, tiles_per_seq, tq):
    tm = x_ref.shape[0]
    s = pl.program_id(0)
    tile = jnp.minimum(s, ntiles - 1)
    first_block = (tile % tiles_per_seq) == 0
    slot = s % 2

    def swa_stages():
        def store(j, value):
            att_ref[slot, j * WINDOW:(j + 1) * WINDOW, :] = value.astype(att_ref.dtype)
        return _swa_prompt_stages(sink_ref, q_ref, kc_ref, kp_ref, vc_ref, vp_ref, first_block, store)

    def finish():
        done = _finish(y3_ref[...], gfin_ref, final)
        o_ref[...] = done[:tm]
        od_ref[...] = done[tm:]

    @pl.when(s == 0)
    def _():
        y3_ref[...] = jnp.zeros_like(y3_ref)
        for _ in swa_stages():
            pass

    @pl.when(jnp.logical_and(s > 0, s <= ntiles))
    def _():
        swa = swa_stages()

        def advance(n=SWA_STAGES_PER_FFN_CHUNK):
            for _ in range(n):
                next(swa, None)

        finish()
        x = jnp.concatenate([x_ref[...], xd_ref[...]], axis=0)
        att = jnp.concatenate([att_ref[1 - slot], attd_ref[...].astype(BF16)], axis=0)
        hm = jnp.concatenate([hm_ref[...], hmd_ref[...].astype(BF16)], axis=0)
        y1, qc = _mix_stage(x, att, hm, wo_ref, gcr_ref, wcq_ref)
        advance()
        o = jnp.concatenate([
            _cross_attention(qc[:tm].astype(BF16), mk_ref[...].astype(BF16), mv_ref[...].astype(BF16)),
            _cross_attention_decode(qc[tm:], mkd_ref, mvd_ref, tq)], axis=0)
        advance()
        y3_ref[...] = _ffn_stage(y1, o, wco_ref, gf_ref, wgu_ref, wd_ref, act_ref, between=advance)
        for _ in swa:
            pass

    @pl.when(s > ntiles)
    def _():
        finish()


_MIX_WEIGHTS = ("w_out", "g_cross", "w_cq")
_FFN_WEIGHTS = ("w_co", "g_ffn", "w_gate_up", "w_down", "g_final")


def _post(x, q, k, v, hm, memk, memv, xd, attd, hmd, memkd_t, memvd_t, sinks, w, nbatch, nseq, final, tm):
    t, d = x.shape
    ntiles = t // tm
    tpb = ntiles // nbatch
    wpt = tm // WINDOW
    m = memk.shape[0] // nbatch
    hidden = w["w_down"].shape[0]
    td = xd.shape[0]
    tq = td // nseq
    slab = td // ntiles
    assert td % ntiles == 0 and slab % tq == 0 and slab % 16 == 0
    mem_d = memkd_t.shape[1]

    def attn_tile(s):
        return jnp.minimum(s, ntiles - 1)

    def post_tile(s):
        return jnp.clip(s - 1, 0, ntiles - 1)

    def done_tile(s):
        return jnp.maximum(s - 2, 0)

    def attn_rows(width):
        return pl.BlockSpec((tm, width), lambda s: (attn_tile(s), 0))

    def prev_window(width):
        return pl.BlockSpec((WINDOW, width), lambda s: (jnp.maximum(attn_tile(s) * wpt - 1, 0), 0))

    def post_rows(width):
        return pl.BlockSpec((tm, width), lambda s: (post_tile(s), 0))

    def post_slab(width):
        return pl.BlockSpec((slab, width), lambda s: (post_tile(s), 0))

    mem_spec = pl.BlockSpec((m, CROSS_WIDTH), lambda s: (post_tile(s) // tpb, 0))
    memd_spec = pl.BlockSpec((slab // tq * CROSS_WIDTH, mem_d), lambda s: (post_tile(s), 0))
    names = _MIX_WEIGHTS + _FFN_WEIGHTS
    return pl.pallas_call(
        functools.partial(_post_kernel, final=final, ntiles=ntiles, tiles_per_seq=tpb, tq=tq),
        grid=(ntiles + 2,),
        in_specs=[
            pl.BlockSpec(memory_space=pltpu.SMEM),
            attn_rows(ATT_WIDTH), attn_rows(KV_WIDTH), prev_window(KV_WIDTH),
            attn_rows(KV_WIDTH), prev_window(KV_WIDTH),
            post_rows(d), post_rows(MLSTM_WIDTH), mem_spec, mem_spec,
            post_slab(d), post_slab(ATT_WIDTH), post_slab(MLSTM_WIDTH), memd_spec, memd_spec,
        ] + _weight_specs(w, names),
        out_specs=[pl.BlockSpec((tm, d), lambda s: (done_tile(s), 0)),
                   pl.BlockSpec((slab, d), lambda s: (done_tile(s), 0))],
        out_shape=[jax.ShapeDtypeStruct((t, d), F32), jax.ShapeDtypeStruct((td, d), F32)],
        scratch_shapes=[pltpu.VMEM((tm + slab, hidden), BF16), pltpu.VMEM((2, tm, ATT_WIDTH), BF16),
                        pltpu.VMEM((tm + slab, d), F32)],
        compiler_params=pltpu.CompilerParams(
            dimension_semantics=("arbitrary",), vmem_limit_bytes=VMEM_LIMIT),
    )(sinks, q, k, k, v, v, x, hm, memk, memv, xd, attd, hmd, memkd_t, memvd_t,
      *[w[n] for n in names])


def _cross_attention_decode(qc, mk_ref, mv_ref, tq):
    n = CROSS_HEADS * tq
    row_head = lax.broadcasted_iota(jnp.int32, (n, CROSS_WIDTH), 0) >> _log2(tq)
    lane_head = lax.broadcasted_iota(jnp.int32, (n, CROSS_WIDTH), 1) >> _log2(HEAD_DIM)
    own = row_head == lane_head
    outs = []
    for b in range(qc.shape[0] // tq):
        q = qc[b * tq:(b + 1) * tq, :]
        qx = jnp.where(own, jnp.concatenate([q] * CROSS_HEADS, axis=0), 0.0).astype(BF16)
        mk = mk_ref[b * CROSS_WIDTH:(b + 1) * CROSS_WIDTH, :].astype(BF16)
        mv = mv_ref[b * CROSS_WIDTH:(b + 1) * CROSS_WIDTH, :].astype(BF16)
        s = _dot(qx, mk) * (HEAD_DIM ** -0.5)
        p = jnp.exp(s - jnp.max(s, axis=-1, keepdims=True))
        pv = _dot_nt(p.astype(BF16), mv) / jnp.sum(p, axis=-1, keepdims=True)
        pv = jnp.where(own, pv, 0.0)
        out = pv[0:tq]
        for h in range(1, CROSS_HEADS):
            out = out + pv[h * tq:(h + 1) * tq]
        outs.append(out)
    return jnp.concatenate(outs, axis=0)


def _layer_weights(l, w_in, b_igate, b_fgate, g_mlstm_head, g_mix, g_cross, g_mem, w_cq, w_ck, w_cv,
                   w_co, g_ffn, g_final):
    gate_w = w_in[l][:, MAIN_WIDTH:]
    gate_b = jnp.concatenate([b_igate[l], b_fgate[l]]).astype(F32)
    ngate = gate_b.shape[0]
    row = lambda a: a.astype(F32).reshape(1, -1)
    w_t = w_in[l].T
    k0 = ATT_WIDTH
    mk0 = ATT_WIDTH + 2 * KV_WIDTH + MQK_WIDTH
    return {
        "w_kvk_t": jnp.concatenate([w_t[k0:k0 + 2 * KV_WIDTH], w_t[mk0:mk0 + MQK_WIDTH]]).astype(BF16),
        "w_main": w_t[:MAIN_WIDTH].astype(BF16),
        "w_gate_cols": jnp.pad(gate_w, ((0, 0), (0, LANES - ngate))).astype(BF16),
        "w_gate_rows": jnp.pad(gate_w.T, ((0, GATE_ROWS - ngate), (0, 0))).astype(BF16),
        "b_gate_cols": jnp.pad(gate_b, (0, LANES - ngate)).reshape(1, LANES),
        "b_gate_rows": jnp.pad(gate_b, (0, GATE_ROWS - ngate)).reshape(GATE_ROWS, 1),
        "g_mix": row(g_mix[l]), "g_cross": row(g_cross[l]), "g_mem": row(g_mem[l]),
        "g_ffn": row(g_ffn[l]), "g_final": row(g_final), "g_head": row(g_mlstm_head[l]),
        "w_cq": w_cq[l].astype(BF16), "w_ck": w_ck[l].astype(BF16),
        "w_cv": w_cv[l].astype(BF16), "w_co": w_co[l].astype(BF16),
    }


TOKEN_TILE = 512
MLSTM_STAGES_PER_PROJ_STAGE = 15
SWA_STAGES_PER_FFN_CHUNK = 3
MLSTM_CHUNK_ROWS = 256
SWA_SAMPLE_BATCH = 16
MLSTM_SAMPLE_BATCH = 16


def kernel(x_prompt, x_sample, mem_prompt, cache_swa_k, cache_swa_v, state_mlstm_C, state_mlstm_n,
           state_mlstm_m, cache_mem_k, cache_mem_v, w_in, b_igate, b_fgate, attn_sinks,
           g_mlstm_head, w_out, g_mix, g_cross, g_mem, w_cq, w_ck, w_cv, w_co, g_ffn, w_gate,
           w_up, w_down, g_final):
    depth = w_in.shape[0]
    bp, sp, d = x_prompt.shape
    bs, ss, _ = x_sample.shape
    mem_tokens = mem_prompt.shape[1]
    past = cache_swa_k.shape[2]
    yp = x_prompt.reshape(bp * sp, d)
    ys = x_sample.reshape(bs * ss, d)
    mem = mem_prompt.reshape(bp * mem_tokens, d)
    outs = [[] for _ in range(12)]

    for l in range(depth):
        final = l == depth - 1
        w = _layer_weights(l, w_in, b_igate, b_fgate, g_mlstm_head, g_mix, g_cross, g_mem, w_cq,
                           w_ck, w_cv, w_co, g_ffn, g_final)
        sinks = attn_sinks[l].astype(F32)

        q, k, v, hm, c_p, n_p, m_p, cast, memk, memv = _mixer_prompt(
            yp, mem, w, w_gate[l].astype(F32), w_up[l].astype(F32), w_down[l].astype(F32),
            w_out[l].astype(F32), bp, MLSTM_CHUNK_ROWS)
        w.update(cast)
        win_shape = (bp, WINDOW, ATT_KV_HEADS, HEAD_DIM)
        outs[0].append(k.reshape(bp, sp, KV_WIDTH)[:, sp - WINDOW:].reshape(win_shape))
        outs[1].append(v.reshape(bp, sp, KV_WIDTH)[:, sp - WINDOW:].reshape(win_shape))
        outs[2].append(c_p)
        outs[3].append(n_p)
        outs[4].append(m_p)
        mem_shape = (bp, mem_tokens, CROSS_HEADS, HEAD_DIM)
        outs[5].append(memk.reshape(mem_shape))
        outs[6].append(memv.reshape(mem_shape))

        def tokens_last(a):
            return jnp.transpose(a.astype(F32), (0, 2, 3, 1)).reshape(-1, a.shape[1])

        qd, mq, mk, mv, og, grow, gcol, knt, vnt, mkt = _decode_proj(ys, w, TOKEN_TILE)
        att_d, kbuf_t, vbuf_t = _swa_sample(
            qd, knt, vnt, tokens_last(cache_swa_k[l]), tokens_last(cache_swa_v[l]), sinks, bs,
            SWA_SAMPLE_BATCH)
        m_rows = jnp.pad(jnp.repeat(state_mlstm_m[l].astype(F32), ss, axis=0),
                         ((0, 0), (0, LANES - MLSTM_HEADS)))
        ct_in = jnp.swapaxes(state_mlstm_C[l].astype(F32), 2, 3).reshape(-1, MLSTM_DV)
        hm_d, ct_s, n_s, mt = _mlstm_sample(
            mq, mk, mkt, mv, og, gcol, grow, m_rows, ct_in,
            state_mlstm_n[l].astype(F32).reshape(bs * MLSTM_HEADS, 1, MLSTM_DK),
            w["g_head"], bs, MLSTM_SAMPLE_BATCH)

        yp, ys = _post(yp, q, k, v, hm, memk, memv, ys, att_d, hm_d, tokens_last(cache_mem_k[l]),
                       tokens_last(cache_mem_v[l]), sinks, w, bp, bs, final, TOKEN_TILE)
        buf_t_shape = (bs, ATT_KV_HEADS, HEAD_DIM, past)
        outs[7].append(jnp.transpose(kbuf_t.reshape(buf_t_shape), (0, 3, 1, 2)))
        outs[8].append(jnp.transpose(vbuf_t.reshape(buf_t_shape), (0, 3, 1, 2)))
        outs[9].append(jnp.swapaxes(ct_s.reshape(bs, MLSTM_HEADS, MLSTM_DK, MLSTM_DV), 2, 3))
        outs[10].append(n_s.reshape(bs, MLSTM_HEADS, MLSTM_DK))
        outs[11].append(mt.reshape(bs, ss, LANES)[:, ss - 1, :MLSTM_HEADS])

    return (yp.reshape(bp, sp, d), ys.reshape(bs, ss, d)) + tuple(jnp.stack(o) for o in outs)
```

```python
import functools

import jax
import jax.numpy as jnp
from jax import lax
from jax.experimental import pallas as pl
from jax.experimental.pallas import tpu as pltpu

F32 = jnp.float32
BF16 = jnp.bfloat16

HEAD_DIM = 64
ATT_GROUP = 4
ATT_KV_HEADS = 2
ATT_WIDTH = 512
KV_WIDTH = ATT_KV_HEADS * HEAD_DIM
WINDOW = 128
MLSTM_HEADS = 4
MLSTM_DV = 128
MLSTM_DK = 64
MQK_WIDTH = MLSTM_HEADS * MLSTM_DK
MLSTM_WIDTH = MLSTM_HEADS * MLSTM_DV
MAIN_WIDTH = ATT_WIDTH + 2 * KV_WIDTH + 2 * MQK_WIDTH + 2 * MLSTM_WIDTH
CROSS_HEADS = 4
CROSS_WIDTH = CROSS_HEADS * HEAD_DIM
EPS = 1e-6
NEG_INF = float("-inf")
assert HEAD_DIM == 4 ** 3

LANES = 128
F32_SUBLANES = 8
BF16_SUBLANES = 16
GATE_ROWS = BF16_SUBLANES
VMEM_LIMIT = 56 * 1024 * 1024

NT = (((1,), (1,)), ((), ()))


def _dot(a, b):
    return jnp.dot(a, b, preferred_element_type=F32)


def _dot_nt(a, b):
    return lax.dot_general(a, b, NT, preferred_element_type=F32)


def _rmsnorm(x, g):
    return x * lax.rsqrt(jnp.mean(x * x, axis=-1, keepdims=True) + EPS) * g


def _log_sigmoid(x):
    return jnp.minimum(x, 0.0) - jnp.log1p(jnp.exp(-jnp.abs(x)))


def _split3(x):
    hi = x.astype(BF16)
    r1 = x - hi.astype(F32)
    mid = r1.astype(BF16)
    lo = (r1 - mid.astype(F32)).astype(BF16)
    return hi, mid, lo


def _cumsum_cols(tri, x):
    hi, mid, lo = _split3(x)
    return _dot(tri, hi) + _dot(tri, mid) + _dot(tri, lo)


def _cumsum_rows(x, tri):
    hi, mid, lo = _split3(x)
    return _dot(hi, tri) + _dot(mid, tri) + _dot(lo, tri)


def _log2(n):
    assert n > 0 and n & (n - 1) == 0, n
    return n.bit_length() - 1


def _low_half(shape):
    return lax.broadcasted_iota(jnp.int32, shape, 1) < HEAD_DIM


def _decode_proj_kernel(x_ref, g_ref, w_ref, wgr_ref, br_ref, wgc_ref, bc_ref, wt_ref,
                        q_ref, mq_ref, mk_ref, mv_ref, og_ref, gr_ref, gc_ref, kt_ref, vt_ref, mkt_ref):
    xn = _rmsnorm(x_ref[...], g_ref[...]).astype(BF16)
    zt = _dot_nt(wt_ref[...], xn)
    kt_ref[...] = zt[:KV_WIDTH]
    vt_ref[...] = zt[KV_WIDTH:2 * KV_WIDTH]
    mkt_ref[...] = zt[2 * KV_WIDTH:] * (MLSTM_DK ** -0.5)

    def mm(lo, hi):
        return _dot_nt(xn, w_ref[lo:hi, :])

    o = 0
    q_ref[...] = mm(o, o + ATT_WIDTH)
    o += ATT_WIDTH + 2 * KV_WIDTH
    mqk = mm(o, o + 2 * MQK_WIDTH)
    mq_ref[...] = mqk[:, :MQK_WIDTH]
    mk_ref[...] = mqk[:, MQK_WIDTH:] * (MLSTM_DK ** -0.5)
    o += 2 * MQK_WIDTH
    mv_ref[...] = mm(o, o + MLSTM_WIDTH)
    o += MLSTM_WIDTH
    og_ref[...] = mm(o, o + MLSTM_WIDTH)

    zc = _dot(xn, wgc_ref[...]) + bc_ref[...]
    lane = lax.broadcasted_iota(jnp.int32, zc.shape, 1)
    gc_ref[...] = jnp.where(lane < MLSTM_HEADS, zc, _log_sigmoid(zc))
    zr = _dot_nt(wgr_ref[...], xn) + br_ref[...]
    row = lax.broadcasted_iota(jnp.int32, zr.shape, 0)
    gr_ref[...] = jnp.where(row < MLSTM_HEADS, zr, _log_sigmoid(zr))


def _const_spec(shape):
    nd = len(shape)
    return pl.BlockSpec(shape, lambda *_: (0,) * nd, pipeline_mode=pl.Buffered(1))


def _weight_specs(w, names):
    return [_const_spec(w[n].shape) for n in names]


def _decode_proj(x, w, tm):
    t, d = x.shape

    def rows(width):
        return pl.BlockSpec((tm, width), lambda i: (i, 0))

    def lanes(height):
        return pl.BlockSpec((height, tm), lambda i: (0, i))

    outs = [(rows, ATT_WIDTH), (rows, MQK_WIDTH), (rows, MQK_WIDTH), (rows, MLSTM_WIDTH),
            (rows, MLSTM_WIDTH), (lanes, GATE_ROWS), (rows, LANES), (lanes, KV_WIDTH),
            (lanes, KV_WIDTH), (lanes, MQK_WIDTH)]
    names = ["g_mix", "w_main", "w_gate_rows", "b_gate_rows", "w_gate_cols", "b_gate_cols", "w_kvk_t"]
    return pl.pallas_call(
        _decode_proj_kernel,
        grid=(t // tm,),
        in_specs=[rows(d)] + _weight_specs(w, names),
        out_specs=[kind(size) for kind, size in outs],
        out_shape=[jax.ShapeDtypeStruct((t, size) if kind is rows else (size, t), F32)
                   for kind, size in outs],
        compiler_params=pltpu.CompilerParams(
            dimension_semantics=("arbitrary",), vmem_limit_bytes=VMEM_LIMIT),
    )(x, *[w[n] for n in names])


def _dup_heads(x):
    swapped = pltpu.roll(x, HEAD_DIM, axis=1)
    lo = _low_half(x.shape)
    return (jnp.where(lo, x, swapped).astype(BF16), jnp.where(lo, swapped, x).astype(BF16))


def _stack_query_heads(q, h):
    parts = []
    for g in range(ATT_GROUP):
        hq = h * ATT_GROUP + g
        slab = q[:, (hq // 2) * LANES:(hq // 2 + 1) * LANES]
        lo = _low_half(slab.shape)
        keep = lo if hq % 2 == 0 else jnp.logical_not(lo)
        parts.append(jnp.where(keep, slab, jnp.zeros_like(slab)))
    return jnp.concatenate(parts, axis=0).astype(BF16)


def _sink_column(sink_ref, h, rows_per_head):
    n = ATT_GROUP * rows_per_head
    grp = lax.broadcasted_iota(jnp.int32, (n, 1), 0) >> _log2(rows_per_head)
    col = jnp.full((n, 1), sink_ref[h * ATT_GROUP], F32)
    for g in range(1, ATT_GROUP):
        col = jnp.where(grp == g, sink_ref[h * ATT_GROUP + g], col)
    return col


def _swa_prompt_stages(sink_ref, q_ref, kc_ref, kp_ref, vc_ref, vp_ref, first_block, store):
    qb = q_ref.shape[0]
    k_all = jnp.concatenate([kp_ref[...], kc_ref[...]], axis=0)
    v_all = jnp.concatenate([vp_ref[...], vc_ref[...]], axis=0)
    kd = _dup_heads(k_all)
    v_t = v_all.T.astype(BF16)

    n = ATT_GROUP * WINDOW
    key = lax.broadcasted_iota(jnp.int32, (2 * WINDOW, n), 0)
    qry = lax.broadcasted_iota(jnp.int32, (2 * WINDOW, n), 1) & (WINDOW - 1)
    delta = key - qry
    in_band = (delta >= 0) & (delta <= WINDOW)
    grp = lax.broadcasted_iota(jnp.int32, (1, n), 1) >> _log2(WINDOW)
    sinks = []
    for h in range(ATT_KV_HEADS):
        row = jnp.full((1, n), sink_ref[h * ATT_GROUP], F32)
        for g in range(1, ATT_GROUP):
            row = jnp.where(grp == g, sink_ref[h * ATT_GROUP + g], row)
        sinks.append(row)
    yield

    probs = []
    for j in range(qb // WINDOW):
        valid = in_band & (key >= jnp.where(first_block, WINDOW, 0)) if j == 0 else in_band
        q = q_ref[j * WINDOW:(j + 1) * WINDOW, :] * (HEAD_DIM ** -0.5)
        band = slice(j * WINDOW, (j + 2) * WINDOW)
        for h in range(ATT_KV_HEADS):
            qh = _stack_query_heads(q, h)
            s = _dot_nt(kd[h][band], qh)
            probs.append(dict(h=h, band=band, s=jnp.where(valid, s, NEG_INF)))
            yield
    for p in probs:
        p["m"] = jnp.maximum(jnp.max(p["s"], axis=0, keepdims=True), sinks[p["h"]])
        yield
    for p in probs:
        p["p"] = jnp.exp(p["s"] - p["m"])
        yield
    for p in probs:
        h = p["h"]
        denom = jnp.sum(p["p"], axis=0, keepdims=True) + jnp.exp(sinks[h] - p["m"])
        p["o_t"] = _dot(v_t[h * HEAD_DIM:(h + 1) * HEAD_DIM, p["band"]], p["p"].astype(BF16)) / denom
        yield
    for j in range(qb // WINDOW):
        pieces = [p["o_t"][:, g * WINDOW:(g + 1) * WINDOW]
                  for p in probs[j * ATT_KV_HEADS:(j + 1) * ATT_KV_HEADS] for g in range(ATT_GROUP)]
        store(j, jnp.concatenate(pieces, axis=0).T)
        yield


def _swa_sample_kernel(sink_ref, q_ref, knt_ref, vnt_ref, ck_ref, cv_ref,
                       o_ref, kbuf_ref, vbuf_ref, *, nb, tq):
    past = ck_ref.shape[1]
    hw = ATT_KV_HEADS * HEAD_DIM
    heads = ATT_KV_HEADS * ATT_GROUP
    n = heads * tq
    fresh = past - tq
    t = lax.broadcasted_iota(jnp.int32, (n, 2 * past), 0) & (tq - 1)
    col = lax.broadcasted_iota(jnp.int32, (n, 2 * past), 1)
    valid = ((col < past) & (col >= t)) | ((col >= past + fresh) & (col - (past + fresh) <= t))
    sink = jnp.concatenate([_sink_column(sink_ref, h, tq) for h in range(ATT_KV_HEADS)], axis=0)
    lane = lax.broadcasted_iota(jnp.int32, (hw, past), 1)
    knt = knt_ref[...]
    vnt = vnt_ref[...]
    zero_slab = jnp.zeros((tq, LANES), F32)
    lo = _low_half((tq, LANES))

    def shifted(old, new_t, b):
        return jnp.where(lane >= fresh, pltpu.roll(new_t, (fresh - b * tq) % past, axis=1),
                         pltpu.roll(old, fresh, axis=1))

    def dup_rows(x):
        h0, h1 = x[:HEAD_DIM], x[HEAD_DIM:]
        return jnp.concatenate([h0, h0, h1, h1], axis=0)

    for b in range(nb):
        k_old = ck_ref[b * hw:(b + 1) * hw, :]
        v_old = cv_ref[b * hw:(b + 1) * hw, :]
        k_new = shifted(k_old, knt, b)
        v_new = shifted(v_old, vnt, b)
        kbuf_ref[b * hw:(b + 1) * hw, :] = k_new
        vbuf_ref[b * hw:(b + 1) * hw, :] = v_new
        kop = jnp.concatenate([dup_rows(k_old), dup_rows(k_new)], axis=1).astype(BF16)
        vop = jnp.concatenate([dup_rows(v_old), dup_rows(v_new)], axis=1).astype(BF16)

        q = q_ref[b * tq:(b + 1) * tq, :]
        blocks = []
        for hq in range(heads):
            slab = q[:, (hq // 2) * LANES:(hq // 2 + 1) * LANES]
            slab = jnp.where(lo if hq % 2 == 0 else jnp.logical_not(lo), slab, 0.0)
            pair = [slab, zero_slab] if hq // ATT_GROUP == 0 else [zero_slab, slab]
            blocks.append(jnp.concatenate(pair, axis=1))
        qx = jnp.concatenate(blocks, axis=0).astype(BF16)

        s = _dot(qx, kop) * (HEAD_DIM ** -0.5)
        s = jnp.where(valid, s, NEG_INF)
        m = jnp.maximum(jnp.max(s, axis=-1, keepdims=True), sink)
        p = jnp.exp(s - m)
        denom = jnp.sum(p, axis=-1, keepdims=True) + jnp.exp(sink - m)
        o = _dot_nt(p.astype(BF16), vop) / denom

        slabs = []
        for j in range(heads // 2):
            h = (2 * j) // ATT_GROUP
            even = o[(2 * j) * tq:(2 * j + 1) * tq, h * LANES:(h + 1) * LANES]
            odd = o[(2 * j + 1) * tq:(2 * j + 2) * tq, h * LANES:(h + 1) * LANES]
            slabs.append(jnp.where(lo, even, odd))
        o_ref[b * tq:(b + 1) * tq, :] = jnp.concatenate(slabs, axis=1)


def _swa_sample(q, knt, vnt, cache_kt, cache_vt, sinks, nbatch, nb):
    t = q.shape[0]
    tq = t // nbatch
    past = cache_kt.shape[1]
    hw = ATT_KV_HEADS * HEAD_DIM
    assert past == WINDOW and tq & (tq - 1) == 0 and nb * tq == past

    def rows(r, width):
        return pl.BlockSpec((r, width), lambda i: (i, 0))

    new_t = pl.BlockSpec((hw, nb * tq), lambda i: (0, i))
    return pl.pallas_call(
        functools.partial(_swa_sample_kernel, nb=nb, tq=tq),
        grid=(nbatch // nb,),
        in_specs=[
            pl.BlockSpec(memory_space=pltpu.SMEM),
            rows(nb * tq, ATT_WIDTH), new_t, new_t, rows(nb * hw, past), rows(nb * hw, past),
        ],
        out_specs=[rows(nb * tq, ATT_WIDTH), rows(nb * hw, past), rows(nb * hw, past)],
        out_shape=[
            jax.ShapeDtypeStruct((t, ATT_WIDTH), F32),
            jax.ShapeDtypeStruct(cache_kt.shape, F32),
            jax.ShapeDtypeStruct(cache_vt.shape, F32),
        ],
        compiler_params=pltpu.CompilerParams(
            dimension_semantics=("arbitrary",), vmem_limit_bytes=VMEM_LIMIT),
    )(sinks, q, knt, vnt, cache_kt, cache_vt)


def _mlstm_intra(q, k, v, it_c, bt_c, it_r, bt_r, m_c, valid):
    d = jnp.where(valid, bt_c - bt_r + it_r, NEG_INF)
    inter = bt_c + m_c
    mt = jnp.maximum(jnp.max(d, axis=-1, keepdims=True), inter)
    sm = _dot_nt(q, k) * jnp.exp(d - mt)
    wi = jnp.exp(inter - mt)
    return _dot(sm.astype(BF16), v), jnp.sum(sm, axis=-1, keepdims=True), mt, wi


def _mlstm_head_out(num, den, mt, og, g_head):
    lower = jnp.maximum(jnp.abs(den), jnp.exp(-mt))
    h = num / lower
    hn = h * lax.rsqrt(jnp.mean(h * h, axis=-1, keepdims=True) + EPS)
    return hn * g_head * jax.nn.sigmoid(og)


def _mlstm_prompt_stages(mq, mk, mv, og, gr, gh_ref, hm_ref, c_ref, n_ref, m_ref, nbatch, chunk):
    s_idx = lax.broadcasted_iota(jnp.int32, (chunk, chunk), 0)
    l_idx = lax.broadcasted_iota(jnp.int32, (chunk, chunk), 1)
    causal = s_idx <= l_idx
    triu = jnp.where(causal, 1.0, 0.0).astype(BF16)
    lo = _low_half((chunk, LANES))
    pad_rows = jnp.zeros((LANES - MLSTM_HEADS, chunk), F32)
    sub8 = (GATE_ROWS, chunk)

    probs = []
    for b in range(nbatch):
        g = gr(b)
        btr = _cumsum_rows(g, triu)
        it_rows = g[:MLSTM_HEADS]
        bt_rows = btr[MLSTM_HEADS:2 * MLSTM_HEADS]
        u_cols = jnp.concatenate([it_rows - bt_rows, pad_rows], axis=0).T
        for h in range(MLSTM_HEADS):
            slab = slice((h // 2) * LANES, (h // 2 + 1) * LANES)
            keep = lo if h % 2 == 0 else jnp.logical_not(lo)
            q = mq(b, slab)
            k = mk(b, slab)
            probs.append(dict(
                b=b, h=h, idx=b * MLSTM_HEADS + h, cols=slice(h * MLSTM_DV, (h + 1) * MLSTM_DV),
                q=jnp.where(keep, q, jnp.zeros_like(q)), k=jnp.where(keep, k, jnp.zeros_like(k)),
                u_c=u_cols[:, h:h + 1], it_r=it_rows[h:h + 1, :], bt_r=bt_rows[h:h + 1, :]))
        yield

    for p in probs:
        p["m_old"] = m_ref[p["idx"]][:, 0:1]
        p["d"] = jnp.where(causal, p["bt_r"] + p["u_c"], NEG_INF)
        p["inter"] = p["bt_r"] + p["m_old"]
        p["q_t"] = p["q"].astype(F32).T.astype(BF16)
        p["kq"] = _dot(p["k"], p["q_t"])
        p["v_t"] = mv(p["b"], p["cols"]).astype(F32).T
        yield
    for p in probs:
        p["mt"] = jnp.maximum(jnp.max(p["d"], axis=0, keepdims=True), p["inter"])
        yield
    for p in probs:
        p["sm"] = p["kq"] * jnp.exp(p["d"] - p["mt"])
        p["wi"] = jnp.exp(p["inter"] - p["mt"])
        yield
    for p in probs:
        c_old = c_ref[p["idx"]]
        n_old = jnp.broadcast_to(n_ref[p["idx"]], (sub8[0], LANES))
        num = (_dot(p["v_t"].astype(BF16), p["sm"].astype(BF16))
               + p["wi"] * _dot(c_old.astype(BF16), p["q_t"]))
        den = (jnp.sum(p["sm"], axis=0, keepdims=True)
               + p["wi"] * _dot(n_old.astype(BF16), p["q_t"])[0:1])
        lower = jnp.maximum(jnp.abs(den), jnp.exp(-p["mt"]))
        h_t = num / lower
        p["hn"] = (h_t * lax.rsqrt(jnp.mean(h_t * h_t, axis=0, keepdims=True) + EPS)).T
        yield
    for p in probs:
        gate = jax.nn.sigmoid(og(p["b"], p["cols"]).astype(F32))
        hm_ref[p["b"], :, p["cols"]] = (p["hn"] * gh_ref[:, p["cols"]] * gate).astype(hm_ref.dtype)
        yield
    for p in probs:
        idx = p["idx"]
        b_last = p["bt_r"][:, chunk - 1:chunk]
        m_new = p["mt"][:, chunk - 1:chunk]
        wk = jnp.exp(b_last - p["bt_r"] + p["it_r"] - m_new)
        wc = jnp.exp(b_last + p["m_old"] - m_new)
        c_ref[idx] = wc * c_ref[idx] + _dot((p["v_t"] * wk).astype(BF16), p["k"])
        n_ref[idx] = wc * n_ref[idx] + _dot(jnp.broadcast_to(wk, sub8).astype(BF16), p["k"])[0:1]
        m_ref[idx] = jnp.broadcast_to(m_new, (1, LANES))
        yield


def _mixer_prompt_kernel(x_ref, g_ref, w_ref, wgr_ref, br_ref, gh_ref, gmem_ref, wck_ref, wcv_ref,
                         wg_ref, wu_ref, wd_ref, wo_ref, mem_ref,
                         q_ref, k_ref, v_ref, hm_ref, c_ref, n_ref, m_ref, wgu_out, wd_out, wo_out,
                         memk_out, memv_out,
                         mq_scr, mk_scr, mv_scr, og_scr, gr_scr, *, nbatch, chunk):
    s = pl.program_id(0)
    wr = s % 2
    rd = 1 - wr
    rows = nbatch * chunk

    def side_work():
        for c in range(0, wg_ref.shape[1], FFN_CHUNK):
            wgu_out[:, 2 * c:2 * c + FFN_CHUNK] = wg_ref[:, c:c + FFN_CHUNK].astype(BF16)
            wgu_out[:, 2 * c + FFN_CHUNK:2 * (c + FFN_CHUNK)] = wu_ref[:, c:c + FFN_CHUNK].astype(BF16)
        wd_out[...] = wd_ref[...].astype(BF16)
        wo_out[...] = wo_ref[...].astype(BF16)
        mn = _rmsnorm(mem_ref[...], gmem_ref[...]).astype(BF16)
        memk_out[...] = _dot(mn, wck_ref[...])
        memv_out[...] = _dot(mn, wcv_ref[...])

    def proj_stages():
        xn = _rmsnorm(x_ref[...].reshape(rows, x_ref.shape[-1]), g_ref[...]).astype(BF16)
        yield

        def mm(lo, hi):
            return _dot_nt(xn, w_ref[lo:hi, :])

        o = 0
        q_ref[...] = mm(o, o + ATT_WIDTH).astype(q_ref.dtype).reshape(q_ref.shape)
        yield
        o += ATT_WIDTH
        kv = mm(o, o + 2 * KV_WIDTH)
        k_ref[...] = kv[:, :KV_WIDTH].reshape(k_ref.shape)
        v_ref[...] = kv[:, KV_WIDTH:].reshape(v_ref.shape)
        yield
        o += 2 * KV_WIDTH
        mqk = mm(o, o + 2 * MQK_WIDTH)
        mq_scr[wr] = mqk[:, :MQK_WIDTH].astype(BF16)
        mk_scr[wr] = (mqk[:, MQK_WIDTH:] * (MLSTM_DK ** -0.5)).astype(BF16)
        yield
        o += 2 * MQK_WIDTH
        mv_scr[wr] = mm(o, o + MLSTM_WIDTH).astype(BF16)
        yield
        o += MLSTM_WIDTH
        og_scr[wr] = mm(o, o + MLSTM_WIDTH).astype(BF16)
        yield
        zr = _dot_nt(wgr_ref[...], xn) + br_ref[...]
        row = lax.broadcasted_iota(jnp.int32, zr.shape, 0)
        gr_scr[wr] = jnp.where(row < MLSTM_HEADS, zr, _log_sigmoid(zr))
        yield

    def mlstm_stages():
        def rows_of(scr):
            return lambda b, lanes: scr[rd, b * chunk:(b + 1) * chunk, lanes]
        return _mlstm_prompt_stages(
            rows_of(mq_scr), rows_of(mk_scr), rows_of(mv_scr), rows_of(og_scr),
            lambda b: gr_scr[rd, :, b * chunk:(b + 1) * chunk],
            gh_ref, hm_ref, c_ref, n_ref, m_ref, nbatch, chunk)

    @pl.when(s == 0)
    def _():
        c_ref[...] = jnp.zeros_like(c_ref)
        n_ref[...] = jnp.zeros_like(n_ref)
        m_ref[...] = jnp.zeros_like(m_ref)
        side_work()
        for _ in proj_stages():
            pass

    @pl.when(s > 0)
    def _():
        side_work()
        mlstm = mlstm_stages()
        for _ in proj_stages():
            for _ in range(MLSTM_STAGES_PER_PROJ_STAGE):
                next(mlstm, None)
        for _ in mlstm:
            pass


def _mixer_prompt(x, mem, w, w_gate, w_up, w_down, w_out, nbatch, chunk):
    t, d = x.shape
    seq = t // nbatch
    nchunks = seq // chunk
    nstate = nbatch * MLSTM_HEADS
    rows = nbatch * chunk
    hidden = w_down.shape[0]
    assert hidden % FFN_CHUNK == 0 and mem.shape[0] % (nchunks * F32_SUBLANES) == 0
    assert d % (nchunks * BF16_SUBLANES) == 0 and hidden % (nchunks * BF16_SUBLANES) == 0

    def proj_chunk(width):
        return pl.BlockSpec((nbatch, chunk, width), lambda s: (0, jnp.minimum(s, nchunks - 1), 0))

    def state(shape):
        return pl.BlockSpec(shape, lambda s: (0, 0, 0))

    def weight_rows(a, width=None):
        return pl.BlockSpec((a.shape[0] // nchunks, width or a.shape[1]),
                            lambda s: (jnp.minimum(s, nchunks - 1), 0))

    names = ["g_mix", "w_main", "w_gate_rows", "b_gate_rows", "g_head", "g_mem", "w_ck", "w_cv"]
    q, k, v, hm, c_st, n_st, m_st, w_gate_up_bf, w_down_bf, w_out_bf, memk, memv = pl.pallas_call(
        functools.partial(_mixer_prompt_kernel, nbatch=nbatch, chunk=chunk),
        grid=(nchunks + 1,),
        in_specs=[proj_chunk(d)] + _weight_specs(w, names) + [
            weight_rows(w_gate), weight_rows(w_up), weight_rows(w_down), weight_rows(w_out),
            weight_rows(mem),
        ],
        out_specs=[
            proj_chunk(ATT_WIDTH), proj_chunk(KV_WIDTH), proj_chunk(KV_WIDTH),
            pl.BlockSpec((nbatch, chunk, MLSTM_WIDTH), lambda s: (0, jnp.maximum(s - 1, 0), 0)),
            state((nstate, MLSTM_DV, LANES)), state((nstate, 1, LANES)), state((nstate, 1, LANES)),
            weight_rows(w_gate, 2 * hidden), weight_rows(w_down), weight_rows(w_out),
            weight_rows(mem, CROSS_WIDTH), weight_rows(mem, CROSS_WIDTH),
        ],
        out_shape=[
            jax.ShapeDtypeStruct((nbatch, seq, ATT_WIDTH), BF16),
            jax.ShapeDtypeStruct((nbatch, seq, KV_WIDTH), F32),
            jax.ShapeDtypeStruct((nbatch, seq, KV_WIDTH), F32),
            jax.ShapeDtypeStruct((nbatch, seq, MLSTM_WIDTH), BF16),
            jax.ShapeDtypeStruct((nstate, MLSTM_DV, LANES), F32),
            jax.ShapeDtypeStruct((nstate, 1, LANES), F32),
            jax.ShapeDtypeStruct((nstate, 1, LANES), F32),
            jax.ShapeDtypeStruct((d, 2 * hidden), BF16),
            jax.ShapeDtypeStruct((hidden, d), BF16),
            jax.ShapeDtypeStruct(w_out.shape, BF16),
            jax.ShapeDtypeStruct((mem.shape[0], CROSS_WIDTH), F32),
            jax.ShapeDtypeStruct((mem.shape[0], CROSS_WIDTH), F32),
        ],
        scratch_shapes=[
            pltpu.VMEM((2, rows, MQK_WIDTH), BF16), pltpu.VMEM((2, rows, MQK_WIDTH), BF16),
            pltpu.VMEM((2, rows, MLSTM_WIDTH), BF16), pltpu.VMEM((2, rows, MLSTM_WIDTH), BF16),
            pltpu.VMEM((2, GATE_ROWS, rows), F32),
        ],
        compiler_params=pltpu.CompilerParams(
            dimension_semantics=("arbitrary",), vmem_limit_bytes=VMEM_LIMIT),
    )(x.reshape(nbatch, seq, d), *[w[n] for n in names], w_gate, w_up, w_down, w_out, mem)
    q = q.reshape(t, ATT_WIDTH)
    k = k.reshape(t, KV_WIDTH)
    v = v.reshape(t, KV_WIDTH)

    c_st = c_st.reshape(nbatch, MLSTM_HEADS, MLSTM_DV, 2, MLSTM_DK)
    n_st = n_st.reshape(nbatch, MLSTM_HEADS, 2, MLSTM_DK)
    c_fin = jnp.stack([c_st[:, h, :, h % 2, :] for h in range(MLSTM_HEADS)], axis=1)
    n_fin = jnp.stack([n_st[:, h, h % 2, :] for h in range(MLSTM_HEADS)], axis=1)
    m_fin = m_st[:, 0, 0].reshape(nbatch, MLSTM_HEADS)
    bf16_weights = {"w_gate_up": w_gate_up_bf, "w_down": w_down_bf, "w_out": w_out_bf}
    return q, k, v, hm.reshape(t, MLSTM_WIDTH), c_fin, n_fin, m_fin, bf16_weights, memk, memv


def _mlstm_sample_kernel(mq_ref, mk_ref, mkt_ref, mv_ref, og_ref, gc_ref, gr_ref, mrow_ref, c_ref,
                         n_ref, gh_ref, hm_ref, co_ref, no_ref, mt_ref, *, nb, tq):
    rows = nb * tq
    shift = tq.bit_length() - 1
    r = lax.broadcasted_iota(jnp.int32, (rows, rows), 0)
    s = lax.broadcasted_iota(jnp.int32, (rows, rows), 1)
    same = (r >> shift) == (s >> shift)
    valid = same & (s <= r)
    tril = jnp.where(valid, 1.0, 0.0).astype(BF16)
    triu = jnp.where(same & (r <= s), 1.0, 0.0).astype(BF16)
    row_batch = lax.broadcasted_iota(jnp.int32, (rows, 1), 0) >> shift
    lane = lax.broadcasted_iota(jnp.int32, (rows, LANES), 1)

    gc = gc_ref[...]
    gr = gr_ref[...]
    btc = _cumsum_cols(tril, gc)
    btr = _cumsum_rows(gr, triu)
    m_rows = mrow_ref[...]
    mt_all = jnp.zeros((rows, LANES), F32)

    def last_of_batch(col):
        parts = [jnp.broadcast_to(col[(b + 1) * tq - 1:(b + 1) * tq], (tq, 1)) for b in range(nb)]
        return jnp.concatenate(parts, axis=0)

    dk = MLSTM_DK
    lo = _low_half((rows, LANES))
    own_q = row_batch == (lax.broadcasted_iota(jnp.int32, (rows, nb * dk), 1) >> _log2(dk))
    lane_batch = lax.broadcasted_iota(jnp.int32, (dk, rows), 1) >> shift

    def state_rows(b, h):
        return slice((b * MLSTM_HEADS + h) * dk, (b * MLSTM_HEADS + h + 1) * dk)

    for h in range(MLSTM_HEADS):
        slab = slice((h // 2) * LANES, (h // 2 + 1) * LANES)
        qs, ks = mq_ref[:, slab], mk_ref[:, slab]
        qsw, ksw = pltpu.roll(qs, dk, axis=1), pltpu.roll(ks, dk, axis=1)
        q2 = jnp.where(lo, qs, qsw) if h % 2 == 0 else jnp.where(lo, qsw, qs)
        k2 = jnp.where(lo, ks, ksw) if h % 2 == 0 else jnp.where(lo, ksw, ks)
        qb, kb = q2[:, :dk].astype(BF16), k2[:, :dk].astype(BF16)
        vb = mv_ref[:, h * MLSTM_DV:(h + 1) * MLSTM_DV].astype(BF16)
        it_c = gc[:, h:h + 1]
        bt_c = btc[:, MLSTM_HEADS + h:MLSTM_HEADS + h + 1]
        it_r = gr[h:h + 1, :]
        bt_r = btr[MLSTM_HEADS + h:MLSTM_HEADS + h + 1, :]
        m_c = m_rows[:, h:h + 1]

        num, ssum, mt, wi = _mlstm_intra(qb, kb, vb, it_c, bt_c, it_r, bt_r, m_c, valid)
        qx = jnp.where(own_q, jnp.concatenate([q2] * (nb // 2), axis=1), 0.0).astype(BF16)
        c_stack = jnp.concatenate([c_ref[state_rows(b, h), :] for b in range(nb)], axis=0)
        qc = _dot(qx, c_stack.astype(BF16))
        n_rows = jnp.concatenate(
            [jnp.broadcast_to(n_ref[b * MLSTM_HEADS + h], (tq, dk)) for b in range(nb)], axis=0)
        num = num + wi * qc
        den = ssum + wi * jnp.sum(qb.astype(F32) * n_rows, axis=-1, keepdims=True)
        og = og_ref[:, h * MLSTM_DV:(h + 1) * MLSTM_DV]
        g_head = gh_ref[:, h * MLSTM_DV:(h + 1) * MLSTM_DV]
        hm_ref[:, h * MLSTM_DV:(h + 1) * MLSTM_DV] = _mlstm_head_out(num, den, mt, og, g_head)

        b_last = last_of_batch(bt_c)
        m_new = last_of_batch(mt)
        wk = jnp.exp(b_last - bt_c + it_c - m_new)
        wc = jnp.exp(b_last + m_c - m_new)
        wkv = (wk * vb.astype(F32)).astype(BF16)
        wkk = wk * kb.astype(F32)
        kt = mkt_ref[h * dk:(h + 1) * dk, :]
        kx = jnp.concatenate([jnp.where(lane_batch == b, kt, 0.0) for b in range(nb)], axis=0)
        upd = _dot(kx.astype(BF16), wkv)
        for b in range(nb):
            idx = b * MLSTM_HEADS + h
            wc_b = wc[b * tq:b * tq + 1]
            co_ref[state_rows(b, h), :] = wc_b * c_ref[state_rows(b, h), :] + upd[b * dk:(b + 1) * dk]
            no_ref[idx] = wc_b * n_ref[idx] + jnp.sum(wkk[b * tq:(b + 1) * tq], axis=0, keepdims=True)
        mt_all = jnp.where(lane == h, mt, mt_all)
    mt_ref[...] = mt_all


def _mlstm_sample(mq, mk, mkt, mv, og, gcol, grow, m_rows, ct_in, n_in, g_head, nbatch, nb):
    t = mq.shape[0]
    tq = t // nbatch
    rows = nb * tq
    assert tq & (tq - 1) == 0 and rows == LANES and nb % 2 == 0

    def tok(width):
        return pl.BlockSpec((rows, width), lambda i: (i, 0))

    ct_spec = pl.BlockSpec((nb * MLSTM_HEADS * MLSTM_DK, MLSTM_DV), lambda i: (i, 0))
    n_spec = pl.BlockSpec((nb * MLSTM_HEADS, 1, MLSTM_DK), lambda i: (i, 0, 0))
    return pl.pallas_call(
        functools.partial(_mlstm_sample_kernel, nb=nb, tq=tq),
        grid=(nbatch // nb,),
        in_specs=[
            tok(MQK_WIDTH), tok(MQK_WIDTH),
            pl.BlockSpec((MQK_WIDTH, rows), lambda i: (0, i)),
            tok(MLSTM_WIDTH), tok(MLSTM_WIDTH), tok(LANES),
            pl.BlockSpec((GATE_ROWS, rows), lambda i: (0, i)),
            tok(LANES),
            ct_spec, n_spec,
            _const_spec((1, MLSTM_WIDTH)),
        ],
        out_specs=[tok(MLSTM_WIDTH), ct_spec, n_spec, tok(LANES)],
        out_shape=[
            jax.ShapeDtypeStruct((t, MLSTM_WIDTH), F32),
            jax.ShapeDtypeStruct(ct_in.shape, F32),
            jax.ShapeDtypeStruct(n_in.shape, F32),
            jax.ShapeDtypeStruct((t, LANES), F32),
        ],
        compiler_params=pltpu.CompilerParams(
            dimension_semantics=("arbitrary",), vmem_limit_bytes=VMEM_LIMIT),
    )(mq, mk, mkt, mv, og, gcol, grow, m_rows, ct_in, n_in, g_head)


def _mix_stage(x, att, hm, wo_ref, gcr_ref, wcq_ref):
    cat = jnp.concatenate([att.astype(BF16), hm.astype(BF16)], axis=1)
    y1 = x + _dot(cat, wo_ref[...])
    qc = _dot(_rmsnorm(y1, gcr_ref[...]).astype(BF16), wcq_ref[...])
    return y1, qc


def _cross_attention(qc, mk, mv):
    slabs = []
    for j in range(CROSS_WIDTH // LANES):
        qs = qc[:, j * LANES:(j + 1) * LANES]
        ks = mk[:, j * LANES:(j + 1) * LANES]
        vs = mv[:, j * LANES:(j + 1) * LANES]
        n = qs.shape[0]
        lo = _low_half(qs.shape)
        zero = jnp.zeros_like(qs)
        q2 = jnp.concatenate([jnp.where(lo, qs, zero), jnp.where(lo, zero, qs)], axis=0)
        s = _dot_nt(q2 * (HEAD_DIM ** -0.5), ks)
        p = jnp.exp(s - jnp.max(s, axis=-1, keepdims=True))
        o2 = _dot(p.astype(BF16), vs) / jnp.sum(p, axis=-1, keepdims=True)
        slabs.append(jnp.where(lo, o2[:n], o2[n:]))
    return jnp.concatenate(slabs, axis=1)


FFN_CHUNK = 256


def _finish(y3, gfin_ref, final):
    return _rmsnorm(y3, gfin_ref[...]) if final else y3


def _ffn_stage(y1, o, wco_ref, gf_ref, wgu_ref, wd_ref, act_ref, between=None):
    y2 = y1 + _dot(o.astype(BF16), wco_ref[...])
    xn = _rmsnorm(y2, gf_ref[...]).astype(BF16)
    for c in range(wd_ref.shape[0] // FFN_CHUNK):
        gu = _dot(xn, wgu_ref[:, 2 * c * FFN_CHUNK:2 * (c + 1) * FFN_CHUNK])
        g, u = gu[:, :FFN_CHUNK], gu[:, FFN_CHUNK:]
        act_ref[:, c * FFN_CHUNK:(c + 1) * FFN_CHUNK] = (g * jax.nn.sigmoid(g) * u).astype(BF16)
        if between is not None:
            between()
    return y2 + _dot(act_ref[...], wd_ref[...])


def _post_kernel(sink_ref, q_ref, kc_ref, kp_ref, vc_ref, vp_ref, x_ref, hm_ref, mk_ref, mv_ref,
                 xd_ref, attd_ref, hmd_ref, mkd_ref, mvd_ref,
                 wo_ref, gcr_ref, wcq_ref, wco_ref, gf_ref, wgu_ref, wd_ref, gfin_ref,
                 o_ref, od_ref, act_ref, att_ref, y3_ref, *, final, ntiles, tiles_per_seq, tq):
    tm = x_ref.shape[0]
    s = pl.program_id(0)
    tile = jnp.minimum(s, ntiles - 1)
    first_block = (tile % tiles_per_seq) == 0
    slot = s % 2

    def swa_stages():
        def store(j, value):
            att_ref[slot, j * WINDOW:(j + 1) * WINDOW, :] = value.astype(att_ref.dtype)
        return _swa_prompt_stages(sink_ref, q_ref, kc_ref, kp_ref, vc_ref, vp_ref, first_block, store)

    def finish():
        done = _finish(y3_ref[...], gfin_ref, final)
        o_ref[...] = done[:tm]
        od_ref[...] = done[tm:]

    @pl.when(s == 0)
    def _():
        y3_ref[...] = jnp.zeros_like(y3_ref)
        for _ in swa_stages():
            pass

    @pl.when(jnp.logical_and(s > 0, s <= ntiles))
    def _():
        swa = swa_stages()

        def advance(n=SWA_STAGES_PER_FFN_CHUNK):
            for _ in range(n):
                next(swa, None)

        finish()
        x = jnp.concatenate([x_ref[...], xd_ref[...]], axis=0)
        att = jnp.concatenate([att_ref[1 - slot], attd_ref[...].astype(BF16)], axis=0)
        hm = jnp.concatenate([hm_ref[...], hmd_ref[...].astype(BF16)], axis=0)
        y1, qc = _mix_stage(x, att, hm, wo_ref, gcr_ref, wcq_ref)
        advance()
        o = jnp.concatenate([
            _cross_attention(qc[:tm].astype(BF16), mk_ref[...].astype(BF16), mv_ref[...].astype(BF16)),
            _cross_attention_decode(qc[tm:], mkd_ref, mvd_ref, tq)], axis=0)
        advance()
        y3_ref[...] = _ffn_stage(y1, o, wco_ref, gf_ref, wgu_ref, wd_ref, act_ref, between=advance)
        for _ in swa:
            pass

    @pl.when(s > ntiles)
    def _():
        finish()


_MIX_WEIGHTS = ("w_out", "g_cross", "w_cq")
_FFN_WEIGHTS = ("w_co", "g_ffn", "w_gate_up", "w_down", "g_final")


def _post(x, q, k, v, hm, memk, memv, xd, attd, hmd, memkd_t, memvd_t, sinks, w, nbatch, nseq, final, tm):
    t, d = x.shape
    ntiles = t // tm
    tpb = ntiles // nbatch
    wpt = tm // WINDOW
    m = memk.shape[0] // nbatch
    hidden = w["w_down"].shape[0]
    td = xd.shape[0]
    tq = td // nseq
    slab = td // ntiles
    assert td % ntiles == 0 and slab % tq == 0 and slab % BF16_SUBLANES == 0
    mem_d = memkd_t.shape[1]

    def attn_tile(s):
        return jnp.minimum(s, ntiles - 1)

    def post_tile(s):
        return jnp.clip(s - 1, 0, ntiles - 1)

    def done_tile(s):
        return jnp.maximum(s - 2, 0)

    def attn_rows(width):
        return pl.BlockSpec((tm, width), lambda s: (attn_tile(s), 0))

    def prev_window(width):
        return pl.BlockSpec((WINDOW, width), lambda s: (jnp.maximum(attn_tile(s) * wpt - 1, 0), 0))

    def post_rows(width):
        return pl.BlockSpec((tm, width), lambda s: (post_tile(s), 0))

    def post_slab(width):
        return pl.BlockSpec((slab, width), lambda s: (post_tile(s), 0))

    mem_spec = pl.BlockSpec((m, CROSS_WIDTH), lambda s: (post_tile(s) // tpb, 0))
    memd_spec = pl.BlockSpec((slab // tq * CROSS_WIDTH, mem_d), lambda s: (post_tile(s), 0))
    names = _MIX_WEIGHTS + _FFN_WEIGHTS
    return pl.pallas_call(
        functools.partial(_post_kernel, final=final, ntiles=ntiles, tiles_per_seq=tpb, tq=tq),
        grid=(ntiles + 2,),
        in_specs=[
            pl.BlockSpec(memory_space=pltpu.SMEM),
            attn_rows(ATT_WIDTH), attn_rows(KV_WIDTH), prev_window(KV_WIDTH),
            attn_rows(KV_WIDTH), prev_window(KV_WIDTH),
            post_rows(d), post_rows(MLSTM_WIDTH), mem_spec, mem_spec,
            post_slab(d), post_slab(ATT_WIDTH), post_slab(MLSTM_WIDTH), memd_spec, memd_spec,
        ] + _weight_specs(w, names),
        out_specs=[pl.BlockSpec((tm, d), lambda s: (done_tile(s), 0)),
                   pl.BlockSpec((slab, d), lambda s: (done_tile(s), 0))],
        out_shape=[jax.ShapeDtypeStruct((t, d), F32), jax.ShapeDtypeStruct((td, d), F32)],
        scratch_shapes=[pltpu.VMEM((tm + slab, hidden), BF16), pltpu.VMEM((2, tm, ATT_WIDTH), BF16),
                        pltpu.VMEM((tm + slab, d), F32)],
        compiler_params=pltpu.CompilerParams(
            dimension_semantics=("arbitrary",), vmem_limit_bytes=VMEM_LIMIT),
    )(sinks, q, k, k, v, v, x, hm, memk, memv, xd, attd, hmd, memkd_t, memvd_t,
      *[w[n] for n in names])


def _cross_attention_decode(qc, mk_ref, mv_ref, tq):
    n = CROSS_HEADS * tq
    row_head = lax.broadcasted_iota(jnp.int32, (n, CROSS_WIDTH), 0) >> _log2(tq)
    lane_head = lax.broadcasted_iota(jnp.int32, (n, CROSS_WIDTH), 1) >> _log2(HEAD_DIM)
    own = row_head == lane_head
    outs = []
    for b in range(qc.shape[0] // tq):
        q = qc[b * tq:(b + 1) * tq, :]
        qx = jnp.where(own, jnp.concatenate([q] * CROSS_HEADS, axis=0), 0.0).astype(BF16)
        mk = mk_ref[b * CROSS_WIDTH:(b + 1) * CROSS_WIDTH, :].astype(BF16)
        mv = mv_ref[b * CROSS_WIDTH:(b + 1) * CROSS_WIDTH, :].astype(BF16)
        s = _dot(qx, mk) * (HEAD_DIM ** -0.5)
        p = jnp.exp(s - jnp.max(s, axis=-1, keepdims=True))
        pv = _dot_nt(p.astype(BF16), mv) / jnp.sum(p, axis=-1, keepdims=True)
        pv = jnp.where(own, pv, 0.0)
        out = pv[0:tq]
        for h in range(1, CROSS_HEADS):
            out = out + pv[h * tq:(h + 1) * tq]
        outs.append(out)
    return jnp.concatenate(outs, axis=0)


def _layer_weights(l, w_in, b_igate, b_fgate, g_mlstm_head, g_mix, g_cross, g_mem, w_cq, w_ck, w_cv,
                   w_co, g_ffn, g_final):
    gate_w = w_in[l][:, MAIN_WIDTH:]
    gate_b = jnp.concatenate([b_igate[l], b_fgate[l]]).astype(F32)
    ngate = gate_b.shape[0]
    row = lambda a: a.astype(F32).reshape(1, -1)
    w_t = w_in[l].T
    k0 = ATT_WIDTH
    mk0 = ATT_WIDTH + 2 * KV_WIDTH + MQK_WIDTH
    return {
        "w_kvk_t": jnp.concatenate([w_t[k0:k0 + 2 * KV_WIDTH], w_t[mk0:mk0 + MQK_WIDTH]]).astype(BF16),
        "w_main": w_t[:MAIN_WIDTH].astype(BF16),
        "w_gate_cols": jnp.pad(gate_w, ((0, 0), (0, LANES - ngate))).astype(BF16),
        "w_gate_rows": jnp.pad(gate_w.T, ((0, GATE_ROWS - ngate), (0, 0))).astype(BF16),
        "b_gate_cols": jnp.pad(gate_b, (0, LANES - ngate)).reshape(1, LANES),
        "b_gate_rows": jnp.pad(gate_b, (0, GATE_ROWS - ngate)).reshape(GATE_ROWS, 1),
        "g_mix": row(g_mix[l]), "g_cross": row(g_cross[l]), "g_mem": row(g_mem[l]),
        "g_ffn": row(g_ffn[l]), "g_final": row(g_final), "g_head": row(g_mlstm_head[l]),
        "w_cq": w_cq[l].astype(BF16), "w_ck": w_ck[l].astype(BF16),
        "w_cv": w_cv[l].astype(BF16), "w_co": w_co[l].astype(BF16),
    }


TOKEN_TILE = 512
MLSTM_STAGES_PER_PROJ_STAGE = 15
SWA_STAGES_PER_FFN_CHUNK = 3
MLSTM_CHUNK_ROWS = 256
SWA_SAMPLE_BATCH = 16
MLSTM_SAMPLE_BATCH = 16


def kernel(x_prompt, x_sample, mem_prompt, cache_swa_k, cache_swa_v, state_mlstm_C, state_mlstm_n,
           state_mlstm_m, cache_mem_k, cache_mem_v, w_in, b_igate, b_fgate, attn_sinks,
           g_mlstm_head, w_out, g_mix, g_cross, g_mem, w_cq, w_ck, w_cv, w_co, g_ffn, w_gate,
           w_up, w_down, g_final):
    depth = w_in.shape[0]
    bp, sp, d = x_prompt.shape
    bs, ss, _ = x_sample.shape
    mem_tokens = mem_prompt.shape[1]
    past = cache_swa_k.shape[2]
    yp = x_prompt.reshape(bp * sp, d)
    ys = x_sample.reshape(bs * ss, d)
    mem = mem_prompt.reshape(bp * mem_tokens, d)
    outs = [[] for _ in range(12)]

    for l in range(depth):
        final = l == depth - 1
        w = _layer_weights(l, w_in, b_igate, b_fgate, g_mlstm_head, g_mix, g_cross, g_mem, w_cq,
                           w_ck, w_cv, w_co, g_ffn, g_final)
        sinks = attn_sinks[l].astype(F32)

        q, k, v, hm, c_p, n_p, m_p, cast, memk, memv = _mixer_prompt(
            yp, mem, w, w_gate[l].astype(F32), w_up[l].astype(F32), w_down[l].astype(F32),
            w_out[l].astype(F32), bp, MLSTM_CHUNK_ROWS)
        w.update(cast)
        win_shape = (bp, WINDOW, ATT_KV_HEADS, HEAD_DIM)
        outs[0].append(k.reshape(bp, sp, KV_WIDTH)[:, sp - WINDOW:].reshape(win_shape))
        outs[1].append(v.reshape(bp, sp, KV_WIDTH)[:, sp - WINDOW:].reshape(win_shape))
        outs[2].append(c_p)
        outs[3].append(n_p)
        outs[4].append(m_p)
        mem_shape = (bp, mem_tokens, CROSS_HEADS, HEAD_DIM)
        outs[5].append(memk.reshape(mem_shape))
        outs[6].append(memv.reshape(mem_shape))

        def tokens_last(a):
            return jnp.transpose(a.astype(F32), (0, 2, 3, 1)).reshape(-1, a.shape[1])

        qd, mq, mk, mv, og, grow, gcol, knt, vnt, mkt = _decode_proj(ys, w, TOKEN_TILE)
        att_d, kbuf_t, vbuf_t = _swa_sample(
            qd, knt, vnt, tokens_last(cache_swa_k[l]), tokens_last(cache_swa_v[l]), sinks, bs,
            SWA_SAMPLE_BATCH)
        m_rows = jnp.pad(jnp.repeat(state_mlstm_m[l].astype(F32), ss, axis=0),
                         ((0, 0), (0, LANES - MLSTM_HEADS)))
        ct_in = jnp.swapaxes(state_mlstm_C[l].astype(F32), 2, 3).reshape(-1, MLSTM_DV)
        hm_d, ct_s, n_s, mt = _mlstm_sample(
            mq, mk, mkt, mv, og, gcol, grow, m_rows, ct_in,
            state_mlstm_n[l].astype(F32).reshape(bs * MLSTM_HEADS, 1, MLSTM_DK),
            w["g_head"], bs, MLSTM_SAMPLE_BATCH)

        yp, ys = _post(yp, q, k, v, hm, memk, memv, ys, att_d, hm_d, tokens_last(cache_mem_k[l]),
                       tokens_last(cache_mem_v[l]), sinks, w, bp, bs, final, TOKEN_TILE)
        buf_t_shape = (bs, ATT_KV_HEADS, HEAD_DIM, past)
        outs[7].append(jnp.transpose(kbuf_t.reshape(buf_t_shape), (0, 3, 1, 2)))
        outs[8].append(jnp.transpose(vbuf_t.reshape(buf_t_shape), (0, 3, 1, 2)))
        outs[9].append(jnp.swapaxes(ct_s.reshape(bs, MLSTM_HEADS, MLSTM_DK, MLSTM_DV), 2, 3))
        outs[10].append(n_s.reshape(bs, MLSTM_HEADS, MLSTM_DK))
        outs[11].append(mt.reshape(bs, ss, LANES)[:, ss - 1, :MLSTM_HEADS])

    return (yp.reshape(bp, sp, d), ys.reshape(bs, ss, d)) + tuple(jnp.stack(o) for o in outs)
```

```python
import functools

import jax
import jax.numpy as jnp
from jax import lax
from jax.experimental import pallas as pl
from jax.experimental.pallas import tpu as pltpu

F32 = jnp.float32
BF16 = jnp.bfloat16

HEAD_DIM = 64
ATT_GROUP = 4
ATT_KV_HEADS = 2
ATT_WIDTH = 512
KV_WIDTH = ATT_KV_HEADS * HEAD_DIM
WINDOW = 128
MLSTM_HEADS = 4
MLSTM_DV = 128
MLSTM_DK = 64
MQK_WIDTH = MLSTM_HEADS * MLSTM_DK
MLSTM_WIDTH = MLSTM_HEADS * MLSTM_DV
MAIN_WIDTH = ATT_WIDTH + 2 * KV_WIDTH + 2 * MQK_WIDTH + 2 * MLSTM_WIDTH
CROSS_HEADS = 4
CROSS_WIDTH = CROSS_HEADS * HEAD_DIM
EPS = 1e-6
NEG_INF = float("-inf")
assert HEAD_DIM == 4 ** 3

LANES = 128
F32_SUBLANES = 8
BF16_SUBLANES = 16
GATE_ROWS = BF16_SUBLANES
VMEM_LIMIT = 56 * 1024 * 1024

NT = (((1,), (1,)), ((), ()))


def _dot(a, b):
    return jnp.dot(a, b, preferred_element_type=F32)


def _dot_nt(a, b):
    return lax.dot_general(a, b, NT, preferred_element_type=F32)


def _rmsnorm(x, g):
    return x * lax.rsqrt(jnp.mean(x * x, axis=-1, keepdims=True) + EPS) * g


def _log_sigmoid(x):
    return jnp.minimum(x, 0.0) - jnp.log1p(jnp.exp(-jnp.abs(x)))


def _split3(x):
    hi = x.astype(BF16)
    r1 = x - hi.astype(F32)
    mid = r1.astype(BF16)
    lo = (r1 - mid.astype(F32)).astype(BF16)
    return hi, mid, lo


def _cumsum_cols(tri, x):
    hi, mid, lo = _split3(x)
    return _dot(tri, hi) + _dot(tri, mid) + _dot(tri, lo)


def _cumsum_rows(x, tri):
    hi, mid, lo = _split3(x)
    return _dot(hi, tri) + _dot(mid, tri) + _dot(lo, tri)


def _log2(n):
    assert n > 0 and n & (n - 1) == 0, n
    return n.bit_length() - 1


def _low_half(shape):
    return lax.broadcasted_iota(jnp.int32, shape, 1) < HEAD_DIM


def _decode_proj_kernel(x_ref, g_ref, w_ref, wgr_ref, br_ref, wgc_ref, bc_ref, wt_ref,
                        q_ref, mq_ref, mk_ref, mv_ref, og_ref, gr_ref, gc_ref, kt_ref, vt_ref, mkt_ref):
    xn = _rmsnorm(x_ref[...], g_ref[...]).astype(BF16)
    zt = _dot_nt(wt_ref[...], xn)
    kt_ref[...] = zt[:KV_WIDTH]
    vt_ref[...] = zt[KV_WIDTH:2 * KV_WIDTH]
    mkt_ref[...] = zt[2 * KV_WIDTH:] * (MLSTM_DK ** -0.5)

    def mm(lo, hi):
        return _dot_nt(xn, w_ref[lo:hi, :])

    o = 0
    q_ref[...] = mm(o, o + ATT_WIDTH)
    o += ATT_WIDTH + 2 * KV_WIDTH
    mqk = mm(o, o + 2 * MQK_WIDTH)
    mq_ref[...] = mqk[:, :MQK_WIDTH]
    mk_ref[...] = mqk[:, MQK_WIDTH:] * (MLSTM_DK ** -0.5)
    o += 2 * MQK_WIDTH
    mv_ref[...] = mm(o, o + MLSTM_WIDTH)
    o += MLSTM_WIDTH
    og_ref[...] = mm(o, o + MLSTM_WIDTH)

    zc = _dot(xn, wgc_ref[...]) + bc_ref[...]
    lane = lax.broadcasted_iota(jnp.int32, zc.shape, 1)
    gc_ref[...] = jnp.where(lane < MLSTM_HEADS, zc, _log_sigmoid(zc))
    zr = _dot_nt(wgr_ref[...], xn) + br_ref[...]
    row = lax.broadcasted_iota(jnp.int32, zr.shape, 0)
    gr_ref[...] = jnp.where(row < MLSTM_HEADS, zr, _log_sigmoid(zr))


def _const_spec(shape):
    nd = len(shape)
    return pl.BlockSpec(shape, lambda *_: (0,) * nd, pipeline_mode=pl.Buffered(1))


def _weight_specs(w, names):
    return [_const_spec(w[n].shape) for n in names]


def _decode_proj(x, w, tm):
    t, d = x.shape

    def rows(width):
        return pl.BlockSpec((tm, width), lambda i: (i, 0))

    def lanes(height):
        return pl.BlockSpec((height, tm), lambda i: (0, i))

    outs = [(rows, ATT_WIDTH), (rows, MQK_WIDTH), (rows, MQK_WIDTH), (rows, MLSTM_WIDTH),
            (rows, MLSTM_WIDTH), (lanes, GATE_ROWS), (rows, LANES), (lanes, KV_WIDTH),
            (lanes, KV_WIDTH), (lanes, MQK_WIDTH)]
    names = ["g_mix", "w_main", "w_gate_rows", "b_gate_rows", "w_gate_cols", "b_gate_cols", "w_kvk_t"]
    return pl.pallas_call(
        _decode_proj_kernel,
        grid=(t // tm,),
        in_specs=[rows(d)] + _weight_specs(w, names),
        out_specs=[kind(size) for kind, size in outs],
        out_shape=[jax.ShapeDtypeStruct((t, size) if kind is rows else (size, t), F32)
                   for kind, size in outs],
        compiler_params=pltpu.CompilerParams(
            dimension_semantics=("arbitrary",), vmem_limit_bytes=VMEM_LIMIT,
            allow_input_fusion=[False] + [True] * len(names)),
    )(x, *[w[n] for n in names])


def _dup_heads(x):
    swapped = pltpu.roll(x, HEAD_DIM, axis=1)
    lo = _low_half(x.shape)
    return (jnp.where(lo, x, swapped).astype(BF16), jnp.where(lo, swapped, x).astype(BF16))


def _stack_query_heads(q, h):
    parts = []
    for g in range(ATT_GROUP):
        hq = h * ATT_GROUP + g
        slab = q[:, (hq // 2) * LANES:(hq // 2 + 1) * LANES]
        lo = _low_half(slab.shape)
        keep = lo if hq % 2 == 0 else jnp.logical_not(lo)
        parts.append(jnp.where(keep, slab, jnp.zeros_like(slab)))
    return jnp.concatenate(parts, axis=0).astype(BF16)


def _sink_column(sink_ref, h, rows_per_head):
    n = ATT_GROUP * rows_per_head
    grp = lax.broadcasted_iota(jnp.int32, (n, 1), 0) >> _log2(rows_per_head)
    col = jnp.full((n, 1), sink_ref[h * ATT_GROUP], F32)
    for g in range(1, ATT_GROUP):
        col = jnp.where(grp == g, sink_ref[h * ATT_GROUP + g], col)
    return col


def _swa_prompt_stages(sink_ref, q_ref, kc_ref, kp_ref, vc_ref, vp_ref, first_block, store):
    qb = q_ref.shape[0]
    k_all = jnp.concatenate([kp_ref[...], kc_ref[...]], axis=0)
    v_all = jnp.concatenate([vp_ref[...], vc_ref[...]], axis=0)
    kd = _dup_heads(k_all)
    v_t = v_all.T.astype(BF16)

    n = ATT_GROUP * WINDOW
    key = lax.broadcasted_iota(jnp.int32, (2 * WINDOW, n), 0)
    qry = lax.broadcasted_iota(jnp.int32, (2 * WINDOW, n), 1) & (WINDOW - 1)
    delta = key - qry
    in_band = (delta >= 0) & (delta <= WINDOW)
    grp = lax.broadcasted_iota(jnp.int32, (1, n), 1) >> _log2(WINDOW)
    sinks = []
    for h in range(ATT_KV_HEADS):
        row = jnp.full((1, n), sink_ref[h * ATT_GROUP], F32)
        for g in range(1, ATT_GROUP):
            row = jnp.where(grp == g, sink_ref[h * ATT_GROUP + g], row)
        sinks.append(row)
    yield

    probs = []
    for j in range(qb // WINDOW):
        valid = in_band & (key >= jnp.where(first_block, WINDOW, 0)) if j == 0 else in_band
        q = q_ref[j * WINDOW:(j + 1) * WINDOW, :] * (HEAD_DIM ** -0.5)
        band = slice(j * WINDOW, (j + 2) * WINDOW)
        for h in range(ATT_KV_HEADS):
            qh = _stack_query_heads(q, h)
            s = _dot_nt(kd[h][band], qh)
            probs.append(dict(h=h, band=band, s=jnp.where(valid, s, NEG_INF)))
            yield
    for p in probs:
        p["m"] = jnp.maximum(jnp.max(p["s"], axis=0, keepdims=True), sinks[p["h"]])
        yield
    for p in probs:
        p["p"] = jnp.exp(p["s"] - p["m"])
        yield
    for p in probs:
        h = p["h"]
        denom = jnp.sum(p["p"], axis=0, keepdims=True) + jnp.exp(sinks[h] - p["m"])
        p["o_t"] = _dot(v_t[h * HEAD_DIM:(h + 1) * HEAD_DIM, p["band"]], p["p"].astype(BF16)) / denom
        yield
    for j in range(qb // WINDOW):
        pieces = [p["o_t"][:, g * WINDOW:(g + 1) * WINDOW]
                  for p in probs[j * ATT_KV_HEADS:(j + 1) * ATT_KV_HEADS] for g in range(ATT_GROUP)]
        store(j, jnp.concatenate(pieces, axis=0).T)
        yield


def _swa_sample_kernel(sink_ref, q_ref, knt_ref, vnt_ref, ck_ref, cv_ref,
                       o_ref, kbuf_ref, vbuf_ref, *, nb, tq):
    past = ck_ref.shape[1]
    hw = ATT_KV_HEADS * HEAD_DIM
    heads = ATT_KV_HEADS * ATT_GROUP
    n = heads * tq
    fresh = past - tq
    t = lax.broadcasted_iota(jnp.int32, (n, 2 * past), 0) & (tq - 1)
    col = lax.broadcasted_iota(jnp.int32, (n, 2 * past), 1)
    valid = ((col < past) & (col >= t)) | ((col >= past + fresh) & (col - (past + fresh) <= t))
    sink = jnp.concatenate([_sink_column(sink_ref, h, tq) for h in range(ATT_KV_HEADS)], axis=0)
    lane = lax.broadcasted_iota(jnp.int32, (hw, past), 1)
    knt = knt_ref[...]
    vnt = vnt_ref[...]
    zero_slab = jnp.zeros((tq, LANES), F32)
    lo = _low_half((tq, LANES))

    def shifted(old, new_t, b):
        return jnp.where(lane >= fresh, pltpu.roll(new_t, (fresh - b * tq) % past, axis=1),
                         pltpu.roll(old, fresh, axis=1))

    def dup_rows(x):
        h0, h1 = x[:HEAD_DIM], x[HEAD_DIM:]
        return jnp.concatenate([h0, h0, h1, h1], axis=0)

    for b in range(nb):
        k_old = ck_ref[b * hw:(b + 1) * hw, :]
        v_old = cv_ref[b * hw:(b + 1) * hw, :]
        k_new = shifted(k_old, knt, b)
        v_new = shifted(v_old, vnt, b)
        kbuf_ref[b * hw:(b + 1) * hw, :] = k_new
        vbuf_ref[b * hw:(b + 1) * hw, :] = v_new
        kop = jnp.concatenate([dup_rows(k_old), dup_rows(k_new)], axis=1).astype(BF16)
        vop = jnp.concatenate([dup_rows(v_old), dup_rows(v_new)], axis=1).astype(BF16)

        q = q_ref[b * tq:(b + 1) * tq, :]
        blocks = []
        for hq in range(heads):
            slab = q[:, (hq // 2) * LANES:(hq // 2 + 1) * LANES]
            slab = jnp.where(lo if hq % 2 == 0 else jnp.logical_not(lo), slab, 0.0)
            pair = [slab, zero_slab] if hq // ATT_GROUP == 0 else [zero_slab, slab]
            blocks.append(jnp.concatenate(pair, axis=1))
        qx = jnp.concatenate(blocks, axis=0).astype(BF16)

        s = _dot(qx, kop) * (HEAD_DIM ** -0.5)
        s = jnp.where(valid, s, NEG_INF)
        m = jnp.maximum(jnp.max(s, axis=-1, keepdims=True), sink)
        p = jnp.exp(s - m)
        denom = jnp.sum(p, axis=-1, keepdims=True) + jnp.exp(sink - m)
        o = _dot_nt(p.astype(BF16), vop) / denom

        slabs = []
        for j in range(heads // 2):
            h = (2 * j) // ATT_GROUP
            even = o[(2 * j) * tq:(2 * j + 1) * tq, h * LANES:(h + 1) * LANES]
            odd = o[(2 * j + 1) * tq:(2 * j + 2) * tq, h * LANES:(h + 1) * LANES]
            slabs.append(jnp.where(lo, even, odd))
        o_ref[b * tq:(b + 1) * tq, :] = jnp.concatenate(slabs, axis=1)


def _swa_sample(q, knt, vnt, cache_kt, cache_vt, sinks, nbatch, nb):
    t = q.shape[0]
    tq = t // nbatch
    past = cache_kt.shape[1]
    hw = ATT_KV_HEADS * HEAD_DIM
    assert past == WINDOW and tq & (tq - 1) == 0 and nb * tq == past

    def rows(r, width):
        return pl.BlockSpec((r, width), lambda i: (i, 0))

    new_t = pl.BlockSpec((hw, nb * tq), lambda i: (0, i))
    return pl.pallas_call(
        functools.partial(_swa_sample_kernel, nb=nb, tq=tq),
        grid=(nbatch // nb,),
        in_specs=[
            pl.BlockSpec(memory_space=pltpu.SMEM),
            rows(nb * tq, ATT_WIDTH), new_t, new_t, rows(nb * hw, past), rows(nb * hw, past),
        ],
        out_specs=[rows(nb * tq, ATT_WIDTH), rows(nb * hw, past), rows(nb * hw, past)],
        out_shape=[
            jax.ShapeDtypeStruct((t, ATT_WIDTH), F32),
            jax.ShapeDtypeStruct(cache_kt.shape, F32),
            jax.ShapeDtypeStruct(cache_vt.shape, F32),
        ],
        compiler_params=pltpu.CompilerParams(
            dimension_semantics=("arbitrary",), vmem_limit_bytes=VMEM_LIMIT),
    )(sinks, q, knt, vnt, cache_kt, cache_vt)


def _mlstm_intra(q, k, v, it_c, bt_c, it_r, bt_r, m_c, valid):
    d = jnp.where(valid, bt_c - bt_r + it_r, NEG_INF)
    inter = bt_c + m_c
    mt = jnp.maximum(jnp.max(d, axis=-1, keepdims=True), inter)
    sm = _dot_nt(q, k) * jnp.exp(d - mt)
    wi = jnp.exp(inter - mt)
    return _dot(sm.astype(BF16), v), jnp.sum(sm, axis=-1, keepdims=True), mt, wi


def _mlstm_head_out(num, den, mt, og, g_head):
    lower = jnp.maximum(jnp.abs(den), jnp.exp(-mt))
    h = num / lower
    hn = h * lax.rsqrt(jnp.mean(h * h, axis=-1, keepdims=True) + EPS)
    return hn * g_head * jax.nn.sigmoid(og)


def _mlstm_prompt_stages(mq, mk, mv, og, gr, gh_ref, hm_ref, c_ref, n_ref, m_ref, nbatch, chunk):
    s_idx = lax.broadcasted_iota(jnp.int32, (chunk, chunk), 0)
    l_idx = lax.broadcasted_iota(jnp.int32, (chunk, chunk), 1)
    causal = s_idx <= l_idx
    triu = jnp.where(causal, 1.0, 0.0).astype(BF16)
    lo = _low_half((chunk, LANES))
    pad_rows = jnp.zeros((LANES - MLSTM_HEADS, chunk), F32)
    sub8 = (GATE_ROWS, chunk)

    probs = []
    for b in range(nbatch):
        g = gr(b)
        btr = _cumsum_rows(g, triu)
        it_rows = g[:MLSTM_HEADS]
        bt_rows = btr[MLSTM_HEADS:2 * MLSTM_HEADS]
        u_cols = jnp.concatenate([it_rows - bt_rows, pad_rows], axis=0).T
        for h in range(MLSTM_HEADS):
            slab = slice((h // 2) * LANES, (h // 2 + 1) * LANES)
            keep = lo if h % 2 == 0 else jnp.logical_not(lo)
            q = mq(b, slab)
            k = mk(b, slab)
            probs.append(dict(
                b=b, h=h, idx=b * MLSTM_HEADS + h, cols=slice(h * MLSTM_DV, (h + 1) * MLSTM_DV),
                q=jnp.where(keep, q, jnp.zeros_like(q)), k=jnp.where(keep, k, jnp.zeros_like(k)),
                u_c=u_cols[:, h:h + 1], it_r=it_rows[h:h + 1, :], bt_r=bt_rows[h:h + 1, :]))
        yield

    for p in probs:
        p["m_old"] = m_ref[p["idx"]][:, 0:1]
        p["d"] = jnp.where(causal, p["bt_r"] + p["u_c"], NEG_INF)
        p["inter"] = p["bt_r"] + p["m_old"]
        p["q_t"] = p["q"].astype(F32).T.astype(BF16)
        p["kq"] = _dot(p["k"], p["q_t"])
        p["v_t"] = mv(p["b"], p["cols"]).astype(F32).T
        yield
    for p in probs:
        p["mt"] = jnp.maximum(jnp.max(p["d"], axis=0, keepdims=True), p["inter"])
        yield
    for p in probs:
        p["sm"] = p["kq"] * jnp.exp(p["d"] - p["mt"])
        p["wi"] = jnp.exp(p["inter"] - p["mt"])
        yield
    for p in probs:
        c_old = c_ref[p["idx"]]
        n_old = jnp.broadcast_to(n_ref[p["idx"]], (sub8[0], LANES))
        num = (_dot(p["v_t"].astype(BF16), p["sm"].astype(BF16))
               + p["wi"] * _dot(c_old.astype(BF16), p["q_t"]))
        den = (jnp.sum(p["sm"], axis=0, keepdims=True)
               + p["wi"] * _dot(n_old.astype(BF16), p["q_t"])[0:1])
        lower = jnp.maximum(jnp.abs(den), jnp.exp(-p["mt"]))
        h_t = num / lower
        p["hn"] = (h_t * lax.rsqrt(jnp.mean(h_t * h_t, axis=0, keepdims=True) + EPS)).T
        yield
    for p in probs:
        gate = jax.nn.sigmoid(og(p["b"], p["cols"]).astype(F32))
        hm_ref[p["b"], :, p["cols"]] = (p["hn"] * gh_ref[:, p["cols"]] * gate).astype(hm_ref.dtype)
        yield
    for p in probs:
        idx = p["idx"]
        b_last = p["bt_r"][:, chunk - 1:chunk]
        m_new = p["mt"][:, chunk - 1:chunk]
        wk = jnp.exp(b_last - p["bt_r"] + p["it_r"] - m_new)
        wc = jnp.exp(b_last + p["m_old"] - m_new)
        c_ref[idx] = wc * c_ref[idx] + _dot((p["v_t"] * wk).astype(BF16), p["k"])
        n_ref[idx] = wc * n_ref[idx] + _dot(jnp.broadcast_to(wk, sub8).astype(BF16), p["k"])[0:1]
        m_ref[idx] = jnp.broadcast_to(m_new, (1, LANES))
        yield


def _mixer_prompt_kernel(x_ref, g_ref, w_ref, wgr_ref, br_ref, gh_ref, gmem_ref, wck_ref, wcv_ref,
                         wg_ref, wu_ref, wd_ref, wo_ref, mem_ref,
                         q_ref, k_ref, v_ref, hm_ref, c_ref, n_ref, m_ref, wgu_out, wd_out, wo_out,
                         memk_out, memv_out,
                         mq_scr, mk_scr, mv_scr, og_scr, gr_scr, *, nbatch, chunk):
    s = pl.program_id(0)
    wr = s % 2
    rd = 1 - wr
    rows = nbatch * chunk

    def side_work():
        for c in range(0, wg_ref.shape[1], FFN_CHUNK):
            wgu_out[:, 2 * c:2 * c + FFN_CHUNK] = wg_ref[:, c:c + FFN_CHUNK].astype(BF16)
            wgu_out[:, 2 * c + FFN_CHUNK:2 * (c + FFN_CHUNK)] = wu_ref[:, c:c + FFN_CHUNK].astype(BF16)
        wd_out[...] = wd_ref[...].astype(BF16)
        wo_out[...] = wo_ref[...].astype(BF16)
        mn = _rmsnorm(mem_ref[...], gmem_ref[...]).astype(BF16)
        memk_out[...] = _dot(mn, wck_ref[...])
        memv_out[...] = _dot(mn, wcv_ref[...])

    def proj_stages():
        xn = _rmsnorm(x_ref[...].reshape(rows, x_ref.shape[-1]), g_ref[...]).astype(BF16)
        yield

        def mm(lo, hi):
            return _dot_nt(xn, w_ref[lo:hi, :])

        o = 0
        q_ref[...] = mm(o, o + ATT_WIDTH).astype(q_ref.dtype).reshape(q_ref.shape)
        yield
        o += ATT_WIDTH
        kv = mm(o, o + 2 * KV_WIDTH)
        k_ref[...] = kv[:, :KV_WIDTH].reshape(k_ref.shape)
        v_ref[...] = kv[:, KV_WIDTH:].reshape(v_ref.shape)
        yield
        o += 2 * KV_WIDTH
        mqk = mm(o, o + 2 * MQK_WIDTH)
        mq_scr[wr] = mqk[:, :MQK_WIDTH].astype(BF16)
        mk_scr[wr] = (mqk[:, MQK_WIDTH:] * (MLSTM_DK ** -0.5)).astype(BF16)
        yield
        o += 2 * MQK_WIDTH
        mv_scr[wr] = mm(o, o + MLSTM_WIDTH).astype(BF16)
        yield
        o += MLSTM_WIDTH
        og_scr[wr] = mm(o, o + MLSTM_WIDTH).astype(BF16)
        yield
        zr = _dot_nt(wgr_ref[...], xn) + br_ref[...]
        row = lax.broadcasted_iota(jnp.int32, zr.shape, 0)
        gr_scr[wr] = jnp.where(row < MLSTM_HEADS, zr, _log_sigmoid(zr))
        yield

    def mlstm_stages():
        def rows_of(scr):
            return lambda b, lanes: scr[rd, b * chunk:(b + 1) * chunk, lanes]
        return _mlstm_prompt_stages(
            rows_of(mq_scr), rows_of(mk_scr), rows_of(mv_scr), rows_of(og_scr),
            lambda b: gr_scr[rd, :, b * chunk:(b + 1) * chunk],
            gh_ref, hm_ref, c_ref, n_ref, m_ref, nbatch, chunk)

    @pl.when(s == 0)
    def _():
        c_ref[...] = jnp.zeros_like(c_ref)
        n_ref[...] = jnp.zeros_like(n_ref)
        m_ref[...] = jnp.zeros_like(m_ref)
        side_work()
        for _ in proj_stages():
            pass

    @pl.when(s > 0)
    def _():
        side_work()
        mlstm = mlstm_stages()
        for _ in proj_stages():
            for _ in range(MLSTM_STAGES_PER_PROJ_STAGE):
                next(mlstm, None)
        for _ in mlstm:
            pass


def _mixer_prompt(x, mem, w, w_gate, w_up, w_down, w_out, nbatch, chunk):
    t, d = x.shape
    seq = t // nbatch
    nchunks = seq // chunk
    nstate = nbatch * MLSTM_HEADS
    rows = nbatch * chunk
    hidden = w_down.shape[0]
    assert hidden % FFN_CHUNK == 0 and mem.shape[0] % (nchunks * F32_SUBLANES) == 0
    assert d % (nchunks * BF16_SUBLANES) == 0 and hidden % (nchunks * BF16_SUBLANES) == 0

    def proj_chunk(width):
        return pl.BlockSpec((nbatch, chunk, width), lambda s: (0, jnp.minimum(s, nchunks - 1), 0))

    def state(shape):
        return pl.BlockSpec(shape, lambda s: (0, 0, 0))

    def weight_rows(a, width=None):
        return pl.BlockSpec((a.shape[0] // nchunks, width or a.shape[1]),
                            lambda s: (jnp.minimum(s, nchunks - 1), 0))

    names = ["g_mix", "w_main", "w_gate_rows", "b_gate_rows", "g_head", "g_mem", "w_ck", "w_cv"]
    q, k, v, hm, c_st, n_st, m_st, w_gate_up_bf, w_down_bf, w_out_bf, memk, memv = pl.pallas_call(
        functools.partial(_mixer_prompt_kernel, nbatch=nbatch, chunk=chunk),
        grid=(nchunks + 1,),
        in_specs=[proj_chunk(d)] + _weight_specs(w, names) + [
            weight_rows(w_gate), weight_rows(w_up), weight_rows(w_down), weight_rows(w_out),
            weight_rows(mem),
        ],
        out_specs=[
            proj_chunk(ATT_WIDTH), proj_chunk(KV_WIDTH), proj_chunk(KV_WIDTH),
            pl.BlockSpec((nbatch, chunk, MLSTM_WIDTH), lambda s: (0, jnp.maximum(s - 1, 0), 0)),
            state((nstate, MLSTM_DV, LANES)), state((nstate, 1, LANES)), state((nstate, 1, LANES)),
            weight_rows(w_gate, 2 * hidden), weight_rows(w_down), weight_rows(w_out),
            weight_rows(mem, CROSS_WIDTH), weight_rows(mem, CROSS_WIDTH),
        ],
        out_shape=[
            jax.ShapeDtypeStruct((nbatch, seq, ATT_WIDTH), BF16),
            jax.ShapeDtypeStruct((nbatch, seq, KV_WIDTH), F32),
            jax.ShapeDtypeStruct((nbatch, seq, KV_WIDTH), F32),
            jax.ShapeDtypeStruct((nbatch, seq, MLSTM_WIDTH), BF16),
            jax.ShapeDtypeStruct((nstate, MLSTM_DV, LANES), F32),
            jax.ShapeDtypeStruct((nstate, 1, LANES), F32),
            jax.ShapeDtypeStruct((nstate, 1, LANES), F32),
            jax.ShapeDtypeStruct((d, 2 * hidden), BF16),
            jax.ShapeDtypeStruct((hidden, d), BF16),
            jax.ShapeDtypeStruct(w_out.shape, BF16),
            jax.ShapeDtypeStruct((mem.shape[0], CROSS_WIDTH), F32),
            jax.ShapeDtypeStruct((mem.shape[0], CROSS_WIDTH), F32),
        ],
        scratch_shapes=[
            pltpu.VMEM((2, rows, MQK_WIDTH), BF16), pltpu.VMEM((2, rows, MQK_WIDTH), BF16),
            pltpu.VMEM((2, rows, MLSTM_WIDTH), BF16), pltpu.VMEM((2, rows, MLSTM_WIDTH), BF16),
            pltpu.VMEM((2, GATE_ROWS, rows), F32),
        ],
        compiler_params=pltpu.CompilerParams(
            dimension_semantics=("arbitrary",), vmem_limit_bytes=VMEM_LIMIT,
            allow_input_fusion=[False] + [True] * len(names) + [False] * 5),
    )(x.reshape(nbatch, seq, d), *[w[n] for n in names], w_gate, w_up, w_down, w_out, mem)
    q = q.reshape(t, ATT_WIDTH)
    k = k.reshape(t, KV_WIDTH)
    v = v.reshape(t, KV_WIDTH)

    c_st = c_st.reshape(nbatch, MLSTM_HEADS, MLSTM_DV, 2, MLSTM_DK)
    n_st = n_st.reshape(nbatch, MLSTM_HEADS, 2, MLSTM_DK)
    c_fin = jnp.stack([c_st[:, h, :, h % 2, :] for h in range(MLSTM_HEADS)], axis=1)
    n_fin = jnp.stack([n_st[:, h, h % 2, :] for h in range(MLSTM_HEADS)], axis=1)
    m_fin = m_st[:, 0, 0].reshape(nbatch, MLSTM_HEADS)
    bf16_weights = {"w_gate_up": w_gate_up_bf, "w_down": w_down_bf, "w_out": w_out_bf}
    return q, k, v, hm.reshape(t, MLSTM_WIDTH), c_fin, n_fin, m_fin, bf16_weights, memk, memv


def _mlstm_sample_kernel(mq_ref, mk_ref, mkt_ref, mv_ref, og_ref, gc_ref, gr_ref, mrow_ref, c_ref,
                         n_ref, gh_ref, hm_ref, co_ref, no_ref, mt_ref, *, nb, tq):
    rows = nb * tq
    shift = tq.bit_length() - 1
    r = lax.broadcasted_iota(jnp.int32, (rows, rows), 0)
    s = lax.broadcasted_iota(jnp.int32, (rows, rows), 1)
    same = (r >> shift) == (s >> shift)
    valid = same & (s <= r)
    tril = jnp.where(valid, 1.0, 0.0).astype(BF16)
    triu = jnp.where(same & (r <= s), 1.0, 0.0).astype(BF16)
    row_batch = lax.broadcasted_iota(jnp.int32, (rows, 1), 0) >> shift
    lane = lax.broadcasted_iota(jnp.int32, (rows, LANES), 1)

    gc = gc_ref[...]
    gr = gr_ref[...]
    btc = _cumsum_cols(tril, gc)
    btr = _cumsum_rows(gr, triu)
    m_rows = mrow_ref[...]
    mt_all = jnp.zeros((rows, LANES), F32)

    def last_of_batch(col):
        parts = [jnp.broadcast_to(col[(b + 1) * tq - 1:(b + 1) * tq], (tq, 1)) for b in range(nb)]
        return jnp.concatenate(parts, axis=0)

    dk = MLSTM_DK
    lo = _low_half((rows, LANES))
    own_q = row_batch == (lax.broadcasted_iota(jnp.int32, (rows, nb * dk), 1) >> _log2(dk))
    lane_batch = lax.broadcasted_iota(jnp.int32, (dk, rows), 1) >> shift

    def state_rows(b, h):
        return slice((b * MLSTM_HEADS + h) * dk, (b * MLSTM_HEADS + h + 1) * dk)

    for h in range(MLSTM_HEADS):
        slab = slice((h // 2) * LANES, (h // 2 + 1) * LANES)
        qs, ks = mq_ref[:, slab], mk_ref[:, slab]
        qsw, ksw = pltpu.roll(qs, dk, axis=1), pltpu.roll(ks, dk, axis=1)
        q2 = jnp.where(lo, qs, qsw) if h % 2 == 0 else jnp.where(lo, qsw, qs)
        k2 = jnp.where(lo, ks, ksw) if h % 2 == 0 else jnp.where(lo, ksw, ks)
        qb, kb = q2[:, :dk].astype(BF16), k2[:, :dk].astype(BF16)
        vb = mv_ref[:, h * MLSTM_DV:(h + 1) * MLSTM_DV].astype(BF16)
        it_c = gc[:, h:h + 1]
        bt_c = btc[:, MLSTM_HEADS + h:MLSTM_HEADS + h + 1]
        it_r = gr[h:h + 1, :]
        bt_r = btr[MLSTM_HEADS + h:MLSTM_HEADS + h + 1, :]
        m_c = m_rows[:, h:h + 1]

        num, ssum, mt, wi = _mlstm_intra(qb, kb, vb, it_c, bt_c, it_r, bt_r, m_c, valid)
        qx = jnp.where(own_q, jnp.concatenate([q2] * (nb // 2), axis=1), 0.0).astype(BF16)
        c_stack = jnp.concatenate([c_ref[state_rows(b, h), :] for b in range(nb)], axis=0)
        qc = _dot(qx, c_stack.astype(BF16))
        n_rows = jnp.concatenate(
            [jnp.broadcast_to(n_ref[b * MLSTM_HEADS + h], (tq, dk)) for b in range(nb)], axis=0)
        num = num + wi * qc
        den = ssum + wi * jnp.sum(qb.astype(F32) * n_rows, axis=-1, keepdims=True)
        og = og_ref[:, h * MLSTM_DV:(h + 1) * MLSTM_DV]
        g_head = gh_ref[:, h * MLSTM_DV:(h + 1) * MLSTM_DV]
        hm_ref[:, h * MLSTM_DV:(h + 1) * MLSTM_DV] = _mlstm_head_out(num, den, mt, og, g_head)

        b_last = last_of_batch(bt_c)
        m_new = last_of_batch(mt)
        wk = jnp.exp(b_last - bt_c + it_c - m_new)
        wc = jnp.exp(b_last + m_c - m_new)
        wkv = (wk * vb.astype(F32)).astype(BF16)
        wkk = wk * kb.astype(F32)
        kt = mkt_ref[h * dk:(h + 1) * dk, :]
        kx = jnp.concatenate([jnp.where(lane_batch == b, kt, 0.0) for b in range(nb)], axis=0)
        upd = _dot(kx.astype(BF16), wkv)
        for b in range(nb):
            idx = b * MLSTM_HEADS + h
            wc_b = wc[b * tq:b * tq + 1]
            co_ref[state_rows(b, h), :] = wc_b * c_ref[state_rows(b, h), :] + upd[b * dk:(b + 1) * dk]
            no_ref[idx] = wc_b * n_ref[idx] + jnp.sum(wkk[b * tq:(b + 1) * tq], axis=0, keepdims=True)
        mt_all = jnp.where(lane == h, mt, mt_all)
    mt_ref[...] = mt_all


def _mlstm_sample(mq, mk, mkt, mv, og, gcol, grow, m_rows, ct_in, n_in, g_head, nbatch, nb):
    t = mq.shape[0]
    tq = t // nbatch
    rows = nb * tq
    assert tq & (tq - 1) == 0 and rows == LANES and nb % 2 == 0

    def tok(width):
        return pl.BlockSpec((rows, width), lambda i: (i, 0))

    ct_spec = pl.BlockSpec((nb * MLSTM_HEADS * MLSTM_DK, MLSTM_DV), lambda i: (i, 0))
    n_spec = pl.BlockSpec((nb * MLSTM_HEADS, 1, MLSTM_DK), lambda i: (i, 0, 0))
    return pl.pallas_call(
        functools.partial(_mlstm_sample_kernel, nb=nb, tq=tq),
        grid=(nbatch // nb,),
        in_specs=[
            tok(MQK_WIDTH), tok(MQK_WIDTH),
            pl.BlockSpec((MQK_WIDTH, rows), lambda i: (0, i)),
            tok(MLSTM_WIDTH), tok(MLSTM_WIDTH), tok(LANES),
            pl.BlockSpec((GATE_ROWS, rows), lambda i: (0, i)),
            tok(LANES),
            ct_spec, n_spec,
            _const_spec((1, MLSTM_WIDTH)),
        ],
        out_specs=[tok(MLSTM_WIDTH), ct_spec, n_spec, tok(LANES)],
        out_shape=[
            jax.ShapeDtypeStruct((t, MLSTM_WIDTH), F32),
            jax.ShapeDtypeStruct(ct_in.shape, F32),
            jax.ShapeDtypeStruct(n_in.shape, F32),
            jax.ShapeDtypeStruct((t, LANES), F32),
        ],
        compiler_params=pltpu.CompilerParams(
            dimension_semantics=("arbitrary",), vmem_limit_bytes=VMEM_LIMIT),
    )(mq, mk, mkt, mv, og, gcol, grow, m_rows, ct_in, n_in, g_head)


def _mix_stage(x, att, hm, wo_ref, gcr_ref, wcq_ref):
    cat = jnp.concatenate([att.astype(BF16), hm.astype(BF16)], axis=1)
    y1 = x + _dot(cat, wo_ref[...])
    qc = _dot(_rmsnorm(y1, gcr_ref[...]).astype(BF16), wcq_ref[...])
    return y1, qc


def _cross_attention(qc, mk, mv):
    slabs = []
    for j in range(CROSS_WIDTH // LANES):
        qs = qc[:, j * LANES:(j + 1) * LANES]
        ks = mk[:, j * LANES:(j + 1) * LANES]
        vs = mv[:, j * LANES:(j + 1) * LANES]
        n = qs.shape[0]
        lo = _low_half(qs.shape)
        zero = jnp.zeros_like(qs)
        q2 = jnp.concatenate([jnp.where(lo, qs, zero), jnp.where(lo, zero, qs)], axis=0)
        s = _dot_nt(q2 * (HEAD_DIM ** -0.5), ks)
        p = jnp.exp(s - jnp.max(s, axis=-1, keepdims=True))
        o2 = _dot(p.astype(BF16), vs) / jnp.sum(p, axis=-1, keepdims=True)
        slabs.append(jnp.where(lo, o2[:n], o2[n:]))
    return jnp.concatenate(slabs, axis=1)


FFN_CHUNK = 256


def _finish(y3, gfin_ref, final):
    return _rmsnorm(y3, gfin_ref[...]) if final else y3


def _ffn_stage(y1, o, wco_ref, gf_ref, wgu_ref, wd_ref, act_ref, between=None):
    y2 = y1 + _dot(o.astype(BF16), wco_ref[...])
    xn = _rmsnorm(y2, gf_ref[...]).astype(BF16)
    for c in range(wd_ref.shape[0] // FFN_CHUNK):
        gu = _dot(xn, wgu_ref[:, 2 * c * FFN_CHUNK:2 * (c + 1) * FFN_CHUNK])
        g, u = gu[:, :FFN_CHUNK], gu[:, FFN_CHUNK:]
        act_ref[:, c * FFN_CHUNK:(c + 1) * FFN_CHUNK] = (g * jax.nn.sigmoid(g) * u).astype(BF16)
        if between is not None:
            between()
    return y2 + _dot(act_ref[...], wd_ref[...])


def _post_kernel(sink_ref, q_ref, kc_ref, kp_ref, vc_ref, vp_ref, x_ref, hm_ref, mk_ref, mv_ref,
                 xd_ref, attd_ref, hmd_ref, mkd_ref, mvd_ref,
                 wo_ref, gcr_ref, wcq_ref, wco_ref, gf_ref, wgu_ref, wd_ref, gfin_ref,
                 o_ref, od_ref, act_ref, att_ref, y3_ref, *, final, ntiles, tiles_per_seq, tq):
    tm = x_ref.shape[0]
    s = pl.program_id(0)
    tile = jnp.minimum(s, ntiles - 1)
    first_block = (tile % tiles_per_seq) == 0
    slot = s % 2

    def swa_stages():
        def store(j, value):
            att_ref[slot, j * WINDOW:(j + 1) * WINDOW, :] = value.astype(att_ref.dtype)
        return _swa_prompt_stages(sink_ref, q_ref, kc_ref, kp_ref, vc_ref, vp_ref, first_block, store)

    def finish():
        done = _finish(y3_ref[...], gfin_ref, final)
        o_ref[...] = done[:tm]
        od_ref[...] = done[tm:]

    @pl.when(s == 0)
    def _():
        y3_ref[...] = jnp.zeros_like(y3_ref)
        for _ in swa_stages():
            pass

    @pl.when(jnp.logical_and(s > 0, s <= ntiles))
    def _():
        swa = swa_stages()

        def advance(n=SWA_STAGES_PER_FFN_CHUNK):
            for _ in range(n):
                next(swa, None)

        finish()
        x = jnp.concatenate([x_ref[...], xd_ref[...]], axis=0)
        att = jnp.concatenate([att_ref[1 - slot], attd_ref[...].astype(BF16)], axis=0)
        hm = jnp.concatenate([hm_ref[...], hmd_ref[...].astype(BF16)], axis=0)
        y1, qc = _mix_stage(x, att, hm, wo_ref, gcr_ref, wcq_ref)
        advance()
        o = jnp.concatenate([
            _cross_attention(qc[:tm].astype(BF16), mk_ref[...].astype(BF16), mv_ref[...].astype(BF16)),
            _cross_attention_decode(qc[tm:], mkd_ref, mvd_ref, tq)], axis=0)
        advance()
        y3_ref[...] = _ffn_stage(y1, o, wco_ref, gf_ref, wgu_ref, wd_ref, act_ref, between=advance)
        for _ in swa:
            pass

    @pl.when(s > ntiles)
    def _():
        finish()


_MIX_WEIGHTS = ("w_out", "g_cross", "w_cq")
_FFN_WEIGHTS = ("w_co", "g_ffn", "w_gate_up", "w_down", "g_final")


def _post(x, q, k, v, hm, memk, memv, xd, attd, hmd, memkd_t, memvd_t, sinks, w, nbatch, nseq, final, tm):
    t, d = x.shape
    ntiles = t // tm
    tpb = ntiles // nbatch
    wpt = tm // WINDOW
    m = memk.shape[0] // nbatch
    hidden = w["w_down"].shape[0]
    td = xd.shape[0]
    tq = td // nseq
    slab = td // ntiles
    assert td % ntiles == 0 and slab % tq == 0 and slab % BF16_SUBLANES == 0
    mem_d = memkd_t.shape[1]

    def attn_tile(s):
        return jnp.minimum(s, ntiles - 1)

    def post_tile(s):
        return jnp.clip(s - 1, 0, ntiles - 1)

    def done_tile(s):
        return jnp.maximum(s - 2, 0)

    def attn_rows(width):
        return pl.BlockSpec((tm, width), lambda s: (attn_tile(s), 0))

    def prev_window(width):
        return pl.BlockSpec((WINDOW, width), lambda s: (jnp.maximum(attn_tile(s) * wpt - 1, 0), 0))

    def post_rows(width):
        return pl.BlockSpec((tm, width), lambda s: (post_tile(s), 0))

    def post_slab(width):
        return pl.BlockSpec((slab, width), lambda s: (post_tile(s), 0))

    mem_spec = pl.BlockSpec((m, CROSS_WIDTH), lambda s: (post_tile(s) // tpb, 0))
    memd_spec = pl.BlockSpec((slab // tq * CROSS_WIDTH, mem_d), lambda s: (post_tile(s), 0))
    names = _MIX_WEIGHTS + _FFN_WEIGHTS
    return pl.pallas_call(
        functools.partial(_post_kernel, final=final, ntiles=ntiles, tiles_per_seq=tpb, tq=tq),
        grid=(ntiles + 2,),
        in_specs=[
            pl.BlockSpec(memory_space=pltpu.SMEM),
            attn_rows(ATT_WIDTH), attn_rows(KV_WIDTH), prev_window(KV_WIDTH),
            attn_rows(KV_WIDTH), prev_window(KV_WIDTH),
            post_rows(d), post_rows(MLSTM_WIDTH), mem_spec, mem_spec,
            post_slab(d), post_slab(ATT_WIDTH), post_slab(MLSTM_WIDTH), memd_spec, memd_spec,
        ] + _weight_specs(w, names),
        out_specs=[pl.BlockSpec((tm, d), lambda s: (done_tile(s), 0)),
                   pl.BlockSpec((slab, d), lambda s: (done_tile(s), 0))],
        out_shape=[jax.ShapeDtypeStruct((t, d), F32), jax.ShapeDtypeStruct((td, d), F32)],
        scratch_shapes=[pltpu.VMEM((tm + slab, hidden), BF16), pltpu.VMEM((2, tm, ATT_WIDTH), BF16),
                        pltpu.VMEM((tm + slab, d), F32)],
        compiler_params=pltpu.CompilerParams(
            dimension_semantics=("arbitrary",), vmem_limit_bytes=VMEM_LIMIT),
    )(sinks, q, k, k, v, v, x, hm, memk, memv, xd, attd, hmd, memkd_t, memvd_t,
      *[w[n] for n in names])


def _cross_attention_decode(qc, mk_ref, mv_ref, tq):
    n = CROSS_HEADS * tq
    row_head = lax.broadcasted_iota(jnp.int32, (n, CROSS_WIDTH), 0) >> _log2(tq)
    lane_head = lax.broadcasted_iota(jnp.int32, (n, CROSS_WIDTH), 1) >> _log2(HEAD_DIM)
    own = row_head == lane_head
    outs = []
    for b in range(qc.shape[0] // tq):
        q = qc[b * tq:(b + 1) * tq, :]
        qx = jnp.where(own, jnp.concatenate([q] * CROSS_HEADS, axis=0), 0.0).astype(BF16)
        mk = mk_ref[b * CROSS_WIDTH:(b + 1) * CROSS_WIDTH, :].astype(BF16)
        mv = mv_ref[b * CROSS_WIDTH:(b + 1) * CROSS_WIDTH, :].astype(BF16)
        s = _dot(qx, mk) * (HEAD_DIM ** -0.5)
        p = jnp.exp(s - jnp.max(s, axis=-1, keepdims=True))
        pv = _dot_nt(p.astype(BF16), mv) / jnp.sum(p, axis=-1, keepdims=True)
        pv = jnp.where(own, pv, 0.0)
        out = pv[0:tq]
        for h in range(1, CROSS_HEADS):
            out = out + pv[h * tq:(h + 1) * tq]
        outs.append(out)
    return jnp.concatenate(outs, axis=0)


def _layer_weights(l, w_in, b_igate, b_fgate, g_mlstm_head, g_mix, g_cross, g_mem, w_cq, w_ck, w_cv,
                   w_co, g_ffn, g_final):
    gate_w = w_in[l][:, MAIN_WIDTH:]
    gate_b = jnp.concatenate([b_igate[l], b_fgate[l]]).astype(F32)
    ngate = gate_b.shape[0]
    row = lambda a: a.astype(F32).reshape(1, -1)
    w_t = w_in[l].T
    k0 = ATT_WIDTH
    mk0 = ATT_WIDTH + 2 * KV_WIDTH + MQK_WIDTH
    return {
        "w_kvk_t": jnp.concatenate([w_t[k0:k0 + 2 * KV_WIDTH], w_t[mk0:mk0 + MQK_WIDTH]]).astype(BF16),
        "w_main": w_t[:MAIN_WIDTH].astype(BF16),
        "w_gate_cols": jnp.pad(gate_w, ((0, 0), (0, LANES - ngate))).astype(BF16),
        "w_gate_rows": jnp.pad(gate_w.T, ((0, GATE_ROWS - ngate), (0, 0))).astype(BF16),
        "b_gate_cols": jnp.pad(gate_b, (0, LANES - ngate)).reshape(1, LANES),
        "b_gate_rows": jnp.pad(gate_b, (0, GATE_ROWS - ngate)).reshape(GATE_ROWS, 1),
        "g_mix": row(g_mix[l]), "g_cross": row(g_cross[l]), "g_mem": row(g_mem[l]),
        "g_ffn": row(g_ffn[l]), "g_final": row(g_final), "g_head": row(g_mlstm_head[l]),
        "w_cq": w_cq[l].astype(BF16), "w_ck": w_ck[l].astype(BF16),
        "w_cv": w_cv[l].astype(BF16), "w_co": w_co[l].astype(BF16),
    }


TOKEN_TILE = 512
MLSTM_STAGES_PER_PROJ_STAGE = 15
SWA_STAGES_PER_FFN_CHUNK = 3
MLSTM_CHUNK_ROWS = 256
SWA_SAMPLE_BATCH = 16
MLSTM_SAMPLE_BATCH = 16


def kernel(x_prompt, x_sample, mem_prompt, cache_swa_k, cache_swa_v, state_mlstm_C, state_mlstm_n,
           state_mlstm_m, cache_mem_k, cache_mem_v, w_in, b_igate, b_fgate, attn_sinks,
           g_mlstm_head, w_out, g_mix, g_cross, g_mem, w_cq, w_ck, w_cv, w_co, g_ffn, w_gate,
           w_up, w_down, g_final):
    depth = w_in.shape[0]
    bp, sp, d = x_prompt.shape
    bs, ss, _ = x_sample.shape
    mem_tokens = mem_prompt.shape[1]
    past = cache_swa_k.shape[2]
    yp = x_prompt.reshape(bp * sp, d)
    ys = x_sample.reshape(bs * ss, d)
    mem = mem_prompt.reshape(bp * mem_tokens, d)
    outs = [[] for _ in range(12)]

    for l in range(depth):
        final = l == depth - 1
        w = _layer_weights(l, w_in, b_igate, b_fgate, g_mlstm_head, g_mix, g_cross, g_mem, w_cq,
                           w_ck, w_cv, w_co, g_ffn, g_final)
        sinks = attn_sinks[l].astype(F32)

        q, k, v, hm, c_p, n_p, m_p, cast, memk, memv = _mixer_prompt(
            yp, mem, w, w_gate[l].astype(F32), w_up[l].astype(F32), w_down[l].astype(F32),
            w_out[l].astype(F32), bp, MLSTM_CHUNK_ROWS)
        w.update(cast)
        win_shape = (bp, WINDOW, ATT_KV_HEADS, HEAD_DIM)
        outs[0].append(k.reshape(bp, sp, KV_WIDTH)[:, sp - WINDOW:].reshape(win_shape))
        outs[1].append(v.reshape(bp, sp, KV_WIDTH)[:, sp - WINDOW:].reshape(win_shape))
        outs[2].append(c_p)
        outs[3].append(n_p)
        outs[4].append(m_p)
        mem_shape = (bp, mem_tokens, CROSS_HEADS, HEAD_DIM)
        outs[5].append(memk.reshape(mem_shape))
        outs[6].append(memv.reshape(mem_shape))

        def tokens_last(a):
            return jnp.transpose(a.astype(F32), (0, 2, 3, 1)).reshape(-1, a.shape[1])

        qd, mq, mk, mv, og, grow, gcol, knt, vnt, mkt = _decode_proj(ys, w, TOKEN_TILE)
        att_d, kbuf_t, vbuf_t = _swa_sample(
            qd, knt, vnt, tokens_last(cache_swa_k[l]), tokens_last(cache_swa_v[l]), sinks, bs,
            SWA_SAMPLE_BATCH)
        m_rows = jnp.pad(jnp.repeat(state_mlstm_m[l].astype(F32), ss, axis=0),
                         ((0, 0), (0, LANES - MLSTM_HEADS)))
        ct_in = jnp.swapaxes(state_mlstm_C[l].astype(F32), 2, 3).reshape(-1, MLSTM_DV)
        hm_d, ct_s, n_s, mt = _mlstm_sample(
            mq, mk, mkt, mv, og, gcol, grow, m_rows, ct_in,
            state_mlstm_n[l].astype(F32).reshape(bs * MLSTM_HEADS, 1, MLSTM_DK),
            w["g_head"], bs, MLSTM_SAMPLE_BATCH)

        yp, ys = _post(yp, q, k, v, hm, memk, memv, ys, att_d, hm_d, tokens_last(cache_mem_k[l]),
                       tokens_last(cache_mem_v[l]), sinks, w, bp, bs, final, TOKEN_TILE)
        buf_t_shape = (bs, ATT_KV_HEADS, HEAD_DIM, past)
        outs[7].append(jnp.transpose(kbuf_t.reshape(buf_t_shape), (0, 3, 1, 2)))
        outs[8].append(jnp.transpose(vbuf_t.reshape(buf_t_shape), (0, 3, 1, 2)))
        outs[9].append(jnp.swapaxes(ct_s.reshape(bs, MLSTM_HEADS, MLSTM_DK, MLSTM_DV), 2, 3))
        outs[10].append(n_s.reshape(bs, MLSTM_HEADS, MLSTM_DK))
        outs[11].append(mt.reshape(bs, ss, LANES)[:, ss - 1, :MLSTM_HEADS])

    return (yp.reshape(bp, sp, d), ys.reshape(bs, ss, d)) + tuple(jnp.stack(o) for o in outs)
```

```python
import functools

import jax
import jax.numpy as jnp
from jax import lax
from jax.experimental import pallas as pl
from jax.experimental.pallas import tpu as pltpu

F32 = jnp.float32
BF16 = jnp.bfloat16

HEAD_DIM = 64
ATT_GROUP = 4
ATT_KV_HEADS = 2
ATT_WIDTH = 512
KV_WIDTH = ATT_KV_HEADS * HEAD_DIM
WINDOW = 128
MLSTM_HEADS = 4
MLSTM_DV = 128
MLSTM_DK = 64
MQK_WIDTH = MLSTM_HEADS * MLSTM_DK
MLSTM_WIDTH = MLSTM_HEADS * MLSTM_DV
MAIN_WIDTH = ATT_WIDTH + 2 * KV_WIDTH + 2 * MQK_WIDTH + 2 * MLSTM_WIDTH
CROSS_HEADS = 4
CROSS_WIDTH = CROSS_HEADS * HEAD_DIM
EPS = 1e-6
NEG_INF = float("-inf")
assert HEAD_DIM == 4 ** 3

LANES = 128
F32_SUBLANES = 8
BF16_SUBLANES = 16
GATE_ROWS = BF16_SUBLANES
VMEM_LIMIT = 56 * 1024 * 1024

NT = (((1,), (1,)), ((), ()))


def _dot(a, b):
    return jnp.dot(a, b, preferred_element_type=F32)


def _dot_nt(a, b):
    return lax.dot_general(a, b, NT, preferred_element_type=F32)


def _rmsnorm(x, g):
    return x * lax.rsqrt(jnp.mean(x * x, axis=-1, keepdims=True) + EPS) * g


def _log_sigmoid(x):
    return jnp.minimum(x, 0.0) - jnp.log1p(jnp.exp(-jnp.abs(x)))


def _split3(x):
    hi = x.astype(BF16)
    r1 = x - hi.astype(F32)
    mid = r1.astype(BF16)
    lo = (r1 - mid.astype(F32)).astype(BF16)
    return hi, mid, lo


def _cumsum_cols(tri, x):
    hi, mid, lo = _split3(x)
    return _dot(tri, hi) + _dot(tri, mid) + _dot(tri, lo)


def _cumsum_rows(x, tri):
    hi, mid, lo = _split3(x)
    return _dot(hi, tri) + _dot(mid, tri) + _dot(lo, tri)


def _log2(n):
    assert n > 0 and n & (n - 1) == 0, n
    return n.bit_length() - 1


def _low_half(shape):
    return lax.broadcasted_iota(jnp.int32, shape, 1) < HEAD_DIM


def _decode_proj_kernel(x_ref, g_ref, w_ref, wgr_ref, br_ref, wgc_ref, bc_ref, wt_ref,
                        q_ref, mq_ref, mk_ref, mv_ref, og_ref, gr_ref, gc_ref, kt_ref, vt_ref, mkt_ref):
    xn = _rmsnorm(x_ref[...], g_ref[...]).astype(BF16)
    zt = _dot_nt(wt_ref[...], xn)
    kt_ref[...] = zt[:KV_WIDTH]
    vt_ref[...] = zt[KV_WIDTH:2 * KV_WIDTH]
    mkt_ref[...] = zt[2 * KV_WIDTH:] * (MLSTM_DK ** -0.5)

    def mm(lo, hi):
        return _dot_nt(xn, w_ref[lo:hi, :])

    o = 0
    q_ref[...] = mm(o, o + ATT_WIDTH)
    o += ATT_WIDTH + 2 * KV_WIDTH
    mqk = mm(o, o + 2 * MQK_WIDTH)
    mq_ref[...] = mqk[:, :MQK_WIDTH]
    mk_ref[...] = mqk[:, MQK_WIDTH:] * (MLSTM_DK ** -0.5)
    o += 2 * MQK_WIDTH
    mv_ref[...] = mm(o, o + MLSTM_WIDTH)
    o += MLSTM_WIDTH
    og_ref[...] = mm(o, o + MLSTM_WIDTH)

    zc = _dot(xn, wgc_ref[...]) + bc_ref[...]
    lane = lax.broadcasted_iota(jnp.int32, zc.shape, 1)
    gc_ref[...] = jnp.where(lane < MLSTM_HEADS, zc, _log_sigmoid(zc))
    zr = _dot_nt(wgr_ref[...], xn) + br_ref[...]
    row = lax.broadcasted_iota(jnp.int32, zr.shape, 0)
    gr_ref[...] = jnp.where(row < MLSTM_HEADS, zr, _log_sigmoid(zr))


def _const_spec(shape):
    nd = len(shape)
    return pl.BlockSpec(shape, lambda *_: (0,) * nd, pipeline_mode=pl.Buffered(1))


def _weight_specs(w, names):
    return [_const_spec(w[n].shape) for n in names]


def _decode_proj(x, w, tm):
    t, d = x.shape

    def rows(width):
        return pl.BlockSpec((tm, width), lambda i: (i, 0))

    def lanes(height):
        return pl.BlockSpec((height, tm), lambda i: (0, i))

    outs = [(rows, ATT_WIDTH), (rows, MQK_WIDTH), (rows, MQK_WIDTH), (rows, MLSTM_WIDTH),
            (rows, MLSTM_WIDTH), (lanes, GATE_ROWS), (rows, LANES), (lanes, KV_WIDTH),
            (lanes, KV_WIDTH), (lanes, MQK_WIDTH)]
    names = ["g_mix", "w_main", "w_gate_rows", "b_gate_rows", "w_gate_cols", "b_gate_cols", "w_kvk_t"]
    return pl.pallas_call(
        _decode_proj_kernel,
        grid=(t // tm,),
        in_specs=[rows(d)] + _weight_specs(w, names),
        out_specs=[kind(size) for kind, size in outs],
        out_shape=[jax.ShapeDtypeStruct((t, size) if kind is rows else (size, t), F32)
                   for kind, size in outs],
        compiler_params=pltpu.CompilerParams(
            dimension_semantics=("arbitrary",), vmem_limit_bytes=VMEM_LIMIT,
            allow_input_fusion=[False] + [True] * len(names)),
    )(x, *[w[n] for n in names])


def _dup_heads(x):
    swapped = pltpu.roll(x, HEAD_DIM, axis=1)
    lo = _low_half(x.shape)
    return (jnp.where(lo, x, swapped).astype(BF16), jnp.where(lo, swapped, x).astype(BF16))


def _stack_query_heads(q, h):
    parts = []
    for g in range(ATT_GROUP):
        hq = h * ATT_GROUP + g
        slab = q[:, (hq // 2) * LANES:(hq // 2 + 1) * LANES]
        lo = _low_half(slab.shape)
        keep = lo if hq % 2 == 0 else jnp.logical_not(lo)
        parts.append(jnp.where(keep, slab, jnp.zeros_like(slab)))
    return jnp.concatenate(parts, axis=0).astype(BF16)


def _sink_column(sink_ref, h, rows_per_head):
    n = ATT_GROUP * rows_per_head
    grp = lax.broadcasted_iota(jnp.int32, (n, 1), 0) >> _log2(rows_per_head)
    col = jnp.full((n, 1), sink_ref[h * ATT_GROUP], F32)
    for g in range(1, ATT_GROUP):
        col = jnp.where(grp == g, sink_ref[h * ATT_GROUP + g], col)
    return col


def _swa_prompt_stages(sink_ref, q_ref, kc_ref, kp_ref, vc_ref, vp_ref, first_block, store):
    qb = q_ref.shape[0]
    k_all = jnp.concatenate([kp_ref[...], kc_ref[...]], axis=0)
    v_all = jnp.concatenate([vp_ref[...], vc_ref[...]], axis=0)
    kd = _dup_heads(k_all)
    v_t = v_all.T.astype(BF16)

    n = ATT_GROUP * WINDOW
    key = lax.broadcasted_iota(jnp.int32, (2 * WINDOW, n), 0)
    qry = lax.broadcasted_iota(jnp.int32, (2 * WINDOW, n), 1) & (WINDOW - 1)
    delta = key - qry
    in_band = (delta >= 0) & (delta <= WINDOW)
    grp = lax.broadcasted_iota(jnp.int32, (1, n), 1) >> _log2(WINDOW)
    sinks = []
    for h in range(ATT_KV_HEADS):
        row = jnp.full((1, n), sink_ref[h * ATT_GROUP], F32)
        for g in range(1, ATT_GROUP):
            row = jnp.where(grp == g, sink_ref[h * ATT_GROUP + g], row)
        sinks.append(row)
    yield

    probs = []
    for j in range(qb // WINDOW):
        valid = in_band & (key >= jnp.where(first_block, WINDOW, 0)) if j == 0 else in_band
        q = q_ref[j * WINDOW:(j + 1) * WINDOW, :] * (HEAD_DIM ** -0.5)
        band = slice(j * WINDOW, (j + 2) * WINDOW)
        for h in range(ATT_KV_HEADS):
            qh = _stack_query_heads(q, h)
            s = _dot_nt(kd[h][band], qh)
            probs.append(dict(h=h, band=band, s=jnp.where(valid, s, NEG_INF)))
            yield
    for p in probs:
        p["m"] = jnp.maximum(jnp.max(p["s"], axis=0, keepdims=True), sinks[p["h"]])
        yield
    for p in probs:
        p["p"] = jnp.exp(p["s"] - p["m"])
        yield
    for p in probs:
        h = p["h"]
        denom = jnp.sum(p["p"], axis=0, keepdims=True) + jnp.exp(sinks[h] - p["m"])
        p["o_t"] = _dot(v_t[h * HEAD_DIM:(h + 1) * HEAD_DIM, p["band"]], p["p"].astype(BF16)) / denom
        yield
    for j in range(qb // WINDOW):
        pieces = [p["o_t"][:, g * WINDOW:(g + 1) * WINDOW]
                  for p in probs[j * ATT_KV_HEADS:(j + 1) * ATT_KV_HEADS] for g in range(ATT_GROUP)]
        store(j, jnp.concatenate(pieces, axis=0).T)
        yield


def _swa_sample_kernel(sink_ref, q_ref, knt_ref, vnt_ref, ck_ref, cv_ref,
                       o_ref, kbuf_ref, vbuf_ref, *, nb, tq):
    past = ck_ref.shape[1]
    hw = ATT_KV_HEADS * HEAD_DIM
    heads = ATT_KV_HEADS * ATT_GROUP
    n = heads * tq
    fresh = past - tq
    t = lax.broadcasted_iota(jnp.int32, (n, 2 * past), 0) & (tq - 1)
    col = lax.broadcasted_iota(jnp.int32, (n, 2 * past), 1)
    valid = ((col < past) & (col >= t)) | ((col >= past + fresh) & (col - (past + fresh) <= t))
    sink = jnp.concatenate([_sink_column(sink_ref, h, tq) for h in range(ATT_KV_HEADS)], axis=0)
    lane = lax.broadcasted_iota(jnp.int32, (hw, past), 1)
    knt = knt_ref[...]
    vnt = vnt_ref[...]
    zero_slab = jnp.zeros((tq, LANES), F32)
    lo = _low_half((tq, LANES))

    def shifted(old, new_t, b):
        return jnp.where(lane >= fresh, pltpu.roll(new_t, (fresh - b * tq) % past, axis=1),
                         pltpu.roll(old, fresh, axis=1))

    def dup_rows(x):
        h0, h1 = x[:HEAD_DIM], x[HEAD_DIM:]
        return jnp.concatenate([h0, h0, h1, h1], axis=0)

    for b in range(nb):
        k_old = ck_ref[b * hw:(b + 1) * hw, :]
        v_old = cv_ref[b * hw:(b + 1) * hw, :]
        k_new = shifted(k_old, knt, b)
        v_new = shifted(v_old, vnt, b)
        kbuf_ref[b * hw:(b + 1) * hw, :] = k_new
        vbuf_ref[b * hw:(b + 1) * hw, :] = v_new
        kop = jnp.concatenate([dup_rows(k_old), dup_rows(k_new)], axis=1).astype(BF16)
        vop = jnp.concatenate([dup_rows(v_old), dup_rows(v_new)], axis=1).astype(BF16)

        q = q_ref[b * tq:(b + 1) * tq, :]
        blocks = []
        for hq in range(heads):
            slab = q[:, (hq // 2) * LANES:(hq // 2 + 1) * LANES]
            slab = jnp.where(lo if hq % 2 == 0 else jnp.logical_not(lo), slab, 0.0)
            pair = [slab, zero_slab] if hq // ATT_GROUP == 0 else [zero_slab, slab]
            blocks.append(jnp.concatenate(pair, axis=1))
        qx = jnp.concatenate(blocks, axis=0).astype(BF16)

        s = _dot(qx, kop) * (HEAD_DIM ** -0.5)
        s = jnp.where(valid, s, NEG_INF)
        m = jnp.maximum(jnp.max(s, axis=-1, keepdims=True), sink)
        p = jnp.exp(s - m)
        denom = jnp.sum(p, axis=-1, keepdims=True) + jnp.exp(sink - m)
        o = _dot_nt(p.astype(BF16), vop) / denom

        slabs = []
        for j in range(heads // 2):
            h = (2 * j) // ATT_GROUP
            even = o[(2 * j) * tq:(2 * j + 1) * tq, h * LANES:(h + 1) * LANES]
            odd = o[(2 * j + 1) * tq:(2 * j + 2) * tq, h * LANES:(h + 1) * LANES]
            slabs.append(jnp.where(lo, even, odd))
        o_ref[b * tq:(b + 1) * tq, :] = jnp.concatenate(slabs, axis=1)


def _swa_sample(q, knt, vnt, cache_kt, cache_vt, sinks, nbatch, nb):
    t = q.shape[0]
    tq = t // nbatch
    past = cache_kt.shape[1]
    hw = ATT_KV_HEADS * HEAD_DIM
    assert past == WINDOW and tq & (tq - 1) == 0 and nb * tq == past

    def rows(r, width):
        return pl.BlockSpec((r, width), lambda i: (i, 0))

    new_t = pl.BlockSpec((hw, nb * tq), lambda i: (0, i))
    return pl.pallas_call(
        functools.partial(_swa_sample_kernel, nb=nb, tq=tq),
        grid=(nbatch // nb,),
        in_specs=[
            pl.BlockSpec(memory_space=pltpu.SMEM),
            rows(nb * tq, ATT_WIDTH), new_t, new_t, rows(nb * hw, past), rows(nb * hw, past),
        ],
        out_specs=[rows(nb * tq, ATT_WIDTH), rows(nb * hw, past), rows(nb * hw, past)],
        out_shape=[
            jax.ShapeDtypeStruct((t, ATT_WIDTH), F32),
            jax.ShapeDtypeStruct(cache_kt.shape, F32),
            jax.ShapeDtypeStruct(cache_vt.shape, F32),
        ],
        compiler_params=pltpu.CompilerParams(
            dimension_semantics=("arbitrary",), vmem_limit_bytes=VMEM_LIMIT),
    )(sinks, q, knt, vnt, cache_kt, cache_vt)


def _mlstm_intra(q, k, v, it_c, bt_c, it_r, bt_r, m_c, valid):
    d = jnp.where(valid, bt_c - bt_r + it_r, NEG_INF)
    inter = bt_c + m_c
    mt = jnp.maximum(jnp.max(d, axis=-1, keepdims=True), inter)
    sm = _dot_nt(q, k) * jnp.exp(d - mt)
    wi = jnp.exp(inter - mt)
    return _dot(sm.astype(BF16), v), jnp.sum(sm, axis=-1, keepdims=True), mt, wi


def _mlstm_head_out(num, den, mt, og, g_head):
    lower = jnp.maximum(jnp.abs(den), jnp.exp(-mt))
    h = num / lower
    hn = h * lax.rsqrt(jnp.mean(h * h, axis=-1, keepdims=True) + EPS)
    return hn * g_head * jax.nn.sigmoid(og)


def _mlstm_prompt_stages(mq, mk, mv, og, gr, gh_ref, hm_ref, c_ref, n_ref, m_ref, nbatch, chunk):
    s_idx = lax.broadcasted_iota(jnp.int32, (chunk, chunk), 0)
    l_idx = lax.broadcasted_iota(jnp.int32, (chunk, chunk), 1)
    causal = s_idx <= l_idx
    triu = jnp.where(causal, 1.0, 0.0).astype(BF16)
    lo = _low_half((chunk, LANES))
    pad_rows = jnp.zeros((LANES - MLSTM_HEADS, chunk), F32)
    sub8 = (GATE_ROWS, chunk)

    probs = []
    for b in range(nbatch):
        g = gr(b)
        btr = _cumsum_rows(g, triu)
        it_rows = g[:MLSTM_HEADS]
        bt_rows = btr[MLSTM_HEADS:2 * MLSTM_HEADS]
        u_cols = jnp.concatenate([it_rows - bt_rows, pad_rows], axis=0).T
        for h in range(MLSTM_HEADS):
            slab = slice((h // 2) * LANES, (h // 2 + 1) * LANES)
            keep = lo if h % 2 == 0 else jnp.logical_not(lo)
            q = mq(b, slab)
            k = mk(b, slab)
            probs.append(dict(
                b=b, h=h, idx=b * MLSTM_HEADS + h, cols=slice(h * MLSTM_DV, (h + 1) * MLSTM_DV),
                q=jnp.where(keep, q, jnp.zeros_like(q)), k=jnp.where(keep, k, jnp.zeros_like(k)),
                u_c=u_cols[:, h:h + 1], it_r=it_rows[h:h + 1, :], bt_r=bt_rows[h:h + 1, :]))
        yield

    for p in probs:
        p["m_old"] = m_ref[p["idx"]][:, 0:1]
        p["d"] = jnp.where(causal, p["bt_r"] + p["u_c"], NEG_INF)
        p["inter"] = p["bt_r"] + p["m_old"]
        p["q_t"] = p["q"].astype(F32).T.astype(BF16)
        p["kq"] = _dot(p["k"], p["q_t"])
        p["v_t"] = mv(p["b"], p["cols"]).astype(F32).T
        yield
    for p in probs:
        p["mt"] = jnp.maximum(jnp.max(p["d"], axis=0, keepdims=True), p["inter"])
        yield
    for p in probs:
        p["sm"] = p["kq"] * jnp.exp(p["d"] - p["mt"])
        p["wi"] = jnp.exp(p["inter"] - p["mt"])
        yield
    for p in probs:
        c_old = c_ref[p["idx"]]
        n_old = jnp.broadcast_to(n_ref[p["idx"]], (sub8[0], LANES))
        num = (_dot(p["v_t"].astype(BF16), p["sm"].astype(BF16))
               + p["wi"] * _dot(c_old.astype(BF16), p["q_t"]))
        den = (jnp.sum(p["sm"], axis=0, keepdims=True)
               + p["wi"] * _dot(n_old.astype(BF16), p["q_t"])[0:1])
        lower = jnp.maximum(jnp.abs(den), jnp.exp(-p["mt"]))
        h_t = num / lower
        p["hn"] = (h_t * lax.rsqrt(jnp.mean(h_t * h_t, axis=0, keepdims=True) + EPS)).T
        yield
    for p in probs:
        gate = jax.nn.sigmoid(og(p["b"], p["cols"]).astype(F32))
        hm_ref[p["b"], :, p["cols"]] = (p["hn"] * gh_ref[:, p["cols"]] * gate).astype(hm_ref.dtype)
        yield
    for p in probs:
        idx = p["idx"]
        b_last = p["bt_r"][:, chunk - 1:chunk]
        m_new = p["mt"][:, chunk - 1:chunk]
        wk = jnp.exp(b_last - p["bt_r"] + p["it_r"] - m_new)
        wc = jnp.exp(b_last + p["m_old"] - m_new)
        c_ref[idx] = wc * c_ref[idx] + _dot((p["v_t"] * wk).astype(BF16), p["k"])
        n_ref[idx] = wc * n_ref[idx] + _dot(jnp.broadcast_to(wk, sub8).astype(BF16), p["k"])[0:1]
        m_ref[idx] = jnp.broadcast_to(m_new, (1, LANES))
        yield


def _mixer_prompt_kernel(x_ref, g_ref, w_ref, wgr_ref, br_ref, gh_ref, gmem_ref, wck_ref, wcv_ref,
                         wg_ref, wu_ref, wd_ref, wo_ref, mem_ref,
                         q_ref, k_ref, v_ref, hm_ref, c_ref, n_ref, m_ref, wgu_out, wd_out, wo_out,
                         memk_out, memv_out,
                         mq_scr, mk_scr, mv_scr, og_scr, gr_scr, *, nbatch, chunk):
    s = pl.program_id(0)
    wr = s % 2
    rd = 1 - wr
    rows = nbatch * chunk

    def side_work():
        for c in range(0, wg_ref.shape[1], FFN_CHUNK):
            wgu_out[:, 2 * c:2 * c + FFN_CHUNK] = wg_ref[:, c:c + FFN_CHUNK].astype(BF16)
            wgu_out[:, 2 * c + FFN_CHUNK:2 * (c + FFN_CHUNK)] = wu_ref[:, c:c + FFN_CHUNK].astype(BF16)
        wd_out[...] = wd_ref[...].astype(BF16)
        wo_out[...] = wo_ref[...].astype(BF16)
        mn = _rmsnorm(mem_ref[...], gmem_ref[...]).astype(BF16)
        memk_out[...] = _dot(mn, wck_ref[...])
        memv_out[...] = _dot(mn, wcv_ref[...])

    def proj_stages():
        xn = _rmsnorm(x_ref[...].reshape(rows, x_ref.shape[-1]), g_ref[...]).astype(BF16)
        yield

        def mm(lo, hi):
            return _dot_nt(xn, w_ref[lo:hi, :])

        o = 0
        q_ref[...] = mm(o, o + ATT_WIDTH).astype(q_ref.dtype).reshape(q_ref.shape)
        yield
        o += ATT_WIDTH
        kv = mm(o, o + 2 * KV_WIDTH)
        k_ref[...] = kv[:, :KV_WIDTH].reshape(k_ref.shape)
        v_ref[...] = kv[:, KV_WIDTH:].reshape(v_ref.shape)
        yield
        o += 2 * KV_WIDTH
        mqk = mm(o, o + 2 * MQK_WIDTH)
        mq_scr[wr] = mqk[:, :MQK_WIDTH].astype(BF16)
        mk_scr[wr] = (mqk[:, MQK_WIDTH:] * (MLSTM_DK ** -0.5)).astype(BF16)
        yield
        o += 2 * MQK_WIDTH
        mv_scr[wr] = mm(o, o + MLSTM_WIDTH).astype(BF16)
        yield
        o += MLSTM_WIDTH
        og_scr[wr] = mm(o, o + MLSTM_WIDTH).astype(BF16)
        yield
        zr = _dot_nt(wgr_ref[...], xn) + br_ref[...]
        row = lax.broadcasted_iota(jnp.int32, zr.shape, 0)
        gr_scr[wr] = jnp.where(row < MLSTM_HEADS, zr, _log_sigmoid(zr))
        yield

    def mlstm_stages():
        def rows_of(scr):
            return lambda b, lanes: scr[rd, b * chunk:(b + 1) * chunk, lanes]
        return _mlstm_prompt_stages(
            rows_of(mq_scr), rows_of(mk_scr), rows_of(mv_scr), rows_of(og_scr),
            lambda b: gr_scr[rd, :, b * chunk:(b + 1) * chunk],
            gh_ref, hm_ref, c_ref, n_ref, m_ref, nbatch, chunk)

    @pl.when(s == 0)
    def _():
        c_ref[...] = jnp.zeros_like(c_ref)
        n_ref[...] = jnp.zeros_like(n_ref)
        m_ref[...] = jnp.zeros_like(m_ref)
        side_work()
        for _ in proj_stages():
            pass

    @pl.when(s > 0)
    def _():
        side_work()
        mlstm = mlstm_stages()
        for _ in proj_stages():
            for _ in range(MLSTM_STAGES_PER_PROJ_STAGE):
                next(mlstm, None)
        for _ in mlstm:
            pass


def _mixer_prompt(x, mem, w, w_gate, w_up, w_down, w_out, nbatch, chunk):
    t, d = x.shape
    seq = t // nbatch
    nchunks = seq // chunk
    nstate = nbatch * MLSTM_HEADS
    rows = nbatch * chunk
    hidden = w_down.shape[0]
    assert hidden % FFN_CHUNK == 0 and mem.shape[0] % (nchunks * F32_SUBLANES) == 0
    assert d % (nchunks * BF16_SUBLANES) == 0 and hidden % (nchunks * BF16_SUBLANES) == 0

    def proj_chunk(width):
        return pl.BlockSpec((nbatch, chunk, width), lambda s: (0, jnp.minimum(s, nchunks - 1), 0))

    def state(shape):
        return pl.BlockSpec(shape, lambda s: (0, 0, 0))

    def weight_rows(a, width=None):
        return pl.BlockSpec((a.shape[0] // nchunks, width or a.shape[1]),
                            lambda s: (jnp.minimum(s, nchunks - 1), 0))

    names = ["g_mix", "w_main", "w_gate_rows", "b_gate_rows", "g_head", "g_mem", "w_ck", "w_cv"]
    q, k, v, hm, c_st, n_st, m_st, w_gate_up_bf, w_down_bf, w_out_bf, memk, memv = pl.pallas_call(
        functools.partial(_mixer_prompt_kernel, nbatch=nbatch, chunk=chunk),
        grid=(nchunks + 1,),
        in_specs=[proj_chunk(d)] + _weight_specs(w, names) + [
            weight_rows(w_gate), weight_rows(w_up), weight_rows(w_down), weight_rows(w_out),
            weight_rows(mem),
        ],
        out_specs=[
            proj_chunk(ATT_WIDTH), proj_chunk(KV_WIDTH), proj_chunk(KV_WIDTH),
            pl.BlockSpec((nbatch, chunk, MLSTM_WIDTH), lambda s: (0, jnp.maximum(s - 1, 0), 0)),
            state((nstate, MLSTM_DV, LANES)), state((nstate, 1, LANES)), state((nstate, 1, LANES)),
            weight_rows(w_gate, 2 * hidden), weight_rows(w_down), weight_rows(w_out),
            weight_rows(mem, CROSS_WIDTH), weight_rows(mem, CROSS_WIDTH),
        ],
        out_shape=[
            jax.ShapeDtypeStruct((nbatch, seq, ATT_WIDTH), BF16),
            jax.ShapeDtypeStruct((nbatch, seq, KV_WIDTH), F32),
            jax.ShapeDtypeStruct((nbatch, seq, KV_WIDTH), F32),
            jax.ShapeDtypeStruct((nbatch, seq, MLSTM_WIDTH), BF16),
            jax.ShapeDtypeStruct((nstate, MLSTM_DV, LANES), F32),
            jax.ShapeDtypeStruct((nstate, 1, LANES), F32),
            jax.ShapeDtypeStruct((nstate, 1, LANES), F32),
            jax.ShapeDtypeStruct((d, 2 * hidden), BF16),
            jax.ShapeDtypeStruct((hidden, d), BF16),
            jax.ShapeDtypeStruct(w_out.shape, BF16),
            jax.ShapeDtypeStruct((mem.shape[0], CROSS_WIDTH), F32),
            jax.ShapeDtypeStruct((mem.shape[0], CROSS_WIDTH), F32),
        ],
        scratch_shapes=[
            pltpu.VMEM((2, rows, MQK_WIDTH), BF16), pltpu.VMEM((2, rows, MQK_WIDTH), BF16),
            pltpu.VMEM((2, rows, MLSTM_WIDTH), BF16), pltpu.VMEM((2, rows, MLSTM_WIDTH), BF16),
            pltpu.VMEM((2, GATE_ROWS, rows), F32),
        ],
        compiler_params=pltpu.CompilerParams(
            dimension_semantics=("arbitrary",), vmem_limit_bytes=VMEM_LIMIT,
            allow_input_fusion=[False] + [True] * len(names) + [False] * 5),
    )(x.reshape(nbatch, seq, d), *[w[n] for n in names], w_gate, w_up, w_down, w_out, mem)
    q = q.reshape(t, ATT_WIDTH)
    k = k.reshape(t, KV_WIDTH)
    v = v.reshape(t, KV_WIDTH)

    c_st = c_st.reshape(nbatch, MLSTM_HEADS, MLSTM_DV, 2, MLSTM_DK)
    n_st = n_st.reshape(nbatch, MLSTM_HEADS, 2, MLSTM_DK)
    c_fin = jnp.stack([c_st[:, h, :, h % 2, :] for h in range(MLSTM_HEADS)], axis=1)
    n_fin = jnp.stack([n_st[:, h, h % 2, :] for h in range(MLSTM_HEADS)], axis=1)
    m_fin = m_st[:, 0, 0].reshape(nbatch, MLSTM_HEADS)
    bf16_weights = {"w_gate_up": w_gate_up_bf, "w_down": w_down_bf, "w_out": w_out_bf}
    return q, k, v, hm.reshape(t, MLSTM_WIDTH), c_fin, n_fin, m_fin, bf16_weights, memk, memv


def _mlstm_sample_kernel(mq_ref, mk_ref, mkt_ref, mv_ref, og_ref, gc_ref, gr_ref, mrow_ref, c_ref,
                         n_ref, gh_ref, hm_ref, co_ref, no_ref, mt_ref, *, nb, tq):
    rows = nb * tq
    shift = tq.bit_length() - 1
    r = lax.broadcasted_iota(jnp.int32, (rows, rows), 0)
    s = lax.broadcasted_iota(jnp.int32, (rows, rows), 1)
    same = (r >> shift) == (s >> shift)
    valid = same & (s <= r)
    tril = jnp.where(valid, 1.0, 0.0).astype(BF16)
    triu = jnp.where(same & (r <= s), 1.0, 0.0).astype(BF16)
    row_batch = lax.broadcasted_iota(jnp.int32, (rows, 1), 0) >> shift
    lane = lax.broadcasted_iota(jnp.int32, (rows, LANES), 1)

    gc = gc_ref[...]
    gr = gr_ref[...]
    btc = _cumsum_cols(tril, gc)
    btr = _cumsum_rows(gr, triu)
    m_rows = mrow_ref[...]
    mt_all = jnp.zeros((rows, LANES), F32)

    def last_of_batch(col):
        parts = [jnp.broadcast_to(col[(b + 1) * tq - 1:(b + 1) * tq], (tq, 1)) for b in range(nb)]
        return jnp.concatenate(parts, axis=0)

    dk = MLSTM_DK
    lo = _low_half((rows, LANES))
    own_q = row_batch == (lax.broadcasted_iota(jnp.int32, (rows, nb * dk), 1) >> _log2(dk))
    lane_batch = lax.broadcasted_iota(jnp.int32, (dk, rows), 1) >> shift

    def state_rows(b, h):
        return slice((b * MLSTM_HEADS + h) * dk, (b * MLSTM_HEADS + h + 1) * dk)

    for h in range(MLSTM_HEADS):
        slab = slice((h // 2) * LANES, (h // 2 + 1) * LANES)
        qs, ks = mq_ref[:, slab], mk_ref[:, slab]
        qsw, ksw = pltpu.roll(qs, dk, axis=1), pltpu.roll(ks, dk, axis=1)
        q2 = jnp.where(lo, qs, qsw) if h % 2 == 0 else jnp.where(lo, qsw, qs)
        k2 = jnp.where(lo, ks, ksw) if h % 2 == 0 else jnp.where(lo, ksw, ks)
        qb, kb = q2[:, :dk].astype(BF16), k2[:, :dk].astype(BF16)
        vb = mv_ref[:, h * MLSTM_DV:(h + 1) * MLSTM_DV].astype(BF16)
        it_c = gc[:, h:h + 1]
        bt_c = btc[:, MLSTM_HEADS + h:MLSTM_HEADS + h + 1]
        it_r = gr[h:h + 1, :]
        bt_r = btr[MLSTM_HEADS + h:MLSTM_HEADS + h + 1, :]
        m_c = m_rows[:, h:h + 1]

        num, ssum, mt, wi = _mlstm_intra(qb, kb, vb, it_c, bt_c, it_r, bt_r, m_c, valid)
        qx = jnp.where(own_q, jnp.concatenate([q2] * (nb // 2), axis=1), 0.0).astype(BF16)
        c_stack = jnp.concatenate([c_ref[state_rows(b, h), :] for b in range(nb)], axis=0)
        qc = _dot(qx, c_stack.astype(BF16))
        n_rows = jnp.concatenate(
            [jnp.broadcast_to(n_ref[b * MLSTM_HEADS + h], (tq, dk)) for b in range(nb)], axis=0)
        num = num + wi * qc
        den = ssum + wi * jnp.sum(qb.astype(F32) * n_rows, axis=-1, keepdims=True)
        og = og_ref[:, h * MLSTM_DV:(h + 1) * MLSTM_DV]
        g_head = gh_ref[:, h * MLSTM_DV:(h + 1) * MLSTM_DV]
        hm_ref[:, h * MLSTM_DV:(h + 1) * MLSTM_DV] = _mlstm_head_out(num, den, mt, og, g_head)

        b_last = last_of_batch(bt_c)
        m_new = last_of_batch(mt)
        wk = jnp.exp(b_last - bt_c + it_c - m_new)
        wc = jnp.exp(b_last + m_c - m_new)
        wkv = (wk * vb.astype(F32)).astype(BF16)
        wkk = wk * kb.astype(F32)
        kt = mkt_ref[h * dk:(h + 1) * dk, :]
        kx = jnp.concatenate([jnp.where(lane_batch == b, kt, 0.0) for b in range(nb)], axis=0)
        upd = _dot(kx.astype(BF16), wkv)
        for b in range(nb):
            idx = b * MLSTM_HEADS + h
            wc_b = wc[b * tq:b * tq + 1]
            co_ref[state_rows(b, h), :] = wc_b * c_ref[state_rows(b, h), :] + upd[b * dk:(b + 1) * dk]
            no_ref[idx] = wc_b * n_ref[idx] + jnp.sum(wkk[b * tq:(b + 1) * tq], axis=0, keepdims=True)
        mt_all = jnp.where(lane == h, mt, mt_all)
    mt_ref[...] = mt_all


def _mlstm_sample(mq, mk, mkt, mv, og, gcol, grow, m_rows, ct_in, n_in, g_head, nbatch, nb):
    t = mq.shape[0]
    tq = t // nbatch
    rows = nb * tq
    assert tq & (tq - 1) == 0 and rows == LANES and nb % 2 == 0

    def tok(width):
        return pl.BlockSpec((rows, width), lambda i: (i, 0))

    ct_spec = pl.BlockSpec((nb * MLSTM_HEADS * MLSTM_DK, MLSTM_DV), lambda i: (i, 0))
    n_spec = pl.BlockSpec((nb * MLSTM_HEADS, 1, MLSTM_DK), lambda i: (i, 0, 0))
    return pl.pallas_call(
        functools.partial(_mlstm_sample_kernel, nb=nb, tq=tq),
        grid=(nbatch // nb,),
        in_specs=[
            tok(MQK_WIDTH), tok(MQK_WIDTH),
            pl.BlockSpec((MQK_WIDTH, rows), lambda i: (0, i)),
            tok(MLSTM_WIDTH), tok(MLSTM_WIDTH), tok(LANES),
            pl.BlockSpec((GATE_ROWS, rows), lambda i: (0, i)),
            tok(LANES),
            ct_spec, n_spec,
            _const_spec((1, MLSTM_WIDTH)),
        ],
        out_specs=[tok(MLSTM_WIDTH), ct_spec, n_spec, tok(LANES)],
        out_shape=[
            jax.ShapeDtypeStruct((t, MLSTM_WIDTH), F32),
            jax.ShapeDtypeStruct(ct_in.shape, F32),
            jax.ShapeDtypeStruct(n_in.shape, F32),
            jax.ShapeDtypeStruct((t, LANES), F32),
        ],
        compiler_params=pltpu.CompilerParams(
            dimension_semantics=("arbitrary",), vmem_limit_bytes=VMEM_LIMIT,
            allow_input_fusion=[False] * 7 + [True, False, True, False]),
    )(mq, mk, mkt, mv, og, gcol, grow, m_rows, ct_in, n_in, g_head)


def _mix_stage(x, att, hm, wo_ref, gcr_ref, wcq_ref):
    cat = jnp.concatenate([att.astype(BF16), hm.astype(BF16)], axis=1)
    y1 = x + _dot(cat, wo_ref[...])
    qc = _dot(_rmsnorm(y1, gcr_ref[...]).astype(BF16), wcq_ref[...])
    return y1, qc


def _cross_attention(qc, mk, mv):
    slabs = []
    for j in range(CROSS_WIDTH // LANES):
        qs = qc[:, j * LANES:(j + 1) * LANES]
        ks = mk[:, j * LANES:(j + 1) * LANES]
        vs = mv[:, j * LANES:(j + 1) * LANES]
        n = qs.shape[0]
        lo = _low_half(qs.shape)
        zero = jnp.zeros_like(qs)
        q2 = jnp.concatenate([jnp.where(lo, qs, zero), jnp.where(lo, zero, qs)], axis=0)
        s = _dot_nt(q2 * (HEAD_DIM ** -0.5), ks)
        p = jnp.exp(s - jnp.max(s, axis=-1, keepdims=True))
        o2 = _dot(p.astype(BF16), vs) / jnp.sum(p, axis=-1, keepdims=True)
        slabs.append(jnp.where(lo, o2[:n], o2[n:]))
    return jnp.concatenate(slabs, axis=1)


FFN_CHUNK = 256


def _finish(y3, gfin_ref, final):
    return _rmsnorm(y3, gfin_ref[...]) if final else y3


def _ffn_stage(y1, o, wco_ref, gf_ref, wgu_ref, wd_ref, act_ref, between=None):
    y2 = y1 + _dot(o.astype(BF16), wco_ref[...])
    xn = _rmsnorm(y2, gf_ref[...]).astype(BF16)
    for c in range(wd_ref.shape[0] // FFN_CHUNK):
        gu = _dot(xn, wgu_ref[:, 2 * c * FFN_CHUNK:2 * (c + 1) * FFN_CHUNK])
        g, u = gu[:, :FFN_CHUNK], gu[:, FFN_CHUNK:]
        act_ref[:, c * FFN_CHUNK:(c + 1) * FFN_CHUNK] = (g * jax.nn.sigmoid(g) * u).astype(BF16)
        if between is not None:
            between()
    return y2 + _dot(act_ref[...], wd_ref[...])


def _post_kernel(sink_ref, q_ref, kc_ref, kp_ref, vc_ref, vp_ref, x_ref, hm_ref, mk_ref, mv_ref,
                 xd_ref, attd_ref, hmd_ref, mkd_ref, mvd_ref,
                 wo_ref, gcr_ref, wcq_ref, wco_ref, gf_ref, wgu_ref, wd_ref, gfin_ref,
                 o_ref, od_ref, act_ref, att_ref, y3_ref, *, final, ntiles, tiles_per_seq, tq):
    tm = x_ref.shape[0]
    s = pl.program_id(0)
    tile = jnp.minimum(s, ntiles - 1)
    first_block = (tile % tiles_per_seq) == 0
    slot = s % 2

    def swa_stages():
        def store(j, value):
            att_ref[slot, j * WINDOW:(j + 1) * WINDOW, :] = value.astype(att_ref.dtype)
        return _swa_prompt_stages(sink_ref, q_ref, kc_ref, kp_ref, vc_ref, vp_ref, first_block, store)

    def finish():
        done = _finish(y3_ref[...], gfin_ref, final)
        o_ref[...] = done[:tm]
        od_ref[...] = done[tm:]

    @pl.when(s == 0)
    def _():
        y3_ref[...] = jnp.zeros_like(y3_ref)
        for _ in swa_stages():
            pass

    @pl.when(jnp.logical_and(s > 0, s <= ntiles))
    def _():
        swa = swa_stages()

        def advance(n=SWA_STAGES_PER_FFN_CHUNK):
            for _ in range(n):
                next(swa, None)

        finish()
        x = jnp.concatenate([x_ref[...], xd_ref[...]], axis=0)
        att = jnp.concatenate([att_ref[1 - slot], attd_ref[...].astype(BF16)], axis=0)
        hm = jnp.concatenate([hm_ref[...], hmd_ref[...].astype(BF16)], axis=0)
        y1, qc = _mix_stage(x, att, hm, wo_ref, gcr_ref, wcq_ref)
        advance()
        o = jnp.concatenate([
            _cross_attention(qc[:tm].astype(BF16), mk_ref[...].astype(BF16), mv_ref[...].astype(BF16)),
            _cross_attention_decode(qc[tm:], mkd_ref, mvd_ref, tq)], axis=0)
        advance()
        y3_ref[...] = _ffn_stage(y1, o, wco_ref, gf_ref, wgu_ref, wd_ref, act_ref, between=advance)
        for _ in swa:
            pass

    @pl.when(s > ntiles)
    def _():
        finish()


_MIX_WEIGHTS = ("w_out", "g_cross", "w_cq")
_FFN_WEIGHTS = ("w_co", "g_ffn", "w_gate_up", "w_down", "g_final")


def _post(x, q, k, v, hm, memk, memv, xd, attd, hmd, memkd_t, memvd_t, sinks, w, nbatch, nseq, final, tm):
    t, d = x.shape
    ntiles = t // tm
    tpb = ntiles // nbatch
    wpt = tm // WINDOW
    m = memk.shape[0] // nbatch
    hidden = w["w_down"].shape[0]
    td = xd.shape[0]
    tq = td // nseq
    slab = td // ntiles
    assert td % ntiles == 0 and slab % tq == 0 and slab % BF16_SUBLANES == 0
    mem_d = memkd_t.shape[1]

    def attn_tile(s):
        return jnp.minimum(s, ntiles - 1)

    def post_tile(s):
        return jnp.clip(s - 1, 0, ntiles - 1)

    def done_tile(s):
        return jnp.maximum(s - 2, 0)

    def attn_rows(width):
        return pl.BlockSpec((tm, width), lambda s: (attn_tile(s), 0))

    def prev_window(width):
        return pl.BlockSpec((WINDOW, width), lambda s: (jnp.maximum(attn_tile(s) * wpt - 1, 0), 0))

    def post_rows(width):
        return pl.BlockSpec((tm, width), lambda s: (post_tile(s), 0))

    def post_slab(width):
        return pl.BlockSpec((slab, width), lambda s: (post_tile(s), 0))

    mem_spec = pl.BlockSpec((m, CROSS_WIDTH), lambda s: (post_tile(s) // tpb, 0))
    memd_spec = pl.BlockSpec((slab // tq * CROSS_WIDTH, mem_d), lambda s: (post_tile(s), 0))
    names = _MIX_WEIGHTS + _FFN_WEIGHTS
    return pl.pallas_call(
        functools.partial(_post_kernel, final=final, ntiles=ntiles, tiles_per_seq=tpb, tq=tq),
        grid=(ntiles + 2,),
        in_specs=[
            pl.BlockSpec(memory_space=pltpu.SMEM),
            attn_rows(ATT_WIDTH), attn_rows(KV_WIDTH), prev_window(KV_WIDTH),
            attn_rows(KV_WIDTH), prev_window(KV_WIDTH),
            post_rows(d), post_rows(MLSTM_WIDTH), mem_spec, mem_spec,
            post_slab(d), post_slab(ATT_WIDTH), post_slab(MLSTM_WIDTH), memd_spec, memd_spec,
        ] + _weight_specs(w, names),
        out_specs=[pl.BlockSpec((tm, d), lambda s: (done_tile(s), 0)),
                   pl.BlockSpec((slab, d), lambda s: (done_tile(s), 0))],
        out_shape=[jax.ShapeDtypeStruct((t, d), F32), jax.ShapeDtypeStruct((td, d), F32)],
        scratch_shapes=[pltpu.VMEM((tm + slab, hidden), BF16), pltpu.VMEM((2, tm, ATT_WIDTH), BF16),
                        pltpu.VMEM((tm + slab, d), F32)],
        compiler_params=pltpu.CompilerParams(
            dimension_semantics=("arbitrary",), vmem_limit_bytes=VMEM_LIMIT),
    )(sinks, q, k, k, v, v, x, hm, memk, memv, xd, attd, hmd, memkd_t, memvd_t,
      *[w[n] for n in names])


def _cross_attention_decode(qc, mk_ref, mv_ref, tq):
    n = CROSS_HEADS * tq
    row_head = lax.broadcasted_iota(jnp.int32, (n, CROSS_WIDTH), 0) >> _log2(tq)
    lane_head = lax.broadcasted_iota(jnp.int32, (n, CROSS_WIDTH), 1) >> _log2(HEAD_DIM)
    own = row_head == lane_head
    outs = []
    for b in range(qc.shape[0] // tq):
        q = qc[b * tq:(b + 1) * tq, :]
        qx = jnp.where(own, jnp.concatenate([q] * CROSS_HEADS, axis=0), 0.0).astype(BF16)
        mk = mk_ref[b * CROSS_WIDTH:(b + 1) * CROSS_WIDTH, :].astype(BF16)
        mv = mv_ref[b * CROSS_WIDTH:(b + 1) * CROSS_WIDTH, :].astype(BF16)
        s = _dot(qx, mk) * (HEAD_DIM ** -0.5)
        p = jnp.exp(s - jnp.max(s, axis=-1, keepdims=True))
        pv = _dot_nt(p.astype(BF16), mv) / jnp.sum(p, axis=-1, keepdims=True)
        pv = jnp.where(own, pv, 0.0)
        out = pv[0:tq]
        for h in range(1, CROSS_HEADS):
            out = out + pv[h * tq:(h + 1) * tq]
        outs.append(out)
    return jnp.concatenate(outs, axis=0)


def _layer_weights(l, w_in, b_igate, b_fgate, g_mlstm_head, g_mix, g_cross, g_mem, w_cq, w_ck, w_cv,
                   w_co, g_ffn, g_final):
    gate_w = w_in[l][:, MAIN_WIDTH:]
    gate_b = jnp.concatenate([b_igate[l], b_fgate[l]]).astype(F32)
    ngate = gate_b.shape[0]
    row = lambda a: a.astype(F32).reshape(1, -1)
    w_t = w_in[l].T
    k0 = ATT_WIDTH
    mk0 = ATT_WIDTH + 2 * KV_WIDTH + MQK_WIDTH
    return {
        "w_kvk_t": jnp.concatenate([w_t[k0:k0 + 2 * KV_WIDTH], w_t[mk0:mk0 + MQK_WIDTH]]).astype(BF16),
        "w_main": w_t[:MAIN_WIDTH].astype(BF16),
        "w_gate_cols": jnp.pad(gate_w, ((0, 0), (0, LANES - ngate))).astype(BF16),
        "w_gate_rows": jnp.pad(gate_w.T, ((0, GATE_ROWS - ngate), (0, 0))).astype(BF16),
        "b_gate_cols": jnp.pad(gate_b, (0, LANES - ngate)).reshape(1, LANES),
        "b_gate_rows": jnp.pad(gate_b, (0, GATE_ROWS - ngate)).reshape(GATE_ROWS, 1),
        "g_mix": row(g_mix[l]), "g_cross": row(g_cross[l]), "g_mem": row(g_mem[l]),
        "g_ffn": row(g_ffn[l]), "g_final": row(g_final), "g_head": row(g_mlstm_head[l]),
        "w_cq": w_cq[l].astype(BF16), "w_ck": w_ck[l].astype(BF16),
        "w_cv": w_cv[l].astype(BF16), "w_co": w_co[l].astype(BF16),
    }


TOKEN_TILE = 512
MLSTM_STAGES_PER_PROJ_STAGE = 15
SWA_STAGES_PER_FFN_CHUNK = 3
MLSTM_CHUNK_ROWS = 256
SWA_SAMPLE_BATCH = 16
MLSTM_SAMPLE_BATCH = 16


def kernel(x_prompt, x_sample, mem_prompt, cache_swa_k, cache_swa_v, state_mlstm_C, state_mlstm_n,
           state_mlstm_m, cache_mem_k, cache_mem_v, w_in, b_igate, b_fgate, attn_sinks,
           g_mlstm_head, w_out, g_mix, g_cross, g_mem, w_cq, w_ck, w_cv, w_co, g_ffn, w_gate,
           w_up, w_down, g_final):
    depth = w_in.shape[0]
    bp, sp, d = x_prompt.shape
    bs, ss, _ = x_sample.shape
    mem_tokens = mem_prompt.shape[1]
    past = cache_swa_k.shape[2]
    yp = x_prompt.reshape(bp * sp, d)
    ys = x_sample.reshape(bs * ss, d)
    mem = mem_prompt.reshape(bp * mem_tokens, d)
    outs = [[] for _ in range(12)]

    for l in range(depth):
        final = l == depth - 1
        w = _layer_weights(l, w_in, b_igate, b_fgate, g_mlstm_head, g_mix, g_cross, g_mem, w_cq,
                           w_ck, w_cv, w_co, g_ffn, g_final)
        sinks = attn_sinks[l].astype(F32)

        q, k, v, hm, c_p, n_p, m_p, cast, memk, memv = _mixer_prompt(
            yp, mem, w, w_gate[l].astype(F32), w_up[l].astype(F32), w_down[l].astype(F32),
            w_out[l].astype(F32), bp, MLSTM_CHUNK_ROWS)
        w.update(cast)
        win_shape = (bp, WINDOW, ATT_KV_HEADS, HEAD_DIM)
        outs[0].append(k.reshape(bp, sp, KV_WIDTH)[:, sp - WINDOW:].reshape(win_shape))
        outs[1].append(v.reshape(bp, sp, KV_WIDTH)[:, sp - WINDOW:].reshape(win_shape))
        outs[2].append(c_p)
        outs[3].append(n_p)
        outs[4].append(m_p)
        mem_shape = (bp, mem_tokens, CROSS_HEADS, HEAD_DIM)
        outs[5].append(memk.reshape(mem_shape))
        outs[6].append(memv.reshape(mem_shape))

        def tokens_last(a):
            return jnp.transpose(a.astype(F32), (0, 2, 3, 1)).reshape(-1, a.shape[1])

        qd, mq, mk, mv, og, grow, gcol, knt, vnt, mkt = _decode_proj(ys, w, TOKEN_TILE)
        att_d, kbuf_t, vbuf_t = _swa_sample(
            qd, knt, vnt, tokens_last(cache_swa_k[l]), tokens_last(cache_swa_v[l]), sinks, bs,
            SWA_SAMPLE_BATCH)
        m_rows = jnp.pad(jnp.repeat(state_mlstm_m[l].astype(F32), ss, axis=0),
                         ((0, 0), (0, LANES - MLSTM_HEADS)))
        ct_in = jnp.swapaxes(state_mlstm_C[l].astype(F32), 2, 3).reshape(-1, MLSTM_DV)
        hm_d, ct_s, n_s, mt = _mlstm_sample(
            mq, mk, mkt, mv, og, gcol, grow, m_rows, ct_in,
            state_mlstm_n[l].astype(F32).reshape(bs * MLSTM_HEADS, 1, MLSTM_DK),
            w["g_head"], bs, MLSTM_SAMPLE_BATCH)

        yp, ys = _post(yp, q, k, v, hm, memk, memv, ys, att_d, hm_d, tokens_last(cache_mem_k[l]),
                       tokens_last(cache_mem_v[l]), sinks, w, bp, bs, final, TOKEN_TILE)
        buf_t_shape = (bs, ATT_KV_HEADS, HEAD_DIM, past)
        outs[7].append(jnp.transpose(kbuf_t.reshape(buf_t_shape), (0, 3, 1, 2)))
        outs[8].append(jnp.transpose(vbuf_t.reshape(buf_t_shape), (0, 3, 1, 2)))
        outs[9].append(jnp.swapaxes(ct_s.reshape(bs, MLSTM_HEADS, MLSTM_DK, MLSTM_DV), 2, 3))
        outs[10].append(n_s.reshape(bs, MLSTM_HEADS, MLSTM_DK))
        outs[11].append(mt.reshape(bs, ss, LANES)[:, ss - 1, :MLSTM_HEADS])

    return (yp.reshape(bp, sp, d), ys.reshape(bs, ss, d)) + tuple(jnp.stack(o) for o in outs)
```

```python
import functools

import jax
import jax.numpy as jnp
from jax import lax
from jax.experimental import pallas as pl
from jax.experimental.pallas import tpu as pltpu

F32 = jnp.float32
BF16 = jnp.bfloat16

HEAD_DIM = 64
ATT_GROUP = 4
ATT_KV_HEADS = 2
ATT_WIDTH = 512
KV_WIDTH = ATT_KV_HEADS * HEAD_DIM
WINDOW = 128
MLSTM_HEADS = 4
MLSTM_DV = 128
MLSTM_DK = 64
MQK_WIDTH = MLSTM_HEADS * MLSTM_DK
MLSTM_WIDTH = MLSTM_HEADS * MLSTM_DV
MAIN_WIDTH = ATT_WIDTH + 2 * KV_WIDTH + 2 * MQK_WIDTH + 2 * MLSTM_WIDTH
CROSS_HEADS = 4
CROSS_WIDTH = CROSS_HEADS * HEAD_DIM
EPS = 1e-6
NEG_INF = float("-inf")
assert HEAD_DIM == 4 ** 3

LANES = 128
F32_SUBLANES = 8
BF16_SUBLANES = 16
GATE_ROWS = BF16_SUBLANES
VMEM_LIMIT = 56 * 1024 * 1024

NT = (((1,), (1,)), ((), ()))


def _dot(a, b):
    return jnp.dot(a, b, preferred_element_type=F32)


def _dot_nt(a, b):
    return lax.dot_general(a, b, NT, preferred_element_type=F32)


def _rmsnorm(x, g):
    return x * lax.rsqrt(jnp.mean(x * x, axis=-1, keepdims=True) + EPS) * g


def _log_sigmoid(x):
    return jnp.minimum(x, 0.0) - jnp.log1p(jnp.exp(-jnp.abs(x)))


def _split3(x):
    hi = x.astype(BF16)
    r1 = x - hi.astype(F32)
    mid = r1.astype(BF16)
    lo = (r1 - mid.astype(F32)).astype(BF16)
    return hi, mid, lo


def _cumsum_cols(tri, x):
    hi, mid, lo = _split3(x)
    return _dot(tri, hi) + _dot(tri, mid) + _dot(tri, lo)


def _cumsum_rows(x, tri):
    hi, mid, lo = _split3(x)
    return _dot(hi, tri) + _dot(mid, tri) + _dot(lo, tri)


def _log2(n):
    assert n > 0 and n & (n - 1) == 0, n
    return n.bit_length() - 1


def _low_half(shape):
    return lax.broadcasted_iota(jnp.int32, shape, 1) < HEAD_DIM


def _decode_proj_kernel(x_ref, g_ref, w_ref, wgr_ref, br_ref, wgc_ref, bc_ref, wt_ref,
                        q_ref, mq_ref, mk_ref, mv_ref, og_ref, gr_ref, gc_ref, kt_ref, vt_ref, mkt_ref):
    xn = _rmsnorm(x_ref[...], g_ref[...]).astype(BF16)
    zt = _dot_nt(wt_ref[...], xn)
    kt_ref[...] = zt[:KV_WIDTH]
    vt_ref[...] = zt[KV_WIDTH:2 * KV_WIDTH]
    mkt_ref[...] = zt[2 * KV_WIDTH:] * (MLSTM_DK ** -0.5)

    def mm(lo, hi):
        return _dot_nt(xn, w_ref[lo:hi, :])

    o = 0
    q_ref[...] = mm(o, o + ATT_WIDTH)
    o += ATT_WIDTH
    mqk = mm(o, o + 2 * MQK_WIDTH)
    mq_ref[...] = mqk[:, :MQK_WIDTH]
    mk_ref[...] = mqk[:, MQK_WIDTH:] * (MLSTM_DK ** -0.5)
    o += 2 * MQK_WIDTH
    mv_ref[...] = mm(o, o + MLSTM_WIDTH)
    o += MLSTM_WIDTH
    og_ref[...] = mm(o, o + MLSTM_WIDTH)

    zc = _dot(xn, wgc_ref[...]) + bc_ref[...]
    lane = lax.broadcasted_iota(jnp.int32, zc.shape, 1)
    gc_ref[...] = jnp.where(lane < MLSTM_HEADS, zc, _log_sigmoid(zc))
    zr = _dot_nt(wgr_ref[...], xn) + br_ref[...]
    row = lax.broadcasted_iota(jnp.int32, zr.shape, 0)
    gr_ref[...] = jnp.where(row < MLSTM_HEADS, zr, _log_sigmoid(zr))


def _const_spec(shape):
    nd = len(shape)
    return pl.BlockSpec(shape, lambda *_: (0,) * nd, pipeline_mode=pl.Buffered(1))


def _weight_specs(w, names):
    return [_const_spec(w[n].shape) for n in names]


def _decode_proj(x, w, tm):
    t, d = x.shape

    def rows(width):
        return pl.BlockSpec((tm, width), lambda i: (i, 0))

    def lanes(height):
        return pl.BlockSpec((height, tm), lambda i: (0, i))

    outs = [(rows, ATT_WIDTH), (rows, MQK_WIDTH), (rows, MQK_WIDTH), (rows, MLSTM_WIDTH),
            (rows, MLSTM_WIDTH), (lanes, GATE_ROWS), (rows, LANES), (lanes, KV_WIDTH),
            (lanes, KV_WIDTH), (lanes, MQK_WIDTH)]
    names = ["g_mix", "w_main_decode", "w_gate_rows", "b_gate_rows", "w_gate_cols", "b_gate_cols", "w_kvk_t"]
    return pl.pallas_call(
        _decode_proj_kernel,
        grid=(t // tm,),
        in_specs=[rows(d)] + _weight_specs(w, names),
        out_specs=[kind(size) for kind, size in outs],
        out_shape=[jax.ShapeDtypeStruct((t, size) if kind is rows else (size, t), F32)
                   for kind, size in outs],
        compiler_params=pltpu.CompilerParams(
            dimension_semantics=("arbitrary",), vmem_limit_bytes=VMEM_LIMIT,
            allow_input_fusion=[False] + [True] * len(names)),
    )(x, *[w[n] for n in names])


def _dup_heads(x):
    swapped = pltpu.roll(x, HEAD_DIM, axis=1)
    lo = _low_half(x.shape)
    return (jnp.where(lo, x, swapped).astype(BF16), jnp.where(lo, swapped, x).astype(BF16))


def _stack_query_heads(q, h):
    parts = []
    for g in range(ATT_GROUP):
        hq = h * ATT_GROUP + g
        slab = q[:, (hq // 2) * LANES:(hq // 2 + 1) * LANES]
        lo = _low_half(slab.shape)
        keep = lo if hq % 2 == 0 else jnp.logical_not(lo)
        parts.append(jnp.where(keep, slab, jnp.zeros_like(slab)))
    return jnp.concatenate(parts, axis=0).astype(BF16)


def _sink_column(sink_ref, h, rows_per_head):
    n = ATT_GROUP * rows_per_head
    grp = lax.broadcasted_iota(jnp.int32, (n, 1), 0) >> _log2(rows_per_head)
    col = jnp.full((n, 1), sink_ref[h * ATT_GROUP], F32)
    for g in range(1, ATT_GROUP):
        col = jnp.where(grp == g, sink_ref[h * ATT_GROUP + g], col)
    return col


def _swa_prompt_stages(sink_ref, q_ref, kc_ref, kp_ref, vc_ref, vp_ref, first_block, store):
    qb = q_ref.shape[0]
    k_all = jnp.concatenate([kp_ref[...], kc_ref[...]], axis=0)
    v_all = jnp.concatenate([vp_ref[...], vc_ref[...]], axis=0)
    kd = _dup_heads(k_all)
    v_t = v_all.T.astype(BF16)

    n = ATT_GROUP * WINDOW
    key = lax.broadcasted_iota(jnp.int32, (2 * WINDOW, n), 0)
    qry = lax.broadcasted_iota(jnp.int32, (2 * WINDOW, n), 1) & (WINDOW - 1)
    delta = key - qry
    in_band = (delta >= 0) & (delta <= WINDOW)
    grp = lax.broadcasted_iota(jnp.int32, (1, n), 1) >> _log2(WINDOW)
    sinks = []
    for h in range(ATT_KV_HEADS):
        row = jnp.full((1, n), sink_ref[h * ATT_GROUP], F32)
        for g in range(1, ATT_GROUP):
            row = jnp.where(grp == g, sink_ref[h * ATT_GROUP + g], row)
        sinks.append(row)
    yield

    probs = []
    for j in range(qb // WINDOW):
        valid = in_band & (key >= jnp.where(first_block, WINDOW, 0)) if j == 0 else in_band
        q = q_ref[j * WINDOW:(j + 1) * WINDOW, :] * (HEAD_DIM ** -0.5)
        band = slice(j * WINDOW, (j + 2) * WINDOW)
        for h in range(ATT_KV_HEADS):
            qh = _stack_query_heads(q, h)
            s = _dot_nt(kd[h][band], qh)
            probs.append(dict(h=h, band=band, s=jnp.where(valid, s, NEG_INF)))
            yield
    for p in probs:
        p["m"] = jnp.maximum(jnp.max(p["s"], axis=0, keepdims=True), sinks[p["h"]])
        yield
    for p in probs:
        p["p"] = jnp.exp(p["s"] - p["m"])
        yield
    for p in probs:
        h = p["h"]
        denom = jnp.sum(p["p"], axis=0, keepdims=True) + jnp.exp(sinks[h] - p["m"])
        p["o_t"] = _dot(v_t[h * HEAD_DIM:(h + 1) * HEAD_DIM, p["band"]], p["p"].astype(BF16)) / denom
        yield
    for j in range(qb // WINDOW):
        pieces = [p["o_t"][:, g * WINDOW:(g + 1) * WINDOW]
                  for p in probs[j * ATT_KV_HEADS:(j + 1) * ATT_KV_HEADS] for g in range(ATT_GROUP)]
        store(j, jnp.concatenate(pieces, axis=0).T)
        yield


def _swa_sample_kernel(sink_ref, q_ref, knt_ref, vnt_ref, ck_ref, cv_ref,
                       o_ref, kbuf_ref, vbuf_ref, *, nb, tq):
    past = ck_ref.shape[1]
    hw = ATT_KV_HEADS * HEAD_DIM
    heads = ATT_KV_HEADS * ATT_GROUP
    n = heads * tq
    fresh = past - tq
    t = lax.broadcasted_iota(jnp.int32, (n, 2 * past), 0) & (tq - 1)
    col = lax.broadcasted_iota(jnp.int32, (n, 2 * past), 1)
    valid = ((col < past) & (col >= t)) | ((col >= past + fresh) & (col - (past + fresh) <= t))
    sink = jnp.concatenate([_sink_column(sink_ref, h, tq) for h in range(ATT_KV_HEADS)], axis=0)
    lane = lax.broadcasted_iota(jnp.int32, (hw, past), 1)
    knt = knt_ref[...]
    vnt = vnt_ref[...]
    zero_slab = jnp.zeros((tq, LANES), F32)
    lo = _low_half((tq, LANES))

    def shifted(old, new_t, b):
        return jnp.where(lane >= fresh, pltpu.roll(new_t, (fresh - b * tq) % past, axis=1),
                         pltpu.roll(old, fresh, axis=1))

    def dup_rows(x):
        h0, h1 = x[:HEAD_DIM], x[HEAD_DIM:]
        return jnp.concatenate([h0, h0, h1, h1], axis=0)

    for b in range(nb):
        k_old = ck_ref[b * hw:(b + 1) * hw, :]
        v_old = cv_ref[b * hw:(b + 1) * hw, :]
        k_new = shifted(k_old, knt, b)
        v_new = shifted(v_old, vnt, b)
        kbuf_ref[b * hw:(b + 1) * hw, :] = k_new
        vbuf_ref[b * hw:(b + 1) * hw, :] = v_new
        kop = jnp.concatenate([dup_rows(k_old), dup_rows(k_new)], axis=1).astype(BF16)
        vop = jnp.concatenate([dup_rows(v_old), dup_rows(v_new)], axis=1).astype(BF16)

        q = q_ref[b * tq:(b + 1) * tq, :]
        blocks = []
        for hq in range(heads):
            slab = q[:, (hq // 2) * LANES:(hq // 2 + 1) * LANES]
            slab = jnp.where(lo if hq % 2 == 0 else jnp.logical_not(lo), slab, 0.0)
            pair = [slab, zero_slab] if hq // ATT_GROUP == 0 else [zero_slab, slab]
            blocks.append(jnp.concatenate(pair, axis=1))
        qx = jnp.concatenate(blocks, axis=0).astype(BF16)

        s = _dot(qx, kop) * (HEAD_DIM ** -0.5)
        s = jnp.where(valid, s, NEG_INF)
        m = jnp.maximum(jnp.max(s, axis=-1, keepdims=True), sink)
        p = jnp.exp(s - m)
        denom = jnp.sum(p, axis=-1, keepdims=True) + jnp.exp(sink - m)
        o = _dot_nt(p.astype(BF16), vop) / denom

        slabs = []
        for j in range(heads // 2):
            h = (2 * j) // ATT_GROUP
            even = o[(2 * j) * tq:(2 * j + 1) * tq, h * LANES:(h + 1) * LANES]
            odd = o[(2 * j + 1) * tq:(2 * j + 2) * tq, h * LANES:(h + 1) * LANES]
            slabs.append(jnp.where(lo, even, odd))
        o_ref[b * tq:(b + 1) * tq, :] = jnp.concatenate(slabs, axis=1)


def _swa_sample(q, knt, vnt, cache_kt, cache_vt, sinks, nbatch, nb):
    t = q.shape[0]
    tq = t // nbatch
    past = cache_kt.shape[1]
    hw = ATT_KV_HEADS * HEAD_DIM
    assert past == WINDOW and tq & (tq - 1) == 0 and nb * tq == past

    def rows(r, width):
        return pl.BlockSpec((r, width), lambda i: (i, 0))

    new_t = pl.BlockSpec((hw, nb * tq), lambda i: (0, i))
    return pl.pallas_call(
        functools.partial(_swa_sample_kernel, nb=nb, tq=tq),
        grid=(nbatch // nb,),
        in_specs=[
            pl.BlockSpec(memory_space=pltpu.SMEM),
            rows(nb * tq, ATT_WIDTH), new_t, new_t, rows(nb * hw, past), rows(nb * hw, past),
        ],
        out_specs=[rows(nb * tq, ATT_WIDTH), rows(nb * hw, past), rows(nb * hw, past)],
        out_shape=[
            jax.ShapeDtypeStruct((t, ATT_WIDTH), F32),
            jax.ShapeDtypeStruct(cache_kt.shape, F32),
            jax.ShapeDtypeStruct(cache_vt.shape, F32),
        ],
        compiler_params=pltpu.CompilerParams(
            dimension_semantics=("arbitrary",), vmem_limit_bytes=VMEM_LIMIT),
    )(sinks, q, knt, vnt, cache_kt, cache_vt)


def _mlstm_intra(q, k, v, it_c, bt_c, it_r, bt_r, m_c, valid):
    d = jnp.where(valid, bt_c - bt_r + it_r, NEG_INF)
    inter = bt_c + m_c
    mt = jnp.maximum(jnp.max(d, axis=-1, keepdims=True), inter)
    sm = _dot_nt(q, k) * jnp.exp(d - mt)
    wi = jnp.exp(inter - mt)
    return _dot(sm.astype(BF16), v), jnp.sum(sm, axis=-1, keepdims=True), mt, wi


def _mlstm_head_out(num, den, mt, og, g_head):
    lower = jnp.maximum(jnp.abs(den), jnp.exp(-mt))
    h = num / lower
    hn = h * lax.rsqrt(jnp.mean(h * h, axis=-1, keepdims=True) + EPS)
    return hn * g_head * jax.nn.sigmoid(og)


def _mlstm_prompt_stages(mq, mk, mv, og, gr, gh_ref, hm_ref, c_ref, n_ref, m_ref, nbatch, chunk):
    s_idx = lax.broadcasted_iota(jnp.int32, (chunk, chunk), 0)
    l_idx = lax.broadcasted_iota(jnp.int32, (chunk, chunk), 1)
    causal = s_idx <= l_idx
    triu = jnp.where(causal, 1.0, 0.0).astype(BF16)
    lo = _low_half((chunk, LANES))
    pad_rows = jnp.zeros((LANES - MLSTM_HEADS, chunk), F32)
    sub8 = (GATE_ROWS, chunk)

    probs = []
    for b in range(nbatch):
        g = gr(b)
        btr = _cumsum_rows(g, triu)
        it_rows = g[:MLSTM_HEADS]
        bt_rows = btr[MLSTM_HEADS:2 * MLSTM_HEADS]
        u_cols = jnp.concatenate([it_rows - bt_rows, pad_rows], axis=0).T
        for h in range(MLSTM_HEADS):
            slab = slice((h // 2) * LANES, (h // 2 + 1) * LANES)
            keep = lo if h % 2 == 0 else jnp.logical_not(lo)
            q = mq(b, slab)
            k = mk(b, slab)
            probs.append(dict(
                b=b, h=h, idx=b * MLSTM_HEADS + h, cols=slice(h * MLSTM_DV, (h + 1) * MLSTM_DV),
                q=jnp.where(keep, q, jnp.zeros_like(q)), k=jnp.where(keep, k, jnp.zeros_like(k)),
                u_c=u_cols[:, h:h + 1], it_r=it_rows[h:h + 1, :], bt_r=bt_rows[h:h + 1, :]))
        yield

    for p in probs:
        p["m_old"] = m_ref[p["idx"]][:, 0:1]
        p["d"] = jnp.where(causal, p["bt_r"] + p["u_c"], NEG_INF)
        p["inter"] = p["bt_r"] + p["m_old"]
        p["q_t"] = p["q"].astype(F32).T.astype(BF16)
        p["kq"] = _dot(p["k"], p["q_t"])
        p["v_t"] = mv(p["b"], p["cols"]).astype(F32).T
        yield
    for p in probs:
        p["mt"] = jnp.maximum(jnp.max(p["d"], axis=0, keepdims=True), p["inter"])
        yield
    for p in probs:
        p["sm"] = p["kq"] * jnp.exp(p["d"] - p["mt"])
        p["wi"] = jnp.exp(p["inter"] - p["mt"])
        yield
    for p in probs:
        c_old = c_ref[p["idx"]]
        n_old = jnp.broadcast_to(n_ref[p["idx"]], (sub8[0], LANES))
        num = (_dot(p["v_t"].astype(BF16), p["sm"].astype(BF16))
               + p["wi"] * _dot(c_old.astype(BF16), p["q_t"]))
        den = (jnp.sum(p["sm"], axis=0, keepdims=True)
               + p["wi"] * _dot(n_old.astype(BF16), p["q_t"])[0:1])
        lower = jnp.maximum(jnp.abs(den), jnp.exp(-p["mt"]))
        h_t = num / lower
        p["hn"] = (h_t * lax.rsqrt(jnp.mean(h_t * h_t, axis=0, keepdims=True) + EPS)).T
        yield
    for p in probs:
        gate = jax.nn.sigmoid(og(p["b"], p["cols"]).astype(F32))
        hm_ref[p["b"], :, p["cols"]] = (p["hn"] * gh_ref[:, p["cols"]] * gate).astype(hm_ref.dtype)
        yield
    for p in probs:
        idx = p["idx"]
        b_last = p["bt_r"][:, chunk - 1:chunk]
        m_new = p["mt"][:, chunk - 1:chunk]
        wk = jnp.exp(b_last - p["bt_r"] + p["it_r"] - m_new)
        wc = jnp.exp(b_last + p["m_old"] - m_new)
        c_ref[idx] = wc * c_ref[idx] + _dot((p["v_t"] * wk).astype(BF16), p["k"])
        n_ref[idx] = wc * n_ref[idx] + _dot(jnp.broadcast_to(wk, sub8).astype(BF16), p["k"])[0:1]
        m_ref[idx] = jnp.broadcast_to(m_new, (1, LANES))
        yield


def _mixer_prompt_kernel(x_ref, g_ref, w_ref, wgr_ref, br_ref, gh_ref, gmem_ref, wck_ref, wcv_ref,
                         wg_ref, wu_ref, wd_ref, wo_ref, mem_ref,
                         q_ref, k_ref, v_ref, hm_ref, c_ref, n_ref, m_ref, wgu_out, wd_out, wo_out,
                         memk_out, memv_out,
                         mq_scr, mk_scr, mv_scr, og_scr, gr_scr, *, nbatch, chunk):
    s = pl.program_id(0)
    wr = s % 2
    rd = 1 - wr
    rows = nbatch * chunk

    def side_work():
        for c in range(0, wg_ref.shape[1], FFN_CHUNK):
            wgu_out[:, 2 * c:2 * c + FFN_CHUNK] = wg_ref[:, c:c + FFN_CHUNK].astype(BF16)
            wgu_out[:, 2 * c + FFN_CHUNK:2 * (c + FFN_CHUNK)] = wu_ref[:, c:c + FFN_CHUNK].astype(BF16)
        wd_out[...] = wd_ref[...].astype(BF16)
        wo_out[...] = wo_ref[...].astype(BF16)
        mn = _rmsnorm(mem_ref[...], gmem_ref[...]).astype(BF16)
        memk_out[...] = _dot(mn, wck_ref[...])
        memv_out[...] = _dot(mn, wcv_ref[...])

    def proj_stages():
        xn = _rmsnorm(x_ref[...].reshape(rows, x_ref.shape[-1]), g_ref[...]).astype(BF16)
        yield

        def mm(lo, hi):
            return _dot_nt(xn, w_ref[lo:hi, :])

        o = 0
        q_ref[...] = mm(o, o + ATT_WIDTH).astype(q_ref.dtype).reshape(q_ref.shape)
        yield
        o += ATT_WIDTH
        kv = mm(o, o + 2 * KV_WIDTH)
        k_ref[...] = kv[:, :KV_WIDTH].reshape(k_ref.shape)
        v_ref[...] = kv[:, KV_WIDTH:].reshape(v_ref.shape)
        yield
        o += 2 * KV_WIDTH
        mqk = mm(o, o + 2 * MQK_WIDTH)
        mq_scr[wr] = mqk[:, :MQK_WIDTH].astype(BF16)
        mk_scr[wr] = (mqk[:, MQK_WIDTH:] * (MLSTM_DK ** -0.5)).astype(BF16)
        yield
        o += 2 * MQK_WIDTH
        mv_scr[wr] = mm(o, o + MLSTM_WIDTH).astype(BF16)
        yield
        o += MLSTM_WIDTH
        og_scr[wr] = mm(o, o + MLSTM_WIDTH).astype(BF16)
        yield
        zr = _dot_nt(wgr_ref[...], xn) + br_ref[...]
        row = lax.broadcasted_iota(jnp.int32, zr.shape, 0)
        gr_scr[wr] = jnp.where(row < MLSTM_HEADS, zr, _log_sigmoid(zr))
        yield

    def mlstm_stages():
        def rows_of(scr):
            return lambda b, lanes: scr[rd, b * chunk:(b + 1) * chunk, lanes]
        return _mlstm_prompt_stages(
            rows_of(mq_scr), rows_of(mk_scr), rows_of(mv_scr), rows_of(og_scr),
            lambda b: gr_scr[rd, :, b * chunk:(b + 1) * chunk],
            gh_ref, hm_ref, c_ref, n_ref, m_ref, nbatch, chunk)

    @pl.when(s == 0)
    def _():
        c_ref[...] = jnp.zeros_like(c_ref)
        n_ref[...] = jnp.zeros_like(n_ref)
        m_ref[...] = jnp.zeros_like(m_ref)
        side_work()
        for _ in proj_stages():
            pass

    @pl.when(s > 0)
    def _():
        side_work()
        mlstm = mlstm_stages()
        for _ in proj_stages():
            for _ in range(MLSTM_STAGES_PER_PROJ_STAGE):
                next(mlstm, None)
        for _ in mlstm:
            pass


def _mixer_prompt(x, mem, w, w_gate, w_up, w_down, w_out, nbatch, chunk):
    t, d = x.shape
    seq = t // nbatch
    nchunks = seq // chunk
    nstate = nbatch * MLSTM_HEADS
    rows = nbatch * chunk
    hidden = w_down.shape[0]
    assert hidden % FFN_CHUNK == 0 and mem.shape[0] % (nchunks * F32_SUBLANES) == 0
    assert d % (nchunks * BF16_SUBLANES) == 0 and hidden % (nchunks * BF16_SUBLANES) == 0

    def proj_chunk(width):
        return pl.BlockSpec((nbatch, chunk, width), lambda s: (0, jnp.minimum(s, nchunks - 1), 0))

    def state(shape):
        return pl.BlockSpec(shape, lambda s: (0, 0, 0))

    def weight_rows(a, width=None):
        return pl.BlockSpec((a.shape[0] // nchunks, width or a.shape[1]),
                            lambda s: (jnp.minimum(s, nchunks - 1), 0))

    names = ["g_mix", "w_main", "w_gate_rows", "b_gate_rows", "g_head", "g_mem", "w_ck", "w_cv"]
    q, k, v, hm, c_st, n_st, m_st, w_gate_up_bf, w_down_bf, w_out_bf, memk, memv = pl.pallas_call(
        functools.partial(_mixer_prompt_kernel, nbatch=nbatch, chunk=chunk),
        grid=(nchunks + 1,),
        in_specs=[proj_chunk(d)] + _weight_specs(w, names) + [
            weight_rows(w_gate), weight_rows(w_up), weight_rows(w_down), weight_rows(w_out),
            weight_rows(mem),
        ],
        out_specs=[
            proj_chunk(ATT_WIDTH), proj_chunk(KV_WIDTH), proj_chunk(KV_WIDTH),
            pl.BlockSpec((nbatch, chunk, MLSTM_WIDTH), lambda s: (0, jnp.maximum(s - 1, 0), 0)),
            state((nstate, MLSTM_DV, LANES)), state((nstate, 1, LANES)), state((nstate, 1, LANES)),
            weight_rows(w_gate, 2 * hidden), weight_rows(w_down), weight_rows(w_out),
            weight_rows(mem, CROSS_WIDTH), weight_rows(mem, CROSS_WIDTH),
        ],
        out_shape=[
            jax.ShapeDtypeStruct((nbatch, seq, ATT_WIDTH), BF16),
            jax.ShapeDtypeStruct((nbatch, seq, KV_WIDTH), F32),
            jax.ShapeDtypeStruct((nbatch, seq, KV_WIDTH), F32),
            jax.ShapeDtypeStruct((nbatch, seq, MLSTM_WIDTH), BF16),
            jax.ShapeDtypeStruct((nstate, MLSTM_DV, LANES), F32),
            jax.ShapeDtypeStruct((nstate, 1, LANES), F32),
            jax.ShapeDtypeStruct((nstate, 1, LANES), F32),
            jax.ShapeDtypeStruct((d, 2 * hidden), BF16),
            jax.ShapeDtypeStruct((hidden, d), BF16),
            jax.ShapeDtypeStruct(w_out.shape, BF16),
            jax.ShapeDtypeStruct((mem.shape[0], CROSS_WIDTH), F32),
            jax.ShapeDtypeStruct((mem.shape[0], CROSS_WIDTH), F32),
        ],
        scratch_shapes=[
            pltpu.VMEM((2, rows, MQK_WIDTH), BF16), pltpu.VMEM((2, rows, MQK_WIDTH), BF16),
            pltpu.VMEM((2, rows, MLSTM_WIDTH), BF16), pltpu.VMEM((2, rows, MLSTM_WIDTH), BF16),
            pltpu.VMEM((2, GATE_ROWS, rows), F32),
        ],
        compiler_params=pltpu.CompilerParams(
            dimension_semantics=("arbitrary",), vmem_limit_bytes=VMEM_LIMIT,
            allow_input_fusion=[False] + [True] * len(names) + [False] * 5),
    )(x.reshape(nbatch, seq, d), *[w[n] for n in names], w_gate, w_up, w_down, w_out, mem)
    q = q.reshape(t, ATT_WIDTH)
    k = k.reshape(t, KV_WIDTH)
    v = v.reshape(t, KV_WIDTH)

    c_st = c_st.reshape(nbatch, MLSTM_HEADS, MLSTM_DV, 2, MLSTM_DK)
    n_st = n_st.reshape(nbatch, MLSTM_HEADS, 2, MLSTM_DK)
    c_fin = jnp.stack([c_st[:, h, :, h % 2, :] for h in range(MLSTM_HEADS)], axis=1)
    n_fin = jnp.stack([n_st[:, h, h % 2, :] for h in range(MLSTM_HEADS)], axis=1)
    m_fin = m_st[:, 0, 0].reshape(nbatch, MLSTM_HEADS)
    bf16_weights = {"w_gate_up": w_gate_up_bf, "w_down": w_down_bf, "w_out": w_out_bf}
    return q, k, v, hm.reshape(t, MLSTM_WIDTH), c_fin, n_fin, m_fin, bf16_weights, memk, memv


def _mlstm_sample_kernel(mq_ref, mk_ref, mkt_ref, mv_ref, og_ref, gc_ref, gr_ref, mrow_ref, c_ref,
                         n_ref, gh_ref, hm_ref, co_ref, no_ref, mt_ref, *, nb, tq):
    rows = nb * tq
    shift = tq.bit_length() - 1
    r = lax.broadcasted_iota(jnp.int32, (rows, rows), 0)
    s = lax.broadcasted_iota(jnp.int32, (rows, rows), 1)
    same = (r >> shift) == (s >> shift)
    valid = same & (s <= r)
    tril = jnp.where(valid, 1.0, 0.0).astype(BF16)
    triu = jnp.where(same & (r <= s), 1.0, 0.0).astype(BF16)
    row_batch = lax.broadcasted_iota(jnp.int32, (rows, 1), 0) >> shift
    lane = lax.broadcasted_iota(jnp.int32, (rows, LANES), 1)

    gc = gc_ref[...]
    gr = gr_ref[...]
    btc = _cumsum_cols(tril, gc)
    btr = _cumsum_rows(gr, triu)
    m_rows = mrow_ref[...]
    mt_all = jnp.zeros((rows, LANES), F32)

    def last_of_batch(col):
        parts = [jnp.broadcast_to(col[(b + 1) * tq - 1:(b + 1) * tq], (tq, 1)) for b in range(nb)]
        return jnp.concatenate(parts, axis=0)

    dk = MLSTM_DK
    lo = _low_half((rows, LANES))
    own_q = row_batch == (lax.broadcasted_iota(jnp.int32, (rows, nb * dk), 1) >> _log2(dk))
    lane_batch = lax.broadcasted_iota(jnp.int32, (dk, rows), 1) >> shift

    def state_rows(b, h):
        return slice((b * MLSTM_HEADS + h) * dk, (b * MLSTM_HEADS + h + 1) * dk)

    for h in range(MLSTM_HEADS):
        slab = slice((h // 2) * LANES, (h // 2 + 1) * LANES)
        qs, ks = mq_ref[:, slab], mk_ref[:, slab]
        qsw, ksw = pltpu.roll(qs, dk, axis=1), pltpu.roll(ks, dk, axis=1)
        q2 = jnp.where(lo, qs, qsw) if h % 2 == 0 else jnp.where(lo, qsw, qs)
        k2 = jnp.where(lo, ks, ksw) if h % 2 == 0 else jnp.where(lo, ksw, ks)
        qb, kb = q2[:, :dk].astype(BF16), k2[:, :dk].astype(BF16)
        vb = mv_ref[:, h * MLSTM_DV:(h + 1) * MLSTM_DV].astype(BF16)
        it_c = gc[:, h:h + 1]
        bt_c = btc[:, MLSTM_HEADS + h:MLSTM_HEADS + h + 1]
        it_r = gr[h:h + 1, :]
        bt_r = btr[MLSTM_HEADS + h:MLSTM_HEADS + h + 1, :]
        m_c = m_rows[:, h:h + 1]

        num, ssum, mt, wi = _mlstm_intra(qb, kb, vb, it_c, bt_c, it_r, bt_r, m_c, valid)
        qx = jnp.where(own_q, jnp.concatenate([q2] * (nb // 2), axis=1), 0.0).astype(BF16)
        c_stack = jnp.concatenate([c_ref[state_rows(b, h), :] for b in range(nb)], axis=0)
        qc = _dot(qx, c_stack.astype(BF16))
        n_rows = jnp.concatenate(
            [jnp.broadcast_to(n_ref[b * MLSTM_HEADS + h], (tq, dk)) for b in range(nb)], axis=0)
        num = num + wi * qc
        den = ssum + wi * jnp.sum(qb.astype(F32) * n_rows, axis=-1, keepdims=True)
        og = og_ref[:, h * MLSTM_DV:(h + 1) * MLSTM_DV]
        g_head = gh_ref[:, h * MLSTM_DV:(h + 1) * MLSTM_DV]
        hm_ref[:, h * MLSTM_DV:(h + 1) * MLSTM_DV] = _mlstm_head_out(num, den, mt, og, g_head)

        b_last = last_of_batch(bt_c)
        m_new = last_of_batch(mt)
        wk = jnp.exp(b_last - bt_c + it_c - m_new)
        wc = jnp.exp(b_last + m_c - m_new)
        wkv = (wk * vb.astype(F32)).astype(BF16)
        wkk = wk * kb.astype(F32)
        kt = mkt_ref[h * dk:(h + 1) * dk, :]
        kx = jnp.concatenate([jnp.where(lane_batch == b, kt, 0.0) for b in range(nb)], axis=0)
        upd = _dot(kx.astype(BF16), wkv)
        for b in range(nb):
            idx = b * MLSTM_HEADS + h
            wc_b = wc[b * tq:b * tq + 1]
            co_ref[state_rows(b, h), :] = wc_b * c_ref[state_rows(b, h), :] + upd[b * dk:(b + 1) * dk]
            no_ref[idx] = wc_b * n_ref[idx] + jnp.sum(wkk[b * tq:(b + 1) * tq], axis=0, keepdims=True)
        mt_all = jnp.where(lane == h, mt, mt_all)
    mt_ref[...] = mt_all


def _mlstm_sample(mq, mk, mkt, mv, og, gcol, grow, m_rows, ct_in, n_in, g_head, nbatch, nb):
    t = mq.shape[0]
    tq = t // nbatch
    rows = nb * tq
    assert tq & (tq - 1) == 0 and rows == LANES and nb % 2 == 0

    def tok(width):
        return pl.BlockSpec((rows, width), lambda i: (i, 0))

    ct_spec = pl.BlockSpec((nb * MLSTM_HEADS * MLSTM_DK, MLSTM_DV), lambda i: (i, 0))
    n_spec = pl.BlockSpec((nb * MLSTM_HEADS, 1, MLSTM_DK), lambda i: (i, 0, 0))
    return pl.pallas_call(
        functools.partial(_mlstm_sample_kernel, nb=nb, tq=tq),
        grid=(nbatch // nb,),
        in_specs=[
            tok(MQK_WIDTH), tok(MQK_WIDTH),
            pl.BlockSpec((MQK_WIDTH, rows), lambda i: (0, i)),
            tok(MLSTM_WIDTH), tok(MLSTM_WIDTH), tok(LANES),
            pl.BlockSpec((GATE_ROWS, rows), lambda i: (0, i)),
            tok(LANES),
            ct_spec, n_spec,
            _const_spec((1, MLSTM_WIDTH)),
        ],
        out_specs=[tok(MLSTM_WIDTH), ct_spec, n_spec, tok(LANES)],
        out_shape=[
            jax.ShapeDtypeStruct((t, MLSTM_WIDTH), F32),
            jax.ShapeDtypeStruct(ct_in.shape, F32),
            jax.ShapeDtypeStruct(n_in.shape, F32),
            jax.ShapeDtypeStruct((t, LANES), F32),
        ],
        compiler_params=pltpu.CompilerParams(
            dimension_semantics=("arbitrary",), vmem_limit_bytes=VMEM_LIMIT,
            allow_input_fusion=[False] * 7 + [True, False, True, False]),
    )(mq, mk, mkt, mv, og, gcol, grow, m_rows, ct_in, n_in, g_head)


def _mix_stage(x, att, hm, wo_ref, gcr_ref, wcq_ref):
    cat = jnp.concatenate([att.astype(BF16), hm.astype(BF16)], axis=1)
    y1 = x + _dot(cat, wo_ref[...])
    qc = _dot(_rmsnorm(y1, gcr_ref[...]).astype(BF16), wcq_ref[...])
    return y1, qc


def _cross_attention(qc, mk, mv):
    slabs = []
    for j in range(CROSS_WIDTH // LANES):
        qs = qc[:, j * LANES:(j + 1) * LANES]
        ks = mk[:, j * LANES:(j + 1) * LANES]
        vs = mv[:, j * LANES:(j + 1) * LANES]
        n = qs.shape[0]
        lo = _low_half(qs.shape)
        zero = jnp.zeros_like(qs)
        q2 = jnp.concatenate([jnp.where(lo, qs, zero), jnp.where(lo, zero, qs)], axis=0)
        s = _dot_nt(q2 * (HEAD_DIM ** -0.5), ks)
        p = jnp.exp(s - jnp.max(s, axis=-1, keepdims=True))
        o2 = _dot(p.astype(BF16), vs) / jnp.sum(p, axis=-1, keepdims=True)
        slabs.append(jnp.where(lo, o2[:n], o2[n:]))
    return jnp.concatenate(slabs, axis=1)


FFN_CHUNK = 256


def _finish(y3, gfin_ref, final):
    return _rmsnorm(y3, gfin_ref[...]) if final else y3


def _ffn_stage(y1, o, wco_ref, gf_ref, wgu_ref, wd_ref, act_ref, between=None):
    y2 = y1 + _dot(o.astype(BF16), wco_ref[...])
    xn = _rmsnorm(y2, gf_ref[...]).astype(BF16)
    for c in range(wd_ref.shape[0] // FFN_CHUNK):
        gu = _dot(xn, wgu_ref[:, 2 * c * FFN_CHUNK:2 * (c + 1) * FFN_CHUNK])
        g, u = gu[:, :FFN_CHUNK], gu[:, FFN_CHUNK:]
        act_ref[:, c * FFN_CHUNK:(c + 1) * FFN_CHUNK] = (g * jax.nn.sigmoid(g) * u).astype(BF16)
        if between is not None:
            between()
    return y2 + _dot(act_ref[...], wd_ref[...])


def _post_kernel(sink_ref, q_ref, kc_ref, kp_ref, vc_ref, vp_ref, x_ref, hm_ref, mk_ref, mv_ref,
                 xd_ref, attd_ref, hmd_ref, mkd_ref, mvd_ref,
                 wo_ref, gcr_ref, wcq_ref, wco_ref, gf_ref, wgu_ref, wd_ref, gfin_ref,
                 o_ref, od_ref, act_ref, att_ref, y3_ref, *, final, ntiles, tiles_per_seq, tq):
    tm = x_ref.shape[0]
    s = pl.program_id(0)
    tile = jnp.minimum(s, ntiles - 1)
    first_block = (tile % tiles_per_seq) == 0
    slot = s % 2

    def swa_stages():
        def store(j, value):
            att_ref[slot, j * WINDOW:(j + 1) * WINDOW, :] = value.astype(att_ref.dtype)
        return _swa_prompt_stages(sink_ref, q_ref, kc_ref, kp_ref, vc_ref, vp_ref, first_block, store)

    def finish():
        done = _finish(y3_ref[...], gfin_ref, final)
        o_ref[...] = done[:tm]
        od_ref[...] = done[tm:]

    @pl.when(s == 0)
    def _():
        y3_ref[...] = jnp.zeros_like(y3_ref)
        for _ in swa_stages():
            pass

    @pl.when(jnp.logical_and(s > 0, s <= ntiles))
    def _():
        swa = swa_stages()

        def advance(n=SWA_STAGES_PER_FFN_CHUNK):
            for _ in range(n):
                next(swa, None)

        finish()
        x = jnp.concatenate([x_ref[...], xd_ref[...]], axis=0)
        att = jnp.concatenate([att_ref[1 - slot], attd_ref[...].astype(BF16)], axis=0)
        hm = jnp.concatenate([hm_ref[...], hmd_ref[...].astype(BF16)], axis=0)
        y1, qc = _mix_stage(x, att, hm, wo_ref, gcr_ref, wcq_ref)
        advance()
        o = jnp.concatenate([
            _cross_attention(qc[:tm].astype(BF16), mk_ref[...].astype(BF16), mv_ref[...].astype(BF16)),
            _cross_attention_decode(qc[tm:], mkd_ref, mvd_ref, tq)], axis=0)
        advance()
        y3_ref[...] = _ffn_stage(y1, o, wco_ref, gf_ref, wgu_ref, wd_ref, act_ref, between=advance)
        for _ in swa:
            pass

    @pl.when(s > ntiles)
    def _():
        finish()


_MIX_WEIGHTS = ("w_out", "g_cross", "w_cq")
_FFN_WEIGHTS = ("w_co", "g_ffn", "w_gate_up", "w_down", "g_final")


def _post(x, q, k, v, hm, memk, memv, xd, attd, hmd, memkd_t, memvd_t, sinks, w, nbatch, nseq, final, tm):
    t, d = x.shape
    ntiles = t // tm
    tpb = ntiles // nbatch
    wpt = tm // WINDOW
    m = memk.shape[0] // nbatch
    hidden = w["w_down"].shape[0]
    td = xd.shape[0]
    tq = td // nseq
    slab = td // ntiles
    assert td % ntiles == 0 and slab % tq == 0 and slab % BF16_SUBLANES == 0
    mem_d = memkd_t.shape[1]

    def attn_tile(s):
        return jnp.minimum(s, ntiles - 1)

    def post_tile(s):
        return jnp.clip(s - 1, 0, ntiles - 1)

    def done_tile(s):
        return jnp.maximum(s - 2, 0)

    def attn_rows(width):
        return pl.BlockSpec((tm, width), lambda s: (attn_tile(s), 0))

    def prev_window(width):
        return pl.BlockSpec((WINDOW, width), lambda s: (jnp.maximum(attn_tile(s) * wpt - 1, 0), 0))

    def post_rows(width):
        return pl.BlockSpec((tm, width), lambda s: (post_tile(s), 0))

    def post_slab(width):
        return pl.BlockSpec((slab, width), lambda s: (post_tile(s), 0))

    mem_spec = pl.BlockSpec((m, CROSS_WIDTH), lambda s: (post_tile(s) // tpb, 0))
    memd_spec = pl.BlockSpec((slab // tq * CROSS_WIDTH, mem_d), lambda s: (post_tile(s), 0))
    names = _MIX_WEIGHTS + _FFN_WEIGHTS
    return pl.pallas_call(
        functools.partial(_post_kernel, final=final, ntiles=ntiles, tiles_per_seq=tpb, tq=tq),
        grid=(ntiles + 2,),
        in_specs=[
            pl.BlockSpec(memory_space=pltpu.SMEM),
            attn_rows(ATT_WIDTH), attn_rows(KV_WIDTH), prev_window(KV_WIDTH),
            attn_rows(KV_WIDTH), prev_window(KV_WIDTH),
            post_rows(d), post_rows(MLSTM_WIDTH), mem_spec, mem_spec,
            post_slab(d), post_slab(ATT_WIDTH), post_slab(MLSTM_WIDTH), memd_spec, memd_spec,
        ] + _weight_specs(w, names),
        out_specs=[pl.BlockSpec((tm, d), lambda s: (done_tile(s), 0)),
                   pl.BlockSpec((slab, d), lambda s: (done_tile(s), 0))],
        out_shape=[jax.ShapeDtypeStruct((t, d), F32), jax.ShapeDtypeStruct((td, d), F32)],
        scratch_shapes=[pltpu.VMEM((tm + slab, hidden), BF16), pltpu.VMEM((2, tm, ATT_WIDTH), BF16),
                        pltpu.VMEM((tm + slab, d), F32)],
        compiler_params=pltpu.CompilerParams(
            dimension_semantics=("arbitrary",), vmem_limit_bytes=VMEM_LIMIT),
    )(sinks, q, k, k, v, v, x, hm, memk, memv, xd, attd, hmd, memkd_t, memvd_t,
      *[w[n] for n in names])


def _cross_attention_decode(qc, mk_ref, mv_ref, tq):
    n = CROSS_HEADS * tq
    row_head = lax.broadcasted_iota(jnp.int32, (n, CROSS_WIDTH), 0) >> _log2(tq)
    lane_head = lax.broadcasted_iota(jnp.int32, (n, CROSS_WIDTH), 1) >> _log2(HEAD_DIM)
    own = row_head == lane_head
    outs = []
    for b in range(qc.shape[0] // tq):
        q = qc[b * tq:(b + 1) * tq, :]
        qx = jnp.where(own, jnp.concatenate([q] * CROSS_HEADS, axis=0), 0.0).astype(BF16)
        mk = mk_ref[b * CROSS_WIDTH:(b + 1) * CROSS_WIDTH, :].astype(BF16)
        mv = mv_ref[b * CROSS_WIDTH:(b + 1) * CROSS_WIDTH, :].astype(BF16)
        s = _dot(qx, mk) * (HEAD_DIM ** -0.5)
        p = jnp.exp(s - jnp.max(s, axis=-1, keepdims=True))
        pv = _dot_nt(p.astype(BF16), mv) / jnp.sum(p, axis=-1, keepdims=True)
        pv = jnp.where(own, pv, 0.0)
        out = pv[0:tq]
        for h in range(1, CROSS_HEADS):
            out = out + pv[h * tq:(h + 1) * tq]
        outs.append(out)
    return jnp.concatenate(outs, axis=0)


def _layer_weights(l, w_in, b_igate, b_fgate, g_mlstm_head, g_mix, g_cross, g_mem, w_cq, w_ck, w_cv,
                   w_co, g_ffn, g_final):
    gate_w = w_in[l][:, MAIN_WIDTH:]
    gate_b = jnp.concatenate([b_igate[l], b_fgate[l]]).astype(F32)
    ngate = gate_b.shape[0]
    row = lambda a: a.astype(F32).reshape(1, -1)
    w_t = w_in[l].T
    k0 = ATT_WIDTH
    mk0 = ATT_WIDTH + 2 * KV_WIDTH + MQK_WIDTH
    return {
        "w_kvk_t": jnp.concatenate([w_t[k0:k0 + 2 * KV_WIDTH], w_t[mk0:mk0 + MQK_WIDTH]]).astype(BF16),
        "w_main": w_t[:MAIN_WIDTH].astype(BF16),
        "w_main_decode": jnp.concatenate([w_t[:k0], w_t[k0 + 2 * KV_WIDTH:MAIN_WIDTH]]).astype(BF16),
        "w_gate_cols": jnp.pad(gate_w, ((0, 0), (0, LANES - ngate))).astype(BF16),
        "w_gate_rows": jnp.pad(gate_w.T, ((0, GATE_ROWS - ngate), (0, 0))).astype(BF16),
        "b_gate_cols": jnp.pad(gate_b, (0, LANES - ngate)).reshape(1, LANES),
        "b_gate_rows": jnp.pad(gate_b, (0, GATE_ROWS - ngate)).reshape(GATE_ROWS, 1),
        "g_mix": row(g_mix[l]), "g_cross": row(g_cross[l]), "g_mem": row(g_mem[l]),
        "g_ffn": row(g_ffn[l]), "g_final": row(g_final), "g_head": row(g_mlstm_head[l]),
        "w_cq": w_cq[l].astype(BF16), "w_ck": w_ck[l].astype(BF16),
        "w_cv": w_cv[l].astype(BF16), "w_co": w_co[l].astype(BF16),
    }


TOKEN_TILE = 512
MLSTM_STAGES_PER_PROJ_STAGE = 15
SWA_STAGES_PER_FFN_CHUNK = 3
MLSTM_CHUNK_ROWS = 256
SWA_SAMPLE_BATCH = 16
MLSTM_SAMPLE_BATCH = 16


def kernel(x_prompt, x_sample, mem_prompt, cache_swa_k, cache_swa_v, state_mlstm_C, state_mlstm_n,
           state_mlstm_m, cache_mem_k, cache_mem_v, w_in, b_igate, b_fgate, attn_sinks,
           g_mlstm_head, w_out, g_mix, g_cross, g_mem, w_cq, w_ck, w_cv, w_co, g_ffn, w_gate,
           w_up, w_down, g_final):
    depth = w_in.shape[0]
    bp, sp, d = x_prompt.shape
    bs, ss, _ = x_sample.shape
    mem_tokens = mem_prompt.shape[1]
    past = cache_swa_k.shape[2]
    yp = x_prompt.reshape(bp * sp, d)
    ys = x_sample.reshape(bs * ss, d)
    mem = mem_prompt.reshape(bp * mem_tokens, d)
    outs = [[] for _ in range(12)]

    for l in range(depth):
        final = l == depth - 1
        w = _layer_weights(l, w_in, b_igate, b_fgate, g_mlstm_head, g_mix, g_cross, g_mem, w_cq,
                           w_ck, w_cv, w_co, g_ffn, g_final)
        sinks = attn_sinks[l].astype(F32)

        q, k, v, hm, c_p, n_p, m_p, cast, memk, memv = _mixer_prompt(
            yp, mem, w, w_gate[l].astype(F32), w_up[l].astype(F32), w_down[l].astype(F32),
            w_out[l].astype(F32), bp, MLSTM_CHUNK_ROWS)
        w.update(cast)
        win_shape = (bp, WINDOW, ATT_KV_HEADS, HEAD_DIM)
        outs[0].append(k.reshape(bp, sp, KV_WIDTH)[:, sp - WINDOW:].reshape(win_shape))
        outs[1].append(v.reshape(bp, sp, KV_WIDTH)[:, sp - WINDOW:].reshape(win_shape))
        outs[2].append(c_p)
        outs[3].append(n_p)
        outs[4].append(m_p)
        mem_shape = (bp, mem_tokens, CROSS_HEADS, HEAD_DIM)
        outs[5].append(memk.reshape(mem_shape))
        outs[6].append(memv.reshape(mem_shape))

        def tokens_last(a):
            return jnp.transpose(a.astype(F32), (0, 2, 3, 1)).reshape(-1, a.shape[1])

        qd, mq, mk, mv, og, grow, gcol, knt, vnt, mkt = _decode_proj(ys, w, TOKEN_TILE)
        att_d, kbuf_t, vbuf_t = _swa_sample(
            qd, knt, vnt, tokens_last(cache_swa_k[l]), tokens_last(cache_swa_v[l]), sinks, bs,
            SWA_SAMPLE_BATCH)
        m_rows = jnp.pad(jnp.repeat(state_mlstm_m[l].astype(F32), ss, axis=0),
                         ((0, 0), (0, LANES - MLSTM_HEADS)))
        ct_in = jnp.swapaxes(state_mlstm_C[l].astype(F32), 2, 3).reshape(-1, MLSTM_DV)
        hm_d, ct_s, n_s, mt = _mlstm_sample(
            mq, mk, mkt, mv, og, gcol, grow, m_rows, ct_in,
            state_mlstm_n[l].astype(F32).reshape(bs * MLSTM_HEADS, 1, MLSTM_DK),
            w["g_head"], bs, MLSTM_SAMPLE_BATCH)

        yp, ys = _post(yp, q, k, v, hm, memk, memv, ys, att_d, hm_d, tokens_last(cache_mem_k[l]),
                       tokens_last(cache_mem_v[l]), sinks, w, bp, bs, final, TOKEN_TILE)
        buf_t_shape = (bs, ATT_KV_HEADS, HEAD_DIM, past)
        outs[7].append(jnp.transpose(kbuf_t.reshape(buf_t_shape), (0, 3, 1, 2)))
        outs[8].append(jnp.transpose(vbuf_t.reshape(buf_t_shape), (0, 3, 1, 2)))
        outs[9].append(jnp.swapaxes(ct_s.reshape(bs, MLSTM_HEADS, MLSTM_DK, MLSTM_DV), 2, 3))
        outs[10].append(n_s.reshape(bs, MLSTM_HEADS, MLSTM_DK))
        outs[11].append(mt.reshape(bs, ss, LANES)[:, ss - 1, :MLSTM_HEADS])

    return (yp.reshape(bp, sp, d), ys.reshape(bs, ss, d)) + tuple(jnp.stack(o) for o in outs)
```

```python
import functools

import jax
import jax.numpy as jnp
from jax import lax
from jax.experimental import pallas as pl
from jax.experimental.pallas import tpu as pltpu

F32 = jnp.float32
BF16 = jnp.bfloat16

HEAD_DIM = 64
ATT_GROUP = 4
ATT_KV_HEADS = 2
ATT_WIDTH = 512
KV_WIDTH = ATT_KV_HEADS * HEAD_DIM
WINDOW = 128
MLSTM_HEADS = 4
MLSTM_DV = 128
MLSTM_DK = 64
MQK_WIDTH = MLSTM_HEADS * MLSTM_DK
MLSTM_WIDTH = MLSTM_HEADS * MLSTM_DV
MAIN_WIDTH = ATT_WIDTH + 2 * KV_WIDTH + 2 * MQK_WIDTH + 2 * MLSTM_WIDTH
CROSS_HEADS = 4
CROSS_WIDTH = CROSS_HEADS * HEAD_DIM
EPS = 1e-6
NEG_INF = float("-inf")
assert HEAD_DIM == 4 ** 3

LANES = 128
F32_SUBLANES = 8
BF16_SUBLANES = 16
GATE_ROWS = BF16_SUBLANES
VMEM_LIMIT = 56 * 1024 * 1024

NT = (((1,), (1,)), ((), ()))


def _dot(a, b):
    return jnp.dot(a, b, preferred_element_type=F32)


def _dot_nt(a, b):
    return lax.dot_general(a, b, NT, preferred_element_type=F32)


def _rmsnorm(x, g):
    return x * lax.rsqrt(jnp.mean(x * x, axis=-1, keepdims=True) + EPS) * g


def _log_sigmoid(x):
    return jnp.minimum(x, 0.0) - jnp.log1p(jnp.exp(-jnp.abs(x)))


def _split3(x):
    hi = x.astype(BF16)
    r1 = x - hi.astype(F32)
    mid = r1.astype(BF16)
    lo = (r1 - mid.astype(F32)).astype(BF16)
    return hi, mid, lo


def _cumsum_cols(tri, x):
    hi, mid, lo = _split3(x)
    return _dot(tri, hi) + _dot(tri, mid) + _dot(tri, lo)


def _cumsum_rows(x, tri):
    hi, mid, lo = _split3(x)
    return _dot(hi, tri) + _dot(mid, tri) + _dot(lo, tri)


def _log2(n):
    assert n > 0 and n & (n - 1) == 0, n
    return n.bit_length() - 1


def _low_half(shape):
    return lax.broadcasted_iota(jnp.int32, shape, 1) < HEAD_DIM


def _decode_proj_kernel(x_ref, g_ref, w_ref, wgr_ref, br_ref, wgc_ref, bc_ref, wt_ref,
                        q_ref, mq_ref, mk_ref, mv_ref, og_ref, gr_ref, gc_ref, kt_ref, vt_ref, mkt_ref):
    xn = _rmsnorm(x_ref[...], g_ref[...]).astype(BF16)
    zt = _dot_nt(wt_ref[...], xn)
    kt_ref[...] = zt[:KV_WIDTH]
    vt_ref[...] = zt[KV_WIDTH:2 * KV_WIDTH]
    mkt_ref[...] = zt[2 * KV_WIDTH:] * (MLSTM_DK ** -0.5)

    def mm(lo, hi):
        return _dot_nt(xn, w_ref[lo:hi, :])

    o = 0
    q_ref[...] = mm(o, o + ATT_WIDTH)
    o += ATT_WIDTH + 2 * KV_WIDTH
    mqk = mm(o, o + 2 * MQK_WIDTH)
    mq_ref[...] = mqk[:, :MQK_WIDTH]
    mk_ref[...] = mqk[:, MQK_WIDTH:] * (MLSTM_DK ** -0.5)
    o += 2 * MQK_WIDTH
    mv_ref[...] = mm(o, o + MLSTM_WIDTH)
    o += MLSTM_WIDTH
    og_ref[...] = mm(o, o + MLSTM_WIDTH)

    zc = _dot(xn, wgc_ref[...]) + bc_ref[...]
    lane = lax.broadcasted_iota(jnp.int32, zc.shape, 1)
    gc_ref[...] = jnp.where(lane < MLSTM_HEADS, zc, _log_sigmoid(zc))
    zr = _dot_nt(wgr_ref[...], xn) + br_ref[...]
    row = lax.broadcasted_iota(jnp.int32, zr.shape, 0)
    gr_ref[...] = jnp.where(row < MLSTM_HEADS, zr, _log_sigmoid(zr))


def _const_spec(shape):
    nd = len(shape)
    return pl.BlockSpec(shape, lambda *_: (0,) * nd, pipeline_mode=pl.Buffered(1))


def _weight_specs(w, names):
    return [_const_spec(w[n].shape) for n in names]


def _decode_proj(x, w, tm):
    t, d = x.shape

    def rows(width):
        return pl.BlockSpec((tm, width), lambda i: (i, 0))

    def lanes(height):
        return pl.BlockSpec((height, tm), lambda i: (0, i))

    outs = [(rows, ATT_WIDTH), (rows, MQK_WIDTH), (rows, MQK_WIDTH), (rows, MLSTM_WIDTH),
            (rows, MLSTM_WIDTH), (lanes, GATE_ROWS), (rows, LANES), (lanes, KV_WIDTH),
            (lanes, KV_WIDTH), (lanes, MQK_WIDTH)]
    names = ["g_mix", "w_main", "w_gate_rows", "b_gate_rows", "w_gate_cols", "b_gate_cols", "w_kvk_t"]
    return pl.pallas_call(
        _decode_proj_kernel,
        grid=(t // tm,),
        in_specs=[rows(d)] + _weight_specs(w, names),
        out_specs=[kind(size) for kind, size in outs],
        out_shape=[jax.ShapeDtypeStruct((t, size) if kind is rows else (size, t), F32)
                   for kind, size in outs],
        compiler_params=pltpu.CompilerParams(
            dimension_semantics=("arbitrary",), vmem_limit_bytes=VMEM_LIMIT,
            allow_input_fusion=[False] + [True] * len(names)),
    )(x, *[w[n] for n in names])


def _dup_heads(x):
    swapped = pltpu.roll(x, HEAD_DIM, axis=1)
    lo = _low_half(x.shape)
    return (jnp.where(lo, x, swapped).astype(BF16), jnp.where(lo, swapped, x).astype(BF16))


def _stack_query_heads(q, h):
    parts = []
    for g in range(ATT_GROUP):
        hq = h * ATT_GROUP + g
        slab = q[:, (hq // 2) * LANES:(hq // 2 + 1) * LANES]
        lo = _low_half(slab.shape)
        keep = lo if hq % 2 == 0 else jnp.logical_not(lo)
        parts.append(jnp.where(keep, slab, jnp.zeros_like(slab)))
    return jnp.concatenate(parts, axis=0).astype(BF16)


def _sink_column(sink_ref, h, rows_per_head):
    n = ATT_GROUP * rows_per_head
    grp = lax.broadcasted_iota(jnp.int32, (n, 1), 0) >> _log2(rows_per_head)
    col = jnp.full((n, 1), sink_ref[h * ATT_GROUP], F32)
    for g in range(1, ATT_GROUP):
        col = jnp.where(grp == g, sink_ref[h * ATT_GROUP + g], col)
    return col


def _swa_prompt_stages(sink_ref, q_ref, kc_ref, kp_ref, vc_ref, vp_ref, first_block, store):
    qb = q_ref.shape[0]
    k_all = jnp.concatenate([kp_ref[...], kc_ref[...]], axis=0)
    v_all = jnp.concatenate([vp_ref[...], vc_ref[...]], axis=0)
    kd = _dup_heads(k_all)
    v_t = v_all.T.astype(BF16)

    n = ATT_GROUP * WINDOW
    key = lax.broadcasted_iota(jnp.int32, (2 * WINDOW, n), 0)
    qry = lax.broadcasted_iota(jnp.int32, (2 * WINDOW, n), 1) & (WINDOW - 1)
    delta = key - qry
    in_band = (delta >= 0) & (delta <= WINDOW)
    grp = lax.broadcasted_iota(jnp.int32, (1, n), 1) >> _log2(WINDOW)
    sinks = []
    for h in range(ATT_KV_HEADS):
        row = jnp.full((1, n), sink_ref[h * ATT_GROUP], F32)
        for g in range(1, ATT_GROUP):
            row = jnp.where(grp == g, sink_ref[h * ATT_GROUP + g], row)
        sinks.append(row)
    yield

    probs = []
    for j in range(qb // WINDOW):
        valid = in_band & (key >= jnp.where(first_block, WINDOW, 0)) if j == 0 else in_band
        q = q_ref[j * WINDOW:(j + 1) * WINDOW, :] * (HEAD_DIM ** -0.5)
        band = slice(j * WINDOW, (j + 2) * WINDOW)
        for h in range(ATT_KV_HEADS):
            qh = _stack_query_heads(q, h)
            s = _dot_nt(kd[h][band], qh)
            probs.append(dict(h=h, band=band, s=jnp.where(valid, s, NEG_INF)))
            yield
    for p in probs:
        p["m"] = jnp.maximum(jnp.max(p["s"], axis=0, keepdims=True), sinks[p["h"]])
        yield
    for p in probs:
        p["p"] = jnp.exp(p["s"] - p["m"])
        yield
    for p in probs:
        h = p["h"]
        denom = jnp.sum(p["p"], axis=0, keepdims=True) + jnp.exp(sinks[h] - p["m"])
        p["o_t"] = _dot(v_t[h * HEAD_DIM:(h + 1) * HEAD_DIM, p["band"]], p["p"].astype(BF16)) / denom
        yield
    for j in range(qb // WINDOW):
        pieces = [p["o_t"][:, g * WINDOW:(g + 1) * WINDOW]
                  for p in probs[j * ATT_KV_HEADS:(j + 1) * ATT_KV_HEADS] for g in range(ATT_GROUP)]
        store(j, jnp.concatenate(pieces, axis=0).T)
        yield


def _swa_sample_kernel(sink_ref, q_ref, knt_ref, vnt_ref, ck_ref, cv_ref,
                       o_ref, kbuf_ref, vbuf_ref, *, nb, tq):
    past = ck_ref.shape[1]
    hw = ATT_KV_HEADS * HEAD_DIM
    heads = ATT_KV_HEADS * ATT_GROUP
    n = heads * tq
    fresh = past - tq
    t = lax.broadcasted_iota(jnp.int32, (n, 2 * past), 0) & (tq - 1)
    col = lax.broadcasted_iota(jnp.int32, (n, 2 * past), 1)
    valid = ((col < past) & (col >= t)) | ((col >= past + fresh) & (col - (past + fresh) <= t))
    sink = jnp.concatenate([_sink_column(sink_ref, h, tq) for h in range(ATT_KV_HEADS)], axis=0)
    lane = lax.broadcasted_iota(jnp.int32, (hw, past), 1)
    knt = knt_ref[...]
    vnt = vnt_ref[...]
    zero_slab = jnp.zeros((tq, LANES), F32)
    lo = _low_half((tq, LANES))

    def shifted(old, new_t, b):
        return jnp.where(lane >= fresh, pltpu.roll(new_t, (fresh - b * tq) % past, axis=1),
                         pltpu.roll(old, fresh, axis=1))

    def dup_rows(x):
        h0, h1 = x[:HEAD_DIM], x[HEAD_DIM:]
        return jnp.concatenate([h0, h0, h1, h1], axis=0)

    for b in range(nb):
        k_old = ck_ref[b * hw:(b + 1) * hw, :]
        v_old = cv_ref[b * hw:(b + 1) * hw, :]
        k_new = shifted(k_old, knt, b)
        v_new = shifted(v_old, vnt, b)
        kbuf_ref[b * hw:(b + 1) * hw, :] = k_new
        vbuf_ref[b * hw:(b + 1) * hw, :] = v_new
        kop = jnp.concatenate([dup_rows(k_old), dup_rows(k_new)], axis=1).astype(BF16)
        vop = jnp.concatenate([dup_rows(v_old), dup_rows(v_new)], axis=1).astype(BF16)

        q = q_ref[b * tq:(b + 1) * tq, :]
        blocks = []
        for hq in range(heads):
            slab = q[:, (hq // 2) * LANES:(hq // 2 + 1) * LANES]
            slab = jnp.where(lo if hq % 2 == 0 else jnp.logical_not(lo), slab, 0.0)
            pair = [slab, zero_slab] if hq // ATT_GROUP == 0 else [zero_slab, slab]
            blocks.append(jnp.concatenate(pair, axis=1))
        qx = jnp.concatenate(blocks, axis=0).astype(BF16)

        s = _dot(qx, kop) * (HEAD_DIM ** -0.5)
        s = jnp.where(valid, s, NEG_INF)
        m = jnp.maximum(jnp.max(s, axis=-1, keepdims=True), sink)
        p = jnp.exp(s - m)
        denom = jnp.sum(p, axis=-1, keepdims=True) + jnp.exp(sink - m)
        o = _dot_nt(p.astype(BF16), vop) / denom

        slabs = []
        for j in range(heads // 2):
            h = (2 * j) // ATT_GROUP
            even = o[(2 * j) * tq:(2 * j + 1) * tq, h * LANES:(h + 1) * LANES]
            odd = o[(2 * j + 1) * tq:(2 * j + 2) * tq, h * LANES:(h + 1) * LANES]
            slabs.append(jnp.where(lo, even, odd))
        o_ref[b * tq:(b + 1) * tq, :] = jnp.concatenate(slabs, axis=1)


def _swa_sample(q, knt, vnt, cache_kt, cache_vt, sinks, nbatch, nb):
    t = q.shape[0]
    tq = t // nbatch
    past = cache_kt.shape[1]
    hw = ATT_KV_HEADS * HEAD_DIM
    assert past == WINDOW and tq & (tq - 1) == 0 and nb * tq == past

    def rows(r, width):
        return pl.BlockSpec((r, width), lambda i: (i, 0))

    new_t = pl.BlockSpec((hw, nb * tq), lambda i: (0, i))
    return pl.pallas_call(
        functools.partial(_swa_sample_kernel, nb=nb, tq=tq),
        grid=(nbatch // nb,),
        in_specs=[
            pl.BlockSpec(memory_space=pltpu.SMEM),
            rows(nb * tq, ATT_WIDTH), new_t, new_t, rows(nb * hw, past), rows(nb * hw, past),
        ],
        out_specs=[rows(nb * tq, ATT_WIDTH), rows(nb * hw, past), rows(nb * hw, past)],
        out_shape=[
            jax.ShapeDtypeStruct((t, ATT_WIDTH), F32),
            jax.ShapeDtypeStruct(cache_kt.shape, F32),
            jax.ShapeDtypeStruct(cache_vt.shape, F32),
        ],
        compiler_params=pltpu.CompilerParams(
            dimension_semantics=("arbitrary",), vmem_limit_bytes=VMEM_LIMIT),
    )(sinks, q, knt, vnt, cache_kt, cache_vt)


def _mlstm_intra(q, k, v, it_c, bt_c, it_r, bt_r, m_c, valid):
    d = jnp.where(valid, bt_c - bt_r + it_r, NEG_INF)
    inter = bt_c + m_c
    mt = jnp.maximum(jnp.max(d, axis=-1, keepdims=True), inter)
    sm = _dot_nt(q, k) * jnp.exp(d - mt)
    wi = jnp.exp(inter - mt)
    return _dot(sm.astype(BF16), v), jnp.sum(sm, axis=-1, keepdims=True), mt, wi


def _mlstm_head_out(num, den, mt, og, g_head):
    lower = jnp.maximum(jnp.abs(den), jnp.exp(-mt))
    h = num / lower
    hn = h * lax.rsqrt(jnp.mean(h * h, axis=-1, keepdims=True) + EPS)
    return hn * g_head * jax.nn.sigmoid(og)


def _mlstm_prompt_stages(mq, mk, mv, og, gr, gh_ref, hm_ref, c_ref, n_ref, m_ref, nbatch, chunk):
    s_idx = lax.broadcasted_iota(jnp.int32, (chunk, chunk), 0)
    l_idx = lax.broadcasted_iota(jnp.int32, (chunk, chunk), 1)
    causal = s_idx <= l_idx
    triu = jnp.where(causal, 1.0, 0.0).astype(BF16)
    lo = _low_half((chunk, LANES))
    pad_rows = jnp.zeros((LANES - MLSTM_HEADS, chunk), F32)
    sub8 = (GATE_ROWS, chunk)

    probs = []
    for b in range(nbatch):
        g = gr(b)
        btr = _cumsum_rows(g, triu)
        it_rows = g[:MLSTM_HEADS]
        bt_rows = btr[MLSTM_HEADS:2 * MLSTM_HEADS]
        u_cols = jnp.concatenate([it_rows - bt_rows, pad_rows], axis=0).T
        for h in range(MLSTM_HEADS):
            slab = slice((h // 2) * LANES, (h // 2 + 1) * LANES)
            keep = lo if h % 2 == 0 else jnp.logical_not(lo)
            q = mq(b, slab)
            k = mk(b, slab)
            probs.append(dict(
                b=b, h=h, idx=b * MLSTM_HEADS + h, cols=slice(h * MLSTM_DV, (h + 1) * MLSTM_DV),
                q=jnp.where(keep, q, jnp.zeros_like(q)), k=jnp.where(keep, k, jnp.zeros_like(k)),
                u_c=u_cols[:, h:h + 1], it_r=it_rows[h:h + 1, :], bt_r=bt_rows[h:h + 1, :]))
        yield

    for p in probs:
        p["m_old"] = m_ref[p["idx"]][:, 0:1]
        p["d"] = jnp.where(causal, p["bt_r"] + p["u_c"], NEG_INF)
        p["inter"] = p["bt_r"] + p["m_old"]
        p["q_t"] = p["q"].astype(F32).T.astype(BF16)
        p["kq"] = _dot(p["k"], p["q_t"])
        p["v_t"] = mv(p["b"], p["cols"]).astype(F32).T
        yield
    for p in probs:
        p["mt"] = jnp.maximum(jnp.max(p["d"], axis=0, keepdims=True), p["inter"])
        yield
    for p in probs:
        p["sm"] = p["kq"] * jnp.exp(p["d"] - p["mt"])
        p["wi"] = jnp.exp(p["inter"] - p["mt"])
        yield
    for p in probs:
        c_old = c_ref[p["idx"]]
        n_old = jnp.broadcast_to(n_ref[p["idx"]], (sub8[0], LANES))
        num = (_dot(p["v_t"].astype(BF16), p["sm"].astype(BF16))
               + p["wi"] * _dot(c_old.astype(BF16), p["q_t"]))
        den = (jnp.sum(p["sm"], axis=0, keepdims=True)
               + p["wi"] * _dot(n_old.astype(BF16), p["q_t"])[0:1])
        lower = jnp.maximum(jnp.abs(den), jnp.exp(-p["mt"]))
        h_t = num / lower
        p["hn"] = (h_t * lax.rsqrt(jnp.mean(h_t * h_t, axis=0, keepdims=True) + EPS)).T
        yield
    for p in probs:
        gate = jax.nn.sigmoid(og(p["b"], p["cols"]).astype(F32))
        hm_ref[p["b"], :, p["cols"]] = (p["hn"] * gh_ref[:, p["cols"]] * gate).astype(hm_ref.dtype)
        yield
    for p in probs:
        idx = p["idx"]
        b_last = p["bt_r"][:, chunk - 1:chunk]
        m_new = p["mt"][:, chunk - 1:chunk]
        wk = jnp.exp(b_last - p["bt_r"] + p["it_r"] - m_new)
        wc = jnp.exp(b_last + p["m_old"] - m_new)
        c_ref[idx] = wc * c_ref[idx] + _dot((p["v_t"] * wk).astype(BF16), p["k"])
        n_ref[idx] = wc * n_ref[idx] + _dot(jnp.broadcast_to(wk, sub8).astype(BF16), p["k"])[0:1]
        m_ref[idx] = jnp.broadcast_to(m_new, (1, LANES))
        yield


def _mixer_prompt_kernel(x_ref, g_ref, w_ref, wgr_ref, br_ref, gh_ref, gmem_ref, wck_ref, wcv_ref,
                         wg_ref, wu_ref, wd_ref, wo_ref, mem_ref,
                         q_ref, k_ref, v_ref, hm_ref, c_ref, n_ref, m_ref, wgu_out, wd_out, wo_out,
                         memk_out, memv_out, memkt_out, memvt_out, kwin_out, vwin_out,
                         mq_scr, mk_scr, mv_scr, og_scr, gr_scr, *, nbatch, chunk):
    s = pl.program_id(0)
    wr = s % 2
    rd = 1 - wr
    rows = nbatch * chunk

    def side_work():
        for c in range(0, wg_ref.shape[1], FFN_CHUNK):
            wgu_out[:, 2 * c:2 * c + FFN_CHUNK] = wg_ref[:, c:c + FFN_CHUNK].astype(BF16)
            wgu_out[:, 2 * c + FFN_CHUNK:2 * (c + FFN_CHUNK)] = wu_ref[:, c:c + FFN_CHUNK].astype(BF16)
        wd_out[...] = wd_ref[...].astype(BF16)
        wo_out[...] = wo_ref[...].astype(BF16)
        mn = _rmsnorm(mem_ref[...], gmem_ref[...]).astype(BF16)
        mk = _dot(mn, wck_ref[...])
        mv = _dot(mn, wcv_ref[...])
        memk_out[...] = mk
        memv_out[...] = mv
        memkt_out[...] = mk.T
        memvt_out[...] = mv.T

    def proj_stages():
        xn = _rmsnorm(x_ref[...].reshape(rows, x_ref.shape[-1]), g_ref[...]).astype(BF16)
        yield

        def mm(lo, hi):
            return _dot_nt(xn, w_ref[lo:hi, :])

        o = 0
        q_ref[...] = mm(o, o + ATT_WIDTH).astype(q_ref.dtype).reshape(q_ref.shape)
        yield
        o += ATT_WIDTH
        kv = mm(o, o + 2 * KV_WIDTH)
        k_ref[...] = kv[:, :KV_WIDTH].reshape(k_ref.shape)
        v_ref[...] = kv[:, KV_WIDTH:].reshape(v_ref.shape)
        yield
        o += 2 * KV_WIDTH
        mqk = mm(o, o + 2 * MQK_WIDTH)
        mq_scr[wr] = mqk[:, :MQK_WIDTH].astype(BF16)
        mk_scr[wr] = (mqk[:, MQK_WIDTH:] * (MLSTM_DK ** -0.5)).astype(BF16)
        yield
        o += 2 * MQK_WIDTH
        mv_scr[wr] = mm(o, o + MLSTM_WIDTH).astype(BF16)
        yield
        o += MLSTM_WIDTH
        og_scr[wr] = mm(o, o + MLSTM_WIDTH).astype(BF16)
        yield
        zr = _dot_nt(wgr_ref[...], xn) + br_ref[...]
        row = lax.broadcasted_iota(jnp.int32, zr.shape, 0)
        gr_scr[wr] = jnp.where(row < MLSTM_HEADS, zr, _log_sigmoid(zr))
        yield

    def mlstm_stages():
        def rows_of(scr):
            return lambda b, lanes: scr[rd, b * chunk:(b + 1) * chunk, lanes]
        return _mlstm_prompt_stages(
            rows_of(mq_scr), rows_of(mk_scr), rows_of(mv_scr), rows_of(og_scr),
            lambda b: gr_scr[rd, :, b * chunk:(b + 1) * chunk],
            gh_ref, hm_ref, c_ref, n_ref, m_ref, nbatch, chunk)

    @pl.when(s == 0)
    def _():
        c_ref[...] = jnp.zeros_like(c_ref)
        n_ref[...] = jnp.zeros_like(n_ref)
        m_ref[...] = jnp.zeros_like(m_ref)
        side_work()
        for _ in proj_stages():
            pass

    @pl.when(s > 0)
    def _():
        side_work()
        mlstm = mlstm_stages()
        for _ in proj_stages():
            for _ in range(MLSTM_STAGES_PER_PROJ_STAGE):
                next(mlstm, None)
        for _ in mlstm:
            pass

    @pl.when(s == pl.num_programs(0) - 2)
    def _():
        for b in range(nbatch):
            kwin_out[b] = k_ref[b, chunk - WINDOW:, :].T
            vwin_out[b] = v_ref[b, chunk - WINDOW:, :].T


def _mixer_prompt(x, mem, w, w_gate, w_up, w_down, w_out, nbatch, chunk):
    t, d = x.shape
    seq = t // nbatch
    nchunks = seq // chunk
    nstate = nbatch * MLSTM_HEADS
    rows = nbatch * chunk
    hidden = w_down.shape[0]
    assert hidden % FFN_CHUNK == 0
    assert d % (nchunks * BF16_SUBLANES) == 0 and hidden % (nchunks * BF16_SUBLANES) == 0
    mem_tokens = mem.shape[0] // nbatch
    blocks_per_batch = mem_tokens // LANES
    nblocks = nbatch * blocks_per_batch
    steps_per_block = nchunks // nblocks
    assert mem_tokens % LANES == 0 and nchunks % nblocks == 0 and chunk >= WINDOW

    def mem_block(s):
        return jnp.minimum(s // steps_per_block, nblocks - 1)

    mem_rows = pl.BlockSpec((LANES, d), lambda s: (mem_block(s), 0))
    mem_kv = pl.BlockSpec((LANES, CROSS_WIDTH), lambda s: (mem_block(s), 0))
    mem_kv_t = pl.BlockSpec((CROSS_WIDTH, LANES),
                            lambda s: (mem_block(s) // blocks_per_batch, mem_block(s) % blocks_per_batch))
    window_t = pl.BlockSpec((nbatch, KV_WIDTH, WINDOW), lambda s: (0, 0, 0))

    def proj_chunk(width):
        return pl.BlockSpec((nbatch, chunk, width), lambda s: (0, jnp.minimum(s, nchunks - 1), 0))

    def state(shape):
        return pl.BlockSpec(shape, lambda s: (0, 0, 0))

    def weight_rows(a, width=None):
        return pl.BlockSpec((a.shape[0] // nchunks, width or a.shape[1]),
                            lambda s: (jnp.minimum(s, nchunks - 1), 0))

    names = ["g_mix", "w_main", "w_gate_rows", "b_gate_rows", "g_head", "g_mem", "w_ck", "w_cv"]
    (q, k, v, hm, c_st, n_st, m_st, w_gate_up_bf, w_down_bf, w_out_bf, memk, memv, memk_t, memv_t,
     kwin_t, vwin_t) = pl.pallas_call(
        functools.partial(_mixer_prompt_kernel, nbatch=nbatch, chunk=chunk),
        grid=(nchunks + 1,),
        in_specs=[proj_chunk(d)] + _weight_specs(w, names) + [
            weight_rows(w_gate), weight_rows(w_up), weight_rows(w_down), weight_rows(w_out),
            mem_rows,
        ],
        out_specs=[
            proj_chunk(ATT_WIDTH), proj_chunk(KV_WIDTH), proj_chunk(KV_WIDTH),
            pl.BlockSpec((nbatch, chunk, MLSTM_WIDTH), lambda s: (0, jnp.maximum(s - 1, 0), 0)),
            state((nstate, MLSTM_DV, LANES)), state((nstate, 1, LANES)), state((nstate, 1, LANES)),
            weight_rows(w_gate, 2 * hidden), weight_rows(w_down), weight_rows(w_out),
            mem_kv, mem_kv, mem_kv_t, mem_kv_t, window_t, window_t,
        ],
        out_shape=[
            jax.ShapeDtypeStruct((nbatch, seq, ATT_WIDTH), BF16),
            jax.ShapeDtypeStruct((nbatch, seq, KV_WIDTH), F32),
            jax.ShapeDtypeStruct((nbatch, seq, KV_WIDTH), F32),
            jax.ShapeDtypeStruct((nbatch, seq, MLSTM_WIDTH), BF16),
            jax.ShapeDtypeStruct((nstate, MLSTM_DV, LANES), F32),
            jax.ShapeDtypeStruct((nstate, 1, LANES), F32),
            jax.ShapeDtypeStruct((nstate, 1, LANES), F32),
            jax.ShapeDtypeStruct((d, 2 * hidden), BF16),
            jax.ShapeDtypeStruct((hidden, d), BF16),
            jax.ShapeDtypeStruct(w_out.shape, BF16),
            jax.ShapeDtypeStruct((mem.shape[0], CROSS_WIDTH), F32),
            jax.ShapeDtypeStruct((mem.shape[0], CROSS_WIDTH), F32),
            jax.ShapeDtypeStruct((nbatch * CROSS_WIDTH, mem_tokens), F32),
            jax.ShapeDtypeStruct((nbatch * CROSS_WIDTH, mem_tokens), F32),
            jax.ShapeDtypeStruct((nbatch, KV_WIDTH, WINDOW), F32),
            jax.ShapeDtypeStruct((nbatch, KV_WIDTH, WINDOW), F32),
        ],
        scratch_shapes=[
            pltpu.VMEM((2, rows, MQK_WIDTH), BF16), pltpu.VMEM((2, rows, MQK_WIDTH), BF16),
            pltpu.VMEM((2, rows, MLSTM_WIDTH), BF16), pltpu.VMEM((2, rows, MLSTM_WIDTH), BF16),
            pltpu.VMEM((2, GATE_ROWS, rows), F32),
        ],
        compiler_params=pltpu.CompilerParams(
            dimension_semantics=("arbitrary",), vmem_limit_bytes=VMEM_LIMIT,
            allow_input_fusion=[False] + [True] * len(names) + [False] * 5),
    )(x.reshape(nbatch, seq, d), *[w[n] for n in names], w_gate, w_up, w_down, w_out, mem)
    q = q.reshape(t, ATT_WIDTH)
    k = k.reshape(t, KV_WIDTH)
    v = v.reshape(t, KV_WIDTH)

    c_st = c_st.reshape(nbatch, MLSTM_HEADS, MLSTM_DV, 2, MLSTM_DK)
    n_st = n_st.reshape(nbatch, MLSTM_HEADS, 2, MLSTM_DK)
    c_fin = jnp.stack([c_st[:, h, :, h % 2, :] for h in range(MLSTM_HEADS)], axis=1)
    n_fin = jnp.stack([n_st[:, h, h % 2, :] for h in range(MLSTM_HEADS)], axis=1)
    m_fin = m_st[:, 0, 0].reshape(nbatch, MLSTM_HEADS)
    bf16_weights = {"w_gate_up": w_gate_up_bf, "w_down": w_down_bf, "w_out": w_out_bf}
    return (q, k, v, hm.reshape(t, MLSTM_WIDTH), c_fin, n_fin, m_fin, bf16_weights, memk, memv,
            memk_t, memv_t, kwin_t, vwin_t)


def _mlstm_sample_kernel(mq_ref, mk_ref, mkt_ref, mv_ref, og_ref, gc_ref, gr_ref, mrow_ref, c_ref,
                         n_ref, gh_ref, hm_ref, co_ref, no_ref, mt_ref, *, nb, tq):
    rows = nb * tq
    shift = tq.bit_length() - 1
    r = lax.broadcasted_iota(jnp.int32, (rows, rows), 0)
    s = lax.broadcasted_iota(jnp.int32, (rows, rows), 1)
    same = (r >> shift) == (s >> shift)
    valid = same & (s <= r)
    tril = jnp.where(valid, 1.0, 0.0).astype(BF16)
    triu = jnp.where(same & (r <= s), 1.0, 0.0).astype(BF16)
    row_batch = lax.broadcasted_iota(jnp.int32, (rows, 1), 0) >> shift
    lane = lax.broadcasted_iota(jnp.int32, (rows, LANES), 1)

    gc = gc_ref[...]
    gr = gr_ref[...]
    btc = _cumsum_cols(tril, gc)
    btr = _cumsum_rows(gr, triu)
    m_rows = mrow_ref[...]
    mt_all = jnp.zeros((rows, LANES), F32)

    def last_of_batch(col):
        parts = [jnp.broadcast_to(col[(b + 1) * tq - 1:(b + 1) * tq], (tq, 1)) for b in range(nb)]
        return jnp.concatenate(parts, axis=0)

    dk = MLSTM_DK
    lo = _low_half((rows, LANES))
    own_q = row_batch == (lax.broadcasted_iota(jnp.int32, (rows, nb * dk), 1) >> _log2(dk))
    lane_batch = lax.broadcasted_iota(jnp.int32, (dk, rows), 1) >> shift

    def state_rows(b, h):
        return slice((b * MLSTM_HEADS + h) * dk, (b * MLSTM_HEADS + h + 1) * dk)

    for h in range(MLSTM_HEADS):
        slab = slice((h // 2) * LANES, (h // 2 + 1) * LANES)
        qs, ks = mq_ref[:, slab], mk_ref[:, slab]
        qsw, ksw = pltpu.roll(qs, dk, axis=1), pltpu.roll(ks, dk, axis=1)
        q2 = jnp.where(lo, qs, qsw) if h % 2 == 0 else jnp.where(lo, qsw, qs)
        k2 = jnp.where(lo, ks, ksw) if h % 2 == 0 else jnp.where(lo, ksw, ks)
        qb, kb = q2[:, :dk].astype(BF16), k2[:, :dk].astype(BF16)
        vb = mv_ref[:, h * MLSTM_DV:(h + 1) * MLSTM_DV].astype(BF16)
        it_c = gc[:, h:h + 1]
        bt_c = btc[:, MLSTM_HEADS + h:MLSTM_HEADS + h + 1]
        it_r = gr[h:h + 1, :]
        bt_r = btr[MLSTM_HEADS + h:MLSTM_HEADS + h + 1, :]
        m_c = m_rows[:, h:h + 1]

        num, ssum, mt, wi = _mlstm_intra(qb, kb, vb, it_c, bt_c, it_r, bt_r, m_c, valid)
        qx = jnp.where(own_q, jnp.concatenate([q2] * (nb // 2), axis=1), 0.0).astype(BF16)
        c_stack = jnp.concatenate([c_ref[state_rows(b, h), :] for b in range(nb)], axis=0)
        qc = _dot(qx, c_stack.astype(BF16))
        n_rows = jnp.concatenate(
            [jnp.broadcast_to(n_ref[b * MLSTM_HEADS + h], (tq, dk)) for b in range(nb)], axis=0)
        num = num + wi * qc
        den = ssum + wi * jnp.sum(qb.astype(F32) * n_rows, axis=-1, keepdims=True)
        og = og_ref[:, h * MLSTM_DV:(h + 1) * MLSTM_DV]
        g_head = gh_ref[:, h * MLSTM_DV:(h + 1) * MLSTM_DV]
        hm_ref[:, h * MLSTM_DV:(h + 1) * MLSTM_DV] = _mlstm_head_out(num, den, mt, og, g_head)

        b_last = last_of_batch(bt_c)
        m_new = last_of_batch(mt)
        wk = jnp.exp(b_last - bt_c + it_c - m_new)
        wc = jnp.exp(b_last + m_c - m_new)
        wkv = (wk * vb.astype(F32)).astype(BF16)
        wkk = wk * kb.astype(F32)
        kt = mkt_ref[h * dk:(h + 1) * dk, :]
        kx = jnp.concatenate([jnp.where(lane_batch == b, kt, 0.0) for b in range(nb)], axis=0)
        upd = _dot(kx.astype(BF16), wkv)
        for b in range(nb):
            idx = b * MLSTM_HEADS + h
            wc_b = wc[b * tq:b * tq + 1]
            co_ref[state_rows(b, h), :] = wc_b * c_ref[state_rows(b, h), :] + upd[b * dk:(b + 1) * dk]
            no_ref[idx] = wc_b * n_ref[idx] + jnp.sum(wkk[b * tq:(b + 1) * tq], axis=0, keepdims=True)
        mt_all = jnp.where(lane == h, mt, mt_all)
    mt_ref[...] = mt_all


def _mlstm_sample(mq, mk, mkt, mv, og, gcol, grow, m_rows, ct_in, n_in, g_head, nbatch, nb):
    t = mq.shape[0]
    tq = t // nbatch
    rows = nb * tq
    assert tq & (tq - 1) == 0 and rows == LANES and nb % 2 == 0

    def tok(width):
        return pl.BlockSpec((rows, width), lambda i: (i, 0))

    ct_spec = pl.BlockSpec((nb * MLSTM_HEADS * MLSTM_DK, MLSTM_DV), lambda i: (i, 0))
    n_spec = pl.BlockSpec((nb * MLSTM_HEADS, 1, MLSTM_DK), lambda i: (i, 0, 0))
    return pl.pallas_call(
        functools.partial(_mlstm_sample_kernel, nb=nb, tq=tq),
        grid=(nbatch // nb,),
        in_specs=[
            tok(MQK_WIDTH), tok(MQK_WIDTH),
            pl.BlockSpec((MQK_WIDTH, rows), lambda i: (0, i)),
            tok(MLSTM_WIDTH), tok(MLSTM_WIDTH), tok(LANES),
            pl.BlockSpec((GATE_ROWS, rows), lambda i: (0, i)),
            tok(LANES),
            ct_spec, n_spec,
            _const_spec((1, MLSTM_WIDTH)),
        ],
        out_specs=[tok(MLSTM_WIDTH), ct_spec, n_spec, tok(LANES)],
        out_shape=[
            jax.ShapeDtypeStruct((t, MLSTM_WIDTH), F32),
            jax.ShapeDtypeStruct(ct_in.shape, F32),
            jax.ShapeDtypeStruct(n_in.shape, F32),
            jax.ShapeDtypeStruct((t, LANES), F32),
        ],
        compiler_params=pltpu.CompilerParams(
            dimension_semantics=("arbitrary",), vmem_limit_bytes=VMEM_LIMIT,
            allow_input_fusion=[False] * 7 + [True, False, True, False]),
    )(mq, mk, mkt, mv, og, gcol, grow, m_rows, ct_in, n_in, g_head)


def _mix_stage(x, att, hm, wo_ref, gcr_ref, wcq_ref):
    cat = jnp.concatenate([att.astype(BF16), hm.astype(BF16)], axis=1)
    y1 = x + _dot(cat, wo_ref[...])
    qc = _dot(_rmsnorm(y1, gcr_ref[...]).astype(BF16), wcq_ref[...])
    return y1, qc


def _cross_attention(qc, mk, mv):
    slabs = []
    for j in range(CROSS_WIDTH // LANES):
        qs = qc[:, j * LANES:(j + 1) * LANES]
        ks = mk[:, j * LANES:(j + 1) * LANES]
        vs = mv[:, j * LANES:(j + 1) * LANES]
        n = qs.shape[0]
        lo = _low_half(qs.shape)
        zero = jnp.zeros_like(qs)
        q2 = jnp.concatenate([jnp.where(lo, qs, zero), jnp.where(lo, zero, qs)], axis=0)
        s = _dot_nt(q2 * (HEAD_DIM ** -0.5), ks)
        p = jnp.exp(s - jnp.max(s, axis=-1, keepdims=True))
        o2 = _dot(p.astype(BF16), vs) / jnp.sum(p, axis=-1, keepdims=True)
        slabs.append(jnp.where(lo, o2[:n], o2[n:]))
    return jnp.concatenate(slabs, axis=1)


FFN_CHUNK = 256


def _finish(y3, gfin_ref, final):
    return _rmsnorm(y3, gfin_ref[...]) if final else y3


def _ffn_stage(y1, o, wco_ref, gf_ref, wgu_ref, wd_ref, act_ref, between=None):
    y2 = y1 + _dot(o.astype(BF16), wco_ref[...])
    xn = _rmsnorm(y2, gf_ref[...]).astype(BF16)
    for c in range(wd_ref.shape[0] // FFN_CHUNK):
        gu = _dot(xn, wgu_ref[:, 2 * c * FFN_CHUNK:2 * (c + 1) * FFN_CHUNK])
        g, u = gu[:, :FFN_CHUNK], gu[:, FFN_CHUNK:]
        act_ref[:, c * FFN_CHUNK:(c + 1) * FFN_CHUNK] = (g * jax.nn.sigmoid(g) * u).astype(BF16)
        if between is not None:
            between()
    return y2 + _dot(act_ref[...], wd_ref[...])


def _post_kernel(sink_ref, q_ref, kc_ref, kp_ref, vc_ref, vp_ref, x_ref, hm_ref, mk_ref, mv_ref,
                 xd_ref, attd_ref, hmd_ref, mkd_ref, mvd_ref,
                 wo_ref, gcr_ref, wcq_ref, wco_ref, gf_ref, wgu_ref, wd_ref, gfin_ref,
                 o_ref, od_ref, act_ref, att_ref, y3_ref, *, final, ntiles, tiles_per_seq, tq):
    tm = x_ref.shape[0]
    s = pl.program_id(0)
    tile = jnp.minimum(s, ntiles - 1)
    first_block = (tile % tiles_per_seq) == 0
    slot = s % 2

    def swa_stages():
        def store(j, value):
            att_ref[slot, j * WINDOW:(j + 1) * WINDOW, :] = value.astype(att_ref.dtype)
        return _swa_prompt_stages(sink_ref, q_ref, kc_ref, kp_ref, vc_ref, vp_ref, first_block, store)

    def finish():
        done = _finish(y3_ref[...], gfin_ref, final)
        o_ref[...] = done[:tm]
        od_ref[...] = done[tm:]

    @pl.when(s == 0)
    def _():
        y3_ref[...] = jnp.zeros_like(y3_ref)
        for _ in swa_stages():
            pass

    @pl.when(jnp.logical_and(s > 0, s <= ntiles))
    def _():
        swa = swa_stages()

        def advance(n=SWA_STAGES_PER_FFN_CHUNK):
            for _ in range(n):
                next(swa, None)

        finish()
        x = jnp.concatenate([x_ref[...], xd_ref[...]], axis=0)
        att = jnp.concatenate([att_ref[1 - slot], attd_ref[...].astype(BF16)], axis=0)
        hm = jnp.concatenate([hm_ref[...], hmd_ref[...].astype(BF16)], axis=0)
        y1, qc = _mix_stage(x, att, hm, wo_ref, gcr_ref, wcq_ref)
        advance()
        o = jnp.concatenate([
            _cross_attention(qc[:tm].astype(BF16), mk_ref[...].astype(BF16), mv_ref[...].astype(BF16)),
            _cross_attention_decode(qc[tm:], mkd_ref, mvd_ref, tq)], axis=0)
        advance()
        y3_ref[...] = _ffn_stage(y1, o, wco_ref, gf_ref, wgu_ref, wd_ref, act_ref, between=advance)
        for _ in swa:
            pass

    @pl.when(s > ntiles)
    def _():
        finish()


_MIX_WEIGHTS = ("w_out", "g_cross", "w_cq")
_FFN_WEIGHTS = ("w_co", "g_ffn", "w_gate_up", "w_down", "g_final")


def _post(x, q, k, v, hm, memk, memv, xd, attd, hmd, memkd_t, memvd_t, sinks, w, nbatch, nseq, final, tm):
    t, d = x.shape
    ntiles = t // tm
    tpb = ntiles // nbatch
    wpt = tm // WINDOW
    m = memk.shape[0] // nbatch
    hidden = w["w_down"].shape[0]
    td = xd.shape[0]
    tq = td // nseq
    slab = td // ntiles
    assert td % ntiles == 0 and slab % tq == 0 and slab % BF16_SUBLANES == 0
    mem_d = memkd_t.shape[1]

    def attn_tile(s):
        return jnp.minimum(s, ntiles - 1)

    def post_tile(s):
        return jnp.clip(s - 1, 0, ntiles - 1)

    def done_tile(s):
        return jnp.maximum(s - 2, 0)

    def attn_rows(width):
        return pl.BlockSpec((tm, width), lambda s: (attn_tile(s), 0))

    def prev_window(width):
        return pl.BlockSpec((WINDOW, width), lambda s: (jnp.maximum(attn_tile(s) * wpt - 1, 0), 0))

    def post_rows(width):
        return pl.BlockSpec((tm, width), lambda s: (post_tile(s), 0))

    def post_slab(width):
        return pl.BlockSpec((slab, width), lambda s: (post_tile(s), 0))

    mem_spec = pl.BlockSpec((m, CROSS_WIDTH), lambda s: (post_tile(s) // tpb, 0))
    memd_spec = pl.BlockSpec((slab // tq * CROSS_WIDTH, mem_d), lambda s: (post_tile(s), 0))
    names = _MIX_WEIGHTS + _FFN_WEIGHTS
    return pl.pallas_call(
        functools.partial(_post_kernel, final=final, ntiles=ntiles, tiles_per_seq=tpb, tq=tq),
        grid=(ntiles + 2,),
        in_specs=[
            pl.BlockSpec(memory_space=pltpu.SMEM),
            attn_rows(ATT_WIDTH), attn_rows(KV_WIDTH), prev_window(KV_WIDTH),
            attn_rows(KV_WIDTH), prev_window(KV_WIDTH),
            post_rows(d), post_rows(MLSTM_WIDTH), mem_spec, mem_spec,
            post_slab(d), post_slab(ATT_WIDTH), post_slab(MLSTM_WIDTH), memd_spec, memd_spec,
        ] + _weight_specs(w, names),
        out_specs=[pl.BlockSpec((tm, d), lambda s: (done_tile(s), 0)),
                   pl.BlockSpec((slab, d), lambda s: (done_tile(s), 0))],
        out_shape=[jax.ShapeDtypeStruct((t, d), F32), jax.ShapeDtypeStruct((td, d), F32)],
        scratch_shapes=[pltpu.VMEM((tm + slab, hidden), BF16), pltpu.VMEM((2, tm, ATT_WIDTH), BF16),
                        pltpu.VMEM((tm + slab, d), F32)],
        compiler_params=pltpu.CompilerParams(
            dimension_semantics=("arbitrary",), vmem_limit_bytes=VMEM_LIMIT),
    )(sinks, q, k, k, v, v, x, hm, memk, memv, xd, attd, hmd, memkd_t, memvd_t,
      *[w[n] for n in names])


def _cross_attention_decode(qc, mk_ref, mv_ref, tq):
    n = CROSS_HEADS * tq
    row_head = lax.broadcasted_iota(jnp.int32, (n, CROSS_WIDTH), 0) >> _log2(tq)
    lane_head = lax.broadcasted_iota(jnp.int32, (n, CROSS_WIDTH), 1) >> _log2(HEAD_DIM)
    own = row_head == lane_head
    outs = []
    for b in range(qc.shape[0] // tq):
        q = qc[b * tq:(b + 1) * tq, :]
        qx = jnp.where(own, jnp.concatenate([q] * CROSS_HEADS, axis=0), 0.0).astype(BF16)
        mk = mk_ref[b * CROSS_WIDTH:(b + 1) * CROSS_WIDTH, :].astype(BF16)
        mv = mv_ref[b * CROSS_WIDTH:(b + 1) * CROSS_WIDTH, :].astype(BF16)
        s = _dot(qx, mk) * (HEAD_DIM ** -0.5)
        p = jnp.exp(s - jnp.max(s, axis=-1, keepdims=True))
        pv = _dot_nt(p.astype(BF16), mv) / jnp.sum(p, axis=-1, keepdims=True)
        pv = jnp.where(own, pv, 0.0)
        out = pv[0:tq]
        for h in range(1, CROSS_HEADS):
            out = out + pv[h * tq:(h + 1) * tq]
        outs.append(out)
    return jnp.concatenate(outs, axis=0)


def _layer_weights(l, w_in, b_igate, b_fgate, g_mlstm_head, g_mix, g_cross, g_mem, w_cq, w_ck, w_cv,
                   w_co, g_ffn, g_final):
    gate_w = w_in[l][:, MAIN_WIDTH:]
    gate_b = jnp.concatenate([b_igate[l], b_fgate[l]]).astype(F32)
    ngate = gate_b.shape[0]
    row = lambda a: a.astype(F32).reshape(1, -1)
    w_t = w_in[l].T
    k0 = ATT_WIDTH
    mk0 = ATT_WIDTH + 2 * KV_WIDTH + MQK_WIDTH
    return {
        "w_kvk_t": jnp.concatenate([w_t[k0:k0 + 2 * KV_WIDTH], w_t[mk0:mk0 + MQK_WIDTH]]).astype(BF16),
        "w_main": w_t[:MAIN_WIDTH].astype(BF16),
        "w_gate_cols": jnp.pad(gate_w, ((0, 0), (0, LANES - ngate))).astype(BF16),
        "w_gate_rows": jnp.pad(gate_w.T, ((0, GATE_ROWS - ngate), (0, 0))).astype(BF16),
        "b_gate_cols": jnp.pad(gate_b, (0, LANES - ngate)).reshape(1, LANES),
        "b_gate_rows": jnp.pad(gate_b, (0, GATE_ROWS - ngate)).reshape(GATE_ROWS, 1),
        "g_mix": row(g_mix[l]), "g_cross": row(g_cross[l]), "g_mem": row(g_mem[l]),
        "g_ffn": row(g_ffn[l]), "g_final": row(g_final), "g_head": row(g_mlstm_head[l]),
        "w_cq": w_cq[l].astype(BF16), "w_ck": w_ck[l].astype(BF16),
        "w_cv": w_cv[l].astype(BF16), "w_co": w_co[l].astype(BF16),
    }


TOKEN_TILE = 512
MLSTM_STAGES_PER_PROJ_STAGE = 15
SWA_STAGES_PER_FFN_CHUNK = 3
MLSTM_CHUNK_ROWS = 256
SWA_SAMPLE_BATCH = 16
MLSTM_SAMPLE_BATCH = 16


def kernel(x_prompt, x_sample, mem_prompt, cache_swa_k, cache_swa_v, state_mlstm_C, state_mlstm_n,
           state_mlstm_m, cache_mem_k, cache_mem_v, w_in, b_igate, b_fgate, attn_sinks,
           g_mlstm_head, w_out, g_mix, g_cross, g_mem, w_cq, w_ck, w_cv, w_co, g_ffn, w_gate,
           w_up, w_down, g_final):
    depth = w_in.shape[0]
    bp, sp, d = x_prompt.shape
    bs, ss, _ = x_sample.shape
    mem_tokens = mem_prompt.shape[1]
    past = cache_swa_k.shape[2]
    yp = x_prompt.reshape(bp * sp, d)
    ys = x_sample.reshape(bs * ss, d)
    mem = mem_prompt.reshape(bp * mem_tokens, d)
    outs = [[] for _ in range(12)]

    for l in range(depth):
        final = l == depth - 1
        w = _layer_weights(l, w_in, b_igate, b_fgate, g_mlstm_head, g_mix, g_cross, g_mem, w_cq,
                           w_ck, w_cv, w_co, g_ffn, g_final)
        sinks = attn_sinks[l].astype(F32)

        q, k, v, hm, c_p, n_p, m_p, cast, memk, memv, memk_t, memv_t, kwin_t, vwin_t = _mixer_prompt(
            yp, mem, w, w_gate[l].astype(F32), w_up[l].astype(F32), w_down[l].astype(F32),
            w_out[l].astype(F32), bp, MLSTM_CHUNK_ROWS)
        w.update(cast)

        def tokens_second(a, heads, tokens):
            return jnp.transpose(a.reshape(bp, heads, HEAD_DIM, tokens), (0, 3, 1, 2))

        outs[0].append(tokens_second(kwin_t, ATT_KV_HEADS, WINDOW))
        outs[1].append(tokens_second(vwin_t, ATT_KV_HEADS, WINDOW))
        outs[2].append(c_p)
        outs[3].append(n_p)
        outs[4].append(m_p)
        outs[5].append(tokens_second(memk_t, CROSS_HEADS, mem_tokens))
        outs[6].append(tokens_second(memv_t, CROSS_HEADS, mem_tokens))

        def tokens_last(a):
            return jnp.transpose(a.astype(F32), (0, 2, 3, 1)).reshape(-1, a.shape[1])

        qd, mq, mk, mv, og, grow, gcol, knt, vnt, mkt = _decode_proj(ys, w, TOKEN_TILE)
        att_d, kbuf_t, vbuf_t = _swa_sample(
            qd, knt, vnt, tokens_last(cache_swa_k[l]), tokens_last(cache_swa_v[l]), sinks, bs,
            SWA_SAMPLE_BATCH)
        m_rows = jnp.pad(jnp.repeat(state_mlstm_m[l].astype(F32), ss, axis=0),
                         ((0, 0), (0, LANES - MLSTM_HEADS)))
        ct_in = jnp.swapaxes(state_mlstm_C[l].astype(F32), 2, 3).reshape(-1, MLSTM_DV)
        hm_d, ct_s, n_s, mt = _mlstm_sample(
            mq, mk, mkt, mv, og, gcol, grow, m_rows, ct_in,
            state_mlstm_n[l].astype(F32).reshape(bs * MLSTM_HEADS, 1, MLSTM_DK),
            w["g_head"], bs, MLSTM_SAMPLE_BATCH)

        yp, ys = _post(yp, q, k, v, hm, memk, memv, ys, att_d, hm_d, tokens_last(cache_mem_k[l]),
                       tokens_last(cache_mem_v[l]), sinks, w, bp, bs, final, TOKEN_TILE)
        buf_t_shape = (bs, ATT_KV_HEADS, HEAD_DIM, past)
        outs[7].append(jnp.transpose(kbuf_t.reshape(buf_t_shape), (0, 3, 1, 2)))
        outs[8].append(jnp.transpose(vbuf_t.reshape(buf_t_shape), (0, 3, 1, 2)))
        outs[9].append(jnp.swapaxes(ct_s.reshape(bs, MLSTM_HEADS, MLSTM_DK, MLSTM_DV), 2, 3))
        outs[10].append(n_s.reshape(bs, MLSTM_HEADS, MLSTM_DK))
        outs[11].append(mt.reshape(bs, ss, LANES)[:, ss - 1, :MLSTM_HEADS])

    return (yp.reshape(bp, sp, d), ys.reshape(bs, ss, d)) + tuple(jnp.stack(o) for o in outs)
```

```python
import functools

import jax
import jax.numpy as jnp
from jax import lax
from jax.experimental import pallas as pl
from jax.experimental.pallas import tpu as pltpu

F32 = jnp.float32
BF16 = jnp.bfloat16

HEAD_DIM = 64
ATT_GROUP = 4
ATT_KV_HEADS = 2
ATT_WIDTH = 512
KV_WIDTH = ATT_KV_HEADS * HEAD_DIM
WINDOW = 128
MLSTM_HEADS = 4
MLSTM_DV = 128
MLSTM_DK = 64
MQK_WIDTH = MLSTM_HEADS * MLSTM_DK
MLSTM_WIDTH = MLSTM_HEADS * MLSTM_DV
MAIN_WIDTH = ATT_WIDTH + 2 * KV_WIDTH + 2 * MQK_WIDTH + 2 * MLSTM_WIDTH
CROSS_HEADS = 4
CROSS_WIDTH = CROSS_HEADS * HEAD_DIM
EPS = 1e-6
NEG_INF = float("-inf")
assert HEAD_DIM == 4 ** 3

LANES = 128
BF16_SUBLANES = 16
GATE_ROWS = BF16_SUBLANES
VMEM_LIMIT = 56 * 1024 * 1024

NT = (((1,), (1,)), ((), ()))


def _dot(a, b):
    return jnp.dot(a, b, preferred_element_type=F32)


def _dot_nt(a, b):
    return lax.dot_general(a, b, NT, preferred_element_type=F32)


def _rmsnorm(x, g):
    return x * lax.rsqrt(jnp.mean(x * x, axis=-1, keepdims=True) + EPS) * g


def _log_sigmoid(x):
    return jnp.minimum(x, 0.0) - jnp.log1p(jnp.exp(-jnp.abs(x)))


def _split3(x):
    hi = x.astype(BF16)
    r1 = x - hi.astype(F32)
    mid = r1.astype(BF16)
    lo = (r1 - mid.astype(F32)).astype(BF16)
    return hi, mid, lo


def _cumsum_cols(tri, x):
    hi, mid, lo = _split3(x)
    return _dot(tri, hi) + _dot(tri, mid) + _dot(tri, lo)


def _cumsum_rows(x, tri):
    hi, mid, lo = _split3(x)
    return _dot(hi, tri) + _dot(mid, tri) + _dot(lo, tri)


def _log2(n):
    assert n > 0 and n & (n - 1) == 0, n
    return n.bit_length() - 1


def _low_half(shape):
    return lax.broadcasted_iota(jnp.int32, shape, 1) < HEAD_DIM


def _decode_proj_kernel(x_ref, g_ref, w_ref, wgr_ref, br_ref, wgc_ref, bc_ref, wt_ref,
                        q_ref, mq_ref, mk_ref, mv_ref, og_ref, gr_ref, gc_ref, kt_ref, vt_ref, mkt_ref):
    xn = _rmsnorm(x_ref[...], g_ref[...]).astype(BF16)
    zt = _dot_nt(wt_ref[...], xn)
    kt_ref[...] = zt[:KV_WIDTH]
    vt_ref[...] = zt[KV_WIDTH:2 * KV_WIDTH]
    mkt_ref[...] = zt[2 * KV_WIDTH:] * (MLSTM_DK ** -0.5)

    def mm(lo, hi):
        return _dot_nt(xn, w_ref[lo:hi, :])

    o = 0
    q_ref[...] = mm(o, o + ATT_WIDTH)
    o += ATT_WIDTH + 2 * KV_WIDTH
    mqk = mm(o, o + 2 * MQK_WIDTH)
    mq_ref[...] = mqk[:, :MQK_WIDTH]
    mk_ref[...] = mqk[:, MQK_WIDTH:] * (MLSTM_DK ** -0.5)
    o += 2 * MQK_WIDTH
    mv_ref[...] = mm(o, o + MLSTM_WIDTH)
    o += MLSTM_WIDTH
    og_ref[...] = mm(o, o + MLSTM_WIDTH)

    zc = _dot(xn, wgc_ref[...]) + bc_ref[...]
    lane = lax.broadcasted_iota(jnp.int32, zc.shape, 1)
    gc_ref[...] = jnp.where(lane < MLSTM_HEADS, zc, _log_sigmoid(zc))
    zr = _dot_nt(wgr_ref[...], xn) + br_ref[...]
    row = lax.broadcasted_iota(jnp.int32, zr.shape, 0)
    gr_ref[...] = jnp.where(row < MLSTM_HEADS, zr, _log_sigmoid(zr))


def _const_spec(shape):
    nd = len(shape)
    return pl.BlockSpec(shape, lambda *_: (0,) * nd, pipeline_mode=pl.Buffered(1))


def _weight_specs(w, names):
    return [_const_spec(w[n].shape) for n in names]


def _decode_proj(x, w, tm):
    t, d = x.shape

    def rows(width):
        return pl.BlockSpec((tm, width), lambda i: (i, 0))

    def lanes(height):
        return pl.BlockSpec((height, tm), lambda i: (0, i))

    outs = [(rows, ATT_WIDTH), (rows, MQK_WIDTH), (rows, MQK_WIDTH), (rows, MLSTM_WIDTH),
            (rows, MLSTM_WIDTH), (lanes, GATE_ROWS), (rows, LANES), (lanes, KV_WIDTH),
            (lanes, KV_WIDTH), (lanes, MQK_WIDTH)]
    names = ["g_mix", "w_main", "w_gate_rows", "b_gate_rows", "w_gate_cols", "b_gate_cols", "w_kvk_t"]
    return pl.pallas_call(
        _decode_proj_kernel,
        grid=(t // tm,),
        in_specs=[rows(d)] + _weight_specs(w, names),
        out_specs=[kind(size) for kind, size in outs],
        out_shape=[jax.ShapeDtypeStruct((t, size) if kind is rows else (size, t), F32)
                   for kind, size in outs],
        compiler_params=pltpu.CompilerParams(
            dimension_semantics=("arbitrary",), vmem_limit_bytes=VMEM_LIMIT,
            allow_input_fusion=[False] + [True] * len(names)),
    )(x, *[w[n] for n in names])


def _dup_heads(x):
    swapped = pltpu.roll(x, HEAD_DIM, axis=1)
    lo = _low_half(x.shape)
    return (jnp.where(lo, x, swapped).astype(BF16), jnp.where(lo, swapped, x).astype(BF16))


def _stack_query_heads(q, h):
    parts = []
    for g in range(ATT_GROUP):
        hq = h * ATT_GROUP + g
        slab = q[:, (hq // 2) * LANES:(hq // 2 + 1) * LANES]
        lo = _low_half(slab.shape)
        keep = lo if hq % 2 == 0 else jnp.logical_not(lo)
        parts.append(jnp.where(keep, slab, jnp.zeros_like(slab)))
    return jnp.concatenate(parts, axis=0).astype(BF16)


def _sink_column(sink_ref, h, rows_per_head):
    n = ATT_GROUP * rows_per_head
    grp = lax.broadcasted_iota(jnp.int32, (n, 1), 0) >> _log2(rows_per_head)
    col = jnp.full((n, 1), sink_ref[h * ATT_GROUP], F32)
    for g in range(1, ATT_GROUP):
        col = jnp.where(grp == g, sink_ref[h * ATT_GROUP + g], col)
    return col


def _swa_prompt_stages(sink_ref, q_ref, kc_ref, kp_ref, vc_ref, vp_ref, first_block, store):
    qb = q_ref.shape[0]
    k_all = jnp.concatenate([kp_ref[...], kc_ref[...]], axis=0)
    v_all = jnp.concatenate([vp_ref[...], vc_ref[...]], axis=0)
    kd = _dup_heads(k_all)
    v_t = v_all.T.astype(BF16)

    n = ATT_GROUP * WINDOW
    key = lax.broadcasted_iota(jnp.int32, (2 * WINDOW, n), 0)
    qry = lax.broadcasted_iota(jnp.int32, (2 * WINDOW, n), 1) & (WINDOW - 1)
    delta = key - qry
    in_band = (delta >= 0) & (delta <= WINDOW)
    grp = lax.broadcasted_iota(jnp.int32, (1, n), 1) >> _log2(WINDOW)
    sinks = []
    for h in range(ATT_KV_HEADS):
        row = jnp.full((1, n), sink_ref[h * ATT_GROUP], F32)
        for g in range(1, ATT_GROUP):
            row = jnp.where(grp == g, sink_ref[h * ATT_GROUP + g], row)
        sinks.append(row)
    yield

    probs = []
    for j in range(qb // WINDOW):
        valid = in_band & (key >= jnp.where(first_block, WINDOW, 0)) if j == 0 else in_band
        q = q_ref[j * WINDOW:(j + 1) * WINDOW, :] * (HEAD_DIM ** -0.5)
        band = slice(j * WINDOW, (j + 2) * WINDOW)
        for h in range(ATT_KV_HEADS):
            qh = _stack_query_heads(q, h)
            s = _dot_nt(kd[h][band], qh)
            probs.append(dict(h=h, band=band, s=jnp.where(valid, s, NEG_INF)))
            yield
    for p in probs:
        p["m"] = jnp.maximum(jnp.max(p["s"], axis=0, keepdims=True), sinks[p["h"]])
        yield
    for p in probs:
        p["p"] = jnp.exp(p["s"] - p["m"])
        yield
    for p in probs:
        h = p["h"]
        denom = jnp.sum(p["p"], axis=0, keepdims=True) + jnp.exp(sinks[h] - p["m"])
        p["o_t"] = _dot(v_t[h * HEAD_DIM:(h + 1) * HEAD_DIM, p["band"]], p["p"].astype(BF16)) / denom
        yield
    for j in range(qb // WINDOW):
        pieces = [p["o_t"][:, g * WINDOW:(g + 1) * WINDOW]
                  for p in probs[j * ATT_KV_HEADS:(j + 1) * ATT_KV_HEADS] for g in range(ATT_GROUP)]
        store(j, jnp.concatenate(pieces, axis=0).T)
        yield


def _swa_sample_kernel(sink_ref, q_ref, knt_ref, vnt_ref, ck_ref, cv_ref,
                       o_ref, kbuf_ref, vbuf_ref, *, nb, tq):
    past = ck_ref.shape[1]
    hw = ATT_KV_HEADS * HEAD_DIM
    heads = ATT_KV_HEADS * ATT_GROUP
    n = heads * tq
    fresh = past - tq
    t = lax.broadcasted_iota(jnp.int32, (n, 2 * past), 0) & (tq - 1)
    col = lax.broadcasted_iota(jnp.int32, (n, 2 * past), 1)
    valid = ((col < past) & (col >= t)) | ((col >= past + fresh) & (col - (past + fresh) <= t))
    sink = jnp.concatenate([_sink_column(sink_ref, h, tq) for h in range(ATT_KV_HEADS)], axis=0)
    lane = lax.broadcasted_iota(jnp.int32, (hw, past), 1)
    knt = knt_ref[...]
    vnt = vnt_ref[...]
    zero_slab = jnp.zeros((tq, LANES), F32)
    lo = _low_half((tq, LANES))

    def shifted(old, new_t, b):
        return jnp.where(lane >= fresh, pltpu.roll(new_t, (fresh - b * tq) % past, axis=1),
                         pltpu.roll(old, fresh, axis=1))

    def dup_rows(x):
        h0, h1 = x[:HEAD_DIM], x[HEAD_DIM:]
        return jnp.concatenate([h0, h0, h1, h1], axis=0)

    for b in range(nb):
        k_old = ck_ref[b * hw:(b + 1) * hw, :]
        v_old = cv_ref[b * hw:(b + 1) * hw, :]
        k_new = shifted(k_old, knt, b)
        v_new = shifted(v_old, vnt, b)
        kbuf_ref[b * hw:(b + 1) * hw, :] = k_new
        vbuf_ref[b * hw:(b + 1) * hw, :] = v_new
        kop = jnp.concatenate([dup_rows(k_old), dup_rows(k_new)], axis=1).astype(BF16)
        vop = jnp.concatenate([dup_rows(v_old), dup_rows(v_new)], axis=1).astype(BF16)

        q = q_ref[b * tq:(b + 1) * tq, :]
        blocks = []
        for hq in range(heads):
            slab = q[:, (hq // 2) * LANES:(hq // 2 + 1) * LANES]
            slab = jnp.where(lo if hq % 2 == 0 else jnp.logical_not(lo), slab, 0.0)
            pair = [slab, zero_slab] if hq // ATT_GROUP == 0 else [zero_slab, slab]
            blocks.append(jnp.concatenate(pair, axis=1))
        qx = jnp.concatenate(blocks, axis=0).astype(BF16)

        s = _dot(qx, kop) * (HEAD_DIM ** -0.5)
        s = jnp.where(valid, s, NEG_INF)
        m = jnp.maximum(jnp.max(s, axis=-1, keepdims=True), sink)
        p = jnp.exp(s - m)
        denom = jnp.sum(p, axis=-1, keepdims=True) + jnp.exp(sink - m)
        o = _dot_nt(p.astype(BF16), vop) / denom

        slabs = []
        for j in range(heads // 2):
            h = (2 * j) // ATT_GROUP
            even = o[(2 * j) * tq:(2 * j + 1) * tq, h * LANES:(h + 1) * LANES]
            odd = o[(2 * j + 1) * tq:(2 * j + 2) * tq, h * LANES:(h + 1) * LANES]
            slabs.append(jnp.where(lo, even, odd))
        o_ref[b * tq:(b + 1) * tq, :] = jnp.concatenate(slabs, axis=1)


def _swa_sample(q, knt, vnt, cache_kt, cache_vt, sinks, nbatch, nb):
    t = q.shape[0]
    tq = t // nbatch
    past = cache_kt.shape[1]
    hw = ATT_KV_HEADS * HEAD_DIM
    assert past == WINDOW and tq & (tq - 1) == 0 and nb * tq == past

    def rows(r, width):
        return pl.BlockSpec((r, width), lambda i: (i, 0))

    new_t = pl.BlockSpec((hw, nb * tq), lambda i: (0, i))
    return pl.pallas_call(
        functools.partial(_swa_sample_kernel, nb=nb, tq=tq),
        grid=(nbatch // nb,),
        in_specs=[
            pl.BlockSpec(memory_space=pltpu.SMEM),
            rows(nb * tq, ATT_WIDTH), new_t, new_t, rows(nb * hw, past), rows(nb * hw, past),
        ],
        out_specs=[rows(nb * tq, ATT_WIDTH), rows(nb * hw, past), rows(nb * hw, past)],
        out_shape=[
            jax.ShapeDtypeStruct((t, ATT_WIDTH), F32),
            jax.ShapeDtypeStruct(cache_kt.shape, F32),
            jax.ShapeDtypeStruct(cache_vt.shape, F32),
        ],
        compiler_params=pltpu.CompilerParams(
            dimension_semantics=("arbitrary",), vmem_limit_bytes=VMEM_LIMIT),
    )(sinks, q, knt, vnt, cache_kt, cache_vt)


def _mlstm_intra(q, k, v, it_c, bt_c, it_r, bt_r, m_c, valid):
    d = jnp.where(valid, bt_c - bt_r + it_r, NEG_INF)
    inter = bt_c + m_c
    mt = jnp.maximum(jnp.max(d, axis=-1, keepdims=True), inter)
    sm = _dot_nt(q, k) * jnp.exp(d - mt)
    wi = jnp.exp(inter - mt)
    return _dot(sm.astype(BF16), v), jnp.sum(sm, axis=-1, keepdims=True), mt, wi


def _mlstm_head_out(num, den, mt, og, g_head):
    lower = jnp.maximum(jnp.abs(den), jnp.exp(-mt))
    h = num / lower
    hn = h * lax.rsqrt(jnp.mean(h * h, axis=-1, keepdims=True) + EPS)
    return hn * g_head * jax.nn.sigmoid(og)


def _mlstm_prompt_stages(mq, mk, mv, og, gr, gh_ref, hm_ref, c_ref, n_ref, m_ref, nbatch, chunk):
    s_idx = lax.broadcasted_iota(jnp.int32, (chunk, chunk), 0)
    l_idx = lax.broadcasted_iota(jnp.int32, (chunk, chunk), 1)
    causal = s_idx <= l_idx
    triu = jnp.where(causal, 1.0, 0.0).astype(BF16)
    lo = _low_half((chunk, LANES))
    pad_rows = jnp.zeros((LANES - MLSTM_HEADS, chunk), F32)
    sub8 = (GATE_ROWS, chunk)

    probs = []
    for b in range(nbatch):
        g = gr(b)
        btr = _cumsum_rows(g, triu)
        it_rows = g[:MLSTM_HEADS]
        bt_rows = btr[MLSTM_HEADS:2 * MLSTM_HEADS]
        u_cols = jnp.concatenate([it_rows - bt_rows, pad_rows], axis=0).T
        for h in range(MLSTM_HEADS):
            slab = slice((h // 2) * LANES, (h // 2 + 1) * LANES)
            keep = lo if h % 2 == 0 else jnp.logical_not(lo)
            q = mq(b, slab)
            k = mk(b, slab)
            probs.append(dict(
                b=b, h=h, idx=b * MLSTM_HEADS + h, cols=slice(h * MLSTM_DV, (h + 1) * MLSTM_DV),
                q=jnp.where(keep, q, jnp.zeros_like(q)), k=jnp.where(keep, k, jnp.zeros_like(k)),
                u_c=u_cols[:, h:h + 1], it_r=it_rows[h:h + 1, :], bt_r=bt_rows[h:h + 1, :]))
        yield

    for p in probs:
        p["m_old"] = m_ref[p["idx"]][:, 0:1]
        p["d"] = jnp.where(causal, p["bt_r"] + p["u_c"], NEG_INF)
        p["inter"] = p["bt_r"] + p["m_old"]
        p["q_t"] = p["q"].astype(F32).T.astype(BF16)
        p["kq"] = _dot(p["k"], p["q_t"])
        p["v_t"] = mv(p["b"], p["cols"]).astype(F32).T
        yield
    for p in probs:
        p["mt"] = jnp.maximum(jnp.max(p["d"], axis=0, keepdims=True), p["inter"])
        yield
    for p in probs:
        p["sm"] = p["kq"] * jnp.exp(p["d"] - p["mt"])
        p["wi"] = jnp.exp(p["inter"] - p["mt"])
        yield
    for p in probs:
        c_old = c_ref[p["idx"]]
        n_old = jnp.broadcast_to(n_ref[p["idx"]], (sub8[0], LANES))
        num = (_dot(p["v_t"].astype(BF16), p["sm"].astype(BF16))
               + p["wi"] * _dot(c_old.astype(BF16), p["q_t"]))
        den = (jnp.sum(p["sm"], axis=0, keepdims=True)
               + p["wi"] * _dot(n_old.astype(BF16), p["q_t"])[0:1])
        lower = jnp.maximum(jnp.abs(den), jnp.exp(-p["mt"]))
        h_t = num / lower
        p["hn"] = (h_t * lax.rsqrt(jnp.mean(h_t * h_t, axis=0, keepdims=True) + EPS)).T
        yield
    for p in probs:
        gate = jax.nn.sigmoid(og(p["b"], p["cols"]).astype(F32))
        hm_ref[p["b"], :, p["cols"]] = (p["hn"] * gh_ref[:, p["cols"]] * gate).astype(hm_ref.dtype)
        yield
    for p in probs:
        idx = p["idx"]
        b_last = p["bt_r"][:, chunk - 1:chunk]
        m_new = p["mt"][:, chunk - 1:chunk]
        wk = jnp.exp(b_last - p["bt_r"] + p["it_r"] - m_new)
        wc = jnp.exp(b_last + p["m_old"] - m_new)
        c_ref[idx] = wc * c_ref[idx] + _dot((p["v_t"] * wk).astype(BF16), p["k"])
        n_ref[idx] = wc * n_ref[idx] + _dot(jnp.broadcast_to(wk, sub8).astype(BF16), p["k"])[0:1]
        m_ref[idx] = jnp.broadcast_to(m_new, (1, LANES))
        yield


def _mixer_prompt_kernel(x_ref, g_ref, w_ref, wgr_ref, br_ref, gh_ref, gmem_ref, wck_ref, wcv_ref,
                         wg_ref, wu_ref, wd_ref, wo_ref, wcq_ref, wco_ref, mem_ref,
                         q_ref, k_ref, v_ref, hm_ref, c_ref, n_ref, m_ref, wgu_out, wd_out, wo_out,
                         wcq_out, wco_out, memk_out, memv_out, memkt_out, memvt_out, kwin_out, vwin_out,
                         mq_scr, mk_scr, mv_scr, og_scr, gr_scr, *, nbatch, chunk):
    s = pl.program_id(0)
    wr = s % 2
    rd = 1 - wr
    rows = nbatch * chunk

    def side_work():
        for c in range(0, wg_ref.shape[1], FFN_CHUNK):
            wgu_out[:, 2 * c:2 * c + FFN_CHUNK] = wg_ref[:, c:c + FFN_CHUNK].astype(BF16)
            wgu_out[:, 2 * c + FFN_CHUNK:2 * (c + FFN_CHUNK)] = wu_ref[:, c:c + FFN_CHUNK].astype(BF16)
        wd_out[...] = wd_ref[...].astype(BF16)
        wo_out[...] = wo_ref[...].astype(BF16)
        wcq_out[...] = wcq_ref[...].astype(BF16)
        wco_out[...] = wco_ref[...].astype(BF16)
        mn = _rmsnorm(mem_ref[...], gmem_ref[...]).astype(BF16)
        mk = _dot(mn, wck_ref[...])
        mv = _dot(mn, wcv_ref[...])
        memk_out[...] = mk
        memv_out[...] = mv
        memkt_out[...] = mk.T
        memvt_out[...] = mv.T

    def proj_stages():
        xn = _rmsnorm(x_ref[...].reshape(rows, x_ref.shape[-1]), g_ref[...]).astype(BF16)
        yield

        def mm(lo, hi):
            return _dot_nt(xn, w_ref[lo:hi, :])

        o = 0
        q_ref[...] = mm(o, o + ATT_WIDTH).astype(q_ref.dtype).reshape(q_ref.shape)
        yield
        o += ATT_WIDTH
        kv = mm(o, o + 2 * KV_WIDTH)
        k_ref[...] = kv[:, :KV_WIDTH].reshape(k_ref.shape)
        v_ref[...] = kv[:, KV_WIDTH:].reshape(v_ref.shape)
        yield
        o += 2 * KV_WIDTH
        mqk = mm(o, o + 2 * MQK_WIDTH)
        mq_scr[wr] = mqk[:, :MQK_WIDTH].astype(BF16)
        mk_scr[wr] = (mqk[:, MQK_WIDTH:] * (MLSTM_DK ** -0.5)).astype(BF16)
        yield
        o += 2 * MQK_WIDTH
        mv_scr[wr] = mm(o, o + MLSTM_WIDTH).astype(BF16)
        yield
        o += MLSTM_WIDTH
        og_scr[wr] = mm(o, o + MLSTM_WIDTH).astype(BF16)
        yield
        zr = _dot_nt(wgr_ref[...], xn) + br_ref[...]
        row = lax.broadcasted_iota(jnp.int32, zr.shape, 0)
        gr_scr[wr] = jnp.where(row < MLSTM_HEADS, zr, _log_sigmoid(zr))
        yield

    def mlstm_stages():
        def rows_of(scr):
            return lambda b, lanes: scr[rd, b * chunk:(b + 1) * chunk, lanes]
        return _mlstm_prompt_stages(
            rows_of(mq_scr), rows_of(mk_scr), rows_of(mv_scr), rows_of(og_scr),
            lambda b: gr_scr[rd, :, b * chunk:(b + 1) * chunk],
            gh_ref, hm_ref, c_ref, n_ref, m_ref, nbatch, chunk)

    @pl.when(s == 0)
    def _():
        c_ref[...] = jnp.zeros_like(c_ref)
        n_ref[...] = jnp.zeros_like(n_ref)
        m_ref[...] = jnp.zeros_like(m_ref)
        side_work()
        for _ in proj_stages():
            pass

    @pl.when(s > 0)
    def _():
        side_work()
        mlstm = mlstm_stages()
        for _ in proj_stages():
            for _ in range(MLSTM_STAGES_PER_PROJ_STAGE):
                next(mlstm, None)
        for _ in mlstm:
            pass

    @pl.when(s == pl.num_programs(0) - 2)
    def _():
        for b in range(nbatch):
            kwin_out[b] = k_ref[b, chunk - WINDOW:, :].T
            vwin_out[b] = v_ref[b, chunk - WINDOW:, :].T


def _mixer_prompt(x, mem, w, w_gate, w_up, w_down, w_out, w_cq, w_co, nbatch, chunk):
    t, d = x.shape
    seq = t // nbatch
    nchunks = seq // chunk
    nstate = nbatch * MLSTM_HEADS
    rows = nbatch * chunk
    hidden = w_down.shape[0]
    assert hidden % FFN_CHUNK == 0
    assert d % (nchunks * BF16_SUBLANES) == 0 and hidden % (nchunks * BF16_SUBLANES) == 0
    assert w_co.shape[0] % (nchunks * BF16_SUBLANES) == 0
    mem_tokens = mem.shape[0] // nbatch
    blocks_per_batch = mem_tokens // LANES
    nblocks = nbatch * blocks_per_batch
    steps_per_block = nchunks // nblocks
    assert mem_tokens % LANES == 0 and nchunks % nblocks == 0 and chunk >= WINDOW

    def mem_block(s):
        return jnp.minimum(s // steps_per_block, nblocks - 1)

    mem_rows = pl.BlockSpec((LANES, d), lambda s: (mem_block(s), 0))
    mem_kv = pl.BlockSpec((LANES, CROSS_WIDTH), lambda s: (mem_block(s), 0))
    mem_kv_t = pl.BlockSpec((CROSS_WIDTH, LANES),
                            lambda s: (mem_block(s) // blocks_per_batch, mem_block(s) % blocks_per_batch))
    window_t = pl.BlockSpec((nbatch, KV_WIDTH, WINDOW), lambda s: (0, 0, 0))

    def proj_chunk(width):
        return pl.BlockSpec((nbatch, chunk, width), lambda s: (0, jnp.minimum(s, nchunks - 1), 0))

    def state(shape):
        return pl.BlockSpec(shape, lambda s: (0, 0, 0))

    def weight_rows(a, width=None):
        return pl.BlockSpec((a.shape[0] // nchunks, width or a.shape[1]),
                            lambda s: (jnp.minimum(s, nchunks - 1), 0))

    names = ["g_mix", "w_main", "w_gate_rows", "b_gate_rows", "g_head", "g_mem", "w_ck", "w_cv"]
    (q, k, v, hm, c_st, n_st, m_st, w_gate_up_bf, w_down_bf, w_out_bf, w_cq_bf, w_co_bf, memk, memv,
     memk_t, memv_t, kwin_t, vwin_t) = pl.pallas_call(
        functools.partial(_mixer_prompt_kernel, nbatch=nbatch, chunk=chunk),
        grid=(nchunks + 1,),
        in_specs=[proj_chunk(d)] + _weight_specs(w, names) + [
            weight_rows(w_gate), weight_rows(w_up), weight_rows(w_down), weight_rows(w_out),
            weight_rows(w_cq), weight_rows(w_co), mem_rows,
        ],
        out_specs=[
            proj_chunk(ATT_WIDTH), proj_chunk(KV_WIDTH), proj_chunk(KV_WIDTH),
            pl.BlockSpec((nbatch, chunk, MLSTM_WIDTH), lambda s: (0, jnp.maximum(s - 1, 0), 0)),
            state((nstate, MLSTM_DV, LANES)), state((nstate, 1, LANES)), state((nstate, 1, LANES)),
            weight_rows(w_gate, 2 * hidden), weight_rows(w_down), weight_rows(w_out),
            weight_rows(w_cq), weight_rows(w_co), mem_kv, mem_kv, mem_kv_t, mem_kv_t, window_t, window_t,
        ],
        out_shape=[
            jax.ShapeDtypeStruct((nbatch, seq, ATT_WIDTH), BF16),
            jax.ShapeDtypeStruct((nbatch, seq, KV_WIDTH), F32),
            jax.ShapeDtypeStruct((nbatch, seq, KV_WIDTH), F32),
            jax.ShapeDtypeStruct((nbatch, seq, MLSTM_WIDTH), BF16),
            jax.ShapeDtypeStruct((nstate, MLSTM_DV, LANES), F32),
            jax.ShapeDtypeStruct((nstate, 1, LANES), F32),
            jax.ShapeDtypeStruct((nstate, 1, LANES), F32),
            jax.ShapeDtypeStruct((d, 2 * hidden), BF16),
            jax.ShapeDtypeStruct((hidden, d), BF16),
            jax.ShapeDtypeStruct(w_out.shape, BF16),
            jax.ShapeDtypeStruct(w_cq.shape, BF16),
            jax.ShapeDtypeStruct(w_co.shape, BF16),
            jax.ShapeDtypeStruct((mem.shape[0], CROSS_WIDTH), F32),
            jax.ShapeDtypeStruct((mem.shape[0], CROSS_WIDTH), F32),
            jax.ShapeDtypeStruct((nbatch * CROSS_WIDTH, mem_tokens), F32),
            jax.ShapeDtypeStruct((nbatch * CROSS_WIDTH, mem_tokens), F32),
            jax.ShapeDtypeStruct((nbatch, KV_WIDTH, WINDOW), F32),
            jax.ShapeDtypeStruct((nbatch, KV_WIDTH, WINDOW), F32),
        ],
        scratch_shapes=[
            pltpu.VMEM((2, rows, MQK_WIDTH), BF16), pltpu.VMEM((2, rows, MQK_WIDTH), BF16),
            pltpu.VMEM((2, rows, MLSTM_WIDTH), BF16), pltpu.VMEM((2, rows, MLSTM_WIDTH), BF16),
            pltpu.VMEM((2, GATE_ROWS, rows), F32),
        ],
        compiler_params=pltpu.CompilerParams(
            dimension_semantics=("arbitrary",), vmem_limit_bytes=VMEM_LIMIT,
            allow_input_fusion=[False] + [True] * len(names) + [False] * 7),
    )(x.reshape(nbatch, seq, d), *[w[n] for n in names], w_gate, w_up, w_down, w_out, w_cq, w_co, mem)
    q = q.reshape(t, ATT_WIDTH)
    k = k.reshape(t, KV_WIDTH)
    v = v.reshape(t, KV_WIDTH)

    c_st = c_st.reshape(nbatch, MLSTM_HEADS, MLSTM_DV, 2, MLSTM_DK)
    n_st = n_st.reshape(nbatch, MLSTM_HEADS, 2, MLSTM_DK)
    c_fin = jnp.stack([c_st[:, h, :, h % 2, :] for h in range(MLSTM_HEADS)], axis=1)
    n_fin = jnp.stack([n_st[:, h, h % 2, :] for h in range(MLSTM_HEADS)], axis=1)
    m_fin = m_st[:, 0, 0].reshape(nbatch, MLSTM_HEADS)
    bf16_weights = {"w_gate_up": w_gate_up_bf, "w_down": w_down_bf, "w_out": w_out_bf,
                    "w_cq": w_cq_bf, "w_co": w_co_bf}
    return (q, k, v, hm.reshape(t, MLSTM_WIDTH), c_fin, n_fin, m_fin, bf16_weights, memk, memv,
            memk_t, memv_t, kwin_t, vwin_t)


def _mlstm_sample_kernel(mq_ref, mk_ref, mkt_ref, mv_ref, og_ref, gc_ref, gr_ref, mrow_ref, c_ref,
                         n_ref, gh_ref, hm_ref, co_ref, no_ref, mt_ref, *, nb, tq):
    rows = nb * tq
    shift = tq.bit_length() - 1
    r = lax.broadcasted_iota(jnp.int32, (rows, rows), 0)
    s = lax.broadcasted_iota(jnp.int32, (rows, rows), 1)
    same = (r >> shift) == (s >> shift)
    valid = same & (s <= r)
    tril = jnp.where(valid, 1.0, 0.0).astype(BF16)
    triu = jnp.where(same & (r <= s), 1.0, 0.0).astype(BF16)
    row_batch = lax.broadcasted_iota(jnp.int32, (rows, 1), 0) >> shift
    lane = lax.broadcasted_iota(jnp.int32, (rows, LANES), 1)

    gc = gc_ref[...]
    gr = gr_ref[...]
    btc = _cumsum_cols(tril, gc)
    btr = _cumsum_rows(gr, triu)
    m_rows = mrow_ref[...]
    mt_all = jnp.zeros((rows, LANES), F32)

    def last_of_batch(col):
        parts = [jnp.broadcast_to(col[(b + 1) * tq - 1:(b + 1) * tq], (tq, 1)) for b in range(nb)]
        return jnp.concatenate(parts, axis=0)

    dk = MLSTM_DK
    lo = _low_half((rows, LANES))
    own_q = row_batch == (lax.broadcasted_iota(jnp.int32, (rows, nb * dk), 1) >> _log2(dk))
    lane_batch = lax.broadcasted_iota(jnp.int32, (dk, rows), 1) >> shift

    def state_rows(b, h):
        return slice((b * MLSTM_HEADS + h) * dk, (b * MLSTM_HEADS + h + 1) * dk)

    for h in range(MLSTM_HEADS):
        slab = slice((h // 2) * LANES, (h // 2 + 1) * LANES)
        qs, ks = mq_ref[:, slab], mk_ref[:, slab]
        qsw, ksw = pltpu.roll(qs, dk, axis=1), pltpu.roll(ks, dk, axis=1)
        q2 = jnp.where(lo, qs, qsw) if h % 2 == 0 else jnp.where(lo, qsw, qs)
        k2 = jnp.where(lo, ks, ksw) if h % 2 == 0 else jnp.where(lo, ksw, ks)
        qb, kb = q2[:, :dk].astype(BF16), k2[:, :dk].astype(BF16)
        vb = mv_ref[:, h * MLSTM_DV:(h + 1) * MLSTM_DV].astype(BF16)
        it_c = gc[:, h:h + 1]
        bt_c = btc[:, MLSTM_HEADS + h:MLSTM_HEADS + h + 1]
        it_r = gr[h:h + 1, :]
        bt_r = btr[MLSTM_HEADS + h:MLSTM_HEADS + h + 1, :]
        m_c = m_rows[:, h:h + 1]

        num, ssum, mt, wi = _mlstm_intra(qb, kb, vb, it_c, bt_c, it_r, bt_r, m_c, valid)
        qx = jnp.where(own_q, jnp.concatenate([q2] * (nb // 2), axis=1), 0.0).astype(BF16)
        c_stack = jnp.concatenate([c_ref[state_rows(b, h), :] for b in range(nb)], axis=0)
        qc = _dot(qx, c_stack.astype(BF16))
        n_rows = jnp.concatenate(
            [jnp.broadcast_to(n_ref[b * MLSTM_HEADS + h], (tq, dk)) for b in range(nb)], axis=0)
        num = num + wi * qc
        den = ssum + wi * jnp.sum(qb.astype(F32) * n_rows, axis=-1, keepdims=True)
        og = og_ref[:, h * MLSTM_DV:(h + 1) * MLSTM_DV]
        g_head = gh_ref[:, h * MLSTM_DV:(h + 1) * MLSTM_DV]
        hm_ref[:, h * MLSTM_DV:(h + 1) * MLSTM_DV] = _mlstm_head_out(num, den, mt, og, g_head)

        b_last = last_of_batch(bt_c)
        m_new = last_of_batch(mt)
        wk = jnp.exp(b_last - bt_c + it_c - m_new)
        wc = jnp.exp(b_last + m_c - m_new)
        wkv = (wk * vb.astype(F32)).astype(BF16)
        wkk = wk * kb.astype(F32)
        kt = mkt_ref[h * dk:(h + 1) * dk, :]
        kx = jnp.concatenate([jnp.where(lane_batch == b, kt, 0.0) for b in range(nb)], axis=0)
        upd = _dot(kx.astype(BF16), wkv)
        for b in range(nb):
            idx = b * MLSTM_HEADS + h
            wc_b = wc[b * tq:b * tq + 1]
            co_ref[state_rows(b, h), :] = wc_b * c_ref[state_rows(b, h), :] + upd[b * dk:(b + 1) * dk]
            no_ref[idx] = wc_b * n_ref[idx] + jnp.sum(wkk[b * tq:(b + 1) * tq], axis=0, keepdims=True)
        mt_all = jnp.where(lane == h, mt, mt_all)
    mt_ref[...] = mt_all


def _mlstm_sample(mq, mk, mkt, mv, og, gcol, grow, m_rows, ct_in, n_in, g_head, nbatch, nb):
    t = mq.shape[0]
    tq = t // nbatch
    rows = nb * tq
    assert tq & (tq - 1) == 0 and rows == LANES and nb % 2 == 0

    def tok(width):
        return pl.BlockSpec((rows, width), lambda i: (i, 0))

    ct_spec = pl.BlockSpec((nb * MLSTM_HEADS * MLSTM_DK, MLSTM_DV), lambda i: (i, 0))
    n_spec = pl.BlockSpec((nb * MLSTM_HEADS, 1, MLSTM_DK), lambda i: (i, 0, 0))
    return pl.pallas_call(
        functools.partial(_mlstm_sample_kernel, nb=nb, tq=tq),
        grid=(nbatch // nb,),
        in_specs=[
            tok(MQK_WIDTH), tok(MQK_WIDTH),
            pl.BlockSpec((MQK_WIDTH, rows), lambda i: (0, i)),
            tok(MLSTM_WIDTH), tok(MLSTM_WIDTH), tok(LANES),
            pl.BlockSpec((GATE_ROWS, rows), lambda i: (0, i)),
            tok(LANES),
            ct_spec, n_spec,
            _const_spec((1, MLSTM_WIDTH)),
        ],
        out_specs=[tok(MLSTM_WIDTH), ct_spec, n_spec, tok(LANES)],
        out_shape=[
            jax.ShapeDtypeStruct((t, MLSTM_WIDTH), F32),
            jax.ShapeDtypeStruct(ct_in.shape, F32),
            jax.ShapeDtypeStruct(n_in.shape, F32),
            jax.ShapeDtypeStruct((t, LANES), F32),
        ],
        compiler_params=pltpu.CompilerParams(
            dimension_semantics=("arbitrary",), vmem_limit_bytes=VMEM_LIMIT,
            allow_input_fusion=[False] * 7 + [True, False, True, False]),
    )(mq, mk, mkt, mv, og, gcol, grow, m_rows, ct_in, n_in, g_head)


def _mix_stage(x, att, hm, wo_ref, gcr_ref, wcq_ref):
    cat = jnp.concatenate([att.astype(BF16), hm.astype(BF16)], axis=1)
    y1 = x + _dot(cat, wo_ref[...])
    qc = _dot(_rmsnorm(y1, gcr_ref[...]).astype(BF16), wcq_ref[...])
    return y1, qc


def _cross_attention(qc, mk, mv):
    slabs = []
    for j in range(CROSS_WIDTH // LANES):
        qs = qc[:, j * LANES:(j + 1) * LANES]
        ks = mk[:, j * LANES:(j + 1) * LANES]
        vs = mv[:, j * LANES:(j + 1) * LANES]
        n = qs.shape[0]
        lo = _low_half(qs.shape)
        zero = jnp.zeros_like(qs)
        q2 = jnp.concatenate([jnp.where(lo, qs, zero), jnp.where(lo, zero, qs)], axis=0)
        s = _dot_nt(q2 * (HEAD_DIM ** -0.5), ks)
        p = jnp.exp(s - jnp.max(s, axis=-1, keepdims=True))
        o2 = _dot(p.astype(BF16), vs) / jnp.sum(p, axis=-1, keepdims=True)
        slabs.append(jnp.where(lo, o2[:n], o2[n:]))
    return jnp.concatenate(slabs, axis=1)


FFN_CHUNK = 256


def _finish(y3, gfin_ref, final):
    return _rmsnorm(y3, gfin_ref[...]) if final else y3


def _ffn_stage(y1, o, wco_ref, gf_ref, wgu_ref, wd_ref, act_ref, between=None):
    y2 = y1 + _dot(o.astype(BF16), wco_ref[...])
    xn = _rmsnorm(y2, gf_ref[...]).astype(BF16)
    for c in range(wd_ref.shape[0] // FFN_CHUNK):
        gu = _dot(xn, wgu_ref[:, 2 * c * FFN_CHUNK:2 * (c + 1) * FFN_CHUNK])
        g, u = gu[:, :FFN_CHUNK], gu[:, FFN_CHUNK:]
        act_ref[:, c * FFN_CHUNK:(c + 1) * FFN_CHUNK] = (g * jax.nn.sigmoid(g) * u).astype(BF16)
        if between is not None:
            between()
    return y2 + _dot(act_ref[...], wd_ref[...])


def _post_kernel(sink_ref, q_ref, kc_ref, kp_ref, vc_ref, vp_ref, x_ref, hm_ref, mk_ref, mv_ref,
                 xd_ref, attd_ref, hmd_ref, mkd_ref, mvd_ref,
                 wo_ref, gcr_ref, wcq_ref, wco_ref, gf_ref, wgu_ref, wd_ref, gfin_ref,
                 o_ref, od_ref, act_ref, att_ref, y3_ref, *, final, ntiles, tiles_per_seq, tq):
    tm = x_ref.shape[0]
    s = pl.program_id(0)
    tile = jnp.minimum(s, ntiles - 1)
    first_block = (tile % tiles_per_seq) == 0
    slot = s % 2

    def swa_stages():
        def store(j, value):
            att_ref[slot, j * WINDOW:(j + 1) * WINDOW, :] = value.astype(att_ref.dtype)
        return _swa_prompt_stages(sink_ref, q_ref, kc_ref, kp_ref, vc_ref, vp_ref, first_block, store)

    def finish():
        done = _finish(y3_ref[...], gfin_ref, final)
        o_ref[...] = done[:tm]
        od_ref[...] = done[tm:]

    @pl.when(s == 0)
    def _():
        y3_ref[...] = jnp.zeros_like(y3_ref)
        for _ in swa_stages():
            pass

    @pl.when(jnp.logical_and(s > 0, s <= ntiles))
    def _():
        swa = swa_stages()

        def advance(n=SWA_STAGES_PER_FFN_CHUNK):
            for _ in range(n):
                next(swa, None)

        finish()
        x = jnp.concatenate([x_ref[...], xd_ref[...]], axis=0)
        att = jnp.concatenate([att_ref[1 - slot], attd_ref[...].astype(BF16)], axis=0)
        hm = jnp.concatenate([hm_ref[...], hmd_ref[...].astype(BF16)], axis=0)
        y1, qc = _mix_stage(x, att, hm, wo_ref, gcr_ref, wcq_ref)
        advance()
        o = jnp.concatenate([
            _cross_attention(qc[:tm].astype(BF16), mk_ref[...].astype(BF16), mv_ref[...].astype(BF16)),
            _cross_attention_decode(qc[tm:], mkd_ref, mvd_ref, tq)], axis=0)
        advance()
        y3_ref[...] = _ffn_stage(y1, o, wco_ref, gf_ref, wgu_ref, wd_ref, act_ref, between=advance)
        for _ in swa:
            pass

    @pl.when(s > ntiles)
    def _():
        finish()


_MIX_WEIGHTS = ("w_out", "g_cross", "w_cq")
_FFN_WEIGHTS = ("w_co", "g_ffn", "w_gate_up", "w_down", "g_final")

def _post(x, q, k, v, hm, memk, memv, xd, attd, hmd, memkd_t, memvd_t, sinks, w, nbatch, nseq, final, tm):
    t, d = x.shape
    ntiles = t // tm
    tpb = ntiles // nbatch
    wpt = tm // WINDOW
    m = memk.shape[0] // nbatch
    hidden = w["w_down"].shape[0]
    td = xd.shape[0]
    tq = td // nseq
    slab = td // ntiles
    assert td % ntiles == 0 and slab % tq == 0 and slab % BF16_SUBLANES == 0
    mem_d = memkd_t.shape[1]

    def attn_tile(s):
        return jnp.minimum(s, ntiles - 1)

    def post_tile(s):
        return jnp.clip(s - 1, 0, ntiles - 1)

    def done_tile(s):
        return jnp.maximum(s - 2, 0)

    def attn_rows(width):
        return pl.BlockSpec((tm, width), lambda s: (attn_tile(s), 0))

    def prev_window(width):
        return pl.BlockSpec((WINDOW, width), lambda s: (jnp.maximum(attn_tile(s) * wpt - 1, 0), 0))

    def post_rows(width):
        return pl.BlockSpec((tm, width), lambda s: (post_tile(s), 0))

    def post_slab(width):
        return pl.BlockSpec((slab, width), lambda s: (post_tile(s), 0))

    mem_spec = pl.BlockSpec((m, CROSS_WIDTH), lambda s: (post_tile(s) // tpb, 0))
    memd_spec = pl.BlockSpec((slab // tq * CROSS_WIDTH, mem_d), lambda s: (post_tile(s), 0))
    names = _MIX_WEIGHTS + _FFN_WEIGHTS
    return pl.pallas_call(
        functools.partial(_post_kernel, final=final, ntiles=ntiles, tiles_per_seq=tpb, tq=tq),
        grid=(ntiles + 2,),
        in_specs=[
            pl.BlockSpec(memory_space=pltpu.SMEM),
            attn_rows(ATT_WIDTH), attn_rows(KV_WIDTH), prev_window(KV_WIDTH),
            attn_rows(KV_WIDTH), prev_window(KV_WIDTH),
            post_rows(d), post_rows(MLSTM_WIDTH), mem_spec, mem_spec,
            post_slab(d), post_slab(ATT_WIDTH), post_slab(MLSTM_WIDTH), memd_spec, memd_spec,
        ] + _weight_specs(w, names),
        out_specs=[pl.BlockSpec((tm, d), lambda s: (done_tile(s), 0)),
                   pl.BlockSpec((slab, d), lambda s: (done_tile(s), 0))],
        out_shape=[jax.ShapeDtypeStruct((t, d), F32), jax.ShapeDtypeStruct((td, d), F32)],
        scratch_shapes=[pltpu.VMEM((tm + slab, hidden), BF16), pltpu.VMEM((2, tm, ATT_WIDTH), BF16),
                        pltpu.VMEM((tm + slab, d), F32)],
        compiler_params=pltpu.CompilerParams(
            dimension_semantics=("arbitrary",), vmem_limit_bytes=VMEM_LIMIT),
    )(sinks, q, k, k, v, v, x, hm, memk, memv, xd, attd, hmd, memkd_t, memvd_t,
      *[w[n] for n in names])


def _cross_attention_decode(qc, mk_ref, mv_ref, tq):
    n = CROSS_HEADS * tq
    row_head = lax.broadcasted_iota(jnp.int32, (n, CROSS_WIDTH), 0) >> _log2(tq)
    lane_head = lax.broadcasted_iota(jnp.int32, (n, CROSS_WIDTH), 1) >> _log2(HEAD_DIM)
    own = row_head == lane_head
    outs = []
    for b in range(qc.shape[0] // tq):
        q = qc[b * tq:(b + 1) * tq, :]
        qx = jnp.where(own, jnp.concatenate([q] * CROSS_HEADS, axis=0), 0.0).astype(BF16)
        mk = mk_ref[b * CROSS_WIDTH:(b + 1) * CROSS_WIDTH, :].astype(BF16)
        mv = mv_ref[b * CROSS_WIDTH:(b + 1) * CROSS_WIDTH, :].astype(BF16)
        s = _dot(qx, mk) * (HEAD_DIM ** -0.5)
        p = jnp.exp(s - jnp.max(s, axis=-1, keepdims=True))
        pv = _dot_nt(p.astype(BF16), mv) / jnp.sum(p, axis=-1, keepdims=True)
        pv = jnp.where(own, pv, 0.0)
        out = pv[0:tq]
        for h in range(1, CROSS_HEADS):
            out = out + pv[h * tq:(h + 1) * tq]
        outs.append(out)
    return jnp.concatenate(outs, axis=0)


def _layer_weights(l, w_in, b_igate, b_fgate, g_mlstm_head, g_mix, g_cross, g_mem, w_ck, w_cv,
                   g_ffn, g_final):
    gate_w = w_in[l][:, MAIN_WIDTH:]
    gate_b = jnp.concatenate([b_igate[l], b_fgate[l]]).astype(F32)
    ngate = gate_b.shape[0]
    row = lambda a: a.astype(F32).reshape(1, -1)
    w_t = w_in[l].T
    k0 = ATT_WIDTH
    mk0 = ATT_WIDTH + 2 * KV_WIDTH + MQK_WIDTH
    return {
        "w_kvk_t": jnp.concatenate([w_t[k0:k0 + 2 * KV_WIDTH], w_t[mk0:mk0 + MQK_WIDTH]]).astype(BF16),
        "w_main": w_t[:MAIN_WIDTH].astype(BF16),
        "w_gate_cols": jnp.pad(gate_w, ((0, 0), (0, LANES - ngate))).astype(BF16),
        "w_gate_rows": jnp.pad(gate_w.T, ((0, GATE_ROWS - ngate), (0, 0))).astype(BF16),
        "b_gate_cols": jnp.pad(gate_b, (0, LANES - ngate)).reshape(1, LANES),
        "b_gate_rows": jnp.pad(gate_b, (0, GATE_ROWS - ngate)).reshape(GATE_ROWS, 1),
        "g_mix": row(g_mix[l]), "g_cross": row(g_cross[l]), "g_mem": row(g_mem[l]),
        "g_ffn": row(g_ffn[l]), "g_final": row(g_final), "g_head": row(g_mlstm_head[l]),
        "w_ck": w_ck[l].astype(BF16), "w_cv": w_cv[l].astype(BF16),
    }


TOKEN_TILE = 512
MLSTM_STAGES_PER_PROJ_STAGE = 15
SWA_STAGES_PER_FFN_CHUNK = 3
MLSTM_CHUNK_ROWS = 256
SWA_SAMPLE_BATCH = 16
MLSTM_SAMPLE_BATCH = 16


def kernel(x_prompt, x_sample, mem_prompt, cache_swa_k, cache_swa_v, state_mlstm_C, state_mlstm_n,
           state_mlstm_m, cache_mem_k, cache_mem_v, w_in, b_igate, b_fgate, attn_sinks,
           g_mlstm_head, w_out, g_mix, g_cross, g_mem, w_cq, w_ck, w_cv, w_co, g_ffn, w_gate,
           w_up, w_down, g_final):
    depth = w_in.shape[0]
    bp, sp, d = x_prompt.shape
    bs, ss, _ = x_sample.shape
    mem_tokens = mem_prompt.shape[1]
    past = cache_swa_k.shape[2]
    yp = x_prompt.reshape(bp * sp, d)
    ys = x_sample.reshape(bs * ss, d)
    mem = mem_prompt.reshape(bp * mem_tokens, d)
    outs = [[] for _ in range(12)]

    for l in range(depth):
        final = l == depth - 1
        w = _layer_weights(l, w_in, b_igate, b_fgate, g_mlstm_head, g_mix, g_cross, g_mem,
                           w_ck, w_cv, g_ffn, g_final)
        sinks = attn_sinks[l].astype(F32)

        q, k, v, hm, c_p, n_p, m_p, cast, memk, memv, memk_t, memv_t, kwin_t, vwin_t = _mixer_prompt(
            yp, mem, w, w_gate[l].astype(F32), w_up[l].astype(F32), w_down[l].astype(F32),
            w_out[l].astype(F32), w_cq[l].astype(F32), w_co[l].astype(F32), bp, MLSTM_CHUNK_ROWS)
        w.update(cast)

        def tokens_second(a, heads, tokens):
            return jnp.transpose(a.reshape(bp, heads, HEAD_DIM, tokens), (0, 3, 1, 2))

        outs[0].append(tokens_second(kwin_t, ATT_KV_HEADS, WINDOW))
        outs[1].append(tokens_second(vwin_t, ATT_KV_HEADS, WINDOW))
        outs[2].append(c_p)
        outs[3].append(n_p)
        outs[4].append(m_p)
        outs[5].append(tokens_second(memk_t, CROSS_HEADS, mem_tokens))
        outs[6].append(tokens_second(memv_t, CROSS_HEADS, mem_tokens))

        def tokens_last(a):
            return jnp.transpose(a.astype(F32), (0, 2, 3, 1)).reshape(-1, a.shape[1])

        qd, mq, mk, mv, og, grow, gcol, knt, vnt, mkt = _decode_proj(ys, w, TOKEN_TILE)
        att_d, kbuf_t, vbuf_t = _swa_sample(
            qd, knt, vnt, tokens_last(cache_swa_k[l]), tokens_last(cache_swa_v[l]), sinks, bs,
            SWA_SAMPLE_BATCH)
        m_rows = jnp.pad(jnp.repeat(state_mlstm_m[l].astype(F32), ss, axis=0),
                         ((0, 0), (0, LANES - MLSTM_HEADS)))
        ct_in = jnp.swapaxes(state_mlstm_C[l].astype(F32), 2, 3).reshape(-1, MLSTM_DV)
        hm_d, ct_s, n_s, mt = _mlstm_sample(
            mq, mk, mkt, mv, og, gcol, grow, m_rows, ct_in,
            state_mlstm_n[l].astype(F32).reshape(bs * MLSTM_HEADS, 1, MLSTM_DK),
            w["g_head"], bs, MLSTM_SAMPLE_BATCH)

        yp, ys = _post(yp, q, k, v, hm, memk, memv, ys, att_d, hm_d, tokens_last(cache_mem_k[l]),
                       tokens_last(cache_mem_v[l]), sinks, w, bp, bs, final, TOKEN_TILE)
        buf_t_shape = (bs, ATT_KV_HEADS, HEAD_DIM, past)
        outs[7].append(jnp.transpose(kbuf_t.reshape(buf_t_shape), (0, 3, 1, 2)))
        outs[8].append(jnp.transpose(vbuf_t.reshape(buf_t_shape), (0, 3, 1, 2)))
        outs[9].append(jnp.swapaxes(ct_s.reshape(bs, MLSTM_HEADS, MLSTM_DK, MLSTM_DV), 2, 3))
        outs[10].append(n_s.reshape(bs, MLSTM_HEADS, MLSTM_DK))
        outs[11].append(mt.reshape(bs, ss, LANES)[:, ss - 1, :MLSTM_HEADS])

    return (yp.reshape(bp, sp, d), ys.reshape(bs, ss, d)) + tuple(jnp.stack(o) for o in outs)
```

```python
import functools

import jax
import jax.numpy as jnp
from jax import lax
from jax.experimental import pallas as pl
from jax.experimental.pallas import tpu as pltpu

F32 = jnp.float32
BF16 = jnp.bfloat16

HEAD_DIM = 64
ATT_GROUP = 4
ATT_KV_HEADS = 2
ATT_WIDTH = 512
KV_WIDTH = ATT_KV_HEADS * HEAD_DIM
WINDOW = 128
MLSTM_HEADS = 4
MLSTM_DV = 128
MLSTM_DK = 64
MQK_WIDTH = MLSTM_HEADS * MLSTM_DK
MLSTM_WIDTH = MLSTM_HEADS * MLSTM_DV
MAIN_WIDTH = ATT_WIDTH + 2 * KV_WIDTH + 2 * MQK_WIDTH + 2 * MLSTM_WIDTH
CROSS_HEADS = 4
CROSS_WIDTH = CROSS_HEADS * HEAD_DIM
EPS = 1e-6
NEG_INF = float("-inf")
assert HEAD_DIM == 4 ** 3

LANES = 128
BF16_SUBLANES = 16
GATE_ROWS = BF16_SUBLANES
VMEM_LIMIT = 56 * 1024 * 1024

NT = (((1,), (1,)), ((), ()))


def _dot(a, b):
    return jnp.dot(a, b, preferred_element_type=F32)


def _dot_nt(a, b):
    return lax.dot_general(a, b, NT, preferred_element_type=F32)


def _rmsnorm(x, g):
    return x * lax.rsqrt(jnp.mean(x * x, axis=-1, keepdims=True) + EPS) * g


def _log_sigmoid(x):
    return jnp.minimum(x, 0.0) - jnp.log1p(jnp.exp(-jnp.abs(x)))


def _split3(x):
    hi = x.astype(BF16)
    r1 = x - hi.astype(F32)
    mid = r1.astype(BF16)
    lo = (r1 - mid.astype(F32)).astype(BF16)
    return hi, mid, lo


def _cumsum_cols(tri, x):
    hi, mid, lo = _split3(x)
    return _dot(tri, hi) + _dot(tri, mid) + _dot(tri, lo)


def _cumsum_rows(x, tri):
    hi, mid, lo = _split3(x)
    return _dot(hi, tri) + _dot(mid, tri) + _dot(lo, tri)


def _log2(n):
    assert n > 0 and n & (n - 1) == 0, n
    return n.bit_length() - 1


def _low_half(shape):
    return lax.broadcasted_iota(jnp.int32, shape, 1) < HEAD_DIM


def _decode_proj_kernel(x_ref, g_ref, w_ref, wgr_ref, br_ref, wgc_ref, bc_ref, wt_ref,
                        q_ref, mq_ref, mk_ref, mv_ref, og_ref, gr_ref, gc_ref, kt_ref, vt_ref, mkt_ref):
    xn = _rmsnorm(x_ref[...], g_ref[...]).astype(BF16)
    zt = _dot_nt(wt_ref[...], xn)
    kt_ref[...] = zt[:KV_WIDTH]
    vt_ref[...] = zt[KV_WIDTH:2 * KV_WIDTH]
    mkt_ref[...] = zt[2 * KV_WIDTH:] * (MLSTM_DK ** -0.5)

    def mm(lo, hi):
        return _dot_nt(xn, w_ref[lo:hi, :])

    o = 0
    q_ref[...] = mm(o, o + ATT_WIDTH)
    o += ATT_WIDTH + 2 * KV_WIDTH
    mqk = mm(o, o + 2 * MQK_WIDTH)
    mq_ref[...] = mqk[:, :MQK_WIDTH]
    mk_ref[...] = mqk[:, MQK_WIDTH:] * (MLSTM_DK ** -0.5)
    o += 2 * MQK_WIDTH
    mv_ref[...] = mm(o, o + MLSTM_WIDTH)
    o += MLSTM_WIDTH
    og_ref[...] = mm(o, o + MLSTM_WIDTH)

    zc = _dot(xn, wgc_ref[...]) + bc_ref[...]
    lane = lax.broadcasted_iota(jnp.int32, zc.shape, 1)
    gc_ref[...] = jnp.where(lane < MLSTM_HEADS, zc, _log_sigmoid(zc))
    zr = _dot_nt(wgr_ref[...], xn) + br_ref[...]
    row = lax.broadcasted_iota(jnp.int32, zr.shape, 0)
    gr_ref[...] = jnp.where(row < MLSTM_HEADS, zr, _log_sigmoid(zr))


def _const_spec(shape):
    nd = len(shape)
    return pl.BlockSpec(shape, lambda *_: (0,) * nd, pipeline_mode=pl.Buffered(1))


def _weight_specs(w, names):
    return [_const_spec(w[n].shape) for n in names]


def _decode_proj(x, w, tm):
    t, d = x.shape

    def rows(width):
        return pl.BlockSpec((tm, width), lambda i: (i, 0))

    def lanes(height):
        return pl.BlockSpec((height, tm), lambda i: (0, i))

    outs = [(rows, ATT_WIDTH), (rows, MQK_WIDTH), (rows, MQK_WIDTH), (rows, MLSTM_WIDTH),
            (rows, MLSTM_WIDTH), (lanes, GATE_ROWS), (rows, LANES), (lanes, KV_WIDTH),
            (lanes, KV_WIDTH), (lanes, MQK_WIDTH)]
    names = ["g_mix", "w_main", "w_gate_rows", "b_gate_rows", "w_gate_cols", "b_gate_cols", "w_kvk_t"]
    return pl.pallas_call(
        _decode_proj_kernel,
        grid=(t // tm,),
        in_specs=[rows(d)] + _weight_specs(w, names),
        out_specs=[kind(size) for kind, size in outs],
        out_shape=[jax.ShapeDtypeStruct((t, size) if kind is rows else (size, t), F32)
                   for kind, size in outs],
        compiler_params=pltpu.CompilerParams(
            dimension_semantics=("arbitrary",), vmem_limit_bytes=VMEM_LIMIT,
            allow_input_fusion=[False] + [True] * len(names)),
    )(x, *[w[n] for n in names])


def _dup_heads(x):
    swapped = pltpu.roll(x, HEAD_DIM, axis=1)
    lo = _low_half(x.shape)
    return (jnp.where(lo, x, swapped).astype(BF16), jnp.where(lo, swapped, x).astype(BF16))


def _stack_query_heads(q, h):
    parts = []
    for g in range(ATT_GROUP):
        hq = h * ATT_GROUP + g
        slab = q[:, (hq // 2) * LANES:(hq // 2 + 1) * LANES]
        lo = _low_half(slab.shape)
        keep = lo if hq % 2 == 0 else jnp.logical_not(lo)
        parts.append(jnp.where(keep, slab, jnp.zeros_like(slab)))
    return jnp.concatenate(parts, axis=0).astype(BF16)


def _sink_column(sink_ref, h, rows_per_head):
    n = ATT_GROUP * rows_per_head
    grp = lax.broadcasted_iota(jnp.int32, (n, 1), 0) >> _log2(rows_per_head)
    col = jnp.full((n, 1), sink_ref[h * ATT_GROUP], F32)
    for g in range(1, ATT_GROUP):
        col = jnp.where(grp == g, sink_ref[h * ATT_GROUP + g], col)
    return col


def _swa_prompt_stages(sink_ref, q_ref, kc_ref, kp_ref, vc_ref, vp_ref, first_block, store):
    qb = q_ref.shape[0]
    k_all = jnp.concatenate([kp_ref[...], kc_ref[...]], axis=0)
    v_all = jnp.concatenate([vp_ref[...], vc_ref[...]], axis=0)
    kd = _dup_heads(k_all)
    v_t = v_all.T.astype(BF16)

    n = ATT_GROUP * WINDOW
    key = lax.broadcasted_iota(jnp.int32, (2 * WINDOW, n), 0)
    qry = lax.broadcasted_iota(jnp.int32, (2 * WINDOW, n), 1) & (WINDOW - 1)
    delta = key - qry
    in_band = (delta >= 0) & (delta <= WINDOW)
    grp = lax.broadcasted_iota(jnp.int32, (1, n), 1) >> _log2(WINDOW)
    sinks = []
    for h in range(ATT_KV_HEADS):
        row = jnp.full((1, n), sink_ref[h * ATT_GROUP], F32)
        for g in range(1, ATT_GROUP):
            row = jnp.where(grp == g, sink_ref[h * ATT_GROUP + g], row)
        sinks.append(row)
    yield

    probs = []
    for j in range(qb // WINDOW):
        valid = in_band & (key >= jnp.where(first_block, WINDOW, 0)) if j == 0 else in_band
        q = q_ref[j * WINDOW:(j + 1) * WINDOW, :] * (HEAD_DIM ** -0.5)
        band = slice(j * WINDOW, (j + 2) * WINDOW)
        for h in range(ATT_KV_HEADS):
            qh = _stack_query_heads(q, h)
            s = _dot_nt(kd[h][band], qh)
            probs.append(dict(h=h, band=band, s=jnp.where(valid, s, NEG_INF)))
            yield
    for p in probs:
        p["m"] = jnp.maximum(jnp.max(p["s"], axis=0, keepdims=True), sinks[p["h"]])
        yield
    for p in probs:
        p["p"] = jnp.exp(p["s"] - p["m"])
        yield
    for p in probs:
        h = p["h"]
        denom = jnp.sum(p["p"], axis=0, keepdims=True) + jnp.exp(sinks[h] - p["m"])
        p["o_t"] = _dot(v_t[h * HEAD_DIM:(h + 1) * HEAD_DIM, p["band"]], p["p"].astype(BF16)) / denom
        yield
    for j in range(qb // WINDOW):
        pieces = [p["o_t"][:, g * WINDOW:(g + 1) * WINDOW]
                  for p in probs[j * ATT_KV_HEADS:(j + 1) * ATT_KV_HEADS] for g in range(ATT_GROUP)]
        store(j, jnp.concatenate(pieces, axis=0).T)
        yield


def _swa_sample_kernel(sink_ref, q_ref, knt_ref, vnt_ref, ck_ref, cv_ref,
                       o_ref, kbuf_ref, vbuf_ref, *, nb, tq):
    past = ck_ref.shape[1]
    hw = ATT_KV_HEADS * HEAD_DIM
    heads = ATT_KV_HEADS * ATT_GROUP
    n = heads * tq
    fresh = past - tq
    t = lax.broadcasted_iota(jnp.int32, (n, 2 * past), 0) & (tq - 1)
    col = lax.broadcasted_iota(jnp.int32, (n, 2 * past), 1)
    valid = ((col < past) & (col >= t)) | ((col >= past + fresh) & (col - (past + fresh) <= t))
    sink = jnp.concatenate([_sink_column(sink_ref, h, tq) for h in range(ATT_KV_HEADS)], axis=0)
    lane = lax.broadcasted_iota(jnp.int32, (hw, past), 1)
    knt = knt_ref[...]
    vnt = vnt_ref[...]
    zero_slab = jnp.zeros((tq, LANES), F32)
    lo = _low_half((tq, LANES))

    def shifted(old, new_t, b):
        return jnp.where(lane >= fresh, pltpu.roll(new_t, (fresh - b * tq) % past, axis=1),
                         pltpu.roll(old, fresh, axis=1))

    def dup_rows(x):
        h0, h1 = x[:HEAD_DIM], x[HEAD_DIM:]
        return jnp.concatenate([h0, h0, h1, h1], axis=0)

    for b in range(nb):
        k_old = ck_ref[b * hw:(b + 1) * hw, :]
        v_old = cv_ref[b * hw:(b + 1) * hw, :]
        k_new = shifted(k_old, knt, b)
        v_new = shifted(v_old, vnt, b)
        kbuf_ref[b * hw:(b + 1) * hw, :] = k_new
        vbuf_ref[b * hw:(b + 1) * hw, :] = v_new
        kop = jnp.concatenate([dup_rows(k_old), dup_rows(k_new)], axis=1).astype(BF16)
        vop = jnp.concatenate([dup_rows(v_old), dup_rows(v_new)], axis=1).astype(BF16)

        q = q_ref[b * tq:(b + 1) * tq, :]
        blocks = []
        for hq in range(heads):
            slab = q[:, (hq // 2) * LANES:(hq // 2 + 1) * LANES]
            slab = jnp.where(lo if hq % 2 == 0 else jnp.logical_not(lo), slab, 0.0)
            pair = [slab, zero_slab] if hq // ATT_GROUP == 0 else [zero_slab, slab]
            blocks.append(jnp.concatenate(pair, axis=1))
        qx = jnp.concatenate(blocks, axis=0).astype(BF16)

        s = _dot(qx, kop) * (HEAD_DIM ** -0.5)
        s = jnp.where(valid, s, NEG_INF)
        m = jnp.maximum(jnp.max(s, axis=-1, keepdims=True), sink)
        p = jnp.exp(s - m)
        denom = jnp.sum(p, axis=-1, keepdims=True) + jnp.exp(sink - m)
        o = _dot_nt(p.astype(BF16), vop) / denom

        slabs = []
        for j in range(heads // 2):
            h = (2 * j) // ATT_GROUP
            even = o[(2 * j) * tq:(2 * j + 1) * tq, h * LANES:(h + 1) * LANES]
            odd = o[(2 * j + 1) * tq:(2 * j + 2) * tq, h * LANES:(h + 1) * LANES]
            slabs.append(jnp.where(lo, even, odd))
        o_ref[b * tq:(b + 1) * tq, :] = jnp.concatenate(slabs, axis=1)


def _swa_sample(q, knt, vnt, cache_kt, cache_vt, sinks, nbatch, nb):
    t = q.shape[0]
    tq = t // nbatch
    past = cache_kt.shape[1]
    hw = ATT_KV_HEADS * HEAD_DIM
    assert past == WINDOW and tq & (tq - 1) == 0 and nb * tq == past

    def rows(r, width):
        return pl.BlockSpec((r, width), lambda i: (i, 0))

    new_t = pl.BlockSpec((hw, nb * tq), lambda i: (0, i))
    return pl.pallas_call(
        functools.partial(_swa_sample_kernel, nb=nb, tq=tq),
        grid=(nbatch // nb,),
        in_specs=[
            pl.BlockSpec(memory_space=pltpu.SMEM),
            rows(nb * tq, ATT_WIDTH), new_t, new_t, rows(nb * hw, past), rows(nb * hw, past),
        ],
        out_specs=[rows(nb * tq, ATT_WIDTH), rows(nb * hw, past), rows(nb * hw, past)],
        out_shape=[
            jax.ShapeDtypeStruct((t, ATT_WIDTH), F32),
            jax.ShapeDtypeStruct(cache_kt.shape, F32),
            jax.ShapeDtypeStruct(cache_vt.shape, F32),
        ],
        compiler_params=pltpu.CompilerParams(
            dimension_semantics=("arbitrary",), vmem_limit_bytes=VMEM_LIMIT),
    )(sinks, q, knt, vnt, cache_kt, cache_vt)


def _mlstm_intra(q, k, v, it_c, bt_c, it_r, bt_r, m_c, valid):
    d = jnp.where(valid, bt_c - bt_r + it_r, NEG_INF)
    inter = bt_c + m_c
    mt = jnp.maximum(jnp.max(d, axis=-1, keepdims=True), inter)
    sm = _dot_nt(q, k) * jnp.exp(d - mt)
    wi = jnp.exp(inter - mt)
    return _dot(sm.astype(BF16), v), jnp.sum(sm, axis=-1, keepdims=True), mt, wi


def _mlstm_head_out(num, den, mt, og, g_head):
    lower = jnp.maximum(jnp.abs(den), jnp.exp(-mt))
    h = num / lower
    hn = h * lax.rsqrt(jnp.mean(h * h, axis=-1, keepdims=True) + EPS)
    return hn * g_head * jax.nn.sigmoid(og)


def _mlstm_prompt_stages(mq, mk, mv, og, gr, gh_ref, hm_ref, c_ref, n_ref, m_ref, nbatch, chunk):
    s_idx = lax.broadcasted_iota(jnp.int32, (chunk, chunk), 0)
    l_idx = lax.broadcasted_iota(jnp.int32, (chunk, chunk), 1)
    causal = s_idx <= l_idx
    triu = jnp.where(causal, 1.0, 0.0).astype(BF16)
    lo = _low_half((chunk, LANES))
    pad_rows = jnp.zeros((LANES - MLSTM_HEADS, chunk), F32)
    sub8 = (GATE_ROWS, chunk)

    probs = []
    for b in range(nbatch):
        g = gr(b)
        btr = _cumsum_rows(g, triu)
        it_rows = g[:MLSTM_HEADS]
        bt_rows = btr[MLSTM_HEADS:2 * MLSTM_HEADS]
        u_cols = jnp.concatenate([it_rows - bt_rows, pad_rows], axis=0).T
        for h in range(MLSTM_HEADS):
            slab = slice((h // 2) * LANES, (h // 2 + 1) * LANES)
            keep = lo if h % 2 == 0 else jnp.logical_not(lo)
            q = mq(b, slab)
            k = mk(b, slab)
            probs.append(dict(
                b=b, h=h, idx=b * MLSTM_HEADS + h, cols=slice(h * MLSTM_DV, (h + 1) * MLSTM_DV),
                q=jnp.where(keep, q, jnp.zeros_like(q)), k=jnp.where(keep, k, jnp.zeros_like(k)),
                u_c=u_cols[:, h:h + 1], it_r=it_rows[h:h + 1, :], bt_r=bt_rows[h:h + 1, :]))
        yield

    for p in probs:
        p["m_old"] = m_ref[p["idx"]][:, 0:1]
        p["d"] = jnp.where(causal, p["bt_r"] + p["u_c"], NEG_INF)
        p["inter"] = p["bt_r"] + p["m_old"]
        p["q_t"] = p["q"].astype(F32).T.astype(BF16)
        p["kq"] = _dot(p["k"], p["q_t"])
        p["v_t"] = mv(p["b"], p["cols"]).astype(F32).T
        yield
    for p in probs:
        p["mt"] = jnp.maximum(jnp.max(p["d"], axis=0, keepdims=True), p["inter"])
        yield
    for p in probs:
        p["sm"] = p["kq"] * jnp.exp(p["d"] - p["mt"])
        p["wi"] = jnp.exp(p["inter"] - p["mt"])
        yield
    for p in probs:
        c_old = c_ref[p["idx"]]
        n_old = jnp.broadcast_to(n_ref[p["idx"]], (sub8[0], LANES))
        num = (_dot(p["v_t"].astype(BF16), p["sm"].astype(BF16))
               + p["wi"] * _dot(c_old.astype(BF16), p["q_t"]))
        den = (jnp.sum(p["sm"], axis=0, keepdims=True)
               + p["wi"] * _dot(n_old.astype(BF16), p["q_t"])[0:1])
        lower = jnp.maximum(jnp.abs(den), jnp.exp(-p["mt"]))
        h_t = num / lower
        p["hn"] = (h_t * lax.rsqrt(jnp.mean(h_t * h_t, axis=0, keepdims=True) + EPS)).T
        yield
    for p in probs:
        gate = jax.nn.sigmoid(og(p["b"], p["cols"]).astype(F32))
        hm_ref[p["b"], :, p["cols"]] = (p["hn"] * gh_ref[:, p["cols"]] * gate).astype(hm_ref.dtype)
        yield
    for p in probs:
        idx = p["idx"]
        b_last = p["bt_r"][:, chunk - 1:chunk]
        m_new = p["mt"][:, chunk - 1:chunk]
        wk = jnp.exp(b_last - p["bt_r"] + p["it_r"] - m_new)
        wc = jnp.exp(b_last + p["m_old"] - m_new)
        c_ref[idx] = wc * c_ref[idx] + _dot((p["v_t"] * wk).astype(BF16), p["k"])
        n_ref[idx] = wc * n_ref[idx] + _dot(jnp.broadcast_to(wk, sub8).astype(BF16), p["k"])[0:1]
        m_ref[idx] = jnp.broadcast_to(m_new, (1, LANES))
        yield


def _mixer_prompt_kernel(x_ref, g_ref, w_ref, wgr_ref, br_ref, gh_ref, gmem_ref, wck_ref, wcv_ref,
                         wg_ref, wu_ref, wd_ref, wo_ref, wcq_ref, wco_ref, mem_ref,
                         q_ref, k_ref, v_ref, hm_ref, ct_out, n_ref, m_ref, wgu_out, wd_out, wo_out,
                         wcq_out, wco_out, memk_out, memv_out, memkt_out, memvt_out, kwin_out, vwin_out,
                         mq_scr, mk_scr, mv_scr, og_scr, gr_scr, c_ref, wck_scr, wcv_scr,
                         *, nbatch, chunk):
    s = pl.program_id(0)
    wr = s % 2
    rd = 1 - wr
    rows = nbatch * chunk

    def side_work():
        for c in range(0, wg_ref.shape[1], FFN_CHUNK):
            wgu_out[:, 2 * c:2 * c + FFN_CHUNK] = wg_ref[:, c:c + FFN_CHUNK].astype(BF16)
            wgu_out[:, 2 * c + FFN_CHUNK:2 * (c + FFN_CHUNK)] = wu_ref[:, c:c + FFN_CHUNK].astype(BF16)
        wd_out[...] = wd_ref[...].astype(BF16)
        wo_out[...] = wo_ref[...].astype(BF16)
        wcq_out[...] = wcq_ref[...].astype(BF16)
        wco_out[...] = wco_ref[...].astype(BF16)
        mn = _rmsnorm(mem_ref[...], gmem_ref[...]).astype(BF16)
        mk = _dot(mn, wck_scr[...])
        mv = _dot(mn, wcv_scr[...])
        memk_out[...] = mk
        memv_out[...] = mv
        memkt_out[...] = mk.T
        memvt_out[...] = mv.T

    def proj_stages():
        xn = _rmsnorm(x_ref[...].reshape(rows, x_ref.shape[-1]), g_ref[...]).astype(BF16)
        yield

        def mm(lo, hi):
            return _dot_nt(xn, w_ref[lo:hi, :])

        o = 0
        q_ref[...] = mm(o, o + ATT_WIDTH).astype(q_ref.dtype).reshape(q_ref.shape)
        yield
        o += ATT_WIDTH
        kv = mm(o, o + 2 * KV_WIDTH)
        k_ref[...] = kv[:, :KV_WIDTH].reshape(k_ref.shape)
        v_ref[...] = kv[:, KV_WIDTH:].reshape(v_ref.shape)
        yield
        o += 2 * KV_WIDTH
        mqk = mm(o, o + 2 * MQK_WIDTH)
        mq_scr[wr] = mqk[:, :MQK_WIDTH].astype(BF16)
        mk_scr[wr] = (mqk[:, MQK_WIDTH:] * (MLSTM_DK ** -0.5)).astype(BF16)
        yield
        o += 2 * MQK_WIDTH
        mv_scr[wr] = mm(o, o + MLSTM_WIDTH).astype(BF16)
        yield
        o += MLSTM_WIDTH
        og_scr[wr] = mm(o, o + MLSTM_WIDTH).astype(BF16)
        yield
        zr = _dot_nt(wgr_ref[...], xn) + br_ref[...]
        row = lax.broadcasted_iota(jnp.int32, zr.shape, 0)
        gr_scr[wr] = jnp.where(row < MLSTM_HEADS, zr, _log_sigmoid(zr))
        yield

    def mlstm_stages():
        def rows_of(scr):
            return lambda b, lanes: scr[rd, b * chunk:(b + 1) * chunk, lanes]
        return _mlstm_prompt_stages(
            rows_of(mq_scr), rows_of(mk_scr), rows_of(mv_scr), rows_of(og_scr),
            lambda b: gr_scr[rd, :, b * chunk:(b + 1) * chunk],
            gh_ref, hm_ref, c_ref, n_ref, m_ref, nbatch, chunk)

    @pl.when(s == 0)
    def _():
        c_ref[...] = jnp.zeros_like(c_ref)
        n_ref[...] = jnp.zeros_like(n_ref)
        m_ref[...] = jnp.zeros_like(m_ref)
        wck_scr[...] = wck_ref[...].astype(BF16)
        wcv_scr[...] = wcv_ref[...].astype(BF16)
        side_work()
        for _ in proj_stages():
            pass

    @pl.when(s > 0)
    def _():
        side_work()
        mlstm = mlstm_stages()
        for _ in proj_stages():
            for _ in range(MLSTM_STAGES_PER_PROJ_STAGE):
                next(mlstm, None)
        for _ in mlstm:
            pass

    @pl.when(s == pl.num_programs(0) - 2)
    def _():
        for b in range(nbatch):
            kwin_out[b] = k_ref[b, chunk - WINDOW:, :].T
            vwin_out[b] = v_ref[b, chunk - WINDOW:, :].T

    @pl.when(s == pl.num_programs(0) - 1)
    def _():
        for i in range(c_ref.shape[0]):
            half = (i % MLSTM_HEADS) % 2
            ct_out[i] = c_ref[i].T[half * MLSTM_DK:(half + 1) * MLSTM_DK, :]


def _mixer_prompt(x, mem, w, w_gate, w_up, w_down, w_out, w_cq, w_co, nbatch, chunk):
    t, d = x.shape
    seq = t // nbatch
    nchunks = seq // chunk
    nstate = nbatch * MLSTM_HEADS
    rows = nbatch * chunk
    hidden = w_down.shape[0]
    assert hidden % FFN_CHUNK == 0
    assert d % (nchunks * BF16_SUBLANES) == 0 and hidden % (nchunks * BF16_SUBLANES) == 0
    assert w_co.shape[0] % (nchunks * BF16_SUBLANES) == 0
    mem_tokens = mem.shape[0] // nbatch
    blocks_per_batch = mem_tokens // LANES
    nblocks = nbatch * blocks_per_batch
    steps_per_block = nchunks // nblocks
    assert mem_tokens % LANES == 0 and nchunks % nblocks == 0 and chunk >= WINDOW

    def mem_block(s):
        return jnp.minimum(s // steps_per_block, nblocks - 1)

    mem_rows = pl.BlockSpec((LANES, d), lambda s: (mem_block(s), 0))
    mem_kv = pl.BlockSpec((LANES, CROSS_WIDTH), lambda s: (mem_block(s), 0))
    mem_kv_t = pl.BlockSpec((CROSS_WIDTH, LANES),
                            lambda s: (mem_block(s) // blocks_per_batch, mem_block(s) % blocks_per_batch))
    window_t = pl.BlockSpec((nbatch, KV_WIDTH, WINDOW), lambda s: (0, 0, 0))

    def proj_chunk(width):
        return pl.BlockSpec((nbatch, chunk, width), lambda s: (0, jnp.minimum(s, nchunks - 1), 0))

    def state(shape):
        return pl.BlockSpec(shape, lambda s: (0, 0, 0))

    def weight_rows(a, width=None):
        return pl.BlockSpec((a.shape[0] // nchunks, width or a.shape[1]),
                            lambda s: (jnp.minimum(s, nchunks - 1), 0))

    names = ["g_mix", "w_main", "w_gate_rows", "b_gate_rows", "g_head", "g_mem", "w_ck", "w_cv"]
    (q, k, v, hm, ct_st, n_st, m_st, w_gate_up_bf, w_down_bf, w_out_bf, w_cq_bf, w_co_bf, memk, memv,
     memk_t, memv_t, kwin_t, vwin_t) = pl.pallas_call(
        functools.partial(_mixer_prompt_kernel, nbatch=nbatch, chunk=chunk),
        grid=(nchunks + 1,),
        in_specs=[proj_chunk(d)] + _weight_specs(w, names) + [
            weight_rows(w_gate), weight_rows(w_up), weight_rows(w_down), weight_rows(w_out),
            weight_rows(w_cq), weight_rows(w_co), mem_rows,
        ],
        out_specs=[
            proj_chunk(ATT_WIDTH), proj_chunk(KV_WIDTH), proj_chunk(KV_WIDTH),
            pl.BlockSpec((nbatch, chunk, MLSTM_WIDTH), lambda s: (0, jnp.maximum(s - 1, 0), 0)),
            state((nstate, MLSTM_DK, MLSTM_DV)), state((nstate, 1, LANES)), state((nstate, 1, LANES)),
            weight_rows(w_gate, 2 * hidden), weight_rows(w_down), weight_rows(w_out),
            weight_rows(w_cq), weight_rows(w_co), mem_kv, mem_kv, mem_kv_t, mem_kv_t, window_t, window_t,
        ],
        out_shape=[
            jax.ShapeDtypeStruct((nbatch, seq, ATT_WIDTH), BF16),
            jax.ShapeDtypeStruct((nbatch, seq, KV_WIDTH), F32),
            jax.ShapeDtypeStruct((nbatch, seq, KV_WIDTH), F32),
            jax.ShapeDtypeStruct((nbatch, seq, MLSTM_WIDTH), BF16),
            jax.ShapeDtypeStruct((nstate, MLSTM_DK, MLSTM_DV), F32),
            jax.ShapeDtypeStruct((nstate, 1, LANES), F32),
            jax.ShapeDtypeStruct((nstate, 1, LANES), F32),
            jax.ShapeDtypeStruct((d, 2 * hidden), BF16),
            jax.ShapeDtypeStruct((hidden, d), BF16),
            jax.ShapeDtypeStruct(w_out.shape, BF16),
            jax.ShapeDtypeStruct(w_cq.shape, BF16),
            jax.ShapeDtypeStruct(w_co.shape, BF16),
            jax.ShapeDtypeStruct((mem.shape[0], CROSS_WIDTH), F32),
            jax.ShapeDtypeStruct((mem.shape[0], CROSS_WIDTH), F32),
            jax.ShapeDtypeStruct((nbatch * CROSS_WIDTH, mem_tokens), F32),
            jax.ShapeDtypeStruct((nbatch * CROSS_WIDTH, mem_tokens), F32),
            jax.ShapeDtypeStruct((nbatch, KV_WIDTH, WINDOW), F32),
            jax.ShapeDtypeStruct((nbatch, KV_WIDTH, WINDOW), F32),
        ],
        scratch_shapes=[
            pltpu.VMEM((2, rows, MQK_WIDTH), BF16), pltpu.VMEM((2, rows, MQK_WIDTH), BF16),
            pltpu.VMEM((2, rows, MLSTM_WIDTH), BF16), pltpu.VMEM((2, rows, MLSTM_WIDTH), BF16),
            pltpu.VMEM((2, GATE_ROWS, rows), F32),
            pltpu.VMEM((nstate, MLSTM_DV, LANES), F32),
            pltpu.VMEM(w["w_ck"].shape, BF16), pltpu.VMEM(w["w_cv"].shape, BF16),
        ],
        compiler_params=pltpu.CompilerParams(
            dimension_semantics=("arbitrary",), vmem_limit_bytes=VMEM_LIMIT,
            allow_input_fusion=[False] + [True] * len(names) + [False] * 7),
    )(x.reshape(nbatch, seq, d), *[w[n] for n in names], w_gate, w_up, w_down, w_out, w_cq, w_co, mem)
    q = q.reshape(t, ATT_WIDTH)
    k = k.reshape(t, KV_WIDTH)
    v = v.reshape(t, KV_WIDTH)

    c_fin = jnp.swapaxes(ct_st.reshape(nbatch, MLSTM_HEADS, MLSTM_DK, MLSTM_DV), 2, 3)
    n_st = n_st.reshape(nbatch, MLSTM_HEADS, 2, MLSTM_DK)
    n_fin = jnp.stack([n_st[:, h, h % 2, :] for h in range(MLSTM_HEADS)], axis=1)
    m_fin = m_st[:, 0, 0].reshape(nbatch, MLSTM_HEADS)
    bf16_weights = {"w_gate_up": w_gate_up_bf, "w_down": w_down_bf, "w_out": w_out_bf,
                    "w_cq": w_cq_bf, "w_co": w_co_bf}
    return (q, k, v, hm.reshape(t, MLSTM_WIDTH), c_fin, n_fin, m_fin, bf16_weights, memk, memv,
            memk_t, memv_t, kwin_t, vwin_t)


def _mlstm_sample_kernel(mq_ref, mk_ref, mkt_ref, mv_ref, og_ref, gc_ref, gr_ref, mrow_ref, c_ref,
                         n_ref, gh_ref, hm_ref, co_ref, no_ref, mt_ref, *, nb, tq):
    rows = nb * tq
    shift = tq.bit_length() - 1
    r = lax.broadcasted_iota(jnp.int32, (rows, rows), 0)
    s = lax.broadcasted_iota(jnp.int32, (rows, rows), 1)
    same = (r >> shift) == (s >> shift)
    valid = same & (s <= r)
    tril = jnp.where(valid, 1.0, 0.0).astype(BF16)
    triu = jnp.where(same & (r <= s), 1.0, 0.0).astype(BF16)
    row_batch = lax.broadcasted_iota(jnp.int32, (rows, 1), 0) >> shift
    lane = lax.broadcasted_iota(jnp.int32, (rows, LANES), 1)

    gc = gc_ref[...]
    gr = gr_ref[...]
    btc = _cumsum_cols(tril, gc)
    btr = _cumsum_rows(gr, triu)
    m_rows = mrow_ref[...]
    mt_all = jnp.zeros((rows, LANES), F32)

    def last_of_batch(col):
        parts = [jnp.broadcast_to(col[(b + 1) * tq - 1:(b + 1) * tq], (tq, 1)) for b in range(nb)]
        return jnp.concatenate(parts, axis=0)

    dk = MLSTM_DK
    lo = _low_half((rows, LANES))
    own_q = row_batch == (lax.broadcasted_iota(jnp.int32, (rows, nb * dk), 1) >> _log2(dk))
    lane_batch = lax.broadcasted_iota(jnp.int32, (dk, rows), 1) >> shift

    def state_rows(b, h):
        return slice((b * MLSTM_HEADS + h) * dk, (b * MLSTM_HEADS + h + 1) * dk)

    for h in range(MLSTM_HEADS):
        slab = slice((h // 2) * LANES, (h // 2 + 1) * LANES)
        qs, ks = mq_ref[:, slab], mk_ref[:, slab]
        qsw, ksw = pltpu.roll(qs, dk, axis=1), pltpu.roll(ks, dk, axis=1)
        q2 = jnp.where(lo, qs, qsw) if h % 2 == 0 else jnp.where(lo, qsw, qs)
        k2 = jnp.where(lo, ks, ksw) if h % 2 == 0 else jnp.where(lo, ksw, ks)
        qb, kb = q2[:, :dk].astype(BF16), k2[:, :dk].astype(BF16)
        vb = mv_ref[:, h * MLSTM_DV:(h + 1) * MLSTM_DV].astype(BF16)
        it_c = gc[:, h:h + 1]
        bt_c = btc[:, MLSTM_HEADS + h:MLSTM_HEADS + h + 1]
        it_r = gr[h:h + 1, :]
        bt_r = btr[MLSTM_HEADS + h:MLSTM_HEADS + h + 1, :]
        m_c = m_rows[:, h:h + 1]

        num, ssum, mt, wi = _mlstm_intra(qb, kb, vb, it_c, bt_c, it_r, bt_r, m_c, valid)
        qx = jnp.where(own_q, jnp.concatenate([q2] * (nb // 2), axis=1), 0.0).astype(BF16)
        c_stack = jnp.concatenate([c_ref[state_rows(b, h), :] for b in range(nb)], axis=0)
        qc = _dot(qx, c_stack.astype(BF16))
        n_rows = jnp.concatenate(
            [jnp.broadcast_to(n_ref[b * MLSTM_HEADS + h], (tq, dk)) for b in range(nb)], axis=0)
        num = num + wi * qc
        den = ssum + wi * jnp.sum(qb.astype(F32) * n_rows, axis=-1, keepdims=True)
        og = og_ref[:, h * MLSTM_DV:(h + 1) * MLSTM_DV]
        g_head = gh_ref[:, h * MLSTM_DV:(h + 1) * MLSTM_DV]
        hm_ref[:, h * MLSTM_DV:(h + 1) * MLSTM_DV] = _mlstm_head_out(num, den, mt, og, g_head)

        b_last = last_of_batch(bt_c)
        m_new = last_of_batch(mt)
        wk = jnp.exp(b_last - bt_c + it_c - m_new)
        wc = jnp.exp(b_last + m_c - m_new)
        wkv = (wk * vb.astype(F32)).astype(BF16)
        wkk = wk * kb.astype(F32)
        kt = mkt_ref[h * dk:(h + 1) * dk, :]
        kx = jnp.concatenate([jnp.where(lane_batch == b, kt, 0.0) for b in range(nb)], axis=0)
        upd = _dot(kx.astype(BF16), wkv)
        for b in range(nb):
            idx = b * MLSTM_HEADS + h
            wc_b = wc[b * tq:b * tq + 1]
            co_ref[state_rows(b, h), :] = wc_b * c_ref[state_rows(b, h), :] + upd[b * dk:(b + 1) * dk]
            no_ref[idx] = wc_b * n_ref[idx] + jnp.sum(wkk[b * tq:(b + 1) * tq], axis=0, keepdims=True)
        mt_all = jnp.where(lane == h, mt, mt_all)
    mt_ref[...] = mt_all


def _mlstm_sample(mq, mk, mkt, mv, og, gcol, grow, m_rows, ct_in, n_in, g_head, nbatch, nb):
    t = mq.shape[0]
    tq = t // nbatch
    rows = nb * tq
    assert tq & (tq - 1) == 0 and rows == LANES and nb % 2 == 0

    def tok(width):
        return pl.BlockSpec((rows, width), lambda i: (i, 0))

    ct_spec = pl.BlockSpec((nb * MLSTM_HEADS * MLSTM_DK, MLSTM_DV), lambda i: (i, 0))
    n_spec = pl.BlockSpec((nb * MLSTM_HEADS, 1, MLSTM_DK), lambda i: (i, 0, 0))
    return pl.pallas_call(
        functools.partial(_mlstm_sample_kernel, nb=nb, tq=tq),
        grid=(nbatch // nb,),
        in_specs=[
            tok(MQK_WIDTH), tok(MQK_WIDTH),
            pl.BlockSpec((MQK_WIDTH, rows), lambda i: (0, i)),
            tok(MLSTM_WIDTH), tok(MLSTM_WIDTH), tok(LANES),
            pl.BlockSpec((GATE_ROWS, rows), lambda i: (0, i)),
            tok(LANES),
            ct_spec, n_spec,
            _const_spec((1, MLSTM_WIDTH)),
        ],
        out_specs=[tok(MLSTM_WIDTH), ct_spec, n_spec, tok(LANES)],
        out_shape=[
            jax.ShapeDtypeStruct((t, MLSTM_WIDTH), F32),
            jax.ShapeDtypeStruct(ct_in.shape, F32),
            jax.ShapeDtypeStruct(n_in.shape, F32),
            jax.ShapeDtypeStruct((t, LANES), F32),
        ],
        compiler_params=pltpu.CompilerParams(
            dimension_semantics=("arbitrary",), vmem_limit_bytes=VMEM_LIMIT,
            allow_input_fusion=[False] * 7 + [True, False, True, False]),
    )(mq, mk, mkt, mv, og, gcol, grow, m_rows, ct_in, n_in, g_head)


def _mix_stage(x, att, hm, wo_ref, gcr_ref, wcq_ref):
    cat = jnp.concatenate([att.astype(BF16), hm.astype(BF16)], axis=1)
    y1 = x + _dot(cat, wo_ref[...])
    qc = _dot(_rmsnorm(y1, gcr_ref[...]).astype(BF16), wcq_ref[...])
    return y1, qc


def _cross_attention(qc, mk, mv):
    slabs = []
    for j in range(CROSS_WIDTH // LANES):
        qs = qc[:, j * LANES:(j + 1) * LANES]
        ks = mk[:, j * LANES:(j + 1) * LANES]
        vs = mv[:, j * LANES:(j + 1) * LANES]
        n = qs.shape[0]
        lo = _low_half(qs.shape)
        zero = jnp.zeros_like(qs)
        q2 = jnp.concatenate([jnp.where(lo, qs, zero), jnp.where(lo, zero, qs)], axis=0)
        s = _dot_nt(q2 * (HEAD_DIM ** -0.5), ks)
        p = jnp.exp(s - jnp.max(s, axis=-1, keepdims=True))
        o2 = _dot(p.astype(BF16), vs) / jnp.sum(p, axis=-1, keepdims=True)
        slabs.append(jnp.where(lo, o2[:n], o2[n:]))
    return jnp.concatenate(slabs, axis=1)


FFN_CHUNK = 256


def _finish(y3, gfin_ref, final):
    return _rmsnorm(y3, gfin_ref[...]) if final else y3


def _ffn_stage(y1, o, wco_ref, gf_ref, wgu_ref, wd_ref, act_ref, between=None):
    y2 = y1 + _dot(o.astype(BF16), wco_ref[...])
    xn = _rmsnorm(y2, gf_ref[...]).astype(BF16)
    for c in range(wd_ref.shape[0] // FFN_CHUNK):
        gu = _dot(xn, wgu_ref[:, 2 * c * FFN_CHUNK:2 * (c + 1) * FFN_CHUNK])
        g, u = gu[:, :FFN_CHUNK], gu[:, FFN_CHUNK:]
        act_ref[:, c * FFN_CHUNK:(c + 1) * FFN_CHUNK] = (g * jax.nn.sigmoid(g) * u).astype(BF16)
        if between is not None:
            between()
    return y2 + _dot(act_ref[...], wd_ref[...])


def _post_kernel(sink_ref, q_ref, kc_ref, kp_ref, vc_ref, vp_ref, x_ref, hm_ref, mk_ref, mv_ref,
                 xd_ref, attd_ref, hmd_ref, mkd_ref, mvd_ref,
                 wo_ref, gcr_ref, wcq_ref, wco_ref, gf_ref, wgu_ref, wd_ref, gfin_ref,
                 o_ref, od_ref, act_ref, att_ref, y3_ref, *, final, ntiles, tiles_per_seq, tq):
    tm = x_ref.shape[0]
    s = pl.program_id(0)
    tile = jnp.minimum(s, ntiles - 1)
    first_block = (tile % tiles_per_seq) == 0
    slot = s % 2

    def swa_stages():
        def store(j, value):
            att_ref[slot, j * WINDOW:(j + 1) * WINDOW, :] = value.astype(att_ref.dtype)
        return _swa_prompt_stages(sink_ref, q_ref, kc_ref, kp_ref, vc_ref, vp_ref, first_block, store)

    def finish():
        done = _finish(y3_ref[...], gfin_ref, final)
        o_ref[...] = done[:tm]
        od_ref[...] = done[tm:]

    @pl.when(s == 0)
    def _():
        y3_ref[...] = jnp.zeros_like(y3_ref)
        for _ in swa_stages():
            pass

    @pl.when(jnp.logical_and(s > 0, s <= ntiles))
    def _():
        swa = swa_stages()

        def advance(n=SWA_STAGES_PER_FFN_CHUNK):
            for _ in range(n):
                next(swa, None)

        finish()
        x = jnp.concatenate([x_ref[...], xd_ref[...]], axis=0)
        att = jnp.concatenate([att_ref[1 - slot], attd_ref[...].astype(BF16)], axis=0)
        hm = jnp.concatenate([hm_ref[...], hmd_ref[...].astype(BF16)], axis=0)
        y1, qc = _mix_stage(x, att, hm, wo_ref, gcr_ref, wcq_ref)
        advance()
        o = jnp.concatenate([
            _cross_attention(qc[:tm].astype(BF16), mk_ref[...].astype(BF16), mv_ref[...].astype(BF16)),
            _cross_attention_decode(qc[tm:], mkd_ref, mvd_ref, tq)], axis=0)
        advance()
        y3_ref[...] = _ffn_stage(y1, o, wco_ref, gf_ref, wgu_ref, wd_ref, act_ref, between=advance)
        for _ in swa:
            pass

    @pl.when(s > ntiles)
    def _():
        finish()


_MIX_WEIGHTS = ("w_out", "g_cross", "w_cq")
_FFN_WEIGHTS = ("w_co", "g_ffn", "w_gate_up", "w_down", "g_final")

def _post(x, q, k, v, hm, memk, memv, xd, attd, hmd, memkd_t, memvd_t, sinks, w, nbatch, nseq, final, tm):
    t, d = x.shape
    ntiles = t // tm
    tpb = ntiles // nbatch
    wpt = tm // WINDOW
    m = memk.shape[0] // nbatch
    hidden = w["w_down"].shape[0]
    td = xd.shape[0]
    tq = td // nseq
    slab = td // ntiles
    assert td % ntiles == 0 and slab % tq == 0 and slab % BF16_SUBLANES == 0
    mem_d = memkd_t.shape[1]

    def attn_tile(s):
        return jnp.minimum(s, ntiles - 1)

    def post_tile(s):
        return jnp.clip(s - 1, 0, ntiles - 1)

    def done_tile(s):
        return jnp.maximum(s - 2, 0)

    def attn_rows(width):
        return pl.BlockSpec((tm, width), lambda s: (attn_tile(s), 0))

    def prev_window(width):
        return pl.BlockSpec((WINDOW, width), lambda s: (jnp.maximum(attn_tile(s) * wpt - 1, 0), 0))

    def post_rows(width):
        return pl.BlockSpec((tm, width), lambda s: (post_tile(s), 0))

    def post_slab(width):
        return pl.BlockSpec((slab, width), lambda s: (post_tile(s), 0))

    mem_spec = pl.BlockSpec((m, CROSS_WIDTH), lambda s: (post_tile(s) // tpb, 0))
    memd_spec = pl.BlockSpec((slab // tq * CROSS_WIDTH, mem_d), lambda s: (post_tile(s), 0))
    names = _MIX_WEIGHTS + _FFN_WEIGHTS
    return pl.pallas_call(
        functools.partial(_post_kernel, final=final, ntiles=ntiles, tiles_per_seq=tpb, tq=tq),
        grid=(ntiles + 2,),
        in_specs=[
            pl.BlockSpec(memory_space=pltpu.SMEM),
            attn_rows(ATT_WIDTH), attn_rows(KV_WIDTH), prev_window(KV_WIDTH),
            attn_rows(KV_WIDTH), prev_window(KV_WIDTH),
            post_rows(d), post_rows(MLSTM_WIDTH), mem_spec, mem_spec,
            post_slab(d), post_slab(ATT_WIDTH), post_slab(MLSTM_WIDTH), memd_spec, memd_spec,
        ] + _weight_specs(w, names),
        out_specs=[pl.BlockSpec((tm, d), lambda s: (done_tile(s), 0)),
                   pl.BlockSpec((slab, d), lambda s: (done_tile(s), 0))],
        out_shape=[jax.ShapeDtypeStruct((t, d), F32), jax.ShapeDtypeStruct((td, d), F32)],
        scratch_shapes=[pltpu.VMEM((tm + slab, hidden), BF16), pltpu.VMEM((2, tm, ATT_WIDTH), BF16),
                        pltpu.VMEM((tm + slab, d), F32)],
        compiler_params=pltpu.CompilerParams(
            dimension_semantics=("arbitrary",), vmem_limit_bytes=VMEM_LIMIT),
    )(sinks, q, k, k, v, v, x, hm, memk, memv, xd, attd, hmd, memkd_t, memvd_t,
      *[w[n] for n in names])


def _cross_attention_decode(qc, mk_ref, mv_ref, tq):
    n = CROSS_HEADS * tq
    row_head = lax.broadcasted_iota(jnp.int32, (n, CROSS_WIDTH), 0) >> _log2(tq)
    lane_head = lax.broadcasted_iota(jnp.int32, (n, CROSS_WIDTH), 1) >> _log2(HEAD_DIM)
    own = row_head == lane_head
    outs = []
    for b in range(qc.shape[0] // tq):
        q = qc[b * tq:(b + 1) * tq, :]
        qx = jnp.where(own, jnp.concatenate([q] * CROSS_HEADS, axis=0), 0.0).astype(BF16)
        mk = mk_ref[b * CROSS_WIDTH:(b + 1) * CROSS_WIDTH, :].astype(BF16)
        mv = mv_ref[b * CROSS_WIDTH:(b + 1) * CROSS_WIDTH, :].astype(BF16)
        s = _dot(qx, mk) * (HEAD_DIM ** -0.5)
        p = jnp.exp(s - jnp.max(s, axis=-1, keepdims=True))
        pv = _dot_nt(p.astype(BF16), mv) / jnp.sum(p, axis=-1, keepdims=True)
        pv = jnp.where(own, pv, 0.0)
        out = pv[0:tq]
        for h in range(1, CROSS_HEADS):
            out = out + pv[h * tq:(h + 1) * tq]
        outs.append(out)
    return jnp.concatenate(outs, axis=0)


def _layer_weights(l, w_in, b_igate, b_fgate, g_mlstm_head, g_mix, g_cross, g_mem, w_ck, w_cv,
                   g_ffn, g_final):
    gate_w = w_in[l][:, MAIN_WIDTH:]
    gate_b = jnp.concatenate([b_igate[l], b_fgate[l]]).astype(F32)
    ngate = gate_b.shape[0]
    row = lambda a: a.astype(F32).reshape(1, -1)
    w_t = w_in[l].T
    k0 = ATT_WIDTH
    mk0 = ATT_WIDTH + 2 * KV_WIDTH + MQK_WIDTH
    return {
        "w_kvk_t": jnp.concatenate([w_t[k0:k0 + 2 * KV_WIDTH], w_t[mk0:mk0 + MQK_WIDTH]]).astype(BF16),
        "w_main": w_t[:MAIN_WIDTH].astype(BF16),
        "w_gate_cols": jnp.pad(gate_w, ((0, 0), (0, LANES - ngate))).astype(BF16),
        "w_gate_rows": jnp.pad(gate_w.T, ((0, GATE_ROWS - ngate), (0, 0))).astype(BF16),
        "b_gate_cols": jnp.pad(gate_b, (0, LANES - ngate)).reshape(1, LANES),
        "b_gate_rows": jnp.pad(gate_b, (0, GATE_ROWS - ngate)).reshape(GATE_ROWS, 1),
        "g_mix": row(g_mix[l]), "g_cross": row(g_cross[l]), "g_mem": row(g_mem[l]),
        "g_ffn": row(g_ffn[l]), "g_final": row(g_final), "g_head": row(g_mlstm_head[l]),
        "w_ck": w_ck[l].astype(F32), "w_cv": w_cv[l].astype(F32),
    }


TOKEN_TILE = 512
MLSTM_STAGES_PER_PROJ_STAGE = 15
SWA_STAGES_PER_FFN_CHUNK = 3
MLSTM_CHUNK_ROWS = 256
SWA_SAMPLE_BATCH = 16
MLSTM_SAMPLE_BATCH = 16


def kernel(x_prompt, x_sample, mem_prompt, cache_swa_k, cache_swa_v, state_mlstm_C, state_mlstm_n,
           state_mlstm_m, cache_mem_k, cache_mem_v, w_in, b_igate, b_fgate, attn_sinks,
           g_mlstm_head, w_out, g_mix, g_cross, g_mem, w_cq, w_ck, w_cv, w_co, g_ffn, w_gate,
           w_up, w_down, g_final):
    depth = w_in.shape[0]
    bp, sp, d = x_prompt.shape
    bs, ss, _ = x_sample.shape
    mem_tokens = mem_prompt.shape[1]
    past = cache_swa_k.shape[2]
    yp = x_prompt.reshape(bp * sp, d)
    ys = x_sample.reshape(bs * ss, d)
    mem = mem_prompt.reshape(bp * mem_tokens, d)
    outs = [[] for _ in range(12)]

    for l in range(depth):
        final = l == depth - 1
        w = _layer_weights(l, w_in, b_igate, b_fgate, g_mlstm_head, g_mix, g_cross, g_mem,
                           w_ck, w_cv, g_ffn, g_final)
        sinks = attn_sinks[l].astype(F32)

        q, k, v, hm, c_p, n_p, m_p, cast, memk, memv, memk_t, memv_t, kwin_t, vwin_t = _mixer_prompt(
            yp, mem, w, w_gate[l].astype(F32), w_up[l].astype(F32), w_down[l].astype(F32),
            w_out[l].astype(F32), w_cq[l].astype(F32), w_co[l].astype(F32), bp, MLSTM_CHUNK_ROWS)
        w.update(cast)

        def tokens_second(a, heads, tokens):
            return jnp.transpose(a.reshape(bp, heads, HEAD_DIM, tokens), (0, 3, 1, 2))

        outs[0].append(tokens_second(kwin_t, ATT_KV_HEADS, WINDOW))
        outs[1].append(tokens_second(vwin_t, ATT_KV_HEADS, WINDOW))
        outs[2].append(c_p)
        outs[3].append(n_p)
        outs[4].append(m_p)
        outs[5].append(tokens_second(memk_t, CROSS_HEADS, mem_tokens))
        outs[6].append(tokens_second(memv_t, CROSS_HEADS, mem_tokens))

        def tokens_last(a):
            return jnp.transpose(a.astype(F32), (0, 2, 3, 1)).reshape(-1, a.shape[1])

        qd, mq, mk, mv, og, grow, gcol, knt, vnt, mkt = _decode_proj(ys, w, TOKEN_TILE)
        att_d, kbuf_t, vbuf_t = _swa_sample(
            qd, knt, vnt, tokens_last(cache_swa_k[l]), tokens_last(cache_swa_v[l]), sinks, bs,
            SWA_SAMPLE_BATCH)
        m_rows = jnp.pad(jnp.repeat(state_mlstm_m[l].astype(F32), ss, axis=0),
                         ((0, 0), (0, LANES - MLSTM_HEADS)))
        ct_in = jnp.swapaxes(state_mlstm_C[l].astype(F32), 2, 3).reshape(-1, MLSTM_DV)
        hm_d, ct_s, n_s, mt = _mlstm_sample(
            mq, mk, mkt, mv, og, gcol, grow, m_rows, ct_in,
            state_mlstm_n[l].astype(F32).reshape(bs * MLSTM_HEADS, 1, MLSTM_DK),
            w["g_head"], bs, MLSTM_SAMPLE_BATCH)

        yp, ys = _post(yp, q, k, v, hm, memk, memv, ys, att_d, hm_d, tokens_last(cache_mem_k[l]),
                       tokens_last(cache_mem_v[l]), sinks, w, bp, bs, final, TOKEN_TILE)
        buf_t_shape = (bs, ATT_KV_HEADS, HEAD_DIM, past)
        outs[7].append(jnp.transpose(kbuf_t.reshape(buf_t_shape), (0, 3, 1, 2)))
        outs[8].append(jnp.transpose(vbuf_t.reshape(buf_t_shape), (0, 3, 1, 2)))
        outs[9].append(jnp.swapaxes(ct_s.reshape(bs, MLSTM_HEADS, MLSTM_DK, MLSTM_DV), 2, 3))
        outs[10].append(n_s.reshape(bs, MLSTM_HEADS, MLSTM_DK))
        outs[11].append(mt.reshape(bs, ss, LANES)[:, ss - 1, :MLSTM_HEADS])

    return (yp.reshape(bp, sp, d), ys.reshape(bs, ss, d)) + tuple(jnp.stack(o) for o in outs)
```

```python
import functools

import jax
import jax.numpy as jnp
from jax import lax
from jax.experimental import pallas as pl
from jax.experimental.pallas import tpu as pltpu

F32 = jnp.float32
BF16 = jnp.bfloat16

HEAD_DIM = 64
ATT_GROUP = 4
ATT_KV_HEADS = 2
ATT_WIDTH = 512
KV_WIDTH = ATT_KV_HEADS * HEAD_DIM
WINDOW = 128
MLSTM_HEADS = 4
MLSTM_DV = 128
MLSTM_DK = 64
MQK_WIDTH = MLSTM_HEADS * MLSTM_DK
MLSTM_WIDTH = MLSTM_HEADS * MLSTM_DV
MAIN_WIDTH = ATT_WIDTH + 2 * KV_WIDTH + 2 * MQK_WIDTH + 2 * MLSTM_WIDTH
CROSS_HEADS = 4
CROSS_WIDTH = CROSS_HEADS * HEAD_DIM
EPS = 1e-6
NEG_INF = float("-inf")
assert HEAD_DIM == 4 ** 3

LANES = 128
BF16_SUBLANES = 16
GATE_ROWS = BF16_SUBLANES
VMEM_LIMIT = 56 * 1024 * 1024

NT = (((1,), (1,)), ((), ()))


def _dot(a, b):
    return jnp.dot(a, b, preferred_element_type=F32)


def _dot_nt(a, b):
    return lax.dot_general(a, b, NT, preferred_element_type=F32)


def _rmsnorm(x, g):
    return x * lax.rsqrt(jnp.mean(x * x, axis=-1, keepdims=True) + EPS) * g


def _log_sigmoid(x):
    return jnp.minimum(x, 0.0) - jnp.log1p(jnp.exp(-jnp.abs(x)))


def _split3(x):
    hi = x.astype(BF16)
    r1 = x - hi.astype(F32)
    mid = r1.astype(BF16)
    lo = (r1 - mid.astype(F32)).astype(BF16)
    return hi, mid, lo


def _cumsum_cols(tri, x):
    hi, mid, lo = _split3(x)
    return _dot(tri, hi) + _dot(tri, mid) + _dot(tri, lo)


def _cumsum_rows(x, tri):
    hi, mid, lo = _split3(x)
    return _dot(hi, tri) + _dot(mid, tri) + _dot(lo, tri)


def _log2(n):
    assert n > 0 and n & (n - 1) == 0, n
    return n.bit_length() - 1


def _low_half(shape):
    return lax.broadcasted_iota(jnp.int32, shape, 1) < HEAD_DIM


def _decode_proj_kernel(x_ref, g_ref, w_ref, wgr_ref, br_ref, wgc_ref, bc_ref, wt_ref,
                        q_ref, mq_ref, mk_ref, mv_ref, og_ref, gr_ref, gc_ref, kt_ref, vt_ref, mkt_ref):
    xn = _rmsnorm(x_ref[...], g_ref[...]).astype(BF16)
    zt = _dot_nt(wt_ref[...], xn)
    kt_ref[...] = zt[:KV_WIDTH]
    vt_ref[...] = zt[KV_WIDTH:2 * KV_WIDTH]
    mkt_ref[...] = zt[2 * KV_WIDTH:] * (MLSTM_DK ** -0.5)

    def mm(lo, hi):
        return _dot_nt(xn, w_ref[lo:hi, :])

    o = 0
    q_ref[...] = mm(o, o + ATT_WIDTH)
    o += ATT_WIDTH + 2 * KV_WIDTH
    mqk = mm(o, o + 2 * MQK_WIDTH)
    mq_ref[...] = mqk[:, :MQK_WIDTH]
    mk_ref[...] = mqk[:, MQK_WIDTH:] * (MLSTM_DK ** -0.5)
    o += 2 * MQK_WIDTH
    mv_ref[...] = mm(o, o + MLSTM_WIDTH)
    o += MLSTM_WIDTH
    og_ref[...] = mm(o, o + MLSTM_WIDTH)

    zc = _dot(xn, wgc_ref[...]) + bc_ref[...]
    lane = lax.broadcasted_iota(jnp.int32, zc.shape, 1)
    gc_ref[...] = jnp.where(lane < MLSTM_HEADS, zc, _log_sigmoid(zc))
    zr = _dot_nt(wgr_ref[...], xn) + br_ref[...]
    row = lax.broadcasted_iota(jnp.int32, zr.shape, 0)
    gr_ref[...] = jnp.where(row < MLSTM_HEADS, zr, _log_sigmoid(zr))


def _const_spec(shape):
    nd = len(shape)
    return pl.BlockSpec(shape, lambda *_: (0,) * nd, pipeline_mode=pl.Buffered(1))


def _weight_specs(w, names):
    return [_const_spec(w[n].shape) for n in names]


def _decode_proj(x, w, tm):
    t, d = x.shape

    def rows(width):
        return pl.BlockSpec((tm, width), lambda i: (i, 0))

    def lanes(height):
        return pl.BlockSpec((height, tm), lambda i: (0, i))

    outs = [(rows, ATT_WIDTH), (rows, MQK_WIDTH), (rows, MQK_WIDTH), (rows, MLSTM_WIDTH),
            (rows, MLSTM_WIDTH), (lanes, GATE_ROWS), (rows, LANES), (lanes, KV_WIDTH),
            (lanes, KV_WIDTH), (lanes, MQK_WIDTH)]
    names = ["g_mix", "w_main", "w_gate_rows", "b_gate_rows", "w_gate_cols", "b_gate_cols", "w_kvk_t"]
    return pl.pallas_call(
        _decode_proj_kernel,
        grid=(t // tm,),
        in_specs=[rows(d)] + _weight_specs(w, names),
        out_specs=[kind(size) for kind, size in outs],
        out_shape=[jax.ShapeDtypeStruct((t, size) if kind is rows else (size, t), F32)
                   for kind, size in outs],
        compiler_params=pltpu.CompilerParams(
            dimension_semantics=("arbitrary",), vmem_limit_bytes=VMEM_LIMIT,
            allow_input_fusion=[False] + [True] * len(names)),
    )(x, *[w[n] for n in names])


def _dup_heads(x):
    swapped = pltpu.roll(x, HEAD_DIM, axis=1)
    lo = _low_half(x.shape)
    return (jnp.where(lo, x, swapped).astype(BF16), jnp.where(lo, swapped, x).astype(BF16))


def _stack_query_heads(q, h):
    parts = []
    for g in range(ATT_GROUP):
        hq = h * ATT_GROUP + g
        slab = q[:, (hq // 2) * LANES:(hq // 2 + 1) * LANES]
        lo = _low_half(slab.shape)
        keep = lo if hq % 2 == 0 else jnp.logical_not(lo)
        parts.append(jnp.where(keep, slab, jnp.zeros_like(slab)))
    return jnp.concatenate(parts, axis=0).astype(BF16)


def _sink_column(sink_ref, h, rows_per_head):
    n = ATT_GROUP * rows_per_head
    grp = lax.broadcasted_iota(jnp.int32, (n, 1), 0) >> _log2(rows_per_head)
    col = jnp.full((n, 1), sink_ref[h * ATT_GROUP], F32)
    for g in range(1, ATT_GROUP):
        col = jnp.where(grp == g, sink_ref[h * ATT_GROUP + g], col)
    return col


def _swa_prompt_stages(sink_ref, q_ref, kc_ref, kp_ref, vc_ref, vp_ref, first_block, store):
    qb = q_ref.shape[0]
    k_all = jnp.concatenate([kp_ref[...], kc_ref[...]], axis=0)
    v_all = jnp.concatenate([vp_ref[...], vc_ref[...]], axis=0)
    kd = _dup_heads(k_all)
    v_t = v_all.T.astype(BF16)

    n = ATT_GROUP * WINDOW
    key = lax.broadcasted_iota(jnp.int32, (2 * WINDOW, n), 0)
    qry = lax.broadcasted_iota(jnp.int32, (2 * WINDOW, n), 1) & (WINDOW - 1)
    delta = key - qry
    in_band = (delta >= 0) & (delta <= WINDOW)
    grp = lax.broadcasted_iota(jnp.int32, (1, n), 1) >> _log2(WINDOW)
    sinks = []
    for h in range(ATT_KV_HEADS):
        row = jnp.full((1, n), sink_ref[h * ATT_GROUP], F32)
        for g in range(1, ATT_GROUP):
            row = jnp.where(grp == g, sink_ref[h * ATT_GROUP + g], row)
        sinks.append(row)
    yield

    probs = []
    for j in range(qb // WINDOW):
        valid = in_band & (key >= jnp.where(first_block, WINDOW, 0)) if j == 0 else in_band
        q = q_ref[j * WINDOW:(j + 1) * WINDOW, :] * (HEAD_DIM ** -0.5)
        band = slice(j * WINDOW, (j + 2) * WINDOW)
        for h in range(ATT_KV_HEADS):
            qh = _stack_query_heads(q, h)
            s = _dot_nt(kd[h][band], qh)
            probs.append(dict(h=h, band=band, s=jnp.where(valid, s, NEG_INF)))
            yield
    for p in probs:
        p["m"] = jnp.maximum(jnp.max(p["s"], axis=0, keepdims=True), sinks[p["h"]])
        yield
    for p in probs:
        p["p"] = jnp.exp(p["s"] - p["m"])
        yield
    for p in probs:
        h = p["h"]
        denom = jnp.sum(p["p"], axis=0, keepdims=True) + jnp.exp(sinks[h] - p["m"])
        p["o_t"] = _dot(v_t[h * HEAD_DIM:(h + 1) * HEAD_DIM, p["band"]], p["p"].astype(BF16)) / denom
        yield
    for j in range(qb // WINDOW):
        pieces = [p["o_t"][:, g * WINDOW:(g + 1) * WINDOW]
                  for p in probs[j * ATT_KV_HEADS:(j + 1) * ATT_KV_HEADS] for g in range(ATT_GROUP)]
        store(j, jnp.concatenate(pieces, axis=0).T)
        yield


def _swa_sample_kernel(sink_ref, q_ref, knt_ref, vnt_ref, ck_ref, cv_ref,
                       o_ref, kbuf_ref, vbuf_ref, *, nb, tq):
    past = ck_ref.shape[1]
    hw = ATT_KV_HEADS * HEAD_DIM
    heads = ATT_KV_HEADS * ATT_GROUP
    n = heads * tq
    fresh = past - tq
    t = lax.broadcasted_iota(jnp.int32, (n, 2 * past), 0) & (tq - 1)
    col = lax.broadcasted_iota(jnp.int32, (n, 2 * past), 1)
    valid = ((col < past) & (col >= t)) | ((col >= past + fresh) & (col - (past + fresh) <= t))
    sink = jnp.concatenate([_sink_column(sink_ref, h, tq) for h in range(ATT_KV_HEADS)], axis=0)
    lane = lax.broadcasted_iota(jnp.int32, (hw, past), 1)
    knt = knt_ref[...]
    vnt = vnt_ref[...]
    zero_slab = jnp.zeros((tq, LANES), F32)
    lo = _low_half((tq, LANES))

    def shifted(old, new_t, b):
        return jnp.where(lane >= fresh, pltpu.roll(new_t, (fresh - b * tq) % past, axis=1),
                         pltpu.roll(old, fresh, axis=1))

    def dup_rows(x):
        h0, h1 = x[:HEAD_DIM], x[HEAD_DIM:]
        return jnp.concatenate([h0, h0, h1, h1], axis=0)

    for b in range(nb):
        k_old = ck_ref[b * hw:(b + 1) * hw, :]
        v_old = cv_ref[b * hw:(b + 1) * hw, :]
        k_new = shifted(k_old, knt, b)
        v_new = shifted(v_old, vnt, b)
        kbuf_ref[b * hw:(b + 1) * hw, :] = k_new
        vbuf_ref[b * hw:(b + 1) * hw, :] = v_new
        kop = jnp.concatenate([dup_rows(k_old), dup_rows(k_new)], axis=1).astype(BF16)
        vop = jnp.concatenate([dup_rows(v_old), dup_rows(v_new)], axis=1).astype(BF16)

        q = q_ref[b * tq:(b + 1) * tq, :]
        blocks = []
        for hq in range(heads):
            slab = q[:, (hq // 2) * LANES:(hq // 2 + 1) * LANES]
            slab = jnp.where(lo if hq % 2 == 0 else jnp.logical_not(lo), slab, 0.0)
            pair = [slab, zero_slab] if hq // ATT_GROUP == 0 else [zero_slab, slab]
            blocks.append(jnp.concatenate(pair, axis=1))
        qx = jnp.concatenate(blocks, axis=0).astype(BF16)

        s = _dot(qx, kop) * (HEAD_DIM ** -0.5)
        s = jnp.where(valid, s, NEG_INF)
        m = jnp.maximum(jnp.max(s, axis=-1, keepdims=True), sink)
        p = jnp.exp(s - m)
        denom = jnp.sum(p, axis=-1, keepdims=True) + jnp.exp(sink - m)
        o = _dot_nt(p.astype(BF16), vop) / denom

        slabs = []
        for j in range(heads // 2):
            h = (2 * j) // ATT_GROUP
            even = o[(2 * j) * tq:(2 * j + 1) * tq, h * LANES:(h + 1) * LANES]
            odd = o[(2 * j + 1) * tq:(2 * j + 2) * tq, h * LANES:(h + 1) * LANES]
            slabs.append(jnp.where(lo, even, odd))
        o_ref[b * tq:(b + 1) * tq, :] = jnp.concatenate(slabs, axis=1)


def _swa_sample(q, knt, vnt, cache_kt, cache_vt, sinks, nbatch, nb):
    t = q.shape[0]
    tq = t // nbatch
    past = cache_kt.shape[1]
    hw = ATT_KV_HEADS * HEAD_DIM
    assert past == WINDOW and tq & (tq - 1) == 0 and nb * tq == past

    def rows(r, width):
        return pl.BlockSpec((r, width), lambda i: (i, 0))

    new_t = pl.BlockSpec((hw, nb * tq), lambda i: (0, i))
    return pl.pallas_call(
        functools.partial(_swa_sample_kernel, nb=nb, tq=tq),
        grid=(nbatch // nb,),
        in_specs=[
            pl.BlockSpec(memory_space=pltpu.SMEM),
            rows(nb * tq, ATT_WIDTH), new_t, new_t, rows(nb * hw, past), rows(nb * hw, past),
        ],
        out_specs=[rows(nb * tq, ATT_WIDTH), rows(nb * hw, past), rows(nb * hw, past)],
        out_shape=[
            jax.ShapeDtypeStruct((t, ATT_WIDTH), F32),
            jax.ShapeDtypeStruct(cache_kt.shape, F32),
            jax.ShapeDtypeStruct(cache_vt.shape, F32),
        ],
        compiler_params=pltpu.CompilerParams(
            dimension_semantics=("arbitrary",), vmem_limit_bytes=VMEM_LIMIT),
    )(sinks, q, knt, vnt, cache_kt, cache_vt)


def _mlstm_intra(q, k, v, it_c, bt_c, it_r, bt_r, m_c, valid):
    d = jnp.where(valid, bt_c - bt_r + it_r, NEG_INF)
    inter = bt_c + m_c
    mt = jnp.maximum(jnp.max(d, axis=-1, keepdims=True), inter)
    sm = _dot_nt(q, k) * jnp.exp(d - mt)
    wi = jnp.exp(inter - mt)
    return _dot(sm.astype(BF16), v), jnp.sum(sm, axis=-1, keepdims=True), mt, wi


def _mlstm_head_out(num, den, mt, og, g_head):
    lower = jnp.maximum(jnp.abs(den), jnp.exp(-mt))
    h = num / lower
    hn = h * lax.rsqrt(jnp.mean(h * h, axis=-1, keepdims=True) + EPS)
    return hn * g_head * jax.nn.sigmoid(og)


def _mlstm_prompt_stages(mq, mk, mv, og, gr, gh_ref, hm_ref, c_ref, n_ref, m_ref, nbatch, chunk):
    s_idx = lax.broadcasted_iota(jnp.int32, (chunk, chunk), 0)
    l_idx = lax.broadcasted_iota(jnp.int32, (chunk, chunk), 1)
    causal = s_idx <= l_idx
    triu = jnp.where(causal, 1.0, 0.0).astype(BF16)
    lo = _low_half((chunk, LANES))
    pad_rows = jnp.zeros((LANES - MLSTM_HEADS, chunk), F32)
    sub8 = (GATE_ROWS, chunk)

    probs = []
    for b in range(nbatch):
        g = gr(b)
        btr = _cumsum_rows(g, triu)
        it_rows = g[:MLSTM_HEADS]
        bt_rows = btr[MLSTM_HEADS:2 * MLSTM_HEADS]
        u_cols = jnp.concatenate([it_rows - bt_rows, pad_rows], axis=0).T
        for h in range(MLSTM_HEADS):
            slab = slice((h // 2) * LANES, (h // 2 + 1) * LANES)
            keep = lo if h % 2 == 0 else jnp.logical_not(lo)
            q = mq(b, slab)
            k = mk(b, slab)
            probs.append(dict(
                b=b, h=h, idx=b * MLSTM_HEADS + h, cols=slice(h * MLSTM_DV, (h + 1) * MLSTM_DV),
                q=jnp.where(keep, q, jnp.zeros_like(q)), k=jnp.where(keep, k, jnp.zeros_like(k)),
                u_c=u_cols[:, h:h + 1], it_r=it_rows[h:h + 1, :], bt_r=bt_rows[h:h + 1, :]))
        yield

    for p in probs:
        p["m_old"] = m_ref[p["idx"]][:, 0:1]
        p["d"] = jnp.where(causal, p["bt_r"] + p["u_c"], NEG_INF)
        p["inter"] = p["bt_r"] + p["m_old"]
        p["q_t"] = p["q"].astype(F32).T.astype(BF16)
        p["kq"] = _dot(p["k"], p["q_t"])
        p["v_t"] = mv(p["b"], p["cols"]).astype(F32).T
        yield
    for p in probs:
        p["mt"] = jnp.maximum(jnp.max(p["d"], axis=0, keepdims=True), p["inter"])
        yield
    for p in probs:
        p["sm"] = p["kq"] * jnp.exp(p["d"] - p["mt"])
        p["wi"] = jnp.exp(p["inter"] - p["mt"])
        yield
    for p in probs:
        c_old = c_ref[p["idx"]]
        n_old = jnp.broadcast_to(n_ref[p["idx"]], (sub8[0], LANES))
        num = (_dot(p["v_t"].astype(BF16), p["sm"].astype(BF16))
               + p["wi"] * _dot(c_old.astype(BF16), p["q_t"]))
        den = (jnp.sum(p["sm"], axis=0, keepdims=True)
               + p["wi"] * _dot(n_old.astype(BF16), p["q_t"])[0:1])
        lower = jnp.maximum(jnp.abs(den), jnp.exp(-p["mt"]))
        h_t = num / lower
        p["hn"] = (h_t * lax.rsqrt(jnp.mean(h_t * h_t, axis=0, keepdims=True) + EPS)).T
        yield
    for p in probs:
        gate = jax.nn.sigmoid(og(p["b"], p["cols"]).astype(F32))
        hm_ref[p["b"], :, p["cols"]] = (p["hn"] * gh_ref[:, p["cols"]] * gate).astype(hm_ref.dtype)
        yield
    for p in probs:
        idx = p["idx"]
        b_last = p["bt_r"][:, chunk - 1:chunk]
        m_new = p["mt"][:, chunk - 1:chunk]
        wk = jnp.exp(b_last - p["bt_r"] + p["it_r"] - m_new)
        wc = jnp.exp(b_last + p["m_old"] - m_new)
        c_ref[idx] = wc * c_ref[idx] + _dot((p["v_t"] * wk).astype(BF16), p["k"])
        n_ref[idx] = wc * n_ref[idx] + _dot(jnp.broadcast_to(wk, sub8).astype(BF16), p["k"])[0:1]
        m_ref[idx] = jnp.broadcast_to(m_new, (1, LANES))
        yield


def _mixer_prompt_kernel(x_ref, g_ref, w_ref, wgr_ref, br_ref, gh_ref, gmem_ref, wck_ref, wcv_ref,
                         wg_ref, wu_ref, wd_ref, wo_ref, wcq_ref, wco_ref, mem_ref,
                         q_ref, k_ref, v_ref, hm_ref, ct_out, n_ref, m_ref, wgu_out, wd_out, wo_out,
                         wcq_out, wco_out, memk_out, memv_out, memkt_out, memvt_out, kwin_out, vwin_out,
                         mq_scr, mk_scr, mv_scr, og_scr, gr_scr, c_ref, wck_scr, wcv_scr,
                         *, nbatch, chunk):
    s = pl.program_id(0)
    wr = s % 2
    rd = 1 - wr
    rows = nbatch * chunk

    def side_work():
        for c in range(0, wg_ref.shape[1], FFN_CHUNK):
            wgu_out[:, 2 * c:2 * c + FFN_CHUNK] = wg_ref[:, c:c + FFN_CHUNK].astype(BF16)
            wgu_out[:, 2 * c + FFN_CHUNK:2 * (c + FFN_CHUNK)] = wu_ref[:, c:c + FFN_CHUNK].astype(BF16)
        wd_out[...] = wd_ref[...].astype(BF16)
        wo_out[...] = wo_ref[...].astype(BF16)
        wcq_out[...] = wcq_ref[...].astype(BF16)
        wco_out[...] = wco_ref[...].astype(BF16)
        mn = _rmsnorm(mem_ref[...], gmem_ref[...]).astype(BF16)
        mk = _dot(mn, wck_scr[...])
        mv = _dot(mn, wcv_scr[...])
        memk_out[...] = mk
        memv_out[...] = mv
        memkt_out[...] = mk.T
        memvt_out[...] = mv.T

    def proj_stages():
        xn = _rmsnorm(x_ref[...].reshape(rows, x_ref.shape[-1]), g_ref[...]).astype(BF16)
        yield

        def mm(lo, hi):
            return _dot_nt(xn, w_ref[lo:hi, :])

        o = 0
        q_ref[...] = mm(o, o + ATT_WIDTH).astype(q_ref.dtype).reshape(q_ref.shape)
        yield
        o += ATT_WIDTH
        kv = mm(o, o + 2 * KV_WIDTH)
        k_ref[...] = kv[:, :KV_WIDTH].reshape(k_ref.shape)
        v_ref[...] = kv[:, KV_WIDTH:].reshape(v_ref.shape)
        yield
        o += 2 * KV_WIDTH
        mqk = mm(o, o + 2 * MQK_WIDTH)
        mq_scr[wr] = mqk[:, :MQK_WIDTH].astype(BF16)
        mk_scr[wr] = (mqk[:, MQK_WIDTH:] * (MLSTM_DK ** -0.5)).astype(BF16)
        yield
        o += 2 * MQK_WIDTH
        mv_scr[wr] = mm(o, o + MLSTM_WIDTH).astype(BF16)
        yield
        o += MLSTM_WIDTH
        og_scr[wr] = mm(o, o + MLSTM_WIDTH).astype(BF16)
        yield
        zr = _dot_nt(wgr_ref[...], xn) + br_ref[...]
        row = lax.broadcasted_iota(jnp.int32, zr.shape, 0)
        gr_scr[wr] = jnp.where(row < MLSTM_HEADS, zr, _log_sigmoid(zr))
        yield

    def mlstm_stages():
        def rows_of(scr):
            return lambda b, lanes: scr[rd, b * chunk:(b + 1) * chunk, lanes]
        return _mlstm_prompt_stages(
            rows_of(mq_scr), rows_of(mk_scr), rows_of(mv_scr), rows_of(og_scr),
            lambda b: gr_scr[rd, :, b * chunk:(b + 1) * chunk],
            gh_ref, hm_ref, c_ref, n_ref, m_ref, nbatch, chunk)

    @pl.when(s == 0)
    def _():
        c_ref[...] = jnp.zeros_like(c_ref)
        n_ref[...] = jnp.zeros_like(n_ref)
        m_ref[...] = jnp.zeros_like(m_ref)
        wck_scr[...] = wck_ref[...].astype(BF16)
        wcv_scr[...] = wcv_ref[...].astype(BF16)
        side_work()
        for _ in proj_stages():
            pass

    @pl.when(s > 0)
    def _():
        side_work()
        mlstm = mlstm_stages()
        for _ in proj_stages():
            for _ in range(MLSTM_STAGES_PER_PROJ_STAGE):
                next(mlstm, None)
        for _ in mlstm:
            pass

    @pl.when(s == pl.num_programs(0) - 2)
    def _():
        for b in range(nbatch):
            kwin_out[b] = k_ref[b, chunk - WINDOW:, :].T
            vwin_out[b] = v_ref[b, chunk - WINDOW:, :].T

    @pl.when(s == pl.num_programs(0) - 1)
    def _():
        for i in range(c_ref.shape[0]):
            half = (i % MLSTM_HEADS) % 2
            ct_out[i] = c_ref[i].T[half * MLSTM_DK:(half + 1) * MLSTM_DK, :]


def _mixer_prompt(x, mem, w, w_gate, w_up, w_down, w_out, w_cq, w_co, nbatch, chunk):
    t, d = x.shape
    seq = t // nbatch
    nchunks = seq // chunk
    nstate = nbatch * MLSTM_HEADS
    rows = nbatch * chunk
    hidden = w_down.shape[0]
    assert hidden % FFN_CHUNK == 0
    assert d % (nchunks * BF16_SUBLANES) == 0 and hidden % (nchunks * BF16_SUBLANES) == 0
    assert w_co.shape[0] % (nchunks * BF16_SUBLANES) == 0
    mem_tokens = mem.shape[0] // nbatch
    blocks_per_batch = mem_tokens // LANES
    nblocks = nbatch * blocks_per_batch
    steps_per_block = nchunks // nblocks
    assert mem_tokens % LANES == 0 and nchunks % nblocks == 0 and chunk >= WINDOW

    def mem_block(s):
        return jnp.minimum(s // steps_per_block, nblocks - 1)

    mem_rows = pl.BlockSpec((LANES, d), lambda s: (mem_block(s), 0))
    mem_kv = pl.BlockSpec((LANES, CROSS_WIDTH), lambda s: (mem_block(s), 0))
    mem_kv_t = pl.BlockSpec((CROSS_WIDTH, LANES),
                            lambda s: (mem_block(s) // blocks_per_batch, mem_block(s) % blocks_per_batch))
    window_t = pl.BlockSpec((nbatch, KV_WIDTH, WINDOW), lambda s: (0, 0, 0))

    def proj_chunk(width):
        return pl.BlockSpec((nbatch, chunk, width), lambda s: (0, jnp.minimum(s, nchunks - 1), 0))

    def state(shape):
        return pl.BlockSpec(shape, lambda s: (0, 0, 0))

    def weight_rows(a, width=None):
        return pl.BlockSpec((a.shape[0] // nchunks, width or a.shape[1]),
                            lambda s: (jnp.minimum(s, nchunks - 1), 0))

    names = ["g_mix", "w_main", "w_gate_rows", "b_gate_rows", "g_head", "g_mem", "w_ck", "w_cv"]
    (q, k, v, hm, ct_st, n_st, m_st, w_gate_up_bf, w_down_bf, w_out_bf, w_cq_bf, w_co_bf, memk, memv,
     memk_t, memv_t, kwin_t, vwin_t) = pl.pallas_call(
        functools.partial(_mixer_prompt_kernel, nbatch=nbatch, chunk=chunk),
        grid=(nchunks + 1,),
        in_specs=[proj_chunk(d)] + _weight_specs(w, names) + [
            weight_rows(w_gate), weight_rows(w_up), weight_rows(w_down), weight_rows(w_out),
            weight_rows(w_cq), weight_rows(w_co), mem_rows,
        ],
        out_specs=[
            proj_chunk(ATT_WIDTH), proj_chunk(KV_WIDTH), proj_chunk(KV_WIDTH),
            pl.BlockSpec((nbatch, chunk, MLSTM_WIDTH), lambda s: (0, jnp.maximum(s - 1, 0), 0)),
            state((nstate, MLSTM_DK, MLSTM_DV)), state((nstate, 1, LANES)), state((nstate, 1, LANES)),
            weight_rows(w_gate, 2 * hidden), weight_rows(w_down), weight_rows(w_out),
            weight_rows(w_cq), weight_rows(w_co), mem_kv, mem_kv, mem_kv_t, mem_kv_t, window_t, window_t,
        ],
        out_shape=[
            jax.ShapeDtypeStruct((nbatch, seq, ATT_WIDTH), BF16),
            jax.ShapeDtypeStruct((nbatch, seq, KV_WIDTH), F32),
            jax.ShapeDtypeStruct((nbatch, seq, KV_WIDTH), F32),
            jax.ShapeDtypeStruct((nbatch, seq, MLSTM_WIDTH), BF16),
            jax.ShapeDtypeStruct((nstate, MLSTM_DK, MLSTM_DV), F32),
            jax.ShapeDtypeStruct((nstate, 1, LANES), F32),
            jax.ShapeDtypeStruct((nstate, 1, LANES), F32),
            jax.ShapeDtypeStruct((d, 2 * hidden), BF16),
            jax.ShapeDtypeStruct((hidden, d), BF16),
            jax.ShapeDtypeStruct(w_out.shape, BF16),
            jax.ShapeDtypeStruct(w_cq.shape, BF16),
            jax.ShapeDtypeStruct(w_co.shape, BF16),
            jax.ShapeDtypeStruct((mem.shape[0], CROSS_WIDTH), F32),
            jax.ShapeDtypeStruct((mem.shape[0], CROSS_WIDTH), F32),
            jax.ShapeDtypeStruct((nbatch * CROSS_WIDTH, mem_tokens), F32),
            jax.ShapeDtypeStruct((nbatch * CROSS_WIDTH, mem_tokens), F32),
            jax.ShapeDtypeStruct((nbatch, KV_WIDTH, WINDOW), F32),
            jax.ShapeDtypeStruct((nbatch, KV_WIDTH, WINDOW), F32),
        ],
        scratch_shapes=[
            pltpu.VMEM((2, rows, MQK_WIDTH), BF16), pltpu.VMEM((2, rows, MQK_WIDTH), BF16),
            pltpu.VMEM((2, rows, MLSTM_WIDTH), BF16), pltpu.VMEM((2, rows, MLSTM_WIDTH), BF16),
            pltpu.VMEM((2, GATE_ROWS, rows), F32),
            pltpu.VMEM((nstate, MLSTM_DV, LANES), F32),
            pltpu.VMEM(w["w_ck"].shape, BF16), pltpu.VMEM(w["w_cv"].shape, BF16),
        ],
        compiler_params=pltpu.CompilerParams(
            dimension_semantics=("arbitrary",), vmem_limit_bytes=VMEM_LIMIT,
            allow_input_fusion=[False] + [True] * len(names) + [False] * 7),
    )(x.reshape(nbatch, seq, d), *[w[n] for n in names], w_gate, w_up, w_down, w_out, w_cq, w_co, mem)
    q = q.reshape(t, ATT_WIDTH)
    k = k.reshape(t, KV_WIDTH)
    v = v.reshape(t, KV_WIDTH)

    c_fin = jnp.swapaxes(ct_st.reshape(nbatch, MLSTM_HEADS, MLSTM_DK, MLSTM_DV), 2, 3)
    n_st = n_st.reshape(nbatch, MLSTM_HEADS, 2, MLSTM_DK)
    n_fin = jnp.stack([n_st[:, h, h % 2, :] for h in range(MLSTM_HEADS)], axis=1)
    m_fin = m_st[:, 0, 0].reshape(nbatch, MLSTM_HEADS)
    bf16_weights = {"w_gate_up": w_gate_up_bf, "w_down": w_down_bf, "w_out": w_out_bf,
                    "w_cq": w_cq_bf, "w_co": w_co_bf}
    return (q, k, v, hm.reshape(t, MLSTM_WIDTH), c_fin, n_fin, m_fin, bf16_weights, memk, memv,
            memk_t, memv_t, kwin_t, vwin_t)


def _mlstm_sample_kernel(mq_ref, mk_ref, mkt_ref, mv_ref, og_ref, gc_ref, gr_ref, mrow_ref, c_ref,
                         n_ref, gh_ref, hm_ref, co_ref, no_ref, mt_ref, *, nb, tq):
    rows = nb * tq
    shift = tq.bit_length() - 1
    r = lax.broadcasted_iota(jnp.int32, (rows, rows), 0)
    s = lax.broadcasted_iota(jnp.int32, (rows, rows), 1)
    same = (r >> shift) == (s >> shift)
    valid = same & (s <= r)
    tril = jnp.where(valid, 1.0, 0.0).astype(BF16)
    triu = jnp.where(same & (r <= s), 1.0, 0.0).astype(BF16)
    row_batch = lax.broadcasted_iota(jnp.int32, (rows, 1), 0) >> shift
    lane = lax.broadcasted_iota(jnp.int32, (rows, LANES), 1)

    gc = gc_ref[...]
    gr = gr_ref[...]
    btc = _cumsum_cols(tril, gc)
    btr = _cumsum_rows(gr, triu)
    m_rows = mrow_ref[...]
    mt_all = jnp.zeros((rows, LANES), F32)

    def last_of_batch(col):
        parts = [jnp.broadcast_to(col[(b + 1) * tq - 1:(b + 1) * tq], (tq, 1)) for b in range(nb)]
        return jnp.concatenate(parts, axis=0)

    dk = MLSTM_DK
    lo = _low_half((rows, LANES))
    own_q = row_batch == (lax.broadcasted_iota(jnp.int32, (rows, nb * dk), 1) >> _log2(dk))
    lane_batch = lax.broadcasted_iota(jnp.int32, (dk, rows), 1) >> shift

    def state_rows(b, h):
        return slice((b * MLSTM_HEADS + h) * dk, (b * MLSTM_HEADS + h + 1) * dk)

    for h in range(MLSTM_HEADS):
        slab = slice((h // 2) * LANES, (h // 2 + 1) * LANES)
        qs, ks = mq_ref[:, slab], mk_ref[:, slab]
        qsw, ksw = pltpu.roll(qs, dk, axis=1), pltpu.roll(ks, dk, axis=1)
        q2 = jnp.where(lo, qs, qsw) if h % 2 == 0 else jnp.where(lo, qsw, qs)
        k2 = jnp.where(lo, ks, ksw) if h % 2 == 0 else jnp.where(lo, ksw, ks)
        qb, kb = q2[:, :dk].astype(BF16), k2[:, :dk].astype(BF16)
        vb = mv_ref[:, h * MLSTM_DV:(h + 1) * MLSTM_DV].astype(BF16)
        it_c = gc[:, h:h + 1]
        bt_c = btc[:, MLSTM_HEADS + h:MLSTM_HEADS + h + 1]
        it_r = gr[h:h + 1, :]
        bt_r = btr[MLSTM_HEADS + h:MLSTM_HEADS + h + 1, :]
        m_c = m_rows[:, h:h + 1]

        num, ssum, mt, wi = _mlstm_intra(qb, kb, vb, it_c, bt_c, it_r, bt_r, m_c, valid)
        qx = jnp.where(own_q, jnp.concatenate([q2] * (nb // 2), axis=1), 0.0).astype(BF16)
        c_stack = jnp.concatenate([c_ref[state_rows(b, h), :] for b in range(nb)], axis=0)
        qc = _dot(qx, c_stack.astype(BF16))
        n_rows = jnp.concatenate(
            [jnp.broadcast_to(n_ref[b * MLSTM_HEADS + h], (tq, dk)) for b in range(nb)], axis=0)
        num = num + wi * qc
        den = ssum + wi * jnp.sum(qb.astype(F32) * n_rows, axis=-1, keepdims=True)
        og = og_ref[:, h * MLSTM_DV:(h + 1) * MLSTM_DV]
        g_head = gh_ref[:, h * MLSTM_DV:(h + 1) * MLSTM_DV]
        hm_ref[:, h * MLSTM_DV:(h + 1) * MLSTM_DV] = _mlstm_head_out(num, den, mt, og, g_head)

        b_last = last_of_batch(bt_c)
        m_new = last_of_batch(mt)
        wk = jnp.exp(b_last - bt_c + it_c - m_new)
        wc = jnp.exp(b_last + m_c - m_new)
        wkv = (wk * vb.astype(F32)).astype(BF16)
        wkk = wk * kb.astype(F32)
        kt = mkt_ref[h * dk:(h + 1) * dk, :]
        kx = jnp.concatenate([jnp.where(lane_batch == b, kt, 0.0) for b in range(nb)], axis=0)
        upd = _dot(kx.astype(BF16), wkv)
        for b in range(nb):
            idx = b * MLSTM_HEADS + h
            wc_b = wc[b * tq:b * tq + 1]
            co_ref[state_rows(b, h), :] = wc_b * c_ref[state_rows(b, h), :] + upd[b * dk:(b + 1) * dk]
            no_ref[idx] = wc_b * n_ref[idx] + jnp.sum(wkk[b * tq:(b + 1) * tq], axis=0, keepdims=True)
        mt_all = jnp.where(lane == h, mt, mt_all)
    mt_ref[...] = mt_all


def _mlstm_sample(mq, mk, mkt, mv, og, gcol, grow, m_rows, ct_in, n_in, g_head, nbatch, nb):
    t = mq.shape[0]
    tq = t // nbatch
    rows = nb * tq
    assert tq & (tq - 1) == 0 and rows == LANES and nb % 2 == 0

    def tok(width):
        return pl.BlockSpec((rows, width), lambda i: (i, 0))

    ct_spec = pl.BlockSpec((nb * MLSTM_HEADS * MLSTM_DK, MLSTM_DV), lambda i: (i, 0))
    n_spec = pl.BlockSpec((nb * MLSTM_HEADS, 1, MLSTM_DK), lambda i: (i, 0, 0))
    return pl.pallas_call(
        functools.partial(_mlstm_sample_kernel, nb=nb, tq=tq),
        grid=(nbatch // nb,),
        in_specs=[
            tok(MQK_WIDTH), tok(MQK_WIDTH),
            pl.BlockSpec((MQK_WIDTH, rows), lambda i: (0, i)),
            tok(MLSTM_WIDTH), tok(MLSTM_WIDTH), tok(LANES),
            pl.BlockSpec((GATE_ROWS, rows), lambda i: (0, i)),
            tok(LANES),
            ct_spec, n_spec,
            _const_spec((1, MLSTM_WIDTH)),
        ],
        out_specs=[tok(MLSTM_WIDTH), ct_spec, n_spec, tok(LANES)],
        out_shape=[
            jax.ShapeDtypeStruct((t, MLSTM_WIDTH), F32),
            jax.ShapeDtypeStruct(ct_in.shape, F32),
            jax.ShapeDtypeStruct(n_in.shape, F32),
            jax.ShapeDtypeStruct((t, LANES), F32),
        ],
        compiler_params=pltpu.CompilerParams(
            dimension_semantics=("arbitrary",), vmem_limit_bytes=VMEM_LIMIT,
            allow_input_fusion=[False] * 7 + [True, False, True, False]),
    )(mq, mk, mkt, mv, og, gcol, grow, m_rows, ct_in, n_in, g_head)


def _mix_stage(x, att, hm, wo_ref, gcr_ref, wcq_ref):
    cat = jnp.concatenate([att.astype(BF16), hm.astype(BF16)], axis=1)
    y1 = x + _dot(cat, wo_ref[...])
    qc = _dot(_rmsnorm(y1, gcr_ref[...]).astype(BF16), wcq_ref[...])
    return y1, qc


def _cross_attention(qc, mk, mv):
    slabs = []
    for j in range(CROSS_WIDTH // LANES):
        qs = qc[:, j * LANES:(j + 1) * LANES]
        ks = mk[:, j * LANES:(j + 1) * LANES]
        vs = mv[:, j * LANES:(j + 1) * LANES]
        n = qs.shape[0]
        lo = _low_half(qs.shape)
        zero = jnp.zeros_like(qs)
        q2 = jnp.concatenate([jnp.where(lo, qs, zero), jnp.where(lo, zero, qs)], axis=0)
        s = _dot_nt(q2 * (HEAD_DIM ** -0.5), ks)
        p = jnp.exp(s - jnp.max(s, axis=-1, keepdims=True))
        o2 = _dot(p.astype(BF16), vs) / jnp.sum(p, axis=-1, keepdims=True)
        slabs.append(jnp.where(lo, o2[:n], o2[n:]))
    return jnp.concatenate(slabs, axis=1)


FFN_CHUNK = 256


def _finish(y3, gfin_ref, final):
    return _rmsnorm(y3, gfin_ref[...]) if final else y3


def _ffn_stage(y1, o, wco_ref, gf_ref, wgu_ref, wd_ref, act_ref, between=None):
    y2 = y1 + _dot(o.astype(BF16), wco_ref[...])
    xn = _rmsnorm(y2, gf_ref[...]).astype(BF16)
    for c in range(wd_ref.shape[0] // FFN_CHUNK):
        gu = _dot(xn, wgu_ref[:, 2 * c * FFN_CHUNK:2 * (c + 1) * FFN_CHUNK])
        g, u = gu[:, :FFN_CHUNK], gu[:, FFN_CHUNK:]
        act_ref[:, c * FFN_CHUNK:(c + 1) * FFN_CHUNK] = (g * jax.nn.sigmoid(g) * u).astype(BF16)
        if between is not None:
            between()
    return y2 + _dot(act_ref[...], wd_ref[...])


def _post_kernel(sink_ref, q_ref, kc_ref, kp_ref, vc_ref, vp_ref, x_ref, hm_ref, mk_ref, mv_ref,
                 xd_ref, attd_ref, hmd_ref, mkd_ref, mvd_ref,
                 wo_ref, gcr_ref, wcq_ref, wco_ref, gf_ref, wgu_ref, wd_ref, gfin_ref,
                 o_ref, od_ref, act_ref, att_ref, y3_ref, *, final, ntiles, tiles_per_seq, tq):
    tm = x_ref.shape[0]
    s = pl.program_id(0)
    tile = jnp.minimum(s, ntiles - 1)
    first_block = (tile % tiles_per_seq) == 0
    slot = s % 2

    def swa_stages():
        def store(j, value):
            att_ref[slot, j * WINDOW:(j + 1) * WINDOW, :] = value.astype(att_ref.dtype)
        return _swa_prompt_stages(sink_ref, q_ref, kc_ref, kp_ref, vc_ref, vp_ref, first_block, store)

    def finish():
        done = _finish(y3_ref[...], gfin_ref, final)
        o_ref[...] = done[:tm]
        od_ref[...] = done[tm:]

    @pl.when(s == 0)
    def _():
        y3_ref[...] = jnp.zeros_like(y3_ref)
        for _ in swa_stages():
            pass

    @pl.when(jnp.logical_and(s > 0, s <= ntiles))
    def _():
        swa = swa_stages()

        def advance(n=SWA_STAGES_PER_FFN_CHUNK):
            for _ in range(n):
                next(swa, None)

        finish()
        x = jnp.concatenate([x_ref[...], xd_ref[...]], axis=0)
        att = jnp.concatenate([att_ref[1 - slot], attd_ref[...].astype(BF16)], axis=0)
        hm = jnp.concatenate([hm_ref[...], hmd_ref[...].astype(BF16)], axis=0)
        y1, qc = _mix_stage(x, att, hm, wo_ref, gcr_ref, wcq_ref)
        advance()
        o = jnp.concatenate([
            _cross_attention(qc[:tm].astype(BF16), mk_ref[...].astype(BF16), mv_ref[...].astype(BF16)),
            _cross_attention_decode(qc[tm:], mkd_ref, mvd_ref, tq)], axis=0)
        advance()
        y3_ref[...] = _ffn_stage(y1, o, wco_ref, gf_ref, wgu_ref, wd_ref, act_ref, between=advance)
        for _ in swa:
            pass

    @pl.when(s > ntiles)
    def _():
        finish()


_MIX_WEIGHTS = ("w_out", "g_cross", "w_cq")
_FFN_WEIGHTS = ("w_co", "g_ffn", "w_gate_up", "w_down", "g_final")

def _post(x, q, k, v, hm, memk, memv, xd, attd, hmd, memkd_t, memvd_t, sinks, w, nbatch, nseq, final, tm):
    t, d = x.shape
    ntiles = t // tm
    tpb = ntiles // nbatch
    wpt = tm // WINDOW
    m = memk.shape[0] // nbatch
    hidden = w["w_down"].shape[0]
    td = xd.shape[0]
    tq = td // nseq
    slab = td // ntiles
    assert td % ntiles == 0 and slab % tq == 0 and slab % BF16_SUBLANES == 0
    mem_d = memkd_t.shape[1]

    def attn_tile(s):
        return jnp.minimum(s, ntiles - 1)

    def post_tile(s):
        return jnp.clip(s - 1, 0, ntiles - 1)

    def done_tile(s):
        return jnp.maximum(s - 2, 0)

    def attn_rows(width):
        return pl.BlockSpec((tm, width), lambda s: (attn_tile(s), 0))

    def prev_window(width):
        return pl.BlockSpec((WINDOW, width), lambda s: (jnp.maximum(attn_tile(s) * wpt - 1, 0), 0))

    def post_rows(width):
        return pl.BlockSpec((tm, width), lambda s: (post_tile(s), 0))

    def post_slab(width):
        return pl.BlockSpec((slab, width), lambda s: (post_tile(s), 0))

    mem_spec = pl.BlockSpec((m, CROSS_WIDTH), lambda s: (post_tile(s) // tpb, 0))
    memd_spec = pl.BlockSpec((slab // tq * CROSS_WIDTH, mem_d), lambda s: (post_tile(s), 0))
    names = _MIX_WEIGHTS + _FFN_WEIGHTS
    return pl.pallas_call(
        functools.partial(_post_kernel, final=final, ntiles=ntiles, tiles_per_seq=tpb, tq=tq),
        grid=(ntiles + 2,),
        in_specs=[
            pl.BlockSpec(memory_space=pltpu.SMEM),
            attn_rows(ATT_WIDTH), attn_rows(KV_WIDTH), prev_window(KV_WIDTH),
            attn_rows(KV_WIDTH), prev_window(KV_WIDTH),
            post_rows(d), post_rows(MLSTM_WIDTH), mem_spec, mem_spec,
            post_slab(d), post_slab(ATT_WIDTH), post_slab(MLSTM_WIDTH), memd_spec, memd_spec,
        ] + _weight_specs(w, names),
        out_specs=[pl.BlockSpec((tm, d), lambda s: (done_tile(s), 0)),
                   pl.BlockSpec((slab, d), lambda s: (done_tile(s), 0))],
        out_shape=[jax.ShapeDtypeStruct((t, d), F32), jax.ShapeDtypeStruct((td, d), F32)],
        scratch_shapes=[pltpu.VMEM((tm + slab, hidden), BF16), pltpu.VMEM((2, tm, ATT_WIDTH), BF16),
                        pltpu.VMEM((tm + slab, d), F32)],
        compiler_params=pltpu.CompilerParams(
            dimension_semantics=("arbitrary",), vmem_limit_bytes=VMEM_LIMIT),
    )(sinks, q, k, k, v, v, x, hm, memk, memv, xd, attd, hmd, memkd_t, memvd_t,
      *[w[n] for n in names])


def _cross_attention_decode(qc, mk_ref, mv_ref, tq):
    n = CROSS_HEADS * tq
    row_head = lax.broadcasted_iota(jnp.int32, (n, CROSS_WIDTH), 0) >> _log2(tq)
    lane_head = lax.broadcasted_iota(jnp.int32, (n, CROSS_WIDTH), 1) >> _log2(HEAD_DIM)
    own = row_head == lane_head
    outs = []
    for b in range(qc.shape[0] // tq):
        q = qc[b * tq:(b + 1) * tq, :]
        qx = jnp.where(own, jnp.concatenate([q] * CROSS_HEADS, axis=0), 0.0).astype(BF16)
        mk = mk_ref[b * CROSS_WIDTH:(b + 1) * CROSS_WIDTH, :].astype(BF16)
        mv = mv_ref[b * CROSS_WIDTH:(b + 1) * CROSS_WIDTH, :].astype(BF16)
        s = _dot(qx, mk) * (HEAD_DIM ** -0.5)
        p = jnp.exp(s - jnp.max(s, axis=-1, keepdims=True))
        pv = _dot_nt(p.astype(BF16), mv) / jnp.sum(p, axis=-1, keepdims=True)
        pv = jnp.where(own, pv, 0.0)
        out = pv[0:tq]
        for h in range(1, CROSS_HEADS):
            out = out + pv[h * tq:(h + 1) * tq]
        outs.append(out)
    return jnp.concatenate(outs, axis=0)


def _layer_weights(l, w_in, b_igate, b_fgate, g_mlstm_head, g_mix, g_cross, g_mem, w_ck, w_cv,
                   g_ffn, g_final):
    gate_w = w_in[l][:, MAIN_WIDTH:]
    gate_b = jnp.concatenate([b_igate[l], b_fgate[l]]).astype(F32)
    ngate = gate_b.shape[0]
    row = lambda a: a.astype(F32).reshape(1, -1)
    w_t = w_in[l].T
    k0 = ATT_WIDTH
    mk0 = ATT_WIDTH + 2 * KV_WIDTH + MQK_WIDTH
    return {
        "w_kvk_t": jnp.concatenate([w_t[k0:k0 + 2 * KV_WIDTH], w_t[mk0:mk0 + MQK_WIDTH]]).astype(BF16),
        "w_main": w_t[:MAIN_WIDTH].astype(BF16),
        "w_gate_cols": jnp.pad(gate_w, ((0, 0), (0, LANES - ngate))).astype(BF16),
        "w_gate_rows": jnp.pad(gate_w.T, ((0, GATE_ROWS - ngate), (0, 0))).astype(BF16),
        "b_gate_cols": jnp.pad(gate_b, (0, LANES - ngate)).reshape(1, LANES),
        "b_gate_rows": jnp.pad(gate_b, (0, GATE_ROWS - ngate)).reshape(GATE_ROWS, 1),
        "g_mix": row(g_mix[l]), "g_cross": row(g_cross[l]), "g_mem": row(g_mem[l]),
        "g_ffn": row(g_ffn[l]), "g_final": row(g_final), "g_head": row(g_mlstm_head[l]),
        "w_ck": w_ck[l].astype(F32), "w_cv": w_cv[l].astype(F32),
    }


TOKEN_TILE = 512
MLSTM_STAGES_PER_PROJ_STAGE = 20
SWA_STAGES_PER_FFN_CHUNK = 3
MLSTM_CHUNK_ROWS = 256
SWA_SAMPLE_BATCH = 16
MLSTM_SAMPLE_BATCH = 16


def kernel(x_prompt, x_sample, mem_prompt, cache_swa_k, cache_swa_v, state_mlstm_C, state_mlstm_n,
           state_mlstm_m, cache_mem_k, cache_mem_v, w_in, b_igate, b_fgate, attn_sinks,
           g_mlstm_head, w_out, g_mix, g_cross, g_mem, w_cq, w_ck, w_cv, w_co, g_ffn, w_gate,
           w_up, w_down, g_final):
    depth = w_in.shape[0]
    bp, sp, d = x_prompt.shape
    bs, ss, _ = x_sample.shape
    mem_tokens = mem_prompt.shape[1]
    past = cache_swa_k.shape[2]
    yp = x_prompt.reshape(bp * sp, d)
    ys = x_sample.reshape(bs * ss, d)
    mem = mem_prompt.reshape(bp * mem_tokens, d)
    outs = [[] for _ in range(12)]

    for l in range(depth):
        final = l == depth - 1
        w = _layer_weights(l, w_in, b_igate, b_fgate, g_mlstm_head, g_mix, g_cross, g_mem,
                           w_ck, w_cv, g_ffn, g_final)
        sinks = attn_sinks[l].astype(F32)

        q, k, v, hm, c_p, n_p, m_p, cast, memk, memv, memk_t, memv_t, kwin_t, vwin_t = _mixer_prompt(
            yp, mem, w, w_gate[l].astype(F32), w_up[l].astype(F32), w_down[l].astype(F32),
            w_out[l].astype(F32), w_cq[l].astype(F32), w_co[l].astype(F32), bp, MLSTM_CHUNK_ROWS)
        w.update(cast)

        def tokens_second(a, heads, tokens):
            return jnp.transpose(a.reshape(bp, heads, HEAD_DIM, tokens), (0, 3, 1, 2))

        outs[0].append(tokens_second(kwin_t, ATT_KV_HEADS, WINDOW))
        outs[1].append(tokens_second(vwin_t, ATT_KV_HEADS, WINDOW))
        outs[2].append(c_p)
        outs[3].append(n_p)
        outs[4].append(m_p)
        outs[5].append(tokens_second(memk_t, CROSS_HEADS, mem_tokens))
        outs[6].append(tokens_second(memv_t, CROSS_HEADS, mem_tokens))

        def tokens_last(a):
            return jnp.transpose(a.astype(F32), (0, 2, 3, 1)).reshape(-1, a.shape[1])

        qd, mq, mk, mv, og, grow, gcol, knt, vnt, mkt = _decode_proj(ys, w, TOKEN_TILE)
        att_d, kbuf_t, vbuf_t = _swa_sample(
            qd, knt, vnt, tokens_last(cache_swa_k[l]), tokens_last(cache_swa_v[l]), sinks, bs,
            SWA_SAMPLE_BATCH)
        m_rows = jnp.pad(jnp.repeat(state_mlstm_m[l].astype(F32), ss, axis=0),
                         ((0, 0), (0, LANES - MLSTM_HEADS)))
        ct_in = jnp.swapaxes(state_mlstm_C[l].astype(F32), 2, 3).reshape(-1, MLSTM_DV)
        hm_d, ct_s, n_s, mt = _mlstm_sample(
            mq, mk, mkt, mv, og, gcol, grow, m_rows, ct_in,
            state_mlstm_n[l].astype(F32).reshape(bs * MLSTM_HEADS, 1, MLSTM_DK),
            w["g_head"], bs, MLSTM_SAMPLE_BATCH)

        yp, ys = _post(yp, q, k, v, hm, memk, memv, ys, att_d, hm_d, tokens_last(cache_mem_k[l]),
                       tokens_last(cache_mem_v[l]), sinks, w, bp, bs, final, TOKEN_TILE)
        buf_t_shape = (bs, ATT_KV_HEADS, HEAD_DIM, past)
        outs[7].append(jnp.transpose(kbuf_t.reshape(buf_t_shape), (0, 3, 1, 2)))
        outs[8].append(jnp.transpose(vbuf_t.reshape(buf_t_shape), (0, 3, 1, 2)))
        outs[9].append(jnp.swapaxes(ct_s.reshape(bs, MLSTM_HEADS, MLSTM_DK, MLSTM_DV), 2, 3))
        outs[10].append(n_s.reshape(bs, MLSTM_HEADS, MLSTM_DK))
        outs[11].append(mt.reshape(bs, ss, LANES)[:, ss - 1, :MLSTM_HEADS])

    return (yp.reshape(bp, sp, d), ys.reshape(bs, ss, d)) + tuple(jnp.stack(o) for o in outs)
```

```python
import functools

import jax
import jax.numpy as jnp
from jax import lax
from jax.experimental import pallas as pl
from jax.experimental.pallas import tpu as pltpu

F32 = jnp.float32
BF16 = jnp.bfloat16

HEAD_DIM = 64
ATT_GROUP = 4
ATT_KV_HEADS = 2
ATT_WIDTH = 512
KV_WIDTH = ATT_KV_HEADS * HEAD_DIM
WINDOW = 128
MLSTM_HEADS = 4
MLSTM_DV = 128
MLSTM_DK = 64
MQK_WIDTH = MLSTM_HEADS * MLSTM_DK
MLSTM_WIDTH = MLSTM_HEADS * MLSTM_DV
MAIN_WIDTH = ATT_WIDTH + 2 * KV_WIDTH + 2 * MQK_WIDTH + 2 * MLSTM_WIDTH
CROSS_HEADS = 4
CROSS_WIDTH = CROSS_HEADS * HEAD_DIM
EPS = 1e-6
NEG_INF = float("-inf")
assert HEAD_DIM == 4 ** 3

LANES = 128
BF16_SUBLANES = 16
GATE_ROWS = BF16_SUBLANES
VMEM_LIMIT = 56 * 1024 * 1024

NT = (((1,), (1,)), ((), ()))


def _dot(a, b):
    return jnp.dot(a, b, preferred_element_type=F32)


def _dot_nt(a, b):
    return lax.dot_general(a, b, NT, preferred_element_type=F32)


def _rmsnorm(x, g):
    return x * lax.rsqrt(jnp.mean(x * x, axis=-1, keepdims=True) + EPS) * g


def _log_sigmoid(x):
    return jnp.minimum(x, 0.0) - jnp.log1p(jnp.exp(-jnp.abs(x)))


def _split3(x):
    hi = x.astype(BF16)
    r1 = x - hi.astype(F32)
    mid = r1.astype(BF16)
    lo = (r1 - mid.astype(F32)).astype(BF16)
    return hi, mid, lo


def _cumsum_cols(tri, x):
    hi, mid, lo = _split3(x)
    return _dot(tri, hi) + _dot(tri, mid) + _dot(tri, lo)


def _cumsum_rows(x, tri):
    hi, mid, lo = _split3(x)
    return _dot(hi, tri) + _dot(mid, tri) + _dot(lo, tri)


def _log2(n):
    assert n > 0 and n & (n - 1) == 0, n
    return n.bit_length() - 1


def _low_half(shape):
    return lax.broadcasted_iota(jnp.int32, shape, 1) < HEAD_DIM


def _decode_proj_kernel(x_ref, g_ref, w_ref, wgr_ref, br_ref, wgc_ref, bc_ref,
                        q_ref, mq_ref, mk_ref, mv_ref, og_ref, gr_ref, gc_ref, kt_ref, vt_ref, mkt_ref):
    xn = _rmsnorm(x_ref[...], g_ref[...]).astype(BF16)
    k0 = ATT_WIDTH
    mk0 = ATT_WIDTH + 2 * KV_WIDTH + MQK_WIDTH
    kvt = _dot_nt(w_ref[k0:k0 + 2 * KV_WIDTH, :], xn)
    kt_ref[...] = kvt[:KV_WIDTH]
    vt_ref[...] = kvt[KV_WIDTH:]
    mkt_ref[...] = _dot_nt(w_ref[mk0:mk0 + MQK_WIDTH, :], xn) * (MLSTM_DK ** -0.5)

    def mm(lo, hi):
        return _dot_nt(xn, w_ref[lo:hi, :])

    o = 0
    q_ref[...] = mm(o, o + ATT_WIDTH)
    o += ATT_WIDTH + 2 * KV_WIDTH
    mqk = mm(o, o + 2 * MQK_WIDTH)
    mq_ref[...] = mqk[:, :MQK_WIDTH]
    mk_ref[...] = mqk[:, MQK_WIDTH:] * (MLSTM_DK ** -0.5)
    o += 2 * MQK_WIDTH
    mv_ref[...] = mm(o, o + MLSTM_WIDTH)
    o += MLSTM_WIDTH
    og_ref[...] = mm(o, o + MLSTM_WIDTH)

    zc = _dot(xn, wgc_ref[...]) + bc_ref[...]
    lane = lax.broadcasted_iota(jnp.int32, zc.shape, 1)
    gc_ref[...] = jnp.where(lane < MLSTM_HEADS, zc, _log_sigmoid(zc))
    zr = _dot_nt(wgr_ref[...], xn) + br_ref[...]
    row = lax.broadcasted_iota(jnp.int32, zr.shape, 0)
    gr_ref[...] = jnp.where(row < MLSTM_HEADS, zr, _log_sigmoid(zr))


def _const_spec(shape):
    nd = len(shape)
    return pl.BlockSpec(shape, lambda *_: (0,) * nd, pipeline_mode=pl.Buffered(1))


def _weight_specs(w, names):
    return [_const_spec(w[n].shape) for n in names]


def _decode_proj(x, w, tm):
    t, d = x.shape

    def rows(width):
        return pl.BlockSpec((tm, width), lambda i: (i, 0))

    def lanes(height):
        return pl.BlockSpec((height, tm), lambda i: (0, i))

    outs = [(rows, ATT_WIDTH), (rows, MQK_WIDTH), (rows, MQK_WIDTH), (rows, MLSTM_WIDTH),
            (rows, MLSTM_WIDTH), (lanes, GATE_ROWS), (rows, LANES), (lanes, KV_WIDTH),
            (lanes, KV_WIDTH), (lanes, MQK_WIDTH)]
    names = ["g_mix", "w_main", "w_gate_rows", "b_gate_rows", "w_gate_cols", "b_gate_cols"]
    return pl.pallas_call(
        _decode_proj_kernel,
        grid=(t // tm,),
        in_specs=[rows(d)] + _weight_specs(w, names),
        out_specs=[kind(size) for kind, size in outs],
        out_shape=[jax.ShapeDtypeStruct((t, size) if kind is rows else (size, t), F32)
                   for kind, size in outs],
        compiler_params=pltpu.CompilerParams(
            dimension_semantics=("arbitrary",), vmem_limit_bytes=VMEM_LIMIT,
            allow_input_fusion=[False] + [True] * len(names)),
    )(x, *[w[n] for n in names])


def _dup_heads(x):
    swapped = pltpu.roll(x, HEAD_DIM, axis=1)
    lo = _low_half(x.shape)
    return (jnp.where(lo, x, swapped).astype(BF16), jnp.where(lo, swapped, x).astype(BF16))


def _stack_query_heads(q, h):
    parts = []
    for g in range(ATT_GROUP):
        hq = h * ATT_GROUP + g
        slab = q[:, (hq // 2) * LANES:(hq // 2 + 1) * LANES]
        lo = _low_half(slab.shape)
        keep = lo if hq % 2 == 0 else jnp.logical_not(lo)
        parts.append(jnp.where(keep, slab, jnp.zeros_like(slab)))
    return jnp.concatenate(parts, axis=0).astype(BF16)


def _sink_column(sink_ref, h, rows_per_head):
    n = ATT_GROUP * rows_per_head
    grp = lax.broadcasted_iota(jnp.int32, (n, 1), 0) >> _log2(rows_per_head)
    col = jnp.full((n, 1), sink_ref[h * ATT_GROUP], F32)
    for g in range(1, ATT_GROUP):
        col = jnp.where(grp == g, sink_ref[h * ATT_GROUP + g], col)
    return col


def _swa_prompt_stages(sink_ref, q_ref, kc_ref, kp_ref, vc_ref, vp_ref, first_block, store):
    qb = q_ref.shape[0]
    k_all = jnp.concatenate([kp_ref[...], kc_ref[...]], axis=0)
    v_all = jnp.concatenate([vp_ref[...], vc_ref[...]], axis=0)
    kd = _dup_heads(k_all)
    v_t = v_all.T.astype(BF16)

    n = ATT_GROUP * WINDOW
    key = lax.broadcasted_iota(jnp.int32, (2 * WINDOW, n), 0)
    qry = lax.broadcasted_iota(jnp.int32, (2 * WINDOW, n), 1) & (WINDOW - 1)
    delta = key - qry
    in_band = (delta >= 0) & (delta <= WINDOW)
    grp = lax.broadcasted_iota(jnp.int32, (1, n), 1) >> _log2(WINDOW)
    sinks = []
    for h in range(ATT_KV_HEADS):
        row = jnp.full((1, n), sink_ref[h * ATT_GROUP], F32)
        for g in range(1, ATT_GROUP):
            row = jnp.where(grp == g, sink_ref[h * ATT_GROUP + g], row)
        sinks.append(row)
    yield

    probs = []
    for j in range(qb // WINDOW):
        valid = in_band & (key >= jnp.where(first_block, WINDOW, 0)) if j == 0 else in_band
        q = q_ref[j * WINDOW:(j + 1) * WINDOW, :] * (HEAD_DIM ** -0.5)
        band = slice(j * WINDOW, (j + 2) * WINDOW)
        for h in range(ATT_KV_HEADS):
            qh = _stack_query_heads(q, h)
            s = _dot_nt(kd[h][band], qh)
            probs.append(dict(h=h, band=band, s=jnp.where(valid, s, NEG_INF)))
            yield
    for p in probs:
        p["m"] = jnp.maximum(jnp.max(p["s"], axis=0, keepdims=True), sinks[p["h"]])
        yield
    for p in probs:
        p["p"] = jnp.exp(p["s"] - p["m"])
        yield
    for p in probs:
        h = p["h"]
        denom = jnp.sum(p["p"], axis=0, keepdims=True) + jnp.exp(sinks[h] - p["m"])
        p["o_t"] = _dot(v_t[h * HEAD_DIM:(h + 1) * HEAD_DIM, p["band"]], p["p"].astype(BF16)) / denom
        yield
    for j in range(qb // WINDOW):
        pieces = [p["o_t"][:, g * WINDOW:(g + 1) * WINDOW]
                  for p in probs[j * ATT_KV_HEADS:(j + 1) * ATT_KV_HEADS] for g in range(ATT_GROUP)]
        store(j, jnp.concatenate(pieces, axis=0).T)
        yield


def _swa_sample_kernel(sink_ref, q_ref, knt_ref, vnt_ref, ck_ref, cv_ref,
                       o_ref, kbuf_ref, vbuf_ref, *, nb, tq):
    past = ck_ref.shape[1]
    hw = ATT_KV_HEADS * HEAD_DIM
    heads = ATT_KV_HEADS * ATT_GROUP
    n = heads * tq
    fresh = past - tq
    t = lax.broadcasted_iota(jnp.int32, (n, 2 * past), 0) & (tq - 1)
    col = lax.broadcasted_iota(jnp.int32, (n, 2 * past), 1)
    valid = ((col < past) & (col >= t)) | ((col >= past + fresh) & (col - (past + fresh) <= t))
    sink = jnp.concatenate([_sink_column(sink_ref, h, tq) for h in range(ATT_KV_HEADS)], axis=0)
    lane = lax.broadcasted_iota(jnp.int32, (hw, past), 1)
    knt = knt_ref[...]
    vnt = vnt_ref[...]
    zero_slab = jnp.zeros((tq, LANES), F32)
    lo = _low_half((tq, LANES))

    def shifted(old, new_t, b):
        return jnp.where(lane >= fresh, pltpu.roll(new_t, (fresh - b * tq) % past, axis=1),
                         pltpu.roll(old, fresh, axis=1))

    def dup_rows(x):
        h0, h1 = x[:HEAD_DIM], x[HEAD_DIM:]
        return jnp.concatenate([h0, h0, h1, h1], axis=0)

    for b in range(nb):
        k_old = ck_ref[b * hw:(b + 1) * hw, :]
        v_old = cv_ref[b * hw:(b + 1) * hw, :]
        k_new = shifted(k_old, knt, b)
        v_new = shifted(v_old, vnt, b)
        kbuf_ref[b * hw:(b + 1) * hw, :] = k_new
        vbuf_ref[b * hw:(b + 1) * hw, :] = v_new
        kop = jnp.concatenate([dup_rows(k_old), dup_rows(k_new)], axis=1).astype(BF16)
        vop = jnp.concatenate([dup_rows(v_old), dup_rows(v_new)], axis=1).astype(BF16)

        q = q_ref[b * tq:(b + 1) * tq, :]
        blocks = []
        for hq in range(heads):
            slab = q[:, (hq // 2) * LANES:(hq // 2 + 1) * LANES]
            slab = jnp.where(lo if hq % 2 == 0 else jnp.logical_not(lo), slab, 0.0)
            pair = [slab, zero_slab] if hq // ATT_GROUP == 0 else [zero_slab, slab]
            blocks.append(jnp.concatenate(pair, axis=1))
        qx = jnp.concatenate(blocks, axis=0).astype(BF16)

        s = _dot(qx, kop) * (HEAD_DIM ** -0.5)
        s = jnp.where(valid, s, NEG_INF)
        m = jnp.maximum(jnp.max(s, axis=-1, keepdims=True), sink)
        p = jnp.exp(s - m)
        denom = jnp.sum(p, axis=-1, keepdims=True) + jnp.exp(sink - m)
        o = _dot_nt(p.astype(BF16), vop) / denom

        slabs = []
        for j in range(heads // 2):
            h = (2 * j) // ATT_GROUP
            even = o[(2 * j) * tq:(2 * j + 1) * tq, h * LANES:(h + 1) * LANES]
            odd = o[(2 * j + 1) * tq:(2 * j + 2) * tq, h * LANES:(h + 1) * LANES]
            slabs.append(jnp.where(lo, even, odd))
        o_ref[b * tq:(b + 1) * tq, :] = jnp.concatenate(slabs, axis=1)


def _swa_sample(q, knt, vnt, cache_kt, cache_vt, sinks, nbatch, nb):
    t = q.shape[0]
    tq = t // nbatch
    past = cache_kt.shape[1]
    hw = ATT_KV_HEADS * HEAD_DIM
    assert past == WINDOW and tq & (tq - 1) == 0 and nb * tq == past

    def rows(r, width):
        return pl.BlockSpec((r, width), lambda i: (i, 0))

    new_t = pl.BlockSpec((hw, nb * tq), lambda i: (0, i))
    return pl.pallas_call(
        functools.partial(_swa_sample_kernel, nb=nb, tq=tq),
        grid=(nbatch // nb,),
        in_specs=[
            pl.BlockSpec(memory_space=pltpu.SMEM),
            rows(nb * tq, ATT_WIDTH), new_t, new_t, rows(nb * hw, past), rows(nb * hw, past),
        ],
        out_specs=[rows(nb * tq, ATT_WIDTH), rows(nb * hw, past), rows(nb * hw, past)],
        out_shape=[
            jax.ShapeDtypeStruct((t, ATT_WIDTH), F32),
            jax.ShapeDtypeStruct(cache_kt.shape, F32),
            jax.ShapeDtypeStruct(cache_vt.shape, F32),
        ],
        compiler_params=pltpu.CompilerParams(
            dimension_semantics=("arbitrary",), vmem_limit_bytes=VMEM_LIMIT),
    )(sinks, q, knt, vnt, cache_kt, cache_vt)


def _mlstm_intra(q, k, v, it_c, bt_c, it_r, bt_r, m_c, valid):
    d = jnp.where(valid, bt_c - bt_r + it_r, NEG_INF)
    inter = bt_c + m_c
    mt = jnp.maximum(jnp.max(d, axis=-1, keepdims=True), inter)
    sm = _dot_nt(q, k) * jnp.exp(d - mt)
    wi = jnp.exp(inter - mt)
    return _dot(sm.astype(BF16), v), jnp.sum(sm, axis=-1, keepdims=True), mt, wi


def _mlstm_head_out(num, den, mt, og, g_head):
    lower = jnp.maximum(jnp.abs(den), jnp.exp(-mt))
    h = num / lower
    hn = h * lax.rsqrt(jnp.mean(h * h, axis=-1, keepdims=True) + EPS)
    return hn * g_head * jax.nn.sigmoid(og)


def _mlstm_prompt_stages(mq, mk, mv, og, gr, gh_ref, hm_ref, c_ref, n_ref, m_ref, nbatch, chunk):
    s_idx = lax.broadcasted_iota(jnp.int32, (chunk, chunk), 0)
    l_idx = lax.broadcasted_iota(jnp.int32, (chunk, chunk), 1)
    causal = s_idx <= l_idx
    triu = jnp.where(causal, 1.0, 0.0).astype(BF16)
    lo = _low_half((chunk, LANES))
    pad_rows = jnp.zeros((LANES - MLSTM_HEADS, chunk), F32)
    sub8 = (GATE_ROWS, chunk)

    probs = []
    for b in range(nbatch):
        g = gr(b)
        btr = _cumsum_rows(g, triu)
        it_rows = g[:MLSTM_HEADS]
        bt_rows = btr[MLSTM_HEADS:2 * MLSTM_HEADS]
        u_cols = jnp.concatenate([it_rows - bt_rows, pad_rows], axis=0).T
        for h in range(MLSTM_HEADS):
            slab = slice((h // 2) * LANES, (h // 2 + 1) * LANES)
            keep = lo if h % 2 == 0 else jnp.logical_not(lo)
            q = mq(b, slab)
            k = mk(b, slab)
            probs.append(dict(
                b=b, h=h, idx=b * MLSTM_HEADS + h, cols=slice(h * MLSTM_DV, (h + 1) * MLSTM_DV),
                q=jnp.where(keep, q, jnp.zeros_like(q)), k=jnp.where(keep, k, jnp.zeros_like(k)),
                u_c=u_cols[:, h:h + 1], it_r=it_rows[h:h + 1, :], bt_r=bt_rows[h:h + 1, :]))
        yield

    for p in probs:
        p["m_old"] = m_ref[p["idx"]][:, 0:1]
        p["d"] = jnp.where(causal, p["bt_r"] + p["u_c"], NEG_INF)
        p["inter"] = p["bt_r"] + p["m_old"]
        p["q_t"] = p["q"].astype(F32).T.astype(BF16)
        p["kq"] = _dot(p["k"], p["q_t"])
        p["v_t"] = mv(p["b"], p["cols"]).astype(F32).T
        yield
    for p in probs:
        p["mt"] = jnp.maximum(jnp.max(p["d"], axis=0, keepdims=True), p["inter"])
        yield
    for p in probs:
        p["sm"] = p["kq"] * jnp.exp(p["d"] - p["mt"])
        p["wi"] = jnp.exp(p["inter"] - p["mt"])
        yield
    for p in probs:
        c_old = c_ref[p["idx"]]
        n_old = jnp.broadcast_to(n_ref[p["idx"]], (sub8[0], LANES))
        num = (_dot(p["v_t"].astype(BF16), p["sm"].astype(BF16))
               + p["wi"] * _dot(c_old.astype(BF16), p["q_t"]))
        den = (jnp.sum(p["sm"], axis=0, keepdims=True)
               + p["wi"] * _dot(n_old.astype(BF16), p["q_t"])[0:1])
        lower = jnp.maximum(jnp.abs(den), jnp.exp(-p["mt"]))
        h_t = num / lower
        p["hn"] = (h_t * lax.rsqrt(jnp.mean(h_t * h_t, axis=0, keepdims=True) + EPS)).T
        yield
    for p in probs:
        gate = jax.nn.sigmoid(og(p["b"], p["cols"]).astype(F32))
        hm_ref[p["b"], :, p["cols"]] = (p["hn"] * gh_ref[:, p["cols"]] * gate).astype(hm_ref.dtype)
        yield
    for p in probs:
        idx = p["idx"]
        b_last = p["bt_r"][:, chunk - 1:chunk]
        m_new = p["mt"][:, chunk - 1:chunk]
        wk = jnp.exp(b_last - p["bt_r"] + p["it_r"] - m_new)
        wc = jnp.exp(b_last + p["m_old"] - m_new)
        c_ref[idx] = wc * c_ref[idx] + _dot((p["v_t"] * wk).astype(BF16), p["k"])
        n_ref[idx] = wc * n_ref[idx] + _dot(jnp.broadcast_to(wk, sub8).astype(BF16), p["k"])[0:1]
        m_ref[idx] = jnp.broadcast_to(m_new, (1, LANES))
        yield


def _mixer_prompt_kernel(x_ref, g_ref, w_ref, wgr_ref, br_ref, gh_ref, gmem_ref, wck_ref, wcv_ref,
                         wg_ref, wu_ref, wd_ref, wo_ref, wcq_ref, wco_ref, mem_ref,
                         q_ref, k_ref, v_ref, hm_ref, ct_out, n_ref, m_ref, wgu_out, wd_out, wo_out,
                         wcq_out, wco_out, memk_out, memv_out, memkt_out, memvt_out, kwin_out, vwin_out,
                         mq_scr, mk_scr, mv_scr, og_scr, gr_scr, c_ref, wck_scr, wcv_scr,
                         *, nbatch, chunk):
    s = pl.program_id(0)
    wr = s % 2
    rd = 1 - wr
    rows = nbatch * chunk

    def side_work():
        for c in range(0, wg_ref.shape[1], FFN_CHUNK):
            wgu_out[:, 2 * c:2 * c + FFN_CHUNK] = wg_ref[:, c:c + FFN_CHUNK].astype(BF16)
            wgu_out[:, 2 * c + FFN_CHUNK:2 * (c + FFN_CHUNK)] = wu_ref[:, c:c + FFN_CHUNK].astype(BF16)
        wd_out[...] = wd_ref[...].astype(BF16)
        wo_out[...] = wo_ref[...].astype(BF16)
        wcq_out[...] = wcq_ref[...].astype(BF16)
        wco_out[...] = wco_ref[...].astype(BF16)
        mn = _rmsnorm(mem_ref[...], gmem_ref[...]).astype(BF16)
        mk = _dot(mn, wck_scr[...])
        mv = _dot(mn, wcv_scr[...])
        memk_out[...] = mk
        memv_out[...] = mv
        memkt_out[...] = mk.T
        memvt_out[...] = mv.T

    def proj_stages():
        xn = _rmsnorm(x_ref[...].reshape(rows, x_ref.shape[-1]), g_ref[...]).astype(BF16)
        yield

        def mm(lo, hi):
            return _dot_nt(xn, w_ref[lo:hi, :])

        o = 0
        q_ref[...] = mm(o, o + ATT_WIDTH).astype(q_ref.dtype).reshape(q_ref.shape)
        yield
        o += ATT_WIDTH
        kv = mm(o, o + 2 * KV_WIDTH)
        k_ref[...] = kv[:, :KV_WIDTH].reshape(k_ref.shape)
        v_ref[...] = kv[:, KV_WIDTH:].reshape(v_ref.shape)
        yield
        o += 2 * KV_WIDTH
        mqk = mm(o, o + 2 * MQK_WIDTH)
        mq_scr[wr] = mqk[:, :MQK_WIDTH].astype(BF16)
        mk_scr[wr] = (mqk[:, MQK_WIDTH:] * (MLSTM_DK ** -0.5)).astype(BF16)
        yield
        o += 2 * MQK_WIDTH
        mv_scr[wr] = mm(o, o + MLSTM_WIDTH).astype(BF16)
        yield
        o += MLSTM_WIDTH
        og_scr[wr] = mm(o, o + MLSTM_WIDTH).astype(BF16)
        yield
        zr = _dot_nt(wgr_ref[...], xn) + br_ref[...]
        row = lax.broadcasted_iota(jnp.int32, zr.shape, 0)
        gr_scr[wr] = jnp.where(row < MLSTM_HEADS, zr, _log_sigmoid(zr))
        yield

    def mlstm_stages():
        def rows_of(scr):
            return lambda b, lanes: scr[rd, b * chunk:(b + 1) * chunk, lanes]
        return _mlstm_prompt_stages(
            rows_of(mq_scr), rows_of(mk_scr), rows_of(mv_scr), rows_of(og_scr),
            lambda b: gr_scr[rd, :, b * chunk:(b + 1) * chunk],
            gh_ref, hm_ref, c_ref, n_ref, m_ref, nbatch, chunk)

    @pl.when(s == 0)
    def _():
        c_ref[...] = jnp.zeros_like(c_ref)
        n_ref[...] = jnp.zeros_like(n_ref)
        m_ref[...] = jnp.zeros_like(m_ref)
        wck_scr[...] = wck_ref[...].astype(BF16)
        wcv_scr[...] = wcv_ref[...].astype(BF16)
        side_work()
        for _ in proj_stages():
            pass

    @pl.when(s > 0)
    def _():
        side_work()
        mlstm = mlstm_stages()
        for _ in proj_stages():
            for _ in range(MLSTM_STAGES_PER_PROJ_STAGE):
                next(mlstm, None)
        for _ in mlstm:
            pass

    @pl.when(s == pl.num_programs(0) - 2)
    def _():
        for b in range(nbatch):
            kwin_out[b] = k_ref[b, chunk - WINDOW:, :].T
            vwin_out[b] = v_ref[b, chunk - WINDOW:, :].T

    @pl.when(s == pl.num_programs(0) - 1)
    def _():
        for i in range(c_ref.shape[0]):
            half = (i % MLSTM_HEADS) % 2
            ct_out[i] = c_ref[i].T[half * MLSTM_DK:(half + 1) * MLSTM_DK, :]


def _mixer_prompt(x, mem, w, w_gate, w_up, w_down, w_out, w_cq, w_co, nbatch, chunk):
    t, d = x.shape
    seq = t // nbatch
    nchunks = seq // chunk
    nstate = nbatch * MLSTM_HEADS
    rows = nbatch * chunk
    hidden = w_down.shape[0]
    assert hidden % FFN_CHUNK == 0
    assert d % (nchunks * BF16_SUBLANES) == 0 and hidden % (nchunks * BF16_SUBLANES) == 0
    assert w_co.shape[0] % (nchunks * BF16_SUBLANES) == 0
    mem_tokens = mem.shape[0] // nbatch
    blocks_per_batch = mem_tokens // LANES
    nblocks = nbatch * blocks_per_batch
    steps_per_block = nchunks // nblocks
    assert mem_tokens % LANES == 0 and nchunks % nblocks == 0 and chunk >= WINDOW

    def mem_block(s):
        return jnp.minimum(s // steps_per_block, nblocks - 1)

    mem_rows = pl.BlockSpec((LANES, d), lambda s: (mem_block(s), 0))
    mem_kv = pl.BlockSpec((LANES, CROSS_WIDTH), lambda s: (mem_block(s), 0))
    mem_kv_t = pl.BlockSpec((CROSS_WIDTH, LANES),
                            lambda s: (mem_block(s) // blocks_per_batch, mem_block(s) % blocks_per_batch))
    window_t = pl.BlockSpec((nbatch, KV_WIDTH, WINDOW), lambda s: (0, 0, 0))

    def proj_chunk(width):
        return pl.BlockSpec((nbatch, chunk, width), lambda s: (0, jnp.minimum(s, nchunks - 1), 0))

    def state(shape):
        return pl.BlockSpec(shape, lambda s: (0, 0, 0))

    def weight_rows(a, width=None):
        return pl.BlockSpec((a.shape[0] // nchunks, width or a.shape[1]),
                            lambda s: (jnp.minimum(s, nchunks - 1), 0))

    names = ["g_mix", "w_main", "w_gate_rows", "b_gate_rows", "g_head", "g_mem", "w_ck", "w_cv"]
    (q, k, v, hm, ct_st, n_st, m_st, w_gate_up_bf, w_down_bf, w_out_bf, w_cq_bf, w_co_bf, memk, memv,
     memk_t, memv_t, kwin_t, vwin_t) = pl.pallas_call(
        functools.partial(_mixer_prompt_kernel, nbatch=nbatch, chunk=chunk),
        grid=(nchunks + 1,),
        in_specs=[proj_chunk(d)] + _weight_specs(w, names) + [
            weight_rows(w_gate), weight_rows(w_up), weight_rows(w_down), weight_rows(w_out),
            weight_rows(w_cq), weight_rows(w_co), mem_rows,
        ],
        out_specs=[
            proj_chunk(ATT_WIDTH), proj_chunk(KV_WIDTH), proj_chunk(KV_WIDTH),
            pl.BlockSpec((nbatch, chunk, MLSTM_WIDTH), lambda s: (0, jnp.maximum(s - 1, 0), 0)),
            state((nstate, MLSTM_DK, MLSTM_DV)), state((nstate, 1, LANES)), state((nstate, 1, LANES)),
            weight_rows(w_gate, 2 * hidden), weight_rows(w_down), weight_rows(w_out),
            weight_rows(w_cq), weight_rows(w_co), mem_kv, mem_kv, mem_kv_t, mem_kv_t, window_t, window_t,
        ],
        out_shape=[
            jax.ShapeDtypeStruct((nbatch, seq, ATT_WIDTH), BF16),
            jax.ShapeDtypeStruct((nbatch, seq, KV_WIDTH), F32),
            jax.ShapeDtypeStruct((nbatch, seq, KV_WIDTH), F32),
            jax.ShapeDtypeStruct((nbatch, seq, MLSTM_WIDTH), BF16),
            jax.ShapeDtypeStruct((nstate, MLSTM_DK, MLSTM_DV), F32),
            jax.ShapeDtypeStruct((nstate, 1, LANES), F32),
            jax.ShapeDtypeStruct((nstate, 1, LANES), F32),
            jax.ShapeDtypeStruct((d, 2 * hidden), BF16),
            jax.ShapeDtypeStruct((hidden, d), BF16),
            jax.ShapeDtypeStruct(w_out.shape, BF16),
            jax.ShapeDtypeStruct(w_cq.shape, BF16),
            jax.ShapeDtypeStruct(w_co.shape, BF16),
            jax.ShapeDtypeStruct((mem.shape[0], CROSS_WIDTH), F32),
            jax.ShapeDtypeStruct((mem.shape[0], CROSS_WIDTH), F32),
            jax.ShapeDtypeStruct((nbatch * CROSS_WIDTH, mem_tokens), F32),
            jax.ShapeDtypeStruct((nbatch * CROSS_WIDTH, mem_tokens), F32),
            jax.ShapeDtypeStruct((nbatch, KV_WIDTH, WINDOW), F32),
            jax.ShapeDtypeStruct((nbatch, KV_WIDTH, WINDOW), F32),
        ],
        scratch_shapes=[
            pltpu.VMEM((2, rows, MQK_WIDTH), BF16), pltpu.VMEM((2, rows, MQK_WIDTH), BF16),
            pltpu.VMEM((2, rows, MLSTM_WIDTH), BF16), pltpu.VMEM((2, rows, MLSTM_WIDTH), BF16),
            pltpu.VMEM((2, GATE_ROWS, rows), F32),
            pltpu.VMEM((nstate, MLSTM_DV, LANES), F32),
            pltpu.VMEM(w["w_ck"].shape, BF16), pltpu.VMEM(w["w_cv"].shape, BF16),
        ],
        compiler_params=pltpu.CompilerParams(
            dimension_semantics=("arbitrary",), vmem_limit_bytes=VMEM_LIMIT,
            allow_input_fusion=[False] + [True] * len(names) + [False] * 7),
    )(x.reshape(nbatch, seq, d), *[w[n] for n in names], w_gate, w_up, w_down, w_out, w_cq, w_co, mem)
    q = q.reshape(t, ATT_WIDTH)
    k = k.reshape(t, KV_WIDTH)
    v = v.reshape(t, KV_WIDTH)

    c_fin = jnp.swapaxes(ct_st.reshape(nbatch, MLSTM_HEADS, MLSTM_DK, MLSTM_DV), 2, 3)
    n_st = n_st.reshape(nbatch, MLSTM_HEADS, 2, MLSTM_DK)
    n_fin = jnp.stack([n_st[:, h, h % 2, :] for h in range(MLSTM_HEADS)], axis=1)
    m_fin = m_st[:, 0, 0].reshape(nbatch, MLSTM_HEADS)
    bf16_weights = {"w_gate_up": w_gate_up_bf, "w_down": w_down_bf, "w_out": w_out_bf,
                    "w_cq": w_cq_bf, "w_co": w_co_bf}
    return (q, k, v, hm.reshape(t, MLSTM_WIDTH), c_fin, n_fin, m_fin, bf16_weights, memk, memv,
            memk_t, memv_t, kwin_t, vwin_t)


def _mlstm_sample_kernel(mq_ref, mk_ref, mkt_ref, mv_ref, og_ref, gc_ref, gr_ref, mrow_ref, c_ref,
                         n_ref, gh_ref, hm_ref, co_ref, no_ref, mt_ref, *, nb, tq):
    rows = nb * tq
    shift = tq.bit_length() - 1
    r = lax.broadcasted_iota(jnp.int32, (rows, rows), 0)
    s = lax.broadcasted_iota(jnp.int32, (rows, rows), 1)
    same = (r >> shift) == (s >> shift)
    valid = same & (s <= r)
    tril = jnp.where(valid, 1.0, 0.0).astype(BF16)
    triu = jnp.where(same & (r <= s), 1.0, 0.0).astype(BF16)
    row_batch = lax.broadcasted_iota(jnp.int32, (rows, 1), 0) >> shift
    lane = lax.broadcasted_iota(jnp.int32, (rows, LANES), 1)

    gc = gc_ref[...]
    gr = gr_ref[...]
    btc = _cumsum_cols(tril, gc)
    btr = _cumsum_rows(gr, triu)
    m_rows = mrow_ref[...]
    mt_all = jnp.zeros((rows, LANES), F32)

    def last_of_batch(col):
        parts = [jnp.broadcast_to(col[(b + 1) * tq - 1:(b + 1) * tq], (tq, 1)) for b in range(nb)]
        return jnp.concatenate(parts, axis=0)

    dk = MLSTM_DK
    lo = _low_half((rows, LANES))
    own_q = row_batch == (lax.broadcasted_iota(jnp.int32, (rows, nb * dk), 1) >> _log2(dk))
    lane_batch = lax.broadcasted_iota(jnp.int32, (dk, rows), 1) >> shift

    def state_rows(b, h):
        return slice((b * MLSTM_HEADS + h) * dk, (b * MLSTM_HEADS + h + 1) * dk)

    for h in range(MLSTM_HEADS):
        slab = slice((h // 2) * LANES, (h // 2 + 1) * LANES)
        qs, ks = mq_ref[:, slab], mk_ref[:, slab]
        qsw, ksw = pltpu.roll(qs, dk, axis=1), pltpu.roll(ks, dk, axis=1)
        q2 = jnp.where(lo, qs, qsw) if h % 2 == 0 else jnp.where(lo, qsw, qs)
        k2 = jnp.where(lo, ks, ksw) if h % 2 == 0 else jnp.where(lo, ksw, ks)
        qb, kb = q2[:, :dk].astype(BF16), k2[:, :dk].astype(BF16)
        vb = mv_ref[:, h * MLSTM_DV:(h + 1) * MLSTM_DV].astype(BF16)
        it_c = gc[:, h:h + 1]
        bt_c = btc[:, MLSTM_HEADS + h:MLSTM_HEADS + h + 1]
        it_r = gr[h:h + 1, :]
        bt_r = btr[MLSTM_HEADS + h:MLSTM_HEADS + h + 1, :]
        m_c = m_rows[:, h:h + 1]

        num, ssum, mt, wi = _mlstm_intra(qb, kb, vb, it_c, bt_c, it_r, bt_r, m_c, valid)
        qx = jnp.where(own_q, jnp.concatenate([q2] * (nb // 2), axis=1), 0.0).astype(BF16)
        c_stack = jnp.concatenate([c_ref[state_rows(b, h), :] for b in range(nb)], axis=0)
        qc = _dot(qx, c_stack.astype(BF16))
        n_rows = jnp.concatenate(
            [jnp.broadcast_to(n_ref[b * MLSTM_HEADS + h], (tq, dk)) for b in range(nb)], axis=0)
        num = num + wi * qc
        den = ssum + wi * jnp.sum(qb.astype(F32) * n_rows, axis=-1, keepdims=True)
        og = og_ref[:, h * MLSTM_DV:(h + 1) * MLSTM_DV]
        g_head = gh_ref[:, h * MLSTM_DV:(h + 1) * MLSTM_DV]
        hm_ref[:, h * MLSTM_DV:(h + 1) * MLSTM_DV] = _mlstm_head_out(num, den, mt, og, g_head)

        b_last = last_of_batch(bt_c)
        m_new = last_of_batch(mt)
        wk = jnp.exp(b_last - bt_c + it_c - m_new)
        wc = jnp.exp(b_last + m_c - m_new)
        wkv = (wk * vb.astype(F32)).astype(BF16)
        wkk = wk * kb.astype(F32)
        kt = mkt_ref[h * dk:(h + 1) * dk, :]
        kx = jnp.concatenate([jnp.where(lane_batch == b, kt, 0.0) for b in range(nb)], axis=0)
        upd = _dot(kx.astype(BF16), wkv)
        for b in range(nb):
            idx = b * MLSTM_HEADS + h
            wc_b = wc[b * tq:b * tq + 1]
            co_ref[state_rows(b, h), :] = wc_b * c_ref[state_rows(b, h), :] + upd[b * dk:(b + 1) * dk]
            no_ref[idx] = wc_b * n_ref[idx] + jnp.sum(wkk[b * tq:(b + 1) * tq], axis=0, keepdims=True)
        mt_all = jnp.where(lane == h, mt, mt_all)
    mt_ref[...] = mt_all


def _mlstm_sample(mq, mk, mkt, mv, og, gcol, grow, m_rows, ct_in, n_in, g_head, nbatch, nb):
    t = mq.shape[0]
    tq = t // nbatch
    rows = nb * tq
    assert tq & (tq - 1) == 0 and rows == LANES and nb % 2 == 0

    def tok(width):
        return pl.BlockSpec((rows, width), lambda i: (i, 0))

    ct_spec = pl.BlockSpec((nb * MLSTM_HEADS * MLSTM_DK, MLSTM_DV), lambda i: (i, 0))
    n_spec = pl.BlockSpec((nb * MLSTM_HEADS, 1, MLSTM_DK), lambda i: (i, 0, 0))
    return pl.pallas_call(
        functools.partial(_mlstm_sample_kernel, nb=nb, tq=tq),
        grid=(nbatch // nb,),
        in_specs=[
            tok(MQK_WIDTH), tok(MQK_WIDTH),
            pl.BlockSpec((MQK_WIDTH, rows), lambda i: (0, i)),
            tok(MLSTM_WIDTH), tok(MLSTM_WIDTH), tok(LANES),
            pl.BlockSpec((GATE_ROWS, rows), lambda i: (0, i)),
            tok(LANES),
            ct_spec, n_spec,
            _const_spec((1, MLSTM_WIDTH)),
        ],
        out_specs=[tok(MLSTM_WIDTH), ct_spec, n_spec, tok(LANES)],
        out_shape=[
            jax.ShapeDtypeStruct((t, MLSTM_WIDTH), F32),
            jax.ShapeDtypeStruct(ct_in.shape, F32),
            jax.ShapeDtypeStruct(n_in.shape, F32),
            jax.ShapeDtypeStruct((t, LANES), F32),
        ],
        compiler_params=pltpu.CompilerParams(
            dimension_semantics=("arbitrary",), vmem_limit_bytes=VMEM_LIMIT,
            allow_input_fusion=[False] * 7 + [True, False, True, False]),
    )(mq, mk, mkt, mv, og, gcol, grow, m_rows, ct_in, n_in, g_head)


def _mix_stage(x, att, hm, wo_ref, gcr_ref, wcq_ref):
    cat = jnp.concatenate([att.astype(BF16), hm.astype(BF16)], axis=1)
    y1 = x + _dot(cat, wo_ref[...])
    qc = _dot(_rmsnorm(y1, gcr_ref[...]).astype(BF16), wcq_ref[...])
    return y1, qc


def _cross_attention(qc, mk, mv):
    slabs = []
    for j in range(CROSS_WIDTH // LANES):
        qs = qc[:, j * LANES:(j + 1) * LANES]
        ks = mk[:, j * LANES:(j + 1) * LANES]
        vs = mv[:, j * LANES:(j + 1) * LANES]
        n = qs.shape[0]
        lo = _low_half(qs.shape)
        zero = jnp.zeros_like(qs)
        q2 = jnp.concatenate([jnp.where(lo, qs, zero), jnp.where(lo, zero, qs)], axis=0)
        s = _dot_nt(q2 * (HEAD_DIM ** -0.5), ks)
        p = jnp.exp(s - jnp.max(s, axis=-1, keepdims=True))
        o2 = _dot(p.astype(BF16), vs) / jnp.sum(p, axis=-1, keepdims=True)
        slabs.append(jnp.where(lo, o2[:n], o2[n:]))
    return jnp.concatenate(slabs, axis=1)


FFN_CHUNK = 256


def _finish(y3, gfin_ref, final):
    return _rmsnorm(y3, gfin_ref[...]) if final else y3


def _ffn_stage(y1, o, wco_ref, gf_ref, wgu_ref, wd_ref, act_ref, between=None):
    y2 = y1 + _dot(o.astype(BF16), wco_ref[...])
    xn = _rmsnorm(y2, gf_ref[...]).astype(BF16)
    for c in range(wd_ref.shape[0] // FFN_CHUNK):
        gu = _dot(xn, wgu_ref[:, 2 * c * FFN_CHUNK:2 * (c + 1) * FFN_CHUNK])
        g, u = gu[:, :FFN_CHUNK], gu[:, FFN_CHUNK:]
        act_ref[:, c * FFN_CHUNK:(c + 1) * FFN_CHUNK] = (g * jax.nn.sigmoid(g) * u).astype(BF16)
        if between is not None:
            between()
    return y2 + _dot(act_ref[...], wd_ref[...])


def _post_kernel(sink_ref, q_ref, kc_ref, kp_ref, vc_ref, vp_ref, x_ref, hm_ref, mk_ref, mv_ref,
                 xd_ref, attd_ref, hmd_ref, mkd_ref, mvd_ref,
                 wo_ref, gcr_ref, wcq_ref, wco_ref, gf_ref, wgu_ref, wd_ref, gfin_ref,
                 o_ref, od_ref, act_ref, att_ref, y3_ref, *, final, ntiles, tiles_per_seq, tq):
    tm = x_ref.shape[0]
    s = pl.program_id(0)
    tile = jnp.minimum(s, ntiles - 1)
    first_block = (tile % tiles_per_seq) == 0
    slot = s % 2

    def swa_stages():
        def store(j, value):
            att_ref[slot, j * WINDOW:(j + 1) * WINDOW, :] = value.astype(att_ref.dtype)
        return _swa_prompt_stages(sink_ref, q_ref, kc_ref, kp_ref, vc_ref, vp_ref, first_block, store)

    def finish():
        done = _finish(y3_ref[...], gfin_ref, final)
        o_ref[...] = done[:tm]
        od_ref[...] = done[tm:]

    @pl.when(s == 0)
    def _():
        y3_ref[...] = jnp.zeros_like(y3_ref)
        for _ in swa_stages():
            pass

    @pl.when(jnp.logical_and(s > 0, s <= ntiles))
    def _():
        swa = swa_stages()

        def advance(n=SWA_STAGES_PER_FFN_CHUNK):
            for _ in range(n):
                next(swa, None)

        finish()
        x = jnp.concatenate([x_ref[...], xd_ref[...]], axis=0)
        att = jnp.concatenate([att_ref[1 - slot], attd_ref[...].astype(BF16)], axis=0)
        hm = jnp.concatenate([hm_ref[...], hmd_ref[...].astype(BF16)], axis=0)
        y1, qc = _mix_stage(x, att, hm, wo_ref, gcr_ref, wcq_ref)
        advance()
        o = jnp.concatenate([
            _cross_attention(qc[:tm].astype(BF16), mk_ref[...].astype(BF16), mv_ref[...].astype(BF16)),
            _cross_attention_decode(qc[tm:], mkd_ref, mvd_ref, tq)], axis=0)
        advance()
        y3_ref[...] = _ffn_stage(y1, o, wco_ref, gf_ref, wgu_ref, wd_ref, act_ref, between=advance)
        for _ in swa:
            pass

    @pl.when(s > ntiles)
    def _():
        finish()


_MIX_WEIGHTS = ("w_out", "g_cross", "w_cq")
_FFN_WEIGHTS = ("w_co", "g_ffn", "w_gate_up", "w_down", "g_final")

def _post(x, q, k, v, hm, memk, memv, xd, attd, hmd, memkd_t, memvd_t, sinks, w, nbatch, nseq, final, tm):
    t, d = x.shape
    ntiles = t // tm
    tpb = ntiles // nbatch
    wpt = tm // WINDOW
    m = memk.shape[0] // nbatch
    hidden = w["w_down"].shape[0]
    td = xd.shape[0]
    tq = td // nseq
    slab = td // ntiles
    assert td % ntiles == 0 and slab % tq == 0 and slab % BF16_SUBLANES == 0
    mem_d = memkd_t.shape[1]

    def attn_tile(s):
        return jnp.minimum(s, ntiles - 1)

    def post_tile(s):
        return jnp.clip(s - 1, 0, ntiles - 1)

    def done_tile(s):
        return jnp.maximum(s - 2, 0)

    def attn_rows(width):
        return pl.BlockSpec((tm, width), lambda s: (attn_tile(s), 0))

    def prev_window(width):
        return pl.BlockSpec((WINDOW, width), lambda s: (jnp.maximum(attn_tile(s) * wpt - 1, 0), 0))

    def post_rows(width):
        return pl.BlockSpec((tm, width), lambda s: (post_tile(s), 0))

    def post_slab(width):
        return pl.BlockSpec((slab, width), lambda s: (post_tile(s), 0))

    mem_spec = pl.BlockSpec((m, CROSS_WIDTH), lambda s: (post_tile(s) // tpb, 0))
    memd_spec = pl.BlockSpec((slab // tq * CROSS_WIDTH, mem_d), lambda s: (post_tile(s), 0))
    names = _MIX_WEIGHTS + _FFN_WEIGHTS
    return pl.pallas_call(
        functools.partial(_post_kernel, final=final, ntiles=ntiles, tiles_per_seq=tpb, tq=tq),
        grid=(ntiles + 2,),
        in_specs=[
            pl.BlockSpec(memory_space=pltpu.SMEM),
            attn_rows(ATT_WIDTH), attn_rows(KV_WIDTH), prev_window(KV_WIDTH),
            attn_rows(KV_WIDTH), prev_window(KV_WIDTH),
            post_rows(d), post_rows(MLSTM_WIDTH), mem_spec, mem_spec,
            post_slab(d), post_slab(ATT_WIDTH), post_slab(MLSTM_WIDTH), memd_spec, memd_spec,
        ] + _weight_specs(w, names),
        out_specs=[pl.BlockSpec((tm, d), lambda s: (done_tile(s), 0)),
                   pl.BlockSpec((slab, d), lambda s: (done_tile(s), 0))],
        out_shape=[jax.ShapeDtypeStruct((t, d), F32), jax.ShapeDtypeStruct((td, d), F32)],
        scratch_shapes=[pltpu.VMEM((tm + slab, hidden), BF16), pltpu.VMEM((2, tm, ATT_WIDTH), BF16),
                        pltpu.VMEM((tm + slab, d), F32)],
        compiler_params=pltpu.CompilerParams(
            dimension_semantics=("arbitrary",), vmem_limit_bytes=VMEM_LIMIT),
    )(sinks, q, k, k, v, v, x, hm, memk, memv, xd, attd, hmd, memkd_t, memvd_t,
      *[w[n] for n in names])


def _cross_attention_decode(qc, mk_ref, mv_ref, tq):
    n = CROSS_HEADS * tq
    row_head = lax.broadcasted_iota(jnp.int32, (n, CROSS_WIDTH), 0) >> _log2(tq)
    lane_head = lax.broadcasted_iota(jnp.int32, (n, CROSS_WIDTH), 1) >> _log2(HEAD_DIM)
    own = row_head == lane_head
    outs = []
    for b in range(qc.shape[0] // tq):
        q = qc[b * tq:(b + 1) * tq, :]
        qx = jnp.where(own, jnp.concatenate([q] * CROSS_HEADS, axis=0), 0.0).astype(BF16)
        mk = mk_ref[b * CROSS_WIDTH:(b + 1) * CROSS_WIDTH, :].astype(BF16)
        mv = mv_ref[b * CROSS_WIDTH:(b + 1) * CROSS_WIDTH, :].astype(BF16)
        s = _dot(qx, mk) * (HEAD_DIM ** -0.5)
        p = jnp.exp(s - jnp.max(s, axis=-1, keepdims=True))
        pv = _dot_nt(p.astype(BF16), mv) / jnp.sum(p, axis=-1, keepdims=True)
        pv = jnp.where(own, pv, 0.0)
        out = pv[0:tq]
        for h in range(1, CROSS_HEADS):
            out = out + pv[h * tq:(h + 1) * tq]
        outs.append(out)
    return jnp.concatenate(outs, axis=0)


def _layer_weights(l, w_in, b_igate, b_fgate, g_mlstm_head, g_mix, g_cross, g_mem, w_ck, w_cv,
                   g_ffn, g_final):
    gate_w = w_in[l][:, MAIN_WIDTH:]
    gate_b = jnp.concatenate([b_igate[l], b_fgate[l]]).astype(F32)
    ngate = gate_b.shape[0]
    row = lambda a: a.astype(F32).reshape(1, -1)
    return {
        "w_main": w_in[l].T[:MAIN_WIDTH].astype(BF16),
        "w_gate_cols": jnp.pad(gate_w, ((0, 0), (0, LANES - ngate))).astype(BF16),
        "w_gate_rows": jnp.pad(gate_w.T, ((0, GATE_ROWS - ngate), (0, 0))).astype(BF16),
        "b_gate_cols": jnp.pad(gate_b, (0, LANES - ngate)).reshape(1, LANES),
        "b_gate_rows": jnp.pad(gate_b, (0, GATE_ROWS - ngate)).reshape(GATE_ROWS, 1),
        "g_mix": row(g_mix[l]), "g_cross": row(g_cross[l]), "g_mem": row(g_mem[l]),
        "g_ffn": row(g_ffn[l]), "g_final": row(g_final), "g_head": row(g_mlstm_head[l]),
        "w_ck": w_ck[l].astype(F32), "w_cv": w_cv[l].astype(F32),
    }


TOKEN_TILE = 512
MLSTM_STAGES_PER_PROJ_STAGE = 20
SWA_STAGES_PER_FFN_CHUNK = 3
MLSTM_CHUNK_ROWS = 256
SWA_SAMPLE_BATCH = 16
MLSTM_SAMPLE_BATCH = 16


def kernel(x_prompt, x_sample, mem_prompt, cache_swa_k, cache_swa_v, state_mlstm_C, state_mlstm_n,
           state_mlstm_m, cache_mem_k, cache_mem_v, w_in, b_igate, b_fgate, attn_sinks,
           g_mlstm_head, w_out, g_mix, g_cross, g_mem, w_cq, w_ck, w_cv, w_co, g_ffn, w_gate,
           w_up, w_down, g_final):
    depth = w_in.shape[0]
    bp, sp, d = x_prompt.shape
    bs, ss, _ = x_sample.shape
    mem_tokens = mem_prompt.shape[1]
    past = cache_swa_k.shape[2]
    yp = x_prompt.reshape(bp * sp, d)
    ys = x_sample.reshape(bs * ss, d)
    mem = mem_prompt.reshape(bp * mem_tokens, d)
    outs = [[] for _ in range(12)]

    for l in range(depth):
        final = l == depth - 1
        w = _layer_weights(l, w_in, b_igate, b_fgate, g_mlstm_head, g_mix, g_cross, g_mem,
                           w_ck, w_cv, g_ffn, g_final)
        sinks = attn_sinks[l].astype(F32)

        q, k, v, hm, c_p, n_p, m_p, cast, memk, memv, memk_t, memv_t, kwin_t, vwin_t = _mixer_prompt(
            yp, mem, w, w_gate[l].astype(F32), w_up[l].astype(F32), w_down[l].astype(F32),
            w_out[l].astype(F32), w_cq[l].astype(F32), w_co[l].astype(F32), bp, MLSTM_CHUNK_ROWS)
        w.update(cast)

        def tokens_second(a, heads, tokens):
            return jnp.transpose(a.reshape(bp, heads, HEAD_DIM, tokens), (0, 3, 1, 2))

        outs[0].append(tokens_second(kwin_t, ATT_KV_HEADS, WINDOW))
        outs[1].append(tokens_second(vwin_t, ATT_KV_HEADS, WINDOW))
        outs[2].append(c_p)
        outs[3].append(n_p)
        outs[4].append(m_p)
        outs[5].append(tokens_second(memk_t, CROSS_HEADS, mem_tokens))
        outs[6].append(tokens_second(memv_t, CROSS_HEADS, mem_tokens))

        def tokens_last(a):
            return jnp.transpose(a.astype(F32), (0, 2, 3, 1)).reshape(-1, a.shape[1])

        qd, mq, mk, mv, og, grow, gcol, knt, vnt, mkt = _decode_proj(ys, w, TOKEN_TILE)
        att_d, kbuf_t, vbuf_t = _swa_sample(
            qd, knt, vnt, tokens_last(cache_swa_k[l]), tokens_last(cache_swa_v[l]), sinks, bs,
            SWA_SAMPLE_BATCH)
        m_rows = jnp.pad(jnp.repeat(state_mlstm_m[l].astype(F32), ss, axis=0),
                         ((0, 0), (0, LANES - MLSTM_HEADS)))
        ct_in = jnp.swapaxes(state_mlstm_C[l].astype(F32), 2, 3).reshape(-1, MLSTM_DV)
        hm_d, ct_s, n_s, mt = _mlstm_sample(
            mq, mk, mkt, mv, og, gcol, grow, m_rows, ct_in,
            state_mlstm_n[l].astype(F32).reshape(bs * MLSTM_HEADS, 1, MLSTM_DK),
            w["g_head"], bs, MLSTM_SAMPLE_BATCH)

        yp, ys = _post(yp, q, k, v, hm, memk, memv, ys, att_d, hm_d, tokens_last(cache_mem_k[l]),
                       tokens_last(cache_mem_v[l]), sinks, w, bp, bs, final, TOKEN_TILE)
        buf_t_shape = (bs, ATT_KV_HEADS, HEAD_DIM, past)
        outs[7].append(jnp.transpose(kbuf_t.reshape(buf_t_shape), (0, 3, 1, 2)))
        outs[8].append(jnp.transpose(vbuf_t.reshape(buf_t_shape), (0, 3, 1, 2)))
        outs[9].append(jnp.swapaxes(ct_s.reshape(bs, MLSTM_HEADS, MLSTM_DK, MLSTM_DV), 2, 3))
        outs[10].append(n_s.reshape(bs, MLSTM_HEADS, MLSTM_DK))
        outs[11].append(mt.reshape(bs, ss, LANES)[:, ss - 1, :MLSTM_HEADS])

    return (yp.reshape(bp, sp, d), ys.reshape(bs, ss, d)) + tuple(jnp.stack(o) for o in outs)
```

```python
import functools

import jax
import jax.numpy as jnp
from jax import lax
from jax.experimental import pallas as pl
from jax.experimental.pallas import tpu as pltpu

F32 = jnp.float32
BF16 = jnp.bfloat16

HEAD_DIM = 64
ATT_GROUP = 4
ATT_KV_HEADS = 2
ATT_WIDTH = 512
KV_WIDTH = ATT_KV_HEADS * HEAD_DIM
WINDOW = 128
MLSTM_HEADS = 4
MLSTM_DV = 128
MLSTM_DK = 64
MQK_WIDTH = MLSTM_HEADS * MLSTM_DK
MLSTM_WIDTH = MLSTM_HEADS * MLSTM_DV
MAIN_WIDTH = ATT_WIDTH + 2 * KV_WIDTH + 2 * MQK_WIDTH + 2 * MLSTM_WIDTH
CROSS_HEADS = 4
CROSS_WIDTH = CROSS_HEADS * HEAD_DIM
EPS = 1e-6
NEG_INF = float("-inf")
assert HEAD_DIM == 4 ** 3

LANES = 128
BF16_SUBLANES = 16
GATE_ROWS = BF16_SUBLANES
VMEM_LIMIT = 56 * 1024 * 1024

NT = (((1,), (1,)), ((), ()))


def _dot(a, b):
    return jnp.dot(a, b, preferred_element_type=F32)


def _dot_nt(a, b):
    return lax.dot_general(a, b, NT, preferred_element_type=F32)


def _rmsnorm(x, g):
    return x * lax.rsqrt(jnp.mean(x * x, axis=-1, keepdims=True) + EPS) * g


def _log_sigmoid(x):
    return jnp.minimum(x, 0.0) - jnp.log1p(jnp.exp(-jnp.abs(x)))


def _split3(x):
    hi = x.astype(BF16)
    r1 = x - hi.astype(F32)
    mid = r1.astype(BF16)
    lo = (r1 - mid.astype(F32)).astype(BF16)
    return hi, mid, lo


def _cumsum_cols(tri, x):
    hi, mid, lo = _split3(x)
    return _dot(tri, hi) + _dot(tri, mid) + _dot(tri, lo)


def _cumsum_rows(x, tri):
    hi, mid, lo = _split3(x)
    return _dot(hi, tri) + _dot(mid, tri) + _dot(lo, tri)


def _log2(n):
    assert n > 0 and n & (n - 1) == 0, n
    return n.bit_length() - 1


def _low_half(shape):
    return lax.broadcasted_iota(jnp.int32, shape, 1) < HEAD_DIM


def _decode_proj_kernel(x_ref, g_ref, w_ref, wgr_ref, br_ref, wgc_ref, bc_ref, wt_ref,
                        q_ref, mq_ref, mk_ref, mv_ref, og_ref, gr_ref, gc_ref, kt_ref, vt_ref, mkt_ref):
    xn = _rmsnorm(x_ref[...], g_ref[...]).astype(BF16)
    zt = _dot_nt(wt_ref[...], xn)
    kt_ref[...] = zt[:KV_WIDTH]
    vt_ref[...] = zt[KV_WIDTH:2 * KV_WIDTH]
    mkt_ref[...] = zt[2 * KV_WIDTH:] * (MLSTM_DK ** -0.5)

    def mm(lo, hi):
        return _dot_nt(xn, w_ref[lo:hi, :])

    o = 0
    q_ref[...] = mm(o, o + ATT_WIDTH)
    o += ATT_WIDTH + 2 * KV_WIDTH
    mqk = mm(o, o + 2 * MQK_WIDTH)
    mq_ref[...] = mqk[:, :MQK_WIDTH]
    mk_ref[...] = mqk[:, MQK_WIDTH:] * (MLSTM_DK ** -0.5)
    o += 2 * MQK_WIDTH
    mv_ref[...] = mm(o, o + MLSTM_WIDTH)
    o += MLSTM_WIDTH
    og_ref[...] = mm(o, o + MLSTM_WIDTH)

    zc = _dot(xn, wgc_ref[...]) + bc_ref[...]
    lane = lax.broadcasted_iota(jnp.int32, zc.shape, 1)
    gc_ref[...] = jnp.where(lane < MLSTM_HEADS, zc, _log_sigmoid(zc))
    zr = _dot_nt(wgr_ref[...], xn) + br_ref[...]
    row = lax.broadcasted_iota(jnp.int32, zr.shape, 0)
    gr_ref[...] = jnp.where(row < MLSTM_HEADS, zr, _log_sigmoid(zr))


def _const_spec(shape):
    nd = len(shape)
    return pl.BlockSpec(shape, lambda *_: (0,) * nd, pipeline_mode=pl.Buffered(1))


def _weight_specs(w, names):
    return [_const_spec(w[n].shape) for n in names]


def _decode_proj(x, w, tm):
    t, d = x.shape

    def rows(width):
        return pl.BlockSpec((tm, width), lambda i: (i, 0))

    def lanes(height):
        return pl.BlockSpec((height, tm), lambda i: (0, i))

    outs = [(rows, ATT_WIDTH), (rows, MQK_WIDTH), (rows, MQK_WIDTH), (rows, MLSTM_WIDTH),
            (rows, MLSTM_WIDTH), (lanes, GATE_ROWS), (rows, LANES), (lanes, KV_WIDTH),
            (lanes, KV_WIDTH), (lanes, MQK_WIDTH)]
    names = ["g_mix", "w_main", "w_gate_rows", "b_gate_rows", "w_gate_cols", "b_gate_cols", "w_kvk_t"]
    return pl.pallas_call(
        _decode_proj_kernel,
        grid=(t // tm,),
        in_specs=[rows(d)] + _weight_specs(w, names),
        out_specs=[kind(size) for kind, size in outs],
        out_shape=[jax.ShapeDtypeStruct((t, size) if kind is rows else (size, t), F32)
                   for kind, size in outs],
        compiler_params=pltpu.CompilerParams(
            dimension_semantics=("arbitrary",), vmem_limit_bytes=VMEM_LIMIT,
            allow_input_fusion=[False] + [True] * len(names)),
    )(x, *[w[n] for n in names])


def _dup_heads(x):
    swapped = pltpu.roll(x, HEAD_DIM, axis=1)
    lo = _low_half(x.shape)
    return (jnp.where(lo, x, swapped).astype(BF16), jnp.where(lo, swapped, x).astype(BF16))


def _stack_query_heads(q, h):
    parts = []
    for g in range(ATT_GROUP):
        hq = h * ATT_GROUP + g
        slab = q[:, (hq // 2) * LANES:(hq // 2 + 1) * LANES]
        lo = _low_half(slab.shape)
        keep = lo if hq % 2 == 0 else jnp.logical_not(lo)
        parts.append(jnp.where(keep, slab, jnp.zeros_like(slab)))
    return jnp.concatenate(parts, axis=0).astype(BF16)


def _sink_column(sink_ref, h, rows_per_head):
    n = ATT_GROUP * rows_per_head
    grp = lax.broadcasted_iota(jnp.int32, (n, 1), 0) >> _log2(rows_per_head)
    col = jnp.full((n, 1), sink_ref[h * ATT_GROUP], F32)
    for g in range(1, ATT_GROUP):
        col = jnp.where(grp == g, sink_ref[h * ATT_GROUP + g], col)
    return col


def _swa_prompt_stages(sink_ref, q_ref, kc_ref, kp_ref, vc_ref, vp_ref, first_block, store):
    qb = q_ref.shape[0]
    k_all = jnp.concatenate([kp_ref[...], kc_ref[...]], axis=0)
    v_all = jnp.concatenate([vp_ref[...], vc_ref[...]], axis=0)
    kd = _dup_heads(k_all)
    v_t = v_all.T.astype(BF16)

    n = ATT_GROUP * WINDOW
    key = lax.broadcasted_iota(jnp.int32, (2 * WINDOW, n), 0)
    qry = lax.broadcasted_iota(jnp.int32, (2 * WINDOW, n), 1) & (WINDOW - 1)
    delta = key - qry
    in_band = (delta >= 0) & (delta <= WINDOW)
    grp = lax.broadcasted_iota(jnp.int32, (1, n), 1) >> _log2(WINDOW)
    sinks = []
    for h in range(ATT_KV_HEADS):
        row = jnp.full((1, n), sink_ref[h * ATT_GROUP], F32)
        for g in range(1, ATT_GROUP):
            row = jnp.where(grp == g, sink_ref[h * ATT_GROUP + g], row)
        sinks.append(row)
    yield

    probs = []
    for j in range(qb // WINDOW):
        valid = in_band & (key >= jnp.where(first_block, WINDOW, 0)) if j == 0 else in_band
        q = q_ref[j * WINDOW:(j + 1) * WINDOW, :] * (HEAD_DIM ** -0.5)
        band = slice(j * WINDOW, (j + 2) * WINDOW)
        for h in range(ATT_KV_HEADS):
            qh = _stack_query_heads(q, h)
            s = _dot_nt(kd[h][band], qh)
            probs.append(dict(h=h, band=band, s=jnp.where(valid, s, NEG_INF)))
            yield
    for p in probs:
        p["m"] = jnp.maximum(jnp.max(p["s"], axis=0, keepdims=True), sinks[p["h"]])
        yield
    for p in probs:
        p["p"] = jnp.exp(p["s"] - p["m"])
        yield
    for p in probs:
        h = p["h"]
        denom = jnp.sum(p["p"], axis=0, keepdims=True) + jnp.exp(sinks[h] - p["m"])
        p["o_t"] = _dot(v_t[h * HEAD_DIM:(h + 1) * HEAD_DIM, p["band"]], p["p"].astype(BF16)) / denom
        yield
    for j in range(qb // WINDOW):
        pieces = [p["o_t"][:, g * WINDOW:(g + 1) * WINDOW]
                  for p in probs[j * ATT_KV_HEADS:(j + 1) * ATT_KV_HEADS] for g in range(ATT_GROUP)]
        store(j, jnp.concatenate(pieces, axis=0).T)
        yield


def _swa_sample_kernel(sink_ref, q_ref, knt_ref, vnt_ref, ck_ref, cv_ref,
                       o_ref, kbuf_ref, vbuf_ref, *, nb, tq):
    past = ck_ref.shape[1]
    hw = ATT_KV_HEADS * HEAD_DIM
    heads = ATT_KV_HEADS * ATT_GROUP
    n = heads * tq
    fresh = past - tq
    t = lax.broadcasted_iota(jnp.int32, (n, 2 * past), 0) & (tq - 1)
    col = lax.broadcasted_iota(jnp.int32, (n, 2 * past), 1)
    valid = ((col < past) & (col >= t)) | ((col >= past + fresh) & (col - (past + fresh) <= t))
    sink = jnp.concatenate([_sink_column(sink_ref, h, tq) for h in range(ATT_KV_HEADS)], axis=0)
    lane = lax.broadcasted_iota(jnp.int32, (hw, past), 1)
    knt = knt_ref[...]
    vnt = vnt_ref[...]
    zero_slab = jnp.zeros((tq, LANES), F32)
    lo = _low_half((tq, LANES))

    def shifted(old, new_t, b):
        return jnp.where(lane >= fresh, pltpu.roll(new_t, (fresh - b * tq) % past, axis=1),
                         pltpu.roll(old, fresh, axis=1))

    def dup_rows(x):
        h0, h1 = x[:HEAD_DIM], x[HEAD_DIM:]
        return jnp.concatenate([h0, h0, h1, h1], axis=0)

    for b in range(nb):
        k_old = ck_ref[b * hw:(b + 1) * hw, :]
        v_old = cv_ref[b * hw:(b + 1) * hw, :]
        k_new = shifted(k_old, knt, b)
        v_new = shifted(v_old, vnt, b)
        kbuf_ref[b * hw:(b + 1) * hw, :] = k_new
        vbuf_ref[b * hw:(b + 1) * hw, :] = v_new
        kop = jnp.concatenate([dup_rows(k_old), dup_rows(k_new)], axis=1).astype(BF16)
        vop = jnp.concatenate([dup_rows(v_old), dup_rows(v_new)], axis=1).astype(BF16)

        q = q_ref[b * tq:(b + 1) * tq, :]
        blocks = []
        for hq in range(heads):
            slab = q[:, (hq // 2) * LANES:(hq // 2 + 1) * LANES]
            slab = jnp.where(lo if hq % 2 == 0 else jnp.logical_not(lo), slab, 0.0)
            pair = [slab, zero_slab] if hq // ATT_GROUP == 0 else [zero_slab, slab]
            blocks.append(jnp.concatenate(pair, axis=1))
        qx = jnp.concatenate(blocks, axis=0).astype(BF16)

        s = _dot(qx, kop) * (HEAD_DIM ** -0.5)
        s = jnp.where(valid, s, NEG_INF)
        m = jnp.maximum(jnp.max(s, axis=-1, keepdims=True), sink)
        p = jnp.exp(s - m)
        denom = jnp.sum(p, axis=-1, keepdims=True) + jnp.exp(sink - m)
        o = _dot_nt(p.astype(BF16), vop) / denom

        slabs = []
        for j in range(heads // 2):
            h = (2 * j) // ATT_GROUP
            even = o[(2 * j) * tq:(2 * j + 1) * tq, h * LANES:(h + 1) * LANES]
            odd = o[(2 * j + 1) * tq:(2 * j + 2) * tq, h * LANES:(h + 1) * LANES]
            slabs.append(jnp.where(lo, even, odd))
        o_ref[b * tq:(b + 1) * tq, :] = jnp.concatenate(slabs, axis=1)


def _swa_sample(q, knt, vnt, cache_kt, cache_vt, sinks, nbatch, nb):
    t = q.shape[0]
    tq = t // nbatch
    past = cache_kt.shape[1]
    hw = ATT_KV_HEADS * HEAD_DIM
    assert past == WINDOW and tq & (tq - 1) == 0 and nb * tq == past

    def rows(r, width):
        return pl.BlockSpec((r, width), lambda i: (i, 0))

    new_t = pl.BlockSpec((hw, nb * tq), lambda i: (0, i))
    return pl.pallas_call(
        functools.partial(_swa_sample_kernel, nb=nb, tq=tq),
        grid=(nbatch // nb,),
        in_specs=[
            pl.BlockSpec(memory_space=pltpu.SMEM),
            rows(nb * tq, ATT_WIDTH), new_t, new_t, rows(nb * hw, past), rows(nb * hw, past),
        ],
        out_specs=[rows(nb * tq, ATT_WIDTH), rows(nb * hw, past), rows(nb * hw, past)],
        out_shape=[
            jax.ShapeDtypeStruct((t, ATT_WIDTH), F32),
            jax.ShapeDtypeStruct(cache_kt.shape, F32),
            jax.ShapeDtypeStruct(cache_vt.shape, F32),
        ],
        compiler_params=pltpu.CompilerParams(
            dimension_semantics=("arbitrary",), vmem_limit_bytes=VMEM_LIMIT),
    )(sinks, q, knt, vnt, cache_kt, cache_vt)


def _mlstm_intra(q, k, v, it_c, bt_c, it_r, bt_r, m_c, valid):
    d = jnp.where(valid, bt_c - bt_r + it_r, NEG_INF)
    inter = bt_c + m_c
    mt = jnp.maximum(jnp.max(d, axis=-1, keepdims=True), inter)
    sm = _dot_nt(q, k) * jnp.exp(d - mt)
    wi = jnp.exp(inter - mt)
    return _dot(sm.astype(BF16), v), jnp.sum(sm, axis=-1, keepdims=True), mt, wi


def _mlstm_head_out(num, den, mt, og, g_head):
    lower = jnp.maximum(jnp.abs(den), jnp.exp(-mt))
    h = num / lower
    hn = h * lax.rsqrt(jnp.mean(h * h, axis=-1, keepdims=True) + EPS)
    return hn * g_head * jax.nn.sigmoid(og)


def _mlstm_prompt_stages(mq, mk, mv, og, gr, gh_ref, hm_ref, c_ref, n_ref, m_ref, nbatch, chunk):
    s_idx = lax.broadcasted_iota(jnp.int32, (chunk, chunk), 0)
    l_idx = lax.broadcasted_iota(jnp.int32, (chunk, chunk), 1)
    causal = s_idx <= l_idx
    triu = jnp.where(causal, 1.0, 0.0).astype(BF16)
    lo = _low_half((chunk, LANES))
    pad_rows = jnp.zeros((LANES - MLSTM_HEADS, chunk), F32)
    sub8 = (GATE_ROWS, chunk)

    probs = []
    for b in range(nbatch):
        g = gr(b)
        btr = _cumsum_rows(g, triu)
        it_rows = g[:MLSTM_HEADS]
        bt_rows = btr[MLSTM_HEADS:2 * MLSTM_HEADS]
        u_cols = jnp.concatenate([it_rows - bt_rows, pad_rows], axis=0).T
        for h in range(MLSTM_HEADS):
            slab = slice((h // 2) * LANES, (h // 2 + 1) * LANES)
            keep = lo if h % 2 == 0 else jnp.logical_not(lo)
            q = mq(b, slab)
            k = mk(b, slab)
            probs.append(dict(
                b=b, h=h, idx=b * MLSTM_HEADS + h, cols=slice(h * MLSTM_DV, (h + 1) * MLSTM_DV),
                q=jnp.where(keep, q, jnp.zeros_like(q)), k=jnp.where(keep, k, jnp.zeros_like(k)),
                u_c=u_cols[:, h:h + 1], it_r=it_rows[h:h + 1, :], bt_r=bt_rows[h:h + 1, :]))
        yield

    for p in probs:
        p["m_old"] = m_ref[p["idx"]][:, 0:1]
        p["d"] = jnp.where(causal, p["bt_r"] + p["u_c"], NEG_INF)
        p["inter"] = p["bt_r"] + p["m_old"]
        p["q_t"] = p["q"].astype(F32).T.astype(BF16)
        p["kq"] = _dot(p["k"], p["q_t"])
        p["v_t"] = mv(p["b"], p["cols"]).astype(F32).T
        yield
    for p in probs:
        p["mt"] = jnp.maximum(jnp.max(p["d"], axis=0, keepdims=True), p["inter"])
        yield
    for p in probs:
        p["sm"] = p["kq"] * jnp.exp(p["d"] - p["mt"])
        p["wi"] = jnp.exp(p["inter"] - p["mt"])
        yield
    for p in probs:
        c_old = c_ref[p["idx"]]
        n_old = jnp.broadcast_to(n_ref[p["idx"]], (sub8[0], LANES))
        num = (_dot(p["v_t"].astype(BF16), p["sm"].astype(BF16))
               + p["wi"] * _dot(c_old.astype(BF16), p["q_t"]))
        den = (jnp.sum(p["sm"], axis=0, keepdims=True)
               + p["wi"] * _dot(n_old.astype(BF16), p["q_t"])[0:1])
        lower = jnp.maximum(jnp.abs(den), jnp.exp(-p["mt"]))
        h_t = num / lower
        p["hn"] = (h_t * lax.rsqrt(jnp.mean(h_t * h_t, axis=0, keepdims=True) + EPS)).T
        yield
    for p in probs:
        gate = jax.nn.sigmoid(og(p["b"], p["cols"]).astype(F32))
        hm_ref[p["b"], :, p["cols"]] = (p["hn"] * gh_ref[:, p["cols"]] * gate).astype(hm_ref.dtype)
        yield
    for p in probs:
        idx = p["idx"]
        b_last = p["bt_r"][:, chunk - 1:chunk]
        m_new = p["mt"][:, chunk - 1:chunk]
        wk = jnp.exp(b_last - p["bt_r"] + p["it_r"] - m_new)
        wc = jnp.exp(b_last + p["m_old"] - m_new)
        c_ref[idx] = wc * c_ref[idx] + _dot((p["v_t"] * wk).astype(BF16), p["k"])
        n_ref[idx] = wc * n_ref[idx] + _dot(jnp.broadcast_to(wk, sub8).astype(BF16), p["k"])[0:1]
        m_ref[idx] = jnp.broadcast_to(m_new, (1, LANES))
        yield


def _mixer_prompt_kernel(x_ref, g_ref, w_ref, wgr_ref, br_ref, gh_ref, gmem_ref, wck_ref, wcv_ref,
                         wg_ref, wu_ref, wd_ref, wo_ref, wcq_ref, wco_ref, mem_ref,
                         q_ref, k_ref, v_ref, hm_ref, ct_out, n_ref, m_ref, wgu_out, wd_out, wo_out,
                         wcq_out, wco_out, memk_out, memv_out, memkt_out, memvt_out, kwin_out, vwin_out,
                         mq_scr, mk_scr, mv_scr, og_scr, gr_scr, c_ref, wck_scr, wcv_scr,
                         *, nbatch, chunk):
    s = pl.program_id(0)
    wr = s % 2
    rd = 1 - wr
    rows = nbatch * chunk

    def side_work():
        for c in range(0, wg_ref.shape[1], FFN_CHUNK):
            wgu_out[:, 2 * c:2 * c + FFN_CHUNK] = wg_ref[:, c:c + FFN_CHUNK].astype(BF16)
            wgu_out[:, 2 * c + FFN_CHUNK:2 * (c + FFN_CHUNK)] = wu_ref[:, c:c + FFN_CHUNK].astype(BF16)
        wd_out[...] = wd_ref[...].astype(BF16)
        wo_out[...] = wo_ref[...].astype(BF16)
        wcq_out[...] = wcq_ref[...].astype(BF16)
        wco_out[...] = wco_ref[...].astype(BF16)
        mn = _rmsnorm(mem_ref[...], gmem_ref[...]).astype(BF16)
        mk = _dot(mn, wck_scr[...])
        mv = _dot(mn, wcv_scr[...])
        memk_out[...] = mk
        memv_out[...] = mv
        memkt_out[...] = mk.T
        memvt_out[...] = mv.T

    def proj_stages():
        xn = _rmsnorm(x_ref[...].reshape(rows, x_ref.shape[-1]), g_ref[...]).astype(BF16)
        yield

        def mm(lo, hi):
            return _dot_nt(xn, w_ref[lo:hi, :])

        o = 0
        q_ref[...] = mm(o, o + ATT_WIDTH).astype(q_ref.dtype).reshape(q_ref.shape)
        yield
        o += ATT_WIDTH
        kv = mm(o, o + 2 * KV_WIDTH)
        k_ref[...] = kv[:, :KV_WIDTH].reshape(k_ref.shape)
        v_ref[...] = kv[:, KV_WIDTH:].reshape(v_ref.shape)
        yield
        o += 2 * KV_WIDTH
        mqk = mm(o, o + 2 * MQK_WIDTH)
        mq_scr[wr] = mqk[:, :MQK_WIDTH].astype(BF16)
        mk_scr[wr] = (mqk[:, MQK_WIDTH:] * (MLSTM_DK ** -0.5)).astype(BF16)
        yield
        o += 2 * MQK_WIDTH
        mv_scr[wr] = mm(o, o + MLSTM_WIDTH).astype(BF16)
        yield
        o += MLSTM_WIDTH
        og_scr[wr] = mm(o, o + MLSTM_WIDTH).astype(BF16)
        yield
        zr = _dot_nt(wgr_ref[...], xn) + br_ref[...]
        row = lax.broadcasted_iota(jnp.int32, zr.shape, 0)
        gr_scr[wr] = jnp.where(row < MLSTM_HEADS, zr, _log_sigmoid(zr))
        yield

    def mlstm_stages():
        def rows_of(scr):
            return lambda b, lanes: scr[rd, b * chunk:(b + 1) * chunk, lanes]
        return _mlstm_prompt_stages(
            rows_of(mq_scr), rows_of(mk_scr), rows_of(mv_scr), rows_of(og_scr),
            lambda b: gr_scr[rd, :, b * chunk:(b + 1) * chunk],
            gh_ref, hm_ref, c_ref, n_ref, m_ref, nbatch, chunk)

    @pl.when(s == 0)
    def _():
        c_ref[...] = jnp.zeros_like(c_ref)
        n_ref[...] = jnp.zeros_like(n_ref)
        m_ref[...] = jnp.zeros_like(m_ref)
        wck_scr[...] = wck_ref[...].astype(BF16)
        wcv_scr[...] = wcv_ref[...].astype(BF16)
        side_work()
        for _ in proj_stages():
            pass

    @pl.when(s > 0)
    def _():
        side_work()
        mlstm = mlstm_stages()
        for _ in proj_stages():
            for _ in range(MLSTM_STAGES_PER_PROJ_STAGE):
                next(mlstm, None)
        for _ in mlstm:
            pass

    @pl.when(s == pl.num_programs(0) - 2)
    def _():
        for b in range(nbatch):
            kwin_out[b] = k_ref[b, chunk - WINDOW:, :].T
            vwin_out[b] = v_ref[b, chunk - WINDOW:, :].T

    @pl.when(s == pl.num_programs(0) - 1)
    def _():
        for i in range(c_ref.shape[0]):
            half = (i % MLSTM_HEADS) % 2
            ct_out[i] = c_ref[i].T[half * MLSTM_DK:(half + 1) * MLSTM_DK, :]


def _mixer_prompt(x, mem, w, w_gate, w_up, w_down, w_out, w_cq, w_co, nbatch, chunk):
    t, d = x.shape
    seq = t // nbatch
    nchunks = seq // chunk
    nstate = nbatch * MLSTM_HEADS
    rows = nbatch * chunk
    hidden = w_down.shape[0]
    assert hidden % FFN_CHUNK == 0
    assert d % (nchunks * BF16_SUBLANES) == 0 and hidden % (nchunks * BF16_SUBLANES) == 0
    assert w_co.shape[0] % (nchunks * BF16_SUBLANES) == 0
    mem_tokens = mem.shape[0] // nbatch
    blocks_per_batch = mem_tokens // LANES
    nblocks = nbatch * blocks_per_batch
    steps_per_block = nchunks // nblocks
    assert mem_tokens % LANES == 0 and nchunks % nblocks == 0 and chunk >= WINDOW

    def mem_block(s):
        return jnp.minimum(s // steps_per_block, nblocks - 1)

    mem_rows = pl.BlockSpec((LANES, d), lambda s: (mem_block(s), 0))
    mem_kv = pl.BlockSpec((LANES, CROSS_WIDTH), lambda s: (mem_block(s), 0))
    mem_kv_t = pl.BlockSpec((CROSS_WIDTH, LANES),
                            lambda s: (mem_block(s) // blocks_per_batch, mem_block(s) % blocks_per_batch))
    window_t = pl.BlockSpec((nbatch, KV_WIDTH, WINDOW), lambda s: (0, 0, 0))

    def proj_chunk(width):
        return pl.BlockSpec((nbatch, chunk, width), lambda s: (0, jnp.minimum(s, nchunks - 1), 0))

    def state(shape):
        return pl.BlockSpec(shape, lambda s: (0, 0, 0))

    def weight_rows(a, width=None):
        return pl.BlockSpec((a.shape[0] // nchunks, width or a.shape[1]),
                            lambda s: (jnp.minimum(s, nchunks - 1), 0))

    names = ["g_mix", "w_main", "w_gate_rows", "b_gate_rows", "g_head", "g_mem", "w_ck", "w_cv"]
    (q, k, v, hm, ct_st, n_st, m_st, w_gate_up_bf, w_down_bf, w_out_bf, w_cq_bf, w_co_bf, memk, memv,
     memk_t, memv_t, kwin_t, vwin_t) = pl.pallas_call(
        functools.partial(_mixer_prompt_kernel, nbatch=nbatch, chunk=chunk),
        grid=(nchunks + 1,),
        in_specs=[proj_chunk(d)] + _weight_specs(w, names) + [
            weight_rows(w_gate), weight_rows(w_up), weight_rows(w_down), weight_rows(w_out),
            weight_rows(w_cq), weight_rows(w_co), mem_rows,
        ],
        out_specs=[
            proj_chunk(ATT_WIDTH), proj_chunk(KV_WIDTH), proj_chunk(KV_WIDTH),
            pl.BlockSpec((nbatch, chunk, MLSTM_WIDTH), lambda s: (0, jnp.maximum(s - 1, 0), 0)),
            state((nstate, MLSTM_DK, MLSTM_DV)), state((nstate, 1, LANES)), state((nstate, 1, LANES)),
            weight_rows(w_gate, 2 * hidden), weight_rows(w_down), weight_rows(w_out),
            weight_rows(w_cq), weight_rows(w_co), mem_kv, mem_kv, mem_kv_t, mem_kv_t, window_t, window_t,
        ],
        out_shape=[
            jax.ShapeDtypeStruct((nbatch, seq, ATT_WIDTH), BF16),
            jax.ShapeDtypeStruct((nbatch, seq, KV_WIDTH), F32),
            jax.ShapeDtypeStruct((nbatch, seq, KV_WIDTH), F32),
            jax.ShapeDtypeStruct((nbatch, seq, MLSTM_WIDTH), BF16),
            jax.ShapeDtypeStruct((nstate, MLSTM_DK, MLSTM_DV), F32),
            jax.ShapeDtypeStruct((nstate, 1, LANES), F32),
            jax.ShapeDtypeStruct((nstate, 1, LANES), F32),
            jax.ShapeDtypeStruct((d, 2 * hidden), BF16),
            jax.ShapeDtypeStruct((hidden, d), BF16),
            jax.ShapeDtypeStruct(w_out.shape, BF16),
            jax.ShapeDtypeStruct(w_cq.shape, BF16),
            jax.ShapeDtypeStruct(w_co.shape, BF16),
            jax.ShapeDtypeStruct((mem.shape[0], CROSS_WIDTH), F32),
            jax.ShapeDtypeStruct((mem.shape[0], CROSS_WIDTH), F32),
            jax.ShapeDtypeStruct((nbatch * CROSS_WIDTH, mem_tokens), F32),
            jax.ShapeDtypeStruct((nbatch * CROSS_WIDTH, mem_tokens), F32),
            jax.ShapeDtypeStruct((nbatch, KV_WIDTH, WINDOW), F32),
            jax.ShapeDtypeStruct((nbatch, KV_WIDTH, WINDOW), F32),
        ],
        scratch_shapes=[
            pltpu.VMEM((2, rows, MQK_WIDTH), BF16), pltpu.VMEM((2, rows, MQK_WIDTH), BF16),
            pltpu.VMEM((2, rows, MLSTM_WIDTH), BF16), pltpu.VMEM((2, rows, MLSTM_WIDTH), BF16),
            pltpu.VMEM((2, GATE_ROWS, rows), F32),
            pltpu.VMEM((nstate, MLSTM_DV, LANES), F32),
            pltpu.VMEM(w["w_ck"].shape, BF16), pltpu.VMEM(w["w_cv"].shape, BF16),
        ],
        compiler_params=pltpu.CompilerParams(
            dimension_semantics=("arbitrary",), vmem_limit_bytes=VMEM_LIMIT,
            allow_input_fusion=[False] + [True] * len(names) + [False] * 7),
    )(x.reshape(nbatch, seq, d), *[w[n] for n in names], w_gate, w_up, w_down, w_out, w_cq, w_co, mem)
    q = q.reshape(t, ATT_WIDTH)
    k = k.reshape(t, KV_WIDTH)
    v = v.reshape(t, KV_WIDTH)

    c_fin = jnp.swapaxes(ct_st.reshape(nbatch, MLSTM_HEADS, MLSTM_DK, MLSTM_DV), 2, 3)
    n_st = n_st.reshape(nbatch, MLSTM_HEADS, 2, MLSTM_DK)
    n_fin = jnp.stack([n_st[:, h, h % 2, :] for h in range(MLSTM_HEADS)], axis=1)
    m_fin = m_st[:, 0, 0].reshape(nbatch, MLSTM_HEADS)
    bf16_weights = {"w_gate_up": w_gate_up_bf, "w_down": w_down_bf, "w_out": w_out_bf,
                    "w_cq": w_cq_bf, "w_co": w_co_bf}
    return (q, k, v, hm.reshape(t, MLSTM_WIDTH), c_fin, n_fin, m_fin, bf16_weights, memk, memv,
            memk_t, memv_t, kwin_t, vwin_t)


def _mlstm_sample_kernel(mq_ref, mk_ref, mkt_ref, mv_ref, og_ref, gc_ref, gr_ref, mrow_ref, c_ref,
                         n_ref, gh_ref, hm_ref, co_ref, no_ref, mt_ref, *, nb, tq):
    rows = nb * tq
    shift = tq.bit_length() - 1
    r = lax.broadcasted_iota(jnp.int32, (rows, rows), 0)
    s = lax.broadcasted_iota(jnp.int32, (rows, rows), 1)
    same = (r >> shift) == (s >> shift)
    valid = same & (s <= r)
    tril = jnp.where(valid, 1.0, 0.0).astype(BF16)
    triu = jnp.where(same & (r <= s), 1.0, 0.0).astype(BF16)
    row_batch = lax.broadcasted_iota(jnp.int32, (rows, 1), 0) >> shift
    lane = lax.broadcasted_iota(jnp.int32, (rows, LANES), 1)

    gc = gc_ref[...]
    gr = gr_ref[...]
    btc = _cumsum_cols(tril, gc)
    btr = _cumsum_rows(gr, triu)
    m_rows = mrow_ref[...]
    mt_all = jnp.zeros((rows, LANES), F32)

    def last_of_batch(col):
        parts = [jnp.broadcast_to(col[(b + 1) * tq - 1:(b + 1) * tq], (tq, 1)) for b in range(nb)]
        return jnp.concatenate(parts, axis=0)

    dk = MLSTM_DK
    lo = _low_half((rows, LANES))
    own_q = row_batch == (lax.broadcasted_iota(jnp.int32, (rows, nb * dk), 1) >> _log2(dk))
    lane_batch = lax.broadcasted_iota(jnp.int32, (dk, rows), 1) >> shift

    def state_rows(b, h):
        return slice((b * MLSTM_HEADS + h) * dk, (b * MLSTM_HEADS + h + 1) * dk)

    for h in range(MLSTM_HEADS):
        slab = slice((h // 2) * LANES, (h // 2 + 1) * LANES)
        qs, ks = mq_ref[:, slab], mk_ref[:, slab]
        qsw, ksw = pltpu.roll(qs, dk, axis=1), pltpu.roll(ks, dk, axis=1)
        q2 = jnp.where(lo, qs, qsw) if h % 2 == 0 else jnp.where(lo, qsw, qs)
        k2 = jnp.where(lo, ks, ksw) if h % 2 == 0 else jnp.where(lo, ksw, ks)
        qb, kb = q2[:, :dk].astype(BF16), k2[:, :dk].astype(BF16)
        vb = mv_ref[:, h * MLSTM_DV:(h + 1) * MLSTM_DV].astype(BF16)
        it_c = gc[:, h:h + 1]
        bt_c = btc[:, MLSTM_HEADS + h:MLSTM_HEADS + h + 1]
        it_r = gr[h:h + 1, :]
        bt_r = btr[MLSTM_HEADS + h:MLSTM_HEADS + h + 1, :]
        m_c = m_rows[:, h:h + 1]

        num, ssum, mt, wi = _mlstm_intra(qb, kb, vb, it_c, bt_c, it_r, bt_r, m_c, valid)
        qx = jnp.where(own_q, jnp.concatenate([q2] * (nb // 2), axis=1), 0.0).astype(BF16)
        c_stack = jnp.concatenate([c_ref[state_rows(b, h), :] for b in range(nb)], axis=0)
        qc = _dot(qx, c_stack.astype(BF16))
        n_rows = jnp.concatenate(
            [jnp.broadcast_to(n_ref[b * MLSTM_HEADS + h], (tq, dk)) for b in range(nb)], axis=0)
        num = num + wi * qc
        den = ssum + wi * jnp.sum(qb.astype(F32) * n_rows, axis=-1, keepdims=True)
        og = og_ref[:, h * MLSTM_DV:(h + 1) * MLSTM_DV]
        g_head = gh_ref[:, h * MLSTM_DV:(h + 1) * MLSTM_DV]
        hm_ref[:, h * MLSTM_DV:(h + 1) * MLSTM_DV] = _mlstm_head_out(num, den, mt, og, g_head)

        b_last = last_of_batch(bt_c)
        m_new = last_of_batch(mt)
        wk = jnp.exp(b_last - bt_c + it_c - m_new)
        wc = jnp.exp(b_last + m_c - m_new)
        wkv = (wk * vb.astype(F32)).astype(BF16)
        wkk = wk * kb.astype(F32)
        kt = mkt_ref[h * dk:(h + 1) * dk, :]
        kx = jnp.concatenate([jnp.where(lane_batch == b, kt, 0.0) for b in range(nb)], axis=0)
        upd = _dot(kx.astype(BF16), wkv)
        for b in range(nb):
            idx = b * MLSTM_HEADS + h
            wc_b = wc[b * tq:b * tq + 1]
            co_ref[state_rows(b, h), :] = wc_b * c_ref[state_rows(b, h), :] + upd[b * dk:(b + 1) * dk]
            no_ref[idx] = wc_b * n_ref[idx] + jnp.sum(wkk[b * tq:(b + 1) * tq], axis=0, keepdims=True)
        mt_all = jnp.where(lane == h, mt, mt_all)
    mt_ref[...] = mt_all


def _mlstm_sample(mq, mk, mkt, mv, og, gcol, grow, m_rows, ct_in, n_in, g_head, nbatch, nb):
    t = mq.shape[0]
    tq = t // nbatch
    rows = nb * tq
    assert tq & (tq - 1) == 0 and rows == LANES and nb % 2 == 0

    def tok(width):
        return pl.BlockSpec((rows, width), lambda i: (i, 0))

    ct_spec = pl.BlockSpec((nb * MLSTM_HEADS * MLSTM_DK, MLSTM_DV), lambda i: (i, 0))
    n_spec = pl.BlockSpec((nb * MLSTM_HEADS, 1, MLSTM_DK), lambda i: (i, 0, 0))
    return pl.pallas_call(
        functools.partial(_mlstm_sample_kernel, nb=nb, tq=tq),
        grid=(nbatch // nb,),
        in_specs=[
            tok(MQK_WIDTH), tok(MQK_WIDTH),
            pl.BlockSpec((MQK_WIDTH, rows), lambda i: (0, i)),
            tok(MLSTM_WIDTH), tok(MLSTM_WIDTH), tok(LANES),
            pl.BlockSpec((GATE_ROWS, rows), lambda i: (0, i)),
            tok(LANES),
            ct_spec, n_spec,
            _const_spec((1, MLSTM_WIDTH)),
        ],
        out_specs=[tok(MLSTM_WIDTH), ct_spec, n_spec, tok(LANES)],
        out_shape=[
            jax.ShapeDtypeStruct((t, MLSTM_WIDTH), F32),
            jax.ShapeDtypeStruct(ct_in.shape, F32),
            jax.ShapeDtypeStruct(n_in.shape, F32),
            jax.ShapeDtypeStruct((t, LANES), F32),
        ],
        compiler_params=pltpu.CompilerParams(
            dimension_semantics=("arbitrary",), vmem_limit_bytes=VMEM_LIMIT,
            allow_input_fusion=[False] * 7 + [True, False, True, False]),
    )(mq, mk, mkt, mv, og, gcol, grow, m_rows, ct_in, n_in, g_head)


def _mix_stage(x, att, hm, wo_ref, gcr_ref, wcq_ref):
    cat = jnp.concatenate([att.astype(BF16), hm.astype(BF16)], axis=1)
    y1 = x + _dot(cat, wo_ref[...])
    qc = _dot(_rmsnorm(y1, gcr_ref[...]).astype(BF16), wcq_ref[...])
    return y1, qc


def _cross_attention(qc, mk, mv):
    slabs = []
    for j in range(CROSS_WIDTH // LANES):
        qs = qc[:, j * LANES:(j + 1) * LANES]
        ks = mk[:, j * LANES:(j + 1) * LANES]
        vs = mv[:, j * LANES:(j + 1) * LANES]
        n = qs.shape[0]
        lo = _low_half(qs.shape)
        zero = jnp.zeros_like(qs)
        q2 = jnp.concatenate([jnp.where(lo, qs, zero), jnp.where(lo, zero, qs)], axis=0)
        s = _dot_nt(q2 * (HEAD_DIM ** -0.5), ks)
        p = jnp.exp(s - jnp.max(s, axis=-1, keepdims=True))
        o2 = _dot(p.astype(BF16), vs) / jnp.sum(p, axis=-1, keepdims=True)
        slabs.append(jnp.where(lo, o2[:n], o2[n:]))
    return jnp.concatenate(slabs, axis=1)


FFN_CHUNK = 256


def _finish(y3, gfin_ref, final):
    return _rmsnorm(y3, gfin_ref[...]) if final else y3


def _ffn_stage(y1, o, wco_ref, gf_ref, wgu_ref, wd_ref, act_ref, between=None):
    y2 = y1 + _dot(o.astype(BF16), wco_ref[...])
    xn = _rmsnorm(y2, gf_ref[...]).astype(BF16)
    for c in range(wd_ref.shape[0] // FFN_CHUNK):
        gu = _dot(xn, wgu_ref[:, 2 * c * FFN_CHUNK:2 * (c + 1) * FFN_CHUNK])
        g, u = gu[:, :FFN_CHUNK], gu[:, FFN_CHUNK:]
        act_ref[:, c * FFN_CHUNK:(c + 1) * FFN_CHUNK] = (g * jax.nn.sigmoid(g) * u).astype(BF16)
        if between is not None:
            between()
    return y2 + _dot(act_ref[...], wd_ref[...])


def _post_kernel(sink_ref, q_ref, kc_ref, kp_ref, vc_ref, vp_ref, x_ref, hm_ref, mk_ref, mv_ref,
                 xd_ref, attd_ref, hmd_ref, mkd_ref, mvd_ref,
                 wo_ref, gcr_ref, wcq_ref, wco_ref, gf_ref, wgu_hbm, wd_hbm, gfin_ref,
                 o_ref, od_ref, act_ref, att_ref, y3_ref, wgu_ref, wd_ref, copy_sem,
                 *, final, ntiles, tiles_per_seq, tq):
    tm = x_ref.shape[0]
    s = pl.program_id(0)
    tile = jnp.minimum(s, ntiles - 1)
    first_block = (tile % tiles_per_seq) == 0
    slot = s % 2

    def swa_stages():
        def store(j, value):
            att_ref[slot, j * WINDOW:(j + 1) * WINDOW, :] = value.astype(att_ref.dtype)
        return _swa_prompt_stages(sink_ref, q_ref, kc_ref, kp_ref, vc_ref, vp_ref, first_block, store)

    def finish():
        done = _finish(y3_ref[...], gfin_ref, final)
        o_ref[...] = done[:tm]
        od_ref[...] = done[tm:]

    def ffn_weight_copies():
        return (pltpu.make_async_copy(wgu_hbm, wgu_ref, copy_sem.at[0]),
                pltpu.make_async_copy(wd_hbm, wd_ref, copy_sem.at[1]))

    @pl.when(s == 0)
    def _():
        for copy in ffn_weight_copies():
            copy.start()
        y3_ref[...] = jnp.zeros_like(y3_ref)
        for _ in swa_stages():
            pass

    @pl.when(s == 1)
    def _():
        for copy in ffn_weight_copies():
            copy.wait()

    @pl.when(jnp.logical_and(s > 0, s <= ntiles))
    def _():
        swa = swa_stages()

        def advance(n=SWA_STAGES_PER_FFN_CHUNK):
            for _ in range(n):
                next(swa, None)

        finish()
        x = jnp.concatenate([x_ref[...], xd_ref[...]], axis=0)
        att = jnp.concatenate([att_ref[1 - slot], attd_ref[...].astype(BF16)], axis=0)
        hm = jnp.concatenate([hm_ref[...], hmd_ref[...].astype(BF16)], axis=0)
        y1, qc = _mix_stage(x, att, hm, wo_ref, gcr_ref, wcq_ref)
        advance()
        o = jnp.concatenate([
            _cross_attention(qc[:tm].astype(BF16), mk_ref[...].astype(BF16), mv_ref[...].astype(BF16)),
            _cross_attention_decode(qc[tm:], mkd_ref, mvd_ref, tq)], axis=0)
        advance()
        y3_ref[...] = _ffn_stage(y1, o, wco_ref, gf_ref, wgu_ref, wd_ref, act_ref, between=advance)
        for _ in swa:
            pass

    @pl.when(s > ntiles)
    def _():
        finish()


_MIX_WEIGHTS = ("w_out", "g_cross", "w_cq")
_FFN_WEIGHTS = ("w_co", "g_ffn", "w_gate_up", "w_down", "g_final")
_COPIED_WEIGHTS = ("w_gate_up", "w_down")

def _post(x, q, k, v, hm, memk, memv, xd, attd, hmd, memkd_t, memvd_t, sinks, w, nbatch, nseq, final, tm):
    t, d = x.shape
    ntiles = t // tm
    tpb = ntiles // nbatch
    wpt = tm // WINDOW
    m = memk.shape[0] // nbatch
    hidden = w["w_down"].shape[0]
    td = xd.shape[0]
    tq = td // nseq
    slab = td // ntiles
    assert td % ntiles == 0 and slab % tq == 0 and slab % BF16_SUBLANES == 0
    mem_d = memkd_t.shape[1]

    def attn_tile(s):
        return jnp.minimum(s, ntiles - 1)

    def post_tile(s):
        return jnp.clip(s - 1, 0, ntiles - 1)

    def done_tile(s):
        return jnp.maximum(s - 2, 0)

    def attn_rows(width):
        return pl.BlockSpec((tm, width), lambda s: (attn_tile(s), 0))

    def prev_window(width):
        return pl.BlockSpec((WINDOW, width), lambda s: (jnp.maximum(attn_tile(s) * wpt - 1, 0), 0))

    def post_rows(width):
        return pl.BlockSpec((tm, width), lambda s: (post_tile(s), 0))

    def post_slab(width):
        return pl.BlockSpec((slab, width), lambda s: (post_tile(s), 0))

    mem_spec = pl.BlockSpec((m, CROSS_WIDTH), lambda s: (post_tile(s) // tpb, 0))
    memd_spec = pl.BlockSpec((slab // tq * CROSS_WIDTH, mem_d), lambda s: (post_tile(s), 0))
    names = _MIX_WEIGHTS + _FFN_WEIGHTS
    return pl.pallas_call(
        functools.partial(_post_kernel, final=final, ntiles=ntiles, tiles_per_seq=tpb, tq=tq),
        grid=(ntiles + 2,),
        in_specs=[
            pl.BlockSpec(memory_space=pltpu.SMEM),
            attn_rows(ATT_WIDTH), attn_rows(KV_WIDTH), prev_window(KV_WIDTH),
            attn_rows(KV_WIDTH), prev_window(KV_WIDTH),
            post_rows(d), post_rows(MLSTM_WIDTH), mem_spec, mem_spec,
            post_slab(d), post_slab(ATT_WIDTH), post_slab(MLSTM_WIDTH), memd_spec, memd_spec,
        ] + [pl.BlockSpec(memory_space=pl.ANY) if n in _COPIED_WEIGHTS else _const_spec(w[n].shape)
             for n in names],
        out_specs=[pl.BlockSpec((tm, d), lambda s: (done_tile(s), 0)),
                   pl.BlockSpec((slab, d), lambda s: (done_tile(s), 0))],
        out_shape=[jax.ShapeDtypeStruct((t, d), F32), jax.ShapeDtypeStruct((td, d), F32)],
        scratch_shapes=[pltpu.VMEM((tm + slab, hidden), BF16), pltpu.VMEM((2, tm, ATT_WIDTH), BF16),
                        pltpu.VMEM((tm + slab, d), F32)]
                       + [pltpu.VMEM(w[n].shape, BF16) for n in _COPIED_WEIGHTS]
                       + [pltpu.SemaphoreType.DMA((len(_COPIED_WEIGHTS),))],
        compiler_params=pltpu.CompilerParams(
            dimension_semantics=("arbitrary",), vmem_limit_bytes=VMEM_LIMIT),
    )(sinks, q, k, k, v, v, x, hm, memk, memv, xd, attd, hmd, memkd_t, memvd_t,
      *[w[n] for n in names])


def _cross_attention_decode(qc, mk_ref, mv_ref, tq):
    n = CROSS_HEADS * tq
    row_head = lax.broadcasted_iota(jnp.int32, (n, CROSS_WIDTH), 0) >> _log2(tq)
    lane_head = lax.broadcasted_iota(jnp.int32, (n, CROSS_WIDTH), 1) >> _log2(HEAD_DIM)
    own = row_head == lane_head
    outs = []
    for b in range(qc.shape[0] // tq):
        q = qc[b * tq:(b + 1) * tq, :]
        qx = jnp.where(own, jnp.concatenate([q] * CROSS_HEADS, axis=0), 0.0).astype(BF16)
        mk = mk_ref[b * CROSS_WIDTH:(b + 1) * CROSS_WIDTH, :].astype(BF16)
        mv = mv_ref[b * CROSS_WIDTH:(b + 1) * CROSS_WIDTH, :].astype(BF16)
        s = _dot(qx, mk) * (HEAD_DIM ** -0.5)
        p = jnp.exp(s - jnp.max(s, axis=-1, keepdims=True))
        pv = _dot_nt(p.astype(BF16), mv) / jnp.sum(p, axis=-1, keepdims=True)
        pv = jnp.where(own, pv, 0.0)
        out = pv[0:tq]
        for h in range(1, CROSS_HEADS):
            out = out + pv[h * tq:(h + 1) * tq]
        outs.append(out)
    return jnp.concatenate(outs, axis=0)


def _layer_weights(l, w_in, b_igate, b_fgate, g_mlstm_head, g_mix, g_cross, g_mem, w_ck, w_cv,
                   g_ffn, g_final):
    gate_w = w_in[l][:, MAIN_WIDTH:]
    gate_b = jnp.concatenate([b_igate[l], b_fgate[l]]).astype(F32)
    ngate = gate_b.shape[0]
    row = lambda a: a.astype(F32).reshape(1, -1)
    w_t = w_in[l].T
    k0 = ATT_WIDTH
    mk0 = ATT_WIDTH + 2 * KV_WIDTH + MQK_WIDTH
    return {
        "w_kvk_t": jnp.concatenate([w_t[k0:k0 + 2 * KV_WIDTH], w_t[mk0:mk0 + MQK_WIDTH]]).astype(BF16),
        "w_main": w_t[:MAIN_WIDTH].astype(BF16),
        "w_gate_cols": jnp.pad(gate_w, ((0, 0), (0, LANES - ngate))).astype(BF16),
        "w_gate_rows": jnp.pad(gate_w.T, ((0, GATE_ROWS - ngate), (0, 0))).astype(BF16),
        "b_gate_cols": jnp.pad(gate_b, (0, LANES - ngate)).reshape(1, LANES),
        "b_gate_rows": jnp.pad(gate_b, (0, GATE_ROWS - ngate)).reshape(GATE_ROWS, 1),
        "g_mix": row(g_mix[l]), "g_cross": row(g_cross[l]), "g_mem": row(g_mem[l]),
        "g_ffn": row(g_ffn[l]), "g_final": row(g_final), "g_head": row(g_mlstm_head[l]),
        "w_ck": w_ck[l].astype(F32), "w_cv": w_cv[l].astype(F32),
    }


TOKEN_TILE = 512
MLSTM_STAGES_PER_PROJ_STAGE = 20
SWA_STAGES_PER_FFN_CHUNK = 3
MLSTM_CHUNK_ROWS = 256
SWA_SAMPLE_BATCH = 16
MLSTM_SAMPLE_BATCH = 16


def kernel(x_prompt, x_sample, mem_prompt, cache_swa_k, cache_swa_v, state_mlstm_C, state_mlstm_n,
           state_mlstm_m, cache_mem_k, cache_mem_v, w_in, b_igate, b_fgate, attn_sinks,
           g_mlstm_head, w_out, g_mix, g_cross, g_mem, w_cq, w_ck, w_cv, w_co, g_ffn, w_gate,
           w_up, w_down, g_final):
    depth = w_in.shape[0]
    bp, sp, d = x_prompt.shape
    bs, ss, _ = x_sample.shape
    mem_tokens = mem_prompt.shape[1]
    past = cache_swa_k.shape[2]
    yp = x_prompt.reshape(bp * sp, d)
    ys = x_sample.reshape(bs * ss, d)
    mem = mem_prompt.reshape(bp * mem_tokens, d)
    outs = [[] for _ in range(12)]

    for l in range(depth):
        final = l == depth - 1
        w = _layer_weights(l, w_in, b_igate, b_fgate, g_mlstm_head, g_mix, g_cross, g_mem,
                           w_ck, w_cv, g_ffn, g_final)
        sinks = attn_sinks[l].astype(F32)

        q, k, v, hm, c_p, n_p, m_p, cast, memk, memv, memk_t, memv_t, kwin_t, vwin_t = _mixer_prompt(
            yp, mem, w, w_gate[l].astype(F32), w_up[l].astype(F32), w_down[l].astype(F32),
            w_out[l].astype(F32), w_cq[l].astype(F32), w_co[l].astype(F32), bp, MLSTM_CHUNK_ROWS)
        w.update(cast)

        def tokens_second(a, heads, tokens):
            return jnp.transpose(a.reshape(bp, heads, HEAD_DIM, tokens), (0, 3, 1, 2))

        outs[0].append(tokens_second(kwin_t, ATT_KV_HEADS, WINDOW))
        outs[1].append(tokens_second(vwin_t, ATT_KV_HEADS, WINDOW))
        outs[2].append(c_p)
        outs[3].append(n_p)
        outs[4].append(m_p)
        outs[5].append(tokens_second(memk_t, CROSS_HEADS, mem_tokens))
        outs[6].append(tokens_second(memv_t, CROSS_HEADS, mem_tokens))

        def tokens_last(a):
            return jnp.transpose(a.astype(F32), (0, 2, 3, 1)).reshape(-1, a.shape[1])

        qd, mq, mk, mv, og, grow, gcol, knt, vnt, mkt = _decode_proj(ys, w, TOKEN_TILE)
        att_d, kbuf_t, vbuf_t = _swa_sample(
            qd, knt, vnt, tokens_last(cache_swa_k[l]), tokens_last(cache_swa_v[l]), sinks, bs,
            SWA_SAMPLE_BATCH)
        m_rows = jnp.pad(jnp.repeat(state_mlstm_m[l].astype(F32), ss, axis=0),
                         ((0, 0), (0, LANES - MLSTM_HEADS)))
        ct_in = jnp.swapaxes(state_mlstm_C[l].astype(F32), 2, 3).reshape(-1, MLSTM_DV)
        hm_d, ct_s, n_s, mt = _mlstm_sample(
            mq, mk, mkt, mv, og, gcol, grow, m_rows, ct_in,
            state_mlstm_n[l].astype(F32).reshape(bs * MLSTM_HEADS, 1, MLSTM_DK),
            w["g_head"], bs, MLSTM_SAMPLE_BATCH)

        yp, ys = _post(yp, q, k, v, hm, memk, memv, ys, att_d, hm_d, tokens_last(cache_mem_k[l]),
                       tokens_last(cache_mem_v[l]), sinks, w, bp, bs, final, TOKEN_TILE)
        buf_t_shape = (bs, ATT_KV_HEADS, HEAD_DIM, past)
        outs[7].append(jnp.transpose(kbuf_t.reshape(buf_t_shape), (0, 3, 1, 2)))
        outs[8].append(jnp.transpose(vbuf_t.reshape(buf_t_shape), (0, 3, 1, 2)))
        outs[9].append(jnp.swapaxes(ct_s.reshape(bs, MLSTM_HEADS, MLSTM_DK, MLSTM_DV), 2, 3))
        outs[10].append(n_s.reshape(bs, MLSTM_HEADS, MLSTM_DK))
        outs[11].append(mt.reshape(bs, ss, LANES)[:, ss - 1, :MLSTM_HEADS])

    return (yp.reshape(bp, sp, d), ys.reshape(bs, ss, d)) + tuple(jnp.stack(o) for o in outs)
```

```python
import functools

import jax
import jax.numpy as jnp
from jax import lax
from jax.experimental import pallas as pl
from jax.experimental.pallas import tpu as pltpu

F32 = jnp.float32
BF16 = jnp.bfloat16

HEAD_DIM = 64
ATT_GROUP = 4
ATT_KV_HEADS = 2
ATT_WIDTH = 512
KV_WIDTH = ATT_KV_HEADS * HEAD_DIM
WINDOW = 128
MLSTM_HEADS = 4
MLSTM_DV = 128
MLSTM_DK = 64
MQK_WIDTH = MLSTM_HEADS * MLSTM_DK
MLSTM_WIDTH = MLSTM_HEADS * MLSTM_DV
MAIN_WIDTH = ATT_WIDTH + 2 * KV_WIDTH + 2 * MQK_WIDTH + 2 * MLSTM_WIDTH
CROSS_HEADS = 4
CROSS_WIDTH = CROSS_HEADS * HEAD_DIM
EPS = 1e-6
NEG_INF = float("-inf")
assert HEAD_DIM == 4 ** 3

LANES = 128
BF16_SUBLANES = 16
GATE_ROWS = BF16_SUBLANES
VMEM_LIMIT = 56 * 1024 * 1024

NT = (((1,), (1,)), ((), ()))


def _dot(a, b):
    return jnp.dot(a, b, preferred_element_type=F32)


def _dot_nt(a, b):
    return lax.dot_general(a, b, NT, preferred_element_type=F32)


def _rmsnorm(x, g):
    return x * lax.rsqrt(jnp.mean(x * x, axis=-1, keepdims=True) + EPS) * g


def _log_sigmoid(x):
    return jnp.minimum(x, 0.0) - jnp.log1p(jnp.exp(-jnp.abs(x)))


def _split3(x):
    hi = x.astype(BF16)
    r1 = x - hi.astype(F32)
    mid = r1.astype(BF16)
    lo = (r1 - mid.astype(F32)).astype(BF16)
    return hi, mid, lo


def _cumsum_cols(tri, x):
    hi, mid, lo = _split3(x)
    return _dot(tri, hi) + _dot(tri, mid) + _dot(tri, lo)


def _cumsum_rows(x, tri):
    hi, mid, lo = _split3(x)
    return _dot(hi, tri) + _dot(mid, tri) + _dot(lo, tri)


def _log2(n):
    assert n > 0 and n & (n - 1) == 0, n
    return n.bit_length() - 1


def _low_half(shape):
    return lax.broadcasted_iota(jnp.int32, shape, 1) < HEAD_DIM


def _decode_proj_kernel(x_ref, g_ref, w_ref, wgr_ref, br_ref, wgc_ref, bc_ref, wt_ref,
                        q_ref, mq_ref, mk_ref, mv_ref, og_ref, gr_ref, gc_ref, kt_ref, vt_ref, mkt_ref):
    xn = _rmsnorm(x_ref[...], g_ref[...]).astype(BF16)
    zt = _dot_nt(wt_ref[...], xn)
    kt_ref[...] = zt[:KV_WIDTH]
    vt_ref[...] = zt[KV_WIDTH:2 * KV_WIDTH]
    mkt_ref[...] = zt[2 * KV_WIDTH:] * (MLSTM_DK ** -0.5)

    def mm(lo, hi):
        return _dot_nt(xn, w_ref[lo:hi, :])

    o = 0
    q_ref[...] = mm(o, o + ATT_WIDTH)
    o += ATT_WIDTH + 2 * KV_WIDTH
    mqk = mm(o, o + 2 * MQK_WIDTH)
    mq_ref[...] = mqk[:, :MQK_WIDTH]
    mk_ref[...] = mqk[:, MQK_WIDTH:] * (MLSTM_DK ** -0.5)
    o += 2 * MQK_WIDTH
    mv_ref[...] = mm(o, o + MLSTM_WIDTH)
    o += MLSTM_WIDTH
    og_ref[...] = mm(o, o + MLSTM_WIDTH)

    zc = _dot(xn, wgc_ref[...]) + bc_ref[...]
    lane = lax.broadcasted_iota(jnp.int32, zc.shape, 1)
    gc_ref[...] = jnp.where(lane < MLSTM_HEADS, zc, _log_sigmoid(zc))
    zr = _dot_nt(wgr_ref[...], xn) + br_ref[...]
    row = lax.broadcasted_iota(jnp.int32, zr.shape, 0)
    gr_ref[...] = jnp.where(row < MLSTM_HEADS, zr, _log_sigmoid(zr))


def _const_spec(shape):
    nd = len(shape)
    return pl.BlockSpec(shape, lambda *_: (0,) * nd, pipeline_mode=pl.Buffered(1))


def _weight_specs(w, names):
    return [_const_spec(w[n].shape) for n in names]


def _decode_proj(x, w, tm):
    t, d = x.shape

    def rows(width):
        return pl.BlockSpec((tm, width), lambda i: (i, 0))

    def lanes(height):
        return pl.BlockSpec((height, tm), lambda i: (0, i))

    outs = [(rows, ATT_WIDTH), (rows, MQK_WIDTH), (rows, MQK_WIDTH), (rows, MLSTM_WIDTH),
            (rows, MLSTM_WIDTH), (lanes, GATE_ROWS), (rows, LANES), (lanes, KV_WIDTH),
            (lanes, KV_WIDTH), (lanes, MQK_WIDTH)]
    names = ["g_mix", "w_main", "w_gate_rows", "b_gate_rows", "w_gate_cols", "b_gate_cols", "w_kvk_t"]
    return pl.pallas_call(
        _decode_proj_kernel,
        grid=(t // tm,),
        in_specs=[rows(d)] + _weight_specs(w, names),
        out_specs=[kind(size) for kind, size in outs],
        out_shape=[jax.ShapeDtypeStruct((t, size) if kind is rows else (size, t), F32)
                   for kind, size in outs],
        compiler_params=pltpu.CompilerParams(
            dimension_semantics=("arbitrary",), vmem_limit_bytes=VMEM_LIMIT,
            allow_input_fusion=[False] + [True] * len(names)),
    )(x, *[w[n] for n in names])


def _dup_heads(x):
    swapped = pltpu.roll(x, HEAD_DIM, axis=1)
    lo = _low_half(x.shape)
    return (jnp.where(lo, x, swapped).astype(BF16), jnp.where(lo, swapped, x).astype(BF16))


def _stack_query_heads(q, h):
    parts = []
    for g in range(ATT_GROUP):
        hq = h * ATT_GROUP + g
        slab = q[:, (hq // 2) * LANES:(hq // 2 + 1) * LANES]
        lo = _low_half(slab.shape)
        keep = lo if hq % 2 == 0 else jnp.logical_not(lo)
        parts.append(jnp.where(keep, slab, jnp.zeros_like(slab)))
    return jnp.concatenate(parts, axis=0).astype(BF16)


def _sink_column(sink_ref, h, rows_per_head):
    n = ATT_GROUP * rows_per_head
    grp = lax.broadcasted_iota(jnp.int32, (n, 1), 0) >> _log2(rows_per_head)
    col = jnp.full((n, 1), sink_ref[h * ATT_GROUP], F32)
    for g in range(1, ATT_GROUP):
        col = jnp.where(grp == g, sink_ref[h * ATT_GROUP + g], col)
    return col


def _swa_prompt_stages(sink_ref, q_ref, kc_ref, kp_ref, vc_ref, vp_ref, first_block, store):
    qb = q_ref.shape[0]
    k_all = jnp.concatenate([kp_ref[...], kc_ref[...]], axis=0)
    v_all = jnp.concatenate([vp_ref[...], vc_ref[...]], axis=0)
    kd = _dup_heads(k_all)
    v_t = v_all.T.astype(BF16)

    n = ATT_GROUP * WINDOW
    key = lax.broadcasted_iota(jnp.int32, (2 * WINDOW, n), 0)
    qry = lax.broadcasted_iota(jnp.int32, (2 * WINDOW, n), 1) & (WINDOW - 1)
    delta = key - qry
    in_band = (delta >= 0) & (delta <= WINDOW)
    grp = lax.broadcasted_iota(jnp.int32, (1, n), 1) >> _log2(WINDOW)
    sinks = []
    for h in range(ATT_KV_HEADS):
        row = jnp.full((1, n), sink_ref[h * ATT_GROUP], F32)
        for g in range(1, ATT_GROUP):
            row = jnp.where(grp == g, sink_ref[h * ATT_GROUP + g], row)
        sinks.append(row)
    yield

    probs = []
    for j in range(qb // WINDOW):
        valid = in_band & (key >= jnp.where(first_block, WINDOW, 0)) if j == 0 else in_band
        q = q_ref[j * WINDOW:(j + 1) * WINDOW, :] * (HEAD_DIM ** -0.5)
        band = slice(j * WINDOW, (j + 2) * WINDOW)
        for h in range(ATT_KV_HEADS):
            qh = _stack_query_heads(q, h)
            s = _dot_nt(kd[h][band], qh)
            probs.append(dict(h=h, band=band, s=jnp.where(valid, s, NEG_INF)))
            yield
    for p in probs:
        p["m"] = jnp.maximum(jnp.max(p["s"], axis=0, keepdims=True), sinks[p["h"]])
        yield
    for p in probs:
        p["p"] = jnp.exp(p["s"] - p["m"])
        yield
    for p in probs:
        h = p["h"]
        denom = jnp.sum(p["p"], axis=0, keepdims=True) + jnp.exp(sinks[h] - p["m"])
        p["o_t"] = _dot(v_t[h * HEAD_DIM:(h + 1) * HEAD_DIM, p["band"]], p["p"].astype(BF16)) / denom
        yield
    for j in range(qb // WINDOW):
        pieces = [p["o_t"][:, g * WINDOW:(g + 1) * WINDOW]
                  for p in probs[j * ATT_KV_HEADS:(j + 1) * ATT_KV_HEADS] for g in range(ATT_GROUP)]
        store(j, jnp.concatenate(pieces, axis=0).T)
        yield


def _swa_sample_kernel(sink_ref, q_ref, knt_ref, vnt_ref, ck_ref, cv_ref,
                       o_ref, kbuf_ref, vbuf_ref, *, nb, tq):
    past = ck_ref.shape[1]
    hw = ATT_KV_HEADS * HEAD_DIM
    heads = ATT_KV_HEADS * ATT_GROUP
    n = heads * tq
    fresh = past - tq
    t = lax.broadcasted_iota(jnp.int32, (n, 2 * past), 0) & (tq - 1)
    col = lax.broadcasted_iota(jnp.int32, (n, 2 * past), 1)
    valid = ((col < past) & (col >= t)) | ((col >= past + fresh) & (col - (past + fresh) <= t))
    sink = jnp.concatenate([_sink_column(sink_ref, h, tq) for h in range(ATT_KV_HEADS)], axis=0)
    lane = lax.broadcasted_iota(jnp.int32, (hw, past), 1)
    knt = knt_ref[...]
    vnt = vnt_ref[...]
    zero_slab = jnp.zeros((tq, LANES), F32)
    lo = _low_half((tq, LANES))

    def shifted(old, new_t, b):
        return jnp.where(lane >= fresh, pltpu.roll(new_t, (fresh - b * tq) % past, axis=1),
                         pltpu.roll(old, fresh, axis=1))

    def dup_rows(x):
        h0, h1 = x[:HEAD_DIM], x[HEAD_DIM:]
        return jnp.concatenate([h0, h0, h1, h1], axis=0)

    for b in range(nb):
        k_old = ck_ref[b * hw:(b + 1) * hw, :]
        v_old = cv_ref[b * hw:(b + 1) * hw, :]
        k_new = shifted(k_old, knt, b)
        v_new = shifted(v_old, vnt, b)
        kbuf_ref[b * hw:(b + 1) * hw, :] = k_new
        vbuf_ref[b * hw:(b + 1) * hw, :] = v_new
        kop = jnp.concatenate([dup_rows(k_old), dup_rows(k_new)], axis=1).astype(BF16)
        vop = jnp.concatenate([dup_rows(v_old), dup_rows(v_new)], axis=1).astype(BF16)

        q = q_ref[b * tq:(b + 1) * tq, :]
        blocks = []
        for hq in range(heads):
            slab = q[:, (hq // 2) * LANES:(hq // 2 + 1) * LANES]
            slab = jnp.where(lo if hq % 2 == 0 else jnp.logical_not(lo), slab, 0.0)
            pair = [slab, zero_slab] if hq // ATT_GROUP == 0 else [zero_slab, slab]
            blocks.append(jnp.concatenate(pair, axis=1))
        qx = jnp.concatenate(blocks, axis=0).astype(BF16)

        s = _dot(qx, kop) * (HEAD_DIM ** -0.5)
        s = jnp.where(valid, s, NEG_INF)
        m = jnp.maximum(jnp.max(s, axis=-1, keepdims=True), sink)
        p = jnp.exp(s - m)
        denom = jnp.sum(p, axis=-1, keepdims=True) + jnp.exp(sink - m)
        o = _dot_nt(p.astype(BF16), vop) / denom

        slabs = []
        for j in range(heads // 2):
            h = (2 * j) // ATT_GROUP
            even = o[(2 * j) * tq:(2 * j + 1) * tq, h * LANES:(h + 1) * LANES]
            odd = o[(2 * j + 1) * tq:(2 * j + 2) * tq, h * LANES:(h + 1) * LANES]
            slabs.append(jnp.where(lo, even, odd))
        o_ref[b * tq:(b + 1) * tq, :] = jnp.concatenate(slabs, axis=1)


def _swa_sample(q, knt, vnt, cache_kt, cache_vt, sinks, nbatch, nb):
    t = q.shape[0]
    tq = t // nbatch
    past = cache_kt.shape[1]
    hw = ATT_KV_HEADS * HEAD_DIM
    assert past == WINDOW and tq & (tq - 1) == 0 and nb * tq == past

    def rows(r, width):
        return pl.BlockSpec((r, width), lambda i: (i, 0))

    new_t = pl.BlockSpec((hw, nb * tq), lambda i: (0, i))
    return pl.pallas_call(
        functools.partial(_swa_sample_kernel, nb=nb, tq=tq),
        grid=(nbatch // nb,),
        in_specs=[
            pl.BlockSpec(memory_space=pltpu.SMEM),
            rows(nb * tq, ATT_WIDTH), new_t, new_t, rows(nb * hw, past), rows(nb * hw, past),
        ],
        out_specs=[rows(nb * tq, ATT_WIDTH), rows(nb * hw, past), rows(nb * hw, past)],
        out_shape=[
            jax.ShapeDtypeStruct((t, ATT_WIDTH), F32),
            jax.ShapeDtypeStruct(cache_kt.shape, F32),
            jax.ShapeDtypeStruct(cache_vt.shape, F32),
        ],
        compiler_params=pltpu.CompilerParams(
            dimension_semantics=("arbitrary",), vmem_limit_bytes=VMEM_LIMIT),
    )(sinks, q, knt, vnt, cache_kt, cache_vt)


def _mlstm_intra(q, k, v, it_c, bt_c, it_r, bt_r, m_c, valid):
    d = jnp.where(valid, bt_c - bt_r + it_r, NEG_INF)
    inter = bt_c + m_c
    mt = jnp.maximum(jnp.max(d, axis=-1, keepdims=True), inter)
    sm = _dot_nt(q, k) * jnp.exp(d - mt)
    wi = jnp.exp(inter - mt)
    return _dot(sm.astype(BF16), v), jnp.sum(sm, axis=-1, keepdims=True), mt, wi


def _mlstm_head_out(num, den, mt, og, g_head):
    lower = jnp.maximum(jnp.abs(den), jnp.exp(-mt))
    h = num / lower
    hn = h * lax.rsqrt(jnp.mean(h * h, axis=-1, keepdims=True) + EPS)
    return hn * g_head * jax.nn.sigmoid(og)


def _mlstm_prompt_stages(mq, mk, mv, og, gr, gh_ref, hm_ref, c_ref, n_ref, m_ref, nbatch, chunk):
    s_idx = lax.broadcasted_iota(jnp.int32, (chunk, chunk), 0)
    l_idx = lax.broadcasted_iota(jnp.int32, (chunk, chunk), 1)
    causal = s_idx <= l_idx
    triu = jnp.where(causal, 1.0, 0.0).astype(BF16)
    lo = _low_half((chunk, LANES))
    pad_rows = jnp.zeros((LANES - MLSTM_HEADS, chunk), F32)
    sub8 = (GATE_ROWS, chunk)

    probs = []
    for b in range(nbatch):
        g = gr(b)
        btr = _cumsum_rows(g, triu)
        it_rows = g[:MLSTM_HEADS]
        bt_rows = btr[MLSTM_HEADS:2 * MLSTM_HEADS]
        u_cols = jnp.concatenate([it_rows - bt_rows, pad_rows], axis=0).T
        for h in range(MLSTM_HEADS):
            slab = slice((h // 2) * LANES, (h // 2 + 1) * LANES)
            keep = lo if h % 2 == 0 else jnp.logical_not(lo)
            q = mq(b, slab)
            k = mk(b, slab)
            probs.append(dict(
                b=b, h=h, idx=b * MLSTM_HEADS + h, cols=slice(h * MLSTM_DV, (h + 1) * MLSTM_DV),
                q=jnp.where(keep, q, jnp.zeros_like(q)), k=jnp.where(keep, k, jnp.zeros_like(k)),
                u_c=u_cols[:, h:h + 1], it_r=it_rows[h:h + 1, :], bt_r=bt_rows[h:h + 1, :]))
        yield

    for p in probs:
        p["m_old"] = m_ref[p["idx"]][:, 0:1]
        p["d"] = jnp.where(causal, p["bt_r"] + p["u_c"], NEG_INF)
        p["inter"] = p["bt_r"] + p["m_old"]
        p["q_t"] = p["q"].astype(F32).T.astype(BF16)
        p["kq"] = _dot(p["k"], p["q_t"])
        p["v_t"] = mv(p["b"], p["cols"]).astype(F32).T
        yield
    for p in probs:
        p["mt"] = jnp.maximum(jnp.max(p["d"], axis=0, keepdims=True), p["inter"])
        yield
    for p in probs:
        p["sm"] = p["kq"] * jnp.exp(p["d"] - p["mt"])
        p["wi"] = jnp.exp(p["inter"] - p["mt"])
        yield
    for p in probs:
        c_old = c_ref[p["idx"]]
        n_old = jnp.broadcast_to(n_ref[p["idx"]], (sub8[0], LANES))
        num = (_dot(p["v_t"].astype(BF16), p["sm"].astype(BF16))
               + p["wi"] * _dot(c_old.astype(BF16), p["q_t"]))
        den = (jnp.sum(p["sm"], axis=0, keepdims=True)
               + p["wi"] * _dot(n_old.astype(BF16), p["q_t"])[0:1])
        lower = jnp.maximum(jnp.abs(den), jnp.exp(-p["mt"]))
        h_t = num / lower
        p["hn"] = (h_t * lax.rsqrt(jnp.mean(h_t * h_t, axis=0, keepdims=True) + EPS)).T
        yield
    for p in probs:
        gate = jax.nn.sigmoid(og(p["b"], p["cols"]).astype(F32))
        hm_ref[p["b"], :, p["cols"]] = (p["hn"] * gh_ref[:, p["cols"]] * gate).astype(hm_ref.dtype)
        yield
    for p in probs:
        idx = p["idx"]
        b_last = p["bt_r"][:, chunk - 1:chunk]
        m_new = p["mt"][:, chunk - 1:chunk]
        wk = jnp.exp(b_last - p["bt_r"] + p["it_r"] - m_new)
        wc = jnp.exp(b_last + p["m_old"] - m_new)
        c_ref[idx] = wc * c_ref[idx] + _dot((p["v_t"] * wk).astype(BF16), p["k"])
        n_ref[idx] = wc * n_ref[idx] + _dot(jnp.broadcast_to(wk, sub8).astype(BF16), p["k"])[0:1]
        m_ref[idx] = jnp.broadcast_to(m_new, (1, LANES))
        yield


def _mixer_prompt_kernel(x_ref, g_ref, w_ref, wgr_ref, br_ref, gh_ref, gmem_ref, wck_ref, wcv_ref,
                         wg_ref, wu_ref, wd_ref, wo_ref, wcq_ref, wco_ref, mem_ref,
                         q_ref, k_ref, v_ref, hm_ref, ct_out, n_ref, m_ref, wgu_out, wd_out, wo_out,
                         wcq_out, wco_out, memk_out, memv_out, memkt_out, memvt_out, kwin_out, vwin_out,
                         mq_scr, mk_scr, mv_scr, og_scr, gr_scr, c_ref, wck_scr, wcv_scr,
                         *, nbatch, chunk):
    s = pl.program_id(0)
    wr = s % 2
    rd = 1 - wr
    rows = nbatch * chunk

    def side_work():
        for c in range(0, wg_ref.shape[1], FFN_CHUNK):
            wgu_out[:, 2 * c:2 * c + FFN_CHUNK] = wg_ref[:, c:c + FFN_CHUNK].astype(BF16)
            wgu_out[:, 2 * c + FFN_CHUNK:2 * (c + FFN_CHUNK)] = wu_ref[:, c:c + FFN_CHUNK].astype(BF16)
        wd_out[...] = wd_ref[...].astype(BF16)
        wo_out[...] = wo_ref[...].astype(BF16)
        wcq_out[...] = wcq_ref[...].astype(BF16)
        wco_out[...] = wco_ref[...].astype(BF16)
        mn = _rmsnorm(mem_ref[...], gmem_ref[...]).astype(BF16)
        mk = _dot(mn, wck_scr[...])
        mv = _dot(mn, wcv_scr[...])
        memk_out[...] = mk
        memv_out[...] = mv
        memkt_out[...] = mk.T
        memvt_out[...] = mv.T

    def proj_stages():
        xn = _rmsnorm(x_ref[...].reshape(rows, x_ref.shape[-1]), g_ref[...]).astype(BF16)
        yield

        def mm(lo, hi):
            return _dot_nt(xn, w_ref[lo:hi, :])

        o = 0
        q_ref[...] = mm(o, o + ATT_WIDTH).astype(q_ref.dtype).reshape(q_ref.shape)
        yield
        o += ATT_WIDTH
        kv = mm(o, o + 2 * KV_WIDTH)
        k_ref[...] = kv[:, :KV_WIDTH].reshape(k_ref.shape)
        v_ref[...] = kv[:, KV_WIDTH:].reshape(v_ref.shape)
        yield
        o += 2 * KV_WIDTH
        mqk = mm(o, o + 2 * MQK_WIDTH)
        mq_scr[wr] = mqk[:, :MQK_WIDTH].astype(BF16)
        mk_scr[wr] = (mqk[:, MQK_WIDTH:] * (MLSTM_DK ** -0.5)).astype(BF16)
        yield
        o += 2 * MQK_WIDTH
        mv_scr[wr] = mm(o, o + MLSTM_WIDTH).astype(BF16)
        yield
        o += MLSTM_WIDTH
        og_scr[wr] = mm(o, o + MLSTM_WIDTH).astype(BF16)
        yield
        zr = _dot_nt(wgr_ref[...], xn) + br_ref[...]
        row = lax.broadcasted_iota(jnp.int32, zr.shape, 0)
        gr_scr[wr] = jnp.where(row < MLSTM_HEADS, zr, _log_sigmoid(zr))
        yield

    def mlstm_stages():
        def rows_of(scr):
            return lambda b, lanes: scr[rd, b * chunk:(b + 1) * chunk, lanes]
        return _mlstm_prompt_stages(
            rows_of(mq_scr), rows_of(mk_scr), rows_of(mv_scr), rows_of(og_scr),
            lambda b: gr_scr[rd, :, b * chunk:(b + 1) * chunk],
            gh_ref, hm_ref, c_ref, n_ref, m_ref, nbatch, chunk)

    @pl.when(s == 0)
    def _():
        c_ref[...] = jnp.zeros_like(c_ref)
        n_ref[...] = jnp.zeros_like(n_ref)
        m_ref[...] = jnp.zeros_like(m_ref)
        wck_scr[...] = wck_ref[...].astype(BF16)
        wcv_scr[...] = wcv_ref[...].astype(BF16)
        side_work()
        for _ in proj_stages():
            pass

    @pl.when(s > 0)
    def _():
        side_work()
        mlstm = mlstm_stages()
        for _ in proj_stages():
            for _ in range(MLSTM_STAGES_PER_PROJ_STAGE):
                next(mlstm, None)
        for _ in mlstm:
            pass

    @pl.when(s == pl.num_programs(0) - 2)
    def _():
        for b in range(nbatch):
            kwin_out[b] = k_ref[b, chunk - WINDOW:, :].T
            vwin_out[b] = v_ref[b, chunk - WINDOW:, :].T

    @pl.when(s == pl.num_programs(0) - 1)
    def _():
        for i in range(c_ref.shape[0]):
            half = (i % MLSTM_HEADS) % 2
            ct_out[i] = c_ref[i].T[half * MLSTM_DK:(half + 1) * MLSTM_DK, :]


def _mixer_prompt(x, mem, w, w_gate, w_up, w_down, w_out, w_cq, w_co, nbatch, chunk):
    t, d = x.shape
    seq = t // nbatch
    nchunks = seq // chunk
    nstate = nbatch * MLSTM_HEADS
    rows = nbatch * chunk
    hidden = w_down.shape[0]
    assert hidden % FFN_CHUNK == 0
    assert d % (nchunks * BF16_SUBLANES) == 0 and hidden % (nchunks * BF16_SUBLANES) == 0
    assert w_co.shape[0] % (nchunks * BF16_SUBLANES) == 0
    mem_tokens = mem.shape[0] // nbatch
    blocks_per_batch = mem_tokens // LANES
    nblocks = nbatch * blocks_per_batch
    steps_per_block = nchunks // nblocks
    assert mem_tokens % LANES == 0 and nchunks % nblocks == 0 and chunk >= WINDOW

    def mem_block(s):
        return jnp.minimum(s // steps_per_block, nblocks - 1)

    mem_rows = pl.BlockSpec((LANES, d), lambda s: (mem_block(s), 0))
    mem_kv = pl.BlockSpec((LANES, CROSS_WIDTH), lambda s: (mem_block(s), 0))
    mem_kv_t = pl.BlockSpec((CROSS_WIDTH, LANES),
                            lambda s: (mem_block(s) // blocks_per_batch, mem_block(s) % blocks_per_batch))
    window_t = pl.BlockSpec((nbatch, KV_WIDTH, WINDOW), lambda s: (0, 0, 0))

    def proj_chunk(width):
        return pl.BlockSpec((nbatch, chunk, width), lambda s: (0, jnp.minimum(s, nchunks - 1), 0))

    def state(shape):
        return pl.BlockSpec(shape, lambda s: (0, 0, 0))

    def weight_rows(a, width=None):
        return pl.BlockSpec((a.shape[0] // nchunks, width or a.shape[1]),
                            lambda s: (jnp.minimum(s, nchunks - 1), 0))

    names = ["g_mix", "w_main", "w_gate_rows", "b_gate_rows", "g_head", "g_mem", "w_ck", "w_cv"]
    (q, k, v, hm, ct_st, n_st, m_st, w_gate_up_bf, w_down_bf, w_out_bf, w_cq_bf, w_co_bf, memk, memv,
     memk_t, memv_t, kwin_t, vwin_t) = pl.pallas_call(
        functools.partial(_mixer_prompt_kernel, nbatch=nbatch, chunk=chunk),
        grid=(nchunks + 1,),
        in_specs=[proj_chunk(d)] + _weight_specs(w, names) + [
            weight_rows(w_gate), weight_rows(w_up), weight_rows(w_down), weight_rows(w_out),
            weight_rows(w_cq), weight_rows(w_co), mem_rows,
        ],
        out_specs=[
            proj_chunk(ATT_WIDTH), proj_chunk(KV_WIDTH), proj_chunk(KV_WIDTH),
            pl.BlockSpec((nbatch, chunk, MLSTM_WIDTH), lambda s: (0, jnp.maximum(s - 1, 0), 0)),
            state((nstate, MLSTM_DK, MLSTM_DV)), state((nstate, 1, LANES)), state((nstate, 1, LANES)),
            weight_rows(w_gate, 2 * hidden), weight_rows(w_down), weight_rows(w_out),
            weight_rows(w_cq), weight_rows(w_co), mem_kv, mem_kv, mem_kv_t, mem_kv_t, window_t, window_t,
        ],
        out_shape=[
            jax.ShapeDtypeStruct((nbatch, seq, ATT_WIDTH), BF16),
            jax.ShapeDtypeStruct((nbatch, seq, KV_WIDTH), F32),
            jax.ShapeDtypeStruct((nbatch, seq, KV_WIDTH), F32),
            jax.ShapeDtypeStruct((nbatch, seq, MLSTM_WIDTH), BF16),
            jax.ShapeDtypeStruct((nstate, MLSTM_DK, MLSTM_DV), F32),
            jax.ShapeDtypeStruct((nstate, 1, LANES), F32),
            jax.ShapeDtypeStruct((nstate, 1, LANES), F32),
            jax.ShapeDtypeStruct((d, 2 * hidden), BF16),
            jax.ShapeDtypeStruct((hidden, d), BF16),
            jax.ShapeDtypeStruct(w_out.shape, BF16),
            jax.ShapeDtypeStruct(w_cq.shape, BF16),
            jax.ShapeDtypeStruct(w_co.shape, BF16),
            jax.ShapeDtypeStruct((mem.shape[0], CROSS_WIDTH), F32),
            jax.ShapeDtypeStruct((mem.shape[0], CROSS_WIDTH), F32),
            jax.ShapeDtypeStruct((nbatch * CROSS_WIDTH, mem_tokens), F32),
            jax.ShapeDtypeStruct((nbatch * CROSS_WIDTH, mem_tokens), F32),
            jax.ShapeDtypeStruct((nbatch, KV_WIDTH, WINDOW), F32),
            jax.ShapeDtypeStruct((nbatch, KV_WIDTH, WINDOW), F32),
        ],
        scratch_shapes=[
            pltpu.VMEM((2, rows, MQK_WIDTH), BF16), pltpu.VMEM((2, rows, MQK_WIDTH), BF16),
            pltpu.VMEM((2, rows, MLSTM_WIDTH), BF16), pltpu.VMEM((2, rows, MLSTM_WIDTH), BF16),
            pltpu.VMEM((2, GATE_ROWS, rows), F32),
            pltpu.VMEM((nstate, MLSTM_DV, LANES), F32),
            pltpu.VMEM(w["w_ck"].shape, BF16), pltpu.VMEM(w["w_cv"].shape, BF16),
        ],
        compiler_params=pltpu.CompilerParams(
            dimension_semantics=("arbitrary",), vmem_limit_bytes=VMEM_LIMIT,
            allow_input_fusion=[False] + [True] * len(names) + [False] * 7),
    )(x.reshape(nbatch, seq, d), *[w[n] for n in names], w_gate, w_up, w_down, w_out, w_cq, w_co, mem)
    q = q.reshape(t, ATT_WIDTH)
    k = k.reshape(t, KV_WIDTH)
    v = v.reshape(t, KV_WIDTH)

    c_fin = jnp.swapaxes(ct_st.reshape(nbatch, MLSTM_HEADS, MLSTM_DK, MLSTM_DV), 2, 3)
    n_st = n_st.reshape(nbatch, MLSTM_HEADS, 2, MLSTM_DK)
    n_fin = jnp.stack([n_st[:, h, h % 2, :] for h in range(MLSTM_HEADS)], axis=1)
    m_fin = m_st[:, 0, 0].reshape(nbatch, MLSTM_HEADS)
    bf16_weights = {"w_gate_up": w_gate_up_bf, "w_down": w_down_bf, "w_out": w_out_bf,
                    "w_cq": w_cq_bf, "w_co": w_co_bf}
    return (q, k, v, hm.reshape(t, MLSTM_WIDTH), c_fin, n_fin, m_fin, bf16_weights, memk, memv,
            memk_t, memv_t, kwin_t, vwin_t)


def _mlstm_sample_kernel(mq_ref, mk_ref, mkt_ref, mv_ref, og_ref, gc_ref, gr_ref, mrow_ref, c_ref,
                         n_ref, gh_ref, hm_ref, co_ref, no_ref, mt_ref, *, nb, tq):
    rows = nb * tq
    shift = tq.bit_length() - 1
    r = lax.broadcasted_iota(jnp.int32, (rows, rows), 0)
    s = lax.broadcasted_iota(jnp.int32, (rows, rows), 1)
    same = (r >> shift) == (s >> shift)
    valid = same & (s <= r)
    tril = jnp.where(valid, 1.0, 0.0).astype(BF16)
    triu = jnp.where(same & (r <= s), 1.0, 0.0).astype(BF16)
    row_batch = lax.broadcasted_iota(jnp.int32, (rows, 1), 0) >> shift
    lane = lax.broadcasted_iota(jnp.int32, (rows, LANES), 1)

    gc = gc_ref[...]
    gr = gr_ref[...]
    btc = _cumsum_cols(tril, gc)
    btr = _cumsum_rows(gr, triu)
    m_rows = mrow_ref[...]
    mt_all = jnp.zeros((rows, LANES), F32)

    def last_of_batch(col):
        parts = [jnp.broadcast_to(col[(b + 1) * tq - 1:(b + 1) * tq], (tq, 1)) for b in range(nb)]
        return jnp.concatenate(parts, axis=0)

    dk = MLSTM_DK
    lo = _low_half((rows, LANES))
    own_q = row_batch == (lax.broadcasted_iota(jnp.int32, (rows, nb * dk), 1) >> _log2(dk))
    lane_batch = lax.broadcasted_iota(jnp.int32, (dk, rows), 1) >> shift

    def state_rows(b, h):
        return slice((b * MLSTM_HEADS + h) * dk, (b * MLSTM_HEADS + h + 1) * dk)

    for h in range(MLSTM_HEADS):
        slab = slice((h // 2) * LANES, (h // 2 + 1) * LANES)
        qs, ks = mq_ref[:, slab], mk_ref[:, slab]
        qsw, ksw = pltpu.roll(qs, dk, axis=1), pltpu.roll(ks, dk, axis=1)
        q2 = jnp.where(lo, qs, qsw) if h % 2 == 0 else jnp.where(lo, qsw, qs)
        k2 = jnp.where(lo, ks, ksw) if h % 2 == 0 else jnp.where(lo, ksw, ks)
        qb, kb = q2[:, :dk].astype(BF16), k2[:, :dk].astype(BF16)
        vb = mv_ref[:, h * MLSTM_DV:(h + 1) * MLSTM_DV].astype(BF16)
        it_c = gc[:, h:h + 1]
        bt_c = btc[:, MLSTM_HEADS + h:MLSTM_HEADS + h + 1]
        it_r = gr[h:h + 1, :]
        bt_r = btr[MLSTM_HEADS + h:MLSTM_HEADS + h + 1, :]
        m_c = m_rows[:, h:h + 1]

        num, ssum, mt, wi = _mlstm_intra(qb, kb, vb, it_c, bt_c, it_r, bt_r, m_c, valid)
        qx = jnp.where(own_q, jnp.concatenate([q2] * (nb // 2), axis=1), 0.0).astype(BF16)
        c_stack = jnp.concatenate([c_ref[state_rows(b, h), :] for b in range(nb)], axis=0)
        qc = _dot(qx, c_stack.astype(BF16))
        n_rows = jnp.concatenate(
            [jnp.broadcast_to(n_ref[b * MLSTM_HEADS + h], (tq, dk)) for b in range(nb)], axis=0)
        num = num + wi * qc
        den = ssum + wi * jnp.sum(qb.astype(F32) * n_rows, axis=-1, keepdims=True)
        og = og_ref[:, h * MLSTM_DV:(h + 1) * MLSTM_DV]
        g_head = gh_ref[:, h * MLSTM_DV:(h + 1) * MLSTM_DV]
        hm_ref[:, h * MLSTM_DV:(h + 1) * MLSTM_DV] = _mlstm_head_out(num, den, mt, og, g_head)

        b_last = last_of_batch(bt_c)
        m_new = last_of_batch(mt)
        wk = jnp.exp(b_last - bt_c + it_c - m_new)
        wc = jnp.exp(b_last + m_c - m_new)
        wkv = (wk * vb.astype(F32)).astype(BF16)
        wkk = wk * kb.astype(F32)
        kt = mkt_ref[h * dk:(h + 1) * dk, :]
        kx = jnp.concatenate([jnp.where(lane_batch == b, kt, 0.0) for b in range(nb)], axis=0)
        upd = _dot(kx.astype(BF16), wkv)
        for b in range(nb):
            idx = b * MLSTM_HEADS + h
            wc_b = wc[b * tq:b * tq + 1]
            co_ref[state_rows(b, h), :] = wc_b * c_ref[state_rows(b, h), :] + upd[b * dk:(b + 1) * dk]
            no_ref[idx] = wc_b * n_ref[idx] + jnp.sum(wkk[b * tq:(b + 1) * tq], axis=0, keepdims=True)
        mt_all = jnp.where(lane == h, mt, mt_all)
    mt_ref[...] = mt_all


def _mlstm_sample(mq, mk, mkt, mv, og, gcol, grow, m_rows, ct_in, n_in, g_head, nbatch, nb):
    t = mq.shape[0]
    tq = t // nbatch
    rows = nb * tq
    assert tq & (tq - 1) == 0 and rows == LANES and nb % 2 == 0

    def tok(width):
        return pl.BlockSpec((rows, width), lambda i: (i, 0))

    ct_spec = pl.BlockSpec((nb * MLSTM_HEADS * MLSTM_DK, MLSTM_DV), lambda i: (i, 0))
    n_spec = pl.BlockSpec((nb * MLSTM_HEADS, 1, MLSTM_DK), lambda i: (i, 0, 0))
    return pl.pallas_call(
        functools.partial(_mlstm_sample_kernel, nb=nb, tq=tq),
        grid=(nbatch // nb,),
        in_specs=[
            tok(MQK_WIDTH), tok(MQK_WIDTH),
            pl.BlockSpec((MQK_WIDTH, rows), lambda i: (0, i)),
            tok(MLSTM_WIDTH), tok(MLSTM_WIDTH), tok(LANES),
            pl.BlockSpec((GATE_ROWS, rows), lambda i: (0, i)),
            tok(LANES),
            ct_spec, n_spec,
            _const_spec((1, MLSTM_WIDTH)),
        ],
        out_specs=[tok(MLSTM_WIDTH), ct_spec, n_spec, tok(LANES)],
        out_shape=[
            jax.ShapeDtypeStruct((t, MLSTM_WIDTH), F32),
            jax.ShapeDtypeStruct(ct_in.shape, F32),
            jax.ShapeDtypeStruct(n_in.shape, F32),
            jax.ShapeDtypeStruct((t, LANES), F32),
        ],
        compiler_params=pltpu.CompilerParams(
            dimension_semantics=("arbitrary",), vmem_limit_bytes=VMEM_LIMIT,
            allow_input_fusion=[False] * 7 + [True, False, True, False]),
    )(mq, mk, mkt, mv, og, gcol, grow, m_rows, ct_in, n_in, g_head)


def _mix_stage(x, att, hm, wo_ref, gcr_ref, wcq_ref):
    cat = jnp.concatenate([att.astype(BF16), hm.astype(BF16)], axis=1)
    y1 = x + _dot(cat, wo_ref[...])
    qc = _dot(_rmsnorm(y1, gcr_ref[...]).astype(BF16), wcq_ref[...])
    return y1, qc


def _cross_attention(qc, mk, mv):
    slabs = []
    for j in range(CROSS_WIDTH // LANES):
        qs = qc[:, j * LANES:(j + 1) * LANES]
        ks = mk[:, j * LANES:(j + 1) * LANES]
        vs = mv[:, j * LANES:(j + 1) * LANES]
        n = qs.shape[0]
        lo = _low_half(qs.shape)
        zero = jnp.zeros_like(qs)
        q2 = jnp.concatenate([jnp.where(lo, qs, zero), jnp.where(lo, zero, qs)], axis=0)
        s = _dot_nt(q2 * (HEAD_DIM ** -0.5), ks)
        p = jnp.exp(s - jnp.max(s, axis=-1, keepdims=True))
        o2 = _dot(p.astype(BF16), vs) / jnp.sum(p, axis=-1, keepdims=True)
        slabs.append(jnp.where(lo, o2[:n], o2[n:]))
    return jnp.concatenate(slabs, axis=1)


FFN_CHUNK = 256


def _finish(y3, gfin_ref, final):
    return _rmsnorm(y3, gfin_ref[...]) if final else y3


def _ffn_stage(y1, o, wco_ref, gf_ref, wgu_ref, wd_ref, act_ref, between=None):
    y2 = y1 + _dot(o.astype(BF16), wco_ref[...])
    xn = _rmsnorm(y2, gf_ref[...]).astype(BF16)
    for c in range(wd_ref.shape[0] // FFN_CHUNK):
        gu = _dot(xn, wgu_ref[:, 2 * c * FFN_CHUNK:2 * (c + 1) * FFN_CHUNK])
        g, u = gu[:, :FFN_CHUNK], gu[:, FFN_CHUNK:]
        act_ref[:, c * FFN_CHUNK:(c + 1) * FFN_CHUNK] = (g * jax.nn.sigmoid(g) * u).astype(BF16)
        if between is not None:
            between()
    return y2 + _dot(act_ref[...], wd_ref[...])


def _post_kernel(sink_ref, q_ref, kc_ref, kp_ref, vc_ref, vp_ref, x_ref, hm_ref, mk_ref, mv_ref,
                 xd_ref, attd_ref, hmd_ref, mkd_ref, mvd_ref,
                 wo_ref, gcr_ref, wcq_ref, wco_ref, gf_ref, wgu_ref, wd_ref, gfin_ref,
                 o_ref, od_ref, act_ref, att_ref, y3_ref, *, final, ntiles, tiles_per_seq, tq):
    tm = x_ref.shape[0]
    s = pl.program_id(0)
    tile = jnp.minimum(s, ntiles - 1)
    first_block = (tile % tiles_per_seq) == 0
    slot = s % 2

    def swa_stages():
        def store(j, value):
            att_ref[slot, j * WINDOW:(j + 1) * WINDOW, :] = value.astype(att_ref.dtype)
        return _swa_prompt_stages(sink_ref, q_ref, kc_ref, kp_ref, vc_ref, vp_ref, first_block, store)

    def finish():
        done = _finish(y3_ref[...], gfin_ref, final)
        o_ref[...] = done[:tm]
        od_ref[...] = done[tm:]

    @pl.when(s == 0)
    def _():
        y3_ref[...] = jnp.zeros_like(y3_ref)
        for _ in swa_stages():
            pass

    @pl.when(jnp.logical_and(s > 0, s <= ntiles))
    def _():
        swa = swa_stages()

        def advance(n=SWA_STAGES_PER_FFN_CHUNK):
            for _ in range(n):
                next(swa, None)

        finish()
        x = jnp.concatenate([x_ref[...], xd_ref[...]], axis=0)
        att = jnp.concatenate([att_ref[1 - slot], attd_ref[...].astype(BF16)], axis=0)
        hm = jnp.concatenate([hm_ref[...], hmd_ref[...].astype(BF16)], axis=0)
        y1, qc = _mix_stage(x, att, hm, wo_ref, gcr_ref, wcq_ref)
        advance()
        o = jnp.concatenate([
            _cross_attention(qc[:tm].astype(BF16), mk_ref[...].astype(BF16), mv_ref[...].astype(BF16)),
            _cross_attention_decode(qc[tm:], mkd_ref, mvd_ref, tq)], axis=0)
        advance()
        y3_ref[...] = _ffn_stage(y1, o, wco_ref, gf_ref, wgu_ref, wd_ref, act_ref, between=advance)
        for _ in swa:
            pass

    @pl.when(s > ntiles)
    def _():
        finish()


_MIX_WEIGHTS = ("w_out", "g_cross", "w_cq")
_FFN_WEIGHTS = ("w_co", "g_ffn", "w_gate_up", "w_down", "g_final")

def _post(x, q, k, v, hm, memk, memv, xd, attd, hmd, memkd_t, memvd_t, sinks, w, nbatch, nseq, final, tm):
    t, d = x.shape
    ntiles = t // tm
    tpb = ntiles // nbatch
    wpt = tm // WINDOW
    m = memk.shape[0] // nbatch
    hidden = w["w_down"].shape[0]
    td = xd.shape[0]
    tq = td // nseq
    slab = td // ntiles
    assert td % ntiles == 0 and slab % tq == 0 and slab % BF16_SUBLANES == 0
    mem_d = memkd_t.shape[1]

    def attn_tile(s):
        return jnp.minimum(s, ntiles - 1)

    def post_tile(s):
        return jnp.clip(s - 1, 0, ntiles - 1)

    def done_tile(s):
        return jnp.maximum(s - 2, 0)

    def attn_rows(width):
        return pl.BlockSpec((tm, width), lambda s: (attn_tile(s), 0))

    def prev_window(width):
        return pl.BlockSpec((WINDOW, width), lambda s: (jnp.maximum(attn_tile(s) * wpt - 1, 0), 0))

    def post_rows(width):
        return pl.BlockSpec((tm, width), lambda s: (post_tile(s), 0))

    def post_slab(width):
        return pl.BlockSpec((slab, width), lambda s: (post_tile(s), 0))

    mem_spec = pl.BlockSpec((m, CROSS_WIDTH), lambda s: (post_tile(s) // tpb, 0))
    memd_spec = pl.BlockSpec((slab // tq * CROSS_WIDTH, mem_d), lambda s: (post_tile(s), 0))
    names = _MIX_WEIGHTS + _FFN_WEIGHTS
    return pl.pallas_call(
        functools.partial(_post_kernel, final=final, ntiles=ntiles, tiles_per_seq=tpb, tq=tq),
        grid=(ntiles + 2,),
        in_specs=[
            pl.BlockSpec(memory_space=pltpu.SMEM),
            attn_rows(ATT_WIDTH), attn_rows(KV_WIDTH), prev_window(KV_WIDTH),
            attn_rows(KV_WIDTH), prev_window(KV_WIDTH),
            post_rows(d), post_rows(MLSTM_WIDTH), mem_spec, mem_spec,
            post_slab(d), post_slab(ATT_WIDTH), post_slab(MLSTM_WIDTH), memd_spec, memd_spec,
        ] + _weight_specs(w, names),
        out_specs=[pl.BlockSpec((tm, d), lambda s: (done_tile(s), 0)),
                   pl.BlockSpec((slab, d), lambda s: (done_tile(s), 0))],
        out_shape=[jax.ShapeDtypeStruct((t, d), F32), jax.ShapeDtypeStruct((td, d), F32)],
        scratch_shapes=[pltpu.VMEM((tm + slab, hidden), BF16), pltpu.VMEM((2, tm, ATT_WIDTH), BF16),
                        pltpu.VMEM((tm + slab, d), F32)],
        compiler_params=pltpu.CompilerParams(
            dimension_semantics=("arbitrary",), vmem_limit_bytes=VMEM_LIMIT),
    )(sinks, q, k, k, v, v, x, hm, memk, memv, xd, attd, hmd, memkd_t, memvd_t,
      *[w[n] for n in names])


def _cross_attention_decode(qc, mk_ref, mv_ref, tq):
    n = CROSS_HEADS * tq
    row_head = lax.broadcasted_iota(jnp.int32, (n, CROSS_WIDTH), 0) >> _log2(tq)
    lane_head = lax.broadcasted_iota(jnp.int32, (n, CROSS_WIDTH), 1) >> _log2(HEAD_DIM)
    own = row_head == lane_head
    outs = []
    for b in range(qc.shape[0] // tq):
        q = qc[b * tq:(b + 1) * tq, :]
        qx = jnp.where(own, jnp.concatenate([q] * CROSS_HEADS, axis=0), 0.0).astype(BF16)
        mk = mk_ref[b * CROSS_WIDTH:(b + 1) * CROSS_WIDTH, :].astype(BF16)
        mv = mv_ref[b * CROSS_WIDTH:(b + 1) * CROSS_WIDTH, :].astype(BF16)
        s = _dot(qx, mk) * (HEAD_DIM ** -0.5)
        p = jnp.exp(s - jnp.max(s, axis=-1, keepdims=True))
        pv = _dot_nt(p.astype(BF16), mv) / jnp.sum(p, axis=-1, keepdims=True)
        pv = jnp.where(own, pv, 0.0)
        out = pv[0:tq]
        for h in range(1, CROSS_HEADS):
            out = out + pv[h * tq:(h + 1) * tq]
        outs.append(out)
    return jnp.concatenate(outs, axis=0)


def _layer_weights(l, w_in, b_igate, b_fgate, g_mlstm_head, g_mix, g_cross, g_mem, w_ck, w_cv,
                   g_ffn, g_final):
    gate_w = w_in[l][:, MAIN_WIDTH:]
    gate_b = jnp.concatenate([b_igate[l], b_fgate[l]]).astype(F32)
    ngate = gate_b.shape[0]
    row = lambda a: a.astype(F32).reshape(1, -1)
    w_t = w_in[l].T
    k0 = ATT_WIDTH
    mk0 = ATT_WIDTH + 2 * KV_WIDTH + MQK_WIDTH
    return {
        "w_kvk_t": jnp.concatenate([w_t[k0:k0 + 2 * KV_WIDTH], w_t[mk0:mk0 + MQK_WIDTH]]).astype(BF16),
        "w_main": w_t[:MAIN_WIDTH].astype(BF16),
        "w_gate_cols": jnp.pad(gate_w, ((0, 0), (0, LANES - ngate))).astype(BF16),
        "w_gate_rows": jnp.pad(gate_w.T, ((0, GATE_ROWS - ngate), (0, 0))).astype(BF16),
        "b_gate_cols": jnp.pad(gate_b, (0, LANES - ngate)).reshape(1, LANES),
        "b_gate_rows": jnp.pad(gate_b, (0, GATE_ROWS - ngate)).reshape(GATE_ROWS, 1),
        "g_mix": row(g_mix[l]), "g_cross": row(g_cross[l]), "g_mem": row(g_mem[l]),
        "g_ffn": row(g_ffn[l]), "g_final": row(g_final), "g_head": row(g_mlstm_head[l]),
        "w_ck": w_ck[l].astype(F32), "w_cv": w_cv[l].astype(F32),
    }


TOKEN_TILE = 512
DECODE_PROJ_TILE = 256
MLSTM_STAGES_PER_PROJ_STAGE = 20
SWA_STAGES_PER_FFN_CHUNK = 3
MLSTM_CHUNK_ROWS = 256
SWA_SAMPLE_BATCH = 16
MLSTM_SAMPLE_BATCH = 16


def kernel(x_prompt, x_sample, mem_prompt, cache_swa_k, cache_swa_v, state_mlstm_C, state_mlstm_n,
           state_mlstm_m, cache_mem_k, cache_mem_v, w_in, b_igate, b_fgate, attn_sinks,
           g_mlstm_head, w_out, g_mix, g_cross, g_mem, w_cq, w_ck, w_cv, w_co, g_ffn, w_gate,
           w_up, w_down, g_final):
    depth = w_in.shape[0]
    bp, sp, d = x_prompt.shape
    bs, ss, _ = x_sample.shape
    mem_tokens = mem_prompt.shape[1]
    past = cache_swa_k.shape[2]
    yp = x_prompt.reshape(bp * sp, d)
    ys = x_sample.reshape(bs * ss, d)
    mem = mem_prompt.reshape(bp * mem_tokens, d)
    outs = [[] for _ in range(12)]

    for l in range(depth):
        final = l == depth - 1
        w = _layer_weights(l, w_in, b_igate, b_fgate, g_mlstm_head, g_mix, g_cross, g_mem,
                           w_ck, w_cv, g_ffn, g_final)
        sinks = attn_sinks[l].astype(F32)

        q, k, v, hm, c_p, n_p, m_p, cast, memk, memv, memk_t, memv_t, kwin_t, vwin_t = _mixer_prompt(
            yp, mem, w, w_gate[l].astype(F32), w_up[l].astype(F32), w_down[l].astype(F32),
            w_out[l].astype(F32), w_cq[l].astype(F32), w_co[l].astype(F32), bp, MLSTM_CHUNK_ROWS)
        w.update(cast)

        def tokens_second(a, heads, tokens):
            return jnp.transpose(a.reshape(bp, heads, HEAD_DIM, tokens), (0, 3, 1, 2))

        outs[0].append(tokens_second(kwin_t, ATT_KV_HEADS, WINDOW))
        outs[1].append(tokens_second(vwin_t, ATT_KV_HEADS, WINDOW))
        outs[2].append(c_p)
        outs[3].append(n_p)
        outs[4].append(m_p)
        outs[5].append(tokens_second(memk_t, CROSS_HEADS, mem_tokens))
        outs[6].append(tokens_second(memv_t, CROSS_HEADS, mem_tokens))

        def tokens_last(a):
            return jnp.transpose(a.astype(F32), (0, 2, 3, 1)).reshape(-1, a.shape[1])

        qd, mq, mk, mv, og, grow, gcol, knt, vnt, mkt = _decode_proj(ys, w, DECODE_PROJ_TILE)
        att_d, kbuf_t, vbuf_t = _swa_sample(
            qd, knt, vnt, tokens_last(cache_swa_k[l]), tokens_last(cache_swa_v[l]), sinks, bs,
            SWA_SAMPLE_BATCH)
        m_rows = jnp.pad(jnp.repeat(state_mlstm_m[l].astype(F32), ss, axis=0),
                         ((0, 0), (0, LANES - MLSTM_HEADS)))
        ct_in = jnp.swapaxes(state_mlstm_C[l].astype(F32), 2, 3).reshape(-1, MLSTM_DV)
        hm_d, ct_s, n_s, mt = _mlstm_sample(
            mq, mk, mkt, mv, og, gcol, grow, m_rows, ct_in,
            state_mlstm_n[l].astype(F32).reshape(bs * MLSTM_HEADS, 1, MLSTM_DK),
            w["g_head"], bs, MLSTM_SAMPLE_BATCH)

        yp, ys = _post(yp, q, k, v, hm, memk, memv, ys, att_d, hm_d, tokens_last(cache_mem_k[l]),
                       tokens_last(cache_mem_v[l]), sinks, w, bp, bs, final, TOKEN_TILE)
        buf_t_shape = (bs, ATT_KV_HEADS, HEAD_DIM, past)
        outs[7].append(jnp.transpose(kbuf_t.reshape(buf_t_shape), (0, 3, 1, 2)))
        outs[8].append(jnp.transpose(vbuf_t.reshape(buf_t_shape), (0, 3, 1, 2)))
        outs[9].append(jnp.swapaxes(ct_s.reshape(bs, MLSTM_HEADS, MLSTM_DK, MLSTM_DV), 2, 3))
        outs[10].append(n_s.reshape(bs, MLSTM_HEADS, MLSTM_DK))
        outs[11].append(mt.reshape(bs, ss, LANES)[:, ss - 1, :MLSTM_HEADS])

    return (yp.reshape(bp, sp, d), ys.reshape(bs, ss, d)) + tuple(jnp.stack(o) for o in outs)
```
